```python
import jax, jax.numpy as jnp
from jax import lax
import numpy as np

D_MODEL = 1024
BATCH = 8
SEQ = 8192
DEPTH = 2

MEM_LEN = 256
EPS = 1e-6

CHUNK = 128
A_GROUPS = 4
A_WIDTH = D_MODEL // 2
A_HEAD = A_WIDTH // A_GROUPS
B_WIDTH = D_MODEL // 2
CONV_WIDTH = 3
POOL_WINDOWS = (2, 4, 8, 16)
C_WIDTH = D_MODEL // 2
C_GROUP = C_WIDTH // len(POOL_WINDOWS)
N_BRANCH = 3
IN_SPLITS = (2 * A_WIDTH, 2 * A_WIDTH + 3 * B_WIDTH, 2 * A_WIDTH + 3 * B_WIDTH + C_WIDTH)
IN_COLS = 2 * A_WIDTH + 3 * B_WIDTH + C_WIDTH + N_BRANCH * D_MODEL
XATTN_HEADS = 4
XATTN_HEAD_DIM = D_MODEL // XATTN_HEADS
FFN_HIDDEN = ((-(-8 * D_MODEL // 3) + 255) // 256) * 256

kernel_name = "hybrid_gated_gmlp_conv_pool_xattn_block"


def rmsnorm(x, g):
    xf = x.astype(jnp.float32)
    y = xf * lax.rsqrt(jnp.mean(xf * xf, axis=-1, keepdims=True) + EPS)
    return (y * g.astype(jnp.float32)).astype(x.dtype)


def layernorm(x, g, b):
    xf = x.astype(jnp.float32)
    mu = jnp.mean(xf, axis=-1, keepdims=True)
    var = jnp.mean(jnp.square(xf - mu), axis=-1, keepdims=True)
    y = (xf - mu) * lax.rsqrt(var + EPS)
    return (y * g.astype(jnp.float32) + b.astype(jnp.float32)).astype(x.dtype)


def chunked_spatial_gating(z, ln_g, ln_b, ws, bs):
    u, v = jnp.split(z, 2, axis=-1)
    v = layernorm(v, ln_g, ln_b)
    nb, s, _ = v.shape
    v = v.reshape(nb, s // CHUNK, CHUNK, A_GROUPS, A_HEAD)
    causal = jnp.tril(jnp.ones((CHUNK, CHUNK), dtype=bool))
    w = jnp.where(causal[None], ws, jnp.zeros_like(ws)).astype(v.dtype)
    mixed = jnp.einsum('gij,bcjgd->bcigd', w, v) + bs.T.astype(v.dtype)[None, None, :, :, None]
    return u * mixed.reshape(nb, s, A_WIDTH)


def short_gated_conv(z, w_conv):
    gb, gc, xin = jnp.split(z, 3, axis=-1)
    y = gc * xin
    k = w_conv[:, None, :].astype(y.dtype)
    y = lax.conv_general_dilated(y, k, window_strides=(1,), padding=[(CONV_WIDTH - 1, 0)],
                                 dimension_numbers=('NWC', 'WIO', 'NWC'),
                                 feature_group_count=B_WIDTH)
    return gb * y


def multiscale_pool(z, wg, scale):
    nb, s, _ = z.shape
    zf = z.astype(jnp.float32)
    cs = jnp.concatenate([jnp.zeros((nb, 1, C_WIDTH), jnp.float32), jnp.cumsum(zf, axis=1)], axis=1)
    pos = jnp.arange(s)
    outs = []
    for gi, win in enumerate(POOL_WINDOWS):
        sl = slice(gi * C_GROUP, (gi + 1) * C_GROUP)
        seg = cs[:, :, sl]
        hi = seg[:, 1:]
        lo = jnp.concatenate([jnp.zeros((nb, win - 1, C_GROUP), jnp.float32), seg[:, :s - win + 1]], axis=1)
        count = jnp.minimum(pos + 1, win).astype(jnp.float32)[None, :, None]
        outs.append((hi - lo) / count - zf[:, :, sl])
    p = jnp.stack(outs, axis=2)
    p = jnp.einsum('bsgc,gcd->bsgd', p, wg.astype(jnp.float32)).reshape(nb, s, C_WIDTH)
    return (p * scale.astype(jnp.float32)).astype(z.dtype)


def mixer_block(x, g_mix, w_in, b_gate, a_ln_g, a_ln_b, a_ws, a_bs, b_conv, c_wg, c_scale,
                w_branch_a, w_branch_b, w_branch_c, w_o):
    nb, s, _ = x.shape
    h = rmsnorm(x, g_mix)
    proj = h @ w_in
    za, zb, zc, zg = jnp.split(proj, IN_SPLITS, axis=-1)
    ya = chunked_spatial_gating(jax.nn.gelu(za, approximate=False), a_ln_g, a_ln_b, a_ws, a_bs) @ w_branch_a
    yb = short_gated_conv(zb, b_conv) @ w_branch_b
    yc = multiscale_pool(zc, c_wg, c_scale) @ w_branch_c
    gates = jax.nn.sigmoid(zg.reshape(nb, s, N_BRANCH, D_MODEL) + b_gate)
    merged = gates[:, :, 0] * ya + gates[:, :, 1] * yb + gates[:, :, 2] * yc
    return x + merged @ w_o


def cross_attention(x, mem, g_x, g_m, w_q, w_kv, w_o):
    nb, s, _ = x.shape
    hq = rmsnorm(x, g_x)
    m = rmsnorm(mem, g_m)
    q = (hq @ w_q).reshape(nb, s, XATTN_HEADS, XATTN_HEAD_DIM)
    k, v = jnp.split(m @ w_kv, 2, axis=-1)
    k = k.reshape(nb, MEM_LEN, XATTN_HEADS, XATTN_HEAD_DIM)
    v = v.reshape(nb, MEM_LEN, XATTN_HEADS, XATTN_HEAD_DIM)
    sc = jnp.einsum('bshd,bmhd->bhsm', q, k).astype(jnp.float32) * (XATTN_HEAD_DIM ** -0.5)
    p = jax.nn.softmax(sc, axis=-1).astype(v.dtype)
    o = jnp.einsum('bhsm,bmhd->bshd', p, v).reshape(nb, s, D_MODEL)
    return x + o @ w_o


def swiglu_ffn(x, g, w_in, w_out):
    h = rmsnorm(x, g)
    gate, up = jnp.split(h @ w_in, 2, axis=-1)
    return x + (jax.nn.silu(gate) * up) @ w_out


def _fwd_setup_inputs(seed: int = 0) -> dict:
    key = jax.random.key(seed)
    ks = jax.random.split(key, 32)
    f32 = jnp.float32
    L, D = DEPTH, D_MODEL

    def nrm(k, shape, fan_in):
        return jax.random.normal(k, shape, f32) * (fan_in ** -0.5)

    def gain(k, shape):
        return 1.0 + 0.02 * jax.random.normal(k, shape, f32)

    def small(k, shape, s=0.02):
        return s * jax.random.normal(k, shape, f32)

    return {
        "x": jax.random.normal(ks[0], (BATCH, SEQ, D), f32),
        "mem": jax.random.normal(ks[1], (BATCH, MEM_LEN, D), f32),
        "g_mix": gain(ks[2], (L, D)),
        "w_in": nrm(ks[3], (L, D, IN_COLS), D),
        "b_gate": small(ks[4], (L, N_BRANCH, D), 0.01),
        "a_ln_g": gain(ks[5], (L, A_WIDTH)),
        "a_ln_b": small(ks[6], (L, A_WIDTH)),
        "a_ws": nrm(ks[7], (L, A_GROUPS, CHUNK, CHUNK), CHUNK),
        "a_bs": gain(ks[8], (L, A_GROUPS, CHUNK)),
        "b_conv": nrm(ks[9], (L, CONV_WIDTH, B_WIDTH), CONV_WIDTH),
        "c_wg": nrm(ks[10], (L, len(POOL_WINDOWS), C_GROUP, C_GROUP), C_GROUP),
        "c_scale": gain(ks[11], (L, C_WIDTH)),
        "w_branch_a": nrm(ks[12], (L, A_WIDTH, D), A_WIDTH),
        "w_branch_b": nrm(ks[13], (L, B_WIDTH, D), B_WIDTH),
        "w_branch_c": nrm(ks[14], (L, C_WIDTH, D), C_WIDTH),
        "w_o": nrm(ks[15], (L, D, D), D),
        "g_xattn": gain(ks[16], (L, D)),
        "g_mem": gain(ks[17], (L, D)),
        "w_xq": nrm(ks[18], (L, D, D), D),
        "w_xkv": nrm(ks[19], (L, D, 2 * D), D),
        "w_xo": nrm(ks[20], (L, D, D), D),
        "g_ffn": gain(ks[21], (L, D)),
        "w_ffn_in": nrm(ks[22], (L, D, 2 * FFN_HIDDEN), D),
        "w_ffn_out": nrm(ks[23], (L, FFN_HIDDEN, D), FFN_HIDDEN),
        "g_final": gain(ks[24], (D,)),
    }


def _fwd_reference(x, mem, g_mix, w_in, b_gate, a_ln_g, a_ln_b, a_ws, a_bs, b_conv, c_wg, c_scale,
              w_branch_a, w_branch_b, w_branch_c, w_o, g_xattn, g_mem, w_xq, w_xkv, w_xo,
              g_ffn, w_ffn_in, w_ffn_out, g_final):
    for l in range(DEPTH):
        x = mixer_block(x, g_mix[l], w_in[l], b_gate[l], a_ln_g[l], a_ln_b[l], a_ws[l], a_bs[l],
                        b_conv[l], c_wg[l], c_scale[l], w_branch_a[l], w_branch_b[l], w_branch_c[l], w_o[l])
        x = cross_attention(x, mem, g_xattn[l], g_mem[l], w_xq[l], w_xkv[l], w_xo[l])
        x = swiglu_ffn(x, g_ffn[l], w_ffn_in[l], w_ffn_out[l])
    return rmsnorm(x, g_final)


import jax as _jax
import jax.numpy as _jnp

TWIN_FORMAT = 'train_step'
FWD_PARAMS = ['x', 'mem', 'g_mix', 'w_in', 'b_gate', 'a_ln_g', 'a_ln_b', 'a_ws', 'a_bs', 'b_conv', 'c_wg', 'c_scale', 'w_branch_a', 'w_branch_b', 'w_branch_c', 'w_o', 'g_xattn', 'g_mem', 'w_xq', 'w_xkv', 'w_xo', 'g_ffn', 'w_ffn_in', 'w_ffn_out', 'g_final']
TWIN_WEIGHTS = ['g_mix', 'w_in', 'b_gate', 'a_ln_g', 'a_ln_b', 'a_ws', 'a_bs', 'b_conv', 'c_wg', 'c_scale', 'w_branch_a', 'w_branch_b', 'w_branch_c', 'w_o', 'g_xattn', 'g_mem', 'w_xq', 'w_xkv', 'w_xo', 'g_ffn', 'w_ffn_in', 'w_ffn_out', 'g_final']
TWIN_DIFF_INPUT = 'x'
TWIN_INPUTS = ['x', 'mem', 'g_mix', 'w_in', 'b_gate', 'a_ln_g', 'a_ln_b', 'a_ws', 'a_bs', 'b_conv', 'c_wg', 'c_scale', 'w_branch_a', 'w_branch_b', 'w_branch_c', 'w_o', 'g_xattn', 'g_mem', 'w_xq', 'w_xkv', 'w_xo', 'g_ffn', 'w_ffn_in', 'w_ffn_out', 'g_final', 'loss_target', 'm_g_mix', 'm_w_in', 'm_b_gate', 'm_a_ln_g', 'm_a_ln_b', 'm_a_ws', 'm_a_bs', 'm_b_conv', 'm_c_wg', 'm_c_scale', 'm_w_branch_a', 'm_w_branch_b', 'm_w_branch_c', 'm_w_o', 'm_g_xattn', 'm_g_mem', 'm_w_xq', 'm_w_xkv', 'm_w_xo', 'm_g_ffn', 'm_w_ffn_in', 'm_w_ffn_out', 'm_g_final', 'v_g_mix', 'v_w_in', 'v_b_gate', 'v_a_ln_g', 'v_a_ln_b', 'v_a_ws', 'v_a_bs', 'v_b_conv', 'v_c_wg', 'v_c_scale', 'v_w_branch_a', 'v_w_branch_b', 'v_w_branch_c', 'v_w_o', 'v_g_xattn', 'v_g_mem', 'v_w_xq', 'v_w_xkv', 'v_w_xo', 'v_g_ffn', 'v_w_ffn_in', 'v_w_ffn_out', 'v_g_final']
TWIN_OUTPUTS = ['loss', 'grad_x', 'grad_g_mix', 'grad_w_in', 'grad_b_gate', 'grad_a_ln_g', 'grad_a_ln_b', 'grad_a_ws', 'grad_a_bs', 'grad_b_conv', 'grad_c_wg', 'grad_c_scale', 'grad_w_branch_a', 'grad_w_branch_b', 'grad_w_branch_c', 'grad_w_o', 'grad_g_xattn', 'grad_g_mem', 'grad_w_xq', 'grad_w_xkv', 'grad_w_xo', 'grad_g_ffn', 'grad_w_ffn_in', 'grad_w_ffn_out', 'grad_g_final', 'delta_g_mix', 'delta_w_in', 'delta_b_gate', 'delta_a_ln_g', 'delta_a_ln_b', 'delta_a_ws', 'delta_a_bs', 'delta_b_conv', 'delta_c_wg', 'delta_c_scale', 'delta_w_branch_a', 'delta_w_branch_b', 'delta_w_branch_c', 'delta_w_o', 'delta_g_xattn', 'delta_g_mem', 'delta_w_xq', 'delta_w_xkv', 'delta_w_xo', 'delta_g_ffn', 'delta_w_ffn_in', 'delta_w_ffn_out', 'delta_g_final', 'new_m_g_mix', 'new_m_w_in', 'new_m_b_gate', 'new_m_a_ln_g', 'new_m_a_ln_b', 'new_m_a_ws', 'new_m_a_bs', 'new_m_b_conv', 'new_m_c_wg', 'new_m_c_scale', 'new_m_w_branch_a', 'new_m_w_branch_b', 'new_m_w_branch_c', 'new_m_w_o', 'new_m_g_xattn', 'new_m_g_mem', 'new_m_w_xq', 'new_m_w_xkv', 'new_m_w_xo', 'new_m_g_ffn', 'new_m_w_ffn_in', 'new_m_w_ffn_out', 'new_m_g_final', 'new_v_g_mix', 'new_v_w_in', 'new_v_b_gate', 'new_v_a_ln_g', 'new_v_a_ln_b', 'new_v_a_ws', 'new_v_a_bs', 'new_v_b_conv', 'new_v_c_wg', 'new_v_c_scale', 'new_v_w_branch_a', 'new_v_w_branch_b', 'new_v_w_branch_c', 'new_v_w_o', 'new_v_g_xattn', 'new_v_g_mem', 'new_v_w_xq', 'new_v_w_xkv', 'new_v_w_xo', 'new_v_g_ffn', 'new_v_w_ffn_in', 'new_v_w_ffn_out', 'new_v_g_final']
TWIN_LEAF_KINDS = {'loss': 'loss', 'grad_x': 'grad_x', 'grad_g_mix': 'grad_w', 'grad_w_in': 'grad_w', 'grad_b_gate': 'grad_w', 'grad_a_ln_g': 'grad_w', 'grad_a_ln_b': 'grad_w', 'grad_a_ws': 'grad_w', 'grad_a_bs': 'grad_w', 'grad_b_conv': 'grad_w', 'grad_c_wg': 'grad_w', 'grad_c_scale': 'grad_w', 'grad_w_branch_a': 'grad_w', 'grad_w_branch_b': 'grad_w', 'grad_w_branch_c': 'grad_w', 'grad_w_o': 'grad_w', 'grad_g_xattn': 'grad_w', 'grad_g_mem': 'grad_w', 'grad_w_xq': 'grad_w', 'grad_w_xkv': 'grad_w', 'grad_w_xo': 'grad_w', 'grad_g_ffn': 'grad_w', 'grad_w_ffn_in': 'grad_w', 'grad_w_ffn_out': 'grad_w', 'grad_g_final': 'grad_w', 'delta_g_mix': 'delta_w', 'delta_w_in': 'delta_w', 'delta_b_gate': 'delta_w', 'delta_a_ln_g': 'delta_w', 'delta_a_ln_b': 'delta_w', 'delta_a_ws': 'delta_w', 'delta_a_bs': 'delta_w', 'delta_b_conv': 'delta_w', 'delta_c_wg': 'delta_w', 'delta_c_scale': 'delta_w', 'delta_w_branch_a': 'delta_w', 'delta_w_branch_b': 'delta_w', 'delta_w_branch_c': 'delta_w', 'delta_w_o': 'delta_w', 'delta_g_xattn': 'delta_w', 'delta_g_mem': 'delta_w', 'delta_w_xq': 'delta_w', 'delta_w_xkv': 'delta_w', 'delta_w_xo': 'delta_w', 'delta_g_ffn': 'delta_w', 'delta_w_ffn_in': 'delta_w', 'delta_w_ffn_out': 'delta_w', 'delta_g_final': 'delta_w', 'new_m_g_mix': 'new_m', 'new_m_w_in': 'new_m', 'new_m_b_gate': 'new_m', 'new_m_a_ln_g': 'new_m', 'new_m_a_ln_b': 'new_m', 'new_m_a_ws': 'new_m', 'new_m_a_bs': 'new_m', 'new_m_b_conv': 'new_m', 'new_m_c_wg': 'new_m', 'new_m_c_scale': 'new_m', 'new_m_w_branch_a': 'new_m', 'new_m_w_branch_b': 'new_m', 'new_m_w_branch_c': 'new_m', 'new_m_w_o': 'new_m', 'new_m_g_xattn': 'new_m', 'new_m_g_mem': 'new_m', 'new_m_w_xq': 'new_m', 'new_m_w_xkv': 'new_m', 'new_m_w_xo': 'new_m', 'new_m_g_ffn': 'new_m', 'new_m_w_ffn_in': 'new_m', 'new_m_w_ffn_out': 'new_m', 'new_m_g_final': 'new_m', 'new_v_g_mix': 'new_v', 'new_v_w_in': 'new_v', 'new_v_b_gate': 'new_v', 'new_v_a_ln_g': 'new_v', 'new_v_a_ln_b': 'new_v', 'new_v_a_ws': 'new_v', 'new_v_a_bs': 'new_v', 'new_v_b_conv': 'new_v', 'new_v_c_wg': 'new_v', 'new_v_c_scale': 'new_v', 'new_v_w_branch_a': 'new_v', 'new_v_w_branch_b': 'new_v', 'new_v_w_branch_c': 'new_v', 'new_v_w_o': 'new_v', 'new_v_g_xattn': 'new_v', 'new_v_g_mem': 'new_v', 'new_v_w_xq': 'new_v', 'new_v_w_xkv': 'new_v', 'new_v_w_xo': 'new_v', 'new_v_g_ffn': 'new_v', 'new_v_w_ffn_in': 'new_v', 'new_v_w_ffn_out': 'new_v', 'new_v_g_final': 'new_v'}


def _forward(args):
    return _fwd_reference(*[args[k] for k in FWD_PARAMS])


def _output_shape():
    out = _jax.eval_shape(lambda: _forward(_fwd_setup_inputs(0)))
    return out.shape, out.dtype

N_MICROBATCH = 1
ADAM_LR = 0.001
ADAM_B1 = 0.9
ADAM_B2 = 0.999
ADAM_EPS = 1e-08
ADAM_WD = 0.01
ADAM_STEP = 10
PER_EXAMPLE_BATCH_AXIS = {'x': 0, 'mem': 0, 'loss_target': 0}
SHARED_INPUTS = []
_WEIGHT_DTYPES = {'g_mix': _jnp.float32, 'w_in': _jnp.float32, 'b_gate': _jnp.float32, 'a_ln_g': _jnp.float32, 'a_ln_b': _jnp.float32, 'a_ws': _jnp.float32, 'a_bs': _jnp.float32, 'b_conv': _jnp.float32, 'c_wg': _jnp.float32, 'c_scale': _jnp.float32, 'w_branch_a': _jnp.float32, 'w_branch_b': _jnp.float32, 'w_branch_c': _jnp.float32, 'w_o': _jnp.float32, 'g_xattn': _jnp.float32, 'g_mem': _jnp.float32, 'w_xq': _jnp.float32, 'w_xkv': _jnp.float32, 'w_xo': _jnp.float32, 'g_ffn': _jnp.float32, 'w_ffn_in': _jnp.float32, 'w_ffn_out': _jnp.float32, 'g_final': _jnp.float32}
MOMENT_SCALE = {'g_mix': 2.618292e-01, 'w_in': 1.050299e-01, 'b_gate': 4.086150e-02, 'a_ln_g': 7.766273e-02, 'a_ln_b': 7.782977e-02, 'a_ws': 7.278643e-02, 'a_bs': 1.070967e-01, 'b_conv': 1.656476e-01, 'c_wg': 1.440373e-01, 'c_scale': 1.425927e-01, 'w_branch_a': 8.995010e-02, 'w_branch_b': 1.166940e-01, 'w_branch_c': 9.961474e-02, 'w_o': 1.778438e-01, 'g_xattn': 2.239815e-02, 'g_mem': 3.204555e-02, 'w_xq': 2.194565e-02, 'w_xkv': 2.188017e-02, 'w_xo': 2.201339e-02, 'g_ffn': 1.560490e-01, 'w_ffn_in': 6.514862e-02, 'w_ffn_out': 1.064519e-01, 'g_final': 6.402403e+01}


def _to_microbatches(a, axis):
    t = _jnp.moveaxis(a, axis, 0)
    t = t.reshape((N_MICROBATCH, t.shape[0] // N_MICROBATCH) + t.shape[1:])
    return _jnp.moveaxis(t, 1, axis + 1)


def setup_inputs(seed: int = 0) -> dict:
    inp = _fwd_setup_inputs(seed)
    key = _jax.random.fold_in(_jax.random.key(seed), 7919)
    shape, _ = _output_shape()
    out = dict(inp)
    out["loss_target"] = _jax.random.normal(_jax.random.fold_in(key, 0), shape, _jnp.float32)
    for i, name in enumerate(TWIN_WEIGHTS):
        w = inp[name].astype(_jnp.float32)
        if MOMENT_SCALE is None:
            s = _jnp.sqrt(_jnp.mean(_jnp.square(w)) + 1e-30)
        else:
            s = MOMENT_SCALE[name]
        km, kv = _jax.random.split(_jax.random.fold_in(key, i + 1))
        out[name] = w
        out["m_" + name] = s * _jax.random.normal(km, w.shape, _jnp.float32)
        out["v_" + name] = (s * s) * _jax.random.uniform(kv, w.shape, _jnp.float32, 0.5, 1.5)
    if N_MICROBATCH > 1:
        for name, axis in PER_EXAMPLE_BATCH_AXIS.items():
            out[name] = _to_microbatches(out[name], axis)
    return {'x': out['x'], 'mem': out['mem'], 'g_mix': out['g_mix'], 'w_in': out['w_in'], 'b_gate': out['b_gate'], 'a_ln_g': out['a_ln_g'], 'a_ln_b': out['a_ln_b'], 'a_ws': out['a_ws'], 'a_bs': out['a_bs'], 'b_conv': out['b_conv'], 'c_wg': out['c_wg'], 'c_scale': out['c_scale'], 'w_branch_a': out['w_branch_a'], 'w_branch_b': out['w_branch_b'], 'w_branch_c': out['w_branch_c'], 'w_o': out['w_o'], 'g_xattn': out['g_xattn'], 'g_mem': out['g_mem'], 'w_xq': out['w_xq'], 'w_xkv': out['w_xkv'], 'w_xo': out['w_xo'], 'g_ffn': out['g_ffn'], 'w_ffn_in': out['w_ffn_in'], 'w_ffn_out': out['w_ffn_out'], 'g_final': out['g_final'], 'loss_target': out['loss_target'], 'm_g_mix': out['m_g_mix'], 'm_w_in': out['m_w_in'], 'm_b_gate': out['m_b_gate'], 'm_a_ln_g': out['m_a_ln_g'], 'm_a_ln_b': out['m_a_ln_b'], 'm_a_ws': out['m_a_ws'], 'm_a_bs': out['m_a_bs'], 'm_b_conv': out['m_b_conv'], 'm_c_wg': out['m_c_wg'], 'm_c_scale': out['m_c_scale'], 'm_w_branch_a': out['m_w_branch_a'], 'm_w_branch_b': out['m_w_branch_b'], 'm_w_branch_c': out['m_w_branch_c'], 'm_w_o': out['m_w_o'], 'm_g_xattn': out['m_g_xattn'], 'm_g_mem': out['m_g_mem'], 'm_w_xq': out['m_w_xq'], 'm_w_xkv': out['m_w_xkv'], 'm_w_xo': out['m_w_xo'], 'm_g_ffn': out['m_g_ffn'], 'm_w_ffn_in': out['m_w_ffn_in'], 'm_w_ffn_out': out['m_w_ffn_out'], 'm_g_final': out['m_g_final'], 'v_g_mix': out['v_g_mix'], 'v_w_in': out['v_w_in'], 'v_b_gate': out['v_b_gate'], 'v_a_ln_g': out['v_a_ln_g'], 'v_a_ln_b': out['v_a_ln_b'], 'v_a_ws': out['v_a_ws'], 'v_a_bs': out['v_a_bs'], 'v_b_conv': out['v_b_conv'], 'v_c_wg': out['v_c_wg'], 'v_c_scale': out['v_c_scale'], 'v_w_branch_a': out['v_w_branch_a'], 'v_w_branch_b': out['v_w_branch_b'], 'v_w_branch_c': out['v_w_branch_c'], 'v_w_o': out['v_w_o'], 'v_g_xattn': out['v_g_xattn'], 'v_g_mem': out['v_g_mem'], 'v_w_xq': out['v_w_xq'], 'v_w_xkv': out['v_w_xkv'], 'v_w_xo': out['v_w_xo'], 'v_g_ffn': out['v_g_ffn'], 'v_w_ffn_in': out['v_w_ffn_in'], 'v_w_ffn_out': out['v_w_ffn_out'], 'v_g_final': out['v_g_final']}


def _loss(weights, diff, rest, loss_target):
    with _jax.named_scope("forward"):
        args = {**rest, TWIN_DIFF_INPUT: diff, **{k: w.astype(_WEIGHT_DTYPES[k]) for k, w in weights.items()}}
        y = _forward(args)
    with _jax.named_scope("loss_head"):
        err = _jnp.square(y.astype(_jnp.float32) - loss_target)
        return 0.5 * _jnp.sum(_jnp.mean(err, axis=-1)) if err.ndim else 0.5 * err


def _adamw(w, g, m, v):
    m = ADAM_B1 * m + (1.0 - ADAM_B1) * g
    v = ADAM_B2 * v + (1.0 - ADAM_B2) * _jnp.square(g)
    m_hat = m / (1.0 - ADAM_B1 ** ADAM_STEP)
    v_hat = v / (1.0 - ADAM_B2 ** ADAM_STEP)
    delta = -ADAM_LR * (m_hat / (_jnp.sqrt(v_hat) + ADAM_EPS) + ADAM_WD * w)
    return delta, m, v


def reference(x, mem, g_mix, w_in, b_gate, a_ln_g, a_ln_b, a_ws, a_bs, b_conv, c_wg, c_scale, w_branch_a, w_branch_b, w_branch_c, w_o, g_xattn, g_mem, w_xq, w_xkv, w_xo, g_ffn, w_ffn_in, w_ffn_out, g_final, loss_target, m_g_mix, m_w_in, m_b_gate, m_a_ln_g, m_a_ln_b, m_a_ws, m_a_bs, m_b_conv, m_c_wg, m_c_scale, m_w_branch_a, m_w_branch_b, m_w_branch_c, m_w_o, m_g_xattn, m_g_mem, m_w_xq, m_w_xkv, m_w_xo, m_g_ffn, m_w_ffn_in, m_w_ffn_out, m_g_final, v_g_mix, v_w_in, v_b_gate, v_a_ln_g, v_a_ln_b, v_a_ws, v_a_bs, v_b_conv, v_c_wg, v_c_scale, v_w_branch_a, v_w_branch_b, v_w_branch_c, v_w_o, v_g_xattn, v_g_mem, v_w_xq, v_w_xkv, v_w_xo, v_g_ffn, v_w_ffn_in, v_w_ffn_out, v_g_final):
    given = dict(x=x, mem=mem, g_mix=g_mix, w_in=w_in, b_gate=b_gate, a_ln_g=a_ln_g, a_ln_b=a_ln_b, a_ws=a_ws, a_bs=a_bs, b_conv=b_conv, c_wg=c_wg, c_scale=c_scale, w_branch_a=w_branch_a, w_branch_b=w_branch_b, w_branch_c=w_branch_c, w_o=w_o, g_xattn=g_xattn, g_mem=g_mem, w_xq=w_xq, w_xkv=w_xkv, w_xo=w_xo, g_ffn=g_ffn, w_ffn_in=w_ffn_in, w_ffn_out=w_ffn_out, g_final=g_final, loss_target=loss_target, m_g_mix=m_g_mix, m_w_in=m_w_in, m_b_gate=m_b_gate, m_a_ln_g=m_a_ln_g, m_a_ln_b=m_a_ln_b, m_a_ws=m_a_ws, m_a_bs=m_a_bs, m_b_conv=m_b_conv, m_c_wg=m_c_wg, m_c_scale=m_c_scale, m_w_branch_a=m_w_branch_a, m_w_branch_b=m_w_branch_b, m_w_branch_c=m_w_branch_c, m_w_o=m_w_o, m_g_xattn=m_g_xattn, m_g_mem=m_g_mem, m_w_xq=m_w_xq, m_w_xkv=m_w_xkv, m_w_xo=m_w_xo, m_g_ffn=m_g_ffn, m_w_ffn_in=m_w_ffn_in, m_w_ffn_out=m_w_ffn_out, m_g_final=m_g_final, v_g_mix=v_g_mix, v_w_in=v_w_in, v_b_gate=v_b_gate, v_a_ln_g=v_a_ln_g, v_a_ln_b=v_a_ln_b, v_a_ws=v_a_ws, v_a_bs=v_a_bs, v_b_conv=v_b_conv, v_c_wg=v_c_wg, v_c_scale=v_c_scale, v_w_branch_a=v_w_branch_a, v_w_branch_b=v_w_branch_b, v_w_branch_c=v_w_branch_c, v_w_o=v_w_o, v_g_xattn=v_g_xattn, v_g_mem=v_g_mem, v_w_xq=v_w_xq, v_w_xkv=v_w_xkv, v_w_xo=v_w_xo, v_g_ffn=v_g_ffn, v_w_ffn_in=v_w_ffn_in, v_w_ffn_out=v_w_ffn_out, v_g_final=v_g_final)
    weights = {n: given[n] for n in TWIN_WEIGHTS}
    shared = {n: given[n] for n in SHARED_INPUTS}
    per_example = {n: given[n] for n in ['x', 'mem']}
    grad_fn = _jax.value_and_grad(_loss, argnums=(0, 1))

    def one_microbatch(ex, loss_target):
        ex = dict(ex)
        diff = ex.pop(TWIN_DIFF_INPUT)
        return grad_fn(weights, diff, {**shared, **ex}, loss_target)

    if N_MICROBATCH == 1:
        loss, (grad_w, grad_x) = one_microbatch(per_example, given["loss_target"])
    else:
        def body(carry, xs):
            loss_sum, grad_sum = carry
            l_k, (gw_k, gx_k) = one_microbatch(xs[0], xs[1])
            with _jax.named_scope("update"):
                return (loss_sum + l_k, _jax.tree.map(_jnp.add, grad_sum, gw_k)), gx_k

        init = (_jnp.zeros((), _jnp.float32), _jax.tree.map(_jnp.zeros_like, weights))
        (loss, grad_w), grad_x = _jax.lax.scan(body, init, (per_example, given["loss_target"]))
    with _jax.named_scope("update"):
        delta_w, new_m, new_v = {}, {}, {}
        for n in TWIN_WEIGHTS:
            delta_w[n], new_m[n], new_v[n] = _adamw(weights[n], grad_w[n], given["m_" + n], given["v_" + n])
    return (loss, grad_x, *[grad_w[n] for n in TWIN_WEIGHTS], *[delta_w[n] for n in TWIN_WEIGHTS],
            *[new_m[n] for n in TWIN_WEIGHTS], *[new_v[n] for n in TWIN_WEIGHTS])
```

```python
import functools
import math

import jax
import jax.numpy as jnp
from jax import lax
from jax.experimental import pallas as pl
from jax.experimental.pallas import tpu as pltpu

F32 = jnp.float32
BF16 = jnp.bfloat16

D_MODEL = 1024
DEPTH = 2
MEM_LEN = 256
EPS = 1e-6
CHUNK = 128
A_GROUPS = 4
A_WIDTH = 512
B_WIDTH = 512
C_WIDTH = 512
C_GROUP = 128
POOL_WINDOWS = (2, 4, 8, 16)
HALO = 16
IN_COLS = 6144
MIX_COLS = 3072
XATTN_HEADS = 4
HEAD_DIM = 256
FFN_HIDDEN = 2816
N_CHIPS = 4

ADAM_LR = 0.001
ADAM_B1 = 0.9
ADAM_B2 = 0.999
ADAM_EPS = 1e-08
ADAM_WD = 0.01
ADAM_STEP = 10

V7X_VMEM_BYTES = 64 * 1024 * 1024
VMEM_LIMIT = (V7X_VMEM_BYTES * 3) // 4
LANES = 128
MESH = pl.DeviceIdType.MESH
ANY = pl.BlockSpec(memory_space=pl.ANY)

BIG = ("w_in", "w_branch_a", "w_branch_b", "w_branch_c", "w_o", "w_xq", "w_xkv", "w_xo", "w_ffn_in", "w_ffn_out")
COL_SHARDED = ("w_in", "w_branch_a", "w_branch_b", "w_branch_c", "w_xkv", "w_ffn_in")
SMALL_REPL = ("g_mix", "a_ln_g", "a_ln_b", "a_ws", "a_bs", "c_wg", "c_scale", "g_xattn", "g_mem", "g_ffn", "g_final")
SMALL_SHARDED = ("b_gate", "b_conv")
WEIGHTS = ("g_mix", "w_in", "b_gate", "a_ln_g", "a_ln_b", "a_ws", "a_bs", "b_conv", "c_wg", "c_scale", "w_branch_a",
           "w_branch_b", "w_branch_c", "w_o", "g_xattn", "g_mem", "w_xq", "w_xkv", "w_xo", "g_ffn", "w_ffn_in",
           "w_ffn_out", "g_final")


def _params(sem=None):
    return pltpu.CompilerParams(dimension_semantics=sem, vmem_limit_bytes=VMEM_LIMIT)


def _blk(dim, pref):
    return pref if dim % pref == 0 else dim


def _mm(a, b, *, mode, out_dtype, name, tm, tn, tk, res=None, b_lead=None, a_col0=0, a_width=None, b_col0=0,
        b_width=None, out_shards=None):
    dims = {"nn": (((1,), (0,)), ((), ())), "nt": (((1,), (1,)), ((), ())), "tn": (((0,), (0,)), ((), ()))}[mode]
    a_width = a.shape[1] if a_width is None else a_width
    b_last = b.shape[-1] if b_width is None else b_width
    b_rows = b.shape[-2]
    if mode == "nn":
        m, k, n = a.shape[0], a_width, b_last
        assert b_rows == k
    elif mode == "nt":
        m, k, n = a.shape[0], a_width, b_rows
        assert b_last == k
    else:
        k, m, n = a.shape[0], a_width, b_last
        assert b_rows == k
    tm, tn, tk = _blk(m, tm), _blk(n, tn), _blk(k, tk)
    nk = k // tk
    grid = (m // tm, n // tn, nk)

    if mode == "tn":
        assert a_col0 % tm == 0 and b_col0 % tn == 0
        ao, bo = a_col0 // tm, b_col0 // tn
        a_spec = pl.BlockSpec((tk, tm), lambda i, j, kk: (kk, i + ao))
        b_block, b_idx = (tk, tn), (lambda i, j, kk: (kk, j + bo))
    else:
        assert a_col0 % tk == 0
        ao = a_col0 // tk
        a_spec = pl.BlockSpec((tm, tk), lambda i, j, kk: (i, kk + ao))
        if mode == "nn":
            assert b_col0 % tn == 0
            bo = b_col0 // tn
            b_block, b_idx = (tk, tn), (lambda i, j, kk: (kk, j + bo))
        else:
            assert b_col0 % tk == 0
            bo = b_col0 // tk
            b_block, b_idx = (tn, tk), (lambda i, j, kk: (j, kk + bo))
    if b_lead is None:
        b_spec = pl.BlockSpec(b_block, b_idx)
    else:
        b_spec = pl.BlockSpec((None,) + b_block, lambda i, j, kk: (b_lead,) + b_idx(i, j, kk))
    in_specs = [a_spec, b_spec]
    operands = [a, b]
    if res is not None:
        in_specs.append(pl.BlockSpec((tm, tn), lambda i, j, kk: (i, j)))
        operands.append(res)
    if out_shards is None:
        out_shape = jax.ShapeDtypeStruct((m, n), out_dtype)
        out_spec = pl.BlockSpec((tm, tn), lambda i, j, kk: (i, j))
    else:
        per = n // out_shards
        assert per % tn == 0
        nps = per // tn
        out_shape = jax.ShapeDtypeStruct((out_shards, m, per), out_dtype)
        out_spec = pl.BlockSpec((None, tm, tn), lambda i, j, kk: (j // nps, i, j % nps))

    def body(*refs):
        a_ref, b_ref = refs[0], refs[1]
        res_ref = refs[2] if res is not None else None
        o_ref = refs[3] if res is not None else refs[2]
        part = lax.dot_general(a_ref[...].astype(BF16), b_ref[...].astype(BF16), dims, preferred_element_type=F32)

        def finish(acc):
            if res_ref is not None:
                acc = acc + res_ref[...]
            o_ref[...] = acc.astype(out_dtype)

        if nk == 1:
            finish(part)
        else:
            acc_ref = refs[-1]
            kk = pl.program_id(2)

            @pl.when(kk == 0)
            def _():
                acc_ref[...] = part

            @pl.when(kk > 0)
            def _():
                acc_ref[...] += part

            @pl.when(kk == nk - 1)
            def _():
                finish(acc_ref[...])

    scratch = [] if nk == 1 else [pltpu.VMEM((tm, tn), F32)]
    return pl.pallas_call(body, out_shape=out_shape, grid=grid, in_specs=in_specs, out_specs=out_spec,
                          scratch_shapes=scratch, name=name,
                          compiler_params=_params(("parallel", "parallel", "arbitrary")))(*operands)


def _rms_fwd(x, g, name):
    s, d = x.shape
    tm = _blk(s, 512)

    def body(x_ref, g_ref, o_ref):
        xv = x_ref[...]
        r = lax.rsqrt(jnp.mean(xv * xv, axis=-1, keepdims=True) + EPS)
        o_ref[...] = (xv * r * g_ref[...]).astype(BF16)

    return pl.pallas_call(body, out_shape=jax.ShapeDtypeStruct((s, d), BF16), grid=(s // tm,),
                          in_specs=[pl.BlockSpec((tm, d), lambda i: (i, 0)), pl.BlockSpec((1, d), lambda i: (0, 0))],
                          out_specs=pl.BlockSpec((tm, d), lambda i: (i, 0)), name=name,
                          compiler_params=_params(("parallel",)))(x, g)


def _rms_bwd(x, g, dh, dres, name):
    s, d = x.shape
    tm = _blk(s, 512)
    has_res = dres is not None

    def body(*refs):
        x_ref, g_ref, dh_ref = refs[0], refs[1], refs[2]
        dx_ref, dg_ref = refs[-2], refs[-1]
        xv = x_ref[...]
        r = lax.rsqrt(jnp.mean(xv * xv, axis=-1, keepdims=True) + EPS)
        xhat = xv * r
        dhv = dh_ref[...].astype(F32)
        part = jnp.sum(dhv * xhat, axis=0, keepdims=True)

        @pl.when(pl.program_id(0) == 0)
        def _():
            dg_ref[...] = part

        @pl.when(pl.program_id(0) > 0)
        def _():
            dg_ref[...] += part

        dxhat = dhv * g_ref[...]
        dx = r * (dxhat - xhat * jnp.mean(dxhat * xhat, axis=-1, keepdims=True))
        if has_res:
            dx = dx + refs[3][...]
        dx_ref[...] = dx

    row = pl.BlockSpec((tm, d), lambda i: (i, 0))
    vec = pl.BlockSpec((1, d), lambda i: (0, 0))
    in_specs = [row, vec, row] + ([row] if has_res else [])
    operands = [x, g, dh] + ([dres] if has_res else [])
    return pl.pallas_call(body, out_shape=(jax.ShapeDtypeStruct((s, d), F32), jax.ShapeDtypeStruct((1, d), F32)),
                          grid=(s // tm,), in_specs=in_specs, out_specs=(row, vec), name=name,
                          compiler_params=_params(("arbitrary",)))(*operands)


def _final_loss(x, g, target, name):
    s, d = x.shape
    tm = _blk(s, 512)

    def body(x_ref, g_ref, t_ref, loss_ref, dx_ref, dg_ref):
        xv = x_ref[...]
        gv = g_ref[...]
        r = lax.rsqrt(jnp.mean(xv * xv, axis=-1, keepdims=True) + EPS)
        xhat = xv * r
        err = xhat * gv - t_ref[...]
        lpart = 0.5 * jnp.sum(jnp.mean(err * err, axis=-1, keepdims=True), axis=0, keepdims=True)
        dy = err * (1.0 / d)
        gpart = jnp.sum(dy * xhat, axis=0, keepdims=True)

        @pl.when(pl.program_id(0) == 0)
        def _():
            loss_ref[...] = lpart
            dg_ref[...] = gpart

        @pl.when(pl.program_id(0) > 0)
        def _():
            loss_ref[...] += lpart
            dg_ref[...] += gpart

        dxhat = dy * gv
        dx_ref[...] = r * (dxhat - xhat * jnp.mean(dxhat * xhat, axis=-1, keepdims=True))

    row = pl.BlockSpec((tm, d), lambda i: (i, 0))
    vec = pl.BlockSpec((1, d), lambda i: (0, 0))
    one = pl.BlockSpec((1, 1), lambda i: (0, 0))
    return pl.pallas_call(body, out_shape=(jax.ShapeDtypeStruct((1, 1), F32), jax.ShapeDtypeStruct((s, d), F32),
                                           jax.ShapeDtypeStruct((1, d), F32)),
                          grid=(s // tm,), in_specs=[row, vec, row], out_specs=(one, row, vec), name=name,
                          compiler_params=_params(("arbitrary",)))(x, g, target)


def _erf_parts(x):
    cdf = 0.5 * (1.0 + lax.erf(x * (1.0 / math.sqrt(2.0))))
    return cdf


def _shift_down(ext, k):
    return pltpu.roll(ext, k, 0)


def _shift_up(ext, k):
    return pltpu.roll(ext, ext.shape[0] - k, 0)


def _mixer_forward_math(z, halo, row0, prm, tm):
    ln_g, ln_b, wtril, bsb, conv_w, wg, scale = prm
    za = z[:, 0:2 * A_WIDTH]
    gb = z[:, 1024:1536]
    gc = z[:, 1536:2048]
    xin = z[:, 2048:2560]
    zc = z[:, 2560:3072]
    cdf = _erf_parts(za)
    act = za * cdf
    u = act[:, :A_WIDTH]
    v = act[:, A_WIDTH:]
    mu = jnp.mean(v, axis=-1, keepdims=True)
    vc = v - mu
    rstd = lax.rsqrt(jnp.mean(vc * vc, axis=-1, keepdims=True) + EPS)
    vhat = vc * rstd
    vn = (vhat * ln_g + ln_b).astype(BF16)
    rows = []
    for c in range(tm // CHUNK):
        cols = []
        for g in range(A_GROUPS):
            blk = vn[c * CHUNK:(c + 1) * CHUNK, g * CHUNK:(g + 1) * CHUNK]
            cols.append(jnp.dot(wtril[g], blk, preferred_element_type=F32) + bsb[g])
        rows.append(jnp.concatenate(cols, axis=1))
    mixed = jnp.concatenate(rows, axis=0)
    ya = u * mixed
    y = gc * xin
    yext = jnp.concatenate([halo[:, 0:512] * halo[:, 512:1024], y], axis=0)
    y1 = _shift_down(yext, 1)[HALO:]
    y2 = _shift_down(yext, 2)[HALO:]
    conv = conv_w[0:1, :] * y2 + conv_w[1:2, :] * y1 + conv_w[2:3, :] * y
    yb = gb * conv
    t = row0 + lax.broadcasted_iota(jnp.int32, (tm, 1), 0)
    zext = jnp.concatenate([halo[:, 1024:1536], zc], axis=0)
    pooled, p = [], []
    for gi, win in enumerate(POOL_WINDOWS):
        sw = zext[:, gi * C_GROUP:(gi + 1) * C_GROUP]
        for step in range(gi + 1):
            sw = sw + _shift_down(sw, 2 ** step)
        cnt = jnp.minimum(t + 1, win).astype(F32)
        pg = sw[HALO:] / cnt - zc[:, gi * C_GROUP:(gi + 1) * C_GROUP]
        pooled.append(pg)
        p.append(jnp.dot(pg.astype(BF16), wg[gi], preferred_element_type=F32))
    p = jnp.concatenate(p, axis=1)
    yc = p * scale
    saved = dict(za=za, cdf=cdf, u=u, mixed=mixed, rstd=rstd, vhat=vhat, vn=vn, gb=gb, gc=gc, xin=xin, y=y, y1=y1, y2=y2,
                 conv=conv, pooled=pooled, p=p, t=t)
    return ya, yb, yc, saved


def _mixer_specs(tm, s):
    nb16 = tm // HALO
    main = pl.BlockSpec((tm, MIX_COLS), lambda i: (i, 0))
    prev = pl.BlockSpec((HALO, 1536), lambda i: (jnp.maximum(i * nb16 - 1, 0), 1))
    return main, prev


def _full(shape):
    n = len(shape)
    return pl.BlockSpec(shape, lambda i: (0,) * n)


def _mixer_fwd(proj, prm, name):
    s = proj.shape[0]
    tm = _blk(s, 256)
    main, prev = _mixer_specs(tm, s)

    def body(z_ref, h_ref, lng, lnb, wt, bsb, cw, wg, sc, o_ref):
        i = pl.program_id(0)
        halo = jnp.where(i > 0, h_ref[...].astype(F32), 0.0)
        prm_v = (lng[...], lnb[...], wt[...], bsb[...], cw[...], wg[...], sc[...])
        ya, yb, yc, _ = _mixer_forward_math(z_ref[...].astype(F32), halo, i * tm, prm_v, tm)
        o_ref[...] = jnp.concatenate([ya, yb, yc], axis=1).astype(BF16)

    return pl.pallas_call(body, out_shape=jax.ShapeDtypeStruct((s, 1536), BF16), grid=(s // tm,),
                          in_specs=[main, prev] + [_full(p.shape) for p in prm],
                          out_specs=pl.BlockSpec((tm, 1536), lambda i: (i, 0)), name=name,
                          compiler_params=_params(("parallel",)))(proj, proj, *prm)


def _mixer_bwd(proj, dypre, dproj_in, prm, wtril_t, name):
    s = proj.shape[0]
    tm = _blk(s, 256)
    nblk = s // tm
    nb16 = tm // HALO
    main, prev = _mixer_specs(tm, s)
    nxt_row = lambda i: jnp.minimum((i + 1) * nb16, s // HALO - 1)
    next_dy = pl.BlockSpec((HALO, 1536), lambda i: (nxt_row(i), 0))
    next_gb = pl.BlockSpec((HALO, 512), lambda i: (nxt_row(i), 2))
    ones8 = jnp.ones((8, CHUNK), F32)

    def body(z_ref, h_ref, dy_ref, ndy_ref, ngb_ref, _alias, lng, lnb, wt, bsb, cw, wg, sc, wtt, ones_ref,
             dz_ref, dlng_ref, dlnb_ref, dws_ref, dbs_ref, dcw_ref, dwg_ref, dsc_ref):
        i = pl.program_id(0)
        first = i == 0
        halo = jnp.where(i > 0, h_ref[...].astype(F32), 0.0)
        prm_v = (lng[...], lnb[...], wt[...], bsb[...], cw[...], wg[...], sc[...])
        _, _, _, sv = _mixer_forward_math(z_ref[...].astype(F32), halo, i * tm, prm_v, tm)
        dy = dy_ref[...].astype(F32)
        dya, dyb, dyc = dy[:, 0:512], dy[:, 512:1024], dy[:, 1024:1536]
        not_last = i < nblk - 1
        ndy = jnp.where(not_last, ndy_ref[...].astype(F32), 0.0)
        ngb = ngb_ref[...].astype(F32)

        def accumulate(ref, val):
            @pl.when(first)
            def _():
                ref[...] = val

            @pl.when(jnp.logical_not(first))
            def _():
                ref[...] += val

        du = dya * sv["mixed"]
        dmixed = dya * sv["u"]
        vn = sv["vn"]
        wttv = wtt[...]
        dvn_rows = []
        dws_parts = [None] * A_GROUPS
        dbs_parts = [None] * A_GROUPS
        for c in range(tm // CHUNK):
            cols = []
            for g in range(A_GROUPS):
                dm = dmixed[c * CHUNK:(c + 1) * CHUNK, g * CHUNK:(g + 1) * CHUNK]
                dmb = dm.astype(BF16)
                vb = vn[c * CHUNK:(c + 1) * CHUNK, g * CHUNK:(g + 1) * CHUNK]
                w_part = lax.dot_general(dmb, vb, (((1,), (1,)), ((), ())), preferred_element_type=F32)
                b_part = lax.dot_general(ones_ref[...], dm, (((1,), (1,)), ((), ())), preferred_element_type=F32,
                                         precision=lax.Precision.HIGHEST)[0:1, :]
                dws_parts[g] = w_part if dws_parts[g] is None else dws_parts[g] + w_part
                dbs_parts[g] = b_part if dbs_parts[g] is None else dbs_parts[g] + b_part
                cols.append(jnp.dot(wttv[g], dmb, preferred_element_type=F32))
            dvn_rows.append(jnp.concatenate(cols, axis=1))
        dvn = jnp.concatenate(dvn_rows, axis=0)
        tri = lax.broadcasted_iota(jnp.int32, (CHUNK, CHUNK), 0) >= lax.broadcasted_iota(jnp.int32, (CHUNK, CHUNK), 1)
        accumulate(dws_ref, jnp.stack([jnp.where(tri, w, 0.0) for w in dws_parts], axis=0))
        accumulate(dbs_ref, jnp.concatenate(dbs_parts, axis=0))
        vhat = sv["vhat"]
        accumulate(dlng_ref, jnp.sum(dvn * vhat, axis=0, keepdims=True))
        accumulate(dlnb_ref, jnp.sum(dvn, axis=0, keepdims=True))
        dvhat = dvn * lng[...]
        dv = sv["rstd"] * (dvhat - jnp.mean(dvhat, axis=-1, keepdims=True)
                           - vhat * jnp.mean(dvhat * vhat, axis=-1, keepdims=True))
        za = sv["za"]
        dgelu = sv["cdf"] + za * jnp.exp(-0.5 * za * za) * (1.0 / math.sqrt(2.0 * math.pi))
        dza = jnp.concatenate([du, dv], axis=1) * dgelu
        cwv = cw[...]
        dconv = dyb * sv["gb"]
        dgb = dyb * sv["conv"]
        dcext = jnp.concatenate([dconv, ndy[:, 512:1024] * ngb], axis=0)
        d1 = _shift_up(dcext, 1)[:tm]
        d2 = _shift_up(dcext, 2)[:tm]
        dyy = cwv[2:3, :] * dconv + cwv[1:2, :] * d1 + cwv[0:1, :] * d2
        dgc = dyy * sv["xin"]
        dxin = dyy * sv["gc"]
        accumulate(dcw_ref, jnp.concatenate([jnp.sum(dconv * sv["y2"], axis=0, keepdims=True),
                                             jnp.sum(dconv * sv["y1"], axis=0, keepdims=True),
                                             jnp.sum(dconv * sv["y"], axis=0, keepdims=True)], axis=0))
        scv = sc[...]
        accumulate(dsc_ref, jnp.sum(dyc * sv["p"], axis=0, keepdims=True))
        dp = dyc * scv
        ndp = ndy[:, 1024:1536] * scv
        wgv = wg[...]
        dzc_cols, dwg_parts = [], []
        for gi, win in enumerate(POOL_WINDOWS):
            sl = slice(gi * C_GROUP, (gi + 1) * C_GROUP)
            dpb = dp[:, sl].astype(BF16)
            dpool = lax.dot_general(dpb, wgv[gi], (((1,), (1,)), ((), ())), preferred_element_type=F32)
            ndpool = lax.dot_general(ndp[:, sl].astype(BF16), wgv[gi], (((1,), (1,)), ((), ())),
                                     preferred_element_type=F32)
            dwg_parts.append(lax.dot_general(sv["pooled"][gi].astype(BF16), dpb, (((0,), (0,)), ((), ())),
                                             preferred_element_type=F32))
            cnt = jnp.minimum(sv["t"] + 1, win).astype(F32)
            fw = jnp.concatenate([dpool / cnt, ndpool * (1.0 / win)], axis=0)
            for step in range(gi + 1):
                fw = fw + _shift_up(fw, 2 ** step)
            dzc_cols.append(fw[:tm] - dpool)
        accumulate(dwg_ref, jnp.stack(dwg_parts, axis=0))
        dzc = jnp.concatenate(dzc_cols, axis=1)
        dz_ref[...] = jnp.concatenate([dza, dgb, dgc, dxin, dzc], axis=1).astype(BF16)

    out_shape = (jax.ShapeDtypeStruct(dproj_in.shape, BF16), jax.ShapeDtypeStruct((1, A_WIDTH), F32),
                 jax.ShapeDtypeStruct((1, A_WIDTH), F32), jax.ShapeDtypeStruct((A_GROUPS, CHUNK, CHUNK), F32),
                 jax.ShapeDtypeStruct((A_GROUPS, CHUNK), F32), jax.ShapeDtypeStruct((3, B_WIDTH), F32),
                 jax.ShapeDtypeStruct((4, C_GROUP, C_GROUP), F32), jax.ShapeDtypeStruct((1, C_WIDTH), F32))
    out_specs = (main,) + tuple(_full(o.shape) for o in out_shape[1:])
    in_specs = [main, prev, pl.BlockSpec((tm, 1536), lambda i: (i, 0)), next_dy, next_gb, ANY] + \
               [_full(p.shape) for p in prm] + [_full(wtril_t.shape), _full(ones8.shape)]
    return pl.pallas_call(body, out_shape=out_shape, grid=(nblk,), in_specs=in_specs, out_specs=out_specs,
                          input_output_aliases={5: 0}, name=name,
                          compiler_params=_params(("arbitrary",)))(proj, proj, dypre, dypre, proj, dproj_in, *prm,
                                                                  wtril_t, ones8)


def _sigmoid(x):
    return 1.0 / (1.0 + jnp.exp(-x))


def _merge_fwd(ypre, proj, wa, wb, wc, bgate, name):
    s = ypre.shape[0]
    tm = _blk(s, 512)

    def body(y_ref, zg_ref, wa_ref, wb_ref, wc_ref, bg_ref, o_ref):
        yv = y_ref[...]
        acc = None
        for i, w_ref in enumerate((wa_ref, wb_ref, wc_ref)):
            br = jnp.dot(yv[:, i * 512:(i + 1) * 512], w_ref[...], preferred_element_type=F32)
            gate = _sigmoid(zg_ref[:, i * D_MODEL:(i + 1) * D_MODEL].astype(F32) + bg_ref[i:i + 1, :])
            acc = gate * br if acc is None else acc + gate * br
        o_ref[...] = acc.astype(BF16)

    wspec = _full(wa.shape)
    return pl.pallas_call(body, out_shape=jax.ShapeDtypeStruct((s, D_MODEL), BF16), grid=(s // tm,),
                          in_specs=[pl.BlockSpec((tm, 1536), lambda i: (i, 0)),
                                    pl.BlockSpec((tm, 3 * D_MODEL), lambda i: (i, 1)), wspec, wspec, wspec,
                                    _full(bgate.shape)],
                          out_specs=pl.BlockSpec((tm, D_MODEL), lambda i: (i, 0)), name=name,
                          compiler_params=_params(("parallel",)))(ypre, proj, wa, wb, wc, bgate)


def _merge_bwd(ypre, proj, dmerged, wa, wb, wc, bgate, name):
    s = ypre.shape[0]
    tm = _blk(s, 512)

    def body(y_ref, zg_ref, dm_ref, wa_ref, wb_ref, wc_ref, bg_ref, dyp_ref, dbr_ref, dzg_ref, dbg_ref):
        yv = y_ref[...]
        dm = dm_ref[...].astype(F32)
        dyp, dbr, dzg, dbg = [], [], [], []
        for i, w_ref in enumerate((wa_ref, wb_ref, wc_ref)):
            wv = w_ref[...]
            br = jnp.dot(yv[:, i * 512:(i + 1) * 512], wv, preferred_element_type=F32)
            gate = _sigmoid(zg_ref[:, i * D_MODEL:(i + 1) * D_MODEL].astype(F32) + bg_ref[i:i + 1, :])
            dbi = (dm * gate).astype(BF16)
            dzi = dm * br * gate * (1.0 - gate)
            dbr.append(dbi)
            dzg.append(dzi.astype(BF16))
            dbg.append(jnp.sum(dzi, axis=0, keepdims=True))
            dyp.append(lax.dot_general(dbi, wv, (((1,), (1,)), ((), ())), preferred_element_type=F32).astype(BF16))
        dyp_ref[...] = jnp.concatenate(dyp, axis=1)
        dbr_ref[...] = jnp.concatenate(dbr, axis=1)
        dzg_ref[...] = jnp.concatenate(dzg, axis=1)
        part = jnp.concatenate(dbg, axis=0)

        @pl.when(pl.program_id(0) == 0)
        def _():
            dbg_ref[...] = part

        @pl.when(pl.program_id(0) > 0)
        def _():
            dbg_ref[...] += part

    wspec = _full(wa.shape)
    out_shape = (jax.ShapeDtypeStruct((s, 1536), BF16), jax.ShapeDtypeStruct((s, 3 * D_MODEL), BF16),
                 jax.ShapeDtypeStruct((s, IN_COLS), BF16), jax.ShapeDtypeStruct((3, D_MODEL), F32))
    return pl.pallas_call(body, out_shape=out_shape, grid=(s // tm,),
                          in_specs=[pl.BlockSpec((tm, 1536), lambda i: (i, 0)),
                                    pl.BlockSpec((tm, 3 * D_MODEL), lambda i: (i, 1)),
                                    pl.BlockSpec((tm, D_MODEL), lambda i: (i, 0)), wspec, wspec, wspec,
                                    _full(bgate.shape)],
                          out_specs=(pl.BlockSpec((tm, 1536), lambda i: (i, 0)),
                                     pl.BlockSpec((tm, 3 * D_MODEL), lambda i: (i, 0)),
                                     pl.BlockSpec((tm, 3 * D_MODEL), lambda i: (i, 1)), _full((3, D_MODEL))),
                          name=name, compiler_params=_params(("arbitrary",)))(ypre, proj, dmerged, wa, wb, wc, bgate)


def _softmax_rows(q, k):
    sc = lax.dot_general(q, k, (((1,), (1,)), ((), ())), preferred_element_type=F32) * (HEAD_DIM ** -0.5)
    e = jnp.exp(sc - jnp.max(sc, axis=-1, keepdims=True))
    return e / jnp.sum(e, axis=-1, keepdims=True)


def _attn_fwd(q, kv, name):
    s = q.shape[0]
    tm = _blk(s, 512)

    def body(q_ref, kv_ref, o_ref):
        outs = []
        for h in range(XATTN_HEADS):
            sl = slice(h * HEAD_DIM, (h + 1) * HEAD_DIM)
            p = _softmax_rows(q_ref[:, sl], kv_ref[:, sl])
            outs.append(jnp.dot(p.astype(BF16), kv_ref[:, D_MODEL + h * HEAD_DIM:D_MODEL + (h + 1) * HEAD_DIM],
                                preferred_element_type=F32))
        o_ref[...] = jnp.concatenate(outs, axis=1).astype(BF16)

    return pl.pallas_call(body, out_shape=jax.ShapeDtypeStruct((s, D_MODEL), BF16), grid=(s // tm,),
                          in_specs=[pl.BlockSpec((tm, D_MODEL), lambda i: (i, 0)), _full(kv.shape)],
                          out_specs=pl.BlockSpec((tm, D_MODEL), lambda i: (i, 0)), name=name,
                          compiler_params=_params(("parallel",)))(q, kv)


def _attn_bwd(q, kv, do, name):
    s = q.shape[0]
    tm = _blk(s, 512)

    def body(q_ref, kv_ref, do_ref, dq_ref, dkv_ref):
        dqs, dks, dvs = [], [], []
        for h in range(XATTN_HEADS):
            sl = slice(h * HEAD_DIM, (h + 1) * HEAD_DIM)
            vsl = slice(D_MODEL + h * HEAD_DIM, D_MODEL + (h + 1) * HEAD_DIM)
            qh, kh, vh, doh = q_ref[:, sl], kv_ref[:, sl], kv_ref[:, vsl], do_ref[:, sl]
            p = _softmax_rows(qh, kh)
            pb = p.astype(BF16)
            dvs.append(lax.dot_general(pb, doh, (((0,), (0,)), ((), ())), preferred_element_type=F32))
            dp = lax.dot_general(doh, vh, (((1,), (1,)), ((), ())), preferred_element_type=F32)
            ds = p * (dp - jnp.sum(dp * p, axis=-1, keepdims=True)) * (HEAD_DIM ** -0.5)
            dsb = ds.astype(BF16)
            dqs.append(jnp.dot(dsb, kh, preferred_element_type=F32))
            dks.append(lax.dot_general(dsb, qh, (((0,), (0,)), ((), ())), preferred_element_type=F32))
        dq_ref[...] = jnp.concatenate(dqs, axis=1).astype(BF16)
        part = jnp.concatenate(dks + dvs, axis=1)

        @pl.when(pl.program_id(0) == 0)
        def _():
            dkv_ref[...] = part

        @pl.when(pl.program_id(0) > 0)
        def _():
            dkv_ref[...] += part

    row = pl.BlockSpec((tm, D_MODEL), lambda i: (i, 0))
    return pl.pallas_call(body, out_shape=(jax.ShapeDtypeStruct((s, D_MODEL), BF16),
                                           jax.ShapeDtypeStruct(kv.shape, F32)),
                          grid=(s // tm,), in_specs=[row, _full(kv.shape), row], out_specs=(row, _full(kv.shape)),
                          name=name, compiler_params=_params(("arbitrary",)))(q, kv, do)


def _swiglu_fwd(gu, name):
    s = gu.shape[0]
    tm = _blk(s, 512)
    half = pl.BlockSpec((tm, FFN_HIDDEN), lambda i: (i, 0))
    up = pl.BlockSpec((tm, FFN_HIDDEN), lambda i: (i, 1))

    def body(g_ref, u_ref, o_ref):
        g = g_ref[...].astype(F32)
        o_ref[...] = (g * _sigmoid(g) * u_ref[...].astype(F32)).astype(BF16)

    return pl.pallas_call(body, out_shape=jax.ShapeDtypeStruct((s, FFN_HIDDEN), BF16), grid=(s // tm,),
                          in_specs=[half, up], out_specs=half, name=name,
                          compiler_params=_params(("parallel",)))(gu, gu)


def _swiglu_bwd(gu, dact, name):
    s = gu.shape[0]
    tm = _blk(s, 512)
    half = pl.BlockSpec((tm, FFN_HIDDEN), lambda i: (i, 0))
    up = pl.BlockSpec((tm, FFN_HIDDEN), lambda i: (i, 1))

    def body(g_ref, u_ref, d_ref, o_ref):
        g = g_ref[...].astype(F32)
        u = u_ref[...].astype(F32)
        d = d_ref[...].astype(F32)
        sg = _sigmoid(g)
        dg = d * u * sg * (1.0 + g * (1.0 - sg))
        du = d * g * sg
        o_ref[...] = jnp.concatenate([dg, du], axis=1).astype(BF16)

    return pl.pallas_call(body, out_shape=jax.ShapeDtypeStruct((s, 2 * FFN_HIDDEN), BF16), grid=(s // tm,),
                          in_specs=[half, up, half], out_specs=pl.BlockSpec((tm, 2 * FFN_HIDDEN), lambda i: (i, 0)),
                          name=name, compiler_params=_params(("parallel",)))(gu, gu, dact)


def _rows_block(rows, cols):
    target = (512 * 1024) // cols
    fits = [cand for cand in range(8, rows + 1, 8) if rows % cand == 0 and cand <= target]
    return fits[-1] if fits else rows


def _elementwise(fn, ins, out_dtypes, name):
    lead, rows, cols = ins[0].shape
    tr = _rows_block(rows, cols)
    spec = pl.BlockSpec((None, tr, cols), lambda l, i: (l, i, 0))
    n_in = len(ins)

    def body(*refs):
        outs = fn(*[r[...] for r in refs[:n_in]])
        for o_ref, o in zip(refs[n_in:], outs):
            o_ref[...] = o.astype(o_ref.dtype)

    out_shape = tuple(jax.ShapeDtypeStruct(ins[0].shape, dt) for dt in out_dtypes)
    return pl.pallas_call(body, out_shape=out_shape, grid=(lead, rows // tr), in_specs=[spec] * n_in,
                          out_specs=tuple([spec] * len(out_dtypes)), name=name,
                          compiler_params=_params(("parallel", "parallel")))(*ins)


def _sum_lead(x, out_dtype, name):
    lead, rows, cols = x.shape
    tr = _rows_block(rows, cols)

    def body(x_ref, o_ref):
        acc = x_ref[0].astype(F32)
        for k in range(1, lead):
            acc = acc + x_ref[k].astype(F32)
        o_ref[...] = acc.astype(out_dtype)

    return pl.pallas_call(body, out_shape=jax.ShapeDtypeStruct((rows, cols), out_dtype), grid=(rows // tr,),
                          in_specs=[pl.BlockSpec((lead, tr, cols), lambda i: (0, i, 0))],
                          out_specs=pl.BlockSpec((tr, cols), lambda i: (i, 0)), name=name,
                          compiler_params=_params(("parallel",)))(x)


def _adamw_math(w, g, m, v):
    m = ADAM_B1 * m + (1.0 - ADAM_B1) * g
    v = ADAM_B2 * v + (1.0 - ADAM_B2) * (g * g)
    m_hat = m / (1.0 - ADAM_B1 ** ADAM_STEP)
    v_hat = v / (1.0 - ADAM_B2 ** ADAM_STEP)
    delta = -ADAM_LR * (m_hat / (jnp.sqrt(v_hat) + ADAM_EPS) + ADAM_WD * w)
    return delta, m, v


def _place():
    x, y, c = lax.axis_index("x"), lax.axis_index("y"), lax.axis_index("c")
    chips = [(1 - x, y), (x, 1 - y), (1 - x, 1 - y)]
    return x, y, c, 2 * x + y, chips


def _run_copies(local, remote):
    for cp in local + [r[0] for r in remote]:
        cp.start()
    for cp in local:
        cp.wait()
    for send, recv in remote:
        send.wait_send()
        recv.wait_recv()


def _chip_gather(items, name):
    n = len(items)

    def body(*refs):
        ins, outs = refs[:n], refs[n:2 * n]
        lsem, ssem, rsem = refs[2 * n:]
        x, y, c, me, chips = _place()
        local, remote = [], []
        for w in range(n):
            src = ins[w].at[c]
            local.append(pltpu.make_async_copy(src, outs[w].at[me], lsem.at[w]))
            for j, (px, py) in enumerate(chips):
                send = pltpu.make_async_remote_copy(src, outs[w].at[me], ssem.at[3 * w + j], rsem.at[3 * w + j],
                                                    device_id=(px, py, c), device_id_type=MESH)
                recv = pltpu.make_async_remote_copy(src, outs[w].at[2 * px + py], ssem.at[3 * w + j],
                                                    rsem.at[3 * w + j], device_id=(px, py, c), device_id_type=MESH)
                remote.append((send, recv))
        _run_copies(local, remote)

    out_shape = tuple(jax.ShapeDtypeStruct((N_CHIPS,) + it.shape[1:], it.dtype) for it in items)
    return pl.pallas_call(body, out_shape=out_shape, in_specs=[ANY] * n, out_specs=tuple([ANY] * n),
                          scratch_shapes=[pltpu.SemaphoreType.DMA((n,)), pltpu.SemaphoreType.DMA((3 * n,)),
                                          pltpu.SemaphoreType.DMA((3 * n,))],
                          name=name)(*items)


def _core_gather(items, name):
    n = len(items)

    def body(*refs):
        ins, outs = refs[:n], refs[n:2 * n]
        lsem, ssem, rsem = refs[2 * n:]
        x, y, c, me, chips = _place()
        local, remote = [], []
        for w in range(n):
            local.append(pltpu.make_async_copy(ins[w], outs[w].at[c], lsem.at[w]))
            send = pltpu.make_async_remote_copy(ins[w], outs[w].at[c], ssem.at[w], rsem.at[w],
                                                device_id=(x, y, 1 - c), device_id_type=MESH)
            recv = pltpu.make_async_remote_copy(ins[w], outs[w].at[1 - c], ssem.at[w], rsem.at[w],
                                                device_id=(x, y, 1 - c), device_id_type=MESH)
            remote.append((send, recv))
        _run_copies(local, remote)

    out_shape = tuple(jax.ShapeDtypeStruct((2,) + it.shape, it.dtype) for it in items)
    return pl.pallas_call(body, out_shape=out_shape, in_specs=[ANY] * n, out_specs=tuple([ANY] * n),
                          scratch_shapes=[pltpu.SemaphoreType.DMA((n,)), pltpu.SemaphoreType.DMA((n,)),
                                          pltpu.SemaphoreType.DMA((n,))],
                          name=name)(*items)


def _core_split(g0, g1, small, name):
    n = len(g0)

    def body(*refs):
        in0, in1, small_ref = refs[:n], refs[n:2 * n], refs[2 * n]
        own, rx, small_rx = refs[2 * n + 1:3 * n + 1], refs[3 * n + 1:4 * n + 1], refs[4 * n + 1]
        lsem, ssem, rsem = refs[4 * n + 2:]
        x, y, c, me, chips = _place()
        sib = (x, y, 1 - c)

        def run(keep, give):
            local, remote = [], []
            for w in range(n):
                local.append(pltpu.make_async_copy(keep[w], own[w], lsem.at[w]))
                cp = pltpu.make_async_remote_copy(give[w], rx[w], ssem.at[w], rsem.at[w], device_id=sib,
                                                  device_id_type=MESH)
                remote.append((cp, cp))
            cp = pltpu.make_async_remote_copy(small_ref, small_rx, ssem.at[n], rsem.at[n], device_id=sib,
                                              device_id_type=MESH)
            remote.append((cp, cp))
            _run_copies(local, remote)

        @pl.when(c == 0)
        def _():
            run(in0, in1)

        @pl.when(c == 1)
        def _():
            run(in1, in0)

    out_shape = tuple(jax.ShapeDtypeStruct(g.shape, g.dtype) for g in g0) * 2 + \
                (jax.ShapeDtypeStruct(small.shape, small.dtype),)
    n_out = 2 * n + 1
    return pl.pallas_call(body, out_shape=out_shape, in_specs=[ANY] * (2 * n + 1), out_specs=tuple([ANY] * n_out),
                          scratch_shapes=[pltpu.SemaphoreType.DMA((n,)), pltpu.SemaphoreType.DMA((n + 1,)),
                                          pltpu.SemaphoreType.DMA((n + 1,))],
                          name=name)(*g0, *g1, small)


def _chip_scatter(items, small, name):
    n = len(items)

    def body(*refs):
        ins, small_ref = refs[:n], refs[n]
        outs, small_out = refs[n + 1:2 * n + 1], refs[2 * n + 1]
        lsem, ssem, rsem = refs[2 * n + 2:]
        x, y, c, me, chips = _place()
        local, remote = [], []
        for w in range(n + 1):
            if w < n:
                local.append(pltpu.make_async_copy(ins[w].at[me], outs[w].at[me], lsem.at[w]))
            else:
                local.append(pltpu.make_async_copy(small_ref, small_out.at[me], lsem.at[w]))
            for j, (px, py) in enumerate(chips):
                k = 2 * px + py
                if w < n:
                    src, dst_there, dst_here = ins[w].at[k], outs[w].at[me], outs[w].at[k]
                else:
                    src, dst_there, dst_here = small_ref, small_out.at[me], small_out.at[k]
                send = pltpu.make_async_remote_copy(src, dst_there, ssem.at[3 * w + j], rsem.at[3 * w + j],
                                                    device_id=(px, py, c), device_id_type=MESH)
                recv = pltpu.make_async_remote_copy(src, dst_here, ssem.at[3 * w + j], rsem.at[3 * w + j],
                                                    device_id=(px, py, c), device_id_type=MESH)
                remote.append((send, recv))
        _run_copies(local, remote)

    out_shape = tuple(jax.ShapeDtypeStruct(it.shape, it.dtype) for it in items) + \
                (jax.ShapeDtypeStruct((N_CHIPS,) + small.shape, small.dtype),)
    return pl.pallas_call(body, out_shape=out_shape, in_specs=[ANY] * (n + 1), out_specs=tuple([ANY] * (n + 1)),
                          scratch_shapes=[pltpu.SemaphoreType.DMA((n + 1,)), pltpu.SemaphoreType.DMA((3 * n + 3,)),
                                          pltpu.SemaphoreType.DMA((3 * n + 3,))],
                          name=name)(*items, small)


def _pack(arrays):
    flat = jnp.concatenate([a.reshape(-1).astype(F32) for a in arrays])
    pad = (-flat.shape[0]) % (8 * LANES)
    return jnp.pad(flat, (0, pad)).reshape(-1, LANES)


def _unpack(packed, shapes):
    flat = packed.reshape(-1)
    out, off = [], 0
    for shp in shapes:
        size = math.prod(shp)
        out.append(flat[off:off + size].reshape(shp))
        off += size
    return out


def _as3d(a):
    return a.reshape((a.shape[0], -1, a.shape[-1])) if a.ndim != 3 else a


def kernel(x, mem, g_mix, w_in, b_gate, a_ln_g, a_ln_b, a_ws, a_bs, b_conv, c_wg, c_scale, w_branch_a, w_branch_b, w_branch_c, w_o, g_xattn, g_mem, w_xq, w_xkv, w_xo, g_ffn, w_ffn_in, w_ffn_out, g_final, loss_target, m_g_mix, m_w_in, m_b_gate, m_a_ln_g, m_a_ln_b, m_a_ws, m_a_bs, m_b_conv, m_c_wg, m_c_scale, m_w_branch_a, m_w_branch_b, m_w_branch_c, m_w_o, m_g_xattn, m_g_mem, m_w_xq, m_w_xkv, m_w_xo, m_g_ffn, m_w_ffn_in, m_w_ffn_out, m_g_final, v_g_mix, v_w_in, v_b_gate, v_a_ln_g, v_a_ln_b, v_a_ws, v_a_bs, v_b_conv, v_c_wg, v_c_scale, v_w_branch_a, v_w_branch_b, v_w_branch_c, v_w_o, v_g_xattn, v_g_mem, v_w_xq, v_w_xkv, v_w_xo, v_g_ffn, v_w_ffn_in, v_w_ffn_out, v_g_final):
    args = locals()
    wts = {n: args[n] for n in WEIGHTS}
    mom = {n: args["m_" + n] for n in WEIGHTS}
    var = {n: args["v_" + n] for n in WEIGHTS}
    xs = x[0]
    mems = mem[0]
    tgt = loss_target[0]
    chip = 2 * lax.axis_index("x") + lax.axis_index("y")

    bias_pack = jnp.zeros((DEPTH, 8, 384), F32)
    bias_pack = bias_pack.at[:, 0:3, 0:256].set(b_gate).at[:, 0:3, 256:384].set(b_conv)
    casted = [_elementwise(lambda a: (a,), [wts[n]], [BF16], name=f"cast_{n}")[0] for n in BIG]
    gathered = _chip_gather(casted + [bias_pack], name="gather_chips")
    full = _core_gather(list(gathered), name="gather_cores")
    fw = {}
    for n, g in zip(BIG, full[:-1]):
        if n in COL_SHARDED:
            fw[n] = jnp.transpose(g, (0, 2, 1, 3)).reshape(DEPTH, g.shape[2], N_CHIPS * g.shape[3])
        else:
            fw[n] = g.reshape(DEPTH, N_CHIPS * g.shape[2], g.shape[3])
    biases = full[-1]
    b_gate_full = jnp.transpose(biases[:, :, 0:3, 0:256], (0, 2, 1, 3)).reshape(DEPTH, 3, D_MODEL)
    b_conv_full = jnp.transpose(biases[:, :, 0:3, 256:384], (0, 2, 1, 3)).reshape(DEPTH, 3, B_WIDTH)

    tril = jnp.tril(jnp.ones((CHUNK, CHUNK), bool))

    def mixer_params(l):
        wtril = jnp.where(tril[None], a_ws[l], 0.0)
        prm = (a_ln_g[l][None], a_ln_b[l][None], wtril.astype(BF16),
               jnp.broadcast_to(a_bs[l][:, :, None], (A_GROUPS, CHUNK, CHUNK)), b_conv_full[l],
               c_wg[l].astype(BF16), c_scale[l][None])
        return prm, jnp.swapaxes(wtril, 1, 2).astype(BF16)

    saved = []
    xc = xs
    for l in range(DEPTH):
        sv = {"x0": xc}
        prm, _ = mixer_params(l)
        sv["h"] = _rms_fwd(xc, g_mix[l][None], name=f"l{l}_mix_norm")
        sv["proj"] = _mm(sv["h"], fw["w_in"], b_lead=l, mode="nn", out_dtype=BF16, name=f"l{l}_proj", tm=1024,
                         tn=512, tk=1024)
        sv["ypre"] = _mixer_fwd(sv["proj"], prm, name=f"l{l}_mixers")
        sv["merged"] = _merge_fwd(sv["ypre"], sv["proj"], fw["w_branch_a"][l], fw["w_branch_b"][l],
                                  fw["w_branch_c"][l], b_gate_full[l], name=f"l{l}_merge")
        xc = _mm(sv["merged"], fw["w_o"], b_lead=l, mode="nn", out_dtype=F32, name=f"l{l}_mix_out", tm=1024, tn=512,
                 tk=1024, res=xc)
        sv["x1"] = xc
        sv["hq"] = _rms_fwd(xc, g_xattn[l][None], name=f"l{l}_xattn_norm")
        sv["m"] = _rms_fwd(mems, g_mem[l][None], name=f"l{l}_mem_norm")
        sv["q"] = _mm(sv["hq"], fw["w_xq"], b_lead=l, mode="nn", out_dtype=BF16, name=f"l{l}_q", tm=1024, tn=512,
                      tk=1024)
        sv["kv"] = _mm(sv["m"], fw["w_xkv"], b_lead=l, mode="nn", out_dtype=BF16, name=f"l{l}_kv", tm=256, tn=512,
                       tk=1024)
        sv["o"] = _attn_fwd(sv["q"], sv["kv"], name=f"l{l}_attn")
        xc = _mm(sv["o"], fw["w_xo"], b_lead=l, mode="nn", out_dtype=F32, name=f"l{l}_xattn_out", tm=1024, tn=512,
                 tk=1024, res=xc)
        sv["x2"] = xc
        sv["h2"] = _rms_fwd(xc, g_ffn[l][None], name=f"l{l}_ffn_norm")
        sv["gu"] = _mm(sv["h2"], fw["w_ffn_in"], b_lead=l, mode="nn", out_dtype=BF16, name=f"l{l}_ffn_in", tm=1024,
                       tn=512, tk=1024)
        sv["act"] = _swiglu_fwd(sv["gu"], name=f"l{l}_swiglu")
        xc = _mm(sv["act"], fw["w_ffn_out"], b_lead=l, mode="nn", out_dtype=F32, name=f"l{l}_ffn_out", tm=1024,
                 tn=512, tk=FFN_HIDDEN, res=xc)
        saved.append(sv)

    loss_part, dx, dg_final = _final_loss(xc, g_final[None], tgt, name="final_loss")
    loss = lax.psum(loss_part[0, 0], ("x", "y", "c"))

    big_grads = [None] * DEPTH
    small_grads = [None] * DEPTH
    for l in reversed(range(DEPTH)):
        sv = saved[l]
        prm, wtril_t = mixer_params(l)
        gb = {}
        sg = {}
        dact = _mm(dx, fw["w_ffn_out"], b_lead=l, mode="nt", out_dtype=BF16, name=f"l{l}_d_act", tm=1024, tn=1408,
                   tk=1024)
        gb["w_ffn_out"] = _mm(sv["act"], dx, mode="tn", out_dtype=BF16, name=f"l{l}_dw_ffn_out", tm=1408, tn=1024,
                              tk=1024).reshape(N_CHIPS, FFN_HIDDEN // N_CHIPS, D_MODEL)
        dgu = _swiglu_bwd(sv["gu"], dact, name=f"l{l}_d_swiglu")
        dh2 = _mm(dgu, fw["w_ffn_in"], b_lead=l, mode="nt", out_dtype=BF16, name=f"l{l}_d_h2", tm=1024, tn=1024,
                  tk=1408)
        gb["w_ffn_in"] = _mm(sv["h2"], dgu, mode="tn", out_dtype=BF16, name=f"l{l}_dw_ffn_in", tm=1024, tn=1408,
                             tk=1024, out_shards=N_CHIPS)
        dx, sg["g_ffn"] = _rms_bwd(sv["x2"], g_ffn[l][None], dh2, dx, name=f"l{l}_d_ffn_norm")
        do = _mm(dx, fw["w_xo"], b_lead=l, mode="nt", out_dtype=BF16, name=f"l{l}_d_o", tm=1024, tn=1024, tk=1024)
        gb["w_xo"] = _mm(sv["o"], dx, mode="tn", out_dtype=BF16, name=f"l{l}_dw_xo", tm=1024, tn=1024,
                         tk=1024).reshape(N_CHIPS, D_MODEL // N_CHIPS, D_MODEL)
        dq, dkv = _attn_bwd(sv["q"], sv["kv"], do, name=f"l{l}_d_attn")
        dhq = _mm(dq, fw["w_xq"], b_lead=l, mode="nt", out_dtype=BF16, name=f"l{l}_d_hq", tm=1024, tn=1024, tk=1024)
        gb["w_xq"] = _mm(sv["hq"], dq, mode="tn", out_dtype=BF16, name=f"l{l}_dw_xq", tm=1024, tn=1024,
                         tk=1024).reshape(N_CHIPS, D_MODEL // N_CHIPS, D_MODEL)
        dm = _mm(dkv, fw["w_xkv"], b_lead=l, mode="nt", out_dtype=BF16, name=f"l{l}_d_m", tm=256, tn=1024, tk=1024)
        gb["w_xkv"] = _mm(sv["m"], dkv, mode="tn", out_dtype=BF16, name=f"l{l}_dw_xkv", tm=1024, tn=512, tk=256,
                          out_shards=N_CHIPS)
        _, sg["g_mem"] = _rms_bwd(mems, g_mem[l][None], dm, None, name=f"l{l}_d_mem_norm")
        dx, sg["g_xattn"] = _rms_bwd(sv["x1"], g_xattn[l][None], dhq, dx, name=f"l{l}_d_xattn_norm")
        dmerged = _mm(dx, fw["w_o"], b_lead=l, mode="nt", out_dtype=BF16, name=f"l{l}_d_merged", tm=1024, tn=1024,
                      tk=1024)
        gb["w_o"] = _mm(sv["merged"], dx, mode="tn", out_dtype=BF16, name=f"l{l}_dw_o", tm=1024, tn=1024,
                        tk=1024).reshape(N_CHIPS, D_MODEL // N_CHIPS, D_MODEL)
        dypre, dbranch, dproj, sg["b_gate"] = _merge_bwd(sv["ypre"], sv["proj"], dmerged, fw["w_branch_a"][l],
                                                         fw["w_branch_b"][l], fw["w_branch_c"][l], b_gate_full[l],
                                                         name=f"l{l}_d_merge")
        for i, n in enumerate(("w_branch_a", "w_branch_b", "w_branch_c")):
            gb[n] = _mm(sv["ypre"], dbranch, mode="tn", out_dtype=BF16, name=f"l{l}_d{n}", tm=512, tn=256, tk=1024,
                        a_col0=512 * i, a_width=512, b_col0=D_MODEL * i, b_width=D_MODEL, out_shards=N_CHIPS)
        (dproj, sg["a_ln_g"], sg["a_ln_b"], sg["a_ws"], sg["a_bs"], sg["b_conv"], sg["c_wg"],
         sg["c_scale"]) = _mixer_bwd(sv["proj"], dypre, dproj, prm, wtril_t, name=f"l{l}_d_mixers")
        dh = _mm(dproj, fw["w_in"], b_lead=l, mode="nt", out_dtype=BF16, name=f"l{l}_d_h", tm=1024, tn=1024, tk=1536)
        gb["w_in"] = _mm(sv["h"], dproj, mode="tn", out_dtype=BF16, name=f"l{l}_dw_in", tm=1024, tn=1536, tk=1024,
                         out_shards=N_CHIPS)
        dx, sg["g_mix"] = _rms_bwd(sv["x0"], g_mix[l][None], dh, dx, name=f"l{l}_d_mix_norm")
        big_grads[l] = [gb[n] for n in BIG]
        small_grads[l] = sg
    grad_x = dx[None]

    small_names = [n for n in WEIGHTS if n not in BIG]
    small_full_shapes = {n: ((DEPTH, 3, D_MODEL) if n == "b_gate" else (DEPTH, 3, B_WIDTH) if n == "b_conv"
                             else wts[n].shape) for n in small_names}
    small_local = []
    for n in small_names:
        if n == "g_final":
            small_local.append(dg_final)
        else:
            small_local.append(jnp.stack([small_grads[l][n].reshape(small_full_shapes[n][1:]) for l in range(DEPTH)]))
    small_pack = _pack(small_local)

    split = _core_split(big_grads[0], big_grads[1], small_pack, name="reduce_cores")
    nb = len(BIG)
    own, rx, small_rx = split[:nb], split[nb:2 * nb], split[2 * nb]
    pair = [_elementwise(lambda a, b: (a.astype(F32) + b.astype(F32),), [own[i], rx[i]], [BF16],
                         name=f"pair_sum_{BIG[i]}")[0] for i in range(nb)]
    small_pair = _elementwise(lambda a, b: (a + b,), [small_pack[None], small_rx[None]], [F32],
                              name="pair_sum_small")[0][0]
    scattered = _chip_scatter(pair, small_pair, name="reduce_chips")
    layer_grads = [_sum_lead(scattered[i], F32, name=f"chip_sum_{BIG[i]}") for i in range(nb)]
    small_sum = _sum_lead(scattered[nb], F32, name="chip_sum_small")
    both = _core_gather(layer_grads, name="share_cores")

    out_g, out_d, out_m, out_v = {}, {}, {}, {}
    for n, g in zip(BIG, both):
        g = g.reshape(wts[n].shape)
        d, m_new, v_new = _elementwise(_adamw_math, [_as3d(wts[n]), _as3d(g), _as3d(mom[n]), _as3d(var[n])],
                                       [F32, F32, F32], name=f"adamw_{n}")
        out_g[n], out_d[n], out_m[n], out_v[n] = g, d.reshape(g.shape), m_new.reshape(g.shape), v_new.reshape(g.shape)
    small_g = dict(zip(small_names, _unpack(small_sum, [small_full_shapes[n] for n in small_names])))
    small_g["b_gate"] = lax.dynamic_slice_in_dim(small_g["b_gate"], chip * 256, 256, axis=2)
    small_g["b_conv"] = lax.dynamic_slice_in_dim(small_g["b_conv"], chip * 128, 128, axis=2)
    packs = [_pack([d[n] for n in small_names])[None] for d in (wts, small_g, mom, var)]
    upd = _elementwise(_adamw_math, packs, [F32, F32, F32], name="adamw_small")
    shapes = [wts[n].shape for n in small_names]
    for n, d, m_new, v_new in zip(small_names, *[_unpack(u[0], shapes) for u in upd]):
        out_g[n], out_d[n], out_m[n], out_v[n] = small_g[n], d, m_new, v_new

    return (loss, grad_x, *[out_g[n] for n in WEIGHTS], *[out_d[n] for n in WEIGHTS], *[out_m[n] for n in WEIGHTS],
            *[out_v[n] for n in WEIGHTS])
```

```python
import functools
import math

import jax
import jax.numpy as jnp
from jax import lax
from jax.experimental import pallas as pl
from jax.experimental.pallas import tpu as pltpu

F32 = jnp.float32
BF16 = jnp.bfloat16

D_MODEL = 1024
DEPTH = 2
MEM_LEN = 256
EPS = 1e-6
CHUNK = 128
A_GROUPS = 4
A_WIDTH = 512
B_WIDTH = 512
C_WIDTH = 512
C_GROUP = 128
POOL_WINDOWS = (2, 4, 8, 16)
HALO = 16
IN_COLS = 6144
MIX_COLS = 3072
XATTN_HEADS = 4
HEAD_DIM = 256
FFN_HIDDEN = 2816
N_CHIPS = 4

ADAM_LR = 0.001
ADAM_B1 = 0.9
ADAM_B2 = 0.999
ADAM_EPS = 1e-08
ADAM_WD = 0.01
ADAM_STEP = 10

V7X_VMEM_BYTES = 64 * 1024 * 1024
VMEM_LIMIT = (V7X_VMEM_BYTES * 3) // 4
LANES = 128
MESH = pl.DeviceIdType.MESH
ANY = pl.BlockSpec(memory_space=pl.ANY)

BIG = ("w_in", "w_branch_a", "w_branch_b", "w_branch_c", "w_o", "w_xq", "w_xkv", "w_xo", "w_ffn_in", "w_ffn_out")
COL_SHARDED = ("w_in", "w_branch_a", "w_branch_b", "w_branch_c", "w_xkv", "w_ffn_in")
SMALL_REPL = ("g_mix", "a_ln_g", "a_ln_b", "a_ws", "a_bs", "c_wg", "c_scale", "g_xattn", "g_mem", "g_ffn", "g_final")
SMALL_SHARDED = ("b_gate", "b_conv")
WEIGHTS = ("g_mix", "w_in", "b_gate", "a_ln_g", "a_ln_b", "a_ws", "a_bs", "b_conv", "c_wg", "c_scale", "w_branch_a",
           "w_branch_b", "w_branch_c", "w_o", "g_xattn", "g_mem", "w_xq", "w_xkv", "w_xo", "g_ffn", "w_ffn_in",
           "w_ffn_out", "g_final")


def _params(sem=None):
    return pltpu.CompilerParams(dimension_semantics=sem, vmem_limit_bytes=VMEM_LIMIT)


def _blk(dim, pref):
    return pref if dim % pref == 0 else dim


def _mm(a, b, *, mode, out_dtype, name, tm, tn, tk, res=None, b_lead=None, a_col0=0, a_width=None, b_col0=0,
        b_width=None, out_shards=None):
    dims = {"nn": (((1,), (0,)), ((), ())), "nt": (((1,), (1,)), ((), ())), "tn": (((0,), (0,)), ((), ()))}[mode]
    a_width = a.shape[1] if a_width is None else a_width
    b_last = b.shape[-1] if b_width is None else b_width
    b_rows = b.shape[-2]
    if mode == "nn":
        m, k, n = a.shape[0], a_width, b_last
        assert b_rows == k
    elif mode == "nt":
        m, k, n = a.shape[0], a_width, b_rows
        assert b_last == k
    else:
        k, m, n = a.shape[0], a_width, b_last
        assert b_rows == k
    tm, tn, tk = _blk(m, tm), _blk(n, tn), _blk(k, tk)
    nk = k // tk
    grid = (m // tm, n // tn, nk)

    if mode == "tn":
        assert a_col0 % tm == 0 and b_col0 % tn == 0
        ao, bo = a_col0 // tm, b_col0 // tn
        a_spec = pl.BlockSpec((tk, tm), lambda i, j, kk: (kk, i + ao))
        b_block, b_idx = (tk, tn), (lambda i, j, kk: (kk, j + bo))
    else:
        assert a_col0 % tk == 0
        ao = a_col0 // tk
        a_spec = pl.BlockSpec((tm, tk), lambda i, j, kk: (i, kk + ao))
        if mode == "nn":
            assert b_col0 % tn == 0
            bo = b_col0 // tn
            b_block, b_idx = (tk, tn), (lambda i, j, kk: (kk, j + bo))
        else:
            assert b_col0 % tk == 0
            bo = b_col0 // tk
            b_block, b_idx = (tn, tk), (lambda i, j, kk: (j, kk + bo))
    if b_lead is None:
        b_spec = pl.BlockSpec(b_block, b_idx)
    else:
        b_spec = pl.BlockSpec((None,) + b_block, lambda i, j, kk: (b_lead,) + b_idx(i, j, kk))
    in_specs = [a_spec, b_spec]
    operands = [a, b]
    if res is not None:
        in_specs.append(pl.BlockSpec((tm, tn), lambda i, j, kk: (i, j)))
        operands.append(res)
    if out_shards is None:
        out_shape = jax.ShapeDtypeStruct((m, n), out_dtype)
        out_spec = pl.BlockSpec((tm, tn), lambda i, j, kk: (i, j))
    else:
        per = n // out_shards
        assert per % tn == 0
        nps = per // tn
        out_shape = jax.ShapeDtypeStruct((out_shards, m, per), out_dtype)
        out_spec = pl.BlockSpec((None, tm, tn), lambda i, j, kk: (j // nps, i, j % nps))

    def body(*refs):
        a_ref, b_ref = refs[0], refs[1]
        res_ref = refs[2] if res is not None else None
        o_ref = refs[3] if res is not None else refs[2]
        part = lax.dot_general(a_ref[...].astype(BF16), b_ref[...].astype(BF16), dims, preferred_element_type=F32)

        def finish(acc):
            if res_ref is not None:
                acc = acc + res_ref[...]
            o_ref[...] = acc.astype(out_dtype)

        if nk == 1:
            finish(part)
        else:
            acc_ref = refs[-1]
            kk = pl.program_id(2)

            @pl.when(kk == 0)
            def _():
                acc_ref[...] = part

            @pl.when(kk > 0)
            def _():
                acc_ref[...] += part

            @pl.when(kk == nk - 1)
            def _():
                finish(acc_ref[...])

    scratch = [] if nk == 1 else [pltpu.VMEM((tm, tn), F32)]
    return pl.pallas_call(body, out_shape=out_shape, grid=grid, in_specs=in_specs, out_specs=out_spec,
                          scratch_shapes=scratch, name=name,
                          compiler_params=_params(("parallel", "parallel", "arbitrary")))(*operands)


def _rms_fwd(x, g, name):
    s, d = x.shape
    tm = _blk(s, 512)

    def body(x_ref, g_ref, o_ref):
        xv = x_ref[...]
        r = lax.rsqrt(jnp.mean(xv * xv, axis=-1, keepdims=True) + EPS)
        o_ref[...] = (xv * r * g_ref[...]).astype(BF16)

    return pl.pallas_call(body, out_shape=jax.ShapeDtypeStruct((s, d), BF16), grid=(s // tm,),
                          in_specs=[pl.BlockSpec((tm, d), lambda i: (i, 0)), pl.BlockSpec((1, d), lambda i: (0, 0))],
                          out_specs=pl.BlockSpec((tm, d), lambda i: (i, 0)), name=name,
                          compiler_params=_params(("parallel",)))(x, g)


def _rms_bwd(x, g, dh, dres, name):
    s, d = x.shape
    tm = _blk(s, 512)
    has_res = dres is not None

    def body(*refs):
        x_ref, g_ref, dh_ref = refs[0], refs[1], refs[2]
        dx_ref, dg_ref = refs[-2], refs[-1]
        xv = x_ref[...]
        r = lax.rsqrt(jnp.mean(xv * xv, axis=-1, keepdims=True) + EPS)
        xhat = xv * r
        dhv = dh_ref[...].astype(F32)
        part = jnp.sum(dhv * xhat, axis=0, keepdims=True)

        @pl.when(pl.program_id(0) == 0)
        def _():
            dg_ref[...] = part

        @pl.when(pl.program_id(0) > 0)
        def _():
            dg_ref[...] += part

        dxhat = dhv * g_ref[...]
        dx = r * (dxhat - xhat * jnp.mean(dxhat * xhat, axis=-1, keepdims=True))
        if has_res:
            dx = dx + refs[3][...]
        dx_ref[...] = dx

    row = pl.BlockSpec((tm, d), lambda i: (i, 0))
    vec = pl.BlockSpec((1, d), lambda i: (0, 0))
    in_specs = [row, vec, row] + ([row] if has_res else [])
    operands = [x, g, dh] + ([dres] if has_res else [])
    return pl.pallas_call(body, out_shape=(jax.ShapeDtypeStruct((s, d), F32), jax.ShapeDtypeStruct((1, d), F32)),
                          grid=(s // tm,), in_specs=in_specs, out_specs=(row, vec), name=name,
                          compiler_params=_params(("arbitrary",)))(*operands)


def _final_loss(x, g, target, name):
    s, d = x.shape
    tm = _blk(s, 512)

    def body(x_ref, g_ref, t_ref, loss_ref, dx_ref, dg_ref):
        xv = x_ref[...]
        gv = g_ref[...]
        r = lax.rsqrt(jnp.mean(xv * xv, axis=-1, keepdims=True) + EPS)
        xhat = xv * r
        err = xhat * gv - t_ref[...]
        lpart = 0.5 * jnp.sum(jnp.mean(err * err, axis=-1, keepdims=True), axis=0, keepdims=True)
        dy = err * (1.0 / d)
        gpart = jnp.sum(dy * xhat, axis=0, keepdims=True)

        @pl.when(pl.program_id(0) == 0)
        def _():
            loss_ref[...] = lpart
            dg_ref[...] = gpart

        @pl.when(pl.program_id(0) > 0)
        def _():
            loss_ref[...] += lpart
            dg_ref[...] += gpart

        dxhat = dy * gv
        dx_ref[...] = r * (dxhat - xhat * jnp.mean(dxhat * xhat, axis=-1, keepdims=True))

    row = pl.BlockSpec((tm, d), lambda i: (i, 0))
    vec = pl.BlockSpec((1, d), lambda i: (0, 0))
    one = pl.BlockSpec((1, 1), lambda i: (0, 0))
    return pl.pallas_call(body, out_shape=(jax.ShapeDtypeStruct((1, 1), F32), jax.ShapeDtypeStruct((s, d), F32),
                                           jax.ShapeDtypeStruct((1, d), F32)),
                          grid=(s // tm,), in_specs=[row, vec, row], out_specs=(one, row, vec), name=name,
                          compiler_params=_params(("arbitrary",)))(x, g, target)


def _erf_parts(x):
    cdf = 0.5 * (1.0 + lax.erf(x * (1.0 / math.sqrt(2.0))))
    return cdf


def _shift_down(ext, k):
    return pltpu.roll(ext, k, 0)


def _shift_up(ext, k):
    return pltpu.roll(ext, ext.shape[0] - k, 0)


def _mixer_forward_math(z, halo, row0, prm, tm):
    ln_g, ln_b, wtril, bsb, conv_w, wg, scale = prm
    za = z[:, 0:2 * A_WIDTH]
    gb = z[:, 1024:1536]
    gc = z[:, 1536:2048]
    xin = z[:, 2048:2560]
    zc = z[:, 2560:3072]
    cdf = _erf_parts(za)
    act = za * cdf
    u = act[:, :A_WIDTH]
    v = act[:, A_WIDTH:]
    mu = jnp.mean(v, axis=-1, keepdims=True)
    vc = v - mu
    rstd = lax.rsqrt(jnp.mean(vc * vc, axis=-1, keepdims=True) + EPS)
    vhat = vc * rstd
    vn = (vhat * ln_g + ln_b).astype(BF16)
    rows = []
    for c in range(tm // CHUNK):
        cols = []
        for g in range(A_GROUPS):
            blk = vn[c * CHUNK:(c + 1) * CHUNK, g * CHUNK:(g + 1) * CHUNK]
            cols.append(jnp.dot(wtril[g], blk, preferred_element_type=F32) + bsb[g])
        rows.append(jnp.concatenate(cols, axis=1))
    mixed = jnp.concatenate(rows, axis=0)
    ya = u * mixed
    y = gc * xin
    yext = jnp.concatenate([halo[:, 0:512] * halo[:, 512:1024], y], axis=0)
    y1 = _shift_down(yext, 1)[HALO:]
    y2 = _shift_down(yext, 2)[HALO:]
    conv = conv_w[0:1, :] * y2 + conv_w[1:2, :] * y1 + conv_w[2:3, :] * y
    yb = gb * conv
    t = row0 + lax.broadcasted_iota(jnp.int32, (tm, 1), 0)
    zext = jnp.concatenate([halo[:, 1024:1536], zc], axis=0)
    pooled, p = [], []
    for gi, win in enumerate(POOL_WINDOWS):
        sw = zext[:, gi * C_GROUP:(gi + 1) * C_GROUP]
        for step in range(gi + 1):
            sw = sw + _shift_down(sw, 2 ** step)
        cnt = jnp.minimum(t + 1, win).astype(F32)
        pg = sw[HALO:] / cnt - zc[:, gi * C_GROUP:(gi + 1) * C_GROUP]
        pooled.append(pg)
        p.append(jnp.dot(pg.astype(BF16), wg[gi], preferred_element_type=F32))
    p = jnp.concatenate(p, axis=1)
    yc = p * scale
    saved = dict(za=za, cdf=cdf, u=u, mixed=mixed, rstd=rstd, vhat=vhat, vn=vn, gb=gb, gc=gc, xin=xin, y=y, y1=y1, y2=y2,
                 conv=conv, pooled=pooled, p=p, t=t)
    return ya, yb, yc, saved


def _mixer_specs(tm, s):
    nb16 = tm // HALO
    main = pl.BlockSpec((tm, MIX_COLS), lambda i: (i, 0))
    prev = pl.BlockSpec((HALO, 1536), lambda i: (jnp.maximum(i * nb16 - 1, 0), 1))
    return main, prev


def _full(shape):
    n = len(shape)
    return pl.BlockSpec(shape, lambda i: (0,) * n)


def _mixer_fwd(proj, prm, name):
    s = proj.shape[0]
    tm = _blk(s, 256)
    main, prev = _mixer_specs(tm, s)

    def body(z_ref, h_ref, lng, lnb, wt, bsb, cw, wg, sc, o_ref):
        i = pl.program_id(0)
        halo = jnp.where(i > 0, h_ref[...].astype(F32), 0.0)
        prm_v = (lng[...], lnb[...], wt[...], bsb[...], cw[...], wg[...], sc[...])
        ya, yb, yc, _ = _mixer_forward_math(z_ref[...].astype(F32), halo, i * tm, prm_v, tm)
        o_ref[...] = jnp.concatenate([ya, yb, yc], axis=1).astype(BF16)

    return pl.pallas_call(body, out_shape=jax.ShapeDtypeStruct((s, 1536), BF16), grid=(s // tm,),
                          in_specs=[main, prev] + [_full(p.shape) for p in prm],
                          out_specs=pl.BlockSpec((tm, 1536), lambda i: (i, 0)), name=name,
                          compiler_params=_params(("parallel",)))(proj, proj, *prm)


def _mixer_bwd(proj, dypre, dproj_in, prm, wtril_t, name):
    s = proj.shape[0]
    tm = _blk(s, 256)
    nblk = s // tm
    nb16 = tm // HALO
    main, prev = _mixer_specs(tm, s)
    nxt_row = lambda i: jnp.minimum((i + 1) * nb16, s // HALO - 1)
    next_dy = pl.BlockSpec((HALO, 1536), lambda i: (nxt_row(i), 0))
    next_gb = pl.BlockSpec((HALO, 512), lambda i: (nxt_row(i), 2))
    ones8 = jnp.ones((8, CHUNK), F32)

    def body(z_ref, h_ref, dy_ref, ndy_ref, ngb_ref, _alias, lng, lnb, wt, bsb, cw, wg, sc, wtt, ones_ref,
             dz_ref, dlng_ref, dlnb_ref, dws_ref, dbs_ref, dcw_ref, dwg_ref, dsc_ref):
        i = pl.program_id(0)
        first = i == 0
        halo = jnp.where(i > 0, h_ref[...].astype(F32), 0.0)
        prm_v = (lng[...], lnb[...], wt[...], bsb[...], cw[...], wg[...], sc[...])
        _, _, _, sv = _mixer_forward_math(z_ref[...].astype(F32), halo, i * tm, prm_v, tm)
        dy = dy_ref[...].astype(F32)
        dya, dyb, dyc = dy[:, 0:512], dy[:, 512:1024], dy[:, 1024:1536]
        not_last = i < nblk - 1
        ndy = jnp.where(not_last, ndy_ref[...].astype(F32), 0.0)
        ngb = ngb_ref[...].astype(F32)

        def accumulate(ref, val):
            @pl.when(first)
            def _():
                ref[...] = val

            @pl.when(jnp.logical_not(first))
            def _():
                ref[...] += val

        du = dya * sv["mixed"]
        dmixed = dya * sv["u"]
        vn = sv["vn"]
        wttv = wtt[...]
        dvn_rows = []
        dws_parts = [None] * A_GROUPS
        dbs_parts = [None] * A_GROUPS
        for c in range(tm // CHUNK):
            cols = []
            for g in range(A_GROUPS):
                dm = dmixed[c * CHUNK:(c + 1) * CHUNK, g * CHUNK:(g + 1) * CHUNK]
                dmb = dm.astype(BF16)
                vb = vn[c * CHUNK:(c + 1) * CHUNK, g * CHUNK:(g + 1) * CHUNK]
                w_part = lax.dot_general(dmb, vb, (((1,), (1,)), ((), ())), preferred_element_type=F32)
                b_part = lax.dot_general(ones_ref[...], dm, (((1,), (1,)), ((), ())), preferred_element_type=F32,
                                         precision=lax.Precision.HIGHEST)[0:1, :]
                dws_parts[g] = w_part if dws_parts[g] is None else dws_parts[g] + w_part
                dbs_parts[g] = b_part if dbs_parts[g] is None else dbs_parts[g] + b_part
                cols.append(jnp.dot(wttv[g], dmb, preferred_element_type=F32))
            dvn_rows.append(jnp.concatenate(cols, axis=1))
        dvn = jnp.concatenate(dvn_rows, axis=0)
        tri = lax.broadcasted_iota(jnp.int32, (CHUNK, CHUNK), 0) >= lax.broadcasted_iota(jnp.int32, (CHUNK, CHUNK), 1)
        accumulate(dws_ref, jnp.stack([jnp.where(tri, w, 0.0) for w in dws_parts], axis=0))
        accumulate(dbs_ref, jnp.concatenate(dbs_parts, axis=0))
        vhat = sv["vhat"]
        accumulate(dlng_ref, jnp.sum(dvn * vhat, axis=0, keepdims=True))
        accumulate(dlnb_ref, jnp.sum(dvn, axis=0, keepdims=True))
        dvhat = dvn * lng[...]
        dv = sv["rstd"] * (dvhat - jnp.mean(dvhat, axis=-1, keepdims=True)
                           - vhat * jnp.mean(dvhat * vhat, axis=-1, keepdims=True))
        za = sv["za"]
        dgelu = sv["cdf"] + za * jnp.exp(-0.5 * za * za) * (1.0 / math.sqrt(2.0 * math.pi))
        dza = jnp.concatenate([du, dv], axis=1) * dgelu
        cwv = cw[...]
        dconv = dyb * sv["gb"]
        dgb = dyb * sv["conv"]
        dcext = jnp.concatenate([dconv, ndy[:, 512:1024] * ngb], axis=0)
        d1 = _shift_up(dcext, 1)[:tm]
        d2 = _shift_up(dcext, 2)[:tm]
        dyy = cwv[2:3, :] * dconv + cwv[1:2, :] * d1 + cwv[0:1, :] * d2
        dgc = dyy * sv["xin"]
        dxin = dyy * sv["gc"]
        accumulate(dcw_ref, jnp.concatenate([jnp.sum(dconv * sv["y2"], axis=0, keepdims=True),
                                             jnp.sum(dconv * sv["y1"], axis=0, keepdims=True),
                                             jnp.sum(dconv * sv["y"], axis=0, keepdims=True)], axis=0))
        scv = sc[...]
        accumulate(dsc_ref, jnp.sum(dyc * sv["p"], axis=0, keepdims=True))
        dp = dyc * scv
        ndp = ndy[:, 1024:1536] * scv
        wgv = wg[...]
        dzc_cols, dwg_parts = [], []
        for gi, win in enumerate(POOL_WINDOWS):
            sl = slice(gi * C_GROUP, (gi + 1) * C_GROUP)
            dpb = dp[:, sl].astype(BF16)
            dpool = lax.dot_general(dpb, wgv[gi], (((1,), (1,)), ((), ())), preferred_element_type=F32)
            ndpool = lax.dot_general(ndp[:, sl].astype(BF16), wgv[gi], (((1,), (1,)), ((), ())),
                                     preferred_element_type=F32)
            dwg_parts.append(lax.dot_general(sv["pooled"][gi].astype(BF16), dpb, (((0,), (0,)), ((), ())),
                                             preferred_element_type=F32))
            cnt = jnp.minimum(sv["t"] + 1, win).astype(F32)
            fw = jnp.concatenate([dpool / cnt, ndpool * (1.0 / win)], axis=0)
            for step in range(gi + 1):
                fw = fw + _shift_up(fw, 2 ** step)
            dzc_cols.append(fw[:tm] - dpool)
        accumulate(dwg_ref, jnp.stack(dwg_parts, axis=0))
        dzc = jnp.concatenate(dzc_cols, axis=1)
        dz_ref[...] = jnp.concatenate([dza, dgb, dgc, dxin, dzc], axis=1).astype(BF16)

    out_shape = (jax.ShapeDtypeStruct(dproj_in.shape, BF16), jax.ShapeDtypeStruct((1, A_WIDTH), F32),
                 jax.ShapeDtypeStruct((1, A_WIDTH), F32), jax.ShapeDtypeStruct((A_GROUPS, CHUNK, CHUNK), F32),
                 jax.ShapeDtypeStruct((A_GROUPS, CHUNK), F32), jax.ShapeDtypeStruct((3, B_WIDTH), F32),
                 jax.ShapeDtypeStruct((4, C_GROUP, C_GROUP), F32), jax.ShapeDtypeStruct((1, C_WIDTH), F32))
    out_specs = (main,) + tuple(_full(o.shape) for o in out_shape[1:])
    in_specs = [main, prev, pl.BlockSpec((tm, 1536), lambda i: (i, 0)), next_dy, next_gb, ANY] + \
               [_full(p.shape) for p in prm] + [_full(wtril_t.shape), _full(ones8.shape)]
    return pl.pallas_call(body, out_shape=out_shape, grid=(nblk,), in_specs=in_specs, out_specs=out_specs,
                          input_output_aliases={5: 0}, name=name,
                          compiler_params=_params(("arbitrary",)))(proj, proj, dypre, dypre, proj, dproj_in, *prm,
                                                                  wtril_t, ones8)


def _sigmoid(x):
    return 1.0 / (1.0 + jnp.exp(-x))


def _merge_fwd(ypre, proj, wa, wb, wc, bgate, name):
    s = ypre.shape[0]
    tm = _blk(s, 512)

    def body(y_ref, zg_ref, wa_ref, wb_ref, wc_ref, bg_ref, o_ref):
        yv = y_ref[...]
        acc = None
        for i, w_ref in enumerate((wa_ref, wb_ref, wc_ref)):
            br = jnp.dot(yv[:, i * 512:(i + 1) * 512], w_ref[...], preferred_element_type=F32)
            gate = _sigmoid(zg_ref[:, i * D_MODEL:(i + 1) * D_MODEL].astype(F32) + bg_ref[i:i + 1, :])
            acc = gate * br if acc is None else acc + gate * br
        o_ref[...] = acc.astype(BF16)

    wspec = _full(wa.shape)
    return pl.pallas_call(body, out_shape=jax.ShapeDtypeStruct((s, D_MODEL), BF16), grid=(s // tm,),
                          in_specs=[pl.BlockSpec((tm, 1536), lambda i: (i, 0)),
                                    pl.BlockSpec((tm, 3 * D_MODEL), lambda i: (i, 1)), wspec, wspec, wspec,
                                    _full(bgate.shape)],
                          out_specs=pl.BlockSpec((tm, D_MODEL), lambda i: (i, 0)), name=name,
                          compiler_params=_params(("parallel",)))(ypre, proj, wa, wb, wc, bgate)


def _merge_bwd(ypre, proj, dmerged, wa, wb, wc, bgate, name):
    s = ypre.shape[0]
    tm = _blk(s, 512)

    def body(y_ref, zg_ref, dm_ref, wa_ref, wb_ref, wc_ref, bg_ref, dyp_ref, dbr_ref, dzg_ref, dbg_ref):
        yv = y_ref[...]
        dm = dm_ref[...].astype(F32)
        dyp, dbr, dzg, dbg = [], [], [], []
        for i, w_ref in enumerate((wa_ref, wb_ref, wc_ref)):
            wv = w_ref[...]
            br = jnp.dot(yv[:, i * 512:(i + 1) * 512], wv, preferred_element_type=F32)
            gate = _sigmoid(zg_ref[:, i * D_MODEL:(i + 1) * D_MODEL].astype(F32) + bg_ref[i:i + 1, :])
            dbi = (dm * gate).astype(BF16)
            dzi = dm * br * gate * (1.0 - gate)
            dbr.append(dbi)
            dzg.append(dzi.astype(BF16))
            dbg.append(jnp.sum(dzi, axis=0, keepdims=True))
            dyp.append(lax.dot_general(dbi, wv, (((1,), (1,)), ((), ())), preferred_element_type=F32).astype(BF16))
        dyp_ref[...] = jnp.concatenate(dyp, axis=1)
        dbr_ref[...] = jnp.concatenate(dbr, axis=1)
        dzg_ref[...] = jnp.concatenate(dzg, axis=1)
        part = jnp.concatenate(dbg, axis=0)

        @pl.when(pl.program_id(0) == 0)
        def _():
            dbg_ref[...] = part

        @pl.when(pl.program_id(0) > 0)
        def _():
            dbg_ref[...] += part

    wspec = _full(wa.shape)
    out_shape = (jax.ShapeDtypeStruct((s, 1536), BF16), jax.ShapeDtypeStruct((s, 3 * D_MODEL), BF16),
                 jax.ShapeDtypeStruct((s, IN_COLS), BF16), jax.ShapeDtypeStruct((3, D_MODEL), F32))
    return pl.pallas_call(body, out_shape=out_shape, grid=(s // tm,),
                          in_specs=[pl.BlockSpec((tm, 1536), lambda i: (i, 0)),
                                    pl.BlockSpec((tm, 3 * D_MODEL), lambda i: (i, 1)),
                                    pl.BlockSpec((tm, D_MODEL), lambda i: (i, 0)), wspec, wspec, wspec,
                                    _full(bgate.shape)],
                          out_specs=(pl.BlockSpec((tm, 1536), lambda i: (i, 0)),
                                     pl.BlockSpec((tm, 3 * D_MODEL), lambda i: (i, 0)),
                                     pl.BlockSpec((tm, 3 * D_MODEL), lambda i: (i, 1)), _full((3, D_MODEL))),
                          name=name, compiler_params=_params(("arbitrary",)))(ypre, proj, dmerged, wa, wb, wc, bgate)


def _softmax_rows(q, k):
    sc = lax.dot_general(q, k, (((1,), (1,)), ((), ())), preferred_element_type=F32) * (HEAD_DIM ** -0.5)
    e = jnp.exp(sc - jnp.max(sc, axis=-1, keepdims=True))
    return e / jnp.sum(e, axis=-1, keepdims=True)


def _attn_fwd(q, kv, name):
    s = q.shape[0]
    tm = _blk(s, 512)

    def body(q_ref, kv_ref, o_ref):
        outs = []
        for h in range(XATTN_HEADS):
            sl = slice(h * HEAD_DIM, (h + 1) * HEAD_DIM)
            p = _softmax_rows(q_ref[:, sl], kv_ref[:, sl])
            outs.append(jnp.dot(p.astype(BF16), kv_ref[:, D_MODEL + h * HEAD_DIM:D_MODEL + (h + 1) * HEAD_DIM],
                                preferred_element_type=F32))
        o_ref[...] = jnp.concatenate(outs, axis=1).astype(BF16)

    return pl.pallas_call(body, out_shape=jax.ShapeDtypeStruct((s, D_MODEL), BF16), grid=(s // tm,),
                          in_specs=[pl.BlockSpec((tm, D_MODEL), lambda i: (i, 0)), _full(kv.shape)],
                          out_specs=pl.BlockSpec((tm, D_MODEL), lambda i: (i, 0)), name=name,
                          compiler_params=_params(("parallel",)))(q, kv)


def _attn_bwd(q, kv, do, name):
    s = q.shape[0]
    tm = _blk(s, 512)

    def body(q_ref, kv_ref, do_ref, dq_ref, dkv_ref):
        dqs, dks, dvs = [], [], []
        for h in range(XATTN_HEADS):
            sl = slice(h * HEAD_DIM, (h + 1) * HEAD_DIM)
            vsl = slice(D_MODEL + h * HEAD_DIM, D_MODEL + (h + 1) * HEAD_DIM)
            qh, kh, vh, doh = q_ref[:, sl], kv_ref[:, sl], kv_ref[:, vsl], do_ref[:, sl]
            p = _softmax_rows(qh, kh)
            pb = p.astype(BF16)
            dvs.append(lax.dot_general(pb, doh, (((0,), (0,)), ((), ())), preferred_element_type=F32))
            dp = lax.dot_general(doh, vh, (((1,), (1,)), ((), ())), preferred_element_type=F32)
            ds = p * (dp - jnp.sum(dp * p, axis=-1, keepdims=True)) * (HEAD_DIM ** -0.5)
            dsb = ds.astype(BF16)
            dqs.append(jnp.dot(dsb, kh, preferred_element_type=F32))
            dks.append(lax.dot_general(dsb, qh, (((0,), (0,)), ((), ())), preferred_element_type=F32))
        dq_ref[...] = jnp.concatenate(dqs, axis=1).astype(BF16)
        part = jnp.concatenate(dks + dvs, axis=1)

        @pl.when(pl.program_id(0) == 0)
        def _():
            dkv_ref[...] = part

        @pl.when(pl.program_id(0) > 0)
        def _():
            dkv_ref[...] += part

    row = pl.BlockSpec((tm, D_MODEL), lambda i: (i, 0))
    return pl.pallas_call(body, out_shape=(jax.ShapeDtypeStruct((s, D_MODEL), BF16),
                                           jax.ShapeDtypeStruct(kv.shape, F32)),
                          grid=(s // tm,), in_specs=[row, _full(kv.shape), row], out_specs=(row, _full(kv.shape)),
                          name=name, compiler_params=_params(("arbitrary",)))(q, kv, do)


def _swiglu_fwd(gu, name):
    s = gu.shape[0]
    tm = _blk(s, 512)
    half = pl.BlockSpec((tm, FFN_HIDDEN), lambda i: (i, 0))
    up = pl.BlockSpec((tm, FFN_HIDDEN), lambda i: (i, 1))

    def body(g_ref, u_ref, o_ref):
        g = g_ref[...].astype(F32)
        o_ref[...] = (g * _sigmoid(g) * u_ref[...].astype(F32)).astype(BF16)

    return pl.pallas_call(body, out_shape=jax.ShapeDtypeStruct((s, FFN_HIDDEN), BF16), grid=(s // tm,),
                          in_specs=[half, up], out_specs=half, name=name,
                          compiler_params=_params(("parallel",)))(gu, gu)


def _swiglu_bwd(gu, dact, name):
    s = gu.shape[0]
    tm = _blk(s, 512)
    half = pl.BlockSpec((tm, FFN_HIDDEN), lambda i: (i, 0))
    up = pl.BlockSpec((tm, FFN_HIDDEN), lambda i: (i, 1))

    def body(g_ref, u_ref, d_ref, o_ref):
        g = g_ref[...].astype(F32)
        u = u_ref[...].astype(F32)
        d = d_ref[...].astype(F32)
        sg = _sigmoid(g)
        dg = d * u * sg * (1.0 + g * (1.0 - sg))
        du = d * g * sg
        o_ref[...] = jnp.concatenate([dg, du], axis=1).astype(BF16)

    return pl.pallas_call(body, out_shape=jax.ShapeDtypeStruct((s, 2 * FFN_HIDDEN), BF16), grid=(s // tm,),
                          in_specs=[half, up, half], out_specs=pl.BlockSpec((tm, 2 * FFN_HIDDEN), lambda i: (i, 0)),
                          name=name, compiler_params=_params(("parallel",)))(gu, gu, dact)


def _rows_block(rows, cols):
    target = (512 * 1024) // cols
    fits = [cand for cand in range(8, rows + 1, 8) if rows % cand == 0 and cand <= target]
    return fits[-1] if fits else rows


def _elementwise(fn, ins, out_dtypes, name, place=None, flat=()):
    lead, rows, cols = [a for i, a in enumerate(ins) if i not in flat][0].shape
    tr = _rows_block(rows, cols)
    spec = pl.BlockSpec((None, tr, cols), lambda l, i, *_: (l, i, 0))
    flat_spec = pl.BlockSpec((tr, cols), lambda l, i, *_: (i, 0))
    n_in = len(ins)
    in_specs = [flat_spec if i in flat else spec for i in range(n_in)]
    out_specs = tuple([spec] * len(out_dtypes))
    out_shape = tuple(jax.ShapeDtypeStruct((lead, rows, cols), dt) for dt in out_dtypes)
    sem = _params(("parallel", "parallel"))

    def write(refs, outs):
        for o_ref, o in zip(refs, outs):
            o_ref[...] = o.astype(o_ref.dtype)

    if place is None:
        def body(*refs):
            write(refs[n_in:], fn(*[r[...] for r in refs[:n_in]]))

        return pl.pallas_call(body, out_shape=out_shape, grid=(lead, rows // tr), in_specs=in_specs,
                              out_specs=out_specs, name=name, compiler_params=sem)(*ins)

    def body_placed(p_ref, *refs):
        write(refs[n_in:], fn(p_ref[0], pl.program_id(0), *[r[...] for r in refs[:n_in]]))

    grid_spec = pltpu.PrefetchScalarGridSpec(num_scalar_prefetch=1, grid=(lead, rows // tr), in_specs=in_specs,
                                             out_specs=out_specs)
    return pl.pallas_call(body_placed, out_shape=out_shape, grid_spec=grid_spec, name=name,
                          compiler_params=sem)(place, *ins)


def _cast_to_slot(w, chip, dtype, name):
    lead, rows, cols = w.shape
    tr = _rows_block(rows, cols)

    def body(chip_ref, w_ref, o_ref):
        o_ref[...] = w_ref[...].astype(dtype)

    grid_spec = pltpu.PrefetchScalarGridSpec(
        num_scalar_prefetch=1, grid=(lead, rows // tr),
        in_specs=[pl.BlockSpec((None, tr, cols), lambda l, i, chip_ref: (l, i, 0))],
        out_specs=pl.BlockSpec((None, None, tr, cols), lambda l, i, chip_ref: (l, chip_ref[0], i, 0)))
    return pl.pallas_call(body, out_shape=jax.ShapeDtypeStruct((lead, N_CHIPS, rows, cols), dtype), grid_spec=grid_spec,
                          name=name, compiler_params=_params(("parallel", "parallel")))(chip, w)


def _chip_sum(pair, rx, chip, name):
    _, rows, cols = pair.shape
    tr = _rows_block(rows, cols)

    def body(chip_ref, p_ref, rx_ref, o_ref):
        acc = p_ref[...].astype(F32)
        for j in range(3):
            acc = acc + rx_ref[j].astype(F32)
        o_ref[...] = acc

    grid_spec = pltpu.PrefetchScalarGridSpec(
        num_scalar_prefetch=1, grid=(rows // tr,),
        in_specs=[pl.BlockSpec((None, tr, cols), lambda i, chip_ref: (chip_ref[0], i, 0)),
                  pl.BlockSpec((3, tr, cols), lambda i, chip_ref: (0, i, 0))],
        out_specs=pl.BlockSpec((tr, cols), lambda i, chip_ref: (i, 0)))
    return pl.pallas_call(body, out_shape=jax.ShapeDtypeStruct((rows, cols), F32), grid_spec=grid_spec, name=name,
                          compiler_params=_params(("parallel",)))(chip, pair, rx)


def _chip_sum_ordered(own, rx, chip, name):
    rows, cols = own.shape

    def body(chip_ref, own_ref, rx_ref, o_ref):
        me = chip_ref[0]
        acc = None
        for k in range(N_CHIPS):
            rel = jnp.bitwise_xor(me, k)
            term = jnp.where(rel == 0, own_ref[...],
                             jnp.where(rel == 2, rx_ref[0], jnp.where(rel == 1, rx_ref[1], rx_ref[2])))
            acc = term if acc is None else acc + term
        o_ref[...] = acc

    grid_spec = pltpu.PrefetchScalarGridSpec(
        num_scalar_prefetch=1, grid=(1,),
        in_specs=[pl.BlockSpec((rows, cols), lambda i, chip_ref: (0, 0)),
                  pl.BlockSpec((3, rows, cols), lambda i, chip_ref: (0, 0, 0))],
        out_specs=pl.BlockSpec((rows, cols), lambda i, chip_ref: (0, 0)))
    return pl.pallas_call(body, out_shape=jax.ShapeDtypeStruct((rows, cols), F32), grid_spec=grid_spec, name=name,
                          compiler_params=_params(("arbitrary",)))(chip, own, rx)


def _adamw_math(w, g, m, v):
    m = ADAM_B1 * m + (1.0 - ADAM_B1) * g
    v = ADAM_B2 * v + (1.0 - ADAM_B2) * (g * g)
    m_hat = m / (1.0 - ADAM_B1 ** ADAM_STEP)
    v_hat = v / (1.0 - ADAM_B2 ** ADAM_STEP)
    delta = -ADAM_LR * (m_hat / (jnp.sqrt(v_hat) + ADAM_EPS) + ADAM_WD * w)
    return delta, m, v


def _place():
    x, y, c = lax.axis_index("x"), lax.axis_index("y"), lax.axis_index("c")
    chips = [(1 - x, y), (x, 1 - y), (1 - x, 1 - y)]
    return x, y, c, 2 * x + y, chips


def _run_remote(copies):
    for send, _ in copies:
        send.start()
    for send, recv in copies:
        send.wait_send()
        recv.wait_recv()


def _chip_gather(fulls, name):
    n = len(fulls)

    def body(*refs):
        outs = refs[n:2 * n]
        ssem, rsem = refs[2 * n:]
        x, y, c, me, chips = _place()
        copies = []
        for w in range(n):
            mine = outs[w].at[c, me]
            for j, (px, py) in enumerate(chips):
                sems = (ssem.at[3 * w + j], rsem.at[3 * w + j])
                peer = dict(device_id=(px, py, c), device_id_type=MESH)
                send = pltpu.make_async_remote_copy(mine, mine, *sems, **peer)
                recv = pltpu.make_async_remote_copy(mine, outs[w].at[c, 2 * px + py], *sems, **peer)
                copies.append((send, recv))
        _run_remote(copies)

    out_shape = tuple(jax.ShapeDtypeStruct(f.shape, f.dtype) for f in fulls)
    return pl.pallas_call(body, out_shape=out_shape, in_specs=[ANY] * n, out_specs=tuple([ANY] * n),
                          input_output_aliases={w: w for w in range(n)},
                          scratch_shapes=[pltpu.SemaphoreType.DMA((3 * n,)), pltpu.SemaphoreType.DMA((3 * n,))],
                          name=name)(*fulls)


def _push_layer(full, core, name):
    _, _, rows, cols = full.shape
    tr = _rows_block(rows, cols)
    nrow = rows // tr

    def body(core_ref, blk_ref, out_ref, ssem, rsem):
        x, y, c = lax.axis_index("x"), lax.axis_index("y"), lax.axis_index("c")
        k, i = pl.program_id(0), pl.program_id(1)
        peer = dict(device_id=(x, y, 1 - c), device_id_type=MESH)
        dst = out_ref.at[pl.ds(c, 1), pl.ds(k, 1), pl.ds(pl.multiple_of(i * tr, 8), tr)]
        cp = pltpu.make_async_remote_copy(blk_ref, dst, ssem, rsem, **peer)
        cp.start()
        cp.wait_send()

        @pl.when(jnp.logical_and(k == N_CHIPS - 1, i == nrow - 1))
        def _():
            landed = out_ref.at[1 - c]
            pltpu.make_async_remote_copy(landed, landed, ssem, rsem, **peer).wait_recv()

    grid_spec = pltpu.PrefetchScalarGridSpec(
        num_scalar_prefetch=1, grid=(N_CHIPS, nrow),
        in_specs=[pl.BlockSpec((1, 1, tr, cols), lambda k, i, core_ref: (core_ref[0], k, i, 0))],
        out_specs=ANY, scratch_shapes=[pltpu.SemaphoreType.DMA, pltpu.SemaphoreType.DMA])
    return pl.pallas_call(body, out_shape=jax.ShapeDtypeStruct(full.shape, full.dtype), grid_spec=grid_spec,
                          input_output_aliases={1: 0}, name=name,
                          compiler_params=_params(("arbitrary", "arbitrary")))(core, full)


def _push_sibling(srcs, name):
    lead, rows, cols = srcs[0].shape
    tr = _rows_block(rows, cols)
    nrow = rows // tr
    nsrc = len(srcs)

    def body(*refs):
        blks, rx_ref = refs[:nsrc], refs[nsrc]
        ssem, rsem = refs[nsrc + 1:]
        x, y, c = lax.axis_index("x"), lax.axis_index("y"), lax.axis_index("c")
        k, i = pl.program_id(0), pl.program_id(1)
        peer = dict(device_id=(x, y, 1 - c), device_id_type=MESH)
        dst = rx_ref.at[pl.ds(k, 1), pl.ds(pl.multiple_of(i * tr, 8), tr)]

        def send(blk):
            cp = pltpu.make_async_remote_copy(blk, dst, ssem, rsem, **peer)
            cp.start()
            cp.wait_send()

        if nsrc == 1:
            send(blks[0])
        else:
            @pl.when(c == 0)
            def _():
                send(blks[1])

            @pl.when(c == 1)
            def _():
                send(blks[0])

        @pl.when(jnp.logical_and(k == lead - 1, i == nrow - 1))
        def _():
            pltpu.make_async_remote_copy(rx_ref, rx_ref, ssem, rsem, **peer).wait_recv()

    spec = pl.BlockSpec((1, tr, cols), lambda k, i: (k, i, 0))
    return pl.pallas_call(body, out_shape=jax.ShapeDtypeStruct(srcs[0].shape, srcs[0].dtype), grid=(lead, nrow),
                          in_specs=[spec] * nsrc, out_specs=ANY,
                          scratch_shapes=[pltpu.SemaphoreType.DMA, pltpu.SemaphoreType.DMA], name=name,
                          compiler_params=_params(("arbitrary", "arbitrary")))(*srcs)


def _chip_scatter(pairs, small, name):
    n = len(pairs)

    def body(*refs):
        ins, small_ref = refs[:n], refs[n]
        outs, small_out = refs[n + 1:2 * n + 1], refs[2 * n + 1]
        ssem, rsem = refs[2 * n + 2:]
        x, y, c, me, chips = _place()
        copies = []
        for w in range(n + 1):
            for j, (px, py) in enumerate(chips):
                src = ins[w].at[2 * px + py] if w < n else small_ref
                dst = (outs[w] if w < n else small_out).at[j]
                cp = pltpu.make_async_remote_copy(src, dst, ssem.at[3 * w + j], rsem.at[3 * w + j],
                                                  device_id=(px, py, c), device_id_type=MESH)
                copies.append((cp, cp))
        _run_remote(copies)

    out_shape = tuple(jax.ShapeDtypeStruct((3,) + it.shape[1:], it.dtype) for it in pairs) + \
                (jax.ShapeDtypeStruct((3,) + small.shape, small.dtype),)
    return pl.pallas_call(body, out_shape=out_shape, in_specs=[ANY] * (n + 1), out_specs=tuple([ANY] * (n + 1)),
                          scratch_shapes=[pltpu.SemaphoreType.DMA((3 * n + 3,)),
                                          pltpu.SemaphoreType.DMA((3 * n + 3,))],
                          name=name)(*pairs, small)


def _pack(arrays):
    flat = jnp.concatenate([a.reshape(-1).astype(F32) for a in arrays])
    pad = (-flat.shape[0]) % (8 * LANES)
    return jnp.pad(flat, (0, pad)).reshape(-1, LANES)


def _unpack(packed, shapes):
    flat = packed.reshape(-1)
    out, off = [], 0
    for shp in shapes:
        size = math.prod(shp)
        out.append(flat[off:off + size].reshape(shp))
        off += size
    return out


def kernel(x, mem, g_mix, w_in, b_gate, a_ln_g, a_ln_b, a_ws, a_bs, b_conv, c_wg, c_scale, w_branch_a, w_branch_b, w_branch_c, w_o, g_xattn, g_mem, w_xq, w_xkv, w_xo, g_ffn, w_ffn_in, w_ffn_out, g_final, loss_target, m_g_mix, m_w_in, m_b_gate, m_a_ln_g, m_a_ln_b, m_a_ws, m_a_bs, m_b_conv, m_c_wg, m_c_scale, m_w_branch_a, m_w_branch_b, m_w_branch_c, m_w_o, m_g_xattn, m_g_mem, m_w_xq, m_w_xkv, m_w_xo, m_g_ffn, m_w_ffn_in, m_w_ffn_out, m_g_final, v_g_mix, v_w_in, v_b_gate, v_a_ln_g, v_a_ln_b, v_a_ws, v_a_bs, v_b_conv, v_c_wg, v_c_scale, v_w_branch_a, v_w_branch_b, v_w_branch_c, v_w_o, v_g_xattn, v_g_mem, v_w_xq, v_w_xkv, v_w_xo, v_g_ffn, v_w_ffn_in, v_w_ffn_out, v_g_final):
    args = locals()
    wts = {n: args[n] for n in WEIGHTS}
    mom = {n: args["m_" + n] for n in WEIGHTS}
    var = {n: args["v_" + n] for n in WEIGHTS}
    xs = x[0]
    mems = mem[0]
    tgt = loss_target[0]
    chip = 2 * lax.axis_index("x") + lax.axis_index("y")

    bias_pack = jnp.zeros((DEPTH, 8, 384), F32)
    bias_pack = bias_pack.at[:, 0:3, 0:256].set(b_gate).at[:, 0:3, 256:384].set(b_conv)
    chip_arr = jnp.reshape(chip, (1,)).astype(jnp.int32)
    core_arr = jnp.reshape(lax.axis_index("c"), (1,)).astype(jnp.int32)
    slots = [_cast_to_slot(wts[n], chip_arr, BF16, name=f"cast_{n}") for n in BIG]
    slots.append(_cast_to_slot(bias_pack, chip_arr, F32, name="cast_biases"))
    gathered = _chip_gather(slots, name="gather_chips")
    full = [_push_layer(g, core_arr, name=f"gather_cores_{n}")
            for g, n in zip(gathered, BIG + ("biases",))]
    fw = {}
    for n, g in zip(BIG, full[:-1]):
        if n in COL_SHARDED:
            fw[n] = jnp.transpose(g, (0, 2, 1, 3)).reshape(DEPTH, g.shape[2], N_CHIPS * g.shape[3])
        else:
            fw[n] = g.reshape(DEPTH, N_CHIPS * g.shape[2], g.shape[3])
    biases = full[-1]
    b_gate_full = jnp.transpose(biases[:, :, 0:3, 0:256], (0, 2, 1, 3)).reshape(DEPTH, 3, D_MODEL)
    b_conv_full = jnp.transpose(biases[:, :, 0:3, 256:384], (0, 2, 1, 3)).reshape(DEPTH, 3, B_WIDTH)

    tril = jnp.tril(jnp.ones((CHUNK, CHUNK), bool))

    def mixer_params(l):
        wtril = jnp.where(tril[None], a_ws[l], 0.0)
        prm = (a_ln_g[l][None], a_ln_b[l][None], wtril.astype(BF16),
               jnp.broadcast_to(a_bs[l][:, :, None], (A_GROUPS, CHUNK, CHUNK)), b_conv_full[l],
               c_wg[l].astype(BF16), c_scale[l][None])
        return prm, jnp.swapaxes(wtril, 1, 2).astype(BF16)

    saved = []
    xc = xs
    for l in range(DEPTH):
        sv = {"x0": xc}
        prm, _ = mixer_params(l)
        sv["h"] = _rms_fwd(xc, g_mix[l][None], name=f"l{l}_mix_norm")
        sv["proj"] = _mm(sv["h"], fw["w_in"], b_lead=l, mode="nn", out_dtype=BF16, name=f"l{l}_proj", tm=1024,
                         tn=512, tk=1024)
        sv["ypre"] = _mixer_fwd(sv["proj"], prm, name=f"l{l}_mixers")
        sv["merged"] = _merge_fwd(sv["ypre"], sv["proj"], fw["w_branch_a"][l], fw["w_branch_b"][l],
                                  fw["w_branch_c"][l], b_gate_full[l], name=f"l{l}_merge")
        xc = _mm(sv["merged"], fw["w_o"], b_lead=l, mode="nn", out_dtype=F32, name=f"l{l}_mix_out", tm=1024, tn=512,
                 tk=1024, res=xc)
        sv["x1"] = xc
        sv["hq"] = _rms_fwd(xc, g_xattn[l][None], name=f"l{l}_xattn_norm")
        sv["m"] = _rms_fwd(mems, g_mem[l][None], name=f"l{l}_mem_norm")
        sv["q"] = _mm(sv["hq"], fw["w_xq"], b_lead=l, mode="nn", out_dtype=BF16, name=f"l{l}_q", tm=1024, tn=512,
                      tk=1024)
        sv["kv"] = _mm(sv["m"], fw["w_xkv"], b_lead=l, mode="nn", out_dtype=BF16, name=f"l{l}_kv", tm=256, tn=512,
                       tk=1024)
        sv["o"] = _attn_fwd(sv["q"], sv["kv"], name=f"l{l}_attn")
        xc = _mm(sv["o"], fw["w_xo"], b_lead=l, mode="nn", out_dtype=F32, name=f"l{l}_xattn_out", tm=1024, tn=512,
                 tk=1024, res=xc)
        sv["x2"] = xc
        sv["h2"] = _rms_fwd(xc, g_ffn[l][None], name=f"l{l}_ffn_norm")
        sv["gu"] = _mm(sv["h2"], fw["w_ffn_in"], b_lead=l, mode="nn", out_dtype=BF16, name=f"l{l}_ffn_in", tm=1024,
                       tn=512, tk=1024)
        sv["act"] = _swiglu_fwd(sv["gu"], name=f"l{l}_swiglu")
        xc = _mm(sv["act"], fw["w_ffn_out"], b_lead=l, mode="nn", out_dtype=F32, name=f"l{l}_ffn_out", tm=1024,
                 tn=512, tk=FFN_HIDDEN, res=xc)
        saved.append(sv)

    loss_part, dx, dg_final = _final_loss(xc, g_final[None], tgt, name="final_loss")
    loss = lax.psum(loss_part[0, 0], ("x", "y", "c"))

    big_grads = [None] * DEPTH
    small_grads = [None] * DEPTH
    for l in reversed(range(DEPTH)):
        sv = saved[l]
        prm, wtril_t = mixer_params(l)
        gb = {}
        sg = {}
        dact = _mm(dx, fw["w_ffn_out"], b_lead=l, mode="nt", out_dtype=BF16, name=f"l{l}_d_act", tm=1024, tn=1408,
                   tk=1024)
        gb["w_ffn_out"] = _mm(sv["act"], dx, mode="tn", out_dtype=BF16, name=f"l{l}_dw_ffn_out", tm=1408, tn=1024,
                              tk=1024).reshape(N_CHIPS, FFN_HIDDEN // N_CHIPS, D_MODEL)
        dgu = _swiglu_bwd(sv["gu"], dact, name=f"l{l}_d_swiglu")
        dh2 = _mm(dgu, fw["w_ffn_in"], b_lead=l, mode="nt", out_dtype=BF16, name=f"l{l}_d_h2", tm=1024, tn=1024,
                  tk=1408)
        gb["w_ffn_in"] = _mm(sv["h2"], dgu, mode="tn", out_dtype=BF16, name=f"l{l}_dw_ffn_in", tm=1024, tn=1408,
                             tk=1024, out_shards=N_CHIPS)
        dx, sg["g_ffn"] = _rms_bwd(sv["x2"], g_ffn[l][None], dh2, dx, name=f"l{l}_d_ffn_norm")
        do = _mm(dx, fw["w_xo"], b_lead=l, mode="nt", out_dtype=BF16, name=f"l{l}_d_o", tm=1024, tn=1024, tk=1024)
        gb["w_xo"] = _mm(sv["o"], dx, mode="tn", out_dtype=BF16, name=f"l{l}_dw_xo", tm=1024, tn=1024,
                         tk=1024).reshape(N_CHIPS, D_MODEL // N_CHIPS, D_MODEL)
        dq, dkv = _attn_bwd(sv["q"], sv["kv"], do, name=f"l{l}_d_attn")
        dhq = _mm(dq, fw["w_xq"], b_lead=l, mode="nt", out_dtype=BF16, name=f"l{l}_d_hq", tm=1024, tn=1024, tk=1024)
        gb["w_xq"] = _mm(sv["hq"], dq, mode="tn", out_dtype=BF16, name=f"l{l}_dw_xq", tm=1024, tn=1024,
                         tk=1024).reshape(N_CHIPS, D_MODEL // N_CHIPS, D_MODEL)
        dm = _mm(dkv, fw["w_xkv"], b_lead=l, mode="nt", out_dtype=BF16, name=f"l{l}_d_m", tm=256, tn=1024, tk=1024)
        gb["w_xkv"] = _mm(sv["m"], dkv, mode="tn", out_dtype=BF16, name=f"l{l}_dw_xkv", tm=1024, tn=512, tk=256,
                          out_shards=N_CHIPS)
        _, sg["g_mem"] = _rms_bwd(mems, g_mem[l][None], dm, None, name=f"l{l}_d_mem_norm")
        dx, sg["g_xattn"] = _rms_bwd(sv["x1"], g_xattn[l][None], dhq, dx, name=f"l{l}_d_xattn_norm")
        dmerged = _mm(dx, fw["w_o"], b_lead=l, mode="nt", out_dtype=BF16, name=f"l{l}_d_merged", tm=1024, tn=1024,
                      tk=1024)
        gb["w_o"] = _mm(sv["merged"], dx, mode="tn", out_dtype=BF16, name=f"l{l}_dw_o", tm=1024, tn=1024,
                        tk=1024).reshape(N_CHIPS, D_MODEL // N_CHIPS, D_MODEL)
        dypre, dbranch, dproj, sg["b_gate"] = _merge_bwd(sv["ypre"], sv["proj"], dmerged, fw["w_branch_a"][l],
                                                         fw["w_branch_b"][l], fw["w_branch_c"][l], b_gate_full[l],
                                                         name=f"l{l}_d_merge")
        for i, n in enumerate(("w_branch_a", "w_branch_b", "w_branch_c")):
            gb[n] = _mm(sv["ypre"], dbranch, mode="tn", out_dtype=BF16, name=f"l{l}_d{n}", tm=512, tn=256, tk=1024,
                        a_col0=512 * i, a_width=512, b_col0=D_MODEL * i, b_width=D_MODEL, out_shards=N_CHIPS)
        (dproj, sg["a_ln_g"], sg["a_ln_b"], sg["a_ws"], sg["a_bs"], sg["b_conv"], sg["c_wg"],
         sg["c_scale"]) = _mixer_bwd(sv["proj"], dypre, dproj, prm, wtril_t, name=f"l{l}_d_mixers")
        dh = _mm(dproj, fw["w_in"], b_lead=l, mode="nt", out_dtype=BF16, name=f"l{l}_d_h", tm=1024, tn=1024, tk=1536)
        gb["w_in"] = _mm(sv["h"], dproj, mode="tn", out_dtype=BF16, name=f"l{l}_dw_in", tm=1024, tn=1536, tk=1024,
                         out_shards=N_CHIPS)
        dx, sg["g_mix"] = _rms_bwd(sv["x0"], g_mix[l][None], dh, dx, name=f"l{l}_d_mix_norm")
        big_grads[l] = [gb[n] for n in BIG]
        small_grads[l] = sg
    grad_x = dx[None]

    small_names = [n for n in WEIGHTS if n not in BIG]
    small_full_shapes = {n: ((DEPTH, 3, D_MODEL) if n == "b_gate" else (DEPTH, 3, B_WIDTH) if n == "b_conv"
                             else wts[n].shape) for n in small_names}
    small_local = []
    for n in small_names:
        if n == "g_final":
            small_local.append(dg_final)
        else:
            small_local.append(jnp.stack([small_grads[l][n].reshape(small_full_shapes[n][1:]) for l in range(DEPTH)]))
    small_pack = _pack(small_local)

    nb = len(BIG)

    def pair_sum(core, _, g0, g1, r):
        return (jnp.where(core == 0, g0, g1).astype(F32) + r.astype(F32),)

    pair = []
    for i, n in enumerate(BIG):
        rx = _push_sibling([big_grads[0][i], big_grads[1][i]], name=f"reduce_cores_{n}")
        pair.append(_elementwise(pair_sum, [big_grads[0][i], big_grads[1][i], rx], [BF16], name=f"pair_sum_{n}",
                                 place=core_arr)[0])
    small_rx = _push_sibling([small_pack[None]], name="reduce_cores_small")
    small_pair = _elementwise(lambda a, b: (a + b,), [small_pack[None], small_rx], [F32], name="pair_sum_small")[0][0]
    scattered = _chip_scatter(pair, small_pair, name="reduce_chips")
    layer_grads = [_chip_sum(pair[i], scattered[i], chip_arr, name=f"chip_sum_{BIG[i]}") for i in range(nb)]
    small_sum = _chip_sum_ordered(small_pair, scattered[nb], chip_arr, name="chip_sum_small")

    def adamw_layers(core, layer, w, own, other, m, v):
        g = jnp.where(layer == core, own, other)
        return (g,) + _adamw_math(w, g, m, v)

    out_g, out_d, out_m, out_v = {}, {}, {}, {}
    for i, n in enumerate(BIG):
        other = _push_sibling([layer_grads[i][None]], name=f"share_cores_{n}")[0]
        out_g[n], out_d[n], out_m[n], out_v[n] = _elementwise(
            adamw_layers, [wts[n], layer_grads[i], other, mom[n], var[n]], [F32, F32, F32, F32], name=f"adamw_{n}",
            place=core_arr, flat=(1, 2))
    small_g = dict(zip(small_names, _unpack(small_sum, [small_full_shapes[n] for n in small_names])))
    small_g["b_gate"] = lax.dynamic_slice_in_dim(small_g["b_gate"], chip * 256, 256, axis=2)
    small_g["b_conv"] = lax.dynamic_slice_in_dim(small_g["b_conv"], chip * 128, 128, axis=2)
    packs = [_pack([d[n] for n in small_names])[None] for d in (wts, small_g, mom, var)]
    upd = _elementwise(_adamw_math, packs, [F32, F32, F32], name="adamw_small")
    shapes = [wts[n].shape for n in small_names]
    for n, d, m_new, v_new in zip(small_names, *[_unpack(u[0], shapes) for u in upd]):
        out_g[n], out_d[n], out_m[n], out_v[n] = small_g[n], d, m_new, v_new

    return (loss, grad_x, *[out_g[n] for n in WEIGHTS], *[out_d[n] for n in WEIGHTS], *[out_m[n] for n in WEIGHTS],
            *[out_v[n] for n in WEIGHTS])
```

```python
import functools
import math

import jax
import jax.numpy as jnp
from jax import lax
from jax.experimental import pallas as pl
from jax.experimental.pallas import tpu as pltpu

F32 = jnp.float32
BF16 = jnp.bfloat16

D_MODEL = 1024
DEPTH = 2
MEM_LEN = 256
EPS = 1e-6
CHUNK = 128
A_GROUPS = 4
A_WIDTH = 512
B_WIDTH = 512
C_WIDTH = 512
C_GROUP = 128
POOL_WINDOWS = (2, 4, 8, 16)
HALO = 16
IN_COLS = 6144
MIX_COLS = 3072
XATTN_HEADS = 4
HEAD_DIM = 256
FFN_HIDDEN = 2816
N_CHIPS = 4

ADAM_LR = 0.001
ADAM_B1 = 0.9
ADAM_B2 = 0.999
ADAM_EPS = 1e-08
ADAM_WD = 0.01
ADAM_STEP = 10

V7X_VMEM_BYTES = 64 * 1024 * 1024
VMEM_LIMIT = (V7X_VMEM_BYTES * 3) // 4
LANES = 128
MESH = pl.DeviceIdType.MESH
ANY = pl.BlockSpec(memory_space=pl.ANY)

BIG = ("w_in", "w_branch_a", "w_branch_b", "w_branch_c", "w_o", "w_xq", "w_xkv", "w_xo", "w_ffn_in", "w_ffn_out")
COL_SHARDED = ("w_in", "w_branch_a", "w_branch_b", "w_branch_c", "w_xkv", "w_ffn_in")
SMALL_REPL = ("g_mix", "a_ln_g", "a_ln_b", "a_ws", "a_bs", "c_wg", "c_scale", "g_xattn", "g_mem", "g_ffn", "g_final")
SMALL_SHARDED = ("b_gate", "b_conv")
WEIGHTS = ("g_mix", "w_in", "b_gate", "a_ln_g", "a_ln_b", "a_ws", "a_bs", "b_conv", "c_wg", "c_scale", "w_branch_a",
           "w_branch_b", "w_branch_c", "w_o", "g_xattn", "g_mem", "w_xq", "w_xkv", "w_xo", "g_ffn", "w_ffn_in",
           "w_ffn_out", "g_final")


def _params(sem=None):
    return pltpu.CompilerParams(dimension_semantics=sem, vmem_limit_bytes=VMEM_LIMIT)


def _blk(dim, pref):
    return pref if dim % pref == 0 else dim


def _mm(a, b, *, mode, out_dtype, name, tm, tn, tk, res=None, b_lead=None, b_halves=False, out_shards=None):
    dims = {"nn": (((1,), (0,)), ((), ())), "nt": (((1,), (1,)), ((), ())), "tn": (((0,), (0,)), ((), ()))}[mode]
    b_rows, b_last = b.shape[-2], b.shape[-1]
    if mode == "nn":
        m, k, n = a.shape[0], a.shape[1], b_last
        assert b_rows == k
    elif mode == "nt":
        m, k, n = a.shape[0], a.shape[1], b_rows
        assert b_last == k
    else:
        k, m, n = a.shape[0], a.shape[1], (2 * b_last if b_halves else b_last)
        assert b_rows == k
    tm, tn, tk = _blk(m, tm), _blk(n, tn), _blk(k, tk)
    nk = k // tk
    grid = (m // tm, n // tn, nk)

    if mode == "tn":
        a_spec = pl.BlockSpec((tk, tm), lambda i, j, kk: (kk, i))
        b_block, b_idx = (tk, tn), (lambda i, j, kk: (kk, j))
    else:
        a_spec = pl.BlockSpec((tm, tk), lambda i, j, kk: (i, kk))
        if mode == "nn":
            b_block, b_idx = (tk, tn), (lambda i, j, kk: (kk, j))
        else:
            b_block, b_idx = (tn, tk), (lambda i, j, kk: (j, kk))
    if b_halves:
        assert mode == "tn" and b_lead is None and b_last % tn == 0
        per_half = b_last // tn
        b_spec = pl.BlockSpec((None,) + b_block, lambda i, j, kk: (j // per_half, kk, j % per_half))
    elif b_lead is None:
        b_spec = pl.BlockSpec(b_block, b_idx)
    else:
        b_spec = pl.BlockSpec((None,) + b_block, lambda i, j, kk: (b_lead,) + b_idx(i, j, kk))
    in_specs = [a_spec, b_spec]
    operands = [a, b]
    if res is not None:
        in_specs.append(pl.BlockSpec((tm, tn), lambda i, j, kk: (i, j)))
        operands.append(res)
    if out_shards is None:
        out_shape = jax.ShapeDtypeStruct((m, n), out_dtype)
        out_spec = pl.BlockSpec((tm, tn), lambda i, j, kk: (i, j))
    else:
        per = n // out_shards
        assert per % tn == 0
        nps = per // tn
        out_shape = jax.ShapeDtypeStruct((out_shards, m, per), out_dtype)
        out_spec = pl.BlockSpec((None, tm, tn), lambda i, j, kk: (j // nps, i, j % nps))

    def body(*refs):
        a_ref, b_ref = refs[0], refs[1]
        res_ref = refs[2] if res is not None else None
        o_ref = refs[3] if res is not None else refs[2]
        part = lax.dot_general(a_ref[...].astype(BF16), b_ref[...].astype(BF16), dims, preferred_element_type=F32)

        def finish(acc):
            if res_ref is not None:
                acc = acc + res_ref[...]
            o_ref[...] = acc.astype(out_dtype)

        if nk == 1:
            finish(part)
        else:
            acc_ref = refs[-1]
            kk = pl.program_id(2)

            @pl.when(kk == 0)
            def _():
                acc_ref[...] = part

            @pl.when(kk > 0)
            def _():
                acc_ref[...] += part

            @pl.when(kk == nk - 1)
            def _():
                finish(acc_ref[...])

    scratch = [] if nk == 1 else [pltpu.VMEM((tm, tn), F32)]
    return pl.pallas_call(body, out_shape=out_shape, grid=grid, in_specs=in_specs, out_specs=out_spec,
                          scratch_shapes=scratch, name=name,
                          compiler_params=_params(("parallel", "parallel", "arbitrary")))(*operands)


ROW_CHUNK = 256
FFN_TILE = 1408
NT_DIMS = (((1,), (1,)), ((), ()))


def _norm_rows_into(x_ref, g_ref, h_ref, hs_ref, tm):
    rc = min(ROW_CHUNK, tm)
    for r0 in range(0, tm, rc):
        xv = x_ref[r0:r0 + rc, :]
        r = lax.rsqrt(jnp.mean(xv * xv, axis=-1, keepdims=True) + EPS)
        hv = (xv * r * g_ref[...]).astype(BF16)
        hs_ref[r0:r0 + rc, :] = hv
        h_ref[r0:r0 + rc, :] = hv


def _norm_mm(x, g, b, b_lead, *, tn, name):
    s, d = x.shape
    n = b.shape[-1]
    tm, tn = _blk(s, 1024), _blk(n, tn)

    def body(x_ref, g_ref, b_ref, h_ref, o_ref, hs_ref):
        @pl.when(pl.program_id(1) == 0)
        def _():
            _norm_rows_into(x_ref, g_ref, h_ref, hs_ref, tm)

        o_ref[...] = jnp.dot(hs_ref[...], b_ref[...], preferred_element_type=F32).astype(BF16)

    row = pl.BlockSpec((tm, d), lambda i, j: (i, 0))
    return pl.pallas_call(body, out_shape=(jax.ShapeDtypeStruct((s, d), BF16), jax.ShapeDtypeStruct((s, n), BF16)),
                          grid=(s // tm, n // tn),
                          in_specs=[row, pl.BlockSpec((1, d), lambda i, j: (0, 0)),
                                    pl.BlockSpec((None, d, tn), lambda i, j: (b_lead, 0, j))],
                          out_specs=(row, pl.BlockSpec((tm, tn), lambda i, j: (i, j))),
                          scratch_shapes=[pltpu.VMEM((tm, d), BF16)], name=name,
                          compiler_params=_params(("parallel", "arbitrary")))(x, g, b)


def _norm_ffn_in(x, g, b, b_lead, name):
    s, d = x.shape
    tm, tn = _blk(s, 512), FFN_TILE
    nj = FFN_HIDDEN // tn

    def body(x_ref, g_ref, bg_ref, bu_ref, h_ref, gu_ref, act_ref, hs_ref):
        @pl.when(pl.program_id(1) == 0)
        def _():
            _norm_rows_into(x_ref, g_ref, h_ref, hs_ref, tm)

        hv = hs_ref[...]
        gate = jnp.dot(hv, bg_ref[...], preferred_element_type=F32)
        up = jnp.dot(hv, bu_ref[...], preferred_element_type=F32)
        gu_ref[0] = gate.astype(BF16)
        gu_ref[1] = up.astype(BF16)
        act_ref[...] = (gate * _sigmoid(gate) * up).astype(BF16)

    row = pl.BlockSpec((tm, d), lambda i, j: (i, 0))
    out_shape = (jax.ShapeDtypeStruct((s, d), BF16), jax.ShapeDtypeStruct((2, s, FFN_HIDDEN), BF16),
                 jax.ShapeDtypeStruct((s, FFN_HIDDEN), BF16))
    return pl.pallas_call(body, out_shape=out_shape, grid=(s // tm, nj),
                          in_specs=[row, pl.BlockSpec((1, d), lambda i, j: (0, 0)),
                                    pl.BlockSpec((None, d, tn), lambda i, j: (b_lead, 0, j)),
                                    pl.BlockSpec((None, d, tn), lambda i, j: (b_lead, 0, j + nj))],
                          out_specs=(row, pl.BlockSpec((2, tm, tn), lambda i, j: (0, i, j)),
                                     pl.BlockSpec((tm, tn), lambda i, j: (i, j))),
                          scratch_shapes=[pltpu.VMEM((tm, d), BF16)], name=name,
                          compiler_params=_params(("parallel", "arbitrary")))(x, g, b, b)


def _d_act_swiglu(dx, w, b_lead, gu, name):
    s, d = dx.shape
    tm, tn = _blk(s, 512), FFN_TILE

    def body(dx_ref, w_ref, gu_ref, o_ref):
        dact = lax.dot_general(dx_ref[...].astype(BF16), w_ref[...], NT_DIMS, preferred_element_type=F32)
        gate = gu_ref[0].astype(F32)
        up = gu_ref[1].astype(F32)
        sg = _sigmoid(gate)
        o_ref[0] = (dact * up * sg * (1.0 + gate * (1.0 - sg))).astype(BF16)
        o_ref[1] = (dact * gate * sg).astype(BF16)

    halves = pl.BlockSpec((2, tm, tn), lambda i, j: (0, i, j))
    return pl.pallas_call(body, out_shape=jax.ShapeDtypeStruct(gu.shape, BF16), grid=(s // tm, FFN_HIDDEN // tn),
                          in_specs=[pl.BlockSpec((tm, d), lambda i, j: (i, 0)),
                                    pl.BlockSpec((None, tn, d), lambda i, j: (b_lead, j, 0)), halves],
                          out_specs=halves, name=name,
                          compiler_params=_params(("parallel", "parallel")))(dx, w, gu)


def _mm_nt_norm_bwd(a, b, b_lead, x, g, dres, *, tk, name, a_halves=False):
    s, d = x.shape
    k = 2 * a.shape[-1] if a_halves else a.shape[1]
    tm, tk = _blk(s, 1024), _blk(k, tk)
    nk = k // tk
    rc = min(ROW_CHUNK, tm)
    if a_halves:
        per_half = a.shape[-1] // tk
        a_spec = pl.BlockSpec((None, tm, tk), lambda i, kk: (kk // per_half, i, kk % per_half))
    else:
        a_spec = pl.BlockSpec((tm, tk), lambda i, kk: (i, kk))

    def body(a_ref, b_ref, x_ref, g_ref, r_ref, dx_ref, dg_ref, acc_ref):
        i, kk = pl.program_id(0), pl.program_id(1)
        part = lax.dot_general(a_ref[...].astype(BF16), b_ref[...], NT_DIMS, preferred_element_type=F32)

        @pl.when(kk == 0)
        def _():
            acc_ref[...] = part

        @pl.when(kk > 0)
        def _():
            acc_ref[...] += part

        @pl.when(kk == nk - 1)
        def _():
            gv = g_ref[...]
            dg_part = None
            for r0 in range(0, tm, rc):
                xv = x_ref[r0:r0 + rc, :]
                dhv = acc_ref[r0:r0 + rc, :]
                r = lax.rsqrt(jnp.mean(xv * xv, axis=-1, keepdims=True) + EPS)
                xhat = xv * r
                p = jnp.sum(dhv * xhat, axis=0, keepdims=True)
                dg_part = p if dg_part is None else dg_part + p
                dxhat = dhv * gv
                dx_ref[r0:r0 + rc, :] = r_ref[r0:r0 + rc, :] + r * (
                    dxhat - xhat * jnp.mean(dxhat * xhat, axis=-1, keepdims=True))

            @pl.when(i == 0)
            def _():
                dg_ref[...] = dg_part

            @pl.when(i > 0)
            def _():
                dg_ref[...] += dg_part

    row = pl.BlockSpec((tm, d), lambda i, kk: (i, 0))
    vec = pl.BlockSpec((1, d), lambda i, kk: (0, 0))
    return pl.pallas_call(body, out_shape=(jax.ShapeDtypeStruct((s, d), F32), jax.ShapeDtypeStruct((1, d), F32)),
                          grid=(s // tm, nk),
                          in_specs=[a_spec, pl.BlockSpec((None, d, tk), lambda i, kk: (b_lead, 0, kk)), row, vec, row],
                          out_specs=(row, vec), scratch_shapes=[pltpu.VMEM((tm, d), F32)], name=name,
                          compiler_params=_params(("arbitrary", "arbitrary")))(a, b, x, g, dres)


def _rms_fwd(x, g, name):
    s, d = x.shape
    tm = _blk(s, 512)

    def body(x_ref, g_ref, o_ref):
        xv = x_ref[...]
        r = lax.rsqrt(jnp.mean(xv * xv, axis=-1, keepdims=True) + EPS)
        o_ref[...] = (xv * r * g_ref[...]).astype(BF16)

    return pl.pallas_call(body, out_shape=jax.ShapeDtypeStruct((s, d), BF16), grid=(s // tm,),
                          in_specs=[pl.BlockSpec((tm, d), lambda i: (i, 0)), pl.BlockSpec((1, d), lambda i: (0, 0))],
                          out_specs=pl.BlockSpec((tm, d), lambda i: (i, 0)), name=name,
                          compiler_params=_params(("parallel",)))(x, g)


def _rms_bwd(x, g, dh, dres, name):
    s, d = x.shape
    tm = _blk(s, 512)
    has_res = dres is not None

    def body(*refs):
        x_ref, g_ref, dh_ref = refs[0], refs[1], refs[2]
        dx_ref, dg_ref = refs[-2], refs[-1]
        xv = x_ref[...]
        r = lax.rsqrt(jnp.mean(xv * xv, axis=-1, keepdims=True) + EPS)
        xhat = xv * r
        dhv = dh_ref[...].astype(F32)
        part = jnp.sum(dhv * xhat, axis=0, keepdims=True)

        @pl.when(pl.program_id(0) == 0)
        def _():
            dg_ref[...] = part

        @pl.when(pl.program_id(0) > 0)
        def _():
            dg_ref[...] += part

        dxhat = dhv * g_ref[...]
        dx = r * (dxhat - xhat * jnp.mean(dxhat * xhat, axis=-1, keepdims=True))
        if has_res:
            dx = dx + refs[3][...]
        dx_ref[...] = dx

    row = pl.BlockSpec((tm, d), lambda i: (i, 0))
    vec = pl.BlockSpec((1, d), lambda i: (0, 0))
    in_specs = [row, vec, row] + ([row] if has_res else [])
    operands = [x, g, dh] + ([dres] if has_res else [])
    return pl.pallas_call(body, out_shape=(jax.ShapeDtypeStruct((s, d), F32), jax.ShapeDtypeStruct((1, d), F32)),
                          grid=(s // tm,), in_specs=in_specs, out_specs=(row, vec), name=name,
                          compiler_params=_params(("arbitrary",)))(*operands)


def _final_loss(x, g, target, name):
    s, d = x.shape
    tm = _blk(s, 512)

    def body(x_ref, g_ref, t_ref, loss_ref, dx_ref, dg_ref):
        xv = x_ref[...]
        gv = g_ref[...]
        r = lax.rsqrt(jnp.mean(xv * xv, axis=-1, keepdims=True) + EPS)
        xhat = xv * r
        err = xhat * gv - t_ref[...]
        lpart = 0.5 * jnp.sum(jnp.mean(err * err, axis=-1, keepdims=True), axis=0, keepdims=True)
        dy = err * (1.0 / d)
        gpart = jnp.sum(dy * xhat, axis=0, keepdims=True)

        @pl.when(pl.program_id(0) == 0)
        def _():
            loss_ref[...] = lpart
            dg_ref[...] = gpart

        @pl.when(pl.program_id(0) > 0)
        def _():
            loss_ref[...] += lpart
            dg_ref[...] += gpart

        dxhat = dy * gv
        dx_ref[...] = r * (dxhat - xhat * jnp.mean(dxhat * xhat, axis=-1, keepdims=True))

    row = pl.BlockSpec((tm, d), lambda i: (i, 0))
    vec = pl.BlockSpec((1, d), lambda i: (0, 0))
    one = pl.BlockSpec((1, 1), lambda i: (0, 0))
    return pl.pallas_call(body, out_shape=(jax.ShapeDtypeStruct((1, 1), F32), jax.ShapeDtypeStruct((s, d), F32),
                                           jax.ShapeDtypeStruct((1, d), F32)),
                          grid=(s // tm,), in_specs=[row, vec, row], out_specs=(one, row, vec), name=name,
                          compiler_params=_params(("arbitrary",)))(x, g, target)


def _erf_parts(x):
    cdf = 0.5 * (1.0 + lax.erf(x * (1.0 / math.sqrt(2.0))))
    return cdf


def _shift_down(ext, k):
    return pltpu.roll(ext, k, 0)


def _shift_up(ext, k):
    return pltpu.roll(ext, ext.shape[0] - k, 0)


def _mixer_forward_math(z, halo, row0, prm, tm):
    ln_g, ln_b, wtril, bsb, conv_w, wg, scale = prm
    za = z[:, 0:2 * A_WIDTH]
    gb = z[:, 1024:1536]
    gc = z[:, 1536:2048]
    xin = z[:, 2048:2560]
    zc = z[:, 2560:3072]
    cdf = _erf_parts(za)
    act = za * cdf
    u = act[:, :A_WIDTH]
    v = act[:, A_WIDTH:]
    mu = jnp.mean(v, axis=-1, keepdims=True)
    vc = v - mu
    rstd = lax.rsqrt(jnp.mean(vc * vc, axis=-1, keepdims=True) + EPS)
    vhat = vc * rstd
    vn = (vhat * ln_g + ln_b).astype(BF16)
    rows = []
    for c in range(tm // CHUNK):
        cols = []
        for g in range(A_GROUPS):
            blk = vn[c * CHUNK:(c + 1) * CHUNK, g * CHUNK:(g + 1) * CHUNK]
            cols.append(jnp.dot(wtril[g], blk, preferred_element_type=F32) + bsb[g])
        rows.append(jnp.concatenate(cols, axis=1))
    mixed = jnp.concatenate(rows, axis=0)
    ya = u * mixed
    y = gc * xin
    yext = jnp.concatenate([halo[:, 0:512] * halo[:, 512:1024], y], axis=0)
    y1 = _shift_down(yext, 1)[HALO:]
    y2 = _shift_down(yext, 2)[HALO:]
    conv = conv_w[0:1, :] * y2 + conv_w[1:2, :] * y1 + conv_w[2:3, :] * y
    yb = gb * conv
    t = row0 + lax.broadcasted_iota(jnp.int32, (tm, 1), 0)
    zext = jnp.concatenate([halo[:, 1024:1536], zc], axis=0)
    pooled, p = [], []
    for gi, win in enumerate(POOL_WINDOWS):
        sw = zext[:, gi * C_GROUP:(gi + 1) * C_GROUP]
        for step in range(gi + 1):
            sw = sw + _shift_down(sw, 2 ** step)
        cnt = jnp.minimum(t + 1, win).astype(F32)
        pg = sw[HALO:] / cnt - zc[:, gi * C_GROUP:(gi + 1) * C_GROUP]
        pooled.append(pg)
        p.append(jnp.dot(pg.astype(BF16), wg[gi], preferred_element_type=F32))
    p = jnp.concatenate(p, axis=1)
    yc = p * scale
    saved = dict(za=za, cdf=cdf, u=u, mixed=mixed, rstd=rstd, vhat=vhat, vn=vn, gb=gb, gc=gc, xin=xin, y=y, y1=y1, y2=y2,
                 conv=conv, pooled=pooled, p=p, t=t)
    return ya, yb, yc, saved


def _mixer_specs(tm, s):
    nb16 = tm // HALO
    main = pl.BlockSpec((tm, MIX_COLS), lambda i: (i, 0))
    prev = pl.BlockSpec((HALO, 1536), lambda i: (jnp.maximum(i * nb16 - 1, 0), 1))
    return main, prev


def _full(shape):
    n = len(shape)
    return pl.BlockSpec(shape, lambda i: (0,) * n)


def _mixer_fwd(proj, prm, name):
    s = proj.shape[0]
    tm = _blk(s, 256)
    main, prev = _mixer_specs(tm, s)

    def body(z_ref, h_ref, lng, lnb, wt, bsb, cw, wg, sc, o_ref):
        i = pl.program_id(0)
        halo = jnp.where(i > 0, h_ref[...].astype(F32), 0.0)
        prm_v = (lng[...], lnb[...], wt[...], bsb[...], cw[...], wg[...], sc[...])
        ya, yb, yc, _ = _mixer_forward_math(z_ref[...].astype(F32), halo, i * tm, prm_v, tm)
        o_ref[...] = jnp.concatenate([ya, yb, yc], axis=1).astype(BF16)

    return pl.pallas_call(body, out_shape=jax.ShapeDtypeStruct((s, 1536), BF16), grid=(s // tm,),
                          in_specs=[main, prev] + [_full(p.shape) for p in prm],
                          out_specs=pl.BlockSpec((tm, 1536), lambda i: (i, 0)), name=name,
                          compiler_params=_params(("parallel",)))(proj, proj, *prm)


def _mixer_bwd(proj, dypre, dproj_in, prm, wtril_t, name):
    s = proj.shape[0]
    tm = _blk(s, 256)
    nblk = s // tm
    nb16 = tm // HALO
    main, prev = _mixer_specs(tm, s)
    nxt_row = lambda i: jnp.minimum((i + 1) * nb16, s // HALO - 1)
    next_dy = pl.BlockSpec((HALO, 1536), lambda i: (nxt_row(i), 0))
    next_gb = pl.BlockSpec((HALO, 512), lambda i: (nxt_row(i), 2))
    ones8 = jnp.ones((8, CHUNK), F32)

    def body(z_ref, h_ref, dy_ref, ndy_ref, ngb_ref, _alias, lng, lnb, wt, bsb, cw, wg, sc, wtt, ones_ref,
             dz_ref, dlng_ref, dlnb_ref, dws_ref, dbs_ref, dcw_ref, dwg_ref, dsc_ref):
        i = pl.program_id(0)
        first = i == 0
        halo = jnp.where(i > 0, h_ref[...].astype(F32), 0.0)
        prm_v = (lng[...], lnb[...], wt[...], bsb[...], cw[...], wg[...], sc[...])
        _, _, _, sv = _mixer_forward_math(z_ref[...].astype(F32), halo, i * tm, prm_v, tm)
        dy = dy_ref[...].astype(F32)
        dya, dyb, dyc = dy[:, 0:512], dy[:, 512:1024], dy[:, 1024:1536]
        not_last = i < nblk - 1
        ndy = jnp.where(not_last, ndy_ref[...].astype(F32), 0.0)
        ngb = ngb_ref[...].astype(F32)

        def accumulate(ref, val):
            @pl.when(first)
            def _():
                ref[...] = val

            @pl.when(jnp.logical_not(first))
            def _():
                ref[...] += val

        du = dya * sv["mixed"]
        dmixed = dya * sv["u"]
        vn = sv["vn"]
        wttv = wtt[...]
        dvn_rows = []
        dws_parts = [None] * A_GROUPS
        dbs_parts = [None] * A_GROUPS
        for c in range(tm // CHUNK):
            cols = []
            for g in range(A_GROUPS):
                dm = dmixed[c * CHUNK:(c + 1) * CHUNK, g * CHUNK:(g + 1) * CHUNK]
                dmb = dm.astype(BF16)
                vb = vn[c * CHUNK:(c + 1) * CHUNK, g * CHUNK:(g + 1) * CHUNK]
                w_part = lax.dot_general(dmb, vb, (((1,), (1,)), ((), ())), preferred_element_type=F32)
                b_part = lax.dot_general(ones_ref[...], dm, (((1,), (1,)), ((), ())), preferred_element_type=F32,
                                         precision=lax.Precision.HIGHEST)[0:1, :]
                dws_parts[g] = w_part if dws_parts[g] is None else dws_parts[g] + w_part
                dbs_parts[g] = b_part if dbs_parts[g] is None else dbs_parts[g] + b_part
                cols.append(jnp.dot(wttv[g], dmb, preferred_element_type=F32))
            dvn_rows.append(jnp.concatenate(cols, axis=1))
        dvn = jnp.concatenate(dvn_rows, axis=0)
        tri = lax.broadcasted_iota(jnp.int32, (CHUNK, CHUNK), 0) >= lax.broadcasted_iota(jnp.int32, (CHUNK, CHUNK), 1)
        accumulate(dws_ref, jnp.stack([jnp.where(tri, w, 0.0) for w in dws_parts], axis=0))
        accumulate(dbs_ref, jnp.concatenate(dbs_parts, axis=0))
        vhat = sv["vhat"]
        accumulate(dlng_ref, jnp.sum(dvn * vhat, axis=0, keepdims=True))
        accumulate(dlnb_ref, jnp.sum(dvn, axis=0, keepdims=True))
        dvhat = dvn * lng[...]
        dv = sv["rstd"] * (dvhat - jnp.mean(dvhat, axis=-1, keepdims=True)
                           - vhat * jnp.mean(dvhat * vhat, axis=-1, keepdims=True))
        za = sv["za"]
        dgelu = sv["cdf"] + za * jnp.exp(-0.5 * za * za) * (1.0 / math.sqrt(2.0 * math.pi))
        dza = jnp.concatenate([du, dv], axis=1) * dgelu
        cwv = cw[...]
        dconv = dyb * sv["gb"]
        dgb = dyb * sv["conv"]
        dcext = jnp.concatenate([dconv, ndy[:, 512:1024] * ngb], axis=0)
        d1 = _shift_up(dcext, 1)[:tm]
        d2 = _shift_up(dcext, 2)[:tm]
        dyy = cwv[2:3, :] * dconv + cwv[1:2, :] * d1 + cwv[0:1, :] * d2
        dgc = dyy * sv["xin"]
        dxin = dyy * sv["gc"]
        accumulate(dcw_ref, jnp.concatenate([jnp.sum(dconv * sv["y2"], axis=0, keepdims=True),
                                             jnp.sum(dconv * sv["y1"], axis=0, keepdims=True),
                                             jnp.sum(dconv * sv["y"], axis=0, keepdims=True)], axis=0))
        scv = sc[...]
        accumulate(dsc_ref, jnp.sum(dyc * sv["p"], axis=0, keepdims=True))
        dp = dyc * scv
        ndp = ndy[:, 1024:1536] * scv
        wgv = wg[...]
        dzc_cols, dwg_parts = [], []
        for gi, win in enumerate(POOL_WINDOWS):
            sl = slice(gi * C_GROUP, (gi + 1) * C_GROUP)
            dpb = dp[:, sl].astype(BF16)
            dpool = lax.dot_general(dpb, wgv[gi], (((1,), (1,)), ((), ())), preferred_element_type=F32)
            ndpool = lax.dot_general(ndp[:, sl].astype(BF16), wgv[gi], (((1,), (1,)), ((), ())),
                                     preferred_element_type=F32)
            dwg_parts.append(lax.dot_general(sv["pooled"][gi].astype(BF16), dpb, (((0,), (0,)), ((), ())),
                                             preferred_element_type=F32))
            cnt = jnp.minimum(sv["t"] + 1, win).astype(F32)
            fw = jnp.concatenate([dpool / cnt, ndpool * (1.0 / win)], axis=0)
            for step in range(gi + 1):
                fw = fw + _shift_up(fw, 2 ** step)
            dzc_cols.append(fw[:tm] - dpool)
        accumulate(dwg_ref, jnp.stack(dwg_parts, axis=0))
        dzc = jnp.concatenate(dzc_cols, axis=1)
        dz_ref[...] = jnp.concatenate([dza, dgb, dgc, dxin, dzc], axis=1).astype(BF16)

    out_shape = (jax.ShapeDtypeStruct(dproj_in.shape, BF16), jax.ShapeDtypeStruct((1, A_WIDTH), F32),
                 jax.ShapeDtypeStruct((1, A_WIDTH), F32), jax.ShapeDtypeStruct((A_GROUPS, CHUNK, CHUNK), F32),
                 jax.ShapeDtypeStruct((A_GROUPS, CHUNK), F32), jax.ShapeDtypeStruct((3, B_WIDTH), F32),
                 jax.ShapeDtypeStruct((4, C_GROUP, C_GROUP), F32), jax.ShapeDtypeStruct((1, C_WIDTH), F32))
    out_specs = (main,) + tuple(_full(o.shape) for o in out_shape[1:])
    in_specs = [main, prev, pl.BlockSpec((tm, 1536), lambda i: (i, 0)), next_dy, next_gb, ANY] + \
               [_full(p.shape) for p in prm] + [_full(wtril_t.shape), _full(ones8.shape)]
    return pl.pallas_call(body, out_shape=out_shape, grid=(nblk,), in_specs=in_specs, out_specs=out_specs,
                          input_output_aliases={5: 0}, name=name,
                          compiler_params=_params(("arbitrary",)))(proj, proj, dypre, dypre, proj, dproj_in, *prm,
                                                                  wtril_t, ones8)


def _sigmoid(x):
    return 1.0 / (1.0 + jnp.exp(-x))


def _merge_fwd(ypre, proj, wa, wb, wc, bgate, name):
    s = ypre.shape[0]
    tm = _blk(s, 512)

    def body(y_ref, zg_ref, wa_ref, wb_ref, wc_ref, bg_ref, o_ref):
        yv = y_ref[...]
        acc = None
        for i, w_ref in enumerate((wa_ref, wb_ref, wc_ref)):
            br = jnp.dot(yv[:, i * 512:(i + 1) * 512], w_ref[...], preferred_element_type=F32)
            gate = _sigmoid(zg_ref[:, i * D_MODEL:(i + 1) * D_MODEL].astype(F32) + bg_ref[i:i + 1, :])
            acc = gate * br if acc is None else acc + gate * br
        o_ref[...] = acc.astype(BF16)

    wspec = _full(wa.shape)
    return pl.pallas_call(body, out_shape=jax.ShapeDtypeStruct((s, D_MODEL), BF16), grid=(s // tm,),
                          in_specs=[pl.BlockSpec((tm, 1536), lambda i: (i, 0)),
                                    pl.BlockSpec((tm, 3 * D_MODEL), lambda i: (i, 1)), wspec, wspec, wspec,
                                    _full(bgate.shape)],
                          out_specs=pl.BlockSpec((tm, D_MODEL), lambda i: (i, 0)), name=name,
                          compiler_params=_params(("parallel",)))(ypre, proj, wa, wb, wc, bgate)


def _merge_bwd(ypre, proj, dmerged, wa, wb, wc, bgate, name):
    s = ypre.shape[0]
    tm = _blk(s, 512)

    def body(y_ref, zg_ref, dm_ref, wa_ref, wb_ref, wc_ref, bg_ref, dyp_ref, dzg_ref, dbg_ref, dwa_ref, dwb_ref,
             dwc_ref):
        first = pl.program_id(0) == 0

        def accumulate(ref, val):
            @pl.when(first)
            def _():
                ref[...] = val

            @pl.when(jnp.logical_not(first))
            def _():
                ref[...] += val

        yv = y_ref[...]
        dm = dm_ref[...].astype(F32)
        dyp, dzg, dbg = [], [], []
        for i, (w_ref, dw_ref) in enumerate(((wa_ref, dwa_ref), (wb_ref, dwb_ref), (wc_ref, dwc_ref))):
            wv = w_ref[...]
            yi = yv[:, i * 512:(i + 1) * 512]
            br = jnp.dot(yi, wv, preferred_element_type=F32)
            gate = _sigmoid(zg_ref[:, i * D_MODEL:(i + 1) * D_MODEL].astype(F32) + bg_ref[i:i + 1, :])
            dbi = (dm * gate).astype(BF16)
            dzi = dm * br * gate * (1.0 - gate)
            dzg.append(dzi.astype(BF16))
            dbg.append(jnp.sum(dzi, axis=0, keepdims=True))
            dyp.append(lax.dot_general(dbi, wv, (((1,), (1,)), ((), ())), preferred_element_type=F32).astype(BF16))
            accumulate(dw_ref, lax.dot_general(yi, dbi, (((0,), (0,)), ((), ())), preferred_element_type=F32))
        dyp_ref[...] = jnp.concatenate(dyp, axis=1)
        dzg_ref[...] = jnp.concatenate(dzg, axis=1)
        accumulate(dbg_ref, jnp.concatenate(dbg, axis=0))

    wspec = _full(wa.shape)
    dw_shape = jax.ShapeDtypeStruct(wa.shape, F32)
    out_shape = (jax.ShapeDtypeStruct((s, 1536), BF16), jax.ShapeDtypeStruct((s, IN_COLS), BF16),
                 jax.ShapeDtypeStruct((3, D_MODEL), F32), dw_shape, dw_shape, dw_shape)
    return pl.pallas_call(body, out_shape=out_shape, grid=(s // tm,),
                          in_specs=[pl.BlockSpec((tm, 1536), lambda i: (i, 0)),
                                    pl.BlockSpec((tm, 3 * D_MODEL), lambda i: (i, 1)),
                                    pl.BlockSpec((tm, D_MODEL), lambda i: (i, 0)), wspec, wspec, wspec,
                                    _full(bgate.shape)],
                          out_specs=(pl.BlockSpec((tm, 1536), lambda i: (i, 0)),
                                     pl.BlockSpec((tm, 3 * D_MODEL), lambda i: (i, 1)), _full((3, D_MODEL)),
                                     wspec, wspec, wspec),
                          name=name, compiler_params=_params(("arbitrary",)))(ypre, proj, dmerged, wa, wb, wc, bgate)


def _softmax_rows(q, k):
    sc = lax.dot_general(q, k, (((1,), (1,)), ((), ())), preferred_element_type=F32) * (HEAD_DIM ** -0.5)
    e = jnp.exp(sc - jnp.max(sc, axis=-1, keepdims=True))
    return e / jnp.sum(e, axis=-1, keepdims=True)


def _attn_fwd(q, kv, name):
    s = q.shape[0]
    tm = _blk(s, 512)

    def body(q_ref, kv_ref, o_ref):
        outs = []
        for h in range(XATTN_HEADS):
            sl = slice(h * HEAD_DIM, (h + 1) * HEAD_DIM)
            p = _softmax_rows(q_ref[:, sl], kv_ref[:, sl])
            outs.append(jnp.dot(p.astype(BF16), kv_ref[:, D_MODEL + h * HEAD_DIM:D_MODEL + (h + 1) * HEAD_DIM],
                                preferred_element_type=F32))
        o_ref[...] = jnp.concatenate(outs, axis=1).astype(BF16)

    return pl.pallas_call(body, out_shape=jax.ShapeDtypeStruct((s, D_MODEL), BF16), grid=(s // tm,),
                          in_specs=[pl.BlockSpec((tm, D_MODEL), lambda i: (i, 0)), _full(kv.shape)],
                          out_specs=pl.BlockSpec((tm, D_MODEL), lambda i: (i, 0)), name=name,
                          compiler_params=_params(("parallel",)))(q, kv)


def _attn_bwd(q, kv, do, name):
    s = q.shape[0]
    tm = _blk(s, 512)

    def body(q_ref, kv_ref, do_ref, dq_ref, dkv_ref):
        dqs, dks, dvs = [], [], []
        for h in range(XATTN_HEADS):
            sl = slice(h * HEAD_DIM, (h + 1) * HEAD_DIM)
            vsl = slice(D_MODEL + h * HEAD_DIM, D_MODEL + (h + 1) * HEAD_DIM)
            qh, kh, vh, doh = q_ref[:, sl], kv_ref[:, sl], kv_ref[:, vsl], do_ref[:, sl]
            p = _softmax_rows(qh, kh)
            pb = p.astype(BF16)
            dvs.append(lax.dot_general(pb, doh, (((0,), (0,)), ((), ())), preferred_element_type=F32))
            dp = lax.dot_general(doh, vh, (((1,), (1,)), ((), ())), preferred_element_type=F32)
            ds = p * (dp - jnp.sum(dp * p, axis=-1, keepdims=True)) * (HEAD_DIM ** -0.5)
            dsb = ds.astype(BF16)
            dqs.append(jnp.dot(dsb, kh, preferred_element_type=F32))
            dks.append(lax.dot_general(dsb, qh, (((0,), (0,)), ((), ())), preferred_element_type=F32))
        dq_ref[...] = jnp.concatenate(dqs, axis=1).astype(BF16)
        part = jnp.concatenate(dks + dvs, axis=1)

        @pl.when(pl.program_id(0) == 0)
        def _():
            dkv_ref[...] = part

        @pl.when(pl.program_id(0) > 0)
        def _():
            dkv_ref[...] += part

    row = pl.BlockSpec((tm, D_MODEL), lambda i: (i, 0))
    return pl.pallas_call(body, out_shape=(jax.ShapeDtypeStruct((s, D_MODEL), BF16),
                                           jax.ShapeDtypeStruct(kv.shape, F32)),
                          grid=(s // tm,), in_specs=[row, _full(kv.shape), row], out_specs=(row, _full(kv.shape)),
                          name=name, compiler_params=_params(("arbitrary",)))(q, kv, do)


def _rows_block(rows, cols):
    target = (512 * 1024) // cols
    fits = [cand for cand in range(8, rows + 1, 8) if rows % cand == 0 and cand <= target]
    return fits[-1] if fits else rows


def _elementwise(fn, ins, out_dtypes, name, place=None, flat=()):
    lead, rows, cols = [a for i, a in enumerate(ins) if i not in flat][0].shape
    tr = _rows_block(rows, cols)
    spec = pl.BlockSpec((None, tr, cols), lambda l, i, *_: (l, i, 0))
    flat_spec = pl.BlockSpec((tr, cols), lambda l, i, *_: (i, 0))
    n_in = len(ins)
    in_specs = [flat_spec if i in flat else spec for i in range(n_in)]
    out_specs = tuple([spec] * len(out_dtypes))
    out_shape = tuple(jax.ShapeDtypeStruct((lead, rows, cols), dt) for dt in out_dtypes)
    sem = _params(("parallel", "parallel"))

    def write(refs, outs):
        for o_ref, o in zip(refs, outs):
            o_ref[...] = o.astype(o_ref.dtype)

    if place is None:
        def body(*refs):
            write(refs[n_in:], fn(*[r[...] for r in refs[:n_in]]))

        return pl.pallas_call(body, out_shape=out_shape, grid=(lead, rows // tr), in_specs=in_specs,
                              out_specs=out_specs, name=name, compiler_params=sem)(*ins)

    def body_placed(p_ref, *refs):
        write(refs[n_in:], fn(p_ref[0], pl.program_id(0), *[r[...] for r in refs[:n_in]]))

    grid_spec = pltpu.PrefetchScalarGridSpec(num_scalar_prefetch=1, grid=(lead, rows // tr), in_specs=in_specs,
                                             out_specs=out_specs)
    return pl.pallas_call(body_placed, out_shape=out_shape, grid_spec=grid_spec, name=name,
                          compiler_params=sem)(place, *ins)


def _cast_to_slot(w, chip, dtype, name):
    lead, rows, cols = w.shape
    tr = _rows_block(rows, cols)

    def body(chip_ref, w_ref, o_ref):
        o_ref[...] = w_ref[...].astype(dtype)

    grid_spec = pltpu.PrefetchScalarGridSpec(
        num_scalar_prefetch=1, grid=(lead, rows // tr),
        in_specs=[pl.BlockSpec((None, tr, cols), lambda l, i, chip_ref: (l, i, 0))],
        out_specs=pl.BlockSpec((None, None, tr, cols), lambda l, i, chip_ref: (l, chip_ref[0], i, 0)))
    return pl.pallas_call(body, out_shape=jax.ShapeDtypeStruct((lead, N_CHIPS, rows, cols), dtype), grid_spec=grid_spec,
                          name=name, compiler_params=_params(("parallel", "parallel")))(chip, w)


def _chip_sum(pair, rx, chip, name):
    _, rows, cols = pair.shape
    tr = _rows_block(rows, cols)

    def body(chip_ref, p_ref, rx_ref, o_ref):
        acc = p_ref[...].astype(F32)
        for j in range(3):
            acc = acc + rx_ref[j].astype(F32)
        o_ref[...] = acc

    grid_spec = pltpu.PrefetchScalarGridSpec(
        num_scalar_prefetch=1, grid=(rows // tr,),
        in_specs=[pl.BlockSpec((None, tr, cols), lambda i, chip_ref: (chip_ref[0], i, 0)),
                  pl.BlockSpec((3, tr, cols), lambda i, chip_ref: (0, i, 0))],
        out_specs=pl.BlockSpec((tr, cols), lambda i, chip_ref: (i, 0)))
    return pl.pallas_call(body, out_shape=jax.ShapeDtypeStruct((rows, cols), F32), grid_spec=grid_spec, name=name,
                          compiler_params=_params(("parallel",)))(chip, pair, rx)


def _chip_sum_ordered(own, rx, chip, name):
    rows, cols = own.shape

    def body(chip_ref, own_ref, rx_ref, o_ref):
        me = chip_ref[0]
        acc = None
        for k in range(N_CHIPS):
            rel = jnp.bitwise_xor(me, k)
            term = jnp.where(rel == 0, own_ref[...],
                             jnp.where(rel == 2, rx_ref[0], jnp.where(rel == 1, rx_ref[1], rx_ref[2])))
            acc = term if acc is None else acc + term
        o_ref[...] = acc

    grid_spec = pltpu.PrefetchScalarGridSpec(
        num_scalar_prefetch=1, grid=(1,),
        in_specs=[pl.BlockSpec((rows, cols), lambda i, chip_ref: (0, 0)),
                  pl.BlockSpec((3, rows, cols), lambda i, chip_ref: (0, 0, 0))],
        out_specs=pl.BlockSpec((rows, cols), lambda i, chip_ref: (0, 0)))
    return pl.pallas_call(body, out_shape=jax.ShapeDtypeStruct((rows, cols), F32), grid_spec=grid_spec, name=name,
                          compiler_params=_params(("arbitrary",)))(chip, own, rx)


def _adamw_math(w, g, m, v):
    m = ADAM_B1 * m + (1.0 - ADAM_B1) * g
    v = ADAM_B2 * v + (1.0 - ADAM_B2) * (g * g)
    m_hat = m / (1.0 - ADAM_B1 ** ADAM_STEP)
    v_hat = v / (1.0 - ADAM_B2 ** ADAM_STEP)
    delta = -ADAM_LR * (m_hat / (jnp.sqrt(v_hat) + ADAM_EPS) + ADAM_WD * w)
    return delta, m, v


def _place():
    x, y, c = lax.axis_index("x"), lax.axis_index("y"), lax.axis_index("c")
    chips = [(1 - x, y), (x, 1 - y), (1 - x, 1 - y)]
    return x, y, c, 2 * x + y, chips


def _run_remote(copies):
    for send, _ in copies:
        send.start()
    for send, recv in copies:
        send.wait_send()
        recv.wait_recv()


def _chip_gather(fulls, name):
    n = len(fulls)

    def body(*refs):
        outs = refs[n:2 * n]
        ssem, rsem = refs[2 * n:]
        x, y, c, me, chips = _place()
        copies = []
        for w in range(n):
            mine = outs[w].at[c, me]
            for j, (px, py) in enumerate(chips):
                sems = (ssem.at[3 * w + j], rsem.at[3 * w + j])
                peer = dict(device_id=(px, py, c), device_id_type=MESH)
                send = pltpu.make_async_remote_copy(mine, mine, *sems, **peer)
                recv = pltpu.make_async_remote_copy(mine, outs[w].at[c, 2 * px + py], *sems, **peer)
                copies.append((send, recv))
        _run_remote(copies)

    out_shape = tuple(jax.ShapeDtypeStruct(f.shape, f.dtype) for f in fulls)
    return pl.pallas_call(body, out_shape=out_shape, in_specs=[ANY] * n, out_specs=tuple([ANY] * n),
                          input_output_aliases={w: w for w in range(n)},
                          scratch_shapes=[pltpu.SemaphoreType.DMA((3 * n,)), pltpu.SemaphoreType.DMA((3 * n,))],
                          name=name)(*fulls)


def _push_layer(full, core, name):
    _, _, rows, cols = full.shape
    tr = _rows_block(rows, cols)
    nrow = rows // tr

    def body(core_ref, blk_ref, out_ref, ssem, rsem):
        x, y, c = lax.axis_index("x"), lax.axis_index("y"), lax.axis_index("c")
        k, i = pl.program_id(0), pl.program_id(1)
        peer = dict(device_id=(x, y, 1 - c), device_id_type=MESH)
        dst = out_ref.at[pl.ds(c, 1), pl.ds(k, 1), pl.ds(pl.multiple_of(i * tr, 8), tr)]
        cp = pltpu.make_async_remote_copy(blk_ref, dst, ssem, rsem, **peer)
        cp.start()
        cp.wait_send()

        @pl.when(jnp.logical_and(k == N_CHIPS - 1, i == nrow - 1))
        def _():
            landed = out_ref.at[1 - c]
            pltpu.make_async_remote_copy(landed, landed, ssem, rsem, **peer).wait_recv()

    grid_spec = pltpu.PrefetchScalarGridSpec(
        num_scalar_prefetch=1, grid=(N_CHIPS, nrow),
        in_specs=[pl.BlockSpec((1, 1, tr, cols), lambda k, i, core_ref: (core_ref[0], k, i, 0))],
        out_specs=ANY, scratch_shapes=[pltpu.SemaphoreType.DMA, pltpu.SemaphoreType.DMA])
    return pl.pallas_call(body, out_shape=jax.ShapeDtypeStruct(full.shape, full.dtype), grid_spec=grid_spec,
                          input_output_aliases={1: 0}, name=name,
                          compiler_params=_params(("arbitrary", "arbitrary")))(core, full)


def _push_sibling(srcs, name):
    lead, rows, cols = srcs[0].shape
    tr = _rows_block(rows, cols)
    nrow = rows // tr
    nsrc = len(srcs)

    def body(*refs):
        blks, rx_ref = refs[:nsrc], refs[nsrc]
        ssem, rsem = refs[nsrc + 1:]
        x, y, c = lax.axis_index("x"), lax.axis_index("y"), lax.axis_index("c")
        k, i = pl.program_id(0), pl.program_id(1)
        peer = dict(device_id=(x, y, 1 - c), device_id_type=MESH)
        dst = rx_ref.at[pl.ds(k, 1), pl.ds(pl.multiple_of(i * tr, 8), tr)]

        def send(blk):
            cp = pltpu.make_async_remote_copy(blk, dst, ssem, rsem, **peer)
            cp.start()
            cp.wait_send()

        if nsrc == 1:
            send(blks[0])
        else:
            @pl.when(c == 0)
            def _():
                send(blks[1])

            @pl.when(c == 1)
            def _():
                send(blks[0])

        @pl.when(jnp.logical_and(k == lead - 1, i == nrow - 1))
        def _():
            pltpu.make_async_remote_copy(rx_ref, rx_ref, ssem, rsem, **peer).wait_recv()

    spec = pl.BlockSpec((1, tr, cols), lambda k, i: (k, i, 0))
    return pl.pallas_call(body, out_shape=jax.ShapeDtypeStruct(srcs[0].shape, srcs[0].dtype), grid=(lead, nrow),
                          in_specs=[spec] * nsrc, out_specs=ANY,
                          scratch_shapes=[pltpu.SemaphoreType.DMA, pltpu.SemaphoreType.DMA], name=name,
                          compiler_params=_params(("arbitrary", "arbitrary")))(*srcs)


def _chip_scatter(pairs, small, name):
    n = len(pairs)

    def body(*refs):
        ins, small_ref = refs[:n], refs[n]
        outs, small_out = refs[n + 1:2 * n + 1], refs[2 * n + 1]
        ssem, rsem = refs[2 * n + 2:]
        x, y, c, me, chips = _place()
        copies = []
        for w in range(n + 1):
            for j, (px, py) in enumerate(chips):
                src = ins[w].at[2 * px + py] if w < n else small_ref
                dst = (outs[w] if w < n else small_out).at[j]
                cp = pltpu.make_async_remote_copy(src, dst, ssem.at[3 * w + j], rsem.at[3 * w + j],
                                                  device_id=(px, py, c), device_id_type=MESH)
                copies.append((cp, cp))
        _run_remote(copies)

    out_shape = tuple(jax.ShapeDtypeStruct((3,) + it.shape[1:], it.dtype) for it in pairs) + \
                (jax.ShapeDtypeStruct((3,) + small.shape, small.dtype),)
    return pl.pallas_call(body, out_shape=out_shape, in_specs=[ANY] * (n + 1), out_specs=tuple([ANY] * (n + 1)),
                          scratch_shapes=[pltpu.SemaphoreType.DMA((3 * n + 3,)),
                                          pltpu.SemaphoreType.DMA((3 * n + 3,))],
                          name=name)(*pairs, small)


def _pack(arrays):
    flat = jnp.concatenate([a.reshape(-1).astype(F32) for a in arrays])
    pad = (-flat.shape[0]) % (8 * LANES)
    return jnp.pad(flat, (0, pad)).reshape(-1, LANES)


def _unpack(packed, shapes):
    flat = packed.reshape(-1)
    out, off = [], 0
    for shp in shapes:
        size = math.prod(shp)
        out.append(flat[off:off + size].reshape(shp))
        off += size
    return out


def kernel(x, mem, g_mix, w_in, b_gate, a_ln_g, a_ln_b, a_ws, a_bs, b_conv, c_wg, c_scale, w_branch_a, w_branch_b, w_branch_c, w_o, g_xattn, g_mem, w_xq, w_xkv, w_xo, g_ffn, w_ffn_in, w_ffn_out, g_final, loss_target, m_g_mix, m_w_in, m_b_gate, m_a_ln_g, m_a_ln_b, m_a_ws, m_a_bs, m_b_conv, m_c_wg, m_c_scale, m_w_branch_a, m_w_branch_b, m_w_branch_c, m_w_o, m_g_xattn, m_g_mem, m_w_xq, m_w_xkv, m_w_xo, m_g_ffn, m_w_ffn_in, m_w_ffn_out, m_g_final, v_g_mix, v_w_in, v_b_gate, v_a_ln_g, v_a_ln_b, v_a_ws, v_a_bs, v_b_conv, v_c_wg, v_c_scale, v_w_branch_a, v_w_branch_b, v_w_branch_c, v_w_o, v_g_xattn, v_g_mem, v_w_xq, v_w_xkv, v_w_xo, v_g_ffn, v_w_ffn_in, v_w_ffn_out, v_g_final):
    args = locals()
    wts = {n: args[n] for n in WEIGHTS}
    mom = {n: args["m_" + n] for n in WEIGHTS}
    var = {n: args["v_" + n] for n in WEIGHTS}
    xs = x[0]
    mems = mem[0]
    tgt = loss_target[0]
    chip = 2 * lax.axis_index("x") + lax.axis_index("y")

    bias_pack = jnp.zeros((DEPTH, 8, 384), F32)
    bias_pack = bias_pack.at[:, 0:3, 0:256].set(b_gate).at[:, 0:3, 256:384].set(b_conv)
    chip_arr = jnp.reshape(chip, (1,)).astype(jnp.int32)
    core_arr = jnp.reshape(lax.axis_index("c"), (1,)).astype(jnp.int32)
    slots = [_cast_to_slot(wts[n], chip_arr, BF16, name=f"cast_{n}") for n in BIG]
    slots.append(_cast_to_slot(bias_pack, chip_arr, F32, name="cast_biases"))
    gathered = _chip_gather(slots, name="gather_chips")
    full = [_push_layer(g, core_arr, name=f"gather_cores_{n}")
            for g, n in zip(gathered, BIG + ("biases",))]
    fw = {}
    for n, g in zip(BIG, full[:-1]):
        if n in COL_SHARDED:
            fw[n] = jnp.transpose(g, (0, 2, 1, 3)).reshape(DEPTH, g.shape[2], N_CHIPS * g.shape[3])
        else:
            fw[n] = g.reshape(DEPTH, N_CHIPS * g.shape[2], g.shape[3])
    biases = full[-1]
    b_gate_full = jnp.transpose(biases[:, :, 0:3, 0:256], (0, 2, 1, 3)).reshape(DEPTH, 3, D_MODEL)
    b_conv_full = jnp.transpose(biases[:, :, 0:3, 256:384], (0, 2, 1, 3)).reshape(DEPTH, 3, B_WIDTH)

    tril = jnp.tril(jnp.ones((CHUNK, CHUNK), bool))

    def mixer_params(l):
        wtril = jnp.where(tril[None], a_ws[l], 0.0)
        prm = (a_ln_g[l][None], a_ln_b[l][None], wtril.astype(BF16),
               jnp.broadcast_to(a_bs[l][:, :, None], (A_GROUPS, CHUNK, CHUNK)), b_conv_full[l],
               c_wg[l].astype(BF16), c_scale[l][None])
        return prm, jnp.swapaxes(wtril, 1, 2).astype(BF16)

    saved = []
    xc = xs
    for l in range(DEPTH):
        sv = {"x0": xc}
        prm, _ = mixer_params(l)
        sv["h"], sv["proj"] = _norm_mm(xc, g_mix[l][None], fw["w_in"], l, tn=1536, name=f"l{l}_proj")
        sv["ypre"] = _mixer_fwd(sv["proj"], prm, name=f"l{l}_mixers")
        sv["merged"] = _merge_fwd(sv["ypre"], sv["proj"], fw["w_branch_a"][l], fw["w_branch_b"][l],
                                  fw["w_branch_c"][l], b_gate_full[l], name=f"l{l}_merge")
        xc = _mm(sv["merged"], fw["w_o"], b_lead=l, mode="nn", out_dtype=F32, name=f"l{l}_mix_out", tm=1024, tn=1024,
                 tk=1024, res=xc)
        sv["x1"] = xc
        sv["hq"], sv["q"] = _norm_mm(xc, g_xattn[l][None], fw["w_xq"], l, tn=1024, name=f"l{l}_q")
        sv["m"] = _rms_fwd(mems, g_mem[l][None], name=f"l{l}_mem_norm")
        sv["kv"] = _mm(sv["m"], fw["w_xkv"], b_lead=l, mode="nn", out_dtype=BF16, name=f"l{l}_kv", tm=256, tn=512,
                       tk=1024)
        sv["o"] = _attn_fwd(sv["q"], sv["kv"], name=f"l{l}_attn")
        xc = _mm(sv["o"], fw["w_xo"], b_lead=l, mode="nn", out_dtype=F32, name=f"l{l}_xattn_out", tm=1024, tn=1024,
                 tk=1024, res=xc)
        sv["x2"] = xc
        sv["h2"], sv["gu"], sv["act"] = _norm_ffn_in(xc, g_ffn[l][None], fw["w_ffn_in"], l, name=f"l{l}_ffn_in")
        xc = _mm(sv["act"], fw["w_ffn_out"], b_lead=l, mode="nn", out_dtype=F32, name=f"l{l}_ffn_out", tm=1024,
                 tn=512, tk=FFN_HIDDEN, res=xc)
        saved.append(sv)

    loss_part, dx, dg_final = _final_loss(xc, g_final[None], tgt, name="final_loss")
    loss = lax.psum(loss_part[0, 0], ("x", "y", "c"))

    big_grads = [None] * DEPTH
    small_grads = [None] * DEPTH
    for l in reversed(range(DEPTH)):
        sv = saved[l]
        prm, wtril_t = mixer_params(l)
        gb = {}
        sg = {}
        dgu = _d_act_swiglu(dx, fw["w_ffn_out"], l, sv["gu"], name=f"l{l}_d_act")
        gb["w_ffn_out"] = _mm(sv["act"], dx, mode="tn", out_dtype=BF16, name=f"l{l}_dw_ffn_out", tm=1408, tn=1024,
                              tk=1024).reshape(N_CHIPS, FFN_HIDDEN // N_CHIPS, D_MODEL)
        gb["w_ffn_in"] = _mm(sv["h2"], dgu, mode="tn", out_dtype=BF16, name=f"l{l}_dw_ffn_in", tm=1024, tn=FFN_TILE,
                             tk=1024, b_halves=True, out_shards=N_CHIPS)
        dx, sg["g_ffn"] = _mm_nt_norm_bwd(dgu, fw["w_ffn_in"], l, sv["x2"], g_ffn[l][None], dx, tk=FFN_TILE,
                                          a_halves=True, name=f"l{l}_d_h2")
        do = _mm(dx, fw["w_xo"], b_lead=l, mode="nt", out_dtype=BF16, name=f"l{l}_d_o", tm=1024, tn=1024, tk=1024)
        gb["w_xo"] = _mm(sv["o"], dx, mode="tn", out_dtype=BF16, name=f"l{l}_dw_xo", tm=1024, tn=1024,
                         tk=1024).reshape(N_CHIPS, D_MODEL // N_CHIPS, D_MODEL)
        dq, dkv = _attn_bwd(sv["q"], sv["kv"], do, name=f"l{l}_d_attn")
        gb["w_xq"] = _mm(sv["hq"], dq, mode="tn", out_dtype=BF16, name=f"l{l}_dw_xq", tm=1024, tn=1024,
                         tk=1024).reshape(N_CHIPS, D_MODEL // N_CHIPS, D_MODEL)
        dm = _mm(dkv, fw["w_xkv"], b_lead=l, mode="nt", out_dtype=BF16, name=f"l{l}_d_m", tm=256, tn=1024, tk=1024)
        gb["w_xkv"] = _mm(sv["m"], dkv, mode="tn", out_dtype=BF16, name=f"l{l}_dw_xkv", tm=1024, tn=512, tk=256,
                          out_shards=N_CHIPS)
        _, sg["g_mem"] = _rms_bwd(mems, g_mem[l][None], dm, None, name=f"l{l}_d_mem_norm")
        dx, sg["g_xattn"] = _mm_nt_norm_bwd(dq, fw["w_xq"], l, sv["x1"], g_xattn[l][None], dx, tk=1024,
                                            name=f"l{l}_d_hq")
        dmerged = _mm(dx, fw["w_o"], b_lead=l, mode="nt", out_dtype=BF16, name=f"l{l}_d_merged", tm=1024, tn=1024,
                      tk=1024)
        gb["w_o"] = _mm(sv["merged"], dx, mode="tn", out_dtype=BF16, name=f"l{l}_dw_o", tm=1024, tn=1024,
                        tk=1024).reshape(N_CHIPS, D_MODEL // N_CHIPS, D_MODEL)
        dypre, dproj, sg["b_gate"], dwa, dwb, dwc = _merge_bwd(sv["ypre"], sv["proj"], dmerged, fw["w_branch_a"][l],
                                                               fw["w_branch_b"][l], fw["w_branch_c"][l],
                                                               b_gate_full[l], name=f"l{l}_d_merge")
        for n, dw in zip(("w_branch_a", "w_branch_b", "w_branch_c"), (dwa, dwb, dwc)):
            gb[n] = jnp.transpose(dw.reshape(A_WIDTH, N_CHIPS, D_MODEL // N_CHIPS), (1, 0, 2)).astype(BF16)
        (dproj, sg["a_ln_g"], sg["a_ln_b"], sg["a_ws"], sg["a_bs"], sg["b_conv"], sg["c_wg"],
         sg["c_scale"]) = _mixer_bwd(sv["proj"], dypre, dproj, prm, wtril_t, name=f"l{l}_d_mixers")
        gb["w_in"] = _mm(sv["h"], dproj, mode="tn", out_dtype=BF16, name=f"l{l}_dw_in", tm=1024, tn=1536, tk=1024,
                         out_shards=N_CHIPS)
        dx, sg["g_mix"] = _mm_nt_norm_bwd(dproj, fw["w_in"], l, sv["x0"], g_mix[l][None], dx, tk=768,
                                          name=f"l{l}_d_h")
        big_grads[l] = [gb[n] for n in BIG]
        small_grads[l] = sg
    grad_x = dx[None]

    small_names = [n for n in WEIGHTS if n not in BIG]
    small_full_shapes = {n: ((DEPTH, 3, D_MODEL) if n == "b_gate" else (DEPTH, 3, B_WIDTH) if n == "b_conv"
                             else wts[n].shape) for n in small_names}
    small_local = []
    for n in small_names:
        if n == "g_final":
            small_local.append(dg_final)
        else:
            small_local.append(jnp.stack([small_grads[l][n].reshape(small_full_shapes[n][1:]) for l in range(DEPTH)]))
    small_pack = _pack(small_local)

    nb = len(BIG)

    def pair_sum(core, _, g0, g1, r):
        return (jnp.where(core == 0, g0, g1).astype(F32) + r.astype(F32),)

    pair = []
    for i, n in enumerate(BIG):
        rx = _push_sibling([big_grads[0][i], big_grads[1][i]], name=f"reduce_cores_{n}")
        pair.append(_elementwise(pair_sum, [big_grads[0][i], big_grads[1][i], rx], [BF16], name=f"pair_sum_{n}",
                                 place=core_arr)[0])
    small_rx = _push_sibling([small_pack[None]], name="reduce_cores_small")
    small_pair = _elementwise(lambda a, b: (a + b,), [small_pack[None], small_rx], [F32], name="pair_sum_small")[0][0]
    scattered = _chip_scatter(pair, small_pair, name="reduce_chips")
    layer_grads = [_chip_sum(pair[i], scattered[i], chip_arr, name=f"chip_sum_{BIG[i]}") for i in range(nb)]
    small_sum = _chip_sum_ordered(small_pair, scattered[nb], chip_arr, name="chip_sum_small")

    def adamw_layers(core, layer, w, own, other, m, v):
        g = jnp.where(layer == core, own, other)
        return (g,) + _adamw_math(w, g, m, v)

    out_g, out_d, out_m, out_v = {}, {}, {}, {}
    for i, n in enumerate(BIG):
        other = _push_sibling([layer_grads[i][None]], name=f"share_cores_{n}")[0]
        out_g[n], out_d[n], out_m[n], out_v[n] = _elementwise(
            adamw_layers, [wts[n], layer_grads[i], other, mom[n], var[n]], [F32, F32, F32, F32], name=f"adamw_{n}",
            place=core_arr, flat=(1, 2))
    small_g = dict(zip(small_names, _unpack(small_sum, [small_full_shapes[n] for n in small_names])))
    small_g["b_gate"] = lax.dynamic_slice_in_dim(small_g["b_gate"], chip * 256, 256, axis=2)
    small_g["b_conv"] = lax.dynamic_slice_in_dim(small_g["b_conv"], chip * 128, 128, axis=2)
    packs = [_pack([d[n] for n in small_names])[None] for d in (wts, small_g, mom, var)]
    upd = _elementwise(_adamw_math, packs, [F32, F32, F32], name="adamw_small")
    shapes = [wts[n].shape for n in small_names]
    for n, d, m_new, v_new in zip(small_names, *[_unpack(u[0], shapes) for u in upd]):
        out_g[n], out_d[n], out_m[n], out_v[n] = small_g[n], d, m_new, v_new

    return (loss, grad_x, *[out_g[n] for n in WEIGHTS], *[out_d[n] for n in WEIGHTS], *[out_m[n] for n in WEIGHTS],
            *[out_v[n] for n in WEIGHTS])
```

```python
import functools
import math

import jax
import jax.numpy as jnp
from jax import lax
from jax.experimental import pallas as pl
from jax.experimental.pallas import tpu as pltpu

F32 = jnp.float32
BF16 = jnp.bfloat16

D_MODEL = 1024
DEPTH = 2
MEM_LEN = 256
EPS = 1e-6
CHUNK = 128
A_GROUPS = 4
A_WIDTH = 512
B_WIDTH = 512
C_WIDTH = 512
C_GROUP = 128
POOL_WINDOWS = (2, 4, 8, 16)
HALO = 16
IN_COLS = 6144
MIX_COLS = 3072
XATTN_HEADS = 4
HEAD_DIM = 256
FFN_HIDDEN = 2816
N_CHIPS = 4

ADAM_LR = 0.001
ADAM_B1 = 0.9
ADAM_B2 = 0.999
ADAM_EPS = 1e-08
ADAM_WD = 0.01
ADAM_STEP = 10

V7X_VMEM_BYTES = 64 * 1024 * 1024
VMEM_LIMIT = (V7X_VMEM_BYTES * 3) // 4
LANES = 128
MESH = pl.DeviceIdType.MESH
ANY = pl.BlockSpec(memory_space=pl.ANY)

BIG = ("w_in", "w_branch_a", "w_branch_b", "w_branch_c", "w_o", "w_xq", "w_xkv", "w_xo", "w_ffn_in", "w_ffn_out")
COL_SHARDED = ("w_in", "w_branch_a", "w_branch_b", "w_branch_c", "w_xkv", "w_ffn_in")
SMALL_REPL = ("g_mix", "a_ln_g", "a_ln_b", "a_ws", "a_bs", "c_wg", "c_scale", "g_xattn", "g_mem", "g_ffn", "g_final")
SMALL_SHARDED = ("b_gate", "b_conv")
WEIGHTS = ("g_mix", "w_in", "b_gate", "a_ln_g", "a_ln_b", "a_ws", "a_bs", "b_conv", "c_wg", "c_scale", "w_branch_a",
           "w_branch_b", "w_branch_c", "w_o", "g_xattn", "g_mem", "w_xq", "w_xkv", "w_xo", "g_ffn", "w_ffn_in",
           "w_ffn_out", "g_final")


def _params(sem=None):
    return pltpu.CompilerParams(dimension_semantics=sem, vmem_limit_bytes=VMEM_LIMIT)


def _blk(dim, pref):
    return pref if dim % pref == 0 else dim


def _mm(a, b, *, mode, out_dtype, name, tm, tn, tk, res=None, b_lead=None, b_halves=False, out_shards=None):
    dims = {"nn": (((1,), (0,)), ((), ())), "nt": (((1,), (1,)), ((), ())), "tn": (((0,), (0,)), ((), ()))}[mode]
    b_rows, b_last = b.shape[-2], b.shape[-1]
    if mode == "nn":
        m, k, n = a.shape[0], a.shape[1], b_last
        assert b_rows == k
    elif mode == "nt":
        m, k, n = a.shape[0], a.shape[1], b_rows
        assert b_last == k
    else:
        k, m, n = a.shape[0], a.shape[1], (2 * b_last if b_halves else b_last)
        assert b_rows == k
    tm, tn, tk = _blk(m, tm), _blk(n, tn), _blk(k, tk)
    nk = k // tk
    grid = (m // tm, n // tn, nk)

    if mode == "tn":
        a_spec = pl.BlockSpec((tk, tm), lambda i, j, kk: (kk, i))
        b_block, b_idx = (tk, tn), (lambda i, j, kk: (kk, j))
    else:
        a_spec = pl.BlockSpec((tm, tk), lambda i, j, kk: (i, kk))
        if mode == "nn":
            b_block, b_idx = (tk, tn), (lambda i, j, kk: (kk, j))
        else:
            b_block, b_idx = (tn, tk), (lambda i, j, kk: (j, kk))
    if b_halves:
        assert mode == "tn" and b_lead is None and b_last % tn == 0
        per_half = b_last // tn
        b_spec = pl.BlockSpec((None,) + b_block, lambda i, j, kk: (j // per_half, kk, j % per_half))
    elif b_lead is None:
        b_spec = pl.BlockSpec(b_block, b_idx)
    else:
        b_spec = pl.BlockSpec((None,) + b_block, lambda i, j, kk: (b_lead,) + b_idx(i, j, kk))
    in_specs = [a_spec, b_spec]
    operands = [a, b]
    if res is not None:
        in_specs.append(pl.BlockSpec((tm, tn), lambda i, j, kk: (i, j)))
        operands.append(res)
    if out_shards is None:
        out_shape = jax.ShapeDtypeStruct((m, n), out_dtype)
        out_spec = pl.BlockSpec((tm, tn), lambda i, j, kk: (i, j))
    else:
        per = n // out_shards
        assert per % tn == 0
        nps = per // tn
        out_shape = jax.ShapeDtypeStruct((out_shards, m, per), out_dtype)
        out_spec = pl.BlockSpec((None, tm, tn), lambda i, j, kk: (j // nps, i, j % nps))

    def body(*refs):
        a_ref, b_ref = refs[0], refs[1]
        res_ref = refs[2] if res is not None else None
        o_ref = refs[3] if res is not None else refs[2]
        part = lax.dot_general(a_ref[...].astype(BF16), b_ref[...].astype(BF16), dims, preferred_element_type=F32)

        def finish(acc):
            if res_ref is not None:
                acc = acc + res_ref[...]
            o_ref[...] = acc.astype(out_dtype)

        if nk == 1:
            finish(part)
        else:
            acc_ref = refs[-1]
            kk = pl.program_id(2)

            @pl.when(kk == 0)
            def _():
                acc_ref[...] = part

            @pl.when(kk > 0)
            def _():
                acc_ref[...] += part

            @pl.when(kk == nk - 1)
            def _():
                finish(acc_ref[...])

    scratch = [] if nk == 1 else [pltpu.VMEM((tm, tn), F32)]
    return pl.pallas_call(body, out_shape=out_shape, grid=grid, in_specs=in_specs, out_specs=out_spec,
                          scratch_shapes=scratch, name=name,
                          compiler_params=_params(("parallel", "parallel", "arbitrary")))(*operands)


ROW_CHUNK = 256
FFN_TILE = 1408
TOKEN_STEP = 2048
NT_DIMS = (((1,), (1,)), ((), ()))


def _norm_rows_into(x_ref, g_ref, h_ref, hs_ref, tm):
    rc = min(ROW_CHUNK, tm)
    for r0 in range(0, tm, rc):
        xv = x_ref[r0:r0 + rc, :]
        r = lax.rsqrt(jnp.mean(xv * xv, axis=-1, keepdims=True) + EPS)
        hv = (xv * r * g_ref[...]).astype(BF16)
        hs_ref[r0:r0 + rc, :] = hv
        h_ref[r0:r0 + rc, :] = hv


def _norm_mm(x, g, b, b_lead, *, tn, name):
    s, d = x.shape
    n = b.shape[-1]
    tm, tn = _blk(s, 1024), _blk(n, tn)

    def body(x_ref, g_ref, b_ref, h_ref, o_ref, hs_ref):
        @pl.when(pl.program_id(1) == 0)
        def _():
            _norm_rows_into(x_ref, g_ref, h_ref, hs_ref, tm)

        o_ref[...] = jnp.dot(hs_ref[...], b_ref[...], preferred_element_type=F32).astype(BF16)

    row = pl.BlockSpec((tm, d), lambda i, j: (i, 0))
    return pl.pallas_call(body, out_shape=(jax.ShapeDtypeStruct((s, d), BF16), jax.ShapeDtypeStruct((s, n), BF16)),
                          grid=(s // tm, n // tn),
                          in_specs=[row, pl.BlockSpec((1, d), lambda i, j: (0, 0)),
                                    pl.BlockSpec((None, d, tn), lambda i, j: (b_lead, 0, j))],
                          out_specs=(row, pl.BlockSpec((tm, tn), lambda i, j: (i, j))),
                          scratch_shapes=[pltpu.VMEM((tm, d), BF16)], name=name,
                          compiler_params=_params(("parallel", "arbitrary")))(x, g, b)


def _norm_ffn_in(x, g, b, b_lead, name):
    s, d = x.shape
    tm, tn = _blk(s, 512), FFN_TILE
    nj = FFN_HIDDEN // tn

    def body(x_ref, g_ref, bg_ref, bu_ref, h_ref, gu_ref, act_ref, hs_ref):
        @pl.when(pl.program_id(1) == 0)
        def _():
            _norm_rows_into(x_ref, g_ref, h_ref, hs_ref, tm)

        hv = hs_ref[...]
        gate = jnp.dot(hv, bg_ref[...], preferred_element_type=F32)
        up = jnp.dot(hv, bu_ref[...], preferred_element_type=F32)
        gu_ref[0] = gate.astype(BF16)
        gu_ref[1] = up.astype(BF16)
        act_ref[...] = (gate * _sigmoid(gate) * up).astype(BF16)

    row = pl.BlockSpec((tm, d), lambda i, j: (i, 0))
    out_shape = (jax.ShapeDtypeStruct((s, d), BF16), jax.ShapeDtypeStruct((2, s, FFN_HIDDEN), BF16),
                 jax.ShapeDtypeStruct((s, FFN_HIDDEN), BF16))
    return pl.pallas_call(body, out_shape=out_shape, grid=(s // tm, nj),
                          in_specs=[row, pl.BlockSpec((1, d), lambda i, j: (0, 0)),
                                    pl.BlockSpec((None, d, tn), lambda i, j: (b_lead, 0, j)),
                                    pl.BlockSpec((None, d, tn), lambda i, j: (b_lead, 0, j + nj))],
                          out_specs=(row, pl.BlockSpec((2, tm, tn), lambda i, j: (0, i, j)),
                                     pl.BlockSpec((tm, tn), lambda i, j: (i, j))),
                          scratch_shapes=[pltpu.VMEM((tm, d), BF16)], name=name,
                          compiler_params=_params(("parallel", "arbitrary")))(x, g, b, b)


def _d_act_swiglu(dx, w, b_lead, gu, name):
    s, d = dx.shape
    tm, tn = _blk(s, 512), FFN_TILE

    def body(dx_ref, w_ref, gu_ref, o_ref):
        dact = lax.dot_general(dx_ref[...].astype(BF16), w_ref[...], NT_DIMS, preferred_element_type=F32)
        gate = gu_ref[0].astype(F32)
        up = gu_ref[1].astype(F32)
        sg = _sigmoid(gate)
        o_ref[0] = (dact * up * sg * (1.0 + gate * (1.0 - sg))).astype(BF16)
        o_ref[1] = (dact * gate * sg).astype(BF16)

    halves = pl.BlockSpec((2, tm, tn), lambda j, i: (0, i, j))
    return pl.pallas_call(body, out_shape=jax.ShapeDtypeStruct(gu.shape, BF16), grid=(FFN_HIDDEN // tn, s // tm),
                          in_specs=[pl.BlockSpec((tm, d), lambda j, i: (i, 0)),
                                    pl.BlockSpec((None, tn, d), lambda j, i: (b_lead, j, 0)), halves],
                          out_specs=halves, name=name,
                          compiler_params=_params(("parallel", "parallel")))(dx, w, gu)


def _mm_nt_norm_bwd(a, b, b_lead, x, g, dres, *, tm, name, a_halves=False):
    s, d = x.shape
    kdim = b.shape[-1]
    tm = _blk(s, tm)
    rc = min(ROW_CHUNK, tm)
    if a_halves:
        a_spec = pl.BlockSpec((2, tm, kdim // 2), lambda i: (0, i, 0))
    else:
        a_spec = pl.BlockSpec((tm, kdim), lambda i: (i, 0))

    def body(a_ref, b_ref, x_ref, g_ref, r_ref, dx_ref, dg_ref):
        i = pl.program_id(0)
        av = jnp.concatenate([a_ref[0], a_ref[1]], axis=1) if a_halves else a_ref[...]
        dh = lax.dot_general(av.astype(BF16), b_ref[...], NT_DIMS, preferred_element_type=F32)
        gv = g_ref[...]
        dg_part = None
        for r0 in range(0, tm, rc):
            xv = x_ref[r0:r0 + rc, :]
            dhv = dh[r0:r0 + rc, :]
            r = lax.rsqrt(jnp.mean(xv * xv, axis=-1, keepdims=True) + EPS)
            xhat = xv * r
            p = jnp.sum(dhv * xhat, axis=0, keepdims=True)
            dg_part = p if dg_part is None else dg_part + p
            dxhat = dhv * gv
            dx_ref[r0:r0 + rc, :] = r_ref[r0:r0 + rc, :] + r * (
                dxhat - xhat * jnp.mean(dxhat * xhat, axis=-1, keepdims=True))

        @pl.when(i == 0)
        def _():
            dg_ref[...] = dg_part

        @pl.when(i > 0)
        def _():
            dg_ref[...] += dg_part

    row = pl.BlockSpec((tm, d), lambda i: (i, 0))
    vec = pl.BlockSpec((1, d), lambda i: (0, 0))
    b_spec = pl.BlockSpec((None, d, kdim), lambda i: (b_lead, 0, 0), pipeline_mode=pl.Buffered(1))
    return pl.pallas_call(body, out_shape=(jax.ShapeDtypeStruct((s, d), F32), jax.ShapeDtypeStruct((1, d), F32)),
                          grid=(s // tm,), in_specs=[a_spec, b_spec, row, vec, row], out_specs=(row, vec), name=name,
                          compiler_params=_params(("arbitrary",)))(a, b, x, g, dres)


def _rms_fwd(x, g, name):
    s, d = x.shape
    tm = _blk(s, 512)

    def body(x_ref, g_ref, o_ref):
        xv = x_ref[...]
        r = lax.rsqrt(jnp.mean(xv * xv, axis=-1, keepdims=True) + EPS)
        o_ref[...] = (xv * r * g_ref[...]).astype(BF16)

    return pl.pallas_call(body, out_shape=jax.ShapeDtypeStruct((s, d), BF16), grid=(s // tm,),
                          in_specs=[pl.BlockSpec((tm, d), lambda i: (i, 0)), pl.BlockSpec((1, d), lambda i: (0, 0))],
                          out_specs=pl.BlockSpec((tm, d), lambda i: (i, 0)), name=name,
                          compiler_params=_params(("parallel",)))(x, g)


def _rms_bwd(x, g, dh, dres, name):
    s, d = x.shape
    tm = _blk(s, 512)
    has_res = dres is not None

    def body(*refs):
        x_ref, g_ref, dh_ref = refs[0], refs[1], refs[2]
        dx_ref, dg_ref = refs[-2], refs[-1]
        xv = x_ref[...]
        r = lax.rsqrt(jnp.mean(xv * xv, axis=-1, keepdims=True) + EPS)
        xhat = xv * r
        dhv = dh_ref[...].astype(F32)
        part = jnp.sum(dhv * xhat, axis=0, keepdims=True)

        @pl.when(pl.program_id(0) == 0)
        def _():
            dg_ref[...] = part

        @pl.when(pl.program_id(0) > 0)
        def _():
            dg_ref[...] += part

        dxhat = dhv * g_ref[...]
        dx = r * (dxhat - xhat * jnp.mean(dxhat * xhat, axis=-1, keepdims=True))
        if has_res:
            dx = dx + refs[3][...]
        dx_ref[...] = dx

    row = pl.BlockSpec((tm, d), lambda i: (i, 0))
    vec = pl.BlockSpec((1, d), lambda i: (0, 0))
    in_specs = [row, vec, row] + ([row] if has_res else [])
    operands = [x, g, dh] + ([dres] if has_res else [])
    return pl.pallas_call(body, out_shape=(jax.ShapeDtypeStruct((s, d), F32), jax.ShapeDtypeStruct((1, d), F32)),
                          grid=(s // tm,), in_specs=in_specs, out_specs=(row, vec), name=name,
                          compiler_params=_params(("arbitrary",)))(*operands)


def _final_loss(x, g, target, name):
    s, d = x.shape
    tm = _blk(s, 512)

    def body(x_ref, g_ref, t_ref, loss_ref, dx_ref, dg_ref):
        xv = x_ref[...]
        gv = g_ref[...]
        r = lax.rsqrt(jnp.mean(xv * xv, axis=-1, keepdims=True) + EPS)
        xhat = xv * r
        err = xhat * gv - t_ref[...]
        lpart = 0.5 * jnp.sum(jnp.mean(err * err, axis=-1, keepdims=True), axis=0, keepdims=True)
        dy = err * (1.0 / d)
        gpart = jnp.sum(dy * xhat, axis=0, keepdims=True)

        @pl.when(pl.program_id(0) == 0)
        def _():
            loss_ref[...] = lpart
            dg_ref[...] = gpart

        @pl.when(pl.program_id(0) > 0)
        def _():
            loss_ref[...] += lpart
            dg_ref[...] += gpart

        dxhat = dy * gv
        dx_ref[...] = r * (dxhat - xhat * jnp.mean(dxhat * xhat, axis=-1, keepdims=True))

    row = pl.BlockSpec((tm, d), lambda i: (i, 0))
    vec = pl.BlockSpec((1, d), lambda i: (0, 0))
    one = pl.BlockSpec((1, 1), lambda i: (0, 0))
    return pl.pallas_call(body, out_shape=(jax.ShapeDtypeStruct((1, 1), F32), jax.ShapeDtypeStruct((s, d), F32),
                                           jax.ShapeDtypeStruct((1, d), F32)),
                          grid=(s // tm,), in_specs=[row, vec, row], out_specs=(one, row, vec), name=name,
                          compiler_params=_params(("arbitrary",)))(x, g, target)


def _erf_parts(x):
    cdf = 0.5 * (1.0 + lax.erf(x * (1.0 / math.sqrt(2.0))))
    return cdf


def _shift_down(ext, k):
    return pltpu.roll(ext, k, 0)


def _shift_up(ext, k):
    return pltpu.roll(ext, ext.shape[0] - k, 0)


def _mixer_forward_math(z, halo, row0, prm, tm):
    ln_g, ln_b, wtril, bsb, conv_w, wg, scale = prm
    za = z[:, 0:2 * A_WIDTH]
    gb = z[:, 1024:1536]
    gc = z[:, 1536:2048]
    xin = z[:, 2048:2560]
    zc = z[:, 2560:3072]
    cdf = _erf_parts(za)
    act = za * cdf
    u = act[:, :A_WIDTH]
    v = act[:, A_WIDTH:]
    mu = jnp.mean(v, axis=-1, keepdims=True)
    vc = v - mu
    rstd = lax.rsqrt(jnp.mean(vc * vc, axis=-1, keepdims=True) + EPS)
    vhat = vc * rstd
    vn = (vhat * ln_g + ln_b).astype(BF16)
    rows = []
    for c in range(tm // CHUNK):
        cols = []
        for g in range(A_GROUPS):
            blk = vn[c * CHUNK:(c + 1) * CHUNK, g * CHUNK:(g + 1) * CHUNK]
            cols.append(jnp.dot(wtril[g], blk, preferred_element_type=F32) + bsb[g])
        rows.append(jnp.concatenate(cols, axis=1))
    mixed = jnp.concatenate(rows, axis=0)
    ya = u * mixed
    y = gc * xin
    yext = jnp.concatenate([halo[:, 0:512] * halo[:, 512:1024], y], axis=0)
    y1 = _shift_down(yext, 1)[HALO:]
    y2 = _shift_down(yext, 2)[HALO:]
    conv = conv_w[0:1, :] * y2 + conv_w[1:2, :] * y1 + conv_w[2:3, :] * y
    yb = gb * conv
    t = row0 + lax.broadcasted_iota(jnp.int32, (tm, 1), 0)
    zext = jnp.concatenate([halo[:, 1024:1536], zc], axis=0)
    pooled, p = [], []
    for gi, win in enumerate(POOL_WINDOWS):
        sw = zext[:, gi * C_GROUP:(gi + 1) * C_GROUP]
        for step in range(gi + 1):
            sw = sw + _shift_down(sw, 2 ** step)
        cnt = jnp.minimum(t + 1, win).astype(F32)
        pg = sw[HALO:] / cnt - zc[:, gi * C_GROUP:(gi + 1) * C_GROUP]
        pooled.append(pg)
        p.append(jnp.dot(pg.astype(BF16), wg[gi], preferred_element_type=F32))
    p = jnp.concatenate(p, axis=1)
    yc = p * scale
    saved = dict(za=za, cdf=cdf, u=u, mixed=mixed, rstd=rstd, vhat=vhat, vn=vn, gb=gb, gc=gc, xin=xin, y=y, y1=y1, y2=y2,
                 conv=conv, pooled=pooled, p=p, t=t)
    return ya, yb, yc, saved


def _mixer_specs(tm, s):
    nb16 = tm // HALO
    main = pl.BlockSpec((tm, MIX_COLS), lambda i: (i, 0))
    prev = pl.BlockSpec((HALO, 1536), lambda i: (jnp.maximum(i * nb16 - 1, 0), 1))
    return main, prev


def _full(shape):
    n = len(shape)
    return pl.BlockSpec(shape, lambda i: (0,) * n)


def _mixer_fwd(proj, prm, name):
    s = proj.shape[0]
    tm = _blk(s, 256)
    main, prev = _mixer_specs(tm, s)

    def body(z_ref, h_ref, lng, lnb, wt, bsb, cw, wg, sc, o_ref):
        i = pl.program_id(0)
        halo = jnp.where(i > 0, h_ref[...].astype(F32), 0.0)
        prm_v = (lng[...], lnb[...], wt[...], bsb[...], cw[...], wg[...], sc[...])
        ya, yb, yc, _ = _mixer_forward_math(z_ref[...].astype(F32), halo, i * tm, prm_v, tm)
        o_ref[...] = jnp.concatenate([ya, yb, yc], axis=1).astype(BF16)

    return pl.pallas_call(body, out_shape=jax.ShapeDtypeStruct((s, 1536), BF16), grid=(s // tm,),
                          in_specs=[main, prev] + [_full(p.shape) for p in prm],
                          out_specs=pl.BlockSpec((tm, 1536), lambda i: (i, 0)), name=name,
                          compiler_params=_params(("parallel",)))(proj, proj, *prm)


def _mixer_bwd(proj, dypre, dproj_in, prm, wtril_t, name):
    s = proj.shape[0]
    tm = _blk(s, 256)
    nblk = s // tm
    nb16 = tm // HALO
    main, prev = _mixer_specs(tm, s)
    nxt_row = lambda i: jnp.minimum((i + 1) * nb16, s // HALO - 1)
    next_dy = pl.BlockSpec((HALO, 1536), lambda i: (nxt_row(i), 0))
    next_gb = pl.BlockSpec((HALO, 512), lambda i: (nxt_row(i), 2))
    ones8 = jnp.ones((8, CHUNK), F32)

    def body(z_ref, h_ref, dy_ref, ndy_ref, ngb_ref, _alias, lng, lnb, wt, bsb, cw, wg, sc, wtt, ones_ref,
             dz_ref, dlng_ref, dlnb_ref, dws_ref, dbs_ref, dcw_ref, dwg_ref, dsc_ref):
        i = pl.program_id(0)
        first = i == 0
        halo = jnp.where(i > 0, h_ref[...].astype(F32), 0.0)
        prm_v = (lng[...], lnb[...], wt[...], bsb[...], cw[...], wg[...], sc[...])
        _, _, _, sv = _mixer_forward_math(z_ref[...].astype(F32), halo, i * tm, prm_v, tm)
        dy = dy_ref[...].astype(F32)
        dya, dyb, dyc = dy[:, 0:512], dy[:, 512:1024], dy[:, 1024:1536]
        not_last = i < nblk - 1
        ndy = jnp.where(not_last, ndy_ref[...].astype(F32), 0.0)
        ngb = ngb_ref[...].astype(F32)

        def accumulate(ref, val):
            @pl.when(first)
            def _():
                ref[...] = val

            @pl.when(jnp.logical_not(first))
            def _():
                ref[...] += val

        du = dya * sv["mixed"]
        dmixed = dya * sv["u"]
        vn = sv["vn"]
        wttv = wtt[...]
        dvn_rows = []
        dws_parts = [None] * A_GROUPS
        dbs_parts = [None] * A_GROUPS
        for c in range(tm // CHUNK):
            cols = []
            for g in range(A_GROUPS):
                dm = dmixed[c * CHUNK:(c + 1) * CHUNK, g * CHUNK:(g + 1) * CHUNK]
                dmb = dm.astype(BF16)
                vb = vn[c * CHUNK:(c + 1) * CHUNK, g * CHUNK:(g + 1) * CHUNK]
                w_part = lax.dot_general(dmb, vb, (((1,), (1,)), ((), ())), preferred_element_type=F32)
                b_part = lax.dot_general(ones_ref[...], dm, (((1,), (1,)), ((), ())), preferred_element_type=F32,
                                         precision=lax.Precision.HIGHEST)[0:1, :]
                dws_parts[g] = w_part if dws_parts[g] is None else dws_parts[g] + w_part
                dbs_parts[g] = b_part if dbs_parts[g] is None else dbs_parts[g] + b_part
                cols.append(jnp.dot(wttv[g], dmb, preferred_element_type=F32))
            dvn_rows.append(jnp.concatenate(cols, axis=1))
        dvn = jnp.concatenate(dvn_rows, axis=0)
        tri = lax.broadcasted_iota(jnp.int32, (CHUNK, CHUNK), 0) >= lax.broadcasted_iota(jnp.int32, (CHUNK, CHUNK), 1)
        accumulate(dws_ref, jnp.stack([jnp.where(tri, w, 0.0) for w in dws_parts], axis=0))
        accumulate(dbs_ref, jnp.concatenate(dbs_parts, axis=0))
        vhat = sv["vhat"]
        accumulate(dlng_ref, jnp.sum(dvn * vhat, axis=0, keepdims=True))
        accumulate(dlnb_ref, jnp.sum(dvn, axis=0, keepdims=True))
        dvhat = dvn * lng[...]
        dv = sv["rstd"] * (dvhat - jnp.mean(dvhat, axis=-1, keepdims=True)
                           - vhat * jnp.mean(dvhat * vhat, axis=-1, keepdims=True))
        za = sv["za"]
        dgelu = sv["cdf"] + za * jnp.exp(-0.5 * za * za) * (1.0 / math.sqrt(2.0 * math.pi))
        dza = jnp.concatenate([du, dv], axis=1) * dgelu
        cwv = cw[...]
        dconv = dyb * sv["gb"]
        dgb = dyb * sv["conv"]
        dcext = jnp.concatenate([dconv, ndy[:, 512:1024] * ngb], axis=0)
        d1 = _shift_up(dcext, 1)[:tm]
        d2 = _shift_up(dcext, 2)[:tm]
        dyy = cwv[2:3, :] * dconv + cwv[1:2, :] * d1 + cwv[0:1, :] * d2
        dgc = dyy * sv["xin"]
        dxin = dyy * sv["gc"]
        accumulate(dcw_ref, jnp.concatenate([jnp.sum(dconv * sv["y2"], axis=0, keepdims=True),
                                             jnp.sum(dconv * sv["y1"], axis=0, keepdims=True),
                                             jnp.sum(dconv * sv["y"], axis=0, keepdims=True)], axis=0))
        scv = sc[...]
        accumulate(dsc_ref, jnp.sum(dyc * sv["p"], axis=0, keepdims=True))
        dp = dyc * scv
        ndp = ndy[:, 1024:1536] * scv
        wgv = wg[...]
        dzc_cols, dwg_parts = [], []
        for gi, win in enumerate(POOL_WINDOWS):
            sl = slice(gi * C_GROUP, (gi + 1) * C_GROUP)
            dpb = dp[:, sl].astype(BF16)
            dpool = lax.dot_general(dpb, wgv[gi], (((1,), (1,)), ((), ())), preferred_element_type=F32)
            ndpool = lax.dot_general(ndp[:, sl].astype(BF16), wgv[gi], (((1,), (1,)), ((), ())),
                                     preferred_element_type=F32)
            dwg_parts.append(lax.dot_general(sv["pooled"][gi].astype(BF16), dpb, (((0,), (0,)), ((), ())),
                                             preferred_element_type=F32))
            cnt = jnp.minimum(sv["t"] + 1, win).astype(F32)
            fw = jnp.concatenate([dpool / cnt, ndpool * (1.0 / win)], axis=0)
            for step in range(gi + 1):
                fw = fw + _shift_up(fw, 2 ** step)
            dzc_cols.append(fw[:tm] - dpool)
        accumulate(dwg_ref, jnp.stack(dwg_parts, axis=0))
        dzc = jnp.concatenate(dzc_cols, axis=1)
        dz_ref[...] = jnp.concatenate([dza, dgb, dgc, dxin, dzc], axis=1).astype(BF16)

    out_shape = (jax.ShapeDtypeStruct(dproj_in.shape, BF16), jax.ShapeDtypeStruct((1, A_WIDTH), F32),
                 jax.ShapeDtypeStruct((1, A_WIDTH), F32), jax.ShapeDtypeStruct((A_GROUPS, CHUNK, CHUNK), F32),
                 jax.ShapeDtypeStruct((A_GROUPS, CHUNK), F32), jax.ShapeDtypeStruct((3, B_WIDTH), F32),
                 jax.ShapeDtypeStruct((4, C_GROUP, C_GROUP), F32), jax.ShapeDtypeStruct((1, C_WIDTH), F32))
    out_specs = (main,) + tuple(_full(o.shape) for o in out_shape[1:])
    in_specs = [main, prev, pl.BlockSpec((tm, 1536), lambda i: (i, 0)), next_dy, next_gb, ANY] + \
               [_full(p.shape) for p in prm] + [_full(wtril_t.shape), _full(ones8.shape)]
    return pl.pallas_call(body, out_shape=out_shape, grid=(nblk,), in_specs=in_specs, out_specs=out_specs,
                          input_output_aliases={5: 0}, name=name,
                          compiler_params=_params(("arbitrary",)))(proj, proj, dypre, dypre, proj, dproj_in, *prm,
                                                                  wtril_t, ones8)


def _sigmoid(x):
    return 1.0 / (1.0 + jnp.exp(-x))


def _merge_fwd(ypre, proj, wa, wb, wc, bgate, name):
    s = ypre.shape[0]
    tm = _blk(s, 512)

    def body(y_ref, zg_ref, wa_ref, wb_ref, wc_ref, bg_ref, o_ref):
        yv = y_ref[...]
        acc = None
        for i, w_ref in enumerate((wa_ref, wb_ref, wc_ref)):
            br = jnp.dot(yv[:, i * 512:(i + 1) * 512], w_ref[...], preferred_element_type=F32)
            gate = _sigmoid(zg_ref[:, i * D_MODEL:(i + 1) * D_MODEL].astype(F32) + bg_ref[i:i + 1, :])
            acc = gate * br if acc is None else acc + gate * br
        o_ref[...] = acc.astype(BF16)

    wspec = _full(wa.shape)
    return pl.pallas_call(body, out_shape=jax.ShapeDtypeStruct((s, D_MODEL), BF16), grid=(s // tm,),
                          in_specs=[pl.BlockSpec((tm, 1536), lambda i: (i, 0)),
                                    pl.BlockSpec((tm, 3 * D_MODEL), lambda i: (i, 1)), wspec, wspec, wspec,
                                    _full(bgate.shape)],
                          out_specs=pl.BlockSpec((tm, D_MODEL), lambda i: (i, 0)), name=name,
                          compiler_params=_params(("parallel",)))(ypre, proj, wa, wb, wc, bgate)


def _merge_bwd(ypre, proj, dmerged, wa, wb, wc, bgate, name):
    s = ypre.shape[0]
    tm = _blk(s, 512)

    def body(y_ref, zg_ref, dm_ref, wa_ref, wb_ref, wc_ref, bg_ref, dyp_ref, dzg_ref, dbg_ref, dwa_ref, dwb_ref,
             dwc_ref):
        first = pl.program_id(0) == 0

        def accumulate(ref, val):
            @pl.when(first)
            def _():
                ref[...] = val

            @pl.when(jnp.logical_not(first))
            def _():
                ref[...] += val

        yv = y_ref[...]
        dm = dm_ref[...].astype(F32)
        dyp, dzg, dbg = [], [], []
        for i, (w_ref, dw_ref) in enumerate(((wa_ref, dwa_ref), (wb_ref, dwb_ref), (wc_ref, dwc_ref))):
            wv = w_ref[...]
            yi = yv[:, i * 512:(i + 1) * 512]
            br = jnp.dot(yi, wv, preferred_element_type=F32)
            gate = _sigmoid(zg_ref[:, i * D_MODEL:(i + 1) * D_MODEL].astype(F32) + bg_ref[i:i + 1, :])
            dbi = (dm * gate).astype(BF16)
            dzi = dm * br * gate * (1.0 - gate)
            dzg.append(dzi.astype(BF16))
            dbg.append(jnp.sum(dzi, axis=0, keepdims=True))
            dyp.append(lax.dot_general(dbi, wv, (((1,), (1,)), ((), ())), preferred_element_type=F32).astype(BF16))
            accumulate(dw_ref, lax.dot_general(yi, dbi, (((0,), (0,)), ((), ())), preferred_element_type=F32))
        dyp_ref[...] = jnp.concatenate(dyp, axis=1)
        dzg_ref[...] = jnp.concatenate(dzg, axis=1)
        accumulate(dbg_ref, jnp.concatenate(dbg, axis=0))

    wspec = _full(wa.shape)
    dw_shape = jax.ShapeDtypeStruct(wa.shape, F32)
    out_shape = (jax.ShapeDtypeStruct((s, 1536), BF16), jax.ShapeDtypeStruct((s, IN_COLS), BF16),
                 jax.ShapeDtypeStruct((3, D_MODEL), F32), dw_shape, dw_shape, dw_shape)
    return pl.pallas_call(body, out_shape=out_shape, grid=(s // tm,),
                          in_specs=[pl.BlockSpec((tm, 1536), lambda i: (i, 0)),
                                    pl.BlockSpec((tm, 3 * D_MODEL), lambda i: (i, 1)),
                                    pl.BlockSpec((tm, D_MODEL), lambda i: (i, 0)), wspec, wspec, wspec,
                                    _full(bgate.shape)],
                          out_specs=(pl.BlockSpec((tm, 1536), lambda i: (i, 0)),
                                     pl.BlockSpec((tm, 3 * D_MODEL), lambda i: (i, 1)), _full((3, D_MODEL)),
                                     wspec, wspec, wspec),
                          name=name, compiler_params=_params(("arbitrary",)))(ypre, proj, dmerged, wa, wb, wc, bgate)


def _softmax_rows(q, k):
    sc = lax.dot_general(q, k, (((1,), (1,)), ((), ())), preferred_element_type=F32) * (HEAD_DIM ** -0.5)
    e = jnp.exp(sc - jnp.max(sc, axis=-1, keepdims=True))
    return e / jnp.sum(e, axis=-1, keepdims=True)


def _attn_fwd(q, kv, name):
    s = q.shape[0]
    tm = _blk(s, 512)

    def body(q_ref, kv_ref, o_ref):
        outs = []
        for h in range(XATTN_HEADS):
            sl = slice(h * HEAD_DIM, (h + 1) * HEAD_DIM)
            p = _softmax_rows(q_ref[:, sl], kv_ref[:, sl])
            outs.append(jnp.dot(p.astype(BF16), kv_ref[:, D_MODEL + h * HEAD_DIM:D_MODEL + (h + 1) * HEAD_DIM],
                                preferred_element_type=F32))
        o_ref[...] = jnp.concatenate(outs, axis=1).astype(BF16)

    return pl.pallas_call(body, out_shape=jax.ShapeDtypeStruct((s, D_MODEL), BF16), grid=(s // tm,),
                          in_specs=[pl.BlockSpec((tm, D_MODEL), lambda i: (i, 0)), _full(kv.shape)],
                          out_specs=pl.BlockSpec((tm, D_MODEL), lambda i: (i, 0)), name=name,
                          compiler_params=_params(("parallel",)))(q, kv)


def _attn_bwd(q, kv, do, name):
    s = q.shape[0]
    tm = _blk(s, 512)

    def body(q_ref, kv_ref, do_ref, dq_ref, dkv_ref):
        dqs, dks, dvs = [], [], []
        for h in range(XATTN_HEADS):
            sl = slice(h * HEAD_DIM, (h + 1) * HEAD_DIM)
            vsl = slice(D_MODEL + h * HEAD_DIM, D_MODEL + (h + 1) * HEAD_DIM)
            qh, kh, vh, doh = q_ref[:, sl], kv_ref[:, sl], kv_ref[:, vsl], do_ref[:, sl]
            p = _softmax_rows(qh, kh)
            pb = p.astype(BF16)
            dvs.append(lax.dot_general(pb, doh, (((0,), (0,)), ((), ())), preferred_element_type=F32))
            dp = lax.dot_general(doh, vh, (((1,), (1,)), ((), ())), preferred_element_type=F32)
            ds = p * (dp - jnp.sum(dp * p, axis=-1, keepdims=True)) * (HEAD_DIM ** -0.5)
            dsb = ds.astype(BF16)
            dqs.append(jnp.dot(dsb, kh, preferred_element_type=F32))
            dks.append(lax.dot_general(dsb, qh, (((0,), (0,)), ((), ())), preferred_element_type=F32))
        dq_ref[...] = jnp.concatenate(dqs, axis=1).astype(BF16)
        part = jnp.concatenate(dks + dvs, axis=1)

        @pl.when(pl.program_id(0) == 0)
        def _():
            dkv_ref[...] = part

        @pl.when(pl.program_id(0) > 0)
        def _():
            dkv_ref[...] += part

    row = pl.BlockSpec((tm, D_MODEL), lambda i: (i, 0))
    return pl.pallas_call(body, out_shape=(jax.ShapeDtypeStruct((s, D_MODEL), BF16),
                                           jax.ShapeDtypeStruct(kv.shape, F32)),
                          grid=(s // tm,), in_specs=[row, _full(kv.shape), row], out_specs=(row, _full(kv.shape)),
                          name=name, compiler_params=_params(("arbitrary",)))(q, kv, do)


def _rows_block(rows, cols):
    target = (512 * 1024) // cols
    fits = [cand for cand in range(8, rows + 1, 8) if rows % cand == 0 and cand <= target]
    return fits[-1] if fits else rows


def _elementwise(fn, ins, out_dtypes, name, place=None, flat=()):
    lead, rows, cols = [a for i, a in enumerate(ins) if i not in flat][0].shape
    tr = _rows_block(rows, cols)
    spec = pl.BlockSpec((None, tr, cols), lambda l, i, *_: (l, i, 0))
    flat_spec = pl.BlockSpec((tr, cols), lambda l, i, *_: (i, 0))
    n_in = len(ins)
    in_specs = [flat_spec if i in flat else spec for i in range(n_in)]
    out_specs = tuple([spec] * len(out_dtypes))
    out_shape = tuple(jax.ShapeDtypeStruct((lead, rows, cols), dt) for dt in out_dtypes)
    sem = _params(("parallel", "parallel"))

    def write(refs, outs):
        for o_ref, o in zip(refs, outs):
            o_ref[...] = o.astype(o_ref.dtype)

    if place is None:
        def body(*refs):
            write(refs[n_in:], fn(*[r[...] for r in refs[:n_in]]))

        return pl.pallas_call(body, out_shape=out_shape, grid=(lead, rows // tr), in_specs=in_specs,
                              out_specs=out_specs, name=name, compiler_params=sem)(*ins)

    def body_placed(p_ref, *refs):
        write(refs[n_in:], fn(p_ref[0], pl.program_id(0), *[r[...] for r in refs[:n_in]]))

    grid_spec = pltpu.PrefetchScalarGridSpec(num_scalar_prefetch=1, grid=(lead, rows // tr), in_specs=in_specs,
                                             out_specs=out_specs)
    return pl.pallas_call(body_placed, out_shape=out_shape, grid_spec=grid_spec, name=name,
                          compiler_params=sem)(place, *ins)


def _cast_to_slot(w, chip, dtype, name):
    lead, rows, cols = w.shape
    tr = _rows_block(rows, cols)

    def body(chip_ref, w_ref, o_ref):
        o_ref[...] = w_ref[...].astype(dtype)

    grid_spec = pltpu.PrefetchScalarGridSpec(
        num_scalar_prefetch=1, grid=(lead, rows // tr),
        in_specs=[pl.BlockSpec((None, tr, cols), lambda l, i, chip_ref: (l, i, 0))],
        out_specs=pl.BlockSpec((None, None, tr, cols), lambda l, i, chip_ref: (l, chip_ref[0], i, 0)))
    return pl.pallas_call(body, out_shape=jax.ShapeDtypeStruct((lead, N_CHIPS, rows, cols), dtype), grid_spec=grid_spec,
                          name=name, compiler_params=_params(("parallel", "parallel")))(chip, w)


def _chip_sum(pair, rx, chip, name):
    _, rows, cols = pair.shape
    tr = _rows_block(rows, cols)

    def body(chip_ref, p_ref, rx_ref, o_ref):
        acc = p_ref[...].astype(F32)
        for j in range(3):
            acc = acc + rx_ref[j].astype(F32)
        o_ref[...] = acc

    grid_spec = pltpu.PrefetchScalarGridSpec(
        num_scalar_prefetch=1, grid=(rows // tr,),
        in_specs=[pl.BlockSpec((None, tr, cols), lambda i, chip_ref: (chip_ref[0], i, 0)),
                  pl.BlockSpec((3, tr, cols), lambda i, chip_ref: (0, i, 0))],
        out_specs=pl.BlockSpec((tr, cols), lambda i, chip_ref: (i, 0)))
    return pl.pallas_call(body, out_shape=jax.ShapeDtypeStruct((rows, cols), F32), grid_spec=grid_spec, name=name,
                          compiler_params=_params(("parallel",)))(chip, pair, rx)


def _chip_sum_ordered(own, rx, chip, name):
    rows, cols = own.shape

    def body(chip_ref, own_ref, rx_ref, o_ref):
        me = chip_ref[0]
        acc = None
        for k in range(N_CHIPS):
            rel = jnp.bitwise_xor(me, k)
            term = jnp.where(rel == 0, own_ref[...],
                             jnp.where(rel == 2, rx_ref[0], jnp.where(rel == 1, rx_ref[1], rx_ref[2])))
            acc = term if acc is None else acc + term
        o_ref[...] = acc

    grid_spec = pltpu.PrefetchScalarGridSpec(
        num_scalar_prefetch=1, grid=(1,),
        in_specs=[pl.BlockSpec((rows, cols), lambda i, chip_ref: (0, 0)),
                  pl.BlockSpec((3, rows, cols), lambda i, chip_ref: (0, 0, 0))],
        out_specs=pl.BlockSpec((rows, cols), lambda i, chip_ref: (0, 0)))
    return pl.pallas_call(body, out_shape=jax.ShapeDtypeStruct((rows, cols), F32), grid_spec=grid_spec, name=name,
                          compiler_params=_params(("arbitrary",)))(chip, own, rx)


def _adamw_math(w, g, m, v):
    m = ADAM_B1 * m + (1.0 - ADAM_B1) * g
    v = ADAM_B2 * v + (1.0 - ADAM_B2) * (g * g)
    m_hat = m / (1.0 - ADAM_B1 ** ADAM_STEP)
    v_hat = v / (1.0 - ADAM_B2 ** ADAM_STEP)
    delta = -ADAM_LR * (m_hat / (jnp.sqrt(v_hat) + ADAM_EPS) + ADAM_WD * w)
    return delta, m, v


def _place():
    x, y, c = lax.axis_index("x"), lax.axis_index("y"), lax.axis_index("c")
    chips = [(1 - x, y), (x, 1 - y), (1 - x, 1 - y)]
    return x, y, c, 2 * x + y, chips


def _run_remote(copies):
    for send, _ in copies:
        send.start()
    for send, recv in copies:
        send.wait_send()
        recv.wait_recv()


def _chip_gather(fulls, name):
    n = len(fulls)

    def body(*refs):
        outs = refs[n:2 * n]
        ssem, rsem = refs[2 * n:]
        x, y, c, me, chips = _place()
        copies = []
        for w in range(n):
            mine = outs[w].at[c, me]
            for j, (px, py) in enumerate(chips):
                sems = (ssem.at[3 * w + j], rsem.at[3 * w + j])
                peer = dict(device_id=(px, py, c), device_id_type=MESH)
                send = pltpu.make_async_remote_copy(mine, mine, *sems, **peer)
                recv = pltpu.make_async_remote_copy(mine, outs[w].at[c, 2 * px + py], *sems, **peer)
                copies.append((send, recv))
        _run_remote(copies)

    out_shape = tuple(jax.ShapeDtypeStruct(f.shape, f.dtype) for f in fulls)
    return pl.pallas_call(body, out_shape=out_shape, in_specs=[ANY] * n, out_specs=tuple([ANY] * n),
                          input_output_aliases={w: w for w in range(n)},
                          scratch_shapes=[pltpu.SemaphoreType.DMA((3 * n,)), pltpu.SemaphoreType.DMA((3 * n,))],
                          name=name)(*fulls)


def _push_layer(full, core, name):
    _, _, rows, cols = full.shape
    tr = _rows_block(rows, cols)
    nrow = rows // tr

    def body(core_ref, blk_ref, out_ref, ssem, rsem):
        x, y, c = lax.axis_index("x"), lax.axis_index("y"), lax.axis_index("c")
        k, i = pl.program_id(0), pl.program_id(1)
        peer = dict(device_id=(x, y, 1 - c), device_id_type=MESH)
        dst = out_ref.at[pl.ds(c, 1), pl.ds(k, 1), pl.ds(pl.multiple_of(i * tr, 8), tr)]
        cp = pltpu.make_async_remote_copy(blk_ref, dst, ssem, rsem, **peer)
        cp.start()
        cp.wait_send()

        @pl.when(jnp.logical_and(k == N_CHIPS - 1, i == nrow - 1))
        def _():
            landed = out_ref.at[1 - c]
            pltpu.make_async_remote_copy(landed, landed, ssem, rsem, **peer).wait_recv()

    grid_spec = pltpu.PrefetchScalarGridSpec(
        num_scalar_prefetch=1, grid=(N_CHIPS, nrow),
        in_specs=[pl.BlockSpec((1, 1, tr, cols), lambda k, i, core_ref: (core_ref[0], k, i, 0))],
        out_specs=ANY, scratch_shapes=[pltpu.SemaphoreType.DMA, pltpu.SemaphoreType.DMA])
    return pl.pallas_call(body, out_shape=jax.ShapeDtypeStruct(full.shape, full.dtype), grid_spec=grid_spec,
                          input_output_aliases={1: 0}, name=name,
                          compiler_params=_params(("arbitrary", "arbitrary")))(core, full)


def _push_sibling(srcs, name):
    lead, rows, cols = srcs[0].shape
    tr = _rows_block(rows, cols)
    nrow = rows // tr
    nsrc = len(srcs)

    def body(*refs):
        blks, rx_ref = refs[:nsrc], refs[nsrc]
        ssem, rsem = refs[nsrc + 1:]
        x, y, c = lax.axis_index("x"), lax.axis_index("y"), lax.axis_index("c")
        k, i = pl.program_id(0), pl.program_id(1)
        peer = dict(device_id=(x, y, 1 - c), device_id_type=MESH)
        dst = rx_ref.at[pl.ds(k, 1), pl.ds(pl.multiple_of(i * tr, 8), tr)]

        def send(blk):
            cp = pltpu.make_async_remote_copy(blk, dst, ssem, rsem, **peer)
            cp.start()
            cp.wait_send()

        if nsrc == 1:
            send(blks[0])
        else:
            @pl.when(c == 0)
            def _():
                send(blks[1])

            @pl.when(c == 1)
            def _():
                send(blks[0])

        @pl.when(jnp.logical_and(k == lead - 1, i == nrow - 1))
        def _():
            pltpu.make_async_remote_copy(rx_ref, rx_ref, ssem, rsem, **peer).wait_recv()

    spec = pl.BlockSpec((1, tr, cols), lambda k, i: (k, i, 0))
    return pl.pallas_call(body, out_shape=jax.ShapeDtypeStruct(srcs[0].shape, srcs[0].dtype), grid=(lead, nrow),
                          in_specs=[spec] * nsrc, out_specs=ANY,
                          scratch_shapes=[pltpu.SemaphoreType.DMA, pltpu.SemaphoreType.DMA], name=name,
                          compiler_params=_params(("arbitrary", "arbitrary")))(*srcs)


def _chip_scatter(pairs, small, name):
    n = len(pairs)

    def body(*refs):
        ins, small_ref = refs[:n], refs[n]
        outs, small_out = refs[n + 1:2 * n + 1], refs[2 * n + 1]
        ssem, rsem = refs[2 * n + 2:]
        x, y, c, me, chips = _place()
        copies = []
        for w in range(n + 1):
            for j, (px, py) in enumerate(chips):
                src = ins[w].at[2 * px + py] if w < n else small_ref
                dst = (outs[w] if w < n else small_out).at[j]
                cp = pltpu.make_async_remote_copy(src, dst, ssem.at[3 * w + j], rsem.at[3 * w + j],
                                                  device_id=(px, py, c), device_id_type=MESH)
                copies.append((cp, cp))
        _run_remote(copies)

    out_shape = tuple(jax.ShapeDtypeStruct((3,) + it.shape[1:], it.dtype) for it in pairs) + \
                (jax.ShapeDtypeStruct((3,) + small.shape, small.dtype),)
    return pl.pallas_call(body, out_shape=out_shape, in_specs=[ANY] * (n + 1), out_specs=tuple([ANY] * (n + 1)),
                          scratch_shapes=[pltpu.SemaphoreType.DMA((3 * n + 3,)),
                                          pltpu.SemaphoreType.DMA((3 * n + 3,))],
                          name=name)(*pairs, small)


def _pack(arrays):
    flat = jnp.concatenate([a.reshape(-1).astype(F32) for a in arrays])
    pad = (-flat.shape[0]) % (8 * LANES)
    return jnp.pad(flat, (0, pad)).reshape(-1, LANES)


def _unpack(packed, shapes):
    flat = packed.reshape(-1)
    out, off = [], 0
    for shp in shapes:
        size = math.prod(shp)
        out.append(flat[off:off + size].reshape(shp))
        off += size
    return out


def kernel(x, mem, g_mix, w_in, b_gate, a_ln_g, a_ln_b, a_ws, a_bs, b_conv, c_wg, c_scale, w_branch_a, w_branch_b, w_branch_c, w_o, g_xattn, g_mem, w_xq, w_xkv, w_xo, g_ffn, w_ffn_in, w_ffn_out, g_final, loss_target, m_g_mix, m_w_in, m_b_gate, m_a_ln_g, m_a_ln_b, m_a_ws, m_a_bs, m_b_conv, m_c_wg, m_c_scale, m_w_branch_a, m_w_branch_b, m_w_branch_c, m_w_o, m_g_xattn, m_g_mem, m_w_xq, m_w_xkv, m_w_xo, m_g_ffn, m_w_ffn_in, m_w_ffn_out, m_g_final, v_g_mix, v_w_in, v_b_gate, v_a_ln_g, v_a_ln_b, v_a_ws, v_a_bs, v_b_conv, v_c_wg, v_c_scale, v_w_branch_a, v_w_branch_b, v_w_branch_c, v_w_o, v_g_xattn, v_g_mem, v_w_xq, v_w_xkv, v_w_xo, v_g_ffn, v_w_ffn_in, v_w_ffn_out, v_g_final):
    args = locals()
    wts = {n: args[n] for n in WEIGHTS}
    mom = {n: args["m_" + n] for n in WEIGHTS}
    var = {n: args["v_" + n] for n in WEIGHTS}
    xs = x[0]
    mems = mem[0]
    tgt = loss_target[0]
    chip = 2 * lax.axis_index("x") + lax.axis_index("y")

    bias_pack = jnp.zeros((DEPTH, 8, 384), F32)
    bias_pack = bias_pack.at[:, 0:3, 0:256].set(b_gate).at[:, 0:3, 256:384].set(b_conv)
    chip_arr = jnp.reshape(chip, (1,)).astype(jnp.int32)
    core_arr = jnp.reshape(lax.axis_index("c"), (1,)).astype(jnp.int32)
    slots = [_cast_to_slot(wts[n], chip_arr, BF16, name=f"cast_{n}") for n in BIG]
    slots.append(_cast_to_slot(bias_pack, chip_arr, F32, name="cast_biases"))
    gathered = _chip_gather(slots, name="gather_chips")
    full = [_push_layer(g, core_arr, name=f"gather_cores_{n}")
            for g, n in zip(gathered, BIG + ("biases",))]
    fw = {}
    for n, g in zip(BIG, full[:-1]):
        if n in COL_SHARDED:
            fw[n] = jnp.transpose(g, (0, 2, 1, 3)).reshape(DEPTH, g.shape[2], N_CHIPS * g.shape[3])
        else:
            fw[n] = g.reshape(DEPTH, N_CHIPS * g.shape[2], g.shape[3])
    biases = full[-1]
    b_gate_full = jnp.transpose(biases[:, :, 0:3, 0:256], (0, 2, 1, 3)).reshape(DEPTH, 3, D_MODEL)
    b_conv_full = jnp.transpose(biases[:, :, 0:3, 256:384], (0, 2, 1, 3)).reshape(DEPTH, 3, B_WIDTH)

    tril = jnp.tril(jnp.ones((CHUNK, CHUNK), bool))

    def mixer_params(l):
        wtril = jnp.where(tril[None], a_ws[l], 0.0)
        prm = (a_ln_g[l][None], a_ln_b[l][None], wtril.astype(BF16),
               jnp.broadcast_to(a_bs[l][:, :, None], (A_GROUPS, CHUNK, CHUNK)), b_conv_full[l],
               c_wg[l].astype(BF16), c_scale[l][None])
        return prm, jnp.swapaxes(wtril, 1, 2).astype(BF16)

    saved = []
    xc = xs
    for l in range(DEPTH):
        sv = {"x0": xc}
        prm, _ = mixer_params(l)
        sv["h"], sv["proj"] = _norm_mm(xc, g_mix[l][None], fw["w_in"], l, tn=1536, name=f"l{l}_proj")
        sv["ypre"] = _mixer_fwd(sv["proj"], prm, name=f"l{l}_mixers")
        sv["merged"] = _merge_fwd(sv["ypre"], sv["proj"], fw["w_branch_a"][l], fw["w_branch_b"][l],
                                  fw["w_branch_c"][l], b_gate_full[l], name=f"l{l}_merge")
        xc = _mm(sv["merged"], fw["w_o"], b_lead=l, mode="nn", out_dtype=F32, name=f"l{l}_mix_out", tm=1024, tn=1024,
                 tk=1024, res=xc)
        sv["x1"] = xc
        sv["hq"], sv["q"] = _norm_mm(xc, g_xattn[l][None], fw["w_xq"], l, tn=1024, name=f"l{l}_q")
        sv["m"] = _rms_fwd(mems, g_mem[l][None], name=f"l{l}_mem_norm")
        sv["kv"] = _mm(sv["m"], fw["w_xkv"], b_lead=l, mode="nn", out_dtype=BF16, name=f"l{l}_kv", tm=256, tn=512,
                       tk=1024)
        sv["o"] = _attn_fwd(sv["q"], sv["kv"], name=f"l{l}_attn")
        xc = _mm(sv["o"], fw["w_xo"], b_lead=l, mode="nn", out_dtype=F32, name=f"l{l}_xattn_out", tm=1024, tn=1024,
                 tk=1024, res=xc)
        sv["x2"] = xc
        sv["h2"], sv["gu"], sv["act"] = _norm_ffn_in(xc, g_ffn[l][None], fw["w_ffn_in"], l, name=f"l{l}_ffn_in")
        xc = _mm(sv["act"], fw["w_ffn_out"], b_lead=l, mode="nn", out_dtype=F32, name=f"l{l}_ffn_out", tm=1024,
                 tn=512, tk=FFN_HIDDEN, res=xc)
        saved.append(sv)

    loss_part, dx, dg_final = _final_loss(xc, g_final[None], tgt, name="final_loss")
    loss = lax.psum(loss_part[0, 0], ("x", "y", "c"))

    big_grads = [None] * DEPTH
    small_grads = [None] * DEPTH
    for l in reversed(range(DEPTH)):
        sv = saved[l]
        prm, wtril_t = mixer_params(l)
        gb = {}
        sg = {}
        dgu = _d_act_swiglu(dx, fw["w_ffn_out"], l, sv["gu"], name=f"l{l}_d_act")
        gb["w_ffn_out"] = _mm(sv["act"], dx, mode="tn", out_dtype=BF16, name=f"l{l}_dw_ffn_out", tm=1408, tn=1024,
                              tk=TOKEN_STEP).reshape(N_CHIPS, FFN_HIDDEN // N_CHIPS, D_MODEL)
        gb["w_ffn_in"] = _mm(sv["h2"], dgu, mode="tn", out_dtype=BF16, name=f"l{l}_dw_ffn_in", tm=1024, tn=FFN_TILE,
                             tk=TOKEN_STEP, b_halves=True, out_shards=N_CHIPS)
        dx, sg["g_ffn"] = _mm_nt_norm_bwd(dgu, fw["w_ffn_in"], l, sv["x2"], g_ffn[l][None], dx, tm=512,
                                          a_halves=True, name=f"l{l}_d_h2")
        do = _mm(dx, fw["w_xo"], b_lead=l, mode="nt", out_dtype=BF16, name=f"l{l}_d_o", tm=1024, tn=1024, tk=1024)
        gb["w_xo"] = _mm(sv["o"], dx, mode="tn", out_dtype=BF16, name=f"l{l}_dw_xo", tm=1024, tn=1024,
                         tk=TOKEN_STEP).reshape(N_CHIPS, D_MODEL // N_CHIPS, D_MODEL)
        dq, dkv = _attn_bwd(sv["q"], sv["kv"], do, name=f"l{l}_d_attn")
        gb["w_xq"] = _mm(sv["hq"], dq, mode="tn", out_dtype=BF16, name=f"l{l}_dw_xq", tm=1024, tn=1024,
                         tk=TOKEN_STEP).reshape(N_CHIPS, D_MODEL // N_CHIPS, D_MODEL)
        dm = _mm(dkv, fw["w_xkv"], b_lead=l, mode="nt", out_dtype=BF16, name=f"l{l}_d_m", tm=256, tn=1024, tk=1024)
        gb["w_xkv"] = _mm(sv["m"], dkv, mode="tn", out_dtype=BF16, name=f"l{l}_dw_xkv", tm=1024, tn=512, tk=256,
                          out_shards=N_CHIPS)
        _, sg["g_mem"] = _rms_bwd(mems, g_mem[l][None], dm, None, name=f"l{l}_d_mem_norm")
        dx, sg["g_xattn"] = _mm_nt_norm_bwd(dq, fw["w_xq"], l, sv["x1"], g_xattn[l][None], dx, tm=1024,
                                            name=f"l{l}_d_hq")
        dmerged = _mm(dx, fw["w_o"], b_lead=l, mode="nt", out_dtype=BF16, name=f"l{l}_d_merged", tm=1024, tn=1024,
                      tk=1024)
        gb["w_o"] = _mm(sv["merged"], dx, mode="tn", out_dtype=BF16, name=f"l{l}_dw_o", tm=1024, tn=1024,
                        tk=TOKEN_STEP).reshape(N_CHIPS, D_MODEL // N_CHIPS, D_MODEL)
        dypre, dproj, sg["b_gate"], dwa, dwb, dwc = _merge_bwd(sv["ypre"], sv["proj"], dmerged, fw["w_branch_a"][l],
                                                               fw["w_branch_b"][l], fw["w_branch_c"][l],
                                                               b_gate_full[l], name=f"l{l}_d_merge")
        for n, dw in zip(("w_branch_a", "w_branch_b", "w_branch_c"), (dwa, dwb, dwc)):
            gb[n] = jnp.transpose(dw.reshape(A_WIDTH, N_CHIPS, D_MODEL // N_CHIPS), (1, 0, 2)).astype(BF16)
        (dproj, sg["a_ln_g"], sg["a_ln_b"], sg["a_ws"], sg["a_bs"], sg["b_conv"], sg["c_wg"],
         sg["c_scale"]) = _mixer_bwd(sv["proj"], dypre, dproj, prm, wtril_t, name=f"l{l}_d_mixers")
        gb["w_in"] = _mm(sv["h"], dproj, mode="tn", out_dtype=BF16, name=f"l{l}_dw_in", tm=1024, tn=1536,
                         tk=TOKEN_STEP, out_shards=N_CHIPS)
        dx, sg["g_mix"] = _mm_nt_norm_bwd(dproj, fw["w_in"], l, sv["x0"], g_mix[l][None], dx, tm=512,
                                          name=f"l{l}_d_h")
        big_grads[l] = [gb[n] for n in BIG]
        small_grads[l] = sg
    grad_x = dx[None]

    small_names = [n for n in WEIGHTS if n not in BIG]
    small_full_shapes = {n: ((DEPTH, 3, D_MODEL) if n == "b_gate" else (DEPTH, 3, B_WIDTH) if n == "b_conv"
                             else wts[n].shape) for n in small_names}
    small_local = []
    for n in small_names:
        if n == "g_final":
            small_local.append(dg_final)
        else:
            small_local.append(jnp.stack([small_grads[l][n].reshape(small_full_shapes[n][1:]) for l in range(DEPTH)]))
    small_pack = _pack(small_local)

    nb = len(BIG)

    def pair_sum(core, _, g0, g1, r):
        return (jnp.where(core == 0, g0, g1).astype(F32) + r.astype(F32),)

    pair = []
    for i, n in enumerate(BIG):
        rx = _push_sibling([big_grads[0][i], big_grads[1][i]], name=f"reduce_cores_{n}")
        pair.append(_elementwise(pair_sum, [big_grads[0][i], big_grads[1][i], rx], [BF16], name=f"pair_sum_{n}",
                                 place=core_arr)[0])
    small_rx = _push_sibling([small_pack[None]], name="reduce_cores_small")
    small_pair = _elementwise(lambda a, b: (a + b,), [small_pack[None], small_rx], [F32], name="pair_sum_small")[0][0]
    scattered = _chip_scatter(pair, small_pair, name="reduce_chips")
    layer_grads = [_chip_sum(pair[i], scattered[i], chip_arr, name=f"chip_sum_{BIG[i]}") for i in range(nb)]
    small_sum = _chip_sum_ordered(small_pair, scattered[nb], chip_arr, name="chip_sum_small")

    def adamw_layers(core, layer, w, own, other, m, v):
        g = jnp.where(layer == core, own, other)
        return (g,) + _adamw_math(w, g, m, v)

    out_g, out_d, out_m, out_v = {}, {}, {}, {}
    for i, n in enumerate(BIG):
        other = _push_sibling([layer_grads[i][None]], name=f"share_cores_{n}")[0]
        out_g[n], out_d[n], out_m[n], out_v[n] = _elementwise(
            adamw_layers, [wts[n], layer_grads[i], other, mom[n], var[n]], [F32, F32, F32, F32], name=f"adamw_{n}",
            place=core_arr, flat=(1, 2))
    small_g = dict(zip(small_names, _unpack(small_sum, [small_full_shapes[n] for n in small_names])))
    small_g["b_gate"] = lax.dynamic_slice_in_dim(small_g["b_gate"], chip * 256, 256, axis=2)
    small_g["b_conv"] = lax.dynamic_slice_in_dim(small_g["b_conv"], chip * 128, 128, axis=2)
    packs = [_pack([d[n] for n in small_names])[None] for d in (wts, small_g, mom, var)]
    upd = _elementwise(_adamw_math, packs, [F32, F32, F32], name="adamw_small")
    shapes = [wts[n].shape for n in small_names]
    for n, d, m_new, v_new in zip(small_names, *[_unpack(u[0], shapes) for u in upd]):
        out_g[n], out_d[n], out_m[n], out_v[n] = small_g[n], d, m_new, v_new

    return (loss, grad_x, *[out_g[n] for n in WEIGHTS], *[out_d[n] for n in WEIGHTS], *[out_m[n] for n in WEIGHTS],
            *[out_v[n] for n in WEIGHTS])
```

```python
import functools
import math
from typing import Callable, NamedTuple

import jax
import jax.numpy as jnp
from jax import lax
from jax.experimental import pallas as pl
from jax.experimental.pallas import tpu as pltpu

F32 = jnp.float32
BF16 = jnp.bfloat16

D_MODEL = 1024
DEPTH = 2
MEM_LEN = 256
EPS = 1e-6
CHUNK = 128
A_GROUPS = 4
A_WIDTH = 512
B_WIDTH = 512
C_WIDTH = 512
C_GROUP = 128
POOL_WINDOWS = (2, 4, 8, 16)
HALO = 16
IN_COLS = 6144
MIX_COLS = 3072
XATTN_HEADS = 4
HEAD_DIM = 256
FFN_HIDDEN = 2816
N_CHIPS = 4

ADAM_LR = 0.001
ADAM_B1 = 0.9
ADAM_B2 = 0.999
ADAM_EPS = 1e-08
ADAM_WD = 0.01
ADAM_STEP = 10

V7X_VMEM_BYTES = 64 * 1024 * 1024
VMEM_LIMIT = (V7X_VMEM_BYTES * 3) // 4
LANES = 128
MESH = pl.DeviceIdType.MESH
ANY = pl.BlockSpec(memory_space=pl.ANY)

BIG = ("w_in", "w_branch_a", "w_branch_b", "w_branch_c", "w_o", "w_xq", "w_xkv", "w_xo", "w_ffn_in", "w_ffn_out")
COL_SHARDED = ("w_in", "w_branch_a", "w_branch_b", "w_branch_c", "w_xkv", "w_ffn_in")
RIDE_FIRST = ("w_in", "w_ffn_out", "w_branch_a", "w_branch_b", "w_branch_c")
SMALL_REPL = ("g_mix", "a_ln_g", "a_ln_b", "a_ws", "a_bs", "c_wg", "c_scale", "g_xattn", "g_mem", "g_ffn", "g_final")
SMALL_SHARDED = ("b_gate", "b_conv")
WEIGHTS = ("g_mix", "w_in", "b_gate", "a_ln_g", "a_ln_b", "a_ws", "a_bs", "b_conv", "c_wg", "c_scale", "w_branch_a",
           "w_branch_b", "w_branch_c", "w_o", "g_xattn", "g_mem", "w_xq", "w_xkv", "w_xo", "g_ffn", "w_ffn_in",
           "w_ffn_out", "g_final")


def _params(sem=None):
    return pltpu.CompilerParams(dimension_semantics=sem, vmem_limit_bytes=VMEM_LIMIT)


def _blk(dim, pref):
    return pref if dim % pref == 0 else dim


class _Carry(NamedTuple):
    arrays: tuple
    outs: tuple
    in_place: bool
    copies: int
    issue: Callable


def _call_with_carry(body, *, out_shape, grid, in_specs, out_specs, scratch_shapes, name, semantics, operands, carry):
    if carry is None:
        return pl.pallas_call(body, out_shape=tuple(out_shape), grid=grid, in_specs=list(in_specs),
                              out_specs=tuple(out_specs), scratch_shapes=list(scratch_shapes), name=name,
                              compiler_params=_params(semantics))(*operands)
    n_in, n_out, na, no = len(operands), len(out_shape), len(carry.arrays), len(carry.outs)

    def carried(*refs):
        ins, cins = refs[:n_in], refs[n_in:n_in + na]
        outs, couts = refs[n_in + na:n_in + na + n_out], refs[n_in + na + n_out:n_in + na + n_out + no]
        rest = refs[n_in + na + n_out + no:]
        scratch, ssem, rsem = rest[:-2], rest[-2], rest[-1]
        first = functools.reduce(jnp.logical_and, [pl.program_id(ax) == 0 for ax in range(len(grid))])
        last = functools.reduce(jnp.logical_and, [pl.program_id(ax) == grid[ax] - 1 for ax in range(len(grid))])

        @pl.when(first)
        def _():
            for send, _ in carry.issue(cins, couts, ssem, rsem):
                send.start()

        body(*ins, *outs, *scratch)

        @pl.when(last)
        def _():
            for send, recv in carry.issue(cins, couts, ssem, rsem):
                send.wait_send()
                recv.wait_recv()

    aliases = {n_in + i: n_out + i for i in range(na)} if carry.in_place else {}
    sems = [pltpu.SemaphoreType.DMA((carry.copies,)), pltpu.SemaphoreType.DMA((carry.copies,))]
    return pl.pallas_call(carried, out_shape=tuple(out_shape) + tuple(carry.outs), grid=grid,
                          in_specs=list(in_specs) + [ANY] * na, out_specs=tuple(out_specs) + tuple([ANY] * no),
                          scratch_shapes=list(scratch_shapes) + sems, input_output_aliases=aliases, name=name,
                          compiler_params=_params(("arbitrary",) * len(grid)))(*operands, *carry.arrays)


def _mm(a, b, *, mode, out_dtype, name, tm, tn, tk, res=None, b_lead=None, b_halves=False, out_shards=None,
        carry=None):
    dims = {"nn": (((1,), (0,)), ((), ())), "nt": (((1,), (1,)), ((), ())), "tn": (((0,), (0,)), ((), ()))}[mode]
    b_rows, b_last = b.shape[-2], b.shape[-1]
    if mode == "nn":
        m, k, n = a.shape[0], a.shape[1], b_last
        assert b_rows == k
    elif mode == "nt":
        m, k, n = a.shape[0], a.shape[1], b_rows
        assert b_last == k
    else:
        k, m, n = a.shape[0], a.shape[1], (2 * b_last if b_halves else b_last)
        assert b_rows == k
    tm, tn, tk = _blk(m, tm), _blk(n, tn), _blk(k, tk)
    nk = k // tk
    grid = (m // tm, n // tn, nk)

    if mode == "tn":
        a_spec = pl.BlockSpec((tk, tm), lambda i, j, kk: (kk, i))
        b_block, b_idx = (tk, tn), (lambda i, j, kk: (kk, j))
    else:
        a_spec = pl.BlockSpec((tm, tk), lambda i, j, kk: (i, kk))
        if mode == "nn":
            b_block, b_idx = (tk, tn), (lambda i, j, kk: (kk, j))
        else:
            b_block, b_idx = (tn, tk), (lambda i, j, kk: (j, kk))
    if b_halves:
        assert mode == "tn" and b_lead is None and b_last % tn == 0
        per_half = b_last // tn
        b_spec = pl.BlockSpec((None,) + b_block, lambda i, j, kk: (j // per_half, kk, j % per_half))
    elif b_lead is None:
        b_spec = pl.BlockSpec(b_block, b_idx)
    else:
        b_spec = pl.BlockSpec((None,) + b_block, lambda i, j, kk: (b_lead,) + b_idx(i, j, kk))
    in_specs = [a_spec, b_spec]
    operands = [a, b]
    if res is not None:
        in_specs.append(pl.BlockSpec((tm, tn), lambda i, j, kk: (i, j)))
        operands.append(res)
    if out_shards is None:
        out_shape = jax.ShapeDtypeStruct((m, n), out_dtype)
        out_spec = pl.BlockSpec((tm, tn), lambda i, j, kk: (i, j))
    else:
        per = n // out_shards
        assert per % tn == 0
        nps = per // tn
        out_shape = jax.ShapeDtypeStruct((out_shards, m, per), out_dtype)
        out_spec = pl.BlockSpec((None, tm, tn), lambda i, j, kk: (j // nps, i, j % nps))

    def body(*refs):
        a_ref, b_ref = refs[0], refs[1]
        res_ref = refs[2] if res is not None else None
        o_ref = refs[3] if res is not None else refs[2]
        part = lax.dot_general(a_ref[...].astype(BF16), b_ref[...].astype(BF16), dims, preferred_element_type=F32)

        def finish(acc):
            if res_ref is not None:
                acc = acc + res_ref[...]
            o_ref[...] = acc.astype(out_dtype)

        if nk == 1:
            finish(part)
        else:
            acc_ref = refs[-1]
            kk = pl.program_id(2)

            @pl.when(kk == 0)
            def _():
                acc_ref[...] = part

            @pl.when(kk > 0)
            def _():
                acc_ref[...] += part

            @pl.when(kk == nk - 1)
            def _():
                finish(acc_ref[...])

    scratch = [] if nk == 1 else [pltpu.VMEM((tm, tn), F32)]
    outs = _call_with_carry(body, out_shape=(out_shape,), grid=grid, in_specs=in_specs, out_specs=(out_spec,),
                            scratch_shapes=scratch, name=name, semantics=("parallel", "parallel", "arbitrary"),
                            operands=operands, carry=carry)
    return outs[0] if carry is None else (outs[0], outs[1:])


ROW_CHUNK = 256
FFN_TILE = 1408
TOKEN_STEP = 2048
NT_DIMS = (((1,), (1,)), ((), ()))


def _norm_rows_into(x_ref, g_ref, h_ref, hs_ref, tm):
    rc = min(ROW_CHUNK, tm)
    for r0 in range(0, tm, rc):
        xv = x_ref[r0:r0 + rc, :]
        r = lax.rsqrt(jnp.mean(xv * xv, axis=-1, keepdims=True) + EPS)
        hv = (xv * r * g_ref[...]).astype(BF16)
        hs_ref[r0:r0 + rc, :] = hv
        h_ref[r0:r0 + rc, :] = hv


def _norm_mm(x, g, b, b_lead, *, tn, name, carry=None):
    s, d = x.shape
    n = b.shape[-1]
    tm, tn = _blk(s, 1024), _blk(n, tn)

    def body(x_ref, g_ref, b_ref, h_ref, o_ref, hs_ref):
        @pl.when(pl.program_id(1) == 0)
        def _():
            _norm_rows_into(x_ref, g_ref, h_ref, hs_ref, tm)

        o_ref[...] = jnp.dot(hs_ref[...], b_ref[...], preferred_element_type=F32).astype(BF16)

    row = pl.BlockSpec((tm, d), lambda i, j: (i, 0))
    outs = _call_with_carry(body, out_shape=(jax.ShapeDtypeStruct((s, d), BF16), jax.ShapeDtypeStruct((s, n), BF16)),
                            grid=(s // tm, n // tn),
                            in_specs=[row, pl.BlockSpec((1, d), lambda i, j: (0, 0)),
                                      pl.BlockSpec((None, d, tn), lambda i, j: (b_lead, 0, j))],
                            out_specs=(row, pl.BlockSpec((tm, tn), lambda i, j: (i, j))),
                            scratch_shapes=[pltpu.VMEM((tm, d), BF16)], name=name,
                            semantics=("parallel", "arbitrary"), operands=[x, g, b], carry=carry)
    return outs if carry is None else (outs[0], outs[1], outs[2:])


def _norm_ffn_in(x, g, b, b_lead, name, carry=None):
    s, d = x.shape
    tm, tn = _blk(s, 512), FFN_TILE
    nj = FFN_HIDDEN // tn

    def body(x_ref, g_ref, bg_ref, bu_ref, h_ref, gu_ref, act_ref, hs_ref):
        @pl.when(pl.program_id(1) == 0)
        def _():
            _norm_rows_into(x_ref, g_ref, h_ref, hs_ref, tm)

        hv = hs_ref[...]
        gate = jnp.dot(hv, bg_ref[...], preferred_element_type=F32)
        up = jnp.dot(hv, bu_ref[...], preferred_element_type=F32)
        gu_ref[0] = gate.astype(BF16)
        gu_ref[1] = up.astype(BF16)
        act_ref[...] = (gate * _sigmoid(gate) * up).astype(BF16)

    row = pl.BlockSpec((tm, d), lambda i, j: (i, 0))
    out_shape = (jax.ShapeDtypeStruct((s, d), BF16), jax.ShapeDtypeStruct((2, s, FFN_HIDDEN), BF16),
                 jax.ShapeDtypeStruct((s, FFN_HIDDEN), BF16))
    outs = _call_with_carry(body, out_shape=out_shape, grid=(s // tm, nj),
                            in_specs=[row, pl.BlockSpec((1, d), lambda i, j: (0, 0)),
                                      pl.BlockSpec((None, d, tn), lambda i, j: (b_lead, 0, j)),
                                      pl.BlockSpec((None, d, tn), lambda i, j: (b_lead, 0, j + nj))],
                            out_specs=(row, pl.BlockSpec((2, tm, tn), lambda i, j: (0, i, j)),
                                       pl.BlockSpec((tm, tn), lambda i, j: (i, j))),
                            scratch_shapes=[pltpu.VMEM((tm, d), BF16)], name=name,
                            semantics=("parallel", "arbitrary"), operands=[x, g, b, b], carry=carry)
    return outs if carry is None else (outs[0], outs[1], outs[2], outs[3:])


def _d_act_swiglu(dx, w, b_lead, gu, name):
    s, d = dx.shape
    tm, tn = _blk(s, 512), FFN_TILE

    def body(dx_ref, w_ref, gu_ref, o_ref):
        dact = lax.dot_general(dx_ref[...].astype(BF16), w_ref[...], NT_DIMS, preferred_element_type=F32)
        gate = gu_ref[0].astype(F32)
        up = gu_ref[1].astype(F32)
        sg = _sigmoid(gate)
        o_ref[0] = (dact * up * sg * (1.0 + gate * (1.0 - sg))).astype(BF16)
        o_ref[1] = (dact * gate * sg).astype(BF16)

    halves = pl.BlockSpec((2, tm, tn), lambda j, i: (0, i, j))
    return pl.pallas_call(body, out_shape=jax.ShapeDtypeStruct(gu.shape, BF16), grid=(FFN_HIDDEN // tn, s // tm),
                          in_specs=[pl.BlockSpec((tm, d), lambda j, i: (i, 0)),
                                    pl.BlockSpec((None, tn, d), lambda j, i: (b_lead, j, 0)), halves],
                          out_specs=halves, name=name,
                          compiler_params=_params(("parallel", "parallel")))(dx, w, gu)


def _mm_nt_norm_bwd(a, b, b_lead, x, g, dres, *, tm, name, a_halves=False):
    s, d = x.shape
    kdim = b.shape[-1]
    tm = _blk(s, tm)
    rc = min(ROW_CHUNK, tm)
    if a_halves:
        a_spec = pl.BlockSpec((2, tm, kdim // 2), lambda i: (0, i, 0))
    else:
        a_spec = pl.BlockSpec((tm, kdim), lambda i: (i, 0))

    def body(a_ref, b_ref, x_ref, g_ref, r_ref, dx_ref, dg_ref):
        i = pl.program_id(0)
        av = jnp.concatenate([a_ref[0], a_ref[1]], axis=1) if a_halves else a_ref[...]
        dh = lax.dot_general(av.astype(BF16), b_ref[...], NT_DIMS, preferred_element_type=F32)
        gv = g_ref[...]
        dg_part = None
        for r0 in range(0, tm, rc):
            xv = x_ref[r0:r0 + rc, :]
            dhv = dh[r0:r0 + rc, :]
            r = lax.rsqrt(jnp.mean(xv * xv, axis=-1, keepdims=True) + EPS)
            xhat = xv * r
            p = jnp.sum(dhv * xhat, axis=0, keepdims=True)
            dg_part = p if dg_part is None else dg_part + p
            dxhat = dhv * gv
            dx_ref[r0:r0 + rc, :] = r_ref[r0:r0 + rc, :] + r * (
                dxhat - xhat * jnp.mean(dxhat * xhat, axis=-1, keepdims=True))

        @pl.when(i == 0)
        def _():
            dg_ref[...] = dg_part

        @pl.when(i > 0)
        def _():
            dg_ref[...] += dg_part

    row = pl.BlockSpec((tm, d), lambda i: (i, 0))
    vec = pl.BlockSpec((1, d), lambda i: (0, 0))
    b_spec = pl.BlockSpec((None, d, kdim), lambda i: (b_lead, 0, 0), pipeline_mode=pl.Buffered(1))
    return pl.pallas_call(body, out_shape=(jax.ShapeDtypeStruct((s, d), F32), jax.ShapeDtypeStruct((1, d), F32)),
                          grid=(s // tm,), in_specs=[a_spec, b_spec, row, vec, row], out_specs=(row, vec), name=name,
                          compiler_params=_params(("arbitrary",)))(a, b, x, g, dres)


def _rms_fwd(x, g, name):
    s, d = x.shape
    tm = _blk(s, 512)

    def body(x_ref, g_ref, o_ref):
        xv = x_ref[...]
        r = lax.rsqrt(jnp.mean(xv * xv, axis=-1, keepdims=True) + EPS)
        o_ref[...] = (xv * r * g_ref[...]).astype(BF16)

    return pl.pallas_call(body, out_shape=jax.ShapeDtypeStruct((s, d), BF16), grid=(s // tm,),
                          in_specs=[pl.BlockSpec((tm, d), lambda i: (i, 0)), pl.BlockSpec((1, d), lambda i: (0, 0))],
                          out_specs=pl.BlockSpec((tm, d), lambda i: (i, 0)), name=name,
                          compiler_params=_params(("parallel",)))(x, g)


def _rms_bwd(x, g, dh, dres, name):
    s, d = x.shape
    tm = _blk(s, 512)
    has_res = dres is not None

    def body(*refs):
        x_ref, g_ref, dh_ref = refs[0], refs[1], refs[2]
        dx_ref, dg_ref = refs[-2], refs[-1]
        xv = x_ref[...]
        r = lax.rsqrt(jnp.mean(xv * xv, axis=-1, keepdims=True) + EPS)
        xhat = xv * r
        dhv = dh_ref[...].astype(F32)
        part = jnp.sum(dhv * xhat, axis=0, keepdims=True)

        @pl.when(pl.program_id(0) == 0)
        def _():
            dg_ref[...] = part

        @pl.when(pl.program_id(0) > 0)
        def _():
            dg_ref[...] += part

        dxhat = dhv * g_ref[...]
        dx = r * (dxhat - xhat * jnp.mean(dxhat * xhat, axis=-1, keepdims=True))
        if has_res:
            dx = dx + refs[3][...]
        dx_ref[...] = dx

    row = pl.BlockSpec((tm, d), lambda i: (i, 0))
    vec = pl.BlockSpec((1, d), lambda i: (0, 0))
    in_specs = [row, vec, row] + ([row] if has_res else [])
    operands = [x, g, dh] + ([dres] if has_res else [])
    return pl.pallas_call(body, out_shape=(jax.ShapeDtypeStruct((s, d), F32), jax.ShapeDtypeStruct((1, d), F32)),
                          grid=(s // tm,), in_specs=in_specs, out_specs=(row, vec), name=name,
                          compiler_params=_params(("arbitrary",)))(*operands)


def _final_loss(x, g, target, name):
    s, d = x.shape
    tm = _blk(s, 512)

    def body(x_ref, g_ref, t_ref, loss_ref, dx_ref, dg_ref):
        xv = x_ref[...]
        gv = g_ref[...]
        r = lax.rsqrt(jnp.mean(xv * xv, axis=-1, keepdims=True) + EPS)
        xhat = xv * r
        err = xhat * gv - t_ref[...]
        lpart = 0.5 * jnp.sum(jnp.mean(err * err, axis=-1, keepdims=True), axis=0, keepdims=True)
        dy = err * (1.0 / d)
        gpart = jnp.sum(dy * xhat, axis=0, keepdims=True)

        @pl.when(pl.program_id(0) == 0)
        def _():
            loss_ref[...] = lpart
            dg_ref[...] = gpart

        @pl.when(pl.program_id(0) > 0)
        def _():
            loss_ref[...] += lpart
            dg_ref[...] += gpart

        dxhat = dy * gv
        dx_ref[...] = r * (dxhat - xhat * jnp.mean(dxhat * xhat, axis=-1, keepdims=True))

    row = pl.BlockSpec((tm, d), lambda i: (i, 0))
    vec = pl.BlockSpec((1, d), lambda i: (0, 0))
    one = pl.BlockSpec((1, 1), lambda i: (0, 0))
    return pl.pallas_call(body, out_shape=(jax.ShapeDtypeStruct((1, 1), F32), jax.ShapeDtypeStruct((s, d), F32),
                                           jax.ShapeDtypeStruct((1, d), F32)),
                          grid=(s // tm,), in_specs=[row, vec, row], out_specs=(one, row, vec), name=name,
                          compiler_params=_params(("arbitrary",)))(x, g, target)


def _erf_parts(x):
    cdf = 0.5 * (1.0 + lax.erf(x * (1.0 / math.sqrt(2.0))))
    return cdf


def _shift_down(ext, k):
    return pltpu.roll(ext, k, 0)


def _shift_up(ext, k):
    return pltpu.roll(ext, ext.shape[0] - k, 0)


def _mixer_forward_math(z, halo, row0, prm, tm):
    ln_g, ln_b, wtril, bsb, conv_w, wg, scale = prm
    za = z[:, 0:2 * A_WIDTH]
    gb = z[:, 1024:1536]
    gc = z[:, 1536:2048]
    xin = z[:, 2048:2560]
    zc = z[:, 2560:3072]
    cdf = _erf_parts(za)
    act = za * cdf
    u = act[:, :A_WIDTH]
    v = act[:, A_WIDTH:]
    mu = jnp.mean(v, axis=-1, keepdims=True)
    vc = v - mu
    rstd = lax.rsqrt(jnp.mean(vc * vc, axis=-1, keepdims=True) + EPS)
    vhat = vc * rstd
    vn = (vhat * ln_g + ln_b).astype(BF16)
    rows = []
    for c in range(tm // CHUNK):
        cols = []
        for g in range(A_GROUPS):
            blk = vn[c * CHUNK:(c + 1) * CHUNK, g * CHUNK:(g + 1) * CHUNK]
            cols.append(jnp.dot(wtril[g], blk, preferred_element_type=F32) + bsb[g])
        rows.append(jnp.concatenate(cols, axis=1))
    mixed = jnp.concatenate(rows, axis=0)
    ya = u * mixed
    y = gc * xin
    yext = jnp.concatenate([halo[:, 0:512] * halo[:, 512:1024], y], axis=0)
    y1 = _shift_down(yext, 1)[HALO:]
    y2 = _shift_down(yext, 2)[HALO:]
    conv = conv_w[0:1, :] * y2 + conv_w[1:2, :] * y1 + conv_w[2:3, :] * y
    yb = gb * conv
    t = row0 + lax.broadcasted_iota(jnp.int32, (tm, 1), 0)
    zext = jnp.concatenate([halo[:, 1024:1536], zc], axis=0)
    pooled, p = [], []
    for gi, win in enumerate(POOL_WINDOWS):
        sw = zext[:, gi * C_GROUP:(gi + 1) * C_GROUP]
        for step in range(gi + 1):
            sw = sw + _shift_down(sw, 2 ** step)
        cnt = jnp.minimum(t + 1, win).astype(F32)
        pg = sw[HALO:] / cnt - zc[:, gi * C_GROUP:(gi + 1) * C_GROUP]
        pooled.append(pg)
        p.append(jnp.dot(pg.astype(BF16), wg[gi], preferred_element_type=F32))
    p = jnp.concatenate(p, axis=1)
    yc = p * scale
    saved = dict(za=za, cdf=cdf, u=u, mixed=mixed, rstd=rstd, vhat=vhat, vn=vn, gb=gb, gc=gc, xin=xin, y=y, y1=y1, y2=y2,
                 conv=conv, pooled=pooled, p=p, t=t)
    return ya, yb, yc, saved


def _mixer_specs(tm, s):
    nb16 = tm // HALO
    main = pl.BlockSpec((tm, MIX_COLS), lambda i: (i, 0))
    prev = pl.BlockSpec((HALO, 1536), lambda i: (jnp.maximum(i * nb16 - 1, 0), 1))
    return main, prev


def _full(shape):
    n = len(shape)
    return pl.BlockSpec(shape, lambda i: (0,) * n)


def _mixer_fwd(proj, prm, name):
    s = proj.shape[0]
    tm = _blk(s, 256)
    main, prev = _mixer_specs(tm, s)

    def body(z_ref, h_ref, lng, lnb, wt, bsb, cw, wg, sc, o_ref):
        i = pl.program_id(0)
        halo = jnp.where(i > 0, h_ref[...].astype(F32), 0.0)
        prm_v = (lng[...], lnb[...], wt[...], bsb[...], cw[...], wg[...], sc[...])
        ya, yb, yc, _ = _mixer_forward_math(z_ref[...].astype(F32), halo, i * tm, prm_v, tm)
        o_ref[...] = jnp.concatenate([ya, yb, yc], axis=1).astype(BF16)

    return pl.pallas_call(body, out_shape=jax.ShapeDtypeStruct((s, 1536), BF16), grid=(s // tm,),
                          in_specs=[main, prev] + [_full(p.shape) for p in prm],
                          out_specs=pl.BlockSpec((tm, 1536), lambda i: (i, 0)), name=name,
                          compiler_params=_params(("parallel",)))(proj, proj, *prm)


def _mixer_bwd(proj, dypre, dproj_in, prm, wtril_t, name):
    s = proj.shape[0]
    tm = _blk(s, 256)
    nblk = s // tm
    nb16 = tm // HALO
    main, prev = _mixer_specs(tm, s)
    nxt_row = lambda i: jnp.minimum((i + 1) * nb16, s // HALO - 1)
    next_dy = pl.BlockSpec((HALO, 1536), lambda i: (nxt_row(i), 0))
    next_gb = pl.BlockSpec((HALO, 512), lambda i: (nxt_row(i), 2))
    ones8 = jnp.ones((8, CHUNK), F32)

    def body(z_ref, h_ref, dy_ref, ndy_ref, ngb_ref, _alias, lng, lnb, wt, bsb, cw, wg, sc, wtt, ones_ref,
             dz_ref, dlng_ref, dlnb_ref, dws_ref, dbs_ref, dcw_ref, dwg_ref, dsc_ref):
        i = pl.program_id(0)
        first = i == 0
        halo = jnp.where(i > 0, h_ref[...].astype(F32), 0.0)
        prm_v = (lng[...], lnb[...], wt[...], bsb[...], cw[...], wg[...], sc[...])
        _, _, _, sv = _mixer_forward_math(z_ref[...].astype(F32), halo, i * tm, prm_v, tm)
        dy = dy_ref[...].astype(F32)
        dya, dyb, dyc = dy[:, 0:512], dy[:, 512:1024], dy[:, 1024:1536]
        not_last = i < nblk - 1
        ndy = jnp.where(not_last, ndy_ref[...].astype(F32), 0.0)
        ngb = ngb_ref[...].astype(F32)

        def accumulate(ref, val):
            @pl.when(first)
            def _():
                ref[...] = val

            @pl.when(jnp.logical_not(first))
            def _():
                ref[...] += val

        du = dya * sv["mixed"]
        dmixed = dya * sv["u"]
        vn = sv["vn"]
        wttv = wtt[...]
        dvn_rows = []
        dws_parts = [None] * A_GROUPS
        dbs_parts = [None] * A_GROUPS
        for c in range(tm // CHUNK):
            cols = []
            for g in range(A_GROUPS):
                dm = dmixed[c * CHUNK:(c + 1) * CHUNK, g * CHUNK:(g + 1) * CHUNK]
                dmb = dm.astype(BF16)
                vb = vn[c * CHUNK:(c + 1) * CHUNK, g * CHUNK:(g + 1) * CHUNK]
                w_part = lax.dot_general(dmb, vb, (((1,), (1,)), ((), ())), preferred_element_type=F32)
                b_part = lax.dot_general(ones_ref[...], dm, (((1,), (1,)), ((), ())), preferred_element_type=F32,
                                         precision=lax.Precision.HIGHEST)[0:1, :]
                dws_parts[g] = w_part if dws_parts[g] is None else dws_parts[g] + w_part
                dbs_parts[g] = b_part if dbs_parts[g] is None else dbs_parts[g] + b_part
                cols.append(jnp.dot(wttv[g], dmb, preferred_element_type=F32))
            dvn_rows.append(jnp.concatenate(cols, axis=1))
        dvn = jnp.concatenate(dvn_rows, axis=0)
        tri = lax.broadcasted_iota(jnp.int32, (CHUNK, CHUNK), 0) >= lax.broadcasted_iota(jnp.int32, (CHUNK, CHUNK), 1)
        accumulate(dws_ref, jnp.stack([jnp.where(tri, w, 0.0) for w in dws_parts], axis=0))
        accumulate(dbs_ref, jnp.concatenate(dbs_parts, axis=0))
        vhat = sv["vhat"]
        accumulate(dlng_ref, jnp.sum(dvn * vhat, axis=0, keepdims=True))
        accumulate(dlnb_ref, jnp.sum(dvn, axis=0, keepdims=True))
        dvhat = dvn * lng[...]
        dv = sv["rstd"] * (dvhat - jnp.mean(dvhat, axis=-1, keepdims=True)
                           - vhat * jnp.mean(dvhat * vhat, axis=-1, keepdims=True))
        za = sv["za"]
        dgelu = sv["cdf"] + za * jnp.exp(-0.5 * za * za) * (1.0 / math.sqrt(2.0 * math.pi))
        dza = jnp.concatenate([du, dv], axis=1) * dgelu
        cwv = cw[...]
        dconv = dyb * sv["gb"]
        dgb = dyb * sv["conv"]
        dcext = jnp.concatenate([dconv, ndy[:, 512:1024] * ngb], axis=0)
        d1 = _shift_up(dcext, 1)[:tm]
        d2 = _shift_up(dcext, 2)[:tm]
        dyy = cwv[2:3, :] * dconv + cwv[1:2, :] * d1 + cwv[0:1, :] * d2
        dgc = dyy * sv["xin"]
        dxin = dyy * sv["gc"]
        accumulate(dcw_ref, jnp.concatenate([jnp.sum(dconv * sv["y2"], axis=0, keepdims=True),
                                             jnp.sum(dconv * sv["y1"], axis=0, keepdims=True),
                                             jnp.sum(dconv * sv["y"], axis=0, keepdims=True)], axis=0))
        scv = sc[...]
        accumulate(dsc_ref, jnp.sum(dyc * sv["p"], axis=0, keepdims=True))
        dp = dyc * scv
        ndp = ndy[:, 1024:1536] * scv
        wgv = wg[...]
        dzc_cols, dwg_parts = [], []
        for gi, win in enumerate(POOL_WINDOWS):
            sl = slice(gi * C_GROUP, (gi + 1) * C_GROUP)
            dpb = dp[:, sl].astype(BF16)
            dpool = lax.dot_general(dpb, wgv[gi], (((1,), (1,)), ((), ())), preferred_element_type=F32)
            ndpool = lax.dot_general(ndp[:, sl].astype(BF16), wgv[gi], (((1,), (1,)), ((), ())),
                                     preferred_element_type=F32)
            dwg_parts.append(lax.dot_general(sv["pooled"][gi].astype(BF16), dpb, (((0,), (0,)), ((), ())),
                                             preferred_element_type=F32))
            cnt = jnp.minimum(sv["t"] + 1, win).astype(F32)
            fw = jnp.concatenate([dpool / cnt, ndpool * (1.0 / win)], axis=0)
            for step in range(gi + 1):
                fw = fw + _shift_up(fw, 2 ** step)
            dzc_cols.append(fw[:tm] - dpool)
        accumulate(dwg_ref, jnp.stack(dwg_parts, axis=0))
        dzc = jnp.concatenate(dzc_cols, axis=1)
        dz_ref[...] = jnp.concatenate([dza, dgb, dgc, dxin, dzc], axis=1).astype(BF16)

    out_shape = (jax.ShapeDtypeStruct(dproj_in.shape, BF16), jax.ShapeDtypeStruct((1, A_WIDTH), F32),
                 jax.ShapeDtypeStruct((1, A_WIDTH), F32), jax.ShapeDtypeStruct((A_GROUPS, CHUNK, CHUNK), F32),
                 jax.ShapeDtypeStruct((A_GROUPS, CHUNK), F32), jax.ShapeDtypeStruct((3, B_WIDTH), F32),
                 jax.ShapeDtypeStruct((4, C_GROUP, C_GROUP), F32), jax.ShapeDtypeStruct((1, C_WIDTH), F32))
    out_specs = (main,) + tuple(_full(o.shape) for o in out_shape[1:])
    in_specs = [main, prev, pl.BlockSpec((tm, 1536), lambda i: (i, 0)), next_dy, next_gb, ANY] + \
               [_full(p.shape) for p in prm] + [_full(wtril_t.shape), _full(ones8.shape)]
    return pl.pallas_call(body, out_shape=out_shape, grid=(nblk,), in_specs=in_specs, out_specs=out_specs,
                          input_output_aliases={5: 0}, name=name,
                          compiler_params=_params(("arbitrary",)))(proj, proj, dypre, dypre, proj, dproj_in, *prm,
                                                                  wtril_t, ones8)


def _sigmoid(x):
    return 1.0 / (1.0 + jnp.exp(-x))


def _merge_fwd(ypre, proj, wa, wb, wc, bgate, name):
    s = ypre.shape[0]
    tm = _blk(s, 512)

    def body(y_ref, zg_ref, wa_ref, wb_ref, wc_ref, bg_ref, o_ref):
        yv = y_ref[...]
        acc = None
        for i, w_ref in enumerate((wa_ref, wb_ref, wc_ref)):
            br = jnp.dot(yv[:, i * 512:(i + 1) * 512], w_ref[...], preferred_element_type=F32)
            gate = _sigmoid(zg_ref[:, i * D_MODEL:(i + 1) * D_MODEL].astype(F32) + bg_ref[i:i + 1, :])
            acc = gate * br if acc is None else acc + gate * br
        o_ref[...] = acc.astype(BF16)

    wspec = _full(wa.shape)
    return pl.pallas_call(body, out_shape=jax.ShapeDtypeStruct((s, D_MODEL), BF16), grid=(s // tm,),
                          in_specs=[pl.BlockSpec((tm, 1536), lambda i: (i, 0)),
                                    pl.BlockSpec((tm, 3 * D_MODEL), lambda i: (i, 1)), wspec, wspec, wspec,
                                    _full(bgate.shape)],
                          out_specs=pl.BlockSpec((tm, D_MODEL), lambda i: (i, 0)), name=name,
                          compiler_params=_params(("parallel",)))(ypre, proj, wa, wb, wc, bgate)


def _merge_bwd(ypre, proj, dmerged, wa, wb, wc, bgate, name):
    s = ypre.shape[0]
    tm = _blk(s, 512)

    def body(y_ref, zg_ref, dm_ref, wa_ref, wb_ref, wc_ref, bg_ref, dyp_ref, dzg_ref, dbg_ref, dwa_ref, dwb_ref,
             dwc_ref):
        first = pl.program_id(0) == 0

        def accumulate(ref, val):
            @pl.when(first)
            def _():
                ref[...] = val

            @pl.when(jnp.logical_not(first))
            def _():
                ref[...] += val

        yv = y_ref[...]
        dm = dm_ref[...].astype(F32)
        dyp, dzg, dbg = [], [], []
        for i, (w_ref, dw_ref) in enumerate(((wa_ref, dwa_ref), (wb_ref, dwb_ref), (wc_ref, dwc_ref))):
            wv = w_ref[...]
            yi = yv[:, i * 512:(i + 1) * 512]
            br = jnp.dot(yi, wv, preferred_element_type=F32)
            gate = _sigmoid(zg_ref[:, i * D_MODEL:(i + 1) * D_MODEL].astype(F32) + bg_ref[i:i + 1, :])
            dbi = (dm * gate).astype(BF16)
            dzi = dm * br * gate * (1.0 - gate)
            dzg.append(dzi.astype(BF16))
            dbg.append(jnp.sum(dzi, axis=0, keepdims=True))
            dyp.append(lax.dot_general(dbi, wv, (((1,), (1,)), ((), ())), preferred_element_type=F32).astype(BF16))
            accumulate(dw_ref, lax.dot_general(yi, dbi, (((0,), (0,)), ((), ())), preferred_element_type=F32))
        dyp_ref[...] = jnp.concatenate(dyp, axis=1)
        dzg_ref[...] = jnp.concatenate(dzg, axis=1)
        accumulate(dbg_ref, jnp.concatenate(dbg, axis=0))

    wspec = _full(wa.shape)
    dw_shape = jax.ShapeDtypeStruct(wa.shape, F32)
    out_shape = (jax.ShapeDtypeStruct((s, 1536), BF16), jax.ShapeDtypeStruct((s, IN_COLS), BF16),
                 jax.ShapeDtypeStruct((3, D_MODEL), F32), dw_shape, dw_shape, dw_shape)
    return pl.pallas_call(body, out_shape=out_shape, grid=(s // tm,),
                          in_specs=[pl.BlockSpec((tm, 1536), lambda i: (i, 0)),
                                    pl.BlockSpec((tm, 3 * D_MODEL), lambda i: (i, 1)),
                                    pl.BlockSpec((tm, D_MODEL), lambda i: (i, 0)), wspec, wspec, wspec,
                                    _full(bgate.shape)],
                          out_specs=(pl.BlockSpec((tm, 1536), lambda i: (i, 0)),
                                     pl.BlockSpec((tm, 3 * D_MODEL), lambda i: (i, 1)), _full((3, D_MODEL)),
                                     wspec, wspec, wspec),
                          name=name, compiler_params=_params(("arbitrary",)))(ypre, proj, dmerged, wa, wb, wc, bgate)


def _softmax_rows(q, k):
    sc = lax.dot_general(q, k, (((1,), (1,)), ((), ())), preferred_element_type=F32) * (HEAD_DIM ** -0.5)
    e = jnp.exp(sc - jnp.max(sc, axis=-1, keepdims=True))
    return e / jnp.sum(e, axis=-1, keepdims=True)


def _attn_fwd(q, kv, name):
    s = q.shape[0]
    tm = _blk(s, 512)

    def body(q_ref, kv_ref, o_ref):
        outs = []
        for h in range(XATTN_HEADS):
            sl = slice(h * HEAD_DIM, (h + 1) * HEAD_DIM)
            p = _softmax_rows(q_ref[:, sl], kv_ref[:, sl])
            outs.append(jnp.dot(p.astype(BF16), kv_ref[:, D_MODEL + h * HEAD_DIM:D_MODEL + (h + 1) * HEAD_DIM],
                                preferred_element_type=F32))
        o_ref[...] = jnp.concatenate(outs, axis=1).astype(BF16)

    return pl.pallas_call(body, out_shape=jax.ShapeDtypeStruct((s, D_MODEL), BF16), grid=(s // tm,),
                          in_specs=[pl.BlockSpec((tm, D_MODEL), lambda i: (i, 0)), _full(kv.shape)],
                          out_specs=pl.BlockSpec((tm, D_MODEL), lambda i: (i, 0)), name=name,
                          compiler_params=_params(("parallel",)))(q, kv)


def _attn_bwd(q, kv, do, name):
    s = q.shape[0]
    tm = _blk(s, 512)

    def body(q_ref, kv_ref, do_ref, dq_ref, dkv_ref):
        dqs, dks, dvs = [], [], []
        for h in range(XATTN_HEADS):
            sl = slice(h * HEAD_DIM, (h + 1) * HEAD_DIM)
            vsl = slice(D_MODEL + h * HEAD_DIM, D_MODEL + (h + 1) * HEAD_DIM)
            qh, kh, vh, doh = q_ref[:, sl], kv_ref[:, sl], kv_ref[:, vsl], do_ref[:, sl]
            p = _softmax_rows(qh, kh)
            pb = p.astype(BF16)
            dvs.append(lax.dot_general(pb, doh, (((0,), (0,)), ((), ())), preferred_element_type=F32))
            dp = lax.dot_general(doh, vh, (((1,), (1,)), ((), ())), preferred_element_type=F32)
            ds = p * (dp - jnp.sum(dp * p, axis=-1, keepdims=True)) * (HEAD_DIM ** -0.5)
            dsb = ds.astype(BF16)
            dqs.append(jnp.dot(dsb, kh, preferred_element_type=F32))
            dks.append(lax.dot_general(dsb, qh, (((0,), (0,)), ((), ())), preferred_element_type=F32))
        dq_ref[...] = jnp.concatenate(dqs, axis=1).astype(BF16)
        part = jnp.concatenate(dks + dvs, axis=1)

        @pl.when(pl.program_id(0) == 0)
        def _():
            dkv_ref[...] = part

        @pl.when(pl.program_id(0) > 0)
        def _():
            dkv_ref[...] += part

    row = pl.BlockSpec((tm, D_MODEL), lambda i: (i, 0))
    return pl.pallas_call(body, out_shape=(jax.ShapeDtypeStruct((s, D_MODEL), BF16),
                                           jax.ShapeDtypeStruct(kv.shape, F32)),
                          grid=(s // tm,), in_specs=[row, _full(kv.shape), row], out_specs=(row, _full(kv.shape)),
                          name=name, compiler_params=_params(("arbitrary",)))(q, kv, do)


def _rows_block(rows, cols):
    target = (512 * 1024) // cols
    fits = [cand for cand in range(8, rows + 1, 8) if rows % cand == 0 and cand <= target]
    return fits[-1] if fits else rows


def _elementwise(fn, ins, out_dtypes, name, place=None, flat=()):
    lead, rows, cols = [a for i, a in enumerate(ins) if i not in flat][0].shape
    tr = _rows_block(rows, cols)
    spec = pl.BlockSpec((None, tr, cols), lambda l, i, *_: (l, i, 0))
    flat_spec = pl.BlockSpec((tr, cols), lambda l, i, *_: (i, 0))
    n_in = len(ins)
    in_specs = [flat_spec if i in flat else spec for i in range(n_in)]
    out_specs = tuple([spec] * len(out_dtypes))
    out_shape = tuple(jax.ShapeDtypeStruct((lead, rows, cols), dt) for dt in out_dtypes)
    sem = _params(("parallel", "parallel"))

    def write(refs, outs):
        for o_ref, o in zip(refs, outs):
            o_ref[...] = o.astype(o_ref.dtype)

    if place is None:
        def body(*refs):
            write(refs[n_in:], fn(*[r[...] for r in refs[:n_in]]))

        return pl.pallas_call(body, out_shape=out_shape, grid=(lead, rows // tr), in_specs=in_specs,
                              out_specs=out_specs, name=name, compiler_params=sem)(*ins)

    def body_placed(p_ref, *refs):
        write(refs[n_in:], fn(p_ref[0], pl.program_id(0), *[r[...] for r in refs[:n_in]]))

    grid_spec = pltpu.PrefetchScalarGridSpec(num_scalar_prefetch=1, grid=(lead, rows // tr), in_specs=in_specs,
                                             out_specs=out_specs)
    return pl.pallas_call(body_placed, out_shape=out_shape, grid_spec=grid_spec, name=name,
                          compiler_params=sem)(place, *ins)


def _cast_to_slots(w, chip, name):
    _, rows, cols = w.shape
    tr = _rows_block(rows, cols)

    def body(chip_ref, w_ref, o0_ref, o1_ref):
        o0_ref[...] = w_ref[0].astype(BF16)
        o1_ref[...] = w_ref[1].astype(BF16)

    slot = pl.BlockSpec((None, tr, cols), lambda i, chip_ref: (chip_ref[0], i, 0))
    grid_spec = pltpu.PrefetchScalarGridSpec(
        num_scalar_prefetch=1, grid=(rows // tr,),
        in_specs=[pl.BlockSpec((DEPTH, tr, cols), lambda i, chip_ref: (0, i, 0))], out_specs=(slot, slot))
    shape = jax.ShapeDtypeStruct((N_CHIPS, rows, cols), BF16)
    return pl.pallas_call(body, out_shape=(shape, shape), grid_spec=grid_spec, name=name,
                          compiler_params=_params(("parallel",)))(chip, w)


def _chip_sum(pair0, rx0, pair1, rx1, chip, name):
    _, rows, cols = pair0.shape
    tr = _rows_block(rows, cols)

    def body(chip_ref, p0_ref, rx0_ref, p1_ref, rx1_ref, o_ref):
        for layer, (p_ref, rx_ref) in enumerate(((p0_ref, rx0_ref), (p1_ref, rx1_ref))):
            acc = p_ref[...].astype(F32)
            for j in range(3):
                acc = acc + rx_ref[j].astype(F32)
            o_ref[layer] = acc

    mine = pl.BlockSpec((None, tr, cols), lambda i, chip_ref: (chip_ref[0], i, 0))
    theirs = pl.BlockSpec((3, tr, cols), lambda i, chip_ref: (0, i, 0))
    grid_spec = pltpu.PrefetchScalarGridSpec(
        num_scalar_prefetch=1, grid=(rows // tr,), in_specs=[mine, theirs, mine, theirs],
        out_specs=pl.BlockSpec((DEPTH, tr, cols), lambda i, chip_ref: (0, i, 0)))
    return pl.pallas_call(body, out_shape=jax.ShapeDtypeStruct((DEPTH, rows, cols), F32), grid_spec=grid_spec,
                          name=name, compiler_params=_params(("parallel",)))(chip, pair0, rx0, pair1, rx1)


def _chip_sum_ordered(own, rx, chip, name):
    rows, cols = own.shape

    def body(chip_ref, own_ref, rx_ref, o_ref):
        me = chip_ref[0]
        acc = None
        for k in range(N_CHIPS):
            rel = jnp.bitwise_xor(me, k)
            term = jnp.where(rel == 0, own_ref[...],
                             jnp.where(rel == 2, rx_ref[0], jnp.where(rel == 1, rx_ref[1], rx_ref[2])))
            acc = term if acc is None else acc + term
        o_ref[...] = acc

    grid_spec = pltpu.PrefetchScalarGridSpec(
        num_scalar_prefetch=1, grid=(1,),
        in_specs=[pl.BlockSpec((rows, cols), lambda i, chip_ref: (0, 0)),
                  pl.BlockSpec((3, rows, cols), lambda i, chip_ref: (0, 0, 0))],
        out_specs=pl.BlockSpec((rows, cols), lambda i, chip_ref: (0, 0)))
    return pl.pallas_call(body, out_shape=jax.ShapeDtypeStruct((rows, cols), F32), grid_spec=grid_spec, name=name,
                          compiler_params=_params(("arbitrary",)))(chip, own, rx)


def _adamw_math(w, g, m, v):
    m = ADAM_B1 * m + (1.0 - ADAM_B1) * g
    v = ADAM_B2 * v + (1.0 - ADAM_B2) * (g * g)
    m_hat = m / (1.0 - ADAM_B1 ** ADAM_STEP)
    v_hat = v / (1.0 - ADAM_B2 ** ADAM_STEP)
    delta = -ADAM_LR * (m_hat / (jnp.sqrt(v_hat) + ADAM_EPS) + ADAM_WD * w)
    return delta, m, v


def _adamw_shard(w, mine, other, m, v, core, name):
    lead, rows, cols = w.shape
    half = rows // 2
    tr = _rows_block(half, cols)
    nhalf = half // tr

    def body(core_ref, w_ref, mine_ref, other_ref, m_ref, v_ref, g_out, d_out, m_out, v_out):
        g = jnp.where(pl.program_id(1) // nhalf == core_ref[0], mine_ref[...], other_ref[...])
        d_new, m_new, v_new = _adamw_math(w_ref[...], g, m_ref[...], v_ref[...])
        g_out[...] = g
        d_out[...] = d_new
        m_out[...] = m_new
        v_out[...] = v_new

    whole = pl.BlockSpec((None, tr, cols), lambda l, i, core_ref: (l, i, 0))
    part = pl.BlockSpec((None, tr, cols), lambda l, i, core_ref: (l, i % nhalf, 0))
    shape = jax.ShapeDtypeStruct(w.shape, F32)
    grid_spec = pltpu.PrefetchScalarGridSpec(num_scalar_prefetch=1, grid=(lead, rows // tr),
                                             in_specs=[whole, part, part, whole, whole], out_specs=(whole,) * 4)
    return pl.pallas_call(body, out_shape=(shape,) * 4, grid_spec=grid_spec, name=name,
                          compiler_params=_params(("parallel", "parallel")))(core, w, mine, other, m, v)


def _place():
    x, y, c = lax.axis_index("x"), lax.axis_index("y"), lax.axis_index("c")
    chips = [(1 - x, y), (x, 1 - y), (1 - x, 1 - y)]
    return x, y, c, 2 * x + y, chips


def _run_remote(copies):
    for send, _ in copies:
        send.start()
    for send, recv in copies:
        send.wait_send()
        recv.wait_recv()


def _half_rows(ref, c):
    half = ref.shape[1] // 2
    return pl.ds(pl.multiple_of(c * half, 8), half)


def _gather_issue(whole):
    def issue(_, refs, ssem, rsem):
        x, y, c, me, chips = _place()
        copies = []
        for w, ref in enumerate(refs):
            part = (me,) if w >= len(refs) - whole else (me, _half_rows(ref, c))
            mine = ref.at[part]
            for j, (px, py) in enumerate(chips):
                sems = (ssem.at[3 * w + j], rsem.at[3 * w + j])
                peer = dict(device_id=(px, py, c), device_id_type=MESH)
                send = pltpu.make_async_remote_copy(mine, mine, *sems, **peer)
                recv = pltpu.make_async_remote_copy(mine, ref.at[(2 * px + py,) + part[1:]], *sems, **peer)
                copies.append((send, recv))
        return copies
    return issue


def _gather_carry(fulls, whole=0):
    return _Carry(arrays=tuple(fulls), outs=tuple(jax.ShapeDtypeStruct(f.shape, f.dtype) for f in fulls),
                  in_place=True, copies=3 * len(fulls), issue=_gather_issue(whole))


def _scatter_issue(ins, outs, ssem, rsem):
    x, y, c, me, chips = _place()
    copies = []
    for w, (src, dst) in enumerate(zip(ins, outs)):
        for j, (px, py) in enumerate(chips):
            cp = pltpu.make_async_remote_copy(src.at[2 * px + py], dst.at[j], ssem.at[3 * w + j], rsem.at[3 * w + j],
                                              device_id=(px, py, c), device_id_type=MESH)
            copies.append((cp, cp))
    return copies


def _scatter_carry(pairs):
    return _Carry(arrays=tuple(pairs), outs=tuple(jax.ShapeDtypeStruct((3,) + p.shape[1:], p.dtype) for p in pairs),
                  in_place=False, copies=3 * len(pairs), issue=_scatter_issue)


def _chip_gather(fulls, whole, name):
    n = len(fulls)
    issue = _gather_issue(whole)

    def body(*refs):
        _run_remote(issue(None, refs[n:2 * n], refs[2 * n], refs[2 * n + 1]))

    out_shape = tuple(jax.ShapeDtypeStruct(f.shape, f.dtype) for f in fulls)
    return pl.pallas_call(body, out_shape=out_shape, in_specs=[ANY] * n, out_specs=tuple([ANY] * n),
                          input_output_aliases={w: w for w in range(n)},
                          scratch_shapes=[pltpu.SemaphoreType.DMA((3 * n,)), pltpu.SemaphoreType.DMA((3 * n,))],
                          name=name)(*fulls)


def _push_gathered(full, place, name):
    _, rows, cols = full.shape
    half = rows // 2
    tr = _rows_block(half, cols)
    nrow = half // tr

    def body(place_ref, blk_ref, out_ref, ssem, rsem):
        x, y, c = lax.axis_index("x"), lax.axis_index("y"), lax.axis_index("c")
        j, i = pl.program_id(0), pl.program_id(1)
        k = jnp.bitwise_xor(place_ref[1], j + 1)
        peer = dict(device_id=(x, y, 1 - c), device_id_type=MESH)
        dst = out_ref.at[pl.ds(k, 1), pl.ds(pl.multiple_of(c * half + i * tr, 8), tr)]
        cp = pltpu.make_async_remote_copy(blk_ref, dst, ssem, rsem, **peer)
        cp.start()
        cp.wait_send()

        @pl.when(jnp.logical_and(j == N_CHIPS - 2, i == nrow - 1))
        def _():
            landed = out_ref.at[pl.ds(0, N_CHIPS - 1), pl.ds(0, half)]
            pltpu.make_async_remote_copy(landed, landed, ssem, rsem, **peer).wait_recv()

    grid_spec = pltpu.PrefetchScalarGridSpec(
        num_scalar_prefetch=1, grid=(N_CHIPS - 1, nrow),
        in_specs=[pl.BlockSpec((1, tr, cols), lambda j, i, place_ref: (jnp.bitwise_xor(place_ref[1], j + 1),
                                                                       place_ref[0] * nrow + i, 0))],
        out_specs=ANY, scratch_shapes=[pltpu.SemaphoreType.DMA, pltpu.SemaphoreType.DMA])
    return pl.pallas_call(body, out_shape=jax.ShapeDtypeStruct(full.shape, full.dtype), grid_spec=grid_spec,
                          input_output_aliases={1: 0}, name=name,
                          compiler_params=_params(("arbitrary", "arbitrary")))(place, full)


def _push_grad_halves(g, core, name):
    lead, rows, cols = g.shape
    half = rows // 2
    tr = _rows_block(half, cols)
    nrow = half // tr

    def body(core_ref, blk_ref, rx_ref, ssem, rsem):
        x, y, c = lax.axis_index("x"), lax.axis_index("y"), lax.axis_index("c")
        k, i = pl.program_id(0), pl.program_id(1)
        peer = dict(device_id=(x, y, 1 - c), device_id_type=MESH)
        dst = rx_ref.at[pl.ds(k, 1), pl.ds(pl.multiple_of(i * tr, 8), tr)]
        cp = pltpu.make_async_remote_copy(blk_ref, dst, ssem, rsem, **peer)
        cp.start()
        cp.wait_send()

        @pl.when(jnp.logical_and(k == lead - 1, i == nrow - 1))
        def _():
            pltpu.make_async_remote_copy(rx_ref, rx_ref, ssem, rsem, **peer).wait_recv()

    grid_spec = pltpu.PrefetchScalarGridSpec(
        num_scalar_prefetch=1, grid=(lead, nrow),
        in_specs=[pl.BlockSpec((1, tr, cols), lambda k, i, core_ref: (k, (1 - core_ref[0]) * nrow + i, 0))],
        out_specs=ANY, scratch_shapes=[pltpu.SemaphoreType.DMA, pltpu.SemaphoreType.DMA])
    return pl.pallas_call(body, out_shape=jax.ShapeDtypeStruct((lead, half, cols), g.dtype), grid_spec=grid_spec,
                          name=name, compiler_params=_params(("arbitrary", "arbitrary")))(core, g)


def _pair_sum(g, rx, core, name):
    lead, half, cols = rx.shape
    tr = _rows_block(half, cols)
    nrow = half // tr

    def body(core_ref, g_ref, rx_ref, o_ref):
        o_ref[...] = (g_ref[...].astype(F32) + rx_ref[...].astype(F32)).astype(o_ref.dtype)

    spec = pl.BlockSpec((None, tr, cols), lambda k, i, core_ref: (k, i, 0))
    grid_spec = pltpu.PrefetchScalarGridSpec(
        num_scalar_prefetch=1, grid=(lead, nrow),
        in_specs=[pl.BlockSpec((None, tr, cols), lambda k, i, core_ref: (k, core_ref[0] * nrow + i, 0)), spec],
        out_specs=spec)
    return pl.pallas_call(body, out_shape=jax.ShapeDtypeStruct(rx.shape, rx.dtype), grid_spec=grid_spec, name=name,
                          compiler_params=_params(("parallel", "parallel")))(core, g, rx)


def _push_sibling(src, name):
    lead, rows, cols = src.shape
    tr = _rows_block(rows, cols)
    nrow = rows // tr

    def body(blk_ref, rx_ref, ssem, rsem):
        x, y, c = lax.axis_index("x"), lax.axis_index("y"), lax.axis_index("c")
        k, i = pl.program_id(0), pl.program_id(1)
        peer = dict(device_id=(x, y, 1 - c), device_id_type=MESH)
        dst = rx_ref.at[pl.ds(k, 1), pl.ds(pl.multiple_of(i * tr, 8), tr)]
        cp = pltpu.make_async_remote_copy(blk_ref, dst, ssem, rsem, **peer)
        cp.start()
        cp.wait_send()

        @pl.when(jnp.logical_and(k == lead - 1, i == nrow - 1))
        def _():
            pltpu.make_async_remote_copy(rx_ref, rx_ref, ssem, rsem, **peer).wait_recv()

    return pl.pallas_call(body, out_shape=jax.ShapeDtypeStruct(src.shape, src.dtype), grid=(lead, nrow),
                          in_specs=[pl.BlockSpec((1, tr, cols), lambda k, i: (k, i, 0))], out_specs=ANY,
                          scratch_shapes=[pltpu.SemaphoreType.DMA, pltpu.SemaphoreType.DMA], name=name,
                          compiler_params=_params(("arbitrary", "arbitrary")))(src)


def _chip_scatter(pairs, small, name):
    n = len(pairs)

    def body(*refs):
        ins, small_ref = refs[:n], refs[n]
        outs, small_out = refs[n + 1:2 * n + 1], refs[2 * n + 1]
        ssem, rsem = refs[2 * n + 2:]
        x, y, c, me, chips = _place()
        copies = []
        for w in range(n + 1):
            for j, (px, py) in enumerate(chips):
                src = ins[w].at[2 * px + py] if w < n else small_ref
                dst = (outs[w] if w < n else small_out).at[j]
                cp = pltpu.make_async_remote_copy(src, dst, ssem.at[3 * w + j], rsem.at[3 * w + j],
                                                  device_id=(px, py, c), device_id_type=MESH)
                copies.append((cp, cp))
        _run_remote(copies)

    out_shape = tuple(jax.ShapeDtypeStruct((3,) + it.shape[1:], it.dtype) for it in pairs) + \
                (jax.ShapeDtypeStruct((3,) + small.shape, small.dtype),)
    return pl.pallas_call(body, out_shape=out_shape, in_specs=[ANY] * (n + 1), out_specs=tuple([ANY] * (n + 1)),
                          scratch_shapes=[pltpu.SemaphoreType.DMA((3 * n + 3,)),
                                          pltpu.SemaphoreType.DMA((3 * n + 3,))],
                          name=name)(*pairs, small)


def _pack(arrays):
    flat = jnp.concatenate([a.reshape(-1).astype(F32) for a in arrays])
    pad = (-flat.shape[0]) % (8 * LANES)
    return jnp.pad(flat, (0, pad)).reshape(-1, LANES)


def _unpack(packed, shapes):
    flat = packed.reshape(-1)
    out, off = [], 0
    for shp in shapes:
        size = math.prod(shp)
        out.append(flat[off:off + size].reshape(shp))
        off += size
    return out


def kernel(x, mem, g_mix, w_in, b_gate, a_ln_g, a_ln_b, a_ws, a_bs, b_conv, c_wg, c_scale, w_branch_a, w_branch_b, w_branch_c, w_o, g_xattn, g_mem, w_xq, w_xkv, w_xo, g_ffn, w_ffn_in, w_ffn_out, g_final, loss_target, m_g_mix, m_w_in, m_b_gate, m_a_ln_g, m_a_ln_b, m_a_ws, m_a_bs, m_b_conv, m_c_wg, m_c_scale, m_w_branch_a, m_w_branch_b, m_w_branch_c, m_w_o, m_g_xattn, m_g_mem, m_w_xq, m_w_xkv, m_w_xo, m_g_ffn, m_w_ffn_in, m_w_ffn_out, m_g_final, v_g_mix, v_w_in, v_b_gate, v_a_ln_g, v_a_ln_b, v_a_ws, v_a_bs, v_b_conv, v_c_wg, v_c_scale, v_w_branch_a, v_w_branch_b, v_w_branch_c, v_w_o, v_g_xattn, v_g_mem, v_w_xq, v_w_xkv, v_w_xo, v_g_ffn, v_w_ffn_in, v_w_ffn_out, v_g_final):
    args = locals()
    wts = {n: args[n] for n in WEIGHTS}
    mom = {n: args["m_" + n] for n in WEIGHTS}
    var = {n: args["v_" + n] for n in WEIGHTS}
    xs = x[0]
    mems = mem[0]
    tgt = loss_target[0]
    chip = 2 * lax.axis_index("x") + lax.axis_index("y")

    bias_pack = jnp.zeros((DEPTH, 8, 384), F32)
    bias_pack = bias_pack.at[:, 0:3, 0:256].set(b_gate).at[:, 0:3, 256:384].set(b_conv)
    bias_slots = lax.dynamic_update_slice(jnp.zeros((N_CHIPS, DEPTH, 8, 384), F32), bias_pack[None], (chip, 0, 0, 0))
    core = lax.axis_index("c")
    chip_arr = jnp.reshape(chip, (1,)).astype(jnp.int32)
    core_arr = jnp.reshape(core, (1,)).astype(jnp.int32)
    place_arr = jnp.stack([core, chip]).astype(jnp.int32)
    slots = [_cast_to_slots(wts[n], chip_arr, name=f"cast_{n}") for n in BIG]

    def assemble(gathered):
        fw = {}
        for n, g in zip(BIG, gathered):
            if n in COL_SHARDED:
                fw[n] = jnp.transpose(g, (1, 0, 2)).reshape(1, g.shape[1], N_CHIPS * g.shape[2])
            else:
                fw[n] = g.reshape(1, N_CHIPS * g.shape[1], g.shape[2])
        return fw

    def push_gathered(gathered, l):
        return [_push_gathered(g, place_arr, name=f"l{l}_gather_cores_{n}") for g, n in zip(gathered, BIG)]

    first = _chip_gather([s[0] for s in slots] + [bias_slots], 1, name="l0_gather_chips")
    layer_weights = [assemble(push_gathered(first[:-1], 0)), None]
    biases = first[-1]
    b_gate_full = jnp.transpose(biases[:, :, 0:3, 0:256], (1, 2, 0, 3)).reshape(DEPTH, 3, D_MODEL)
    b_conv_full = jnp.transpose(biases[:, :, 0:3, 256:384], (1, 2, 0, 3)).reshape(DEPTH, 3, B_WIDTH)
    riders = [[i for i, n in enumerate(BIG) if n in RIDE_FIRST], [i for i, n in enumerate(BIG) if n not in RIDE_FIRST]]

    tril = jnp.tril(jnp.ones((CHUNK, CHUNK), bool))

    def mixer_params(l):
        wtril = jnp.where(tril[None], a_ws[l], 0.0)
        prm = (a_ln_g[l][None], a_ln_b[l][None], wtril.astype(BF16),
               jnp.broadcast_to(a_bs[l][:, :, None], (A_GROUPS, CHUNK, CHUNK)), b_conv_full[l],
               c_wg[l].astype(BF16), c_scale[l][None])
        return prm, jnp.swapaxes(wtril, 1, 2).astype(BF16)

    saved = []
    xc = xs
    for l in range(DEPTH):
        sv = {"x0": xc}
        prm, _ = mixer_params(l)
        fw = layer_weights[l]
        ride = [None, None] if l else [_gather_carry([slots[i][1] for i in grp]) for grp in riders]
        out = _norm_mm(xc, g_mix[l][None], fw["w_in"], 0, tn=1536, name=f"l{l}_proj", carry=ride[0])
        sv["h"], sv["proj"] = out[0], out[1]
        sv["ypre"] = _mixer_fwd(sv["proj"], prm, name=f"l{l}_mixers")
        sv["merged"] = _merge_fwd(sv["ypre"], sv["proj"], fw["w_branch_a"][0], fw["w_branch_b"][0],
                                  fw["w_branch_c"][0], b_gate_full[l], name=f"l{l}_merge")
        xc = _mm(sv["merged"], fw["w_o"], b_lead=0, mode="nn", out_dtype=F32, name=f"l{l}_mix_out", tm=1024, tn=1024,
                 tk=1024, res=xc)
        sv["x1"] = xc
        sv["hq"], sv["q"] = _norm_mm(xc, g_xattn[l][None], fw["w_xq"], 0, tn=1024, name=f"l{l}_q")
        sv["m"] = _rms_fwd(mems, g_mem[l][None], name=f"l{l}_mem_norm")
        sv["kv"] = _mm(sv["m"], fw["w_xkv"], b_lead=0, mode="nn", out_dtype=BF16, name=f"l{l}_kv", tm=256, tn=512,
                       tk=1024)
        sv["o"] = _attn_fwd(sv["q"], sv["kv"], name=f"l{l}_attn")
        xc = _mm(sv["o"], fw["w_xo"], b_lead=0, mode="nn", out_dtype=F32, name=f"l{l}_xattn_out", tm=1024, tn=1024,
                 tk=1024, res=xc)
        sv["x2"] = xc
        out2 = _norm_ffn_in(xc, g_ffn[l][None], fw["w_ffn_in"], 0, name=f"l{l}_ffn_in", carry=ride[1])
        sv["h2"], sv["gu"], sv["act"] = out2[0], out2[1], out2[2]
        xc = _mm(sv["act"], fw["w_ffn_out"], b_lead=0, mode="nn", out_dtype=F32, name=f"l{l}_ffn_out", tm=1024,
                 tn=512, tk=FFN_HIDDEN, res=xc)
        if l == 0:
            second = [None] * len(BIG)
            for grp, got in zip(riders, (out[2], out2[3])):
                for i, g in zip(grp, got):
                    second[i] = g
            layer_weights[1] = assemble(push_gathered(second, 1))
        saved.append(sv)

    loss_part, dx, dg_final = _final_loss(xc, g_final[None], tgt, name="final_loss")
    loss = lax.psum(loss_part[0, 0], ("x", "y", "c"))

    pairs = [None] * DEPTH
    received = [None, [None] * len(BIG)]
    small_grads = [None] * DEPTH
    for l in reversed(range(DEPTH)):
        sv = saved[l]
        prm, wtril_t = mixer_params(l)
        gb = {}
        sg = {}
        fw = layer_weights[l]
        ride = [None, None] if l else [_scatter_carry([pairs[1][i] for i in grp]) for grp in riders]
        dgu = _d_act_swiglu(dx, fw["w_ffn_out"], 0, sv["gu"], name=f"l{l}_d_act")
        gb["w_ffn_out"] = _mm(sv["act"], dx, mode="tn", out_dtype=BF16, name=f"l{l}_dw_ffn_out", tm=1408, tn=1024,
                              tk=TOKEN_STEP).reshape(N_CHIPS, FFN_HIDDEN // N_CHIPS, D_MODEL)
        gb["w_ffn_in"] = _mm(sv["h2"], dgu, mode="tn", out_dtype=BF16, name=f"l{l}_dw_ffn_in", tm=1024, tn=FFN_TILE,
                             tk=TOKEN_STEP, b_halves=True, out_shards=N_CHIPS, carry=ride[0])
        dx, sg["g_ffn"] = _mm_nt_norm_bwd(dgu, fw["w_ffn_in"], 0, sv["x2"], g_ffn[l][None], dx, tm=512,
                                          a_halves=True, name=f"l{l}_d_h2")
        do = _mm(dx, fw["w_xo"], b_lead=0, mode="nt", out_dtype=BF16, name=f"l{l}_d_o", tm=1024, tn=1024, tk=1024)
        gb["w_xo"] = _mm(sv["o"], dx, mode="tn", out_dtype=BF16, name=f"l{l}_dw_xo", tm=1024, tn=1024,
                         tk=TOKEN_STEP).reshape(N_CHIPS, D_MODEL // N_CHIPS, D_MODEL)
        dq, dkv = _attn_bwd(sv["q"], sv["kv"], do, name=f"l{l}_d_attn")
        gb["w_xq"] = _mm(sv["hq"], dq, mode="tn", out_dtype=BF16, name=f"l{l}_dw_xq", tm=1024, tn=1024,
                         tk=TOKEN_STEP).reshape(N_CHIPS, D_MODEL // N_CHIPS, D_MODEL)
        dm = _mm(dkv, fw["w_xkv"], b_lead=0, mode="nt", out_dtype=BF16, name=f"l{l}_d_m", tm=256, tn=1024, tk=1024)
        gb["w_xkv"] = _mm(sv["m"], dkv, mode="tn", out_dtype=BF16, name=f"l{l}_dw_xkv", tm=1024, tn=512, tk=256,
                          out_shards=N_CHIPS)
        _, sg["g_mem"] = _rms_bwd(mems, g_mem[l][None], dm, None, name=f"l{l}_d_mem_norm")
        dx, sg["g_xattn"] = _mm_nt_norm_bwd(dq, fw["w_xq"], 0, sv["x1"], g_xattn[l][None], dx, tm=1024,
                                            name=f"l{l}_d_hq")
        dmerged = _mm(dx, fw["w_o"], b_lead=0, mode="nt", out_dtype=BF16, name=f"l{l}_d_merged", tm=1024, tn=1024,
                      tk=1024)
        gb["w_o"] = _mm(sv["merged"], dx, mode="tn", out_dtype=BF16, name=f"l{l}_dw_o", tm=1024, tn=1024,
                        tk=TOKEN_STEP).reshape(N_CHIPS, D_MODEL // N_CHIPS, D_MODEL)
        dypre, dproj, sg["b_gate"], dwa, dwb, dwc = _merge_bwd(sv["ypre"], sv["proj"], dmerged, fw["w_branch_a"][0],
                                                               fw["w_branch_b"][0], fw["w_branch_c"][0],
                                                               b_gate_full[l], name=f"l{l}_d_merge")
        for n, dw in zip(("w_branch_a", "w_branch_b", "w_branch_c"), (dwa, dwb, dwc)):
            gb[n] = jnp.transpose(dw.reshape(A_WIDTH, N_CHIPS, D_MODEL // N_CHIPS), (1, 0, 2)).astype(BF16)
        (dproj, sg["a_ln_g"], sg["a_ln_b"], sg["a_ws"], sg["a_bs"], sg["b_conv"], sg["c_wg"],
         sg["c_scale"]) = _mixer_bwd(sv["proj"], dypre, dproj, prm, wtril_t, name=f"l{l}_d_mixers")
        gb["w_in"] = _mm(sv["h"], dproj, mode="tn", out_dtype=BF16, name=f"l{l}_dw_in", tm=1024, tn=1536,
                         tk=TOKEN_STEP, out_shards=N_CHIPS, carry=ride[1])
        dx, sg["g_mix"] = _mm_nt_norm_bwd(dproj, fw["w_in"], 0, sv["x0"], g_mix[l][None], dx, tm=512,
                                          name=f"l{l}_d_h")
        if l == 0:
            for grp, key in zip(riders, ("w_ffn_in", "w_in")):
                gb[key], got = gb[key]
                for i, r in zip(grp, got):
                    received[1][i] = r
        pairs[l] = [_pair_sum(gb[n], _push_grad_halves(gb[n], core_arr, name=f"l{l}_reduce_cores_{n}"), core_arr,
                              name=f"l{l}_pair_sum_{n}") for n in BIG]
        small_grads[l] = sg
    grad_x = dx[None]

    small_names = [n for n in WEIGHTS if n not in BIG]
    small_full_shapes = {n: ((DEPTH, 3, D_MODEL) if n == "b_gate" else (DEPTH, 3, B_WIDTH) if n == "b_conv"
                             else wts[n].shape) for n in small_names}
    small_local = []
    for n in small_names:
        if n == "g_final":
            small_local.append(dg_final)
        else:
            small_local.append(jnp.stack([small_grads[l][n].reshape(small_full_shapes[n][1:]) for l in range(DEPTH)]))
    small_pack = _pack(small_local)

    nb = len(BIG)
    small_rx = _push_sibling(small_pack[None], name="reduce_cores_small")
    small_pair = _elementwise(lambda a, b: (a + b,), [small_pack[None], small_rx], [F32], name="pair_sum_small")[0][0]
    scattered = _chip_scatter(pairs[0], small_pair, name="l0_reduce_chips")
    received[0] = list(scattered[:nb])
    small_sum = _chip_sum_ordered(small_pair, scattered[nb], chip_arr, name="chip_sum_small")

    out_g, out_d, out_m, out_v = {}, {}, {}, {}
    for i, n in enumerate(BIG):
        mine = _chip_sum(pairs[0][i], received[0][i], pairs[1][i], received[1][i], chip_arr, name=f"chip_sum_{n}")
        other = _push_sibling(mine, name=f"share_cores_{n}")
        out_g[n], out_d[n], out_m[n], out_v[n] = _adamw_shard(wts[n], mine, other, mom[n], var[n], core_arr,
                                                              name=f"adamw_{n}")
    small_g = dict(zip(small_names, _unpack(small_sum, [small_full_shapes[n] for n in small_names])))
    small_g["b_gate"] = lax.dynamic_slice_in_dim(small_g["b_gate"], chip * 256, 256, axis=2)
    small_g["b_conv"] = lax.dynamic_slice_in_dim(small_g["b_conv"], chip * 128, 128, axis=2)
    packs = [_pack([d[n] for n in small_names])[None] for d in (wts, small_g, mom, var)]
    upd = _elementwise(_adamw_math, packs, [F32, F32, F32], name="adamw_small")
    shapes = [wts[n].shape for n in small_names]
    for n, d, m_new, v_new in zip(small_names, *[_unpack(u[0], shapes) for u in upd]):
        out_g[n], out_d[n], out_m[n], out_v[n] = small_g[n], d, m_new, v_new

    return (loss, grad_x, *[out_g[n] for n in WEIGHTS], *[out_d[n] for n in WEIGHTS], *[out_m[n] for n in WEIGHTS],
            *[out_v[n] for n in WEIGHTS])
```

```python
import functools
import math
from typing import Callable, NamedTuple

import jax
import jax.numpy as jnp
from jax import lax
from jax.experimental import pallas as pl
from jax.experimental.pallas import tpu as pltpu

F32 = jnp.float32
BF16 = jnp.bfloat16

D_MODEL = 1024
DEPTH = 2
MEM_LEN = 256
EPS = 1e-6
CHUNK = 128
A_GROUPS = 4
A_WIDTH = 512
B_WIDTH = 512
C_WIDTH = 512
C_GROUP = 128
POOL_WINDOWS = (2, 4, 8, 16)
HALO = 16
IN_COLS = 6144
MIX_COLS = 3072
XATTN_HEADS = 4
HEAD_DIM = 256
FFN_HIDDEN = 2816
N_CHIPS = 4

ADAM_LR = 0.001
ADAM_B1 = 0.9
ADAM_B2 = 0.999
ADAM_EPS = 1e-08
ADAM_WD = 0.01
ADAM_STEP = 10

V7X_VMEM_BYTES = 64 * 1024 * 1024
VMEM_LIMIT = (V7X_VMEM_BYTES * 3) // 4
LANES = 128
MESH = pl.DeviceIdType.MESH
ANY = pl.BlockSpec(memory_space=pl.ANY)

BIG = ("w_in", "w_branch_a", "w_branch_b", "w_branch_c", "w_o", "w_xq", "w_xkv", "w_xo", "w_ffn_in", "w_ffn_out")
COL_SHARDED = ("w_in", "w_branch_a", "w_branch_b", "w_branch_c", "w_xkv", "w_ffn_in")
RIDE_FIRST = ("w_in", "w_ffn_out", "w_branch_a", "w_branch_b", "w_branch_c")
SMALL_REPL = ("g_mix", "a_ln_g", "a_ln_b", "a_ws", "a_bs", "c_wg", "c_scale", "g_xattn", "g_mem", "g_ffn", "g_final")
SMALL_SHARDED = ("b_gate", "b_conv")
WEIGHTS = ("g_mix", "w_in", "b_gate", "a_ln_g", "a_ln_b", "a_ws", "a_bs", "b_conv", "c_wg", "c_scale", "w_branch_a",
           "w_branch_b", "w_branch_c", "w_o", "g_xattn", "g_mem", "w_xq", "w_xkv", "w_xo", "g_ffn", "w_ffn_in",
           "w_ffn_out", "g_final")


def _params(sem=None):
    return pltpu.CompilerParams(dimension_semantics=sem, vmem_limit_bytes=VMEM_LIMIT)


def _blk(dim, pref):
    return pref if dim % pref == 0 else dim


class _Carry(NamedTuple):
    arrays: tuple
    outs: tuple
    in_place: bool
    copies: int
    issue: Callable


def _call_with_carry(body, *, out_shape, grid, in_specs, out_specs, scratch_shapes, name, semantics, operands, carry):
    if carry is None:
        return pl.pallas_call(body, out_shape=tuple(out_shape), grid=grid, in_specs=list(in_specs),
                              out_specs=tuple(out_specs), scratch_shapes=list(scratch_shapes), name=name,
                              compiler_params=_params(semantics))(*operands)
    n_in, n_out, na, no = len(operands), len(out_shape), len(carry.arrays), len(carry.outs)

    def carried(*refs):
        ins, cins = refs[:n_in], refs[n_in:n_in + na]
        outs, couts = refs[n_in + na:n_in + na + n_out], refs[n_in + na + n_out:n_in + na + n_out + no]
        rest = refs[n_in + na + n_out + no:]
        scratch, ssem, rsem = rest[:-2], rest[-2], rest[-1]
        first = functools.reduce(jnp.logical_and, [pl.program_id(ax) == 0 for ax in range(len(grid))])
        last = functools.reduce(jnp.logical_and, [pl.program_id(ax) == grid[ax] - 1 for ax in range(len(grid))])

        @pl.when(first)
        def _():
            for send, _ in carry.issue(cins, couts, ssem, rsem):
                send.start()

        body(*ins, *outs, *scratch)

        @pl.when(last)
        def _():
            for send, recv in carry.issue(cins, couts, ssem, rsem):
                send.wait_send()
                recv.wait_recv()

    aliases = {n_in + i: n_out + i for i in range(na)} if carry.in_place else {}
    sems = [pltpu.SemaphoreType.DMA((carry.copies,)), pltpu.SemaphoreType.DMA((carry.copies,))]
    return pl.pallas_call(carried, out_shape=tuple(out_shape) + tuple(carry.outs), grid=grid,
                          in_specs=list(in_specs) + [ANY] * na, out_specs=tuple(out_specs) + tuple([ANY] * no),
                          scratch_shapes=list(scratch_shapes) + sems, input_output_aliases=aliases, name=name,
                          compiler_params=_params(("arbitrary",) * len(grid)))(*operands, *carry.arrays)


def _mm(a, b, *, mode, out_dtype, name, tm, tn, tk, res=None, b_lead=None, b_halves=False, out_shards=None,
        carry=None):
    dims = {"nn": (((1,), (0,)), ((), ())), "nt": (((1,), (1,)), ((), ())), "tn": (((0,), (0,)), ((), ()))}[mode]
    b_rows, b_last = b.shape[-2], b.shape[-1]
    if mode == "nn":
        m, k, n = a.shape[0], a.shape[1], b_last
        assert b_rows == k
    elif mode == "nt":
        m, k, n = a.shape[0], a.shape[1], b_rows
        assert b_last == k
    else:
        k, m, n = a.shape[0], a.shape[1], (2 * b_last if b_halves else b_last)
        assert b_rows == k
    tm, tn, tk = _blk(m, tm), _blk(n, tn), _blk(k, tk)
    nk = k // tk
    grid = (m // tm, n // tn, nk)

    if mode == "tn":
        a_spec = pl.BlockSpec((tk, tm), lambda i, j, kk: (kk, i))
        b_block, b_idx = (tk, tn), (lambda i, j, kk: (kk, j))
    else:
        a_spec = pl.BlockSpec((tm, tk), lambda i, j, kk: (i, kk))
        if mode == "nn":
            b_block, b_idx = (tk, tn), (lambda i, j, kk: (kk, j))
        else:
            b_block, b_idx = (tn, tk), (lambda i, j, kk: (j, kk))
    if b_halves:
        assert mode == "tn" and b_lead is None and b_last % tn == 0
        per_half = b_last // tn
        b_spec = pl.BlockSpec((None,) + b_block, lambda i, j, kk: (j // per_half, kk, j % per_half))
    elif b_lead is None:
        b_spec = pl.BlockSpec(b_block, b_idx)
    else:
        b_spec = pl.BlockSpec((None,) + b_block, lambda i, j, kk: (b_lead,) + b_idx(i, j, kk))
    in_specs = [a_spec, b_spec]
    operands = [a, b]
    if res is not None:
        in_specs.append(pl.BlockSpec((tm, tn), lambda i, j, kk: (i, j)))
        operands.append(res)
    if out_shards is None:
        out_shape = jax.ShapeDtypeStruct((m, n), out_dtype)
        out_spec = pl.BlockSpec((tm, tn), lambda i, j, kk: (i, j))
    else:
        per = n // out_shards
        assert per % tn == 0
        nps = per // tn
        out_shape = jax.ShapeDtypeStruct((out_shards, m, per), out_dtype)
        out_spec = pl.BlockSpec((None, tm, tn), lambda i, j, kk: (j // nps, i, j % nps))

    def body(*refs):
        a_ref, b_ref = refs[0], refs[1]
        res_ref = refs[2] if res is not None else None
        o_ref = refs[3] if res is not None else refs[2]
        part = lax.dot_general(a_ref[...].astype(BF16), b_ref[...].astype(BF16), dims, preferred_element_type=F32)

        def finish(acc):
            if res_ref is not None:
                acc = acc + res_ref[...]
            o_ref[...] = acc.astype(out_dtype)

        if nk == 1:
            finish(part)
        else:
            acc_ref = refs[-1]
            kk = pl.program_id(2)

            @pl.when(kk == 0)
            def _():
                acc_ref[...] = part

            @pl.when(kk > 0)
            def _():
                acc_ref[...] += part

            @pl.when(kk == nk - 1)
            def _():
                finish(acc_ref[...])

    scratch = [] if nk == 1 else [pltpu.VMEM((tm, tn), F32)]
    outs = _call_with_carry(body, out_shape=(out_shape,), grid=grid, in_specs=in_specs, out_specs=(out_spec,),
                            scratch_shapes=scratch, name=name, semantics=("parallel", "parallel", "arbitrary"),
                            operands=operands, carry=carry)
    return outs[0] if carry is None else (outs[0], outs[1:])


ROW_CHUNK = 256
FFN_TILE = 1408
TOKEN_STEP = 2048
NT_DIMS = (((1,), (1,)), ((), ()))


def _norm_rows_into(x_ref, g_ref, h_ref, hs_ref, tm):
    rc = min(ROW_CHUNK, tm)
    for r0 in range(0, tm, rc):
        xv = x_ref[r0:r0 + rc, :]
        r = lax.rsqrt(jnp.mean(xv * xv, axis=-1, keepdims=True) + EPS)
        hv = (xv * r * g_ref[...]).astype(BF16)
        hs_ref[r0:r0 + rc, :] = hv
        h_ref[r0:r0 + rc, :] = hv


def _norm_mm(x, g, b, b_lead, *, tn, name, carry=None):
    s, d = x.shape
    n = b.shape[-1]
    tm, tn = _blk(s, 1024), _blk(n, tn)

    def body(x_ref, g_ref, b_ref, h_ref, o_ref, hs_ref):
        @pl.when(pl.program_id(1) == 0)
        def _():
            _norm_rows_into(x_ref, g_ref, h_ref, hs_ref, tm)

        o_ref[...] = jnp.dot(hs_ref[...], b_ref[...], preferred_element_type=F32).astype(BF16)

    row = pl.BlockSpec((tm, d), lambda i, j: (i, 0))
    outs = _call_with_carry(body, out_shape=(jax.ShapeDtypeStruct((s, d), BF16), jax.ShapeDtypeStruct((s, n), BF16)),
                            grid=(s // tm, n // tn),
                            in_specs=[row, pl.BlockSpec((1, d), lambda i, j: (0, 0)),
                                      pl.BlockSpec((None, d, tn), lambda i, j: (b_lead, 0, j))],
                            out_specs=(row, pl.BlockSpec((tm, tn), lambda i, j: (i, j))),
                            scratch_shapes=[pltpu.VMEM((tm, d), BF16)], name=name,
                            semantics=("parallel", "arbitrary"), operands=[x, g, b], carry=carry)
    return outs if carry is None else (outs[0], outs[1], outs[2:])


def _norm_ffn_in(x, g, b, b_lead, name, carry=None):
    s, d = x.shape
    tm, tn = _blk(s, 512), FFN_TILE
    nj = FFN_HIDDEN // tn

    def body(x_ref, g_ref, bg_ref, bu_ref, h_ref, gu_ref, act_ref, hs_ref):
        @pl.when(pl.program_id(1) == 0)
        def _():
            _norm_rows_into(x_ref, g_ref, h_ref, hs_ref, tm)

        hv = hs_ref[...]
        gate = jnp.dot(hv, bg_ref[...], preferred_element_type=F32)
        up = jnp.dot(hv, bu_ref[...], preferred_element_type=F32)
        gu_ref[0] = gate.astype(BF16)
        gu_ref[1] = up.astype(BF16)
        act_ref[...] = (gate * _sigmoid(gate) * up).astype(BF16)

    row = pl.BlockSpec((tm, d), lambda i, j: (i, 0))
    out_shape = (jax.ShapeDtypeStruct((s, d), BF16), jax.ShapeDtypeStruct((2, s, FFN_HIDDEN), BF16),
                 jax.ShapeDtypeStruct((s, FFN_HIDDEN), BF16))
    outs = _call_with_carry(body, out_shape=out_shape, grid=(s // tm, nj),
                            in_specs=[row, pl.BlockSpec((1, d), lambda i, j: (0, 0)),
                                      pl.BlockSpec((None, d, tn), lambda i, j: (b_lead, 0, j)),
                                      pl.BlockSpec((None, d, tn), lambda i, j: (b_lead, 0, j + nj))],
                            out_specs=(row, pl.BlockSpec((2, tm, tn), lambda i, j: (0, i, j)),
                                       pl.BlockSpec((tm, tn), lambda i, j: (i, j))),
                            scratch_shapes=[pltpu.VMEM((tm, d), BF16)], name=name,
                            semantics=("parallel", "arbitrary"), operands=[x, g, b, b], carry=carry)
    return outs if carry is None else (outs[0], outs[1], outs[2], outs[3:])


def _d_act_swiglu(dx, w, b_lead, gu, name):
    s, d = dx.shape
    tm, tn = _blk(s, 512), FFN_TILE

    def body(dx_ref, w_ref, gu_ref, o_ref):
        dact = lax.dot_general(dx_ref[...].astype(BF16), w_ref[...], NT_DIMS, preferred_element_type=F32)
        gate = gu_ref[0].astype(F32)
        up = gu_ref[1].astype(F32)
        sg = _sigmoid(gate)
        o_ref[0] = (dact * up * sg * (1.0 + gate * (1.0 - sg))).astype(BF16)
        o_ref[1] = (dact * gate * sg).astype(BF16)

    halves = pl.BlockSpec((2, tm, tn), lambda j, i: (0, i, j))
    return pl.pallas_call(body, out_shape=jax.ShapeDtypeStruct(gu.shape, BF16), grid=(FFN_HIDDEN // tn, s // tm),
                          in_specs=[pl.BlockSpec((tm, d), lambda j, i: (i, 0)),
                                    pl.BlockSpec((None, tn, d), lambda j, i: (b_lead, j, 0)), halves],
                          out_specs=halves, name=name,
                          compiler_params=_params(("parallel", "parallel")))(dx, w, gu)


def _mm_nt_norm_bwd(a, b, b_lead, x, g, dres, *, tm, name, a_halves=False):
    s, d = x.shape
    kdim = b.shape[-1]
    tm = _blk(s, tm)
    rc = min(ROW_CHUNK, tm)
    if a_halves:
        a_spec = pl.BlockSpec((2, tm, kdim // 2), lambda i: (0, i, 0))
    else:
        a_spec = pl.BlockSpec((tm, kdim), lambda i: (i, 0))

    def body(a_ref, b_ref, x_ref, g_ref, r_ref, dx_ref, dg_ref):
        i = pl.program_id(0)
        av = jnp.concatenate([a_ref[0], a_ref[1]], axis=1) if a_halves else a_ref[...]
        dh = lax.dot_general(av.astype(BF16), b_ref[...], NT_DIMS, preferred_element_type=F32)
        gv = g_ref[...]
        dg_part = None
        for r0 in range(0, tm, rc):
            xv = x_ref[r0:r0 + rc, :]
            dhv = dh[r0:r0 + rc, :]
            r = lax.rsqrt(jnp.mean(xv * xv, axis=-1, keepdims=True) + EPS)
            xhat = xv * r
            p = jnp.sum(dhv * xhat, axis=0, keepdims=True)
            dg_part = p if dg_part is None else dg_part + p
            dxhat = dhv * gv
            dx_ref[r0:r0 + rc, :] = r_ref[r0:r0 + rc, :] + r * (
                dxhat - xhat * jnp.mean(dxhat * xhat, axis=-1, keepdims=True))

        @pl.when(i == 0)
        def _():
            dg_ref[...] = dg_part

        @pl.when(i > 0)
        def _():
            dg_ref[...] += dg_part

    row = pl.BlockSpec((tm, d), lambda i: (i, 0))
    vec = pl.BlockSpec((1, d), lambda i: (0, 0))
    b_spec = pl.BlockSpec((None, d, kdim), lambda i: (b_lead, 0, 0), pipeline_mode=pl.Buffered(1))
    return pl.pallas_call(body, out_shape=(jax.ShapeDtypeStruct((s, d), F32), jax.ShapeDtypeStruct((1, d), F32)),
                          grid=(s // tm,), in_specs=[a_spec, b_spec, row, vec, row], out_specs=(row, vec), name=name,
                          compiler_params=_params(("arbitrary",)))(a, b, x, g, dres)


def _rms_fwd(x, g, name):
    s, d = x.shape
    tm = _blk(s, 512)

    def body(x_ref, g_ref, o_ref):
        xv = x_ref[...]
        r = lax.rsqrt(jnp.mean(xv * xv, axis=-1, keepdims=True) + EPS)
        o_ref[...] = (xv * r * g_ref[...]).astype(BF16)

    return pl.pallas_call(body, out_shape=jax.ShapeDtypeStruct((s, d), BF16), grid=(s // tm,),
                          in_specs=[pl.BlockSpec((tm, d), lambda i: (i, 0)), pl.BlockSpec((1, d), lambda i: (0, 0))],
                          out_specs=pl.BlockSpec((tm, d), lambda i: (i, 0)), name=name,
                          compiler_params=_params(("parallel",)))(x, g)


def _rms_bwd(x, g, dh, dres, name):
    s, d = x.shape
    tm = _blk(s, 512)
    has_res = dres is not None

    def body(*refs):
        x_ref, g_ref, dh_ref = refs[0], refs[1], refs[2]
        dx_ref, dg_ref = refs[-2], refs[-1]
        xv = x_ref[...]
        r = lax.rsqrt(jnp.mean(xv * xv, axis=-1, keepdims=True) + EPS)
        xhat = xv * r
        dhv = dh_ref[...].astype(F32)
        part = jnp.sum(dhv * xhat, axis=0, keepdims=True)

        @pl.when(pl.program_id(0) == 0)
        def _():
            dg_ref[...] = part

        @pl.when(pl.program_id(0) > 0)
        def _():
            dg_ref[...] += part

        dxhat = dhv * g_ref[...]
        dx = r * (dxhat - xhat * jnp.mean(dxhat * xhat, axis=-1, keepdims=True))
        if has_res:
            dx = dx + refs[3][...]
        dx_ref[...] = dx

    row = pl.BlockSpec((tm, d), lambda i: (i, 0))
    vec = pl.BlockSpec((1, d), lambda i: (0, 0))
    in_specs = [row, vec, row] + ([row] if has_res else [])
    operands = [x, g, dh] + ([dres] if has_res else [])
    return pl.pallas_call(body, out_shape=(jax.ShapeDtypeStruct((s, d), F32), jax.ShapeDtypeStruct((1, d), F32)),
                          grid=(s // tm,), in_specs=in_specs, out_specs=(row, vec), name=name,
                          compiler_params=_params(("arbitrary",)))(*operands)


def _final_loss(x, g, target, name):
    s, d = x.shape
    tm = _blk(s, 512)

    def body(x_ref, g_ref, t_ref, loss_ref, dx_ref, dg_ref):
        xv = x_ref[...]
        gv = g_ref[...]
        r = lax.rsqrt(jnp.mean(xv * xv, axis=-1, keepdims=True) + EPS)
        xhat = xv * r
        err = xhat * gv - t_ref[...]
        lpart = 0.5 * jnp.sum(jnp.mean(err * err, axis=-1, keepdims=True), axis=0, keepdims=True)
        dy = err * (1.0 / d)
        gpart = jnp.sum(dy * xhat, axis=0, keepdims=True)

        @pl.when(pl.program_id(0) == 0)
        def _():
            loss_ref[...] = lpart
            dg_ref[...] = gpart

        @pl.when(pl.program_id(0) > 0)
        def _():
            loss_ref[...] += lpart
            dg_ref[...] += gpart

        dxhat = dy * gv
        dx_ref[...] = r * (dxhat - xhat * jnp.mean(dxhat * xhat, axis=-1, keepdims=True))

    row = pl.BlockSpec((tm, d), lambda i: (i, 0))
    vec = pl.BlockSpec((1, d), lambda i: (0, 0))
    one = pl.BlockSpec((1, 1), lambda i: (0, 0))
    return pl.pallas_call(body, out_shape=(jax.ShapeDtypeStruct((1, 1), F32), jax.ShapeDtypeStruct((s, d), F32),
                                           jax.ShapeDtypeStruct((1, d), F32)),
                          grid=(s // tm,), in_specs=[row, vec, row], out_specs=(one, row, vec), name=name,
                          compiler_params=_params(("arbitrary",)))(x, g, target)


def _erf_parts(x):
    cdf = 0.5 * (1.0 + lax.erf(x * (1.0 / math.sqrt(2.0))))
    return cdf


def _shift_down(ext, k):
    return pltpu.roll(ext, k, 0)


def _shift_up(ext, k):
    return pltpu.roll(ext, ext.shape[0] - k, 0)


def _mixer_forward_math(z, halo, row0, prm, tm):
    ln_g, ln_b, wtril, bsb, conv_w, wg, scale = prm
    za = z[:, 0:2 * A_WIDTH]
    gb = z[:, 1024:1536]
    gc = z[:, 1536:2048]
    xin = z[:, 2048:2560]
    zc = z[:, 2560:3072]
    cdf = _erf_parts(za)
    act = za * cdf
    u = act[:, :A_WIDTH]
    v = act[:, A_WIDTH:]
    mu = jnp.mean(v, axis=-1, keepdims=True)
    vc = v - mu
    rstd = lax.rsqrt(jnp.mean(vc * vc, axis=-1, keepdims=True) + EPS)
    vhat = vc * rstd
    vn = (vhat * ln_g + ln_b).astype(BF16)
    rows = []
    for c in range(tm // CHUNK):
        cols = []
        for g in range(A_GROUPS):
            blk = vn[c * CHUNK:(c + 1) * CHUNK, g * CHUNK:(g + 1) * CHUNK]
            cols.append(jnp.dot(wtril[g], blk, preferred_element_type=F32) + bsb[g])
        rows.append(jnp.concatenate(cols, axis=1))
    mixed = jnp.concatenate(rows, axis=0)
    ya = u * mixed
    y = gc * xin
    yext = jnp.concatenate([halo[:, 0:512] * halo[:, 512:1024], y], axis=0)
    y1 = _shift_down(yext, 1)[HALO:]
    y2 = _shift_down(yext, 2)[HALO:]
    conv = conv_w[0:1, :] * y2 + conv_w[1:2, :] * y1 + conv_w[2:3, :] * y
    yb = gb * conv
    t = row0 + lax.broadcasted_iota(jnp.int32, (tm, 1), 0)
    zext = jnp.concatenate([halo[:, 1024:1536], zc], axis=0)
    pooled, p = [], []
    for gi, win in enumerate(POOL_WINDOWS):
        sw = zext[:, gi * C_GROUP:(gi + 1) * C_GROUP]
        for step in range(gi + 1):
            sw = sw + _shift_down(sw, 2 ** step)
        cnt = jnp.minimum(t + 1, win).astype(F32)
        pg = sw[HALO:] / cnt - zc[:, gi * C_GROUP:(gi + 1) * C_GROUP]
        pooled.append(pg)
        p.append(jnp.dot(pg.astype(BF16), wg[gi], preferred_element_type=F32))
    p = jnp.concatenate(p, axis=1)
    yc = p * scale
    saved = dict(za=za, cdf=cdf, u=u, mixed=mixed, rstd=rstd, vhat=vhat, vn=vn, gb=gb, gc=gc, xin=xin, y=y, y1=y1, y2=y2,
                 conv=conv, pooled=pooled, p=p, t=t)
    return ya, yb, yc, saved


def _mixer_specs(tm, s):
    nb16 = tm // HALO
    main = pl.BlockSpec((tm, MIX_COLS), lambda i: (i, 0))
    prev = pl.BlockSpec((HALO, 1536), lambda i: (jnp.maximum(i * nb16 - 1, 0), 1))
    return main, prev


def _full(shape):
    n = len(shape)
    return pl.BlockSpec(shape, lambda i: (0,) * n)


def _mixer_fwd(proj, prm, name):
    s = proj.shape[0]
    tm = _blk(s, 256)
    main, prev = _mixer_specs(tm, s)

    def body(z_ref, h_ref, lng, lnb, wt, bsb, cw, wg, sc, o_ref):
        i = pl.program_id(0)
        halo = jnp.where(i > 0, h_ref[...].astype(F32), 0.0)
        prm_v = (lng[...], lnb[...], wt[...], bsb[...], cw[...], wg[...], sc[...])
        ya, yb, yc, _ = _mixer_forward_math(z_ref[...].astype(F32), halo, i * tm, prm_v, tm)
        o_ref[...] = jnp.concatenate([ya, yb, yc], axis=1).astype(BF16)

    return pl.pallas_call(body, out_shape=jax.ShapeDtypeStruct((s, 1536), BF16), grid=(s // tm,),
                          in_specs=[main, prev] + [_full(p.shape) for p in prm],
                          out_specs=pl.BlockSpec((tm, 1536), lambda i: (i, 0)), name=name,
                          compiler_params=_params(("parallel",)))(proj, proj, *prm)


def _mixer_bwd(proj, dypre, dproj_in, prm, wtril_t, name):
    s = proj.shape[0]
    tm = _blk(s, 256)
    nblk = s // tm
    nb16 = tm // HALO
    main, prev = _mixer_specs(tm, s)
    nxt_row = lambda i: jnp.minimum((i + 1) * nb16, s // HALO - 1)
    next_dy = pl.BlockSpec((HALO, 1536), lambda i: (nxt_row(i), 0))
    next_gb = pl.BlockSpec((HALO, 512), lambda i: (nxt_row(i), 2))
    ones8 = jnp.ones((8, CHUNK), F32)

    def body(z_ref, h_ref, dy_ref, ndy_ref, ngb_ref, _alias, lng, lnb, wt, bsb, cw, wg, sc, wtt, ones_ref,
             dz_ref, dlng_ref, dlnb_ref, dws_ref, dbs_ref, dcw_ref, dwg_ref, dsc_ref):
        i = pl.program_id(0)
        first = i == 0
        halo = jnp.where(i > 0, h_ref[...].astype(F32), 0.0)
        prm_v = (lng[...], lnb[...], wt[...], bsb[...], cw[...], wg[...], sc[...])
        _, _, _, sv = _mixer_forward_math(z_ref[...].astype(F32), halo, i * tm, prm_v, tm)
        dy = dy_ref[...].astype(F32)
        dya, dyb, dyc = dy[:, 0:512], dy[:, 512:1024], dy[:, 1024:1536]
        not_last = i < nblk - 1
        ndy = jnp.where(not_last, ndy_ref[...].astype(F32), 0.0)
        ngb = ngb_ref[...].astype(F32)

        def accumulate(ref, val):
            @pl.when(first)
            def _():
                ref[...] = val

            @pl.when(jnp.logical_not(first))
            def _():
                ref[...] += val

        du = dya * sv["mixed"]
        dmixed = dya * sv["u"]
        vn = sv["vn"]
        wttv = wtt[...]
        dvn_rows = []
        dws_parts = [None] * A_GROUPS
        dbs_parts = [None] * A_GROUPS
        for c in range(tm // CHUNK):
            cols = []
            for g in range(A_GROUPS):
                dm = dmixed[c * CHUNK:(c + 1) * CHUNK, g * CHUNK:(g + 1) * CHUNK]
                dmb = dm.astype(BF16)
                vb = vn[c * CHUNK:(c + 1) * CHUNK, g * CHUNK:(g + 1) * CHUNK]
                w_part = lax.dot_general(dmb, vb, (((1,), (1,)), ((), ())), preferred_element_type=F32)
                b_part = lax.dot_general(ones_ref[...], dm, (((1,), (1,)), ((), ())), preferred_element_type=F32,
                                         precision=lax.Precision.HIGHEST)[0:1, :]
                dws_parts[g] = w_part if dws_parts[g] is None else dws_parts[g] + w_part
                dbs_parts[g] = b_part if dbs_parts[g] is None else dbs_parts[g] + b_part
                cols.append(jnp.dot(wttv[g], dmb, preferred_element_type=F32))
            dvn_rows.append(jnp.concatenate(cols, axis=1))
        dvn = jnp.concatenate(dvn_rows, axis=0)
        tri = lax.broadcasted_iota(jnp.int32, (CHUNK, CHUNK), 0) >= lax.broadcasted_iota(jnp.int32, (CHUNK, CHUNK), 1)
        accumulate(dws_ref, jnp.stack([jnp.where(tri, w, 0.0) for w in dws_parts], axis=0))
        accumulate(dbs_ref, jnp.concatenate(dbs_parts, axis=0))
        vhat = sv["vhat"]
        accumulate(dlng_ref, jnp.sum(dvn * vhat, axis=0, keepdims=True))
        accumulate(dlnb_ref, jnp.sum(dvn, axis=0, keepdims=True))
        dvhat = dvn * lng[...]
        dv = sv["rstd"] * (dvhat - jnp.mean(dvhat, axis=-1, keepdims=True)
                           - vhat * jnp.mean(dvhat * vhat, axis=-1, keepdims=True))
        za = sv["za"]
        dgelu = sv["cdf"] + za * jnp.exp(-0.5 * za * za) * (1.0 / math.sqrt(2.0 * math.pi))
        dza = jnp.concatenate([du, dv], axis=1) * dgelu
        cwv = cw[...]
        dconv = dyb * sv["gb"]
        dgb = dyb * sv["conv"]
        dcext = jnp.concatenate([dconv, ndy[:, 512:1024] * ngb], axis=0)
        d1 = _shift_up(dcext, 1)[:tm]
        d2 = _shift_up(dcext, 2)[:tm]
        dyy = cwv[2:3, :] * dconv + cwv[1:2, :] * d1 + cwv[0:1, :] * d2
        dgc = dyy * sv["xin"]
        dxin = dyy * sv["gc"]
        accumulate(dcw_ref, jnp.concatenate([jnp.sum(dconv * sv["y2"], axis=0, keepdims=True),
                                             jnp.sum(dconv * sv["y1"], axis=0, keepdims=True),
                                             jnp.sum(dconv * sv["y"], axis=0, keepdims=True)], axis=0))
        scv = sc[...]
        accumulate(dsc_ref, jnp.sum(dyc * sv["p"], axis=0, keepdims=True))
        dp = dyc * scv
        ndp = ndy[:, 1024:1536] * scv
        wgv = wg[...]
        dzc_cols, dwg_parts = [], []
        for gi, win in enumerate(POOL_WINDOWS):
            sl = slice(gi * C_GROUP, (gi + 1) * C_GROUP)
            dpb = dp[:, sl].astype(BF16)
            dpool = lax.dot_general(dpb, wgv[gi], (((1,), (1,)), ((), ())), preferred_element_type=F32)
            ndpool = lax.dot_general(ndp[:, sl].astype(BF16), wgv[gi], (((1,), (1,)), ((), ())),
                                     preferred_element_type=F32)
            dwg_parts.append(lax.dot_general(sv["pooled"][gi].astype(BF16), dpb, (((0,), (0,)), ((), ())),
                                             preferred_element_type=F32))
            cnt = jnp.minimum(sv["t"] + 1, win).astype(F32)
            fw = jnp.concatenate([dpool / cnt, ndpool * (1.0 / win)], axis=0)
            for step in range(gi + 1):
                fw = fw + _shift_up(fw, 2 ** step)
            dzc_cols.append(fw[:tm] - dpool)
        accumulate(dwg_ref, jnp.stack(dwg_parts, axis=0))
        dzc = jnp.concatenate(dzc_cols, axis=1)
        dz_ref[...] = jnp.concatenate([dza, dgb, dgc, dxin, dzc], axis=1).astype(BF16)

    out_shape = (jax.ShapeDtypeStruct(dproj_in.shape, BF16), jax.ShapeDtypeStruct((1, A_WIDTH), F32),
                 jax.ShapeDtypeStruct((1, A_WIDTH), F32), jax.ShapeDtypeStruct((A_GROUPS, CHUNK, CHUNK), F32),
                 jax.ShapeDtypeStruct((A_GROUPS, CHUNK), F32), jax.ShapeDtypeStruct((3, B_WIDTH), F32),
                 jax.ShapeDtypeStruct((4, C_GROUP, C_GROUP), F32), jax.ShapeDtypeStruct((1, C_WIDTH), F32))
    out_specs = (main,) + tuple(_full(o.shape) for o in out_shape[1:])
    in_specs = [main, prev, pl.BlockSpec((tm, 1536), lambda i: (i, 0)), next_dy, next_gb, ANY] + \
               [_full(p.shape) for p in prm] + [_full(wtril_t.shape), _full(ones8.shape)]
    return pl.pallas_call(body, out_shape=out_shape, grid=(nblk,), in_specs=in_specs, out_specs=out_specs,
                          input_output_aliases={5: 0}, name=name,
                          compiler_params=_params(("arbitrary",)))(proj, proj, dypre, dypre, proj, dproj_in, *prm,
                                                                  wtril_t, ones8)


def _sigmoid(x):
    return 1.0 / (1.0 + jnp.exp(-x))


def _merge_fwd(ypre, proj, wa, wb, wc, bgate, name):
    s = ypre.shape[0]
    tm = _blk(s, 512)

    def body(y_ref, zg_ref, wa_ref, wb_ref, wc_ref, bg_ref, o_ref):
        yv = y_ref[...]
        acc = None
        for i, w_ref in enumerate((wa_ref, wb_ref, wc_ref)):
            br = jnp.dot(yv[:, i * 512:(i + 1) * 512], w_ref[...], preferred_element_type=F32)
            gate = _sigmoid(zg_ref[:, i * D_MODEL:(i + 1) * D_MODEL].astype(F32) + bg_ref[i:i + 1, :])
            acc = gate * br if acc is None else acc + gate * br
        o_ref[...] = acc.astype(BF16)

    wspec = _full(wa.shape)
    return pl.pallas_call(body, out_shape=jax.ShapeDtypeStruct((s, D_MODEL), BF16), grid=(s // tm,),
                          in_specs=[pl.BlockSpec((tm, 1536), lambda i: (i, 0)),
                                    pl.BlockSpec((tm, 3 * D_MODEL), lambda i: (i, 1)), wspec, wspec, wspec,
                                    _full(bgate.shape)],
                          out_specs=pl.BlockSpec((tm, D_MODEL), lambda i: (i, 0)), name=name,
                          compiler_params=_params(("parallel",)))(ypre, proj, wa, wb, wc, bgate)


def _merge_bwd(ypre, proj, dmerged, wa, wb, wc, bgate, name):
    s = ypre.shape[0]
    tm = _blk(s, 512)

    def body(y_ref, zg_ref, dm_ref, wa_ref, wb_ref, wc_ref, bg_ref, dyp_ref, dzg_ref, dbg_ref, dwa_ref, dwb_ref,
             dwc_ref):
        first = pl.program_id(0) == 0

        def accumulate(ref, val):
            @pl.when(first)
            def _():
                ref[...] = val

            @pl.when(jnp.logical_not(first))
            def _():
                ref[...] += val

        yv = y_ref[...]
        dm = dm_ref[...].astype(F32)
        dyp, dzg, dbg = [], [], []
        for i, (w_ref, dw_ref) in enumerate(((wa_ref, dwa_ref), (wb_ref, dwb_ref), (wc_ref, dwc_ref))):
            wv = w_ref[...]
            yi = yv[:, i * 512:(i + 1) * 512]
            br = jnp.dot(yi, wv, preferred_element_type=F32)
            gate = _sigmoid(zg_ref[:, i * D_MODEL:(i + 1) * D_MODEL].astype(F32) + bg_ref[i:i + 1, :])
            dbi = (dm * gate).astype(BF16)
            dzi = dm * br * gate * (1.0 - gate)
            dzg.append(dzi.astype(BF16))
            dbg.append(jnp.sum(dzi, axis=0, keepdims=True))
            dyp.append(lax.dot_general(dbi, wv, (((1,), (1,)), ((), ())), preferred_element_type=F32).astype(BF16))
            accumulate(dw_ref, lax.dot_general(yi, dbi, (((0,), (0,)), ((), ())), preferred_element_type=F32))
        dyp_ref[...] = jnp.concatenate(dyp, axis=1)
        dzg_ref[...] = jnp.concatenate(dzg, axis=1)
        accumulate(dbg_ref, jnp.concatenate(dbg, axis=0))

    wspec = _full(wa.shape)
    dw_shape = jax.ShapeDtypeStruct(wa.shape, F32)
    out_shape = (jax.ShapeDtypeStruct((s, 1536), BF16), jax.ShapeDtypeStruct((s, IN_COLS), BF16),
                 jax.ShapeDtypeStruct((3, D_MODEL), F32), dw_shape, dw_shape, dw_shape)
    return pl.pallas_call(body, out_shape=out_shape, grid=(s // tm,),
                          in_specs=[pl.BlockSpec((tm, 1536), lambda i: (i, 0)),
                                    pl.BlockSpec((tm, 3 * D_MODEL), lambda i: (i, 1)),
                                    pl.BlockSpec((tm, D_MODEL), lambda i: (i, 0)), wspec, wspec, wspec,
                                    _full(bgate.shape)],
                          out_specs=(pl.BlockSpec((tm, 1536), lambda i: (i, 0)),
                                     pl.BlockSpec((tm, 3 * D_MODEL), lambda i: (i, 1)), _full((3, D_MODEL)),
                                     wspec, wspec, wspec),
                          name=name, compiler_params=_params(("arbitrary",)))(ypre, proj, dmerged, wa, wb, wc, bgate)


def _softmax_rows(q, k):
    sc = lax.dot_general(q, k, (((1,), (1,)), ((), ())), preferred_element_type=F32) * (HEAD_DIM ** -0.5)
    e = jnp.exp(sc - jnp.max(sc, axis=-1, keepdims=True))
    return e / jnp.sum(e, axis=-1, keepdims=True)


def _attn_fwd(q, kv, name):
    s = q.shape[0]
    tm = _blk(s, 512)

    def body(q_ref, kv_ref, o_ref):
        outs = []
        for h in range(XATTN_HEADS):
            sl = slice(h * HEAD_DIM, (h + 1) * HEAD_DIM)
            p = _softmax_rows(q_ref[:, sl], kv_ref[:, sl])
            outs.append(jnp.dot(p.astype(BF16), kv_ref[:, D_MODEL + h * HEAD_DIM:D_MODEL + (h + 1) * HEAD_DIM],
                                preferred_element_type=F32))
        o_ref[...] = jnp.concatenate(outs, axis=1).astype(BF16)

    return pl.pallas_call(body, out_shape=jax.ShapeDtypeStruct((s, D_MODEL), BF16), grid=(s // tm,),
                          in_specs=[pl.BlockSpec((tm, D_MODEL), lambda i: (i, 0)), _full(kv.shape)],
                          out_specs=pl.BlockSpec((tm, D_MODEL), lambda i: (i, 0)), name=name,
                          compiler_params=_params(("parallel",)))(q, kv)


def _attn_bwd(q, kv, do, name):
    s = q.shape[0]
    tm = _blk(s, 512)

    def body(q_ref, kv_ref, do_ref, dq_ref, dkv_ref):
        dqs, dks, dvs = [], [], []
        for h in range(XATTN_HEADS):
            sl = slice(h * HEAD_DIM, (h + 1) * HEAD_DIM)
            vsl = slice(D_MODEL + h * HEAD_DIM, D_MODEL + (h + 1) * HEAD_DIM)
            qh, kh, vh, doh = q_ref[:, sl], kv_ref[:, sl], kv_ref[:, vsl], do_ref[:, sl]
            p = _softmax_rows(qh, kh)
            pb = p.astype(BF16)
            dvs.append(lax.dot_general(pb, doh, (((0,), (0,)), ((), ())), preferred_element_type=F32))
            dp = lax.dot_general(doh, vh, (((1,), (1,)), ((), ())), preferred_element_type=F32)
            ds = p * (dp - jnp.sum(dp * p, axis=-1, keepdims=True)) * (HEAD_DIM ** -0.5)
            dsb = ds.astype(BF16)
            dqs.append(jnp.dot(dsb, kh, preferred_element_type=F32))
            dks.append(lax.dot_general(dsb, qh, (((0,), (0,)), ((), ())), preferred_element_type=F32))
        dq_ref[...] = jnp.concatenate(dqs, axis=1).astype(BF16)
        part = jnp.concatenate(dks + dvs, axis=1)

        @pl.when(pl.program_id(0) == 0)
        def _():
            dkv_ref[...] = part

        @pl.when(pl.program_id(0) > 0)
        def _():
            dkv_ref[...] += part

    row = pl.BlockSpec((tm, D_MODEL), lambda i: (i, 0))
    return pl.pallas_call(body, out_shape=(jax.ShapeDtypeStruct((s, D_MODEL), BF16),
                                           jax.ShapeDtypeStruct(kv.shape, F32)),
                          grid=(s // tm,), in_specs=[row, _full(kv.shape), row], out_specs=(row, _full(kv.shape)),
                          name=name, compiler_params=_params(("arbitrary",)))(q, kv, do)


def _rows_block(rows, cols):
    target = (512 * 1024) // cols
    fits = [cand for cand in range(8, rows + 1, 8) if rows % cand == 0 and cand <= target]
    return fits[-1] if fits else rows


def _elementwise(fn, ins, out_dtypes, name):
    lead, rows, cols = ins[0].shape
    tr = _rows_block(rows, cols)
    spec = pl.BlockSpec((None, tr, cols), lambda l, i: (l, i, 0))
    n_in = len(ins)

    def body(*refs):
        for o_ref, o in zip(refs[n_in:], fn(*[r[...] for r in refs[:n_in]])):
            o_ref[...] = o.astype(o_ref.dtype)

    out_shape = tuple(jax.ShapeDtypeStruct((lead, rows, cols), dt) for dt in out_dtypes)
    return pl.pallas_call(body, out_shape=out_shape, grid=(lead, rows // tr), in_specs=[spec] * n_in,
                          out_specs=tuple([spec] * len(out_dtypes)), name=name,
                          compiler_params=_params(("parallel", "parallel")))(*ins)


def _cast_to_slots(w, chip, name):
    _, rows, cols = w.shape
    tr = _rows_block(rows, cols)

    def body(chip_ref, w_ref, o0_ref, o1_ref):
        o0_ref[...] = w_ref[0].astype(BF16)
        o1_ref[...] = w_ref[1].astype(BF16)

    slot = pl.BlockSpec((None, tr, cols), lambda i, chip_ref: (chip_ref[0], i, 0))
    grid_spec = pltpu.PrefetchScalarGridSpec(
        num_scalar_prefetch=1, grid=(rows // tr,),
        in_specs=[pl.BlockSpec((DEPTH, tr, cols), lambda i, chip_ref: (0, i, 0))], out_specs=(slot, slot))
    shape = jax.ShapeDtypeStruct((N_CHIPS, rows, cols), BF16)
    return pl.pallas_call(body, out_shape=(shape, shape), grid_spec=grid_spec, name=name,
                          compiler_params=_params(("parallel",)))(chip, w)


def _chip_sum(pair0, rx0, pair1, rx1, chip, name):
    _, rows, cols = pair0.shape
    tr = _rows_block(rows, cols)

    def body(chip_ref, p0_ref, rx0_ref, p1_ref, rx1_ref, o_ref):
        for layer, (p_ref, rx_ref) in enumerate(((p0_ref, rx0_ref), (p1_ref, rx1_ref))):
            acc = p_ref[...].astype(F32)
            for j in range(3):
                acc = acc + rx_ref[j].astype(F32)
            o_ref[layer] = acc

    mine = pl.BlockSpec((None, tr, cols), lambda i, chip_ref: (chip_ref[0], i, 0))
    theirs = pl.BlockSpec((3, tr, cols), lambda i, chip_ref: (0, i, 0))
    grid_spec = pltpu.PrefetchScalarGridSpec(
        num_scalar_prefetch=1, grid=(rows // tr,), in_specs=[mine, theirs, mine, theirs],
        out_specs=pl.BlockSpec((DEPTH, tr, cols), lambda i, chip_ref: (0, i, 0)))
    return pl.pallas_call(body, out_shape=jax.ShapeDtypeStruct((DEPTH, rows, cols), F32), grid_spec=grid_spec,
                          name=name, compiler_params=_params(("parallel",)))(chip, pair0, rx0, pair1, rx1)


def _chip_sum_ordered(own, rx, chip, name):
    rows, cols = own.shape

    def body(chip_ref, own_ref, rx_ref, o_ref):
        me = chip_ref[0]
        acc = None
        for k in range(N_CHIPS):
            rel = jnp.bitwise_xor(me, k)
            term = jnp.where(rel == 0, own_ref[...],
                             jnp.where(rel == 2, rx_ref[0], jnp.where(rel == 1, rx_ref[1], rx_ref[2])))
            acc = term if acc is None else acc + term
        o_ref[...] = acc

    grid_spec = pltpu.PrefetchScalarGridSpec(
        num_scalar_prefetch=1, grid=(1,),
        in_specs=[pl.BlockSpec((rows, cols), lambda i, chip_ref: (0, 0)),
                  pl.BlockSpec((3, rows, cols), lambda i, chip_ref: (0, 0, 0))],
        out_specs=pl.BlockSpec((rows, cols), lambda i, chip_ref: (0, 0)))
    return pl.pallas_call(body, out_shape=jax.ShapeDtypeStruct((rows, cols), F32), grid_spec=grid_spec, name=name,
                          compiler_params=_params(("arbitrary",)))(chip, own, rx)


def _adamw_math(w, g, m, v):
    m = ADAM_B1 * m + (1.0 - ADAM_B1) * g
    v = ADAM_B2 * v + (1.0 - ADAM_B2) * (g * g)
    m_hat = m / (1.0 - ADAM_B1 ** ADAM_STEP)
    v_hat = v / (1.0 - ADAM_B2 ** ADAM_STEP)
    delta = -ADAM_LR * (m_hat / (jnp.sqrt(v_hat) + ADAM_EPS) + ADAM_WD * w)
    return delta, m, v


def _adamw_shard(w, mine, other, m, v, core, name):
    lead, rows, cols = w.shape
    half = rows // 2
    tr = _rows_block(half, cols)
    nhalf = half // tr

    def body(core_ref, w_ref, mine_ref, other_ref, m_ref, v_ref, g_out, d_out, m_out, v_out):
        g = jnp.where(pl.program_id(1) // nhalf == core_ref[0], mine_ref[...], other_ref[...])
        d_new, m_new, v_new = _adamw_math(w_ref[...], g, m_ref[...], v_ref[...])
        g_out[...] = g
        d_out[...] = d_new
        m_out[...] = m_new
        v_out[...] = v_new

    whole = pl.BlockSpec((None, tr, cols), lambda l, i, core_ref: (l, i, 0))
    part = pl.BlockSpec((None, tr, cols), lambda l, i, core_ref: (l, i % nhalf, 0))
    shape = jax.ShapeDtypeStruct(w.shape, F32)
    grid_spec = pltpu.PrefetchScalarGridSpec(num_scalar_prefetch=1, grid=(lead, rows // tr),
                                             in_specs=[whole, part, part, whole, whole], out_specs=(whole,) * 4)
    return pl.pallas_call(body, out_shape=(shape,) * 4, grid_spec=grid_spec, name=name,
                          compiler_params=_params(("parallel", "parallel")))(core, w, mine, other, m, v)


def _place():
    x, y, c = lax.axis_index("x"), lax.axis_index("y"), lax.axis_index("c")
    chips = [(1 - x, y), (x, 1 - y), (1 - x, 1 - y)]
    return x, y, c, 2 * x + y, chips


def _run_remote(copies):
    for send, _ in copies:
        send.start()
    for send, recv in copies:
        send.wait_send()
        recv.wait_recv()


def _half_rows(ref, c):
    half = ref.shape[1] // 2
    return pl.ds(pl.multiple_of(c * half, 8), half)


def _gather_issue(whole):
    def issue(_, refs, ssem, rsem):
        x, y, c, me, chips = _place()
        copies = []
        for w, ref in enumerate(refs):
            part = (me,) if w >= len(refs) - whole else (me, _half_rows(ref, c))
            mine = ref.at[part]
            for j, (px, py) in enumerate(chips):
                sems = (ssem.at[3 * w + j], rsem.at[3 * w + j])
                peer = dict(device_id=(px, py, c), device_id_type=MESH)
                send = pltpu.make_async_remote_copy(mine, mine, *sems, **peer)
                recv = pltpu.make_async_remote_copy(mine, ref.at[(2 * px + py,) + part[1:]], *sems, **peer)
                copies.append((send, recv))
        return copies
    return issue


def _gather_carry(fulls, whole=0):
    return _Carry(arrays=tuple(fulls), outs=tuple(jax.ShapeDtypeStruct(f.shape, f.dtype) for f in fulls),
                  in_place=True, copies=3 * len(fulls), issue=_gather_issue(whole))


def _scatter_issue(ins, outs, ssem, rsem):
    x, y, c, me, chips = _place()
    copies = []
    for w, (src, dst) in enumerate(zip(ins, outs)):
        for j, (px, py) in enumerate(chips):
            cp = pltpu.make_async_remote_copy(src.at[2 * px + py], dst.at[j], ssem.at[3 * w + j], rsem.at[3 * w + j],
                                              device_id=(px, py, c), device_id_type=MESH)
            copies.append((cp, cp))
    return copies


def _scatter_carry(pairs):
    return _Carry(arrays=tuple(pairs), outs=tuple(jax.ShapeDtypeStruct((3,) + p.shape[1:], p.dtype) for p in pairs),
                  in_place=False, copies=3 * len(pairs), issue=_scatter_issue)


def _chip_gather(fulls, whole, name):
    n = len(fulls)
    issue = _gather_issue(whole)

    def body(*refs):
        _run_remote(issue(None, refs[n:2 * n], refs[2 * n], refs[2 * n + 1]))

    out_shape = tuple(jax.ShapeDtypeStruct(f.shape, f.dtype) for f in fulls)
    return pl.pallas_call(body, out_shape=out_shape, in_specs=[ANY] * n, out_specs=tuple([ANY] * n),
                          input_output_aliases={w: w for w in range(n)},
                          scratch_shapes=[pltpu.SemaphoreType.DMA((3 * n,)), pltpu.SemaphoreType.DMA((3 * n,))],
                          name=name)(*fulls)


class _Blocks(NamedTuple):
    tile: tuple
    dtype: object
    moves: list
    landed: object


def _rows(start, size):
    return pl.ds(start if isinstance(start, int) else pl.multiple_of(start, 8), size)


def _plan_gathered(_, refs, c, me):
    work = []
    for ref in refs:
        _, rows, cols = ref.shape
        half = rows // 2
        tr = _rows_block(half, cols)
        moves = []
        for j in range(N_CHIPS - 1):
            k = jnp.bitwise_xor(me, j + 1)
            for b in range(half // tr):
                part = ref.at[k, _rows(c * half + b * tr, tr)]
                moves.append((part, part))
        work.append(_Blocks((tr, cols), ref.dtype, moves, ref.at[pl.ds(0, N_CHIPS - 1), pl.ds(0, half)]))
    return work


def _plan_grad_halves(grads, rxs, c, me):
    work = []
    for g, rx in zip(grads, rxs):
        lead, half, cols = rx.shape
        tr = _rows_block(half, cols)
        moves = [(g.at[k, _rows((1 - c) * half + b * tr, tr)], rx.at[k, _rows(b * tr, tr)])
                 for k in range(lead) for b in range(half // tr)]
        work.append(_Blocks((tr, cols), g.dtype, moves, rx))
    return work


def _plan_whole(srcs, dsts, c, me):
    work = []
    for src, dst in zip(srcs, dsts):
        lead, rows, cols = src.shape
        tr = _rows_block(rows, cols)
        moves = [(src.at[k, _rows(b * tr, tr)], dst.at[k, _rows(b * tr, tr)])
                 for k in range(lead) for b in range(rows // tr)]
        work.append(_Blocks((tr, cols), src.dtype, moves, dst))
    return work


def _push_staged(arrays, outs, in_place, plan, name):
    n, no = len(arrays), len(outs)

    def body(*refs):
        ins, out_refs = refs[:n], refs[n:n + no]
        rsem, idle = refs[n + no], refs[n + no + 1]
        x, y, c, me, _ = _place()
        peer = dict(device_id=(x, y, 1 - c), device_id_type=MESH)
        work = plan(ins, out_refs, c, me)
        for w, blocks in enumerate(work):
            def staged(buf, lsem, ssem, w=w, blocks=blocks):
                sends = []
                for i, (src, dst) in enumerate(blocks.moves):
                    slot = i % 2
                    if i >= 2:
                        sends[i - 2].wait_send()
                    load = pltpu.make_async_copy(src, buf.at[slot], lsem.at[slot])
                    load.start()
                    load.wait()
                    send = pltpu.make_async_remote_copy(buf.at[slot], dst, ssem.at[slot], rsem.at[w], **peer)
                    send.start()
                    sends.append(send)
                for send in sends[-2:]:
                    send.wait_send()

            pl.run_scoped(staged, pltpu.VMEM((2,) + blocks.tile, blocks.dtype), pltpu.SemaphoreType.DMA((2,)),
                          pltpu.SemaphoreType.DMA((2,)))
        for w, blocks in enumerate(work):
            pltpu.make_async_remote_copy(blocks.landed, blocks.landed, idle, rsem.at[w], **peer).wait_recv()

    return pl.pallas_call(body, out_shape=tuple(outs), in_specs=[ANY] * n, out_specs=tuple([ANY] * no),
                          input_output_aliases={w: w for w in range(n)} if in_place else {},
                          scratch_shapes=[pltpu.SemaphoreType.DMA((n,)), pltpu.SemaphoreType.DMA],
                          name=name, compiler_params=_params())(*arrays)


def _pair_sum(g, rx, core, name):
    lead, half, cols = rx.shape
    tr = _rows_block(half, cols)
    nrow = half // tr

    def body(core_ref, g_ref, rx_ref, o_ref):
        o_ref[...] = (g_ref[...].astype(F32) + rx_ref[...].astype(F32)).astype(o_ref.dtype)

    spec = pl.BlockSpec((None, tr, cols), lambda k, i, core_ref: (k, i, 0))
    grid_spec = pltpu.PrefetchScalarGridSpec(
        num_scalar_prefetch=1, grid=(lead, nrow),
        in_specs=[pl.BlockSpec((None, tr, cols), lambda k, i, core_ref: (k, core_ref[0] * nrow + i, 0)), spec],
        out_specs=spec)
    return pl.pallas_call(body, out_shape=jax.ShapeDtypeStruct(rx.shape, rx.dtype), grid_spec=grid_spec, name=name,
                          compiler_params=_params(("parallel", "parallel")))(core, g, rx)


def _push_sibling(src, name):
    lead, rows, cols = src.shape
    tr = _rows_block(rows, cols)
    nrow = rows // tr

    def body(blk_ref, rx_ref, ssem, rsem):
        x, y, c = lax.axis_index("x"), lax.axis_index("y"), lax.axis_index("c")
        k, i = pl.program_id(0), pl.program_id(1)
        peer = dict(device_id=(x, y, 1 - c), device_id_type=MESH)
        dst = rx_ref.at[pl.ds(k, 1), pl.ds(pl.multiple_of(i * tr, 8), tr)]
        cp = pltpu.make_async_remote_copy(blk_ref, dst, ssem, rsem, **peer)
        cp.start()
        cp.wait_send()

        @pl.when(jnp.logical_and(k == lead - 1, i == nrow - 1))
        def _():
            pltpu.make_async_remote_copy(rx_ref, rx_ref, ssem, rsem, **peer).wait_recv()

    return pl.pallas_call(body, out_shape=jax.ShapeDtypeStruct(src.shape, src.dtype), grid=(lead, nrow),
                          in_specs=[pl.BlockSpec((1, tr, cols), lambda k, i: (k, i, 0))], out_specs=ANY,
                          scratch_shapes=[pltpu.SemaphoreType.DMA, pltpu.SemaphoreType.DMA], name=name,
                          compiler_params=_params(("arbitrary", "arbitrary")))(src)


def _chip_scatter(pairs, small, name):
    n = len(pairs)

    def body(*refs):
        ins, small_ref = refs[:n], refs[n]
        outs, small_out = refs[n + 1:2 * n + 1], refs[2 * n + 1]
        ssem, rsem = refs[2 * n + 2:]
        x, y, c, me, chips = _place()
        copies = []
        for w in range(n + 1):
            for j, (px, py) in enumerate(chips):
                src = ins[w].at[2 * px + py] if w < n else small_ref
                dst = (outs[w] if w < n else small_out).at[j]
                cp = pltpu.make_async_remote_copy(src, dst, ssem.at[3 * w + j], rsem.at[3 * w + j],
                                                  device_id=(px, py, c), device_id_type=MESH)
                copies.append((cp, cp))
        _run_remote(copies)

    out_shape = tuple(jax.ShapeDtypeStruct((3,) + it.shape[1:], it.dtype) for it in pairs) + \
                (jax.ShapeDtypeStruct((3,) + small.shape, small.dtype),)
    return pl.pallas_call(body, out_shape=out_shape, in_specs=[ANY] * (n + 1), out_specs=tuple([ANY] * (n + 1)),
                          scratch_shapes=[pltpu.SemaphoreType.DMA((3 * n + 3,)),
                                          pltpu.SemaphoreType.DMA((3 * n + 3,))],
                          name=name)(*pairs, small)


def _pack(arrays):
    flat = jnp.concatenate([a.reshape(-1).astype(F32) for a in arrays])
    pad = (-flat.shape[0]) % (8 * LANES)
    return jnp.pad(flat, (0, pad)).reshape(-1, LANES)


def _unpack(packed, shapes):
    flat = packed.reshape(-1)
    out, off = [], 0
    for shp in shapes:
        size = math.prod(shp)
        out.append(flat[off:off + size].reshape(shp))
        off += size
    return out


def kernel(x, mem, g_mix, w_in, b_gate, a_ln_g, a_ln_b, a_ws, a_bs, b_conv, c_wg, c_scale, w_branch_a, w_branch_b, w_branch_c, w_o, g_xattn, g_mem, w_xq, w_xkv, w_xo, g_ffn, w_ffn_in, w_ffn_out, g_final, loss_target, m_g_mix, m_w_in, m_b_gate, m_a_ln_g, m_a_ln_b, m_a_ws, m_a_bs, m_b_conv, m_c_wg, m_c_scale, m_w_branch_a, m_w_branch_b, m_w_branch_c, m_w_o, m_g_xattn, m_g_mem, m_w_xq, m_w_xkv, m_w_xo, m_g_ffn, m_w_ffn_in, m_w_ffn_out, m_g_final, v_g_mix, v_w_in, v_b_gate, v_a_ln_g, v_a_ln_b, v_a_ws, v_a_bs, v_b_conv, v_c_wg, v_c_scale, v_w_branch_a, v_w_branch_b, v_w_branch_c, v_w_o, v_g_xattn, v_g_mem, v_w_xq, v_w_xkv, v_w_xo, v_g_ffn, v_w_ffn_in, v_w_ffn_out, v_g_final):
    args = locals()
    wts = {n: args[n] for n in WEIGHTS}
    mom = {n: args["m_" + n] for n in WEIGHTS}
    var = {n: args["v_" + n] for n in WEIGHTS}
    xs = x[0]
    mems = mem[0]
    tgt = loss_target[0]
    chip = 2 * lax.axis_index("x") + lax.axis_index("y")

    bias_pack = jnp.zeros((DEPTH, 8, 384), F32)
    bias_pack = bias_pack.at[:, 0:3, 0:256].set(b_gate).at[:, 0:3, 256:384].set(b_conv)
    bias_slots = lax.dynamic_update_slice(jnp.zeros((N_CHIPS, DEPTH, 8, 384), F32), bias_pack[None], (chip, 0, 0, 0))
    core = lax.axis_index("c")
    chip_arr = jnp.reshape(chip, (1,)).astype(jnp.int32)
    core_arr = jnp.reshape(core, (1,)).astype(jnp.int32)
    slots = [_cast_to_slots(wts[n], chip_arr, name=f"cast_{n}") for n in BIG]

    def assemble(gathered):
        fw = {}
        for n, g in zip(BIG, gathered):
            if n in COL_SHARDED:
                fw[n] = jnp.transpose(g, (1, 0, 2)).reshape(1, g.shape[1], N_CHIPS * g.shape[2])
            else:
                fw[n] = g.reshape(1, N_CHIPS * g.shape[1], g.shape[2])
        return fw

    def push_gathered(gathered, l):
        return _push_staged(gathered, [jax.ShapeDtypeStruct(g.shape, g.dtype) for g in gathered], True,
                            _plan_gathered, name=f"l{l}_gather_cores")

    first = _chip_gather([s[0] for s in slots] + [bias_slots], 1, name="l0_gather_chips")
    layer_weights = [assemble(push_gathered(first[:-1], 0)), None]
    biases = first[-1]
    b_gate_full = jnp.transpose(biases[:, :, 0:3, 0:256], (1, 2, 0, 3)).reshape(DEPTH, 3, D_MODEL)
    b_conv_full = jnp.transpose(biases[:, :, 0:3, 256:384], (1, 2, 0, 3)).reshape(DEPTH, 3, B_WIDTH)
    riders = [[i for i, n in enumerate(BIG) if n in RIDE_FIRST], [i for i, n in enumerate(BIG) if n not in RIDE_FIRST]]

    tril = jnp.tril(jnp.ones((CHUNK, CHUNK), bool))

    def mixer_params(l):
        wtril = jnp.where(tril[None], a_ws[l], 0.0)
        prm = (a_ln_g[l][None], a_ln_b[l][None], wtril.astype(BF16),
               jnp.broadcast_to(a_bs[l][:, :, None], (A_GROUPS, CHUNK, CHUNK)), b_conv_full[l],
               c_wg[l].astype(BF16), c_scale[l][None])
        return prm, jnp.swapaxes(wtril, 1, 2).astype(BF16)

    saved = []
    xc = xs
    for l in range(DEPTH):
        sv = {"x0": xc}
        prm, _ = mixer_params(l)
        fw = layer_weights[l]
        ride = [None, None] if l else [_gather_carry([slots[i][1] for i in grp]) for grp in riders]
        out = _norm_mm(xc, g_mix[l][None], fw["w_in"], 0, tn=1536, name=f"l{l}_proj", carry=ride[0])
        sv["h"], sv["proj"] = out[0], out[1]
        sv["ypre"] = _mixer_fwd(sv["proj"], prm, name=f"l{l}_mixers")
        sv["merged"] = _merge_fwd(sv["ypre"], sv["proj"], fw["w_branch_a"][0], fw["w_branch_b"][0],
                                  fw["w_branch_c"][0], b_gate_full[l], name=f"l{l}_merge")
        xc = _mm(sv["merged"], fw["w_o"], b_lead=0, mode="nn", out_dtype=F32, name=f"l{l}_mix_out", tm=1024, tn=1024,
                 tk=1024, res=xc)
        sv["x1"] = xc
        sv["hq"], sv["q"] = _norm_mm(xc, g_xattn[l][None], fw["w_xq"], 0, tn=1024, name=f"l{l}_q")
        sv["m"] = _rms_fwd(mems, g_mem[l][None], name=f"l{l}_mem_norm")
        sv["kv"] = _mm(sv["m"], fw["w_xkv"], b_lead=0, mode="nn", out_dtype=BF16, name=f"l{l}_kv", tm=256, tn=512,
                       tk=1024)
        sv["o"] = _attn_fwd(sv["q"], sv["kv"], name=f"l{l}_attn")
        xc = _mm(sv["o"], fw["w_xo"], b_lead=0, mode="nn", out_dtype=F32, name=f"l{l}_xattn_out", tm=1024, tn=1024,
                 tk=1024, res=xc)
        sv["x2"] = xc
        out2 = _norm_ffn_in(xc, g_ffn[l][None], fw["w_ffn_in"], 0, name=f"l{l}_ffn_in", carry=ride[1])
        sv["h2"], sv["gu"], sv["act"] = out2[0], out2[1], out2[2]
        xc = _mm(sv["act"], fw["w_ffn_out"], b_lead=0, mode="nn", out_dtype=F32, name=f"l{l}_ffn_out", tm=1024,
                 tn=512, tk=FFN_HIDDEN, res=xc)
        if l == 0:
            second = [None] * len(BIG)
            for grp, got in zip(riders, (out[2], out2[3])):
                for i, g in zip(grp, got):
                    second[i] = g
            layer_weights[1] = assemble(push_gathered(second, 1))
        saved.append(sv)

    loss_part, dx, dg_final = _final_loss(xc, g_final[None], tgt, name="final_loss")
    loss = lax.psum(loss_part[0, 0], ("x", "y", "c"))

    pairs = [None] * DEPTH
    received = [None, [None] * len(BIG)]
    small_grads = [None] * DEPTH
    for l in reversed(range(DEPTH)):
        sv = saved[l]
        prm, wtril_t = mixer_params(l)
        gb = {}
        sg = {}
        fw = layer_weights[l]
        ride = [None, None] if l else [_scatter_carry([pairs[1][i] for i in grp]) for grp in riders]
        dgu = _d_act_swiglu(dx, fw["w_ffn_out"], 0, sv["gu"], name=f"l{l}_d_act")
        gb["w_ffn_out"] = _mm(sv["act"], dx, mode="tn", out_dtype=BF16, name=f"l{l}_dw_ffn_out", tm=1408, tn=1024,
                              tk=TOKEN_STEP).reshape(N_CHIPS, FFN_HIDDEN // N_CHIPS, D_MODEL)
        gb["w_ffn_in"] = _mm(sv["h2"], dgu, mode="tn", out_dtype=BF16, name=f"l{l}_dw_ffn_in", tm=1024, tn=FFN_TILE,
                             tk=TOKEN_STEP, b_halves=True, out_shards=N_CHIPS, carry=ride[0])
        dx, sg["g_ffn"] = _mm_nt_norm_bwd(dgu, fw["w_ffn_in"], 0, sv["x2"], g_ffn[l][None], dx, tm=512,
                                          a_halves=True, name=f"l{l}_d_h2")
        do = _mm(dx, fw["w_xo"], b_lead=0, mode="nt", out_dtype=BF16, name=f"l{l}_d_o", tm=1024, tn=1024, tk=1024)
        gb["w_xo"] = _mm(sv["o"], dx, mode="tn", out_dtype=BF16, name=f"l{l}_dw_xo", tm=1024, tn=1024,
                         tk=TOKEN_STEP).reshape(N_CHIPS, D_MODEL // N_CHIPS, D_MODEL)
        dq, dkv = _attn_bwd(sv["q"], sv["kv"], do, name=f"l{l}_d_attn")
        gb["w_xq"] = _mm(sv["hq"], dq, mode="tn", out_dtype=BF16, name=f"l{l}_dw_xq", tm=1024, tn=1024,
                         tk=TOKEN_STEP).reshape(N_CHIPS, D_MODEL // N_CHIPS, D_MODEL)
        dm = _mm(dkv, fw["w_xkv"], b_lead=0, mode="nt", out_dtype=BF16, name=f"l{l}_d_m", tm=256, tn=1024, tk=1024)
        gb["w_xkv"] = _mm(sv["m"], dkv, mode="tn", out_dtype=BF16, name=f"l{l}_dw_xkv", tm=1024, tn=512, tk=256,
                          out_shards=N_CHIPS)
        _, sg["g_mem"] = _rms_bwd(mems, g_mem[l][None], dm, None, name=f"l{l}_d_mem_norm")
        dx, sg["g_xattn"] = _mm_nt_norm_bwd(dq, fw["w_xq"], 0, sv["x1"], g_xattn[l][None], dx, tm=1024,
                                            name=f"l{l}_d_hq")
        dmerged = _mm(dx, fw["w_o"], b_lead=0, mode="nt", out_dtype=BF16, name=f"l{l}_d_merged", tm=1024, tn=1024,
                      tk=1024)
        gb["w_o"] = _mm(sv["merged"], dx, mode="tn", out_dtype=BF16, name=f"l{l}_dw_o", tm=1024, tn=1024,
                        tk=TOKEN_STEP).reshape(N_CHIPS, D_MODEL // N_CHIPS, D_MODEL)
        dypre, dproj, sg["b_gate"], dwa, dwb, dwc = _merge_bwd(sv["ypre"], sv["proj"], dmerged, fw["w_branch_a"][0],
                                                               fw["w_branch_b"][0], fw["w_branch_c"][0],
                                                               b_gate_full[l], name=f"l{l}_d_merge")
        for n, dw in zip(("w_branch_a", "w_branch_b", "w_branch_c"), (dwa, dwb, dwc)):
            gb[n] = jnp.transpose(dw.reshape(A_WIDTH, N_CHIPS, D_MODEL // N_CHIPS), (1, 0, 2)).astype(BF16)
        (dproj, sg["a_ln_g"], sg["a_ln_b"], sg["a_ws"], sg["a_bs"], sg["b_conv"], sg["c_wg"],
         sg["c_scale"]) = _mixer_bwd(sv["proj"], dypre, dproj, prm, wtril_t, name=f"l{l}_d_mixers")
        gb["w_in"] = _mm(sv["h"], dproj, mode="tn", out_dtype=BF16, name=f"l{l}_dw_in", tm=1024, tn=1536,
                         tk=TOKEN_STEP, out_shards=N_CHIPS, carry=ride[1])
        dx, sg["g_mix"] = _mm_nt_norm_bwd(dproj, fw["w_in"], 0, sv["x0"], g_mix[l][None], dx, tm=512,
                                          name=f"l{l}_d_h")
        if l == 0:
            for grp, key in zip(riders, ("w_ffn_in", "w_in")):
                gb[key], got = gb[key]
                for i, r in zip(grp, got):
                    received[1][i] = r
        grads = [gb[n] for n in BIG]
        halves = [jax.ShapeDtypeStruct((g.shape[0], g.shape[1] // 2, g.shape[2]), g.dtype) for g in grads]
        came = _push_staged(grads, halves, False, _plan_grad_halves, name=f"l{l}_reduce_cores")
        pairs[l] = [_pair_sum(g, rx, core_arr, name=f"l{l}_pair_sum_{n}") for g, rx, n in zip(grads, came, BIG)]
        small_grads[l] = sg
    grad_x = dx[None]

    small_names = [n for n in WEIGHTS if n not in BIG]
    small_full_shapes = {n: ((DEPTH, 3, D_MODEL) if n == "b_gate" else (DEPTH, 3, B_WIDTH) if n == "b_conv"
                             else wts[n].shape) for n in small_names}
    small_local = []
    for n in small_names:
        if n == "g_final":
            small_local.append(dg_final)
        else:
            small_local.append(jnp.stack([small_grads[l][n].reshape(small_full_shapes[n][1:]) for l in range(DEPTH)]))
    small_pack = _pack(small_local)

    nb = len(BIG)
    small_rx = _push_sibling(small_pack[None], name="reduce_cores_small")
    small_pair = _elementwise(lambda a, b: (a + b,), [small_pack[None], small_rx], [F32], name="pair_sum_small")[0][0]
    scattered = _chip_scatter(pairs[0], small_pair, name="l0_reduce_chips")
    received[0] = list(scattered[:nb])
    small_sum = _chip_sum_ordered(small_pair, scattered[nb], chip_arr, name="chip_sum_small")

    out_g, out_d, out_m, out_v = {}, {}, {}, {}
    mine = [_chip_sum(pairs[0][i], received[0][i], pairs[1][i], received[1][i], chip_arr, name=f"chip_sum_{n}")
            for i, n in enumerate(BIG)]
    other = _push_staged(mine, [jax.ShapeDtypeStruct(g.shape, g.dtype) for g in mine], False, _plan_whole,
                         name="share_cores")
    for i, n in enumerate(BIG):
        out_g[n], out_d[n], out_m[n], out_v[n] = _adamw_shard(wts[n], mine[i], other[i], mom[n], var[n], core_arr,
                                                              name=f"adamw_{n}")
    small_g = dict(zip(small_names, _unpack(small_sum, [small_full_shapes[n] for n in small_names])))
    small_g["b_gate"] = lax.dynamic_slice_in_dim(small_g["b_gate"], chip * 256, 256, axis=2)
    small_g["b_conv"] = lax.dynamic_slice_in_dim(small_g["b_conv"], chip * 128, 128, axis=2)
    packs = [_pack([d[n] for n in small_names])[None] for d in (wts, small_g, mom, var)]
    upd = _elementwise(_adamw_math, packs, [F32, F32, F32], name="adamw_small")
    shapes = [wts[n].shape for n in small_names]
    for n, d, m_new, v_new in zip(small_names, *[_unpack(u[0], shapes) for u in upd]):
        out_g[n], out_d[n], out_m[n], out_v[n] = small_g[n], d, m_new, v_new

    return (loss, grad_x, *[out_g[n] for n in WEIGHTS], *[out_d[n] for n in WEIGHTS], *[out_m[n] for n in WEIGHTS],
            *[out_v[n] for n in WEIGHTS])
```

```python
import functools
import math
from typing import Callable, NamedTuple

import jax
import jax.numpy as jnp
from jax import lax
from jax.experimental import pallas as pl
from jax.experimental.pallas import tpu as pltpu

F32 = jnp.float32
BF16 = jnp.bfloat16

D_MODEL = 1024
DEPTH = 2
MEM_LEN = 256
EPS = 1e-6
CHUNK = 128
A_GROUPS = 4
A_WIDTH = 512
B_WIDTH = 512
C_WIDTH = 512
C_GROUP = 128
POOL_WINDOWS = (2, 4, 8, 16)
HALO = 16
IN_COLS = 6144
MIX_COLS = 3072
XATTN_HEADS = 4
HEAD_DIM = 256
FFN_HIDDEN = 2816
N_CHIPS = 4

ADAM_LR = 0.001
ADAM_B1 = 0.9
ADAM_B2 = 0.999
ADAM_EPS = 1e-08
ADAM_WD = 0.01
ADAM_STEP = 10

V7X_VMEM_BYTES = 64 * 1024 * 1024
VMEM_LIMIT = (V7X_VMEM_BYTES * 3) // 4
LANES = 128
MESH = pl.DeviceIdType.MESH
ANY = pl.BlockSpec(memory_space=pl.ANY)

BIG = ("w_in", "w_branch_a", "w_branch_b", "w_branch_c", "w_o", "w_xq", "w_xkv", "w_xo", "w_ffn_in", "w_ffn_out")
COL_SHARDED = ("w_in", "w_branch_a", "w_branch_b", "w_branch_c", "w_xkv", "w_ffn_in")
RIDE_FIRST = ("w_in", "w_ffn_out", "w_branch_a", "w_branch_b", "w_branch_c")
NEED_FIRST = ("w_in", "w_branch_a", "w_branch_b", "w_branch_c", "w_o")
NEED_LAST = ("w_ffn_in", "w_ffn_out")
SMALL_REPL = ("g_mix", "a_ln_g", "a_ln_b", "a_ws", "a_bs", "c_wg", "c_scale", "g_xattn", "g_mem", "g_ffn", "g_final")
SMALL_SHARDED = ("b_gate", "b_conv")
WEIGHTS = ("g_mix", "w_in", "b_gate", "a_ln_g", "a_ln_b", "a_ws", "a_bs", "b_conv", "c_wg", "c_scale", "w_branch_a",
           "w_branch_b", "w_branch_c", "w_o", "g_xattn", "g_mem", "w_xq", "w_xkv", "w_xo", "g_ffn", "w_ffn_in",
           "w_ffn_out", "g_final")


def _params(sem=None):
    return pltpu.CompilerParams(dimension_semantics=sem, vmem_limit_bytes=VMEM_LIMIT)


def _blk(dim, pref):
    return pref if dim % pref == 0 else dim


class _Carry(NamedTuple):
    arrays: tuple
    outs: tuple
    in_place: bool
    copies: int
    issue: Callable


def _call_with_carry(body, *, out_shape, grid, in_specs, out_specs, scratch_shapes, name, semantics, operands, carry):
    if carry is None:
        return pl.pallas_call(body, out_shape=tuple(out_shape), grid=grid, in_specs=list(in_specs),
                              out_specs=tuple(out_specs), scratch_shapes=list(scratch_shapes), name=name,
                              compiler_params=_params(semantics))(*operands)
    n_in, n_out, na, no = len(operands), len(out_shape), len(carry.arrays), len(carry.outs)

    def carried(*refs):
        ins, cins = refs[:n_in], refs[n_in:n_in + na]
        outs, couts = refs[n_in + na:n_in + na + n_out], refs[n_in + na + n_out:n_in + na + n_out + no]
        rest = refs[n_in + na + n_out + no:]
        scratch, ssem, rsem = rest[:-2], rest[-2], rest[-1]
        first = functools.reduce(jnp.logical_and, [pl.program_id(ax) == 0 for ax in range(len(grid))])
        last = functools.reduce(jnp.logical_and, [pl.program_id(ax) == grid[ax] - 1 for ax in range(len(grid))])

        @pl.when(first)
        def _():
            for send, _ in carry.issue(cins, couts, ssem, rsem):
                send.start()

        body(*ins, *outs, *scratch)

        @pl.when(last)
        def _():
            for send, recv in carry.issue(cins, couts, ssem, rsem):
                send.wait_send()
                recv.wait_recv()

    aliases = {n_in + i: n_out + i for i in range(na)} if carry.in_place else {}
    sems = [pltpu.SemaphoreType.DMA((carry.copies,)), pltpu.SemaphoreType.DMA((carry.copies,))]
    return pl.pallas_call(carried, out_shape=tuple(out_shape) + tuple(carry.outs), grid=grid,
                          in_specs=list(in_specs) + [ANY] * na, out_specs=tuple(out_specs) + tuple([ANY] * no),
                          scratch_shapes=list(scratch_shapes) + sems, input_output_aliases=aliases, name=name,
                          compiler_params=_params(("arbitrary",) * len(grid)))(*operands, *carry.arrays)


def _mm(a, b, *, mode, out_dtype, name, tm, tn, tk, res=None, b_lead=None, b_halves=False, out_shards=None,
        carry=None):
    dims = {"nn": (((1,), (0,)), ((), ())), "nt": (((1,), (1,)), ((), ())), "tn": (((0,), (0,)), ((), ()))}[mode]
    b_rows, b_last = b.shape[-2], b.shape[-1]
    if mode == "nn":
        m, k, n = a.shape[0], a.shape[1], b_last
        assert b_rows == k
    elif mode == "nt":
        m, k, n = a.shape[0], a.shape[1], b_rows
        assert b_last == k
    else:
        k, m, n = a.shape[0], a.shape[1], (2 * b_last if b_halves else b_last)
        assert b_rows == k
    tm, tn, tk = _blk(m, tm), _blk(n, tn), _blk(k, tk)
    nk = k // tk
    grid = (m // tm, n // tn, nk)

    if mode == "tn":
        a_spec = pl.BlockSpec((tk, tm), lambda i, j, kk: (kk, i))
        b_block, b_idx = (tk, tn), (lambda i, j, kk: (kk, j))
    else:
        a_spec = pl.BlockSpec((tm, tk), lambda i, j, kk: (i, kk))
        if mode == "nn":
            b_block, b_idx = (tk, tn), (lambda i, j, kk: (kk, j))
        else:
            b_block, b_idx = (tn, tk), (lambda i, j, kk: (j, kk))
    if b_halves:
        assert mode == "tn" and b_lead is None and b_last % tn == 0
        per_half = b_last // tn
        b_spec = pl.BlockSpec((None,) + b_block, lambda i, j, kk: (j // per_half, kk, j % per_half))
    elif b_lead is None:
        b_spec = pl.BlockSpec(b_block, b_idx)
    else:
        b_spec = pl.BlockSpec((None,) + b_block, lambda i, j, kk: (b_lead,) + b_idx(i, j, kk))
    in_specs = [a_spec, b_spec]
    operands = [a, b]
    if res is not None:
        in_specs.append(pl.BlockSpec((tm, tn), lambda i, j, kk: (i, j)))
        operands.append(res)
    if out_shards is None:
        out_shape = jax.ShapeDtypeStruct((m, n), out_dtype)
        out_spec = pl.BlockSpec((tm, tn), lambda i, j, kk: (i, j))
    else:
        per = n // out_shards
        assert per % tn == 0
        nps = per // tn
        out_shape = jax.ShapeDtypeStruct((out_shards, m, per), out_dtype)
        out_spec = pl.BlockSpec((None, tm, tn), lambda i, j, kk: (j // nps, i, j % nps))

    def body(*refs):
        a_ref, b_ref = refs[0], refs[1]
        res_ref = refs[2] if res is not None else None
        o_ref = refs[3] if res is not None else refs[2]
        part = lax.dot_general(a_ref[...].astype(BF16), b_ref[...].astype(BF16), dims, preferred_element_type=F32)

        def finish(acc):
            if res_ref is not None:
                acc = acc + res_ref[...]
            o_ref[...] = acc.astype(out_dtype)

        if nk == 1:
            finish(part)
        else:
            acc_ref = refs[-1]
            kk = pl.program_id(2)

            @pl.when(kk == 0)
            def _():
                acc_ref[...] = part

            @pl.when(kk > 0)
            def _():
                acc_ref[...] += part

            @pl.when(kk == nk - 1)
            def _():
                finish(acc_ref[...])

    scratch = [] if nk == 1 else [pltpu.VMEM((tm, tn), F32)]
    outs = _call_with_carry(body, out_shape=(out_shape,), grid=grid, in_specs=in_specs, out_specs=(out_spec,),
                            scratch_shapes=scratch, name=name, semantics=("parallel", "parallel", "arbitrary"),
                            operands=operands, carry=carry)
    return outs[0] if carry is None else (outs[0], outs[1:])


ROW_CHUNK = 256
FFN_TILE = 1408
TOKEN_STEP = 2048
NT_DIMS = (((1,), (1,)), ((), ()))


def _norm_rows_into(x_ref, g_ref, h_ref, hs_ref, tm):
    rc = min(ROW_CHUNK, tm)
    for r0 in range(0, tm, rc):
        xv = x_ref[r0:r0 + rc, :]
        r = lax.rsqrt(jnp.mean(xv * xv, axis=-1, keepdims=True) + EPS)
        hv = (xv * r * g_ref[...]).astype(BF16)
        hs_ref[r0:r0 + rc, :] = hv
        h_ref[r0:r0 + rc, :] = hv


def _norm_mm(x, g, b, b_lead, *, tn, name, carry=None):
    s, d = x.shape
    n = b.shape[-1]
    tm, tn = _blk(s, 1024), _blk(n, tn)

    def body(x_ref, g_ref, b_ref, h_ref, o_ref, hs_ref):
        @pl.when(pl.program_id(1) == 0)
        def _():
            _norm_rows_into(x_ref, g_ref, h_ref, hs_ref, tm)

        o_ref[...] = jnp.dot(hs_ref[...], b_ref[...], preferred_element_type=F32).astype(BF16)

    row = pl.BlockSpec((tm, d), lambda i, j: (i, 0))
    outs = _call_with_carry(body, out_shape=(jax.ShapeDtypeStruct((s, d), BF16), jax.ShapeDtypeStruct((s, n), BF16)),
                            grid=(s // tm, n // tn),
                            in_specs=[row, pl.BlockSpec((1, d), lambda i, j: (0, 0)),
                                      pl.BlockSpec((None, d, tn), lambda i, j: (b_lead, 0, j))],
                            out_specs=(row, pl.BlockSpec((tm, tn), lambda i, j: (i, j))),
                            scratch_shapes=[pltpu.VMEM((tm, d), BF16)], name=name,
                            semantics=("parallel", "arbitrary"), operands=[x, g, b], carry=carry)
    return outs if carry is None else (outs[0], outs[1], outs[2:])


def _norm_ffn_in(x, g, b, b_lead, name, carry=None):
    s, d = x.shape
    tm, tn = _blk(s, 512), FFN_TILE
    nj = FFN_HIDDEN // tn

    def body(x_ref, g_ref, bg_ref, bu_ref, h_ref, gu_ref, act_ref, hs_ref):
        @pl.when(pl.program_id(1) == 0)
        def _():
            _norm_rows_into(x_ref, g_ref, h_ref, hs_ref, tm)

        hv = hs_ref[...]
        gate = jnp.dot(hv, bg_ref[...], preferred_element_type=F32)
        up = jnp.dot(hv, bu_ref[...], preferred_element_type=F32)
        gu_ref[0] = gate.astype(BF16)
        gu_ref[1] = up.astype(BF16)
        act_ref[...] = (gate * _sigmoid(gate) * up).astype(BF16)

    row = pl.BlockSpec((tm, d), lambda i, j: (i, 0))
    out_shape = (jax.ShapeDtypeStruct((s, d), BF16), jax.ShapeDtypeStruct((2, s, FFN_HIDDEN), BF16),
                 jax.ShapeDtypeStruct((s, FFN_HIDDEN), BF16))
    outs = _call_with_carry(body, out_shape=out_shape, grid=(s // tm, nj),
                            in_specs=[row, pl.BlockSpec((1, d), lambda i, j: (0, 0)),
                                      pl.BlockSpec((None, d, tn), lambda i, j: (b_lead, 0, j)),
                                      pl.BlockSpec((None, d, tn), lambda i, j: (b_lead, 0, j + nj))],
                            out_specs=(row, pl.BlockSpec((2, tm, tn), lambda i, j: (0, i, j)),
                                       pl.BlockSpec((tm, tn), lambda i, j: (i, j))),
                            scratch_shapes=[pltpu.VMEM((tm, d), BF16)], name=name,
                            semantics=("parallel", "arbitrary"), operands=[x, g, b, b], carry=carry)
    return outs if carry is None else (outs[0], outs[1], outs[2], outs[3:])


def _d_act_swiglu(dx, w, b_lead, gu, name):
    s, d = dx.shape
    tm, tn = _blk(s, 512), FFN_TILE

    def body(dx_ref, w_ref, gu_ref, o_ref):
        dact = lax.dot_general(dx_ref[...].astype(BF16), w_ref[...], NT_DIMS, preferred_element_type=F32)
        gate = gu_ref[0].astype(F32)
        up = gu_ref[1].astype(F32)
        sg = _sigmoid(gate)
        o_ref[0] = (dact * up * sg * (1.0 + gate * (1.0 - sg))).astype(BF16)
        o_ref[1] = (dact * gate * sg).astype(BF16)

    halves = pl.BlockSpec((2, tm, tn), lambda j, i: (0, i, j))
    return pl.pallas_call(body, out_shape=jax.ShapeDtypeStruct(gu.shape, BF16), grid=(FFN_HIDDEN // tn, s // tm),
                          in_specs=[pl.BlockSpec((tm, d), lambda j, i: (i, 0)),
                                    pl.BlockSpec((None, tn, d), lambda j, i: (b_lead, j, 0)), halves],
                          out_specs=halves, name=name,
                          compiler_params=_params(("parallel", "parallel")))(dx, w, gu)


def _mm_nt_norm_bwd(a, b, b_lead, x, g, dres, *, tm, name, a_halves=False, carry=None):
    s, d = x.shape
    kdim = b.shape[-1]
    tm = _blk(s, tm)
    rc = min(ROW_CHUNK, tm)
    if a_halves:
        a_spec = pl.BlockSpec((2, tm, kdim // 2), lambda i: (0, i, 0))
    else:
        a_spec = pl.BlockSpec((tm, kdim), lambda i: (i, 0))

    def body(a_ref, b_ref, x_ref, g_ref, r_ref, dx_ref, dg_ref):
        i = pl.program_id(0)
        av = jnp.concatenate([a_ref[0], a_ref[1]], axis=1) if a_halves else a_ref[...]
        dh = lax.dot_general(av.astype(BF16), b_ref[...], NT_DIMS, preferred_element_type=F32)
        gv = g_ref[...]
        dg_part = None
        for r0 in range(0, tm, rc):
            xv = x_ref[r0:r0 + rc, :]
            dhv = dh[r0:r0 + rc, :]
            r = lax.rsqrt(jnp.mean(xv * xv, axis=-1, keepdims=True) + EPS)
            xhat = xv * r
            p = jnp.sum(dhv * xhat, axis=0, keepdims=True)
            dg_part = p if dg_part is None else dg_part + p
            dxhat = dhv * gv
            dx_ref[r0:r0 + rc, :] = r_ref[r0:r0 + rc, :] + r * (
                dxhat - xhat * jnp.mean(dxhat * xhat, axis=-1, keepdims=True))

        @pl.when(i == 0)
        def _():
            dg_ref[...] = dg_part

        @pl.when(i > 0)
        def _():
            dg_ref[...] += dg_part

    row = pl.BlockSpec((tm, d), lambda i: (i, 0))
    vec = pl.BlockSpec((1, d), lambda i: (0, 0))
    b_spec = pl.BlockSpec((None, d, kdim), lambda i: (b_lead, 0, 0), pipeline_mode=pl.Buffered(1))
    outs = _call_with_carry(body, out_shape=(jax.ShapeDtypeStruct((s, d), F32), jax.ShapeDtypeStruct((1, d), F32)),
                            grid=(s // tm,), in_specs=[a_spec, b_spec, row, vec, row], out_specs=(row, vec),
                            scratch_shapes=[], name=name, semantics=("arbitrary",), operands=[a, b, x, g, dres],
                            carry=carry)
    return outs if carry is None else (outs[0], outs[1], outs[2:])


def _rms_fwd(x, g, name):
    s, d = x.shape
    tm = _blk(s, 512)

    def body(x_ref, g_ref, o_ref):
        xv = x_ref[...]
        r = lax.rsqrt(jnp.mean(xv * xv, axis=-1, keepdims=True) + EPS)
        o_ref[...] = (xv * r * g_ref[...]).astype(BF16)

    return pl.pallas_call(body, out_shape=jax.ShapeDtypeStruct((s, d), BF16), grid=(s // tm,),
                          in_specs=[pl.BlockSpec((tm, d), lambda i: (i, 0)), pl.BlockSpec((1, d), lambda i: (0, 0))],
                          out_specs=pl.BlockSpec((tm, d), lambda i: (i, 0)), name=name,
                          compiler_params=_params(("parallel",)))(x, g)


def _rms_bwd(x, g, dh, dres, name):
    s, d = x.shape
    tm = _blk(s, 512)
    has_res = dres is not None

    def body(*refs):
        x_ref, g_ref, dh_ref = refs[0], refs[1], refs[2]
        dx_ref, dg_ref = refs[-2], refs[-1]
        xv = x_ref[...]
        r = lax.rsqrt(jnp.mean(xv * xv, axis=-1, keepdims=True) + EPS)
        xhat = xv * r
        dhv = dh_ref[...].astype(F32)
        part = jnp.sum(dhv * xhat, axis=0, keepdims=True)

        @pl.when(pl.program_id(0) == 0)
        def _():
            dg_ref[...] = part

        @pl.when(pl.program_id(0) > 0)
        def _():
            dg_ref[...] += part

        dxhat = dhv * g_ref[...]
        dx = r * (dxhat - xhat * jnp.mean(dxhat * xhat, axis=-1, keepdims=True))
        if has_res:
            dx = dx + refs[3][...]
        dx_ref[...] = dx

    row = pl.BlockSpec((tm, d), lambda i: (i, 0))
    vec = pl.BlockSpec((1, d), lambda i: (0, 0))
    in_specs = [row, vec, row] + ([row] if has_res else [])
    operands = [x, g, dh] + ([dres] if has_res else [])
    return pl.pallas_call(body, out_shape=(jax.ShapeDtypeStruct((s, d), F32), jax.ShapeDtypeStruct((1, d), F32)),
                          grid=(s // tm,), in_specs=in_specs, out_specs=(row, vec), name=name,
                          compiler_params=_params(("arbitrary",)))(*operands)


def _final_loss(x, g, target, name):
    s, d = x.shape
    tm = _blk(s, 512)

    def body(x_ref, g_ref, t_ref, loss_ref, dx_ref, dg_ref):
        xv = x_ref[...]
        gv = g_ref[...]
        r = lax.rsqrt(jnp.mean(xv * xv, axis=-1, keepdims=True) + EPS)
        xhat = xv * r
        err = xhat * gv - t_ref[...]
        lpart = 0.5 * jnp.sum(jnp.mean(err * err, axis=-1, keepdims=True), axis=0, keepdims=True)
        dy = err * (1.0 / d)
        gpart = jnp.sum(dy * xhat, axis=0, keepdims=True)

        @pl.when(pl.program_id(0) == 0)
        def _():
            loss_ref[...] = lpart
            dg_ref[...] = gpart

        @pl.when(pl.program_id(0) > 0)
        def _():
            loss_ref[...] += lpart
            dg_ref[...] += gpart

        dxhat = dy * gv
        dx_ref[...] = r * (dxhat - xhat * jnp.mean(dxhat * xhat, axis=-1, keepdims=True))

    row = pl.BlockSpec((tm, d), lambda i: (i, 0))
    vec = pl.BlockSpec((1, d), lambda i: (0, 0))
    one = pl.BlockSpec((1, 1), lambda i: (0, 0))
    return pl.pallas_call(body, out_shape=(jax.ShapeDtypeStruct((1, 1), F32), jax.ShapeDtypeStruct((s, d), F32),
                                           jax.ShapeDtypeStruct((1, d), F32)),
                          grid=(s // tm,), in_specs=[row, vec, row], out_specs=(one, row, vec), name=name,
                          compiler_params=_params(("arbitrary",)))(x, g, target)


def _erf_parts(x):
    cdf = 0.5 * (1.0 + lax.erf(x * (1.0 / math.sqrt(2.0))))
    return cdf


def _shift_down(ext, k):
    return pltpu.roll(ext, k, 0)


def _shift_up(ext, k):
    return pltpu.roll(ext, ext.shape[0] - k, 0)


def _mixer_forward_math(z, halo, row0, prm, tm):
    ln_g, ln_b, wtril, bsb, conv_w, wg, scale = prm
    za = z[:, 0:2 * A_WIDTH]
    gb = z[:, 1024:1536]
    gc = z[:, 1536:2048]
    xin = z[:, 2048:2560]
    zc = z[:, 2560:3072]
    cdf = _erf_parts(za)
    act = za * cdf
    u = act[:, :A_WIDTH]
    v = act[:, A_WIDTH:]
    mu = jnp.mean(v, axis=-1, keepdims=True)
    vc = v - mu
    rstd = lax.rsqrt(jnp.mean(vc * vc, axis=-1, keepdims=True) + EPS)
    vhat = vc * rstd
    vn = (vhat * ln_g + ln_b).astype(BF16)
    rows = []
    for c in range(tm // CHUNK):
        cols = []
        for g in range(A_GROUPS):
            blk = vn[c * CHUNK:(c + 1) * CHUNK, g * CHUNK:(g + 1) * CHUNK]
            cols.append(jnp.dot(wtril[g], blk, preferred_element_type=F32) + bsb[g])
        rows.append(jnp.concatenate(cols, axis=1))
    mixed = jnp.concatenate(rows, axis=0)
    ya = u * mixed
    y = gc * xin
    yext = jnp.concatenate([halo[:, 0:512] * halo[:, 512:1024], y], axis=0)
    y1 = _shift_down(yext, 1)[HALO:]
    y2 = _shift_down(yext, 2)[HALO:]
    conv = conv_w[0:1, :] * y2 + conv_w[1:2, :] * y1 + conv_w[2:3, :] * y
    yb = gb * conv
    t = row0 + lax.broadcasted_iota(jnp.int32, (tm, 1), 0)
    zext = jnp.concatenate([halo[:, 1024:1536], zc], axis=0)
    pooled, p = [], []
    for gi, win in enumerate(POOL_WINDOWS):
        sw = zext[:, gi * C_GROUP:(gi + 1) * C_GROUP]
        for step in range(gi + 1):
            sw = sw + _shift_down(sw, 2 ** step)
        cnt = jnp.minimum(t + 1, win).astype(F32)
        pg = sw[HALO:] / cnt - zc[:, gi * C_GROUP:(gi + 1) * C_GROUP]
        pooled.append(pg)
        p.append(jnp.dot(pg.astype(BF16), wg[gi], preferred_element_type=F32))
    p = jnp.concatenate(p, axis=1)
    yc = p * scale
    saved = dict(za=za, cdf=cdf, u=u, mixed=mixed, rstd=rstd, vhat=vhat, vn=vn, gb=gb, gc=gc, xin=xin, y=y, y1=y1, y2=y2,
                 conv=conv, pooled=pooled, p=p, t=t)
    return ya, yb, yc, saved


def _mixer_specs(tm, s):
    nb16 = tm // HALO
    main = pl.BlockSpec((tm, MIX_COLS), lambda i: (i, 0))
    prev = pl.BlockSpec((HALO, 1536), lambda i: (jnp.maximum(i * nb16 - 1, 0), 1))
    return main, prev


def _full(shape):
    n = len(shape)
    return pl.BlockSpec(shape, lambda i: (0,) * n)


def _mixer_fwd(proj, prm, name):
    s = proj.shape[0]
    tm = _blk(s, 256)
    main, prev = _mixer_specs(tm, s)

    def body(z_ref, h_ref, lng, lnb, wt, bsb, cw, wg, sc, o_ref):
        i = pl.program_id(0)
        halo = jnp.where(i > 0, h_ref[...].astype(F32), 0.0)
        prm_v = (lng[...], lnb[...], wt[...], bsb[...], cw[...], wg[...], sc[...])
        ya, yb, yc, _ = _mixer_forward_math(z_ref[...].astype(F32), halo, i * tm, prm_v, tm)
        o_ref[...] = jnp.concatenate([ya, yb, yc], axis=1).astype(BF16)

    return pl.pallas_call(body, out_shape=jax.ShapeDtypeStruct((s, 1536), BF16), grid=(s // tm,),
                          in_specs=[main, prev] + [_full(p.shape) for p in prm],
                          out_specs=pl.BlockSpec((tm, 1536), lambda i: (i, 0)), name=name,
                          compiler_params=_params(("parallel",)))(proj, proj, *prm)


def _mixer_bwd(proj, dypre, dproj_in, prm, wtril_t, name):
    s = proj.shape[0]
    tm = _blk(s, 256)
    nblk = s // tm
    nb16 = tm // HALO
    main, prev = _mixer_specs(tm, s)
    nxt_row = lambda i: jnp.minimum((i + 1) * nb16, s // HALO - 1)
    next_dy = pl.BlockSpec((HALO, 1536), lambda i: (nxt_row(i), 0))
    next_gb = pl.BlockSpec((HALO, 512), lambda i: (nxt_row(i), 2))
    ones8 = jnp.ones((8, CHUNK), F32)

    def body(z_ref, h_ref, dy_ref, ndy_ref, ngb_ref, _alias, lng, lnb, wt, bsb, cw, wg, sc, wtt, ones_ref,
             dz_ref, dlng_ref, dlnb_ref, dws_ref, dbs_ref, dcw_ref, dwg_ref, dsc_ref):
        i = pl.program_id(0)
        first = i == 0
        halo = jnp.where(i > 0, h_ref[...].astype(F32), 0.0)
        prm_v = (lng[...], lnb[...], wt[...], bsb[...], cw[...], wg[...], sc[...])
        _, _, _, sv = _mixer_forward_math(z_ref[...].astype(F32), halo, i * tm, prm_v, tm)
        dy = dy_ref[...].astype(F32)
        dya, dyb, dyc = dy[:, 0:512], dy[:, 512:1024], dy[:, 1024:1536]
        not_last = i < nblk - 1
        ndy = jnp.where(not_last, ndy_ref[...].astype(F32), 0.0)
        ngb = ngb_ref[...].astype(F32)

        def accumulate(ref, val):
            @pl.when(first)
            def _():
                ref[...] = val

            @pl.when(jnp.logical_not(first))
            def _():
                ref[...] += val

        du = dya * sv["mixed"]
        dmixed = dya * sv["u"]
        vn = sv["vn"]
        wttv = wtt[...]
        dvn_rows = []
        dws_parts = [None] * A_GROUPS
        dbs_parts = [None] * A_GROUPS
        for c in range(tm // CHUNK):
            cols = []
            for g in range(A_GROUPS):
                dm = dmixed[c * CHUNK:(c + 1) * CHUNK, g * CHUNK:(g + 1) * CHUNK]
                dmb = dm.astype(BF16)
                vb = vn[c * CHUNK:(c + 1) * CHUNK, g * CHUNK:(g + 1) * CHUNK]
                w_part = lax.dot_general(dmb, vb, (((1,), (1,)), ((), ())), preferred_element_type=F32)
                b_part = lax.dot_general(ones_ref[...], dm, (((1,), (1,)), ((), ())), preferred_element_type=F32,
                                         precision=lax.Precision.HIGHEST)[0:1, :]
                dws_parts[g] = w_part if dws_parts[g] is None else dws_parts[g] + w_part
                dbs_parts[g] = b_part if dbs_parts[g] is None else dbs_parts[g] + b_part
                cols.append(jnp.dot(wttv[g], dmb, preferred_element_type=F32))
            dvn_rows.append(jnp.concatenate(cols, axis=1))
        dvn = jnp.concatenate(dvn_rows, axis=0)
        tri = lax.broadcasted_iota(jnp.int32, (CHUNK, CHUNK), 0) >= lax.broadcasted_iota(jnp.int32, (CHUNK, CHUNK), 1)
        accumulate(dws_ref, jnp.stack([jnp.where(tri, w, 0.0) for w in dws_parts], axis=0))
        accumulate(dbs_ref, jnp.concatenate(dbs_parts, axis=0))
        vhat = sv["vhat"]
        accumulate(dlng_ref, jnp.sum(dvn * vhat, axis=0, keepdims=True))
        accumulate(dlnb_ref, jnp.sum(dvn, axis=0, keepdims=True))
        dvhat = dvn * lng[...]
        dv = sv["rstd"] * (dvhat - jnp.mean(dvhat, axis=-1, keepdims=True)
                           - vhat * jnp.mean(dvhat * vhat, axis=-1, keepdims=True))
        za = sv["za"]
        dgelu = sv["cdf"] + za * jnp.exp(-0.5 * za * za) * (1.0 / math.sqrt(2.0 * math.pi))
        dza = jnp.concatenate([du, dv], axis=1) * dgelu
        cwv = cw[...]
        dconv = dyb * sv["gb"]
        dgb = dyb * sv["conv"]
        dcext = jnp.concatenate([dconv, ndy[:, 512:1024] * ngb], axis=0)
        d1 = _shift_up(dcext, 1)[:tm]
        d2 = _shift_up(dcext, 2)[:tm]
        dyy = cwv[2:3, :] * dconv + cwv[1:2, :] * d1 + cwv[0:1, :] * d2
        dgc = dyy * sv["xin"]
        dxin = dyy * sv["gc"]
        accumulate(dcw_ref, jnp.concatenate([jnp.sum(dconv * sv["y2"], axis=0, keepdims=True),
                                             jnp.sum(dconv * sv["y1"], axis=0, keepdims=True),
                                             jnp.sum(dconv * sv["y"], axis=0, keepdims=True)], axis=0))
        scv = sc[...]
        accumulate(dsc_ref, jnp.sum(dyc * sv["p"], axis=0, keepdims=True))
        dp = dyc * scv
        ndp = ndy[:, 1024:1536] * scv
        wgv = wg[...]
        dzc_cols, dwg_parts = [], []
        for gi, win in enumerate(POOL_WINDOWS):
            sl = slice(gi * C_GROUP, (gi + 1) * C_GROUP)
            dpb = dp[:, sl].astype(BF16)
            dpool = lax.dot_general(dpb, wgv[gi], (((1,), (1,)), ((), ())), preferred_element_type=F32)
            ndpool = lax.dot_general(ndp[:, sl].astype(BF16), wgv[gi], (((1,), (1,)), ((), ())),
                                     preferred_element_type=F32)
            dwg_parts.append(lax.dot_general(sv["pooled"][gi].astype(BF16), dpb, (((0,), (0,)), ((), ())),
                                             preferred_element_type=F32))
            cnt = jnp.minimum(sv["t"] + 1, win).astype(F32)
            fw = jnp.concatenate([dpool / cnt, ndpool * (1.0 / win)], axis=0)
            for step in range(gi + 1):
                fw = fw + _shift_up(fw, 2 ** step)
            dzc_cols.append(fw[:tm] - dpool)
        accumulate(dwg_ref, jnp.stack(dwg_parts, axis=0))
        dzc = jnp.concatenate(dzc_cols, axis=1)
        dz_ref[...] = jnp.concatenate([dza, dgb, dgc, dxin, dzc], axis=1).astype(BF16)

    out_shape = (jax.ShapeDtypeStruct(dproj_in.shape, BF16), jax.ShapeDtypeStruct((1, A_WIDTH), F32),
                 jax.ShapeDtypeStruct((1, A_WIDTH), F32), jax.ShapeDtypeStruct((A_GROUPS, CHUNK, CHUNK), F32),
                 jax.ShapeDtypeStruct((A_GROUPS, CHUNK), F32), jax.ShapeDtypeStruct((3, B_WIDTH), F32),
                 jax.ShapeDtypeStruct((4, C_GROUP, C_GROUP), F32), jax.ShapeDtypeStruct((1, C_WIDTH), F32))
    out_specs = (main,) + tuple(_full(o.shape) for o in out_shape[1:])
    in_specs = [main, prev, pl.BlockSpec((tm, 1536), lambda i: (i, 0)), next_dy, next_gb, ANY] + \
               [_full(p.shape) for p in prm] + [_full(wtril_t.shape), _full(ones8.shape)]
    return pl.pallas_call(body, out_shape=out_shape, grid=(nblk,), in_specs=in_specs, out_specs=out_specs,
                          input_output_aliases={5: 0}, name=name,
                          compiler_params=_params(("arbitrary",)))(proj, proj, dypre, dypre, proj, dproj_in, *prm,
                                                                  wtril_t, ones8)


def _sigmoid(x):
    return 1.0 / (1.0 + jnp.exp(-x))


def _merge_fwd(ypre, proj, wa, wb, wc, bgate, name):
    s = ypre.shape[0]
    tm = _blk(s, 512)

    def body(y_ref, zg_ref, wa_ref, wb_ref, wc_ref, bg_ref, o_ref):
        yv = y_ref[...]
        acc = None
        for i, w_ref in enumerate((wa_ref, wb_ref, wc_ref)):
            br = jnp.dot(yv[:, i * 512:(i + 1) * 512], w_ref[...], preferred_element_type=F32)
            gate = _sigmoid(zg_ref[:, i * D_MODEL:(i + 1) * D_MODEL].astype(F32) + bg_ref[i:i + 1, :])
            acc = gate * br if acc is None else acc + gate * br
        o_ref[...] = acc.astype(BF16)

    wspec = _full(wa.shape)
    return pl.pallas_call(body, out_shape=jax.ShapeDtypeStruct((s, D_MODEL), BF16), grid=(s // tm,),
                          in_specs=[pl.BlockSpec((tm, 1536), lambda i: (i, 0)),
                                    pl.BlockSpec((tm, 3 * D_MODEL), lambda i: (i, 1)), wspec, wspec, wspec,
                                    _full(bgate.shape)],
                          out_specs=pl.BlockSpec((tm, D_MODEL), lambda i: (i, 0)), name=name,
                          compiler_params=_params(("parallel",)))(ypre, proj, wa, wb, wc, bgate)


def _merge_bwd(ypre, proj, dmerged, wa, wb, wc, bgate, name, carry=None):
    s = ypre.shape[0]
    tm = _blk(s, 512)

    def body(y_ref, zg_ref, dm_ref, wa_ref, wb_ref, wc_ref, bg_ref, dyp_ref, dzg_ref, dbg_ref, dwa_ref, dwb_ref,
             dwc_ref):
        first = pl.program_id(0) == 0

        def accumulate(ref, val):
            @pl.when(first)
            def _():
                ref[...] = val

            @pl.when(jnp.logical_not(first))
            def _():
                ref[...] += val

        yv = y_ref[...]
        dm = dm_ref[...].astype(F32)
        dyp, dzg, dbg = [], [], []
        for i, (w_ref, dw_ref) in enumerate(((wa_ref, dwa_ref), (wb_ref, dwb_ref), (wc_ref, dwc_ref))):
            wv = w_ref[...]
            yi = yv[:, i * 512:(i + 1) * 512]
            br = jnp.dot(yi, wv, preferred_element_type=F32)
            gate = _sigmoid(zg_ref[:, i * D_MODEL:(i + 1) * D_MODEL].astype(F32) + bg_ref[i:i + 1, :])
            dbi = (dm * gate).astype(BF16)
            dzi = dm * br * gate * (1.0 - gate)
            dzg.append(dzi.astype(BF16))
            dbg.append(jnp.sum(dzi, axis=0, keepdims=True))
            dyp.append(lax.dot_general(dbi, wv, (((1,), (1,)), ((), ())), preferred_element_type=F32).astype(BF16))
            accumulate(dw_ref, lax.dot_general(yi, dbi, (((0,), (0,)), ((), ())), preferred_element_type=F32))
        dyp_ref[...] = jnp.concatenate(dyp, axis=1)
        dzg_ref[...] = jnp.concatenate(dzg, axis=1)
        accumulate(dbg_ref, jnp.concatenate(dbg, axis=0))

    wspec = _full(wa.shape)
    dw_shape = jax.ShapeDtypeStruct(wa.shape, F32)
    out_shape = (jax.ShapeDtypeStruct((s, 1536), BF16), jax.ShapeDtypeStruct((s, IN_COLS), BF16),
                 jax.ShapeDtypeStruct((3, D_MODEL), F32), dw_shape, dw_shape, dw_shape)
    outs = _call_with_carry(body, out_shape=out_shape, grid=(s // tm,),
                            in_specs=[pl.BlockSpec((tm, 1536), lambda i: (i, 0)),
                                      pl.BlockSpec((tm, 3 * D_MODEL), lambda i: (i, 1)),
                                      pl.BlockSpec((tm, D_MODEL), lambda i: (i, 0)), wspec, wspec, wspec,
                                      _full(bgate.shape)],
                            out_specs=(pl.BlockSpec((tm, 1536), lambda i: (i, 0)),
                                       pl.BlockSpec((tm, 3 * D_MODEL), lambda i: (i, 1)), _full((3, D_MODEL)),
                                       wspec, wspec, wspec),
                            scratch_shapes=[], name=name, semantics=("arbitrary",),
                            operands=[ypre, proj, dmerged, wa, wb, wc, bgate], carry=carry)
    return tuple(outs) if carry is None else tuple(outs[:6]) + (outs[6:],)


def _softmax_rows(q, k):
    sc = lax.dot_general(q, k, (((1,), (1,)), ((), ())), preferred_element_type=F32) * (HEAD_DIM ** -0.5)
    e = jnp.exp(sc - jnp.max(sc, axis=-1, keepdims=True))
    return e / jnp.sum(e, axis=-1, keepdims=True)


def _attn_fwd(q, kv, name):
    s = q.shape[0]
    tm = _blk(s, 512)

    def body(q_ref, kv_ref, o_ref):
        outs = []
        for h in range(XATTN_HEADS):
            sl = slice(h * HEAD_DIM, (h + 1) * HEAD_DIM)
            p = _softmax_rows(q_ref[:, sl], kv_ref[:, sl])
            outs.append(jnp.dot(p.astype(BF16), kv_ref[:, D_MODEL + h * HEAD_DIM:D_MODEL + (h + 1) * HEAD_DIM],
                                preferred_element_type=F32))
        o_ref[...] = jnp.concatenate(outs, axis=1).astype(BF16)

    return pl.pallas_call(body, out_shape=jax.ShapeDtypeStruct((s, D_MODEL), BF16), grid=(s // tm,),
                          in_specs=[pl.BlockSpec((tm, D_MODEL), lambda i: (i, 0)), _full(kv.shape)],
                          out_specs=pl.BlockSpec((tm, D_MODEL), lambda i: (i, 0)), name=name,
                          compiler_params=_params(("parallel",)))(q, kv)


def _attn_bwd(q, kv, do, name):
    s = q.shape[0]
    tm = _blk(s, 512)

    def body(q_ref, kv_ref, do_ref, dq_ref, dkv_ref):
        dqs, dks, dvs = [], [], []
        for h in range(XATTN_HEADS):
            sl = slice(h * HEAD_DIM, (h + 1) * HEAD_DIM)
            vsl = slice(D_MODEL + h * HEAD_DIM, D_MODEL + (h + 1) * HEAD_DIM)
            qh, kh, vh, doh = q_ref[:, sl], kv_ref[:, sl], kv_ref[:, vsl], do_ref[:, sl]
            p = _softmax_rows(qh, kh)
            pb = p.astype(BF16)
            dvs.append(lax.dot_general(pb, doh, (((0,), (0,)), ((), ())), preferred_element_type=F32))
            dp = lax.dot_general(doh, vh, (((1,), (1,)), ((), ())), preferred_element_type=F32)
            ds = p * (dp - jnp.sum(dp * p, axis=-1, keepdims=True)) * (HEAD_DIM ** -0.5)
            dsb = ds.astype(BF16)
            dqs.append(jnp.dot(dsb, kh, preferred_element_type=F32))
            dks.append(lax.dot_general(dsb, qh, (((0,), (0,)), ((), ())), preferred_element_type=F32))
        dq_ref[...] = jnp.concatenate(dqs, axis=1).astype(BF16)
        part = jnp.concatenate(dks + dvs, axis=1)

        @pl.when(pl.program_id(0) == 0)
        def _():
            dkv_ref[...] = part

        @pl.when(pl.program_id(0) > 0)
        def _():
            dkv_ref[...] += part

    row = pl.BlockSpec((tm, D_MODEL), lambda i: (i, 0))
    return pl.pallas_call(body, out_shape=(jax.ShapeDtypeStruct((s, D_MODEL), BF16),
                                           jax.ShapeDtypeStruct(kv.shape, F32)),
                          grid=(s // tm,), in_specs=[row, _full(kv.shape), row], out_specs=(row, _full(kv.shape)),
                          name=name, compiler_params=_params(("arbitrary",)))(q, kv, do)


def _rows_block(rows, cols):
    target = (512 * 1024) // cols
    fits = [cand for cand in range(8, rows + 1, 8) if rows % cand == 0 and cand <= target]
    return fits[-1] if fits else rows


def _elementwise(fn, ins, out_dtypes, name):
    lead, rows, cols = ins[0].shape
    tr = _rows_block(rows, cols)
    spec = pl.BlockSpec((None, tr, cols), lambda l, i: (l, i, 0))
    n_in = len(ins)

    def body(*refs):
        for o_ref, o in zip(refs[n_in:], fn(*[r[...] for r in refs[:n_in]])):
            o_ref[...] = o.astype(o_ref.dtype)

    out_shape = tuple(jax.ShapeDtypeStruct((lead, rows, cols), dt) for dt in out_dtypes)
    return pl.pallas_call(body, out_shape=out_shape, grid=(lead, rows // tr), in_specs=[spec] * n_in,
                          out_specs=tuple([spec] * len(out_dtypes)), name=name,
                          compiler_params=_params(("parallel", "parallel")))(*ins)


def _cast_to_slots(w, chip, name):
    _, rows, cols = w.shape
    tr = _rows_block(rows, cols)

    def body(chip_ref, w_ref, o0_ref, o1_ref):
        o0_ref[...] = w_ref[0].astype(BF16)
        o1_ref[...] = w_ref[1].astype(BF16)

    slot = pl.BlockSpec((None, tr, cols), lambda i, chip_ref: (chip_ref[0], i, 0))
    grid_spec = pltpu.PrefetchScalarGridSpec(
        num_scalar_prefetch=1, grid=(rows // tr,),
        in_specs=[pl.BlockSpec((DEPTH, tr, cols), lambda i, chip_ref: (0, i, 0))], out_specs=(slot, slot))
    shape = jax.ShapeDtypeStruct((N_CHIPS, rows, cols), BF16)
    return pl.pallas_call(body, out_shape=(shape, shape), grid_spec=grid_spec, name=name,
                          compiler_params=_params(("parallel",)))(chip, w)


def _chip_sum(pair0, rx0, pair1, rx1, chip, name):
    _, rows, cols = pair0.shape
    tr = _rows_block(rows, cols)

    def body(chip_ref, p0_ref, rx0_ref, p1_ref, rx1_ref, o_ref):
        for layer, (p_ref, rx_ref) in enumerate(((p0_ref, rx0_ref), (p1_ref, rx1_ref))):
            acc = p_ref[...].astype(F32)
            for j in range(3):
                acc = acc + rx_ref[j].astype(F32)
            o_ref[layer] = acc

    mine = pl.BlockSpec((None, tr, cols), lambda i, chip_ref: (chip_ref[0], i, 0))
    theirs = pl.BlockSpec((3, tr, cols), lambda i, chip_ref: (0, i, 0))
    grid_spec = pltpu.PrefetchScalarGridSpec(
        num_scalar_prefetch=1, grid=(rows // tr,), in_specs=[mine, theirs, mine, theirs],
        out_specs=pl.BlockSpec((DEPTH, tr, cols), lambda i, chip_ref: (0, i, 0)))
    return pl.pallas_call(body, out_shape=jax.ShapeDtypeStruct((DEPTH, rows, cols), F32), grid_spec=grid_spec,
                          name=name, compiler_params=_params(("parallel",)))(chip, pair0, rx0, pair1, rx1)


def _chip_sum_ordered(own, rx, chip, name):
    rows, cols = own.shape

    def body(chip_ref, own_ref, rx_ref, o_ref):
        me = chip_ref[0]
        acc = None
        for k in range(N_CHIPS):
            rel = jnp.bitwise_xor(me, k)
            term = jnp.where(rel == 0, own_ref[...],
                             jnp.where(rel == 2, rx_ref[0], jnp.where(rel == 1, rx_ref[1], rx_ref[2])))
            acc = term if acc is None else acc + term
        o_ref[...] = acc

    grid_spec = pltpu.PrefetchScalarGridSpec(
        num_scalar_prefetch=1, grid=(1,),
        in_specs=[pl.BlockSpec((rows, cols), lambda i, chip_ref: (0, 0)),
                  pl.BlockSpec((3, rows, cols), lambda i, chip_ref: (0, 0, 0))],
        out_specs=pl.BlockSpec((rows, cols), lambda i, chip_ref: (0, 0)))
    return pl.pallas_call(body, out_shape=jax.ShapeDtypeStruct((rows, cols), F32), grid_spec=grid_spec, name=name,
                          compiler_params=_params(("arbitrary",)))(chip, own, rx)


def _adamw_math(w, g, m, v):
    m = ADAM_B1 * m + (1.0 - ADAM_B1) * g
    v = ADAM_B2 * v + (1.0 - ADAM_B2) * (g * g)
    m_hat = m / (1.0 - ADAM_B1 ** ADAM_STEP)
    v_hat = v / (1.0 - ADAM_B2 ** ADAM_STEP)
    delta = -ADAM_LR * (m_hat / (jnp.sqrt(v_hat) + ADAM_EPS) + ADAM_WD * w)
    return delta, m, v


def _adamw_shard(w, mine, other, m, v, core, name):
    lead, rows, cols = w.shape
    half = rows // 2
    tr = _rows_block(half, cols)
    nhalf = half // tr

    def body(core_ref, w_ref, mine_ref, other_ref, m_ref, v_ref, g_out, d_out, m_out, v_out):
        g = jnp.where(pl.program_id(1) // nhalf == core_ref[0], mine_ref[...], other_ref[...])
        d_new, m_new, v_new = _adamw_math(w_ref[...], g, m_ref[...], v_ref[...])
        g_out[...] = g
        d_out[...] = d_new
        m_out[...] = m_new
        v_out[...] = v_new

    whole = pl.BlockSpec((None, tr, cols), lambda l, i, core_ref: (l, i, 0))
    part = pl.BlockSpec((None, tr, cols), lambda l, i, core_ref: (l, i % nhalf, 0))
    shape = jax.ShapeDtypeStruct(w.shape, F32)
    grid_spec = pltpu.PrefetchScalarGridSpec(num_scalar_prefetch=1, grid=(lead, rows // tr),
                                             in_specs=[whole, part, part, whole, whole], out_specs=(whole,) * 4)
    return pl.pallas_call(body, out_shape=(shape,) * 4, grid_spec=grid_spec, name=name,
                          compiler_params=_params(("parallel", "parallel")))(core, w, mine, other, m, v)


def _place():
    x, y, c = lax.axis_index("x"), lax.axis_index("y"), lax.axis_index("c")
    chips = [(1 - x, y), (x, 1 - y), (1 - x, 1 - y)]
    return x, y, c, 2 * x + y, chips


def _run_remote(copies):
    for send, _ in copies:
        send.start()
    for send, recv in copies:
        send.wait_send()
        recv.wait_recv()


def _half_rows(ref, c):
    half = ref.shape[1] // 2
    return pl.ds(pl.multiple_of(c * half, 8), half)


def _gather_issue(whole):
    def issue(_, refs, ssem, rsem):
        x, y, c, me, chips = _place()
        copies = []
        for w, ref in enumerate(refs):
            part = (me,) if w >= len(refs) - whole else (me, _half_rows(ref, c))
            mine = ref.at[part]
            for j, (px, py) in enumerate(chips):
                sems = (ssem.at[3 * w + j], rsem.at[3 * w + j])
                peer = dict(device_id=(px, py, c), device_id_type=MESH)
                send = pltpu.make_async_remote_copy(mine, mine, *sems, **peer)
                recv = pltpu.make_async_remote_copy(mine, ref.at[(2 * px + py,) + part[1:]], *sems, **peer)
                copies.append((send, recv))
        return copies
    return issue


def _gather_carry(fulls, whole=0):
    return _Carry(arrays=tuple(fulls), outs=tuple(jax.ShapeDtypeStruct(f.shape, f.dtype) for f in fulls),
                  in_place=True, copies=3 * len(fulls), issue=_gather_issue(whole))


def _scatter_issue(ins, outs, ssem, rsem):
    x, y, c, me, chips = _place()
    copies = []
    for w, (src, dst) in enumerate(zip(ins, outs)):
        for j, (px, py) in enumerate(chips):
            cp = pltpu.make_async_remote_copy(src.at[2 * px + py], dst.at[j], ssem.at[3 * w + j], rsem.at[3 * w + j],
                                              device_id=(px, py, c), device_id_type=MESH)
            copies.append((cp, cp))
    return copies


def _scatter_carry(pairs):
    return _Carry(arrays=tuple(pairs), outs=tuple(jax.ShapeDtypeStruct((3,) + p.shape[1:], p.dtype) for p in pairs),
                  in_place=False, copies=3 * len(pairs), issue=_scatter_issue)


def _chip_gather(fulls, whole, name):
    n = len(fulls)
    issue = _gather_issue(whole)

    def body(*refs):
        _run_remote(issue(None, refs[n:2 * n], refs[2 * n], refs[2 * n + 1]))

    out_shape = tuple(jax.ShapeDtypeStruct(f.shape, f.dtype) for f in fulls)
    return pl.pallas_call(body, out_shape=out_shape, in_specs=[ANY] * n, out_specs=tuple([ANY] * n),
                          input_output_aliases={w: w for w in range(n)},
                          scratch_shapes=[pltpu.SemaphoreType.DMA((3 * n,)), pltpu.SemaphoreType.DMA((3 * n,))],
                          name=name)(*fulls)


class _Blocks(NamedTuple):
    tile: tuple
    dtype: object
    moves: list
    landed: object


STAGE_BYTES = 4 * 1024 * 1024


def _rows(start, size):
    return pl.ds(start if isinstance(start, int) else pl.multiple_of(start, 8), size)


def _stage_tile(lead, rows, cols, dtype):
    per_row = cols * jnp.dtype(dtype).itemsize
    if lead * rows * per_row <= STAGE_BYTES:
        return lead, rows
    fits = [cand for cand in range(8, rows + 1, 8) if rows % cand == 0 and cand * per_row <= STAGE_BYTES]
    return 1, (fits[-1] if fits else rows)


def _plan_gathered(_, refs, c, me):
    work = []
    for ref in refs:
        lead, rows, cols = ref.shape
        half = rows // 2
        nk, tr = _stage_tile(lead, half, cols, ref.dtype)
        if nk == lead:
            part = ref.at[pl.ds(0, lead), _rows(c * half, half)]
            moves, landed = [(part, part)], ref.at[pl.ds(0, lead), pl.ds(0, half)]
        else:
            moves = []
            for j in range(N_CHIPS - 1):
                k = jnp.bitwise_xor(me, j + 1)
                for b in range(half // tr):
                    part = ref.at[pl.ds(k, 1), _rows(c * half + b * tr, tr)]
                    moves.append((part, part))
            landed = ref.at[pl.ds(0, N_CHIPS - 1), pl.ds(0, half)]
        work.append(_Blocks((nk, tr, cols), ref.dtype, moves, landed))
    return work


def _plan_grad_halves(grads, rxs, c, me):
    work = []
    for g, rx in zip(grads, rxs):
        lead, half, cols = rx.shape
        nk, tr = _stage_tile(lead, half, cols, g.dtype)
        moves = [(g.at[pl.ds(k, nk), _rows((1 - c) * half + b * tr, tr)], rx.at[pl.ds(k, nk), _rows(b * tr, tr)])
                 for k in range(0, lead, nk) for b in range(half // tr)]
        work.append(_Blocks((nk, tr, cols), g.dtype, moves, rx))
    return work


def _plan_whole(srcs, dsts, c, me):
    work = []
    for src, dst in zip(srcs, dsts):
        lead, rows, cols = src.shape
        nk, tr = _stage_tile(lead, rows, cols, src.dtype)
        moves = [(src.at[pl.ds(k, nk), _rows(b * tr, tr)], dst.at[pl.ds(k, nk), _rows(b * tr, tr)])
                 for k in range(0, lead, nk) for b in range(rows // tr)]
        work.append(_Blocks((nk, tr, cols), src.dtype, moves, dst))
    return work


def _push_staged(arrays, outs, in_place, plan, name):
    n, no = len(arrays), len(outs)

    def body(*refs):
        ins, out_refs = refs[:n], refs[n:n + no]
        rsem, idle = refs[n + no], refs[n + no + 1]
        x, y, c, me, _ = _place()
        peer = dict(device_id=(x, y, 1 - c), device_id_type=MESH)
        work = plan(ins, out_refs, c, me)
        for w, blocks in enumerate(work):
            def staged(buf, lsem, ssem, w=w, blocks=blocks):
                sends = []
                for i, (src, dst) in enumerate(blocks.moves):
                    slot = i % 2
                    if i >= 2:
                        sends[i - 2].wait_send()
                    load = pltpu.make_async_copy(src, buf.at[slot], lsem.at[slot])
                    load.start()
                    load.wait()
                    send = pltpu.make_async_remote_copy(buf.at[slot], dst, ssem.at[slot], rsem.at[w], **peer)
                    send.start()
                    sends.append(send)
                for send in sends[-2:]:
                    send.wait_send()

            pl.run_scoped(staged, pltpu.VMEM((2,) + blocks.tile, blocks.dtype), pltpu.SemaphoreType.DMA((2,)),
                          pltpu.SemaphoreType.DMA((2,)))
        for w, blocks in enumerate(work):
            pltpu.make_async_remote_copy(blocks.landed, blocks.landed, idle, rsem.at[w], **peer).wait_recv()

    return pl.pallas_call(body, out_shape=tuple(outs), in_specs=[ANY] * n, out_specs=tuple([ANY] * no),
                          input_output_aliases={w: w for w in range(n)} if in_place else {},
                          scratch_shapes=[pltpu.SemaphoreType.DMA((n,)), pltpu.SemaphoreType.DMA],
                          name=name, compiler_params=_params())(*arrays)


def _pair_sum(g, rx, core, name):
    lead, half, cols = rx.shape
    tr = _rows_block(half, cols)
    nrow = half // tr

    def body(core_ref, g_ref, rx_ref, o_ref):
        o_ref[...] = (g_ref[...].astype(F32) + rx_ref[...].astype(F32)).astype(o_ref.dtype)

    spec = pl.BlockSpec((None, tr, cols), lambda k, i, core_ref: (k, i, 0))
    grid_spec = pltpu.PrefetchScalarGridSpec(
        num_scalar_prefetch=1, grid=(lead, nrow),
        in_specs=[pl.BlockSpec((None, tr, cols), lambda k, i, core_ref: (k, core_ref[0] * nrow + i, 0)), spec],
        out_specs=spec)
    return pl.pallas_call(body, out_shape=jax.ShapeDtypeStruct(rx.shape, rx.dtype), grid_spec=grid_spec, name=name,
                          compiler_params=_params(("parallel", "parallel")))(core, g, rx)


def _push_sibling(src, name):
    lead, rows, cols = src.shape
    tr = _rows_block(rows, cols)
    nrow = rows // tr

    def body(blk_ref, rx_ref, ssem, rsem):
        x, y, c = lax.axis_index("x"), lax.axis_index("y"), lax.axis_index("c")
        k, i = pl.program_id(0), pl.program_id(1)
        peer = dict(device_id=(x, y, 1 - c), device_id_type=MESH)
        dst = rx_ref.at[pl.ds(k, 1), pl.ds(pl.multiple_of(i * tr, 8), tr)]
        cp = pltpu.make_async_remote_copy(blk_ref, dst, ssem, rsem, **peer)
        cp.start()
        cp.wait_send()

        @pl.when(jnp.logical_and(k == lead - 1, i == nrow - 1))
        def _():
            pltpu.make_async_remote_copy(rx_ref, rx_ref, ssem, rsem, **peer).wait_recv()

    return pl.pallas_call(body, out_shape=jax.ShapeDtypeStruct(src.shape, src.dtype), grid=(lead, nrow),
                          in_specs=[pl.BlockSpec((1, tr, cols), lambda k, i: (k, i, 0))], out_specs=ANY,
                          scratch_shapes=[pltpu.SemaphoreType.DMA, pltpu.SemaphoreType.DMA], name=name,
                          compiler_params=_params(("arbitrary", "arbitrary")))(src)


def _chip_scatter(pairs, small, name):
    n = len(pairs)

    def body(*refs):
        ins, small_ref = refs[:n], refs[n]
        outs, small_out = refs[n + 1:2 * n + 1], refs[2 * n + 1]
        ssem, rsem = refs[2 * n + 2:]
        x, y, c, me, chips = _place()
        copies = []
        for w in range(n + 1):
            for j, (px, py) in enumerate(chips):
                src = ins[w].at[2 * px + py] if w < n else small_ref
                dst = (outs[w] if w < n else small_out).at[j]
                cp = pltpu.make_async_remote_copy(src, dst, ssem.at[3 * w + j], rsem.at[3 * w + j],
                                                  device_id=(px, py, c), device_id_type=MESH)
                copies.append((cp, cp))
        _run_remote(copies)

    out_shape = tuple(jax.ShapeDtypeStruct((3,) + it.shape[1:], it.dtype) for it in pairs) + \
                (jax.ShapeDtypeStruct((3,) + small.shape, small.dtype),)
    return pl.pallas_call(body, out_shape=out_shape, in_specs=[ANY] * (n + 1), out_specs=tuple([ANY] * (n + 1)),
                          scratch_shapes=[pltpu.SemaphoreType.DMA((3 * n + 3,)),
                                          pltpu.SemaphoreType.DMA((3 * n + 3,))],
                          name=name)(*pairs, small)


def _pack(arrays):
    flat = jnp.concatenate([a.reshape(-1).astype(F32) for a in arrays])
    pad = (-flat.shape[0]) % (8 * LANES)
    return jnp.pad(flat, (0, pad)).reshape(-1, LANES)


def _unpack(packed, shapes):
    flat = packed.reshape(-1)
    out, off = [], 0
    for shp in shapes:
        size = math.prod(shp)
        out.append(flat[off:off + size].reshape(shp))
        off += size
    return out


def kernel(x, mem, g_mix, w_in, b_gate, a_ln_g, a_ln_b, a_ws, a_bs, b_conv, c_wg, c_scale, w_branch_a, w_branch_b, w_branch_c, w_o, g_xattn, g_mem, w_xq, w_xkv, w_xo, g_ffn, w_ffn_in, w_ffn_out, g_final, loss_target, m_g_mix, m_w_in, m_b_gate, m_a_ln_g, m_a_ln_b, m_a_ws, m_a_bs, m_b_conv, m_c_wg, m_c_scale, m_w_branch_a, m_w_branch_b, m_w_branch_c, m_w_o, m_g_xattn, m_g_mem, m_w_xq, m_w_xkv, m_w_xo, m_g_ffn, m_w_ffn_in, m_w_ffn_out, m_g_final, v_g_mix, v_w_in, v_b_gate, v_a_ln_g, v_a_ln_b, v_a_ws, v_a_bs, v_b_conv, v_c_wg, v_c_scale, v_w_branch_a, v_w_branch_b, v_w_branch_c, v_w_o, v_g_xattn, v_g_mem, v_w_xq, v_w_xkv, v_w_xo, v_g_ffn, v_w_ffn_in, v_w_ffn_out, v_g_final):
    args = locals()
    wts = {n: args[n] for n in WEIGHTS}
    mom = {n: args["m_" + n] for n in WEIGHTS}
    var = {n: args["v_" + n] for n in WEIGHTS}
    xs = x[0]
    mems = mem[0]
    tgt = loss_target[0]
    chip = 2 * lax.axis_index("x") + lax.axis_index("y")

    bias_pack = jnp.zeros((DEPTH, 8, 384), F32)
    bias_pack = bias_pack.at[:, 0:3, 0:256].set(b_gate).at[:, 0:3, 256:384].set(b_conv)
    bias_slots = lax.dynamic_update_slice(jnp.zeros((N_CHIPS, DEPTH, 8, 384), F32), bias_pack[None], (chip, 0, 0, 0))
    core = lax.axis_index("c")
    chip_arr = jnp.reshape(chip, (1,)).astype(jnp.int32)
    core_arr = jnp.reshape(core, (1,)).astype(jnp.int32)
    slots = {n: _cast_to_slots(wts[n], chip_arr, name=f"cast_{n}") for n in BIG}
    layer_weights = [{}, {}]

    def land(l, names, gathered, tag):
        shapes = [jax.ShapeDtypeStruct(g.shape, g.dtype) for g in gathered]
        for n, g in zip(names, _push_staged(list(gathered), shapes, True, _plan_gathered,
                                            name=f"l{l}_gather_cores_{tag}")):
            if n in COL_SHARDED:
                layer_weights[l][n] = jnp.transpose(g, (1, 0, 2)).reshape(1, g.shape[1], N_CHIPS * g.shape[2])
            else:
                layer_weights[l][n] = g.reshape(1, N_CHIPS * g.shape[1], g.shape[2])

    def gather_ride(l, names):
        return _gather_carry([slots[n][l] for n in names])

    rest_of_first = tuple(n for n in BIG if n not in NEED_FIRST)
    most_of_second = tuple(n for n in BIG if n not in NEED_LAST)
    first = _chip_gather([slots[n][0] for n in NEED_FIRST] + [bias_slots], 1, name="l0_gather_chips")
    land(0, NEED_FIRST, first[:-1], "first")
    biases = first[-1]
    b_gate_full = jnp.transpose(biases[:, :, 0:3, 0:256], (1, 2, 0, 3)).reshape(DEPTH, 3, D_MODEL)
    b_conv_full = jnp.transpose(biases[:, :, 0:3, 256:384], (1, 2, 0, 3)).reshape(DEPTH, 3, B_WIDTH)

    tril = jnp.tril(jnp.ones((CHUNK, CHUNK), bool))

    def mixer_params(l):
        wtril = jnp.where(tril[None], a_ws[l], 0.0)
        prm = (a_ln_g[l][None], a_ln_b[l][None], wtril.astype(BF16),
               jnp.broadcast_to(a_bs[l][:, :, None], (A_GROUPS, CHUNK, CHUNK)), b_conv_full[l],
               c_wg[l].astype(BF16), c_scale[l][None])
        return prm, jnp.swapaxes(wtril, 1, 2).astype(BF16)

    saved = []
    xc = xs
    for l in range(DEPTH):
        sv = {"x0": xc}
        prm, _ = mixer_params(l)
        fw = layer_weights[l]
        riding = rest_of_first if l == 0 else NEED_LAST
        out = _norm_mm(xc, g_mix[l][None], fw["w_in"], 0, tn=1536, name=f"l{l}_proj", carry=gather_ride(l, riding))
        sv["h"], sv["proj"] = out[0], out[1]
        land(l, riding, out[2], "rest")
        sv["ypre"] = _mixer_fwd(sv["proj"], prm, name=f"l{l}_mixers")
        sv["merged"] = _merge_fwd(sv["ypre"], sv["proj"], fw["w_branch_a"][0], fw["w_branch_b"][0],
                                  fw["w_branch_c"][0], b_gate_full[l], name=f"l{l}_merge")
        xc = _mm(sv["merged"], fw["w_o"], b_lead=0, mode="nn", out_dtype=F32, name=f"l{l}_mix_out", tm=1024, tn=1024,
                 tk=1024, res=xc)
        sv["x1"] = xc
        sv["hq"], sv["q"] = _norm_mm(xc, g_xattn[l][None], fw["w_xq"], 0, tn=1024, name=f"l{l}_q")
        sv["m"] = _rms_fwd(mems, g_mem[l][None], name=f"l{l}_mem_norm")
        sv["kv"] = _mm(sv["m"], fw["w_xkv"], b_lead=0, mode="nn", out_dtype=BF16, name=f"l{l}_kv", tm=256, tn=512,
                       tk=1024)
        sv["o"] = _attn_fwd(sv["q"], sv["kv"], name=f"l{l}_attn")
        xc = _mm(sv["o"], fw["w_xo"], b_lead=0, mode="nn", out_dtype=F32, name=f"l{l}_xattn_out", tm=1024, tn=1024,
                 tk=1024, res=xc)
        sv["x2"] = xc
        out2 = _norm_ffn_in(xc, g_ffn[l][None], fw["w_ffn_in"], 0, name=f"l{l}_ffn_in",
                            carry=gather_ride(1, most_of_second) if l == 0 else None)
        sv["h2"], sv["gu"], sv["act"] = out2[0], out2[1], out2[2]
        if l == 0:
            land(1, most_of_second, out2[3], "most")
        xc = _mm(sv["act"], fw["w_ffn_out"], b_lead=0, mode="nn", out_dtype=F32, name=f"l{l}_ffn_out", tm=1024,
                 tn=512, tk=FFN_HIDDEN, res=xc)
        saved.append(sv)

    loss_part, dx, dg_final = _final_loss(xc, g_final[None], tgt, name="final_loss")
    loss = lax.psum(loss_part[0, 0], ("x", "y", "c"))

    pairs = [{}, {}]
    received = [{}, {}]
    small_grads = [None] * DEPTH

    def core_pairs(l, names, grads, tag):
        halves = [jax.ShapeDtypeStruct((g.shape[0], g.shape[1] // 2, g.shape[2]), g.dtype) for g in grads]
        came = _push_staged(grads, halves, False, _plan_grad_halves, name=f"l{l}_reduce_cores_{tag}")
        for n, g, rx in zip(names, grads, came):
            pairs[l][n] = _pair_sum(g, rx, core_arr, name=f"l{l}_pair_sum_{n}")

    def scatter_ride(l, names, when):
        return _scatter_carry([pairs[l][n] for n in names]) if when else None

    ride_second = tuple(n for n in BIG if n not in RIDE_FIRST)
    early, early_big, early_small = NEED_LAST + ("w_xo", "w_xq", "w_xkv"), NEED_LAST + ("w_xkv",), ("w_xo", "w_xq")
    for l in reversed(range(DEPTH)):
        sv = saved[l]
        prm, wtril_t = mixer_params(l)
        gb = {}
        sg = {}
        fw = layer_weights[l]
        dgu = _d_act_swiglu(dx, fw["w_ffn_out"], 0, sv["gu"], name=f"l{l}_d_act")
        gb["w_ffn_out"] = _mm(sv["act"], dx, mode="tn", out_dtype=BF16, name=f"l{l}_dw_ffn_out", tm=1408, tn=1024,
                              tk=TOKEN_STEP).reshape(N_CHIPS, FFN_HIDDEN // N_CHIPS, D_MODEL)
        out = _mm(sv["h2"], dgu, mode="tn", out_dtype=BF16, name=f"l{l}_dw_ffn_in", tm=1024, tn=FFN_TILE,
                  tk=TOKEN_STEP, b_halves=True, out_shards=N_CHIPS, carry=scatter_ride(1, RIDE_FIRST, l == 0))
        gb["w_ffn_in"] = out[0] if l == 0 else out
        out2 = _mm_nt_norm_bwd(dgu, fw["w_ffn_in"], 0, sv["x2"], g_ffn[l][None], dx, tm=512, a_halves=True,
                               name=f"l{l}_d_h2", carry=scatter_ride(1, ride_second, l == 0))
        dx, sg["g_ffn"] = out2[0], out2[1]
        if l == 0:
            received[1].update(zip(RIDE_FIRST, out[1]))
            received[1].update(zip(ride_second, out2[2]))
        do = _mm(dx, fw["w_xo"], b_lead=0, mode="nt", out_dtype=BF16, name=f"l{l}_d_o", tm=1024, tn=1024, tk=1024)
        gb["w_xo"] = _mm(sv["o"], dx, mode="tn", out_dtype=BF16, name=f"l{l}_dw_xo", tm=1024, tn=1024,
                         tk=TOKEN_STEP).reshape(N_CHIPS, D_MODEL // N_CHIPS, D_MODEL)
        dq, dkv = _attn_bwd(sv["q"], sv["kv"], do, name=f"l{l}_d_attn")
        gb["w_xq"] = _mm(sv["hq"], dq, mode="tn", out_dtype=BF16, name=f"l{l}_dw_xq", tm=1024, tn=1024,
                         tk=TOKEN_STEP).reshape(N_CHIPS, D_MODEL // N_CHIPS, D_MODEL)
        dm = _mm(dkv, fw["w_xkv"], b_lead=0, mode="nt", out_dtype=BF16, name=f"l{l}_d_m", tm=256, tn=1024, tk=1024)
        gb["w_xkv"] = _mm(sv["m"], dkv, mode="tn", out_dtype=BF16, name=f"l{l}_dw_xkv", tm=1024, tn=512, tk=256,
                          out_shards=N_CHIPS)
        _, sg["g_mem"] = _rms_bwd(mems, g_mem[l][None], dm, None, name=f"l{l}_d_mem_norm")
        if l == 0:
            core_pairs(0, early, [gb[n] for n in early], "early")
        dx, sg["g_xattn"] = _mm_nt_norm_bwd(dq, fw["w_xq"], 0, sv["x1"], g_xattn[l][None], dx, tm=1024,
                                            name=f"l{l}_d_hq")
        dmerged = _mm(dx, fw["w_o"], b_lead=0, mode="nt", out_dtype=BF16, name=f"l{l}_d_merged", tm=1024, tn=1024,
                      tk=1024)
        gb["w_o"] = _mm(sv["merged"], dx, mode="tn", out_dtype=BF16, name=f"l{l}_dw_o", tm=1024, tn=1024,
                        tk=TOKEN_STEP).reshape(N_CHIPS, D_MODEL // N_CHIPS, D_MODEL)
        out3 = _merge_bwd(sv["ypre"], sv["proj"], dmerged, fw["w_branch_a"][0], fw["w_branch_b"][0],
                          fw["w_branch_c"][0], b_gate_full[l], name=f"l{l}_d_merge",
                          carry=scatter_ride(0, early_big, l == 0))
        dypre, dproj, sg["b_gate"] = out3[0], out3[1], out3[2]
        for n, dw in zip(("w_branch_a", "w_branch_b", "w_branch_c"), out3[3:6]):
            gb[n] = jnp.transpose(dw.reshape(A_WIDTH, N_CHIPS, D_MODEL // N_CHIPS), (1, 0, 2)).astype(BF16)
        (dproj, sg["a_ln_g"], sg["a_ln_b"], sg["a_ws"], sg["a_bs"], sg["b_conv"], sg["c_wg"],
         sg["c_scale"]) = _mixer_bwd(sv["proj"], dypre, dproj, prm, wtril_t, name=f"l{l}_d_mixers")
        out4 = _mm(sv["h"], dproj, mode="tn", out_dtype=BF16, name=f"l{l}_dw_in", tm=1024, tn=1536, tk=TOKEN_STEP,
                   out_shards=N_CHIPS, carry=scatter_ride(0, early_small, l == 0))
        gb["w_in"] = out4[0] if l == 0 else out4
        dx, sg["g_mix"] = _mm_nt_norm_bwd(dproj, fw["w_in"], 0, sv["x0"], g_mix[l][None], dx, tm=512,
                                          name=f"l{l}_d_h")
        if l == 0:
            received[0].update(zip(early_big, out3[6]))
            received[0].update(zip(early_small, out4[1]))
            core_pairs(0, NEED_FIRST, [gb[n] for n in NEED_FIRST], "late")
        else:
            core_pairs(1, BIG, [gb[n] for n in BIG], "all")
        small_grads[l] = sg
    grad_x = dx[None]

    small_names = [n for n in WEIGHTS if n not in BIG]
    small_full_shapes = {n: ((DEPTH, 3, D_MODEL) if n == "b_gate" else (DEPTH, 3, B_WIDTH) if n == "b_conv"
                             else wts[n].shape) for n in small_names}
    small_local = []
    for n in small_names:
        if n == "g_final":
            small_local.append(dg_final)
        else:
            small_local.append(jnp.stack([small_grads[l][n].reshape(small_full_shapes[n][1:]) for l in range(DEPTH)]))
    small_pack = _pack(small_local)

    small_rx = _push_sibling(small_pack[None], name="reduce_cores_small")
    small_pair = _elementwise(lambda a, b: (a + b,), [small_pack[None], small_rx], [F32], name="pair_sum_small")[0][0]
    scattered = _chip_scatter([pairs[0][n] for n in NEED_FIRST], small_pair, name="l0_reduce_chips")
    received[0].update(zip(NEED_FIRST, scattered[:-1]))
    small_sum = _chip_sum_ordered(small_pair, scattered[-1], chip_arr, name="chip_sum_small")

    out_g, out_d, out_m, out_v = {}, {}, {}, {}
    mine = [_chip_sum(pairs[0][n], received[0][n], pairs[1][n], received[1][n], chip_arr, name=f"chip_sum_{n}")
            for n in BIG]
    other = _push_staged(mine, [jax.ShapeDtypeStruct(g.shape, g.dtype) for g in mine], False, _plan_whole,
                         name="share_cores")
    for i, n in enumerate(BIG):
        out_g[n], out_d[n], out_m[n], out_v[n] = _adamw_shard(wts[n], mine[i], other[i], mom[n], var[n], core_arr,
                                                              name=f"adamw_{n}")
    small_g = dict(zip(small_names, _unpack(small_sum, [small_full_shapes[n] for n in small_names])))
    small_g["b_gate"] = lax.dynamic_slice_in_dim(small_g["b_gate"], chip * 256, 256, axis=2)
    small_g["b_conv"] = lax.dynamic_slice_in_dim(small_g["b_conv"], chip * 128, 128, axis=2)
    packs = [_pack([d[n] for n in small_names])[None] for d in (wts, small_g, mom, var)]
    upd = _elementwise(_adamw_math, packs, [F32, F32, F32], name="adamw_small")
    shapes = [wts[n].shape for n in small_names]
    for n, d, m_new, v_new in zip(small_names, *[_unpack(u[0], shapes) for u in upd]):
        out_g[n], out_d[n], out_m[n], out_v[n] = small_g[n], d, m_new, v_new

    return (loss, grad_x, *[out_g[n] for n in WEIGHTS], *[out_d[n] for n in WEIGHTS], *[out_m[n] for n in WEIGHTS],
            *[out_v[n] for n in WEIGHTS])
```

```python
import functools
import math
from typing import Callable, NamedTuple

import jax
import jax.numpy as jnp
from jax import lax
from jax.experimental import pallas as pl
from jax.experimental.pallas import tpu as pltpu

F32 = jnp.float32
BF16 = jnp.bfloat16

D_MODEL = 1024
DEPTH = 2
MEM_LEN = 256
EPS = 1e-6
CHUNK = 128
A_GROUPS = 4
A_WIDTH = 512
B_WIDTH = 512
C_WIDTH = 512
C_GROUP = 128
POOL_WINDOWS = (2, 4, 8, 16)
HALO = 16
IN_COLS = 6144
MIX_COLS = 3072
XATTN_HEADS = 4
HEAD_DIM = 256
FFN_HIDDEN = 2816
N_CHIPS = 4

ADAM_LR = 0.001
ADAM_B1 = 0.9
ADAM_B2 = 0.999
ADAM_EPS = 1e-08
ADAM_WD = 0.01
ADAM_STEP = 10

V7X_VMEM_BYTES = 64 * 1024 * 1024
VMEM_LIMIT = (V7X_VMEM_BYTES * 3) // 4
LANES = 128
MESH = pl.DeviceIdType.MESH
ANY = pl.BlockSpec(memory_space=pl.ANY)

BIG = ("w_in", "w_branch_a", "w_branch_b", "w_branch_c", "w_o", "w_xq", "w_xkv", "w_xo", "w_ffn_in", "w_ffn_out")
COL_SHARDED = ("w_in", "w_branch_a", "w_branch_b", "w_branch_c", "w_xkv", "w_ffn_in")
RIDE_FIRST = ("w_in", "w_ffn_out", "w_branch_a", "w_branch_b", "w_branch_c")
NEED_FIRST = ("w_in", "w_branch_a", "w_branch_b", "w_branch_c", "w_o")
NEED_LAST = ("w_ffn_in", "w_ffn_out")
SMALL_REPL = ("g_mix", "a_ln_g", "a_ln_b", "a_ws", "a_bs", "c_wg", "c_scale", "g_xattn", "g_mem", "g_ffn", "g_final")
SMALL_SHARDED = ("b_gate", "b_conv")
WEIGHTS = ("g_mix", "w_in", "b_gate", "a_ln_g", "a_ln_b", "a_ws", "a_bs", "b_conv", "c_wg", "c_scale", "w_branch_a",
           "w_branch_b", "w_branch_c", "w_o", "g_xattn", "g_mem", "w_xq", "w_xkv", "w_xo", "g_ffn", "w_ffn_in",
           "w_ffn_out", "g_final")


def _params(sem=None):
    return pltpu.CompilerParams(dimension_semantics=sem, vmem_limit_bytes=VMEM_LIMIT)


def _blk(dim, pref):
    return pref if dim % pref == 0 else dim


class _Carry(NamedTuple):
    arrays: tuple
    outs: tuple
    in_place: bool
    copies: int
    issue: Callable


def _call_with_carry(body, *, out_shape, grid, in_specs, out_specs, scratch_shapes, name, semantics, operands, carry):
    if carry is None:
        return pl.pallas_call(body, out_shape=tuple(out_shape), grid=grid, in_specs=list(in_specs),
                              out_specs=tuple(out_specs), scratch_shapes=list(scratch_shapes), name=name,
                              compiler_params=_params(semantics))(*operands)
    n_in, n_out, na, no = len(operands), len(out_shape), len(carry.arrays), len(carry.outs)

    def carried(*refs):
        ins, cins = refs[:n_in], refs[n_in:n_in + na]
        outs, couts = refs[n_in + na:n_in + na + n_out], refs[n_in + na + n_out:n_in + na + n_out + no]
        rest = refs[n_in + na + n_out + no:]
        scratch, ssem, rsem = rest[:-2], rest[-2], rest[-1]
        first = functools.reduce(jnp.logical_and, [pl.program_id(ax) == 0 for ax in range(len(grid))])
        last = functools.reduce(jnp.logical_and, [pl.program_id(ax) == grid[ax] - 1 for ax in range(len(grid))])

        @pl.when(first)
        def _():
            for send, _ in carry.issue(cins, couts, ssem, rsem):
                send.start()

        body(*ins, *outs, *scratch)

        @pl.when(last)
        def _():
            for send, recv in carry.issue(cins, couts, ssem, rsem):
                send.wait_send()
                recv.wait_recv()

    aliases = {n_in + i: n_out + i for i in range(na)} if carry.in_place else {}
    sems = [pltpu.SemaphoreType.DMA((carry.copies,)), pltpu.SemaphoreType.DMA((carry.copies,))]
    return pl.pallas_call(carried, out_shape=tuple(out_shape) + tuple(carry.outs), grid=grid,
                          in_specs=list(in_specs) + [ANY] * na, out_specs=tuple(out_specs) + tuple([ANY] * no),
                          scratch_shapes=list(scratch_shapes) + sems, input_output_aliases=aliases, name=name,
                          compiler_params=_params(("arbitrary",) * len(grid)))(*operands, *carry.arrays)


def _mm(a, b, *, mode, out_dtype, name, tm, tn, tk, res=None, b_lead=None, b_halves=False, out_shards=None,
        carry=None):
    dims = {"nn": (((1,), (0,)), ((), ())), "nt": (((1,), (1,)), ((), ())), "tn": (((0,), (0,)), ((), ()))}[mode]
    b_rows, b_last = b.shape[-2], b.shape[-1]
    if mode == "nn":
        m, k, n = a.shape[0], a.shape[1], b_last
        assert b_rows == k
    elif mode == "nt":
        m, k, n = a.shape[0], a.shape[1], b_rows
        assert b_last == k
    else:
        k, m, n = a.shape[0], a.shape[1], (2 * b_last if b_halves else b_last)
        assert b_rows == k
    tm, tn, tk = _blk(m, tm), _blk(n, tn), _blk(k, tk)
    nk = k // tk
    grid = (m // tm, n // tn, nk)

    if mode == "tn":
        a_spec = pl.BlockSpec((tk, tm), lambda i, j, kk: (kk, i))
        b_block, b_idx = (tk, tn), (lambda i, j, kk: (kk, j))
    else:
        a_spec = pl.BlockSpec((tm, tk), lambda i, j, kk: (i, kk))
        if mode == "nn":
            b_block, b_idx = (tk, tn), (lambda i, j, kk: (kk, j))
        else:
            b_block, b_idx = (tn, tk), (lambda i, j, kk: (j, kk))
    if b_halves:
        assert mode == "tn" and b_lead is None and b_last % tn == 0
        per_half = b_last // tn
        b_spec = pl.BlockSpec((None,) + b_block, lambda i, j, kk: (j // per_half, kk, j % per_half))
    elif b_lead is None:
        b_spec = pl.BlockSpec(b_block, b_idx)
    else:
        b_spec = pl.BlockSpec((None,) + b_block, lambda i, j, kk: (b_lead,) + b_idx(i, j, kk))
    in_specs = [a_spec, b_spec]
    operands = [a, b]
    if res is not None:
        in_specs.append(pl.BlockSpec((tm, tn), lambda i, j, kk: (i, j)))
        operands.append(res)
    if out_shards is None:
        out_shape = jax.ShapeDtypeStruct((m, n), out_dtype)
        out_spec = pl.BlockSpec((tm, tn), lambda i, j, kk: (i, j))
    else:
        per = n // out_shards
        assert per % tn == 0
        nps = per // tn
        out_shape = jax.ShapeDtypeStruct((out_shards, m, per), out_dtype)
        out_spec = pl.BlockSpec((None, tm, tn), lambda i, j, kk: (j // nps, i, j % nps))

    def body(*refs):
        a_ref, b_ref = refs[0], refs[1]
        res_ref = refs[2] if res is not None else None
        o_ref = refs[3] if res is not None else refs[2]
        part = lax.dot_general(a_ref[...].astype(BF16), b_ref[...].astype(BF16), dims, preferred_element_type=F32)

        def finish(acc):
            if res_ref is not None:
                acc = acc + res_ref[...]
            o_ref[...] = acc.astype(out_dtype)

        if nk == 1:
            finish(part)
        else:
            acc_ref = refs[-1]
            kk = pl.program_id(2)

            @pl.when(kk == 0)
            def _():
                acc_ref[...] = part

            @pl.when(kk > 0)
            def _():
                acc_ref[...] += part

            @pl.when(kk == nk - 1)
            def _():
                finish(acc_ref[...])

    scratch = [] if nk == 1 else [pltpu.VMEM((tm, tn), F32)]
    outs = _call_with_carry(body, out_shape=(out_shape,), grid=grid, in_specs=in_specs, out_specs=(out_spec,),
                            scratch_shapes=scratch, name=name, semantics=("parallel", "parallel", "arbitrary"),
                            operands=operands, carry=carry)
    return outs[0] if carry is None else (outs[0], outs[1:])


ROW_CHUNK = 256
FFN_TILE = 1408
TOKEN_STEP = 2048
NT_DIMS = (((1,), (1,)), ((), ()))


def _norm_rows_into(x_ref, g_ref, h_ref, hs_ref, tm):
    rc = min(ROW_CHUNK, tm)
    for r0 in range(0, tm, rc):
        xv = x_ref[r0:r0 + rc, :]
        r = lax.rsqrt(jnp.mean(xv * xv, axis=-1, keepdims=True) + EPS)
        hv = (xv * r * g_ref[...]).astype(BF16)
        hs_ref[r0:r0 + rc, :] = hv
        h_ref[r0:r0 + rc, :] = hv


def _norm_mm(x, g, b, b_lead, *, tn, name, carry=None):
    s, d = x.shape
    n = b.shape[-1]
    tm, tn = _blk(s, 1024), _blk(n, tn)

    def body(x_ref, g_ref, b_ref, h_ref, o_ref, hs_ref):
        @pl.when(pl.program_id(1) == 0)
        def _():
            _norm_rows_into(x_ref, g_ref, h_ref, hs_ref, tm)

        o_ref[...] = jnp.dot(hs_ref[...], b_ref[...], preferred_element_type=F32).astype(BF16)

    row = pl.BlockSpec((tm, d), lambda i, j: (i, 0))
    outs = _call_with_carry(body, out_shape=(jax.ShapeDtypeStruct((s, d), BF16), jax.ShapeDtypeStruct((s, n), BF16)),
                            grid=(s // tm, n // tn),
                            in_specs=[row, pl.BlockSpec((1, d), lambda i, j: (0, 0)),
                                      pl.BlockSpec((None, d, tn), lambda i, j: (b_lead, 0, j))],
                            out_specs=(row, pl.BlockSpec((tm, tn), lambda i, j: (i, j))),
                            scratch_shapes=[pltpu.VMEM((tm, d), BF16)], name=name,
                            semantics=("parallel", "arbitrary"), operands=[x, g, b], carry=carry)
    return outs if carry is None else (outs[0], outs[1], outs[2:])


def _norm_ffn_in(x, g, b, b_lead, name, carry=None):
    s, d = x.shape
    tm, tn = _blk(s, 512), FFN_TILE
    nj = FFN_HIDDEN // tn

    def body(x_ref, g_ref, bg_ref, bu_ref, h_ref, gu_ref, act_ref, hs_ref):
        @pl.when(pl.program_id(1) == 0)
        def _():
            _norm_rows_into(x_ref, g_ref, h_ref, hs_ref, tm)

        hv = hs_ref[...]
        gate = jnp.dot(hv, bg_ref[...], preferred_element_type=F32)
        up = jnp.dot(hv, bu_ref[...], preferred_element_type=F32)
        gu_ref[0] = gate.astype(BF16)
        gu_ref[1] = up.astype(BF16)
        act_ref[...] = (gate * _sigmoid(gate) * up).astype(BF16)

    row = pl.BlockSpec((tm, d), lambda i, j: (i, 0))
    out_shape = (jax.ShapeDtypeStruct((s, d), BF16), jax.ShapeDtypeStruct((2, s, FFN_HIDDEN), BF16),
                 jax.ShapeDtypeStruct((s, FFN_HIDDEN), BF16))
    outs = _call_with_carry(body, out_shape=out_shape, grid=(s // tm, nj),
                            in_specs=[row, pl.BlockSpec((1, d), lambda i, j: (0, 0)),
                                      pl.BlockSpec((None, d, tn), lambda i, j: (b_lead, 0, j)),
                                      pl.BlockSpec((None, d, tn), lambda i, j: (b_lead, 0, j + nj))],
                            out_specs=(row, pl.BlockSpec((2, tm, tn), lambda i, j: (0, i, j)),
                                       pl.BlockSpec((tm, tn), lambda i, j: (i, j))),
                            scratch_shapes=[pltpu.VMEM((tm, d), BF16)], name=name,
                            semantics=("parallel", "arbitrary"), operands=[x, g, b, b], carry=carry)
    return outs if carry is None else (outs[0], outs[1], outs[2], outs[3:])


def _d_act_swiglu(dx, w, b_lead, gu, name):
    s, d = dx.shape
    tm, tn = _blk(s, 512), FFN_TILE

    def body(dx_ref, w_ref, gu_ref, o_ref):
        dact = lax.dot_general(dx_ref[...].astype(BF16), w_ref[...], NT_DIMS, preferred_element_type=F32)
        gate = gu_ref[0].astype(F32)
        up = gu_ref[1].astype(F32)
        sg = _sigmoid(gate)
        o_ref[0] = (dact * up * sg * (1.0 + gate * (1.0 - sg))).astype(BF16)
        o_ref[1] = (dact * gate * sg).astype(BF16)

    halves = pl.BlockSpec((2, tm, tn), lambda j, i: (0, i, j))
    return pl.pallas_call(body, out_shape=jax.ShapeDtypeStruct(gu.shape, BF16), grid=(FFN_HIDDEN // tn, s // tm),
                          in_specs=[pl.BlockSpec((tm, d), lambda j, i: (i, 0)),
                                    pl.BlockSpec((None, tn, d), lambda j, i: (b_lead, j, 0)), halves],
                          out_specs=halves, name=name,
                          compiler_params=_params(("parallel", "parallel")))(dx, w, gu)


def _mm_nt_norm_bwd(a, b, b_lead, x, g, dres, *, tm, name, a_halves=False, carry=None):
    s, d = x.shape
    kdim = b.shape[-1]
    tm = _blk(s, tm)
    rc = min(ROW_CHUNK, tm)
    if a_halves:
        a_spec = pl.BlockSpec((2, tm, kdim // 2), lambda i: (0, i, 0))
    else:
        a_spec = pl.BlockSpec((tm, kdim), lambda i: (i, 0))

    def body(a_ref, b_ref, x_ref, g_ref, r_ref, dx_ref, dg_ref):
        i = pl.program_id(0)
        av = jnp.concatenate([a_ref[0], a_ref[1]], axis=1) if a_halves else a_ref[...]
        dh = lax.dot_general(av.astype(BF16), b_ref[...], NT_DIMS, preferred_element_type=F32)
        gv = g_ref[...]
        dg_part = None
        for r0 in range(0, tm, rc):
            xv = x_ref[r0:r0 + rc, :]
            dhv = dh[r0:r0 + rc, :]
            r = lax.rsqrt(jnp.mean(xv * xv, axis=-1, keepdims=True) + EPS)
            xhat = xv * r
            p = jnp.sum(dhv * xhat, axis=0, keepdims=True)
            dg_part = p if dg_part is None else dg_part + p
            dxhat = dhv * gv
            dx_ref[r0:r0 + rc, :] = r_ref[r0:r0 + rc, :] + r * (
                dxhat - xhat * jnp.mean(dxhat * xhat, axis=-1, keepdims=True))

        @pl.when(i == 0)
        def _():
            dg_ref[...] = dg_part

        @pl.when(i > 0)
        def _():
            dg_ref[...] += dg_part

    row = pl.BlockSpec((tm, d), lambda i: (i, 0))
    vec = pl.BlockSpec((1, d), lambda i: (0, 0))
    b_spec = pl.BlockSpec((None, d, kdim), lambda i: (b_lead, 0, 0), pipeline_mode=pl.Buffered(1))
    outs = _call_with_carry(body, out_shape=(jax.ShapeDtypeStruct((s, d), F32), jax.ShapeDtypeStruct((1, d), F32)),
                            grid=(s // tm,), in_specs=[a_spec, b_spec, row, vec, row], out_specs=(row, vec),
                            scratch_shapes=[], name=name, semantics=("arbitrary",), operands=[a, b, x, g, dres],
                            carry=carry)
    return outs if carry is None else (outs[0], outs[1], outs[2:])


def _rms_fwd(x, g, name):
    s, d = x.shape
    tm = _blk(s, 512)

    def body(x_ref, g_ref, o_ref):
        xv = x_ref[...]
        r = lax.rsqrt(jnp.mean(xv * xv, axis=-1, keepdims=True) + EPS)
        o_ref[...] = (xv * r * g_ref[...]).astype(BF16)

    return pl.pallas_call(body, out_shape=jax.ShapeDtypeStruct((s, d), BF16), grid=(s // tm,),
                          in_specs=[pl.BlockSpec((tm, d), lambda i: (i, 0)), pl.BlockSpec((1, d), lambda i: (0, 0))],
                          out_specs=pl.BlockSpec((tm, d), lambda i: (i, 0)), name=name,
                          compiler_params=_params(("parallel",)))(x, g)


def _rms_bwd(x, g, dh, dres, name):
    s, d = x.shape
    tm = _blk(s, 512)
    has_res = dres is not None

    def body(*refs):
        x_ref, g_ref, dh_ref = refs[0], refs[1], refs[2]
        dx_ref, dg_ref = refs[-2], refs[-1]
        xv = x_ref[...]
        r = lax.rsqrt(jnp.mean(xv * xv, axis=-1, keepdims=True) + EPS)
        xhat = xv * r
        dhv = dh_ref[...].astype(F32)
        part = jnp.sum(dhv * xhat, axis=0, keepdims=True)

        @pl.when(pl.program_id(0) == 0)
        def _():
            dg_ref[...] = part

        @pl.when(pl.program_id(0) > 0)
        def _():
            dg_ref[...] += part

        dxhat = dhv * g_ref[...]
        dx = r * (dxhat - xhat * jnp.mean(dxhat * xhat, axis=-1, keepdims=True))
        if has_res:
            dx = dx + refs[3][...]
        dx_ref[...] = dx

    row = pl.BlockSpec((tm, d), lambda i: (i, 0))
    vec = pl.BlockSpec((1, d), lambda i: (0, 0))
    in_specs = [row, vec, row] + ([row] if has_res else [])
    operands = [x, g, dh] + ([dres] if has_res else [])
    return pl.pallas_call(body, out_shape=(jax.ShapeDtypeStruct((s, d), F32), jax.ShapeDtypeStruct((1, d), F32)),
                          grid=(s // tm,), in_specs=in_specs, out_specs=(row, vec), name=name,
                          compiler_params=_params(("arbitrary",)))(*operands)


def _final_loss(x, g, target, name):
    s, d = x.shape
    tm = _blk(s, 512)

    def body(x_ref, g_ref, t_ref, loss_ref, dx_ref, dg_ref):
        xv = x_ref[...]
        gv = g_ref[...]
        r = lax.rsqrt(jnp.mean(xv * xv, axis=-1, keepdims=True) + EPS)
        xhat = xv * r
        err = xhat * gv - t_ref[...]
        lpart = 0.5 * jnp.sum(jnp.mean(err * err, axis=-1, keepdims=True), axis=0, keepdims=True)
        dy = err * (1.0 / d)
        gpart = jnp.sum(dy * xhat, axis=0, keepdims=True)

        @pl.when(pl.program_id(0) == 0)
        def _():
            loss_ref[...] = lpart
            dg_ref[...] = gpart

        @pl.when(pl.program_id(0) > 0)
        def _():
            loss_ref[...] += lpart
            dg_ref[...] += gpart

        dxhat = dy * gv
        dx_ref[...] = r * (dxhat - xhat * jnp.mean(dxhat * xhat, axis=-1, keepdims=True))

    row = pl.BlockSpec((tm, d), lambda i: (i, 0))
    vec = pl.BlockSpec((1, d), lambda i: (0, 0))
    one = pl.BlockSpec((1, 1), lambda i: (0, 0))
    return pl.pallas_call(body, out_shape=(jax.ShapeDtypeStruct((1, 1), F32), jax.ShapeDtypeStruct((s, d), F32),
                                           jax.ShapeDtypeStruct((1, d), F32)),
                          grid=(s // tm,), in_specs=[row, vec, row], out_specs=(one, row, vec), name=name,
                          compiler_params=_params(("arbitrary",)))(x, g, target)


def _erf_parts(x):
    cdf = 0.5 * (1.0 + lax.erf(x * (1.0 / math.sqrt(2.0))))
    return cdf


def _shift_down(ext, k):
    return pltpu.roll(ext, k, 0)


def _shift_up(ext, k):
    return pltpu.roll(ext, ext.shape[0] - k, 0)


def _mixer_forward_math(z, halo, row0, prm, tm, cdf=None):
    ln_g, ln_b, wtril, bsb, conv_w, wg, scale = prm
    za = z[:, 0:2 * A_WIDTH]
    gb = z[:, 1024:1536]
    gc = z[:, 1536:2048]
    xin = z[:, 2048:2560]
    zc = z[:, 2560:3072]
    if cdf is None:
        cdf = _erf_parts(za)
    act = za * cdf
    u = act[:, :A_WIDTH]
    v = act[:, A_WIDTH:]
    mu = jnp.mean(v, axis=-1, keepdims=True)
    vc = v - mu
    rstd = lax.rsqrt(jnp.mean(vc * vc, axis=-1, keepdims=True) + EPS)
    vhat = vc * rstd
    vn = (vhat * ln_g + ln_b).astype(BF16)
    rows = []
    for c in range(tm // CHUNK):
        cols = []
        for g in range(A_GROUPS):
            blk = vn[c * CHUNK:(c + 1) * CHUNK, g * CHUNK:(g + 1) * CHUNK]
            cols.append(jnp.dot(wtril[g], blk, preferred_element_type=F32) + bsb[g])
        rows.append(jnp.concatenate(cols, axis=1))
    mixed = jnp.concatenate(rows, axis=0)
    ya = u * mixed
    y = gc * xin
    yext = jnp.concatenate([halo[:, 0:512] * halo[:, 512:1024], y], axis=0)
    y1 = _shift_down(yext, 1)[HALO:]
    y2 = _shift_down(yext, 2)[HALO:]
    conv = conv_w[0:1, :] * y2 + conv_w[1:2, :] * y1 + conv_w[2:3, :] * y
    yb = gb * conv
    t = row0 + lax.broadcasted_iota(jnp.int32, (tm, 1), 0)
    zext = jnp.concatenate([halo[:, 1024:1536], zc], axis=0)
    pooled, p = [], []
    for gi, win in enumerate(POOL_WINDOWS):
        sw = zext[:, gi * C_GROUP:(gi + 1) * C_GROUP]
        for step in range(gi + 1):
            sw = sw + _shift_down(sw, 2 ** step)
        cnt = jnp.minimum(t + 1, win).astype(F32)
        pg = sw[HALO:] / cnt - zc[:, gi * C_GROUP:(gi + 1) * C_GROUP]
        pooled.append(pg)
        p.append(jnp.dot(pg.astype(BF16), wg[gi], preferred_element_type=F32))
    p = jnp.concatenate(p, axis=1)
    yc = p * scale
    saved = dict(za=za, cdf=cdf, u=u, mixed=mixed, rstd=rstd, vhat=vhat, vn=vn, gb=gb, gc=gc, xin=xin, y=y, y1=y1, y2=y2,
                 conv=conv, pooled=pooled, p=p, t=t)
    return ya, yb, yc, saved


def _mixer_specs(tm, s):
    nb16 = tm // HALO
    main = pl.BlockSpec((tm, MIX_COLS), lambda i: (i, 0))
    prev = pl.BlockSpec((HALO, 1536), lambda i: (jnp.maximum(i * nb16 - 1, 0), 1))
    return main, prev


def _full(shape):
    n = len(shape)
    return pl.BlockSpec(shape, lambda i: (0,) * n)


def _mixer_fwd(proj, prm, name):
    s = proj.shape[0]
    tm = _blk(s, 256)
    main, prev = _mixer_specs(tm, s)

    def body(z_ref, h_ref, lng, lnb, wt, bsb, cw, wg, sc, o_ref, cdf_ref):
        i = pl.program_id(0)
        halo = jnp.where(i > 0, h_ref[...].astype(F32), 0.0)
        prm_v = (lng[...], lnb[...], wt[...], bsb[...], cw[...], wg[...], sc[...])
        ya, yb, yc, sv = _mixer_forward_math(z_ref[...].astype(F32), halo, i * tm, prm_v, tm)
        o_ref[...] = jnp.concatenate([ya, yb, yc], axis=1).astype(BF16)
        cdf_ref[...] = sv["cdf"].astype(BF16)

    return pl.pallas_call(body, out_shape=(jax.ShapeDtypeStruct((s, 1536), BF16),
                                           jax.ShapeDtypeStruct((s, 2 * A_WIDTH), BF16)), grid=(s // tm,),
                          in_specs=[main, prev] + [_full(p.shape) for p in prm],
                          out_specs=(pl.BlockSpec((tm, 1536), lambda i: (i, 0)),
                                     pl.BlockSpec((tm, 2 * A_WIDTH), lambda i: (i, 0))), name=name,
                          compiler_params=_params(("parallel",)))(proj, proj, *prm)


def _mixer_bwd(proj, dypre, dproj_in, prm, wtril_t, cdf, name):
    s = proj.shape[0]
    tm = _blk(s, 256)
    nblk = s // tm
    nb16 = tm // HALO
    main, prev = _mixer_specs(tm, s)
    nxt_row = lambda i: jnp.minimum((i + 1) * nb16, s // HALO - 1)
    next_dy = pl.BlockSpec((HALO, 1536), lambda i: (nxt_row(i), 0))
    next_gb = pl.BlockSpec((HALO, 512), lambda i: (nxt_row(i), 2))
    ones8 = jnp.ones((8, CHUNK), F32)

    def body(z_ref, h_ref, dy_ref, ndy_ref, ngb_ref, _alias, lng, lnb, wt, bsb, cw, wg, sc, wtt, ones_ref, cdf_ref,
             dz_ref, dlng_ref, dlnb_ref, dws_ref, dbs_ref, dcw_ref, dwg_ref, dsc_ref):
        i = pl.program_id(0)
        first = i == 0
        halo = jnp.where(i > 0, h_ref[...].astype(F32), 0.0)
        prm_v = (lng[...], lnb[...], wt[...], bsb[...], cw[...], wg[...], sc[...])
        _, _, _, sv = _mixer_forward_math(z_ref[...].astype(F32), halo, i * tm, prm_v, tm,
                                          cdf=cdf_ref[...].astype(F32))
        dy = dy_ref[...].astype(F32)
        dya, dyb, dyc = dy[:, 0:512], dy[:, 512:1024], dy[:, 1024:1536]
        not_last = i < nblk - 1
        ndy = jnp.where(not_last, ndy_ref[...].astype(F32), 0.0)
        ngb = ngb_ref[...].astype(F32)

        def accumulate(ref, val):
            @pl.when(first)
            def _():
                ref[...] = val

            @pl.when(jnp.logical_not(first))
            def _():
                ref[...] += val

        du = dya * sv["mixed"]
        dmixed = dya * sv["u"]
        vn = sv["vn"]
        wttv = wtt[...]
        dvn_rows = []
        dws_parts = [None] * A_GROUPS
        dbs_parts = [None] * A_GROUPS
        for c in range(tm // CHUNK):
            cols = []
            for g in range(A_GROUPS):
                dm = dmixed[c * CHUNK:(c + 1) * CHUNK, g * CHUNK:(g + 1) * CHUNK]
                dmb = dm.astype(BF16)
                vb = vn[c * CHUNK:(c + 1) * CHUNK, g * CHUNK:(g + 1) * CHUNK]
                w_part = lax.dot_general(dmb, vb, (((1,), (1,)), ((), ())), preferred_element_type=F32)
                b_part = lax.dot_general(ones_ref[...], dm, (((1,), (1,)), ((), ())), preferred_element_type=F32,
                                         precision=lax.Precision.HIGHEST)[0:1, :]
                dws_parts[g] = w_part if dws_parts[g] is None else dws_parts[g] + w_part
                dbs_parts[g] = b_part if dbs_parts[g] is None else dbs_parts[g] + b_part
                cols.append(jnp.dot(wttv[g], dmb, preferred_element_type=F32))
            dvn_rows.append(jnp.concatenate(cols, axis=1))
        dvn = jnp.concatenate(dvn_rows, axis=0)
        tri = lax.broadcasted_iota(jnp.int32, (CHUNK, CHUNK), 0) >= lax.broadcasted_iota(jnp.int32, (CHUNK, CHUNK), 1)
        accumulate(dws_ref, jnp.stack([jnp.where(tri, w, 0.0) for w in dws_parts], axis=0))
        accumulate(dbs_ref, jnp.concatenate(dbs_parts, axis=0))
        vhat = sv["vhat"]
        accumulate(dlng_ref, jnp.sum(dvn * vhat, axis=0, keepdims=True))
        accumulate(dlnb_ref, jnp.sum(dvn, axis=0, keepdims=True))
        dvhat = dvn * lng[...]
        dv = sv["rstd"] * (dvhat - jnp.mean(dvhat, axis=-1, keepdims=True)
                           - vhat * jnp.mean(dvhat * vhat, axis=-1, keepdims=True))
        za = sv["za"]
        dgelu = sv["cdf"] + za * jnp.exp(-0.5 * za * za) * (1.0 / math.sqrt(2.0 * math.pi))
        dza = jnp.concatenate([du, dv], axis=1) * dgelu
        cwv = cw[...]
        dconv = dyb * sv["gb"]
        dgb = dyb * sv["conv"]
        dcext = jnp.concatenate([dconv, ndy[:, 512:1024] * ngb], axis=0)
        d1 = _shift_up(dcext, 1)[:tm]
        d2 = _shift_up(dcext, 2)[:tm]
        dyy = cwv[2:3, :] * dconv + cwv[1:2, :] * d1 + cwv[0:1, :] * d2
        dgc = dyy * sv["xin"]
        dxin = dyy * sv["gc"]
        accumulate(dcw_ref, jnp.concatenate([jnp.sum(dconv * sv["y2"], axis=0, keepdims=True),
                                             jnp.sum(dconv * sv["y1"], axis=0, keepdims=True),
                                             jnp.sum(dconv * sv["y"], axis=0, keepdims=True)], axis=0))
        scv = sc[...]
        accumulate(dsc_ref, jnp.sum(dyc * sv["p"], axis=0, keepdims=True))
        dp = dyc * scv
        ndp = ndy[:, 1024:1536] * scv
        wgv = wg[...]
        dzc_cols, dwg_parts = [], []
        for gi, win in enumerate(POOL_WINDOWS):
            sl = slice(gi * C_GROUP, (gi + 1) * C_GROUP)
            dpb = dp[:, sl].astype(BF16)
            dpool = lax.dot_general(dpb, wgv[gi], (((1,), (1,)), ((), ())), preferred_element_type=F32)
            ndpool = lax.dot_general(ndp[:, sl].astype(BF16), wgv[gi], (((1,), (1,)), ((), ())),
                                     preferred_element_type=F32)
            dwg_parts.append(lax.dot_general(sv["pooled"][gi].astype(BF16), dpb, (((0,), (0,)), ((), ())),
                                             preferred_element_type=F32))
            cnt = jnp.minimum(sv["t"] + 1, win).astype(F32)
            fw = jnp.concatenate([dpool / cnt, ndpool * (1.0 / win)], axis=0)
            for step in range(gi + 1):
                fw = fw + _shift_up(fw, 2 ** step)
            dzc_cols.append(fw[:tm] - dpool)
        accumulate(dwg_ref, jnp.stack(dwg_parts, axis=0))
        dzc = jnp.concatenate(dzc_cols, axis=1)
        dz_ref[...] = jnp.concatenate([dza, dgb, dgc, dxin, dzc], axis=1).astype(BF16)

    out_shape = (jax.ShapeDtypeStruct(dproj_in.shape, BF16), jax.ShapeDtypeStruct((1, A_WIDTH), F32),
                 jax.ShapeDtypeStruct((1, A_WIDTH), F32), jax.ShapeDtypeStruct((A_GROUPS, CHUNK, CHUNK), F32),
                 jax.ShapeDtypeStruct((A_GROUPS, CHUNK), F32), jax.ShapeDtypeStruct((3, B_WIDTH), F32),
                 jax.ShapeDtypeStruct((4, C_GROUP, C_GROUP), F32), jax.ShapeDtypeStruct((1, C_WIDTH), F32))
    out_specs = (main,) + tuple(_full(o.shape) for o in out_shape[1:])
    in_specs = [main, prev, pl.BlockSpec((tm, 1536), lambda i: (i, 0)), next_dy, next_gb, ANY] + \
               [_full(p.shape) for p in prm] + [_full(wtril_t.shape), _full(ones8.shape),
                                                pl.BlockSpec((tm, 2 * A_WIDTH), lambda i: (i, 0))]
    return pl.pallas_call(body, out_shape=out_shape, grid=(nblk,), in_specs=in_specs, out_specs=out_specs,
                          input_output_aliases={5: 0}, name=name,
                          compiler_params=_params(("arbitrary",)))(proj, proj, dypre, dypre, proj, dproj_in, *prm,
                                                                  wtril_t, ones8, cdf)


def _sigmoid(x):
    return 1.0 / (1.0 + jnp.exp(-x))


def _merge_fwd(ypre, proj, wa, wb, wc, bgate, name):
    s = ypre.shape[0]
    tm = _blk(s, 512)

    def body(y_ref, zg_ref, wa_ref, wb_ref, wc_ref, bg_ref, o_ref):
        yv = y_ref[...]
        acc = None
        for i, w_ref in enumerate((wa_ref, wb_ref, wc_ref)):
            br = jnp.dot(yv[:, i * 512:(i + 1) * 512], w_ref[...], preferred_element_type=F32)
            gate = _sigmoid(zg_ref[:, i * D_MODEL:(i + 1) * D_MODEL].astype(F32) + bg_ref[i:i + 1, :])
            acc = gate * br if acc is None else acc + gate * br
        o_ref[...] = acc.astype(BF16)

    wspec = _full(wa.shape)
    return pl.pallas_call(body, out_shape=jax.ShapeDtypeStruct((s, D_MODEL), BF16), grid=(s // tm,),
                          in_specs=[pl.BlockSpec((tm, 1536), lambda i: (i, 0)),
                                    pl.BlockSpec((tm, 3 * D_MODEL), lambda i: (i, 1)), wspec, wspec, wspec,
                                    _full(bgate.shape)],
                          out_specs=pl.BlockSpec((tm, D_MODEL), lambda i: (i, 0)), name=name,
                          compiler_params=_params(("parallel",)))(ypre, proj, wa, wb, wc, bgate)


def _merge_bwd(ypre, proj, dmerged, wa, wb, wc, bgate, name, carry=None):
    s = ypre.shape[0]
    tm = _blk(s, 512)

    def body(y_ref, zg_ref, dm_ref, wa_ref, wb_ref, wc_ref, bg_ref, dyp_ref, dzg_ref, dbg_ref, dwa_ref, dwb_ref,
             dwc_ref):
        first = pl.program_id(0) == 0

        def accumulate(ref, val):
            @pl.when(first)
            def _():
                ref[...] = val

            @pl.when(jnp.logical_not(first))
            def _():
                ref[...] += val

        yv = y_ref[...]
        dm = dm_ref[...].astype(F32)
        dyp, dzg, dbg = [], [], []
        for i, (w_ref, dw_ref) in enumerate(((wa_ref, dwa_ref), (wb_ref, dwb_ref), (wc_ref, dwc_ref))):
            wv = w_ref[...]
            yi = yv[:, i * 512:(i + 1) * 512]
            br = jnp.dot(yi, wv, preferred_element_type=F32)
            gate = _sigmoid(zg_ref[:, i * D_MODEL:(i + 1) * D_MODEL].astype(F32) + bg_ref[i:i + 1, :])
            dbi = (dm * gate).astype(BF16)
            dzi = dm * br * gate * (1.0 - gate)
            dzg.append(dzi.astype(BF16))
            dbg.append(jnp.sum(dzi, axis=0, keepdims=True))
            dyp.append(lax.dot_general(dbi, wv, (((1,), (1,)), ((), ())), preferred_element_type=F32).astype(BF16))
            accumulate(dw_ref, lax.dot_general(yi, dbi, (((0,), (0,)), ((), ())), preferred_element_type=F32))
        dyp_ref[...] = jnp.concatenate(dyp, axis=1)
        dzg_ref[...] = jnp.concatenate(dzg, axis=1)
        accumulate(dbg_ref, jnp.concatenate(dbg, axis=0))

    wspec = _full(wa.shape)
    dw_shape = jax.ShapeDtypeStruct(wa.shape, F32)
    out_shape = (jax.ShapeDtypeStruct((s, 1536), BF16), jax.ShapeDtypeStruct((s, IN_COLS), BF16),
                 jax.ShapeDtypeStruct((3, D_MODEL), F32), dw_shape, dw_shape, dw_shape)
    outs = _call_with_carry(body, out_shape=out_shape, grid=(s // tm,),
                            in_specs=[pl.BlockSpec((tm, 1536), lambda i: (i, 0)),
                                      pl.BlockSpec((tm, 3 * D_MODEL), lambda i: (i, 1)),
                                      pl.BlockSpec((tm, D_MODEL), lambda i: (i, 0)), wspec, wspec, wspec,
                                      _full(bgate.shape)],
                            out_specs=(pl.BlockSpec((tm, 1536), lambda i: (i, 0)),
                                       pl.BlockSpec((tm, 3 * D_MODEL), lambda i: (i, 1)), _full((3, D_MODEL)),
                                       wspec, wspec, wspec),
                            scratch_shapes=[], name=name, semantics=("arbitrary",),
                            operands=[ypre, proj, dmerged, wa, wb, wc, bgate], carry=carry)
    return tuple(outs) if carry is None else tuple(outs[:6]) + (outs[6:],)


def _softmax_rows(q, k):
    sc = lax.dot_general(q, k, (((1,), (1,)), ((), ())), preferred_element_type=F32) * (HEAD_DIM ** -0.5)
    e = jnp.exp(sc - jnp.max(sc, axis=-1, keepdims=True))
    return e / jnp.sum(e, axis=-1, keepdims=True)


def _attn_fwd(q, kv, name):
    s = q.shape[0]
    tm = _blk(s, 512)

    def body(q_ref, kv_ref, o_ref):
        outs = []
        for h in range(XATTN_HEADS):
            sl = slice(h * HEAD_DIM, (h + 1) * HEAD_DIM)
            p = _softmax_rows(q_ref[:, sl], kv_ref[:, sl])
            outs.append(jnp.dot(p.astype(BF16), kv_ref[:, D_MODEL + h * HEAD_DIM:D_MODEL + (h + 1) * HEAD_DIM],
                                preferred_element_type=F32))
        o_ref[...] = jnp.concatenate(outs, axis=1).astype(BF16)

    return pl.pallas_call(body, out_shape=jax.ShapeDtypeStruct((s, D_MODEL), BF16), grid=(s // tm,),
                          in_specs=[pl.BlockSpec((tm, D_MODEL), lambda i: (i, 0)), _full(kv.shape)],
                          out_specs=pl.BlockSpec((tm, D_MODEL), lambda i: (i, 0)), name=name,
                          compiler_params=_params(("parallel",)))(q, kv)


def _attn_bwd(q, kv, do, name):
    s = q.shape[0]
    tm = _blk(s, 512)

    def body(q_ref, kv_ref, do_ref, dq_ref, dkv_ref):
        dqs, dks, dvs = [], [], []
        for h in range(XATTN_HEADS):
            sl = slice(h * HEAD_DIM, (h + 1) * HEAD_DIM)
            vsl = slice(D_MODEL + h * HEAD_DIM, D_MODEL + (h + 1) * HEAD_DIM)
            qh, kh, vh, doh = q_ref[:, sl], kv_ref[:, sl], kv_ref[:, vsl], do_ref[:, sl]
            p = _softmax_rows(qh, kh)
            pb = p.astype(BF16)
            dvs.append(lax.dot_general(pb, doh, (((0,), (0,)), ((), ())), preferred_element_type=F32))
            dp = lax.dot_general(doh, vh, (((1,), (1,)), ((), ())), preferred_element_type=F32)
            ds = p * (dp - jnp.sum(dp * p, axis=-1, keepdims=True)) * (HEAD_DIM ** -0.5)
            dsb = ds.astype(BF16)
            dqs.append(jnp.dot(dsb, kh, preferred_element_type=F32))
            dks.append(lax.dot_general(dsb, qh, (((0,), (0,)), ((), ())), preferred_element_type=F32))
        dq_ref[...] = jnp.concatenate(dqs, axis=1).astype(BF16)
        part = jnp.concatenate(dks + dvs, axis=1)

        @pl.when(pl.program_id(0) == 0)
        def _():
            dkv_ref[...] = part

        @pl.when(pl.program_id(0) > 0)
        def _():
            dkv_ref[...] += part

    row = pl.BlockSpec((tm, D_MODEL), lambda i: (i, 0))
    return pl.pallas_call(body, out_shape=(jax.ShapeDtypeStruct((s, D_MODEL), BF16),
                                           jax.ShapeDtypeStruct(kv.shape, F32)),
                          grid=(s // tm,), in_specs=[row, _full(kv.shape), row], out_specs=(row, _full(kv.shape)),
                          name=name, compiler_params=_params(("arbitrary",)))(q, kv, do)


def _rows_block(rows, cols):
    target = (512 * 1024) // cols
    fits = [cand for cand in range(8, rows + 1, 8) if rows % cand == 0 and cand <= target]
    return fits[-1] if fits else rows


def _elementwise(fn, ins, out_dtypes, name):
    lead, rows, cols = ins[0].shape
    tr = _rows_block(rows, cols)
    spec = pl.BlockSpec((None, tr, cols), lambda l, i: (l, i, 0))
    n_in = len(ins)

    def body(*refs):
        for o_ref, o in zip(refs[n_in:], fn(*[r[...] for r in refs[:n_in]])):
            o_ref[...] = o.astype(o_ref.dtype)

    out_shape = tuple(jax.ShapeDtypeStruct((lead, rows, cols), dt) for dt in out_dtypes)
    return pl.pallas_call(body, out_shape=out_shape, grid=(lead, rows // tr), in_specs=[spec] * n_in,
                          out_specs=tuple([spec] * len(out_dtypes)), name=name,
                          compiler_params=_params(("parallel", "parallel")))(*ins)


def _cast_to_slots(w, chip, name):
    _, rows, cols = w.shape
    tr = _rows_block(rows, cols)

    def body(chip_ref, w_ref, o0_ref, o1_ref):
        o0_ref[...] = w_ref[0].astype(BF16)
        o1_ref[...] = w_ref[1].astype(BF16)

    slot = pl.BlockSpec((None, tr, cols), lambda i, chip_ref: (chip_ref[0], i, 0))
    grid_spec = pltpu.PrefetchScalarGridSpec(
        num_scalar_prefetch=1, grid=(rows // tr,),
        in_specs=[pl.BlockSpec((DEPTH, tr, cols), lambda i, chip_ref: (0, i, 0))], out_specs=(slot, slot))
    shape = jax.ShapeDtypeStruct((N_CHIPS, rows, cols), BF16)
    return pl.pallas_call(body, out_shape=(shape, shape), grid_spec=grid_spec, name=name,
                          compiler_params=_params(("parallel",)))(chip, w)


def _chip_sum(pair0, rx0, pair1, rx1, chip, name):
    _, rows, cols = pair0.shape
    tr = _rows_block(rows, cols)

    def body(chip_ref, p0_ref, rx0_ref, p1_ref, rx1_ref, o_ref):
        for layer, (p_ref, rx_ref) in enumerate(((p0_ref, rx0_ref), (p1_ref, rx1_ref))):
            acc = p_ref[...].astype(F32)
            for j in range(3):
                acc = acc + rx_ref[j].astype(F32)
            o_ref[layer] = acc

    mine = pl.BlockSpec((None, tr, cols), lambda i, chip_ref: (chip_ref[0], i, 0))
    theirs = pl.BlockSpec((3, tr, cols), lambda i, chip_ref: (0, i, 0))
    grid_spec = pltpu.PrefetchScalarGridSpec(
        num_scalar_prefetch=1, grid=(rows // tr,), in_specs=[mine, theirs, mine, theirs],
        out_specs=pl.BlockSpec((DEPTH, tr, cols), lambda i, chip_ref: (0, i, 0)))
    return pl.pallas_call(body, out_shape=jax.ShapeDtypeStruct((DEPTH, rows, cols), F32), grid_spec=grid_spec,
                          name=name, compiler_params=_params(("parallel",)))(chip, pair0, rx0, pair1, rx1)


def _chip_sum_ordered(own, rx, chip, name):
    rows, cols = own.shape

    def body(chip_ref, own_ref, rx_ref, o_ref):
        me = chip_ref[0]
        acc = None
        for k in range(N_CHIPS):
            rel = jnp.bitwise_xor(me, k)
            term = jnp.where(rel == 0, own_ref[...],
                             jnp.where(rel == 2, rx_ref[0], jnp.where(rel == 1, rx_ref[1], rx_ref[2])))
            acc = term if acc is None else acc + term
        o_ref[...] = acc

    grid_spec = pltpu.PrefetchScalarGridSpec(
        num_scalar_prefetch=1, grid=(1,),
        in_specs=[pl.BlockSpec((rows, cols), lambda i, chip_ref: (0, 0)),
                  pl.BlockSpec((3, rows, cols), lambda i, chip_ref: (0, 0, 0))],
        out_specs=pl.BlockSpec((rows, cols), lambda i, chip_ref: (0, 0)))
    return pl.pallas_call(body, out_shape=jax.ShapeDtypeStruct((rows, cols), F32), grid_spec=grid_spec, name=name,
                          compiler_params=_params(("arbitrary",)))(chip, own, rx)


def _adamw_math(w, g, m, v):
    m = ADAM_B1 * m + (1.0 - ADAM_B1) * g
    v = ADAM_B2 * v + (1.0 - ADAM_B2) * (g * g)
    m_hat = m / (1.0 - ADAM_B1 ** ADAM_STEP)
    v_hat = v / (1.0 - ADAM_B2 ** ADAM_STEP)
    delta = -ADAM_LR * (m_hat / (jnp.sqrt(v_hat) + ADAM_EPS) + ADAM_WD * w)
    return delta, m, v


def _adamw_small(ws, gs, ms, vs, name):
    n = len(ws)

    def body(*refs):
        for i in range(n):
            w_ref, g_ref, m_ref, v_ref = (refs[k * n + i] for k in range(4))
            d_out, m_out, v_out = (refs[(4 + k) * n + i] for k in range(3))
            d_out[...], m_out[...], v_out[...] = _adamw_math(w_ref[...], g_ref[...], m_ref[...], v_ref[...])

    vmem = pl.BlockSpec(memory_space=pltpu.VMEM)
    shapes = tuple(jax.ShapeDtypeStruct(w.shape, F32) for w in ws)
    outs = pl.pallas_call(body, out_shape=shapes * 3, in_specs=[vmem] * (4 * n), out_specs=tuple([vmem] * (3 * n)),
                          name=name, compiler_params=_params())(*ws, *gs, *ms, *vs)
    return outs[:n], outs[n:2 * n], outs[2 * n:]


def _adamw_shard(w, mine, other, m, v, core, name):
    lead, rows, cols = w.shape
    half = rows // 2
    tr = _rows_block(half, cols)
    nhalf = half // tr

    def body(core_ref, w_ref, mine_ref, other_ref, m_ref, v_ref, g_out, d_out, m_out, v_out):
        g = jnp.where(pl.program_id(1) // nhalf == core_ref[0], mine_ref[...], other_ref[...])
        d_new, m_new, v_new = _adamw_math(w_ref[...], g, m_ref[...], v_ref[...])
        g_out[...] = g
        d_out[...] = d_new
        m_out[...] = m_new
        v_out[...] = v_new

    whole = pl.BlockSpec((None, tr, cols), lambda l, i, core_ref: (l, i, 0))
    own = pl.BlockSpec((None, tr, cols),
                       lambda l, i, core_ref: (l, jnp.where(i // nhalf == core_ref[0], i % nhalf, 0), 0))
    far = pl.BlockSpec((None, tr, cols),
                       lambda l, i, core_ref: (l, jnp.where(i // nhalf == core_ref[0], 0, i % nhalf), 0))
    shape = jax.ShapeDtypeStruct(w.shape, F32)
    grid_spec = pltpu.PrefetchScalarGridSpec(num_scalar_prefetch=1, grid=(lead, rows // tr),
                                             in_specs=[whole, own, far, whole, whole], out_specs=(whole,) * 4)
    return pl.pallas_call(body, out_shape=(shape,) * 4, grid_spec=grid_spec, name=name,
                          compiler_params=_params(("parallel", "parallel")))(core, w, mine, other, m, v)


def _place():
    x, y, c = lax.axis_index("x"), lax.axis_index("y"), lax.axis_index("c")
    chips = [(1 - x, y), (x, 1 - y), (1 - x, 1 - y)]
    return x, y, c, 2 * x + y, chips


def _run_remote(copies):
    for send, _ in copies:
        send.start()
    for send, recv in copies:
        send.wait_send()
        recv.wait_recv()


def _half_rows(ref, c):
    half = ref.shape[1] // 2
    return pl.ds(pl.multiple_of(c * half, 8), half)


def _gather_issue(whole):
    def issue(_, refs, ssem, rsem):
        x, y, c, me, chips = _place()
        copies = []
        for w, ref in enumerate(refs):
            part = (me,) if w >= len(refs) - whole else (me, _half_rows(ref, c))
            mine = ref.at[part]
            for j, (px, py) in enumerate(chips):
                sems = (ssem.at[3 * w + j], rsem.at[3 * w + j])
                peer = dict(device_id=(px, py, c), device_id_type=MESH)
                send = pltpu.make_async_remote_copy(mine, mine, *sems, **peer)
                recv = pltpu.make_async_remote_copy(mine, ref.at[(2 * px + py,) + part[1:]], *sems, **peer)
                copies.append((send, recv))
        return copies
    return issue


def _gather_carry(fulls, whole=0):
    return _Carry(arrays=tuple(fulls), outs=tuple(jax.ShapeDtypeStruct(f.shape, f.dtype) for f in fulls),
                  in_place=True, copies=3 * len(fulls), issue=_gather_issue(whole))


def _scatter_issue(ins, outs, ssem, rsem):
    x, y, c, me, chips = _place()
    copies = []
    for w, (src, dst) in enumerate(zip(ins, outs)):
        for j, (px, py) in enumerate(chips):
            cp = pltpu.make_async_remote_copy(src.at[2 * px + py], dst.at[j], ssem.at[3 * w + j], rsem.at[3 * w + j],
                                              device_id=(px, py, c), device_id_type=MESH)
            copies.append((cp, cp))
    return copies


def _scatter_carry(pairs):
    return _Carry(arrays=tuple(pairs), outs=tuple(jax.ShapeDtypeStruct((3,) + p.shape[1:], p.dtype) for p in pairs),
                  in_place=False, copies=3 * len(pairs), issue=_scatter_issue)


def _chip_gather(fulls, whole, name):
    n = len(fulls)
    issue = _gather_issue(whole)

    def body(*refs):
        _run_remote(issue(None, refs[n:2 * n], refs[2 * n], refs[2 * n + 1]))

    out_shape = tuple(jax.ShapeDtypeStruct(f.shape, f.dtype) for f in fulls)
    return pl.pallas_call(body, out_shape=out_shape, in_specs=[ANY] * n, out_specs=tuple([ANY] * n),
                          input_output_aliases={w: w for w in range(n)},
                          scratch_shapes=[pltpu.SemaphoreType.DMA((3 * n,)), pltpu.SemaphoreType.DMA((3 * n,))],
                          name=name)(*fulls)


class _Blocks(NamedTuple):
    tile: tuple
    dtype: object
    moves: list
    landed: object


STAGE_BYTES = 4 * 1024 * 1024


def _rows(start, size):
    return pl.ds(start if isinstance(start, int) else pl.multiple_of(start, 8), size)


def _stage_tile(lead, rows, cols, dtype):
    per_row = cols * jnp.dtype(dtype).itemsize
    if lead * rows * per_row <= STAGE_BYTES:
        return lead, rows
    fits = [cand for cand in range(8, rows + 1, 8) if rows % cand == 0 and cand * per_row <= STAGE_BYTES]
    return 1, (fits[-1] if fits else rows)


def _plan_gathered(_, refs, c, me):
    work = []
    for ref in refs:
        lead, rows, cols = ref.shape
        half = rows // 2
        nk, tr = _stage_tile(lead, half, cols, ref.dtype)
        if nk == lead:
            part = ref.at[pl.ds(0, lead), _rows(c * half, half)]
            moves, landed = [(part, part)], ref.at[pl.ds(0, lead), pl.ds(0, half)]
        else:
            moves = []
            for j in range(N_CHIPS - 1):
                k = jnp.bitwise_xor(me, j + 1)
                for b in range(half // tr):
                    part = ref.at[pl.ds(k, 1), _rows(c * half + b * tr, tr)]
                    moves.append((part, part))
            landed = ref.at[pl.ds(0, N_CHIPS - 1), pl.ds(0, half)]
        work.append(_Blocks((nk, tr, cols), ref.dtype, moves, landed))
    return work


def _plan_grad_halves(grads, rxs, c, me):
    work = []
    for g, rx in zip(grads, rxs):
        lead, half, cols = rx.shape
        nk, tr = _stage_tile(lead, half, cols, g.dtype)
        moves = [(g.at[pl.ds(k, nk), _rows((1 - c) * half + b * tr, tr)], rx.at[pl.ds(k, nk), _rows(b * tr, tr)])
                 for k in range(0, lead, nk) for b in range(half // tr)]
        work.append(_Blocks((nk, tr, cols), g.dtype, moves, rx))
    return work


def _plan_whole(srcs, dsts, c, me):
    work = []
    for src, dst in zip(srcs, dsts):
        lead, rows, cols = src.shape
        nk, tr = _stage_tile(lead, rows, cols, src.dtype)
        moves = [(src.at[pl.ds(k, nk), _rows(b * tr, tr)], dst.at[pl.ds(k, nk), _rows(b * tr, tr)])
                 for k in range(0, lead, nk) for b in range(rows // tr)]
        work.append(_Blocks((nk, tr, cols), src.dtype, moves, dst))
    return work


def _push_staged(arrays, outs, in_place, plan, name):
    n, no = len(arrays), len(outs)

    def body(*refs):
        ins, out_refs = refs[:n], refs[n:n + no]
        rsem, idle = refs[n + no], refs[n + no + 1]
        x, y, c, me, _ = _place()
        peer = dict(device_id=(x, y, 1 - c), device_id_type=MESH)
        work = plan(ins, out_refs, c, me)
        for w, blocks in enumerate(work):
            def staged(buf, lsem, ssem, w=w, blocks=blocks):
                sends = []
                for i, (src, dst) in enumerate(blocks.moves):
                    slot = i % 2
                    if i >= 2:
                        sends[i - 2].wait_send()
                    load = pltpu.make_async_copy(src, buf.at[slot], lsem.at[slot])
                    load.start()
                    load.wait()
                    send = pltpu.make_async_remote_copy(buf.at[slot], dst, ssem.at[slot], rsem.at[w], **peer)
                    send.start()
                    sends.append(send)
                for send in sends[-2:]:
                    send.wait_send()

            pl.run_scoped(staged, pltpu.VMEM((2,) + blocks.tile, blocks.dtype), pltpu.SemaphoreType.DMA((2,)),
                          pltpu.SemaphoreType.DMA((2,)))
        for w, blocks in enumerate(work):
            pltpu.make_async_remote_copy(blocks.landed, blocks.landed, idle, rsem.at[w], **peer).wait_recv()

    return pl.pallas_call(body, out_shape=tuple(outs), in_specs=[ANY] * n, out_specs=tuple([ANY] * no),
                          input_output_aliases={w: w for w in range(n)} if in_place else {},
                          scratch_shapes=[pltpu.SemaphoreType.DMA((n,)), pltpu.SemaphoreType.DMA],
                          name=name, compiler_params=_params())(*arrays)


def _pair_sum(g, rx, core, name):
    lead, half, cols = rx.shape
    tr = _rows_block(half, cols)
    nrow = half // tr

    def body(core_ref, g_ref, rx_ref, o_ref):
        o_ref[...] = (g_ref[...].astype(F32) + rx_ref[...].astype(F32)).astype(o_ref.dtype)

    spec = pl.BlockSpec((None, tr, cols), lambda k, i, core_ref: (k, i, 0))
    grid_spec = pltpu.PrefetchScalarGridSpec(
        num_scalar_prefetch=1, grid=(lead, nrow),
        in_specs=[pl.BlockSpec((None, tr, cols), lambda k, i, core_ref: (k, core_ref[0] * nrow + i, 0)), spec],
        out_specs=spec)
    return pl.pallas_call(body, out_shape=jax.ShapeDtypeStruct(rx.shape, rx.dtype), grid_spec=grid_spec, name=name,
                          compiler_params=_params(("parallel", "parallel")))(core, g, rx)


def _push_sibling(src, name):
    lead, rows, cols = src.shape
    tr = _rows_block(rows, cols)
    nrow = rows // tr

    def body(blk_ref, rx_ref, ssem, rsem):
        x, y, c = lax.axis_index("x"), lax.axis_index("y"), lax.axis_index("c")
        k, i = pl.program_id(0), pl.program_id(1)
        peer = dict(device_id=(x, y, 1 - c), device_id_type=MESH)
        dst = rx_ref.at[pl.ds(k, 1), pl.ds(pl.multiple_of(i * tr, 8), tr)]
        cp = pltpu.make_async_remote_copy(blk_ref, dst, ssem, rsem, **peer)
        cp.start()
        cp.wait_send()

        @pl.when(jnp.logical_and(k == lead - 1, i == nrow - 1))
        def _():
            pltpu.make_async_remote_copy(rx_ref, rx_ref, ssem, rsem, **peer).wait_recv()

    return pl.pallas_call(body, out_shape=jax.ShapeDtypeStruct(src.shape, src.dtype), grid=(lead, nrow),
                          in_specs=[pl.BlockSpec((1, tr, cols), lambda k, i: (k, i, 0))], out_specs=ANY,
                          scratch_shapes=[pltpu.SemaphoreType.DMA, pltpu.SemaphoreType.DMA], name=name,
                          compiler_params=_params(("arbitrary", "arbitrary")))(src)


def _chip_scatter(pairs, small, name):
    n = len(pairs)

    def body(*refs):
        ins, small_ref = refs[:n], refs[n]
        outs, small_out = refs[n + 1:2 * n + 1], refs[2 * n + 1]
        ssem, rsem = refs[2 * n + 2:]
        x, y, c, me, chips = _place()
        copies = []
        for w in range(n + 1):
            for j, (px, py) in enumerate(chips):
                src = ins[w].at[2 * px + py] if w < n else small_ref
                dst = (outs[w] if w < n else small_out).at[j]
                cp = pltpu.make_async_remote_copy(src, dst, ssem.at[3 * w + j], rsem.at[3 * w + j],
                                                  device_id=(px, py, c), device_id_type=MESH)
                copies.append((cp, cp))
        _run_remote(copies)

    out_shape = tuple(jax.ShapeDtypeStruct((3,) + it.shape[1:], it.dtype) for it in pairs) + \
                (jax.ShapeDtypeStruct((3,) + small.shape, small.dtype),)
    return pl.pallas_call(body, out_shape=out_shape, in_specs=[ANY] * (n + 1), out_specs=tuple([ANY] * (n + 1)),
                          scratch_shapes=[pltpu.SemaphoreType.DMA((3 * n + 3,)),
                                          pltpu.SemaphoreType.DMA((3 * n + 3,))],
                          name=name)(*pairs, small)


def _pack(arrays):
    flat = jnp.concatenate([a.reshape(-1).astype(F32) for a in arrays])
    pad = (-flat.shape[0]) % (8 * LANES)
    return jnp.pad(flat, (0, pad)).reshape(-1, LANES)


def _unpack(packed, shapes):
    flat = packed.reshape(-1)
    out, off = [], 0
    for shp in shapes:
        size = math.prod(shp)
        out.append(flat[off:off + size].reshape(shp))
        off += size
    return out


def kernel(x, mem, g_mix, w_in, b_gate, a_ln_g, a_ln_b, a_ws, a_bs, b_conv, c_wg, c_scale, w_branch_a, w_branch_b, w_branch_c, w_o, g_xattn, g_mem, w_xq, w_xkv, w_xo, g_ffn, w_ffn_in, w_ffn_out, g_final, loss_target, m_g_mix, m_w_in, m_b_gate, m_a_ln_g, m_a_ln_b, m_a_ws, m_a_bs, m_b_conv, m_c_wg, m_c_scale, m_w_branch_a, m_w_branch_b, m_w_branch_c, m_w_o, m_g_xattn, m_g_mem, m_w_xq, m_w_xkv, m_w_xo, m_g_ffn, m_w_ffn_in, m_w_ffn_out, m_g_final, v_g_mix, v_w_in, v_b_gate, v_a_ln_g, v_a_ln_b, v_a_ws, v_a_bs, v_b_conv, v_c_wg, v_c_scale, v_w_branch_a, v_w_branch_b, v_w_branch_c, v_w_o, v_g_xattn, v_g_mem, v_w_xq, v_w_xkv, v_w_xo, v_g_ffn, v_w_ffn_in, v_w_ffn_out, v_g_final):
    args = locals()
    wts = {n: args[n] for n in WEIGHTS}
    mom = {n: args["m_" + n] for n in WEIGHTS}
    var = {n: args["v_" + n] for n in WEIGHTS}
    xs = x[0]
    mems = mem[0]
    tgt = loss_target[0]
    chip = 2 * lax.axis_index("x") + lax.axis_index("y")

    bias_pack = jnp.zeros((DEPTH, 8, 384), F32)
    bias_pack = bias_pack.at[:, 0:3, 0:256].set(b_gate).at[:, 0:3, 256:384].set(b_conv)
    bias_slots = lax.dynamic_update_slice(jnp.zeros((N_CHIPS, DEPTH, 8, 384), F32), bias_pack[None], (chip, 0, 0, 0))
    core = lax.axis_index("c")
    chip_arr = jnp.reshape(chip, (1,)).astype(jnp.int32)
    core_arr = jnp.reshape(core, (1,)).astype(jnp.int32)
    slots = {n: _cast_to_slots(wts[n], chip_arr, name=f"cast_{n}") for n in BIG}
    layer_weights = [{}, {}]

    def land(l, names, gathered, tag):
        shapes = [jax.ShapeDtypeStruct(g.shape, g.dtype) for g in gathered]
        for n, g in zip(names, _push_staged(list(gathered), shapes, True, _plan_gathered,
                                            name=f"l{l}_gather_cores_{tag}")):
            if n in COL_SHARDED:
                layer_weights[l][n] = jnp.transpose(g, (1, 0, 2)).reshape(1, g.shape[1], N_CHIPS * g.shape[2])
            else:
                layer_weights[l][n] = g.reshape(1, N_CHIPS * g.shape[1], g.shape[2])

    def gather_ride(l, names):
        return _gather_carry([slots[n][l] for n in names])

    rest_of_first = tuple(n for n in BIG if n not in NEED_FIRST)
    most_of_second = tuple(n for n in BIG if n not in NEED_LAST)
    first = _chip_gather([slots[n][0] for n in NEED_FIRST] + [bias_slots], 1, name="l0_gather_chips")
    land(0, NEED_FIRST, first[:-1], "first")
    biases = first[-1]
    b_gate_full = jnp.transpose(biases[:, :, 0:3, 0:256], (1, 2, 0, 3)).reshape(DEPTH, 3, D_MODEL)
    b_conv_full = jnp.transpose(biases[:, :, 0:3, 256:384], (1, 2, 0, 3)).reshape(DEPTH, 3, B_WIDTH)

    tril = jnp.tril(jnp.ones((CHUNK, CHUNK), bool))

    def mixer_params(l):
        wtril = jnp.where(tril[None], a_ws[l], 0.0)
        prm = (a_ln_g[l][None], a_ln_b[l][None], wtril.astype(BF16),
               jnp.broadcast_to(a_bs[l][:, :, None], (A_GROUPS, CHUNK, CHUNK)), b_conv_full[l],
               c_wg[l].astype(BF16), c_scale[l][None])
        return prm, jnp.swapaxes(wtril, 1, 2).astype(BF16)

    saved = []
    xc = xs
    for l in range(DEPTH):
        sv = {"x0": xc}
        prm, _ = mixer_params(l)
        fw = layer_weights[l]
        riding = rest_of_first if l == 0 else NEED_LAST
        out = _norm_mm(xc, g_mix[l][None], fw["w_in"], 0, tn=1536, name=f"l{l}_proj", carry=gather_ride(l, riding))
        sv["h"], sv["proj"] = out[0], out[1]
        land(l, riding, out[2], "rest")
        sv["ypre"], sv["cdf"] = _mixer_fwd(sv["proj"], prm, name=f"l{l}_mixers")
        sv["merged"] = _merge_fwd(sv["ypre"], sv["proj"], fw["w_branch_a"][0], fw["w_branch_b"][0],
                                  fw["w_branch_c"][0], b_gate_full[l], name=f"l{l}_merge")
        xc = _mm(sv["merged"], fw["w_o"], b_lead=0, mode="nn", out_dtype=F32, name=f"l{l}_mix_out", tm=1024, tn=1024,
                 tk=1024, res=xc)
        sv["x1"] = xc
        sv["hq"], sv["q"] = _norm_mm(xc, g_xattn[l][None], fw["w_xq"], 0, tn=1024, name=f"l{l}_q")
        sv["m"] = _rms_fwd(mems, g_mem[l][None], name=f"l{l}_mem_norm")
        sv["kv"] = _mm(sv["m"], fw["w_xkv"], b_lead=0, mode="nn", out_dtype=BF16, name=f"l{l}_kv", tm=256, tn=512,
                       tk=1024)
        sv["o"] = _attn_fwd(sv["q"], sv["kv"], name=f"l{l}_attn")
        xc = _mm(sv["o"], fw["w_xo"], b_lead=0, mode="nn", out_dtype=F32, name=f"l{l}_xattn_out", tm=1024, tn=1024,
                 tk=1024, res=xc)
        sv["x2"] = xc
        out2 = _norm_ffn_in(xc, g_ffn[l][None], fw["w_ffn_in"], 0, name=f"l{l}_ffn_in",
                            carry=gather_ride(1, most_of_second) if l == 0 else None)
        sv["h2"], sv["gu"], sv["act"] = out2[0], out2[1], out2[2]
        if l == 0:
            land(1, most_of_second, out2[3], "most")
        xc = _mm(sv["act"], fw["w_ffn_out"], b_lead=0, mode="nn", out_dtype=F32, name=f"l{l}_ffn_out", tm=1024,
                 tn=512, tk=FFN_HIDDEN, res=xc)
        saved.append(sv)

    loss_part, dx, dg_final = _final_loss(xc, g_final[None], tgt, name="final_loss")
    loss = lax.psum(loss_part[0, 0], ("x", "y", "c"))

    pairs = [{}, {}]
    received = [{}, {}]
    small_grads = [None] * DEPTH

    def core_pairs(l, names, grads, tag):
        halves = [jax.ShapeDtypeStruct((g.shape[0], g.shape[1] // 2, g.shape[2]), g.dtype) for g in grads]
        came = _push_staged(grads, halves, False, _plan_grad_halves, name=f"l{l}_reduce_cores_{tag}")
        for n, g, rx in zip(names, grads, came):
            pairs[l][n] = _pair_sum(g, rx, core_arr, name=f"l{l}_pair_sum_{n}")

    def scatter_ride(l, names, when):
        return _scatter_carry([pairs[l][n] for n in names]) if when else None

    ride_second = tuple(n for n in BIG if n not in RIDE_FIRST)
    early, early_big, early_small = NEED_LAST + ("w_xo", "w_xq", "w_xkv"), NEED_LAST + ("w_xkv",), ("w_xo", "w_xq")
    for l in reversed(range(DEPTH)):
        sv = saved[l]
        prm, wtril_t = mixer_params(l)
        gb = {}
        sg = {}
        fw = layer_weights[l]
        dgu = _d_act_swiglu(dx, fw["w_ffn_out"], 0, sv["gu"], name=f"l{l}_d_act")
        gb["w_ffn_out"] = _mm(sv["act"], dx, mode="tn", out_dtype=BF16, name=f"l{l}_dw_ffn_out", tm=1408, tn=1024,
                              tk=TOKEN_STEP).reshape(N_CHIPS, FFN_HIDDEN // N_CHIPS, D_MODEL)
        out = _mm(sv["h2"], dgu, mode="tn", out_dtype=BF16, name=f"l{l}_dw_ffn_in", tm=1024, tn=FFN_TILE,
                  tk=TOKEN_STEP, b_halves=True, out_shards=N_CHIPS, carry=scatter_ride(1, RIDE_FIRST, l == 0))
        gb["w_ffn_in"] = out[0] if l == 0 else out
        out2 = _mm_nt_norm_bwd(dgu, fw["w_ffn_in"], 0, sv["x2"], g_ffn[l][None], dx, tm=512, a_halves=True,
                               name=f"l{l}_d_h2", carry=scatter_ride(1, ride_second, l == 0))
        dx, sg["g_ffn"] = out2[0], out2[1]
        if l == 0:
            received[1].update(zip(RIDE_FIRST, out[1]))
            received[1].update(zip(ride_second, out2[2]))
        do = _mm(dx, fw["w_xo"], b_lead=0, mode="nt", out_dtype=BF16, name=f"l{l}_d_o", tm=1024, tn=1024, tk=1024)
        gb["w_xo"] = _mm(sv["o"], dx, mode="tn", out_dtype=BF16, name=f"l{l}_dw_xo", tm=1024, tn=1024,
                         tk=TOKEN_STEP).reshape(N_CHIPS, D_MODEL // N_CHIPS, D_MODEL)
        dq, dkv = _attn_bwd(sv["q"], sv["kv"], do, name=f"l{l}_d_attn")
        gb["w_xq"] = _mm(sv["hq"], dq, mode="tn", out_dtype=BF16, name=f"l{l}_dw_xq", tm=1024, tn=1024,
                         tk=TOKEN_STEP).reshape(N_CHIPS, D_MODEL // N_CHIPS, D_MODEL)
        dm = _mm(dkv, fw["w_xkv"], b_lead=0, mode="nt", out_dtype=BF16, name=f"l{l}_d_m", tm=256, tn=1024, tk=1024)
        gb["w_xkv"] = _mm(sv["m"], dkv, mode="tn", out_dtype=BF16, name=f"l{l}_dw_xkv", tm=1024, tn=512, tk=256,
                          out_shards=N_CHIPS)
        _, sg["g_mem"] = _rms_bwd(mems, g_mem[l][None], dm, None, name=f"l{l}_d_mem_norm")
        if l == 0:
            core_pairs(0, early, [gb[n] for n in early], "early")
        dx, sg["g_xattn"] = _mm_nt_norm_bwd(dq, fw["w_xq"], 0, sv["x1"], g_xattn[l][None], dx, tm=1024,
                                            name=f"l{l}_d_hq")
        dmerged = _mm(dx, fw["w_o"], b_lead=0, mode="nt", out_dtype=BF16, name=f"l{l}_d_merged", tm=1024, tn=1024,
                      tk=1024)
        gb["w_o"] = _mm(sv["merged"], dx, mode="tn", out_dtype=BF16, name=f"l{l}_dw_o", tm=1024, tn=1024,
                        tk=TOKEN_STEP).reshape(N_CHIPS, D_MODEL // N_CHIPS, D_MODEL)
        out3 = _merge_bwd(sv["ypre"], sv["proj"], dmerged, fw["w_branch_a"][0], fw["w_branch_b"][0],
                          fw["w_branch_c"][0], b_gate_full[l], name=f"l{l}_d_merge",
                          carry=scatter_ride(0, early_big, l == 0))
        dypre, dproj, sg["b_gate"] = out3[0], out3[1], out3[2]
        for n, dw in zip(("w_branch_a", "w_branch_b", "w_branch_c"), out3[3:6]):
            gb[n] = jnp.transpose(dw.reshape(A_WIDTH, N_CHIPS, D_MODEL // N_CHIPS), (1, 0, 2)).astype(BF16)
        (dproj, sg["a_ln_g"], sg["a_ln_b"], sg["a_ws"], sg["a_bs"], sg["b_conv"], sg["c_wg"],
         sg["c_scale"]) = _mixer_bwd(sv["proj"], dypre, dproj, prm, wtril_t, sv["cdf"], name=f"l{l}_d_mixers")
        out4 = _mm(sv["h"], dproj, mode="tn", out_dtype=BF16, name=f"l{l}_dw_in", tm=1024, tn=1536, tk=TOKEN_STEP,
                   out_shards=N_CHIPS, carry=scatter_ride(0, early_small, l == 0))
        gb["w_in"] = out4[0] if l == 0 else out4
        dx, sg["g_mix"] = _mm_nt_norm_bwd(dproj, fw["w_in"], 0, sv["x0"], g_mix[l][None], dx, tm=512,
                                          name=f"l{l}_d_h")
        if l == 0:
            received[0].update(zip(early_big, out3[6]))
            received[0].update(zip(early_small, out4[1]))
            core_pairs(0, NEED_FIRST, [gb[n] for n in NEED_FIRST], "late")
        else:
            core_pairs(1, BIG, [gb[n] for n in BIG], "all")
        small_grads[l] = sg
    grad_x = dx[None]

    small_names = [n for n in WEIGHTS if n not in BIG]
    small_full_shapes = {n: ((DEPTH, 3, D_MODEL) if n == "b_gate" else (DEPTH, 3, B_WIDTH) if n == "b_conv"
                             else wts[n].shape) for n in small_names}
    small_local = []
    for n in small_names:
        if n == "g_final":
            small_local.append(dg_final)
        else:
            small_local.append(jnp.stack([small_grads[l][n].reshape(small_full_shapes[n][1:]) for l in range(DEPTH)]))
    small_pack = _pack(small_local)

    small_rx = _push_sibling(small_pack[None], name="reduce_cores_small")
    small_pair = _elementwise(lambda a, b: (a + b,), [small_pack[None], small_rx], [F32], name="pair_sum_small")[0][0]
    scattered = _chip_scatter([pairs[0][n] for n in NEED_FIRST], small_pair, name="l0_reduce_chips")
    received[0].update(zip(NEED_FIRST, scattered[:-1]))
    small_sum = _chip_sum_ordered(small_pair, scattered[-1], chip_arr, name="chip_sum_small")

    out_g, out_d, out_m, out_v = {}, {}, {}, {}
    mine = [_chip_sum(pairs[0][n], received[0][n], pairs[1][n], received[1][n], chip_arr, name=f"chip_sum_{n}")
            for n in BIG]
    other = _push_staged(mine, [jax.ShapeDtypeStruct(g.shape, g.dtype) for g in mine], False, _plan_whole,
                         name="share_cores")
    for i, n in enumerate(BIG):
        out_g[n], out_d[n], out_m[n], out_v[n] = _adamw_shard(wts[n], mine[i], other[i], mom[n], var[n], core_arr,
                                                              name=f"adamw_{n}")
    small_g = dict(zip(small_names, _unpack(small_sum, [small_full_shapes[n] for n in small_names])))
    small_g["b_gate"] = lax.dynamic_slice_in_dim(small_g["b_gate"], chip * 256, 256, axis=2)
    small_g["b_conv"] = lax.dynamic_slice_in_dim(small_g["b_conv"], chip * 128, 128, axis=2)
    def two_d(a):
        return a.reshape(1, -1) if a.ndim == 1 else a

    upd = _adamw_small(*[[two_d(d[n]) for n in small_names] for d in (wts, small_g, mom, var)], name="adamw_small")
    for n, d, m_new, v_new in zip(small_names, *upd):
        shape = wts[n].shape
        out_g[n], out_d[n], out_m[n], out_v[n] = small_g[n], d.reshape(shape), m_new.reshape(shape), v_new.reshape(shape)

    return (loss, grad_x, *[out_g[n] for n in WEIGHTS], *[out_d[n] for n in WEIGHTS], *[out_m[n] for n in WEIGHTS],
            *[out_v[n] for n in WEIGHTS])
```

```python
import functools
import math
from typing import Callable, NamedTuple

import jax
import jax.numpy as jnp
from jax import lax
from jax.experimental import pallas as pl
from jax.experimental.pallas import tpu as pltpu

F32 = jnp.float32
BF16 = jnp.bfloat16

D_MODEL = 1024
DEPTH = 2
MEM_LEN = 256
EPS = 1e-6
CHUNK = 128
A_GROUPS = 4
A_WIDTH = 512
B_WIDTH = 512
C_WIDTH = 512
C_GROUP = 128
POOL_WINDOWS = (2, 4, 8, 16)
HALO = 16
IN_COLS = 6144
MIX_COLS = 3072
XATTN_HEADS = 4
HEAD_DIM = 256
FFN_HIDDEN = 2816
N_CHIPS = 4

ADAM_LR = 0.001
ADAM_B1 = 0.9
ADAM_B2 = 0.999
ADAM_EPS = 1e-08
ADAM_WD = 0.01
ADAM_STEP = 10

V7X_VMEM_BYTES = 64 * 1024 * 1024
VMEM_LIMIT = (V7X_VMEM_BYTES * 3) // 4
LANES = 128
MESH = pl.DeviceIdType.MESH
ANY = pl.BlockSpec(memory_space=pl.ANY)

BIG = ("w_in", "w_branch_a", "w_branch_b", "w_branch_c", "w_o", "w_xq", "w_xkv", "w_xo", "w_ffn_in", "w_ffn_out")
COL_SHARDED = ("w_in", "w_branch_a", "w_branch_b", "w_branch_c", "w_xkv", "w_ffn_in")
RIDE_FIRST = ("w_in", "w_ffn_out", "w_branch_a", "w_branch_b", "w_branch_c")
NEED_FIRST = ("w_in", "w_branch_a", "w_branch_b", "w_branch_c", "w_o")
NEED_LAST = ("w_ffn_in", "w_ffn_out")
SMALL_REPL = ("g_mix", "a_ln_g", "a_ln_b", "a_ws", "a_bs", "c_wg", "c_scale", "g_xattn", "g_mem", "g_ffn", "g_final")
SMALL_SHARDED = ("b_gate", "b_conv")
WEIGHTS = ("g_mix", "w_in", "b_gate", "a_ln_g", "a_ln_b", "a_ws", "a_bs", "b_conv", "c_wg", "c_scale", "w_branch_a",
           "w_branch_b", "w_branch_c", "w_o", "g_xattn", "g_mem", "w_xq", "w_xkv", "w_xo", "g_ffn", "w_ffn_in",
           "w_ffn_out", "g_final")


def _params(sem=None):
    return pltpu.CompilerParams(dimension_semantics=sem, vmem_limit_bytes=VMEM_LIMIT)


def _blk(dim, pref):
    return pref if dim % pref == 0 else dim


class _Carry(NamedTuple):
    arrays: tuple
    outs: tuple
    in_place: bool
    copies: int
    issue: Callable


def _call_with_carry(body, *, out_shape, grid, in_specs, out_specs, scratch_shapes, name, semantics, operands, carry,
                     aliases=None):
    own_aliases = dict(aliases or {})
    if carry is None:
        return pl.pallas_call(body, out_shape=tuple(out_shape), grid=grid, in_specs=list(in_specs),
                              out_specs=tuple(out_specs), scratch_shapes=list(scratch_shapes), name=name,
                              input_output_aliases=own_aliases, compiler_params=_params(semantics))(*operands)
    n_in, n_out, na, no = len(operands), len(out_shape), len(carry.arrays), len(carry.outs)

    def carried(*refs):
        ins, cins = refs[:n_in], refs[n_in:n_in + na]
        outs, couts = refs[n_in + na:n_in + na + n_out], refs[n_in + na + n_out:n_in + na + n_out + no]
        rest = refs[n_in + na + n_out + no:]
        scratch, ssem, rsem = rest[:-2], rest[-2], rest[-1]
        first = functools.reduce(jnp.logical_and, [pl.program_id(ax) == 0 for ax in range(len(grid))])
        last = functools.reduce(jnp.logical_and, [pl.program_id(ax) == grid[ax] - 1 for ax in range(len(grid))])

        @pl.when(first)
        def _():
            for send, _ in carry.issue(cins, couts, ssem, rsem):
                send.start()

        body(*ins, *outs, *scratch)

        @pl.when(last)
        def _():
            for send, recv in carry.issue(cins, couts, ssem, rsem):
                send.wait_send()
                recv.wait_recv()

    aliases = dict(own_aliases)
    if carry.in_place:
        aliases.update({n_in + i: n_out + i for i in range(na)})
    sems = [pltpu.SemaphoreType.DMA((carry.copies,)), pltpu.SemaphoreType.DMA((carry.copies,))]
    return pl.pallas_call(carried, out_shape=tuple(out_shape) + tuple(carry.outs), grid=grid,
                          in_specs=list(in_specs) + [ANY] * na, out_specs=tuple(out_specs) + tuple([ANY] * no),
                          scratch_shapes=list(scratch_shapes) + sems, input_output_aliases=aliases, name=name,
                          compiler_params=_params(("arbitrary",) * len(grid)))(*operands, *carry.arrays)


def _mm(a, b, *, mode, out_dtype, name, tm, tn, tk, res=None, b_lead=None, b_halves=False, out_shards=None,
        carry=None):
    dims = {"nn": (((1,), (0,)), ((), ())), "nt": (((1,), (1,)), ((), ())), "tn": (((0,), (0,)), ((), ()))}[mode]
    b_rows, b_last = b.shape[-2], b.shape[-1]
    if mode == "nn":
        m, k, n = a.shape[0], a.shape[1], b_last
        assert b_rows == k
    elif mode == "nt":
        m, k, n = a.shape[0], a.shape[1], b_rows
        assert b_last == k
    else:
        k, m, n = a.shape[0], a.shape[1], (2 * b_last if b_halves else b_last)
        assert b_rows == k
    tm, tn, tk = _blk(m, tm), _blk(n, tn), _blk(k, tk)
    nk = k // tk
    grid = (m // tm, n // tn, nk)

    if mode == "tn":
        a_spec = pl.BlockSpec((tk, tm), lambda i, j, kk: (kk, i))
        b_block, b_idx = (tk, tn), (lambda i, j, kk: (kk, j))
    else:
        a_spec = pl.BlockSpec((tm, tk), lambda i, j, kk: (i, kk))
        if mode == "nn":
            b_block, b_idx = (tk, tn), (lambda i, j, kk: (kk, j))
        else:
            b_block, b_idx = (tn, tk), (lambda i, j, kk: (j, kk))
    if b_halves:
        assert mode == "tn" and b_lead is None and b_last % tn == 0
        per_half = b_last // tn
        b_spec = pl.BlockSpec((None,) + b_block, lambda i, j, kk: (j // per_half, kk, j % per_half))
    elif b_lead is None:
        b_spec = pl.BlockSpec(b_block, b_idx)
    else:
        b_spec = pl.BlockSpec((None,) + b_block, lambda i, j, kk: (b_lead,) + b_idx(i, j, kk))
    in_specs = [a_spec, b_spec]
    operands = [a, b]
    if res is not None:
        in_specs.append(pl.BlockSpec((tm, tn), lambda i, j, kk: (i, j)))
        operands.append(res)
    if out_shards is None:
        out_shape = jax.ShapeDtypeStruct((m, n), out_dtype)
        out_spec = pl.BlockSpec((tm, tn), lambda i, j, kk: (i, j))
    else:
        per = n // out_shards
        assert per % tn == 0
        nps = per // tn
        out_shape = jax.ShapeDtypeStruct((out_shards, m, per), out_dtype)
        out_spec = pl.BlockSpec((None, tm, tn), lambda i, j, kk: (j // nps, i, j % nps))

    def body(*refs):
        a_ref, b_ref = refs[0], refs[1]
        res_ref = refs[2] if res is not None else None
        o_ref = refs[3] if res is not None else refs[2]
        part = lax.dot_general(a_ref[...].astype(BF16), b_ref[...].astype(BF16), dims, preferred_element_type=F32)

        def finish(acc):
            if res_ref is not None:
                acc = acc + res_ref[...]
            o_ref[...] = acc.astype(out_dtype)

        if nk == 1:
            finish(part)
        else:
            acc_ref = refs[-1]
            kk = pl.program_id(2)

            @pl.when(kk == 0)
            def _():
                acc_ref[...] = part

            @pl.when(kk > 0)
            def _():
                acc_ref[...] += part

            @pl.when(kk == nk - 1)
            def _():
                finish(acc_ref[...])

    scratch = [] if nk == 1 else [pltpu.VMEM((tm, tn), F32)]
    outs = _call_with_carry(body, out_shape=(out_shape,), grid=grid, in_specs=in_specs, out_specs=(out_spec,),
                            scratch_shapes=scratch, name=name, semantics=("parallel", "parallel", "arbitrary"),
                            operands=operands, carry=carry)
    return outs[0] if carry is None else (outs[0], outs[1:])


ROW_CHUNK = 256
FFN_TILE = 1408
TOKEN_STEP = 2048
NT_DIMS = (((1,), (1,)), ((), ()))


def _norm_rows_into(x_ref, g_ref, h_ref, hs_ref, tm):
    rc = min(ROW_CHUNK, tm)
    for r0 in range(0, tm, rc):
        xv = x_ref[r0:r0 + rc, :]
        r = lax.rsqrt(jnp.mean(xv * xv, axis=-1, keepdims=True) + EPS)
        hv = (xv * r * g_ref[...]).astype(BF16)
        hs_ref[r0:r0 + rc, :] = hv
        h_ref[r0:r0 + rc, :] = hv


def _norm_mm(x, g, b, b_lead, *, tn, name, carry=None):
    s, d = x.shape
    n = b.shape[-1]
    tm, tn = _blk(s, 1024), _blk(n, tn)

    def body(x_ref, g_ref, b_ref, h_ref, o_ref, hs_ref):
        @pl.when(pl.program_id(1) == 0)
        def _():
            _norm_rows_into(x_ref, g_ref, h_ref, hs_ref, tm)

        o_ref[...] = jnp.dot(hs_ref[...], b_ref[...], preferred_element_type=F32).astype(BF16)

    row = pl.BlockSpec((tm, d), lambda i, j: (i, 0))
    outs = _call_with_carry(body, out_shape=(jax.ShapeDtypeStruct((s, d), BF16), jax.ShapeDtypeStruct((s, n), BF16)),
                            grid=(s // tm, n // tn),
                            in_specs=[row, pl.BlockSpec((1, d), lambda i, j: (0, 0)),
                                      pl.BlockSpec((None, d, tn), lambda i, j: (b_lead, 0, j))],
                            out_specs=(row, pl.BlockSpec((tm, tn), lambda i, j: (i, j))),
                            scratch_shapes=[pltpu.VMEM((tm, d), BF16)], name=name,
                            semantics=("parallel", "arbitrary"), operands=[x, g, b], carry=carry)
    return outs if carry is None else (outs[0], outs[1], outs[2:])


def _norm_ffn_in(x, g, b, b_lead, name, carry=None):
    s, d = x.shape
    tm, tn = _blk(s, 512), FFN_TILE
    nj = FFN_HIDDEN // tn

    def body(x_ref, g_ref, bg_ref, bu_ref, h_ref, gu_ref, act_ref, hs_ref):
        @pl.when(pl.program_id(1) == 0)
        def _():
            _norm_rows_into(x_ref, g_ref, h_ref, hs_ref, tm)

        hv = hs_ref[...]
        gate = jnp.dot(hv, bg_ref[...], preferred_element_type=F32)
        up = jnp.dot(hv, bu_ref[...], preferred_element_type=F32)
        gu_ref[0] = gate.astype(BF16)
        gu_ref[1] = up.astype(BF16)
        act_ref[...] = (gate * _sigmoid(gate) * up).astype(BF16)

    row = pl.BlockSpec((tm, d), lambda i, j: (i, 0))
    out_shape = (jax.ShapeDtypeStruct((s, d), BF16), jax.ShapeDtypeStruct((2, s, FFN_HIDDEN), BF16),
                 jax.ShapeDtypeStruct((s, FFN_HIDDEN), BF16))
    outs = _call_with_carry(body, out_shape=out_shape, grid=(s // tm, nj),
                            in_specs=[row, pl.BlockSpec((1, d), lambda i, j: (0, 0)),
                                      pl.BlockSpec((None, d, tn), lambda i, j: (b_lead, 0, j)),
                                      pl.BlockSpec((None, d, tn), lambda i, j: (b_lead, 0, j + nj))],
                            out_specs=(row, pl.BlockSpec((2, tm, tn), lambda i, j: (0, i, j)),
                                       pl.BlockSpec((tm, tn), lambda i, j: (i, j))),
                            scratch_shapes=[pltpu.VMEM((tm, d), BF16)], name=name,
                            semantics=("parallel", "arbitrary"), operands=[x, g, b, b], carry=carry)
    return outs if carry is None else (outs[0], outs[1], outs[2], outs[3:])


def _d_act_swiglu(dx, w, b_lead, gu, name):
    s, d = dx.shape
    tm, tn = _blk(s, 512), FFN_TILE

    def body(dx_ref, w_ref, gu_ref, o_ref):
        dact = lax.dot_general(dx_ref[...].astype(BF16), w_ref[...], NT_DIMS, preferred_element_type=F32)
        gate = gu_ref[0].astype(F32)
        up = gu_ref[1].astype(F32)
        sg = _sigmoid(gate)
        o_ref[0] = (dact * up * sg * (1.0 + gate * (1.0 - sg))).astype(BF16)
        o_ref[1] = (dact * gate * sg).astype(BF16)

    halves = pl.BlockSpec((2, tm, tn), lambda j, i: (0, i, j))
    return pl.pallas_call(body, out_shape=jax.ShapeDtypeStruct(gu.shape, BF16), grid=(FFN_HIDDEN // tn, s // tm),
                          in_specs=[pl.BlockSpec((tm, d), lambda j, i: (i, 0)),
                                    pl.BlockSpec((None, tn, d), lambda j, i: (b_lead, j, 0)), halves],
                          out_specs=halves, name=name,
                          compiler_params=_params(("parallel", "parallel")))(dx, w, gu)


def _mm_nt_norm_bwd(a, b, b_lead, x, g, dres, *, tm, name, a_halves=False, carry=None):
    s, d = x.shape
    kdim = b.shape[-1]
    tm = _blk(s, tm)
    rc = min(ROW_CHUNK, tm)
    if a_halves:
        a_spec = pl.BlockSpec((2, tm, kdim // 2), lambda i: (0, i, 0))
    else:
        a_spec = pl.BlockSpec((tm, kdim), lambda i: (i, 0))

    def body(a_ref, b_ref, x_ref, g_ref, r_ref, dx_ref, dg_ref):
        i = pl.program_id(0)
        av = jnp.concatenate([a_ref[0], a_ref[1]], axis=1) if a_halves else a_ref[...]
        dh = lax.dot_general(av.astype(BF16), b_ref[...], NT_DIMS, preferred_element_type=F32)
        gv = g_ref[...]
        dg_part = None
        for r0 in range(0, tm, rc):
            xv = x_ref[r0:r0 + rc, :]
            dhv = dh[r0:r0 + rc, :]
            r = lax.rsqrt(jnp.mean(xv * xv, axis=-1, keepdims=True) + EPS)
            xhat = xv * r
            p = jnp.sum(dhv * xhat, axis=0, keepdims=True)
            dg_part = p if dg_part is None else dg_part + p
            dxhat = dhv * gv
            dx_ref[r0:r0 + rc, :] = r_ref[r0:r0 + rc, :] + r * (
                dxhat - xhat * jnp.mean(dxhat * xhat, axis=-1, keepdims=True))

        @pl.when(i == 0)
        def _():
            dg_ref[...] = dg_part

        @pl.when(i > 0)
        def _():
            dg_ref[...] += dg_part

    row = pl.BlockSpec((tm, d), lambda i: (i, 0))
    vec = pl.BlockSpec((1, d), lambda i: (0, 0))
    b_spec = pl.BlockSpec((None, d, kdim), lambda i: (b_lead, 0, 0), pipeline_mode=pl.Buffered(1))
    outs = _call_with_carry(body, out_shape=(jax.ShapeDtypeStruct((s, d), F32), jax.ShapeDtypeStruct((1, d), F32)),
                            grid=(s // tm,), in_specs=[a_spec, b_spec, row, vec, row], out_specs=(row, vec),
                            scratch_shapes=[], name=name, semantics=("arbitrary",), operands=[a, b, x, g, dres],
                            carry=carry)
    return outs if carry is None else (outs[0], outs[1], outs[2:])


def _rms_fwd(x, g, name):
    s, d = x.shape
    tm = _blk(s, 512)

    def body(x_ref, g_ref, o_ref):
        xv = x_ref[...]
        r = lax.rsqrt(jnp.mean(xv * xv, axis=-1, keepdims=True) + EPS)
        o_ref[...] = (xv * r * g_ref[...]).astype(BF16)

    return pl.pallas_call(body, out_shape=jax.ShapeDtypeStruct((s, d), BF16), grid=(s // tm,),
                          in_specs=[pl.BlockSpec((tm, d), lambda i: (i, 0)), pl.BlockSpec((1, d), lambda i: (0, 0))],
                          out_specs=pl.BlockSpec((tm, d), lambda i: (i, 0)), name=name,
                          compiler_params=_params(("parallel",)))(x, g)


def _rms_bwd(x, g, dh, dres, name):
    s, d = x.shape
    tm = _blk(s, 512)
    has_res = dres is not None

    def body(*refs):
        x_ref, g_ref, dh_ref = refs[0], refs[1], refs[2]
        dx_ref, dg_ref = refs[-2], refs[-1]
        xv = x_ref[...]
        r = lax.rsqrt(jnp.mean(xv * xv, axis=-1, keepdims=True) + EPS)
        xhat = xv * r
        dhv = dh_ref[...].astype(F32)
        part = jnp.sum(dhv * xhat, axis=0, keepdims=True)

        @pl.when(pl.program_id(0) == 0)
        def _():
            dg_ref[...] = part

        @pl.when(pl.program_id(0) > 0)
        def _():
            dg_ref[...] += part

        dxhat = dhv * g_ref[...]
        dx = r * (dxhat - xhat * jnp.mean(dxhat * xhat, axis=-1, keepdims=True))
        if has_res:
            dx = dx + refs[3][...]
        dx_ref[...] = dx

    row = pl.BlockSpec((tm, d), lambda i: (i, 0))
    vec = pl.BlockSpec((1, d), lambda i: (0, 0))
    in_specs = [row, vec, row] + ([row] if has_res else [])
    operands = [x, g, dh] + ([dres] if has_res else [])
    return pl.pallas_call(body, out_shape=(jax.ShapeDtypeStruct((s, d), F32), jax.ShapeDtypeStruct((1, d), F32)),
                          grid=(s // tm,), in_specs=in_specs, out_specs=(row, vec), name=name,
                          compiler_params=_params(("arbitrary",)))(*operands)


def _final_loss(x, g, target, name):
    s, d = x.shape
    tm = _blk(s, 512)

    def body(x_ref, g_ref, t_ref, loss_ref, dx_ref, dg_ref):
        xv = x_ref[...]
        gv = g_ref[...]
        r = lax.rsqrt(jnp.mean(xv * xv, axis=-1, keepdims=True) + EPS)
        xhat = xv * r
        err = xhat * gv - t_ref[...]
        lpart = 0.5 * jnp.sum(jnp.mean(err * err, axis=-1, keepdims=True), axis=0, keepdims=True)
        dy = err * (1.0 / d)
        gpart = jnp.sum(dy * xhat, axis=0, keepdims=True)

        @pl.when(pl.program_id(0) == 0)
        def _():
            loss_ref[...] = lpart
            dg_ref[...] = gpart

        @pl.when(pl.program_id(0) > 0)
        def _():
            loss_ref[...] += lpart
            dg_ref[...] += gpart

        dxhat = dy * gv
        dx_ref[...] = r * (dxhat - xhat * jnp.mean(dxhat * xhat, axis=-1, keepdims=True))

    row = pl.BlockSpec((tm, d), lambda i: (i, 0))
    vec = pl.BlockSpec((1, d), lambda i: (0, 0))
    one = pl.BlockSpec((1, 1), lambda i: (0, 0))
    return pl.pallas_call(body, out_shape=(jax.ShapeDtypeStruct((1, 1), F32), jax.ShapeDtypeStruct((s, d), F32),
                                           jax.ShapeDtypeStruct((1, d), F32)),
                          grid=(s // tm,), in_specs=[row, vec, row], out_specs=(one, row, vec), name=name,
                          compiler_params=_params(("arbitrary",)))(x, g, target)


def _erf_parts(x):
    cdf = 0.5 * (1.0 + lax.erf(x * (1.0 / math.sqrt(2.0))))
    return cdf


def _shift_down(ext, k):
    return pltpu.roll(ext, k, 0)


def _shift_up(ext, k):
    return pltpu.roll(ext, ext.shape[0] - k, 0)


def _mixer_forward_math(z, halo, row0, prm, tm):
    ln_g, ln_b, wtril, bsb, conv_w, wg, scale = prm
    za = z[:, 0:2 * A_WIDTH]
    gb = z[:, 1024:1536]
    gc = z[:, 1536:2048]
    xin = z[:, 2048:2560]
    zc = z[:, 2560:3072]
    cdf = _erf_parts(za)
    act = za * cdf
    u = act[:, :A_WIDTH]
    v = act[:, A_WIDTH:]
    mu = jnp.mean(v, axis=-1, keepdims=True)
    vc = v - mu
    rstd = lax.rsqrt(jnp.mean(vc * vc, axis=-1, keepdims=True) + EPS)
    vhat = vc * rstd
    vn = (vhat * ln_g + ln_b).astype(BF16)
    rows = []
    for c in range(tm // CHUNK):
        cols = []
        for g in range(A_GROUPS):
            blk = vn[c * CHUNK:(c + 1) * CHUNK, g * CHUNK:(g + 1) * CHUNK]
            cols.append(jnp.dot(wtril[g], blk, preferred_element_type=F32) + bsb[g])
        rows.append(jnp.concatenate(cols, axis=1))
    mixed = jnp.concatenate(rows, axis=0)
    ya = u * mixed
    y = gc * xin
    yext = jnp.concatenate([halo[:, 0:512] * halo[:, 512:1024], y], axis=0)
    y1 = _shift_down(yext, 1)[HALO:]
    y2 = _shift_down(yext, 2)[HALO:]
    conv = conv_w[0:1, :] * y2 + conv_w[1:2, :] * y1 + conv_w[2:3, :] * y
    yb = gb * conv
    t = row0 + lax.broadcasted_iota(jnp.int32, (tm, 1), 0)
    zext = jnp.concatenate([halo[:, 1024:1536], zc], axis=0)
    pooled, p = [], []
    for gi, win in enumerate(POOL_WINDOWS):
        sw = zext[:, gi * C_GROUP:(gi + 1) * C_GROUP]
        for step in range(gi + 1):
            sw = sw + _shift_down(sw, 2 ** step)
        cnt = jnp.minimum(t + 1, win).astype(F32)
        pg = sw[HALO:] / cnt - zc[:, gi * C_GROUP:(gi + 1) * C_GROUP]
        pooled.append(pg)
        p.append(jnp.dot(pg.astype(BF16), wg[gi], preferred_element_type=F32))
    p = jnp.concatenate(p, axis=1)
    yc = p * scale
    saved = dict(za=za, cdf=cdf, u=u, mixed=mixed, rstd=rstd, vhat=vhat, vn=vn, gb=gb, gc=gc, xin=xin, y=y, y1=y1, y2=y2,
                 conv=conv, pooled=pooled, p=p, t=t)
    return ya, yb, yc, saved


def _mixer_specs(tm, s):
    nb16 = tm // HALO
    main = pl.BlockSpec((tm, MIX_COLS), lambda i: (i, 0))
    prev = pl.BlockSpec((HALO, 1536), lambda i: (jnp.maximum(i * nb16 - 1, 0), 1))
    return main, prev


def _full(shape):
    n = len(shape)
    return pl.BlockSpec(shape, lambda i: (0,) * n)


def _mixer_fwd(proj, prm, name):
    s = proj.shape[0]
    tm = _blk(s, 256)
    main, prev = _mixer_specs(tm, s)

    def body(z_ref, h_ref, lng, lnb, wt, bsb, cw, wg, sc, o_ref):
        i = pl.program_id(0)
        halo = jnp.where(i > 0, h_ref[...].astype(F32), 0.0)
        prm_v = (lng[...], lnb[...], wt[...], bsb[...], cw[...], wg[...], sc[...])
        ya, yb, yc, _ = _mixer_forward_math(z_ref[...].astype(F32), halo, i * tm, prm_v, tm)
        o_ref[...] = jnp.concatenate([ya, yb, yc], axis=1).astype(BF16)

    return pl.pallas_call(body, out_shape=jax.ShapeDtypeStruct((s, 1536), BF16), grid=(s // tm,),
                          in_specs=[main, prev] + [_full(p.shape) for p in prm],
                          out_specs=pl.BlockSpec((tm, 1536), lambda i: (i, 0)), name=name,
                          compiler_params=_params(("parallel",)))(proj, proj, *prm)


def _mixer_bwd(proj, dypre, dproj_in, prm, wtril_t, name, carry=None):
    s = proj.shape[0]
    tm = _blk(s, 256)
    nblk = s // tm
    nb16 = tm // HALO
    main, prev = _mixer_specs(tm, s)
    nxt_row = lambda i: jnp.minimum((i + 1) * nb16, s // HALO - 1)
    next_dy = pl.BlockSpec((HALO, 1536), lambda i: (nxt_row(i), 0))
    next_gb = pl.BlockSpec((HALO, 512), lambda i: (nxt_row(i), 2))
    ones8 = jnp.ones((8, CHUNK), F32)

    def body(z_ref, h_ref, dy_ref, ndy_ref, ngb_ref, _alias, lng, lnb, wt, bsb, cw, wg, sc, wtt, ones_ref,
             dz_ref, dlng_ref, dlnb_ref, dws_ref, dbs_ref, dcw_ref, dwg_ref, dsc_ref):
        i = pl.program_id(0)
        first = i == 0
        halo = jnp.where(i > 0, h_ref[...].astype(F32), 0.0)
        prm_v = (lng[...], lnb[...], wt[...], bsb[...], cw[...], wg[...], sc[...])
        _, _, _, sv = _mixer_forward_math(z_ref[...].astype(F32), halo, i * tm, prm_v, tm)
        dy = dy_ref[...].astype(F32)
        dya, dyb, dyc = dy[:, 0:512], dy[:, 512:1024], dy[:, 1024:1536]
        not_last = i < nblk - 1
        ndy = jnp.where(not_last, ndy_ref[...].astype(F32), 0.0)
        ngb = ngb_ref[...].astype(F32)

        def accumulate(ref, val):
            @pl.when(first)
            def _():
                ref[...] = val

            @pl.when(jnp.logical_not(first))
            def _():
                ref[...] += val

        du = dya * sv["mixed"]
        dmixed = dya * sv["u"]
        vn = sv["vn"]
        wttv = wtt[...]
        dvn_rows = []
        dws_parts = [None] * A_GROUPS
        dbs_parts = [None] * A_GROUPS
        for c in range(tm // CHUNK):
            cols = []
            for g in range(A_GROUPS):
                dm = dmixed[c * CHUNK:(c + 1) * CHUNK, g * CHUNK:(g + 1) * CHUNK]
                dmb = dm.astype(BF16)
                vb = vn[c * CHUNK:(c + 1) * CHUNK, g * CHUNK:(g + 1) * CHUNK]
                w_part = lax.dot_general(dmb, vb, (((1,), (1,)), ((), ())), preferred_element_type=F32)
                b_part = lax.dot_general(ones_ref[...], dm, (((1,), (1,)), ((), ())), preferred_element_type=F32,
                                         precision=lax.Precision.HIGHEST)[0:1, :]
                dws_parts[g] = w_part if dws_parts[g] is None else dws_parts[g] + w_part
                dbs_parts[g] = b_part if dbs_parts[g] is None else dbs_parts[g] + b_part
                cols.append(jnp.dot(wttv[g], dmb, preferred_element_type=F32))
            dvn_rows.append(jnp.concatenate(cols, axis=1))
        dvn = jnp.concatenate(dvn_rows, axis=0)
        tri = lax.broadcasted_iota(jnp.int32, (CHUNK, CHUNK), 0) >= lax.broadcasted_iota(jnp.int32, (CHUNK, CHUNK), 1)
        accumulate(dws_ref, jnp.stack([jnp.where(tri, w, 0.0) for w in dws_parts], axis=0))
        accumulate(dbs_ref, jnp.concatenate(dbs_parts, axis=0))
        vhat = sv["vhat"]
        accumulate(dlng_ref, jnp.sum(dvn * vhat, axis=0, keepdims=True))
        accumulate(dlnb_ref, jnp.sum(dvn, axis=0, keepdims=True))
        dvhat = dvn * lng[...]
        dv = sv["rstd"] * (dvhat - jnp.mean(dvhat, axis=-1, keepdims=True)
                           - vhat * jnp.mean(dvhat * vhat, axis=-1, keepdims=True))
        za = sv["za"]
        dgelu = sv["cdf"] + za * jnp.exp(-0.5 * za * za) * (1.0 / math.sqrt(2.0 * math.pi))
        dza = jnp.concatenate([du, dv], axis=1) * dgelu
        cwv = cw[...]
        dconv = dyb * sv["gb"]
        dgb = dyb * sv["conv"]
        dcext = jnp.concatenate([dconv, ndy[:, 512:1024] * ngb], axis=0)
        d1 = _shift_up(dcext, 1)[:tm]
        d2 = _shift_up(dcext, 2)[:tm]
        dyy = cwv[2:3, :] * dconv + cwv[1:2, :] * d1 + cwv[0:1, :] * d2
        dgc = dyy * sv["xin"]
        dxin = dyy * sv["gc"]
        accumulate(dcw_ref, jnp.concatenate([jnp.sum(dconv * sv["y2"], axis=0, keepdims=True),
                                             jnp.sum(dconv * sv["y1"], axis=0, keepdims=True),
                                             jnp.sum(dconv * sv["y"], axis=0, keepdims=True)], axis=0))
        scv = sc[...]
        accumulate(dsc_ref, jnp.sum(dyc * sv["p"], axis=0, keepdims=True))
        dp = dyc * scv
        ndp = ndy[:, 1024:1536] * scv
        wgv = wg[...]
        dzc_cols, dwg_parts = [], []
        for gi, win in enumerate(POOL_WINDOWS):
            sl = slice(gi * C_GROUP, (gi + 1) * C_GROUP)
            dpb = dp[:, sl].astype(BF16)
            dpool = lax.dot_general(dpb, wgv[gi], (((1,), (1,)), ((), ())), preferred_element_type=F32)
            ndpool = lax.dot_general(ndp[:, sl].astype(BF16), wgv[gi], (((1,), (1,)), ((), ())),
                                     preferred_element_type=F32)
            dwg_parts.append(lax.dot_general(sv["pooled"][gi].astype(BF16), dpb, (((0,), (0,)), ((), ())),
                                             preferred_element_type=F32))
            cnt = jnp.minimum(sv["t"] + 1, win).astype(F32)
            fw = jnp.concatenate([dpool / cnt, ndpool * (1.0 / win)], axis=0)
            for step in range(gi + 1):
                fw = fw + _shift_up(fw, 2 ** step)
            dzc_cols.append(fw[:tm] - dpool)
        accumulate(dwg_ref, jnp.stack(dwg_parts, axis=0))
        dzc = jnp.concatenate(dzc_cols, axis=1)
        dz_ref[...] = jnp.concatenate([dza, dgb, dgc, dxin, dzc], axis=1).astype(BF16)

    out_shape = (jax.ShapeDtypeStruct(dproj_in.shape, BF16), jax.ShapeDtypeStruct((1, A_WIDTH), F32),
                 jax.ShapeDtypeStruct((1, A_WIDTH), F32), jax.ShapeDtypeStruct((A_GROUPS, CHUNK, CHUNK), F32),
                 jax.ShapeDtypeStruct((A_GROUPS, CHUNK), F32), jax.ShapeDtypeStruct((3, B_WIDTH), F32),
                 jax.ShapeDtypeStruct((4, C_GROUP, C_GROUP), F32), jax.ShapeDtypeStruct((1, C_WIDTH), F32))
    out_specs = (main,) + tuple(_full(o.shape) for o in out_shape[1:])
    in_specs = [main, prev, pl.BlockSpec((tm, 1536), lambda i: (i, 0)), next_dy, next_gb, ANY] + \
               [_full(p.shape) for p in prm] + [_full(wtril_t.shape), _full(ones8.shape)]
    outs = _call_with_carry(body, out_shape=out_shape, grid=(nblk,), in_specs=in_specs, out_specs=out_specs,
                            scratch_shapes=[], name=name, semantics=("arbitrary",),
                            operands=[proj, proj, dypre, dypre, proj, dproj_in, *prm, wtril_t, ones8], carry=carry,
                            aliases={5: 0})
    return tuple(outs) if carry is None else tuple(outs[:8]) + (outs[8:],)


def _sigmoid(x):
    return 1.0 / (1.0 + jnp.exp(-x))


def _merge_fwd(ypre, proj, wa, wb, wc, bgate, name):
    s = ypre.shape[0]
    tm = _blk(s, 512)

    def body(y_ref, zg_ref, wa_ref, wb_ref, wc_ref, bg_ref, o_ref):
        yv = y_ref[...]
        acc = None
        for i, w_ref in enumerate((wa_ref, wb_ref, wc_ref)):
            br = jnp.dot(yv[:, i * 512:(i + 1) * 512], w_ref[...], preferred_element_type=F32)
            gate = _sigmoid(zg_ref[:, i * D_MODEL:(i + 1) * D_MODEL].astype(F32) + bg_ref[i:i + 1, :])
            acc = gate * br if acc is None else acc + gate * br
        o_ref[...] = acc.astype(BF16)

    wspec = _full(wa.shape)
    return pl.pallas_call(body, out_shape=jax.ShapeDtypeStruct((s, D_MODEL), BF16), grid=(s // tm,),
                          in_specs=[pl.BlockSpec((tm, 1536), lambda i: (i, 0)),
                                    pl.BlockSpec((tm, 3 * D_MODEL), lambda i: (i, 1)), wspec, wspec, wspec,
                                    _full(bgate.shape)],
                          out_specs=pl.BlockSpec((tm, D_MODEL), lambda i: (i, 0)), name=name,
                          compiler_params=_params(("parallel",)))(ypre, proj, wa, wb, wc, bgate)


def _merge_bwd(ypre, proj, dmerged, wa, wb, wc, bgate, name, carry=None):
    s = ypre.shape[0]
    tm = _blk(s, 512)

    def body(y_ref, zg_ref, dm_ref, wa_ref, wb_ref, wc_ref, bg_ref, dyp_ref, dzg_ref, dbg_ref, dwa_ref, dwb_ref,
             dwc_ref):
        first = pl.program_id(0) == 0

        def accumulate(ref, val):
            @pl.when(first)
            def _():
                ref[...] = val

            @pl.when(jnp.logical_not(first))
            def _():
                ref[...] += val

        yv = y_ref[...]
        dm = dm_ref[...].astype(F32)
        dyp, dzg, dbg = [], [], []
        for i, (w_ref, dw_ref) in enumerate(((wa_ref, dwa_ref), (wb_ref, dwb_ref), (wc_ref, dwc_ref))):
            wv = w_ref[...]
            yi = yv[:, i * 512:(i + 1) * 512]
            br = jnp.dot(yi, wv, preferred_element_type=F32)
            gate = _sigmoid(zg_ref[:, i * D_MODEL:(i + 1) * D_MODEL].astype(F32) + bg_ref[i:i + 1, :])
            dbi = (dm * gate).astype(BF16)
            dzi = dm * br * gate * (1.0 - gate)
            dzg.append(dzi.astype(BF16))
            dbg.append(jnp.sum(dzi, axis=0, keepdims=True))
            dyp.append(lax.dot_general(dbi, wv, (((1,), (1,)), ((), ())), preferred_element_type=F32).astype(BF16))
            accumulate(dw_ref, lax.dot_general(yi, dbi, (((0,), (0,)), ((), ())), preferred_element_type=F32))
        dyp_ref[...] = jnp.concatenate(dyp, axis=1)
        dzg_ref[...] = jnp.concatenate(dzg, axis=1)
        accumulate(dbg_ref, jnp.concatenate(dbg, axis=0))

    wspec = _full(wa.shape)
    dw_shape = jax.ShapeDtypeStruct(wa.shape, F32)
    out_shape = (jax.ShapeDtypeStruct((s, 1536), BF16), jax.ShapeDtypeStruct((s, IN_COLS), BF16),
                 jax.ShapeDtypeStruct((3, D_MODEL), F32), dw_shape, dw_shape, dw_shape)
    outs = _call_with_carry(body, out_shape=out_shape, grid=(s // tm,),
                            in_specs=[pl.BlockSpec((tm, 1536), lambda i: (i, 0)),
                                      pl.BlockSpec((tm, 3 * D_MODEL), lambda i: (i, 1)),
                                      pl.BlockSpec((tm, D_MODEL), lambda i: (i, 0)), wspec, wspec, wspec,
                                      _full(bgate.shape)],
                            out_specs=(pl.BlockSpec((tm, 1536), lambda i: (i, 0)),
                                       pl.BlockSpec((tm, 3 * D_MODEL), lambda i: (i, 1)), _full((3, D_MODEL)),
                                       wspec, wspec, wspec),
                            scratch_shapes=[], name=name, semantics=("arbitrary",),
                            operands=[ypre, proj, dmerged, wa, wb, wc, bgate], carry=carry)
    return tuple(outs) if carry is None else tuple(outs[:6]) + (outs[6:],)


def _softmax_rows(q, k):
    sc = lax.dot_general(q, k, (((1,), (1,)), ((), ())), preferred_element_type=F32) * (HEAD_DIM ** -0.5)
    e = jnp.exp(sc - jnp.max(sc, axis=-1, keepdims=True))
    return e / jnp.sum(e, axis=-1, keepdims=True)


def _attn_fwd(q, kv, name):
    s = q.shape[0]
    tm = _blk(s, 512)

    def body(q_ref, kv_ref, o_ref):
        outs = []
        for h in range(XATTN_HEADS):
            sl = slice(h * HEAD_DIM, (h + 1) * HEAD_DIM)
            p = _softmax_rows(q_ref[:, sl], kv_ref[:, sl])
            outs.append(jnp.dot(p.astype(BF16), kv_ref[:, D_MODEL + h * HEAD_DIM:D_MODEL + (h + 1) * HEAD_DIM],
                                preferred_element_type=F32))
        o_ref[...] = jnp.concatenate(outs, axis=1).astype(BF16)

    return pl.pallas_call(body, out_shape=jax.ShapeDtypeStruct((s, D_MODEL), BF16), grid=(s // tm,),
                          in_specs=[pl.BlockSpec((tm, D_MODEL), lambda i: (i, 0)), _full(kv.shape)],
                          out_specs=pl.BlockSpec((tm, D_MODEL), lambda i: (i, 0)), name=name,
                          compiler_params=_params(("parallel",)))(q, kv)


def _attn_bwd(q, kv, do, name):
    s = q.shape[0]
    tm = _blk(s, 512)

    def body(q_ref, kv_ref, do_ref, dq_ref, dkv_ref):
        dqs, dks, dvs = [], [], []
        for h in range(XATTN_HEADS):
            sl = slice(h * HEAD_DIM, (h + 1) * HEAD_DIM)
            vsl = slice(D_MODEL + h * HEAD_DIM, D_MODEL + (h + 1) * HEAD_DIM)
            qh, kh, vh, doh = q_ref[:, sl], kv_ref[:, sl], kv_ref[:, vsl], do_ref[:, sl]
            p = _softmax_rows(qh, kh)
            pb = p.astype(BF16)
            dvs.append(lax.dot_general(pb, doh, (((0,), (0,)), ((), ())), preferred_element_type=F32))
            dp = lax.dot_general(doh, vh, (((1,), (1,)), ((), ())), preferred_element_type=F32)
            ds = p * (dp - jnp.sum(dp * p, axis=-1, keepdims=True)) * (HEAD_DIM ** -0.5)
            dsb = ds.astype(BF16)
            dqs.append(jnp.dot(dsb, kh, preferred_element_type=F32))
            dks.append(lax.dot_general(dsb, qh, (((0,), (0,)), ((), ())), preferred_element_type=F32))
        dq_ref[...] = jnp.concatenate(dqs, axis=1).astype(BF16)
        part = jnp.concatenate(dks + dvs, axis=1)

        @pl.when(pl.program_id(0) == 0)
        def _():
            dkv_ref[...] = part

        @pl.when(pl.program_id(0) > 0)
        def _():
            dkv_ref[...] += part

    row = pl.BlockSpec((tm, D_MODEL), lambda i: (i, 0))
    return pl.pallas_call(body, out_shape=(jax.ShapeDtypeStruct((s, D_MODEL), BF16),
                                           jax.ShapeDtypeStruct(kv.shape, F32)),
                          grid=(s // tm,), in_specs=[row, _full(kv.shape), row], out_specs=(row, _full(kv.shape)),
                          name=name, compiler_params=_params(("arbitrary",)))(q, kv, do)


def _rows_block(rows, cols):
    target = (512 * 1024) // cols
    fits = [cand for cand in range(8, rows + 1, 8) if rows % cand == 0 and cand <= target]
    return fits[-1] if fits else rows


def _elementwise(fn, ins, out_dtypes, name):
    lead, rows, cols = ins[0].shape
    tr = _rows_block(rows, cols)
    spec = pl.BlockSpec((None, tr, cols), lambda l, i: (l, i, 0))
    n_in = len(ins)

    def body(*refs):
        for o_ref, o in zip(refs[n_in:], fn(*[r[...] for r in refs[:n_in]])):
            o_ref[...] = o.astype(o_ref.dtype)

    out_shape = tuple(jax.ShapeDtypeStruct((lead, rows, cols), dt) for dt in out_dtypes)
    return pl.pallas_call(body, out_shape=out_shape, grid=(lead, rows // tr), in_specs=[spec] * n_in,
                          out_specs=tuple([spec] * len(out_dtypes)), name=name,
                          compiler_params=_params(("parallel", "parallel")))(*ins)


def _cast_to_slots(w, chip, name):
    _, rows, cols = w.shape
    tr = _rows_block(rows, cols)

    def body(chip_ref, w_ref, o0_ref, o1_ref):
        o0_ref[...] = w_ref[0].astype(BF16)
        o1_ref[...] = w_ref[1].astype(BF16)

    slot = pl.BlockSpec((None, tr, cols), lambda i, chip_ref: (chip_ref[0], i, 0))
    grid_spec = pltpu.PrefetchScalarGridSpec(
        num_scalar_prefetch=1, grid=(rows // tr,),
        in_specs=[pl.BlockSpec((DEPTH, tr, cols), lambda i, chip_ref: (0, i, 0))], out_specs=(slot, slot))
    shape = jax.ShapeDtypeStruct((N_CHIPS, rows, cols), BF16)
    return pl.pallas_call(body, out_shape=(shape, shape), grid_spec=grid_spec, name=name,
                          compiler_params=_params(("parallel",)))(chip, w)


def _chip_sum(pair0, rx0, pair1, rx1, chip, name):
    _, rows, cols = pair0.shape
    tr = _rows_block(rows, cols)

    def body(chip_ref, p0_ref, rx0_ref, p1_ref, rx1_ref, o_ref):
        for layer, (p_ref, rx_ref) in enumerate(((p0_ref, rx0_ref), (p1_ref, rx1_ref))):
            acc = p_ref[...].astype(F32)
            for j in range(3):
                acc = acc + rx_ref[j].astype(F32)
            o_ref[layer] = acc

    mine = pl.BlockSpec((None, tr, cols), lambda i, chip_ref: (chip_ref[0], i, 0))
    theirs = pl.BlockSpec((3, tr, cols), lambda i, chip_ref: (0, i, 0))
    grid_spec = pltpu.PrefetchScalarGridSpec(
        num_scalar_prefetch=1, grid=(rows // tr,), in_specs=[mine, theirs, mine, theirs],
        out_specs=pl.BlockSpec((DEPTH, tr, cols), lambda i, chip_ref: (0, i, 0)))
    return pl.pallas_call(body, out_shape=jax.ShapeDtypeStruct((DEPTH, rows, cols), F32), grid_spec=grid_spec,
                          name=name, compiler_params=_params(("parallel",)))(chip, pair0, rx0, pair1, rx1)


def _chip_sum_ordered(own, rx, chip, name):
    rows, cols = own.shape

    def body(chip_ref, own_ref, rx_ref, o_ref):
        me = chip_ref[0]
        acc = None
        for k in range(N_CHIPS):
            rel = jnp.bitwise_xor(me, k)
            term = jnp.where(rel == 0, own_ref[...],
                             jnp.where(rel == 2, rx_ref[0], jnp.where(rel == 1, rx_ref[1], rx_ref[2])))
            acc = term if acc is None else acc + term
        o_ref[...] = acc

    grid_spec = pltpu.PrefetchScalarGridSpec(
        num_scalar_prefetch=1, grid=(1,),
        in_specs=[pl.BlockSpec((rows, cols), lambda i, chip_ref: (0, 0)),
                  pl.BlockSpec((3, rows, cols), lambda i, chip_ref: (0, 0, 0))],
        out_specs=pl.BlockSpec((rows, cols), lambda i, chip_ref: (0, 0)))
    return pl.pallas_call(body, out_shape=jax.ShapeDtypeStruct((rows, cols), F32), grid_spec=grid_spec, name=name,
                          compiler_params=_params(("arbitrary",)))(chip, own, rx)


def _adamw_math(w, g, m, v):
    m = ADAM_B1 * m + (1.0 - ADAM_B1) * g
    v = ADAM_B2 * v + (1.0 - ADAM_B2) * (g * g)
    m_hat = m / (1.0 - ADAM_B1 ** ADAM_STEP)
    v_hat = v / (1.0 - ADAM_B2 ** ADAM_STEP)
    delta = -ADAM_LR * (m_hat / (jnp.sqrt(v_hat) + ADAM_EPS) + ADAM_WD * w)
    return delta, m, v


def _adamw_small(ws, gs, ms, vs, name):
    n = len(ws)

    def body(*refs):
        for i in range(n):
            w_ref, g_ref, m_ref, v_ref = (refs[k * n + i] for k in range(4))
            d_out, m_out, v_out = (refs[(4 + k) * n + i] for k in range(3))
            d_out[...], m_out[...], v_out[...] = _adamw_math(w_ref[...], g_ref[...], m_ref[...], v_ref[...])

    vmem = pl.BlockSpec(memory_space=pltpu.VMEM)
    shapes = tuple(jax.ShapeDtypeStruct(w.shape, F32) for w in ws)
    outs = pl.pallas_call(body, out_shape=shapes * 3, in_specs=[vmem] * (4 * n), out_specs=tuple([vmem] * (3 * n)),
                          name=name, compiler_params=_params())(*ws, *gs, *ms, *vs)
    return outs[:n], outs[n:2 * n], outs[2 * n:]


def _adamw_shard(w, mine, other, m, v, core, name):
    lead, rows, cols = w.shape
    half = rows // 2
    tr = _rows_block(half, cols)
    nhalf = half // tr

    def body(core_ref, w_ref, mine_ref, other_ref, m_ref, v_ref, g_out, d_out, m_out, v_out):
        g = jnp.where(pl.program_id(1) // nhalf == core_ref[0], mine_ref[...], other_ref[...])
        d_new, m_new, v_new = _adamw_math(w_ref[...], g, m_ref[...], v_ref[...])
        g_out[...] = g
        d_out[...] = d_new
        m_out[...] = m_new
        v_out[...] = v_new

    whole = pl.BlockSpec((None, tr, cols), lambda l, i, core_ref: (l, i, 0))
    own = pl.BlockSpec((None, tr, cols),
                       lambda l, i, core_ref: (l, jnp.where(i // nhalf == core_ref[0], i % nhalf, 0), 0))
    far = pl.BlockSpec((None, tr, cols),
                       lambda l, i, core_ref: (l, jnp.where(i // nhalf == core_ref[0], 0, i % nhalf), 0))
    shape = jax.ShapeDtypeStruct(w.shape, F32)
    grid_spec = pltpu.PrefetchScalarGridSpec(num_scalar_prefetch=1, grid=(lead, rows // tr),
                                             in_specs=[whole, own, far, whole, whole], out_specs=(whole,) * 4)
    return pl.pallas_call(body, out_shape=(shape,) * 4, grid_spec=grid_spec, name=name,
                          compiler_params=_params(("parallel", "parallel")))(core, w, mine, other, m, v)


def _place():
    x, y, c = lax.axis_index("x"), lax.axis_index("y"), lax.axis_index("c")
    chips = [(1 - x, y), (x, 1 - y), (1 - x, 1 - y)]
    return x, y, c, 2 * x + y, chips


def _run_remote(copies):
    for send, _ in copies:
        send.start()
    for send, recv in copies:
        send.wait_send()
        recv.wait_recv()


def _half_rows(ref, c):
    half = ref.shape[1] // 2
    return pl.ds(pl.multiple_of(c * half, 8), half)


def _gather_issue(whole):
    def issue(_, refs, ssem, rsem):
        x, y, c, me, chips = _place()
        copies = []
        for w, ref in enumerate(refs):
            part = (me,) if w >= len(refs) - whole else (me, _half_rows(ref, c))
            mine = ref.at[part]
            for j, (px, py) in enumerate(chips):
                sems = (ssem.at[3 * w + j], rsem.at[3 * w + j])
                peer = dict(device_id=(px, py, c), device_id_type=MESH)
                send = pltpu.make_async_remote_copy(mine, mine, *sems, **peer)
                recv = pltpu.make_async_remote_copy(mine, ref.at[(2 * px + py,) + part[1:]], *sems, **peer)
                copies.append((send, recv))
        return copies
    return issue


def _gather_carry(fulls, whole=0):
    return _Carry(arrays=tuple(fulls), outs=tuple(jax.ShapeDtypeStruct(f.shape, f.dtype) for f in fulls),
                  in_place=True, copies=3 * len(fulls), issue=_gather_issue(whole))


def _scatter_issue(ins, outs, ssem, rsem):
    x, y, c, me, chips = _place()
    copies = []
    for w, (src, dst) in enumerate(zip(ins, outs)):
        for j, (px, py) in enumerate(chips):
            cp = pltpu.make_async_remote_copy(src.at[2 * px + py], dst.at[j], ssem.at[3 * w + j], rsem.at[3 * w + j],
                                              device_id=(px, py, c), device_id_type=MESH)
            copies.append((cp, cp))
    return copies


def _scatter_carry(pairs):
    return _Carry(arrays=tuple(pairs), outs=tuple(jax.ShapeDtypeStruct((3,) + p.shape[1:], p.dtype) for p in pairs),
                  in_place=False, copies=3 * len(pairs), issue=_scatter_issue)


def _chip_gather(fulls, whole, name):
    n = len(fulls)
    issue = _gather_issue(whole)

    def body(*refs):
        _run_remote(issue(None, refs[n:2 * n], refs[2 * n], refs[2 * n + 1]))

    out_shape = tuple(jax.ShapeDtypeStruct(f.shape, f.dtype) for f in fulls)
    return pl.pallas_call(body, out_shape=out_shape, in_specs=[ANY] * n, out_specs=tuple([ANY] * n),
                          input_output_aliases={w: w for w in range(n)},
                          scratch_shapes=[pltpu.SemaphoreType.DMA((3 * n,)), pltpu.SemaphoreType.DMA((3 * n,))],
                          name=name)(*fulls)


class _Blocks(NamedTuple):
    tile: tuple
    dtype: object
    moves: list
    landed: object


STAGE_BYTES = 4 * 1024 * 1024


def _rows(start, size):
    return pl.ds(start if isinstance(start, int) else pl.multiple_of(start, 8), size)


def _stage_tile(lead, rows, cols, dtype):
    per_row = cols * jnp.dtype(dtype).itemsize
    if lead * rows * per_row <= STAGE_BYTES:
        return lead, rows
    fits = [cand for cand in range(8, rows + 1, 8) if rows % cand == 0 and cand * per_row <= STAGE_BYTES]
    return 1, (fits[-1] if fits else rows)


def _plan_gathered(_, refs, c, me):
    work = []
    for ref in refs:
        lead, rows, cols = ref.shape
        half = rows // 2
        nk, tr = _stage_tile(lead, half, cols, ref.dtype)
        if nk == lead:
            part = ref.at[pl.ds(0, lead), _rows(c * half, half)]
            moves, landed = [(part, part)], ref.at[pl.ds(0, lead), pl.ds(0, half)]
        else:
            moves = []
            for j in range(N_CHIPS - 1):
                k = jnp.bitwise_xor(me, j + 1)
                for b in range(half // tr):
                    part = ref.at[pl.ds(k, 1), _rows(c * half + b * tr, tr)]
                    moves.append((part, part))
            landed = ref.at[pl.ds(0, N_CHIPS - 1), pl.ds(0, half)]
        work.append(_Blocks((nk, tr, cols), ref.dtype, moves, landed))
    return work


def _plan_grad_halves(grads, rxs, c, me):
    work = []
    for g, rx in zip(grads, rxs):
        lead, half, cols = rx.shape
        nk, tr = _stage_tile(lead, half, cols, g.dtype)
        moves = [(g.at[pl.ds(k, nk), _rows((1 - c) * half + b * tr, tr)], rx.at[pl.ds(k, nk), _rows(b * tr, tr)])
                 for k in range(0, lead, nk) for b in range(half // tr)]
        work.append(_Blocks((nk, tr, cols), g.dtype, moves, rx))
    return work


def _plan_whole(srcs, dsts, c, me):
    work = []
    for src, dst in zip(srcs, dsts):
        lead, rows, cols = src.shape
        nk, tr = _stage_tile(lead, rows, cols, src.dtype)
        moves = [(src.at[pl.ds(k, nk), _rows(b * tr, tr)], dst.at[pl.ds(k, nk), _rows(b * tr, tr)])
                 for k in range(0, lead, nk) for b in range(rows // tr)]
        work.append(_Blocks((nk, tr, cols), src.dtype, moves, dst))
    return work


def _push_staged(arrays, outs, in_place, plan, name):
    n, no = len(arrays), len(outs)

    def body(*refs):
        ins, out_refs = refs[:n], refs[n:n + no]
        rsem, idle = refs[n + no], refs[n + no + 1]
        x, y, c, me, _ = _place()
        peer = dict(device_id=(x, y, 1 - c), device_id_type=MESH)
        work = plan(ins, out_refs, c, me)
        for w, blocks in enumerate(work):
            def staged(buf, lsem, ssem, w=w, blocks=blocks):
                sends = []
                for i, (src, dst) in enumerate(blocks.moves):
                    slot = i % 2
                    if i >= 2:
                        sends[i - 2].wait_send()
                    load = pltpu.make_async_copy(src, buf.at[slot], lsem.at[slot])
                    load.start()
                    load.wait()
                    send = pltpu.make_async_remote_copy(buf.at[slot], dst, ssem.at[slot], rsem.at[w], **peer)
                    send.start()
                    sends.append(send)
                for send in sends[-2:]:
                    send.wait_send()

            pl.run_scoped(staged, pltpu.VMEM((2,) + blocks.tile, blocks.dtype), pltpu.SemaphoreType.DMA((2,)),
                          pltpu.SemaphoreType.DMA((2,)))
        for w, blocks in enumerate(work):
            pltpu.make_async_remote_copy(blocks.landed, blocks.landed, idle, rsem.at[w], **peer).wait_recv()

    return pl.pallas_call(body, out_shape=tuple(outs), in_specs=[ANY] * n, out_specs=tuple([ANY] * no),
                          input_output_aliases={w: w for w in range(n)} if in_place else {},
                          scratch_shapes=[pltpu.SemaphoreType.DMA((n,)), pltpu.SemaphoreType.DMA],
                          name=name, compiler_params=_params())(*arrays)


def _pair_sum(g, rx, core, name):
    lead, half, cols = rx.shape
    tr = _rows_block(half, cols)
    nrow = half // tr

    def body(core_ref, g_ref, rx_ref, o_ref):
        o_ref[...] = (g_ref[...].astype(F32) + rx_ref[...].astype(F32)).astype(o_ref.dtype)

    spec = pl.BlockSpec((None, tr, cols), lambda k, i, core_ref: (k, i, 0))
    grid_spec = pltpu.PrefetchScalarGridSpec(
        num_scalar_prefetch=1, grid=(lead, nrow),
        in_specs=[pl.BlockSpec((None, tr, cols), lambda k, i, core_ref: (k, core_ref[0] * nrow + i, 0)), spec],
        out_specs=spec)
    return pl.pallas_call(body, out_shape=jax.ShapeDtypeStruct(rx.shape, rx.dtype), grid_spec=grid_spec, name=name,
                          compiler_params=_params(("parallel", "parallel")))(core, g, rx)


def _push_sibling(src, name):
    lead, rows, cols = src.shape
    tr = _rows_block(rows, cols)
    nrow = rows // tr

    def body(blk_ref, rx_ref, ssem, rsem):
        x, y, c = lax.axis_index("x"), lax.axis_index("y"), lax.axis_index("c")
        k, i = pl.program_id(0), pl.program_id(1)
        peer = dict(device_id=(x, y, 1 - c), device_id_type=MESH)
        dst = rx_ref.at[pl.ds(k, 1), pl.ds(pl.multiple_of(i * tr, 8), tr)]
        cp = pltpu.make_async_remote_copy(blk_ref, dst, ssem, rsem, **peer)
        cp.start()
        cp.wait_send()

        @pl.when(jnp.logical_and(k == lead - 1, i == nrow - 1))
        def _():
            pltpu.make_async_remote_copy(rx_ref, rx_ref, ssem, rsem, **peer).wait_recv()

    return pl.pallas_call(body, out_shape=jax.ShapeDtypeStruct(src.shape, src.dtype), grid=(lead, nrow),
                          in_specs=[pl.BlockSpec((1, tr, cols), lambda k, i: (k, i, 0))], out_specs=ANY,
                          scratch_shapes=[pltpu.SemaphoreType.DMA, pltpu.SemaphoreType.DMA], name=name,
                          compiler_params=_params(("arbitrary", "arbitrary")))(src)


def _chip_scatter(pairs, small, name):
    n = len(pairs)

    def body(*refs):
        ins, small_ref = refs[:n], refs[n]
        outs, small_out = refs[n + 1:2 * n + 1], refs[2 * n + 1]
        ssem, rsem = refs[2 * n + 2:]
        x, y, c, me, chips = _place()
        copies = []
        for w in range(n + 1):
            for j, (px, py) in enumerate(chips):
                src = ins[w].at[2 * px + py] if w < n else small_ref
                dst = (outs[w] if w < n else small_out).at[j]
                cp = pltpu.make_async_remote_copy(src, dst, ssem.at[3 * w + j], rsem.at[3 * w + j],
                                                  device_id=(px, py, c), device_id_type=MESH)
                copies.append((cp, cp))
        _run_remote(copies)

    out_shape = tuple(jax.ShapeDtypeStruct((3,) + it.shape[1:], it.dtype) for it in pairs) + \
                (jax.ShapeDtypeStruct((3,) + small.shape, small.dtype),)
    return pl.pallas_call(body, out_shape=out_shape, in_specs=[ANY] * (n + 1), out_specs=tuple([ANY] * (n + 1)),
                          scratch_shapes=[pltpu.SemaphoreType.DMA((3 * n + 3,)),
                                          pltpu.SemaphoreType.DMA((3 * n + 3,))],
                          name=name)(*pairs, small)


def _pack(arrays):
    flat = jnp.concatenate([a.reshape(-1).astype(F32) for a in arrays])
    pad = (-flat.shape[0]) % (8 * LANES)
    return jnp.pad(flat, (0, pad)).reshape(-1, LANES)


def _unpack(packed, shapes):
    flat = packed.reshape(-1)
    out, off = [], 0
    for shp in shapes:
        size = math.prod(shp)
        out.append(flat[off:off + size].reshape(shp))
        off += size
    return out


def kernel(x, mem, g_mix, w_in, b_gate, a_ln_g, a_ln_b, a_ws, a_bs, b_conv, c_wg, c_scale, w_branch_a, w_branch_b, w_branch_c, w_o, g_xattn, g_mem, w_xq, w_xkv, w_xo, g_ffn, w_ffn_in, w_ffn_out, g_final, loss_target, m_g_mix, m_w_in, m_b_gate, m_a_ln_g, m_a_ln_b, m_a_ws, m_a_bs, m_b_conv, m_c_wg, m_c_scale, m_w_branch_a, m_w_branch_b, m_w_branch_c, m_w_o, m_g_xattn, m_g_mem, m_w_xq, m_w_xkv, m_w_xo, m_g_ffn, m_w_ffn_in, m_w_ffn_out, m_g_final, v_g_mix, v_w_in, v_b_gate, v_a_ln_g, v_a_ln_b, v_a_ws, v_a_bs, v_b_conv, v_c_wg, v_c_scale, v_w_branch_a, v_w_branch_b, v_w_branch_c, v_w_o, v_g_xattn, v_g_mem, v_w_xq, v_w_xkv, v_w_xo, v_g_ffn, v_w_ffn_in, v_w_ffn_out, v_g_final):
    args = locals()
    wts = {n: args[n] for n in WEIGHTS}
    mom = {n: args["m_" + n] for n in WEIGHTS}
    var = {n: args["v_" + n] for n in WEIGHTS}
    xs = x[0]
    mems = mem[0]
    tgt = loss_target[0]
    chip = 2 * lax.axis_index("x") + lax.axis_index("y")

    bias_pack = jnp.zeros((DEPTH, 8, 384), F32)
    bias_pack = bias_pack.at[:, 0:3, 0:256].set(b_gate).at[:, 0:3, 256:384].set(b_conv)
    bias_slots = lax.dynamic_update_slice(jnp.zeros((N_CHIPS, DEPTH, 8, 384), F32), bias_pack[None], (chip, 0, 0, 0))
    core = lax.axis_index("c")
    chip_arr = jnp.reshape(chip, (1,)).astype(jnp.int32)
    core_arr = jnp.reshape(core, (1,)).astype(jnp.int32)
    slots = {n: _cast_to_slots(wts[n], chip_arr, name=f"cast_{n}") for n in BIG}
    layer_weights = [{}, {}]

    def land(l, names, gathered, tag):
        shapes = [jax.ShapeDtypeStruct(g.shape, g.dtype) for g in gathered]
        for n, g in zip(names, _push_staged(list(gathered), shapes, True, _plan_gathered,
                                            name=f"l{l}_gather_cores_{tag}")):
            if n in COL_SHARDED:
                layer_weights[l][n] = jnp.transpose(g, (1, 0, 2)).reshape(1, g.shape[1], N_CHIPS * g.shape[2])
            else:
                layer_weights[l][n] = g.reshape(1, N_CHIPS * g.shape[1], g.shape[2])

    def gather_ride(l, names):
        return _gather_carry([slots[n][l] for n in names])

    rest_of_first = tuple(n for n in BIG if n not in NEED_FIRST)
    most_of_second = tuple(n for n in BIG if n not in NEED_LAST)
    first = _chip_gather([slots[n][0] for n in NEED_FIRST] + [bias_slots], 1, name="l0_gather_chips")
    land(0, NEED_FIRST, first[:-1], "first")
    biases = first[-1]
    b_gate_full = jnp.transpose(biases[:, :, 0:3, 0:256], (1, 2, 0, 3)).reshape(DEPTH, 3, D_MODEL)
    b_conv_full = jnp.transpose(biases[:, :, 0:3, 256:384], (1, 2, 0, 3)).reshape(DEPTH, 3, B_WIDTH)

    tril = jnp.tril(jnp.ones((CHUNK, CHUNK), bool))

    def mixer_params(l):
        wtril = jnp.where(tril[None], a_ws[l], 0.0)
        prm = (a_ln_g[l][None], a_ln_b[l][None], wtril.astype(BF16),
               jnp.broadcast_to(a_bs[l][:, :, None], (A_GROUPS, CHUNK, CHUNK)), b_conv_full[l],
               c_wg[l].astype(BF16), c_scale[l][None])
        return prm, jnp.swapaxes(wtril, 1, 2).astype(BF16)

    saved = []
    xc = xs
    for l in range(DEPTH):
        sv = {"x0": xc}
        prm, _ = mixer_params(l)
        fw = layer_weights[l]
        riding = rest_of_first if l == 0 else NEED_LAST
        out = _norm_mm(xc, g_mix[l][None], fw["w_in"], 0, tn=1536, name=f"l{l}_proj", carry=gather_ride(l, riding))
        sv["h"], sv["proj"] = out[0], out[1]
        land(l, riding, out[2], "rest")
        sv["ypre"] = _mixer_fwd(sv["proj"], prm, name=f"l{l}_mixers")
        sv["merged"] = _merge_fwd(sv["ypre"], sv["proj"], fw["w_branch_a"][0], fw["w_branch_b"][0],
                                  fw["w_branch_c"][0], b_gate_full[l], name=f"l{l}_merge")
        xc = _mm(sv["merged"], fw["w_o"], b_lead=0, mode="nn", out_dtype=F32, name=f"l{l}_mix_out", tm=1024, tn=1024,
                 tk=1024, res=xc)
        sv["x1"] = xc
        sv["hq"], sv["q"] = _norm_mm(xc, g_xattn[l][None], fw["w_xq"], 0, tn=1024, name=f"l{l}_q")
        sv["m"] = _rms_fwd(mems, g_mem[l][None], name=f"l{l}_mem_norm")
        sv["kv"] = _mm(sv["m"], fw["w_xkv"], b_lead=0, mode="nn", out_dtype=BF16, name=f"l{l}_kv", tm=256, tn=512,
                       tk=1024)
        sv["o"] = _attn_fwd(sv["q"], sv["kv"], name=f"l{l}_attn")
        xc = _mm(sv["o"], fw["w_xo"], b_lead=0, mode="nn", out_dtype=F32, name=f"l{l}_xattn_out", tm=1024, tn=1024,
                 tk=1024, res=xc)
        sv["x2"] = xc
        out2 = _norm_ffn_in(xc, g_ffn[l][None], fw["w_ffn_in"], 0, name=f"l{l}_ffn_in",
                            carry=gather_ride(1, most_of_second) if l == 0 else None)
        sv["h2"], sv["gu"], sv["act"] = out2[0], out2[1], out2[2]
        if l == 0:
            land(1, most_of_second, out2[3], "most")
        xc = _mm(sv["act"], fw["w_ffn_out"], b_lead=0, mode="nn", out_dtype=F32, name=f"l{l}_ffn_out", tm=1024,
                 tn=512, tk=FFN_HIDDEN, res=xc)
        saved.append(sv)

    loss_part, dx, dg_final = _final_loss(xc, g_final[None], tgt, name="final_loss")
    loss = lax.psum(loss_part[0, 0], ("x", "y", "c"))

    pairs = [{}, {}]
    received = [{}, {}]
    small_grads = [None] * DEPTH

    def core_pairs(l, names, grads, tag):
        halves = [jax.ShapeDtypeStruct((g.shape[0], g.shape[1] // 2, g.shape[2]), g.dtype) for g in grads]
        came = _push_staged(grads, halves, False, _plan_grad_halves, name=f"l{l}_reduce_cores_{tag}")
        for n, g, rx in zip(names, grads, came):
            pairs[l][n] = _pair_sum(g, rx, core_arr, name=f"l{l}_pair_sum_{n}")

    def scatter_ride(l, names, when):
        return _scatter_carry([pairs[l][n] for n in names]) if when else None

    ride_second = tuple(n for n in BIG if n not in RIDE_FIRST)
    early, early_big, early_small = NEED_LAST + ("w_xo", "w_xq", "w_xkv"), NEED_LAST + ("w_xkv",), ("w_xo", "w_xq")
    middle = ("w_o", "w_branch_a", "w_branch_b", "w_branch_c")
    for l in reversed(range(DEPTH)):
        sv = saved[l]
        prm, wtril_t = mixer_params(l)
        gb = {}
        sg = {}
        fw = layer_weights[l]
        dgu = _d_act_swiglu(dx, fw["w_ffn_out"], 0, sv["gu"], name=f"l{l}_d_act")
        gb["w_ffn_out"] = _mm(sv["act"], dx, mode="tn", out_dtype=BF16, name=f"l{l}_dw_ffn_out", tm=1408, tn=1024,
                              tk=TOKEN_STEP).reshape(N_CHIPS, FFN_HIDDEN // N_CHIPS, D_MODEL)
        out = _mm(sv["h2"], dgu, mode="tn", out_dtype=BF16, name=f"l{l}_dw_ffn_in", tm=1024, tn=FFN_TILE,
                  tk=TOKEN_STEP, b_halves=True, out_shards=N_CHIPS, carry=scatter_ride(1, RIDE_FIRST, l == 0))
        gb["w_ffn_in"] = out[0] if l == 0 else out
        out2 = _mm_nt_norm_bwd(dgu, fw["w_ffn_in"], 0, sv["x2"], g_ffn[l][None], dx, tm=512, a_halves=True,
                               name=f"l{l}_d_h2", carry=scatter_ride(1, ride_second, l == 0))
        dx, sg["g_ffn"] = out2[0], out2[1]
        if l == 0:
            received[1].update(zip(RIDE_FIRST, out[1]))
            received[1].update(zip(ride_second, out2[2]))
        do = _mm(dx, fw["w_xo"], b_lead=0, mode="nt", out_dtype=BF16, name=f"l{l}_d_o", tm=1024, tn=1024, tk=1024)
        gb["w_xo"] = _mm(sv["o"], dx, mode="tn", out_dtype=BF16, name=f"l{l}_dw_xo", tm=1024, tn=1024,
                         tk=TOKEN_STEP).reshape(N_CHIPS, D_MODEL // N_CHIPS, D_MODEL)
        dq, dkv = _attn_bwd(sv["q"], sv["kv"], do, name=f"l{l}_d_attn")
        gb["w_xq"] = _mm(sv["hq"], dq, mode="tn", out_dtype=BF16, name=f"l{l}_dw_xq", tm=1024, tn=1024,
                         tk=TOKEN_STEP).reshape(N_CHIPS, D_MODEL // N_CHIPS, D_MODEL)
        dm = _mm(dkv, fw["w_xkv"], b_lead=0, mode="nt", out_dtype=BF16, name=f"l{l}_d_m", tm=256, tn=1024, tk=1024)
        gb["w_xkv"] = _mm(sv["m"], dkv, mode="tn", out_dtype=BF16, name=f"l{l}_dw_xkv", tm=1024, tn=512, tk=256,
                          out_shards=N_CHIPS)
        _, sg["g_mem"] = _rms_bwd(mems, g_mem[l][None], dm, None, name=f"l{l}_d_mem_norm")
        if l == 0:
            core_pairs(0, early, [gb[n] for n in early], "early")
        dx, sg["g_xattn"] = _mm_nt_norm_bwd(dq, fw["w_xq"], 0, sv["x1"], g_xattn[l][None], dx, tm=1024,
                                            name=f"l{l}_d_hq")
        dmerged = _mm(dx, fw["w_o"], b_lead=0, mode="nt", out_dtype=BF16, name=f"l{l}_d_merged", tm=1024, tn=1024,
                      tk=1024)
        gb["w_o"] = _mm(sv["merged"], dx, mode="tn", out_dtype=BF16, name=f"l{l}_dw_o", tm=1024, tn=1024,
                        tk=TOKEN_STEP).reshape(N_CHIPS, D_MODEL // N_CHIPS, D_MODEL)
        out3 = _merge_bwd(sv["ypre"], sv["proj"], dmerged, fw["w_branch_a"][0], fw["w_branch_b"][0],
                          fw["w_branch_c"][0], b_gate_full[l], name=f"l{l}_d_merge",
                          carry=scatter_ride(0, early_big, l == 0))
        dypre, dproj, sg["b_gate"] = out3[0], out3[1], out3[2]
        for n, dw in zip(("w_branch_a", "w_branch_b", "w_branch_c"), out3[3:6]):
            gb[n] = jnp.transpose(dw.reshape(A_WIDTH, N_CHIPS, D_MODEL // N_CHIPS), (1, 0, 2)).astype(BF16)
        if l == 0:
            core_pairs(0, middle, [gb[n] for n in middle], "middle")
        out5 = _mixer_bwd(sv["proj"], dypre, dproj, prm, wtril_t, name=f"l{l}_d_mixers",
                          carry=scatter_ride(0, middle, l == 0))
        (dproj, sg["a_ln_g"], sg["a_ln_b"], sg["a_ws"], sg["a_bs"], sg["b_conv"], sg["c_wg"], sg["c_scale"]) = out5[:8]
        out4 = _mm(sv["h"], dproj, mode="tn", out_dtype=BF16, name=f"l{l}_dw_in", tm=1024, tn=1536, tk=TOKEN_STEP,
                   out_shards=N_CHIPS, carry=scatter_ride(0, early_small, l == 0))
        gb["w_in"] = out4[0] if l == 0 else out4
        if l == 0:
            core_pairs(0, ("w_in",), [gb["w_in"]], "last")
        out6 = _mm_nt_norm_bwd(dproj, fw["w_in"], 0, sv["x0"], g_mix[l][None], dx, tm=512, name=f"l{l}_d_h",
                               carry=scatter_ride(0, ("w_in",), l == 0))
        dx, sg["g_mix"] = out6[0], out6[1]
        if l == 0:
            received[0].update(zip(early_big, out3[6]))
            received[0].update(zip(middle, out5[8]))
            received[0].update(zip(early_small, out4[1]))
            received[0].update(zip(("w_in",), out6[2]))
        else:
            core_pairs(1, BIG, [gb[n] for n in BIG], "all")
        small_grads[l] = sg
    grad_x = dx[None]

    small_names = [n for n in WEIGHTS if n not in BIG]
    small_full_shapes = {n: ((DEPTH, 3, D_MODEL) if n == "b_gate" else (DEPTH, 3, B_WIDTH) if n == "b_conv"
                             else wts[n].shape) for n in small_names}
    small_local = []
    for n in small_names:
        if n == "g_final":
            small_local.append(dg_final)
        else:
            small_local.append(jnp.stack([small_grads[l][n].reshape(small_full_shapes[n][1:]) for l in range(DEPTH)]))
    small_pack = _pack(small_local)

    small_rx = _push_sibling(small_pack[None], name="reduce_cores_small")
    small_pair = _elementwise(lambda a, b: (a + b,), [small_pack[None], small_rx], [F32], name="pair_sum_small")[0][0]
    small_came = _chip_scatter([], small_pair, name="reduce_chips_small")[0]
    small_sum = _chip_sum_ordered(small_pair, small_came, chip_arr, name="chip_sum_small")

    out_g, out_d, out_m, out_v = {}, {}, {}, {}
    mine = [_chip_sum(pairs[0][n], received[0][n], pairs[1][n], received[1][n], chip_arr, name=f"chip_sum_{n}")
            for n in BIG]
    other = _push_staged(mine, [jax.ShapeDtypeStruct(g.shape, g.dtype) for g in mine], False, _plan_whole,
                         name="share_cores")
    for i, n in enumerate(BIG):
        out_g[n], out_d[n], out_m[n], out_v[n] = _adamw_shard(wts[n], mine[i], other[i], mom[n], var[n], core_arr,
                                                              name=f"adamw_{n}")
    small_g = dict(zip(small_names, _unpack(small_sum, [small_full_shapes[n] for n in small_names])))
    small_g["b_gate"] = lax.dynamic_slice_in_dim(small_g["b_gate"], chip * 256, 256, axis=2)
    small_g["b_conv"] = lax.dynamic_slice_in_dim(small_g["b_conv"], chip * 128, 128, axis=2)
    def two_d(a):
        return a.reshape(1, -1) if a.ndim == 1 else a

    upd = _adamw_small(*[[two_d(d[n]) for n in small_names] for d in (wts, small_g, mom, var)], name="adamw_small")
    for n, d, m_new, v_new in zip(small_names, *upd):
        shape = wts[n].shape
        out_g[n], out_d[n], out_m[n], out_v[n] = small_g[n], d.reshape(shape), m_new.reshape(shape), v_new.reshape(shape)

    return (loss, grad_x, *[out_g[n] for n in WEIGHTS], *[out_d[n] for n in WEIGHTS], *[out_m[n] for n in WEIGHTS],
            *[out_v[n] for n in WEIGHTS])
```

```python
import functools
import math
from typing import Callable, NamedTuple

import jax
import jax.numpy as jnp
from jax import lax
from jax.experimental import pallas as pl
from jax.experimental.pallas import tpu as pltpu

F32 = jnp.float32
BF16 = jnp.bfloat16

D_MODEL = 1024
DEPTH = 2
MEM_LEN = 256
EPS = 1e-6
CHUNK = 128
A_GROUPS = 4
A_WIDTH = 512
B_WIDTH = 512
C_WIDTH = 512
C_GROUP = 128
POOL_WINDOWS = (2, 4, 8, 16)
HALO = 16
IN_COLS = 6144
MIX_COLS = 3072
XATTN_HEADS = 4
HEAD_DIM = 256
FFN_HIDDEN = 2816
N_CHIPS = 4

ADAM_LR = 0.001
ADAM_B1 = 0.9
ADAM_B2 = 0.999
ADAM_EPS = 1e-08
ADAM_WD = 0.01
ADAM_STEP = 10

V7X_VMEM_BYTES = 64 * 1024 * 1024
VMEM_LIMIT = (V7X_VMEM_BYTES * 3) // 4
LANES = 128
MESH = pl.DeviceIdType.MESH
ANY = pl.BlockSpec(memory_space=pl.ANY)

BIG = ("w_in", "w_branch_a", "w_branch_b", "w_branch_c", "w_o", "w_xq", "w_xkv", "w_xo", "w_ffn_in", "w_ffn_out")
COL_SHARDED = ("w_in", "w_branch_a", "w_branch_b", "w_branch_c", "w_xkv", "w_ffn_in")
RIDE_FIRST = ("w_in", "w_ffn_out", "w_branch_a", "w_branch_b", "w_branch_c")
NEED_FIRST = ("w_in", "w_branch_a", "w_branch_b", "w_branch_c", "w_o")
NEED_LAST = ("w_ffn_in", "w_ffn_out")
READ_AS_SLABS = ("w_in", "w_ffn_in")
SMALL_REPL = ("g_mix", "a_ln_g", "a_ln_b", "a_ws", "a_bs", "c_wg", "c_scale", "g_xattn", "g_mem", "g_ffn", "g_final")
SMALL_SHARDED = ("b_gate", "b_conv")
WEIGHTS = ("g_mix", "w_in", "b_gate", "a_ln_g", "a_ln_b", "a_ws", "a_bs", "b_conv", "c_wg", "c_scale", "w_branch_a",
           "w_branch_b", "w_branch_c", "w_o", "g_xattn", "g_mem", "w_xq", "w_xkv", "w_xo", "g_ffn", "w_ffn_in",
           "w_ffn_out", "g_final")


def _params(sem=None):
    return pltpu.CompilerParams(dimension_semantics=sem, vmem_limit_bytes=VMEM_LIMIT)


def _blk(dim, pref):
    return pref if dim % pref == 0 else dim


class _Carry(NamedTuple):
    arrays: tuple
    outs: tuple
    in_place: bool
    copies: int
    issue: Callable


def _call_with_carry(body, *, out_shape, grid, in_specs, out_specs, scratch_shapes, name, semantics, operands, carry,
                     aliases=None):
    own_aliases = dict(aliases or {})
    if carry is None:
        return pl.pallas_call(body, out_shape=tuple(out_shape), grid=grid, in_specs=list(in_specs),
                              out_specs=tuple(out_specs), scratch_shapes=list(scratch_shapes), name=name,
                              input_output_aliases=own_aliases, compiler_params=_params(semantics))(*operands)
    n_in, n_out, na, no = len(operands), len(out_shape), len(carry.arrays), len(carry.outs)

    def carried(*refs):
        ins, cins = refs[:n_in], refs[n_in:n_in + na]
        outs, couts = refs[n_in + na:n_in + na + n_out], refs[n_in + na + n_out:n_in + na + n_out + no]
        rest = refs[n_in + na + n_out + no:]
        scratch, ssem, rsem = rest[:-2], rest[-2], rest[-1]
        first = functools.reduce(jnp.logical_and, [pl.program_id(ax) == 0 for ax in range(len(grid))])
        last = functools.reduce(jnp.logical_and, [pl.program_id(ax) == grid[ax] - 1 for ax in range(len(grid))])

        @pl.when(first)
        def _():
            for send, _ in carry.issue(cins, couts, ssem, rsem):
                send.start()

        body(*ins, *outs, *scratch)

        @pl.when(last)
        def _():
            for send, recv in carry.issue(cins, couts, ssem, rsem):
                send.wait_send()
                recv.wait_recv()

    aliases = dict(own_aliases)
    if carry.in_place:
        aliases.update({n_in + i: n_out + i for i in range(na)})
    sems = [pltpu.SemaphoreType.DMA((carry.copies,)), pltpu.SemaphoreType.DMA((carry.copies,))]
    return pl.pallas_call(carried, out_shape=tuple(out_shape) + tuple(carry.outs), grid=grid,
                          in_specs=list(in_specs) + [ANY] * na, out_specs=tuple(out_specs) + tuple([ANY] * no),
                          scratch_shapes=list(scratch_shapes) + sems, input_output_aliases=aliases, name=name,
                          compiler_params=_params(("arbitrary",) * len(grid)))(*operands, *carry.arrays)


def _mm(a, b, *, mode, out_dtype, name, tm, tn, tk, res=None, b_lead=None, b_halves=False, out_shards=None,
        carry=None):
    dims = {"nn": (((1,), (0,)), ((), ())), "nt": (((1,), (1,)), ((), ())), "tn": (((0,), (0,)), ((), ()))}[mode]
    b_rows, b_last = b.shape[-2], b.shape[-1]
    if mode == "nn":
        m, k, n = a.shape[0], a.shape[1], b_last
        assert b_rows == k
    elif mode == "nt":
        m, k, n = a.shape[0], a.shape[1], b_rows
        assert b_last == k
    else:
        k, m, n = a.shape[0], a.shape[1], (2 * b_last if b_halves else b_last)
        assert b_rows == k
    tm, tn, tk = _blk(m, tm), _blk(n, tn), _blk(k, tk)
    nk = k // tk
    grid = (m // tm, n // tn, nk)

    if mode == "tn":
        a_spec = pl.BlockSpec((tk, tm), lambda i, j, kk: (kk, i))
        b_block, b_idx = (tk, tn), (lambda i, j, kk: (kk, j))
    else:
        a_spec = pl.BlockSpec((tm, tk), lambda i, j, kk: (i, kk))
        if mode == "nn":
            b_block, b_idx = (tk, tn), (lambda i, j, kk: (kk, j))
        else:
            b_block, b_idx = (tn, tk), (lambda i, j, kk: (j, kk))
    if b_halves:
        assert mode == "tn" and b_lead is None and b_last % tn == 0
        per_half = b_last // tn
        b_spec = pl.BlockSpec((None,) + b_block, lambda i, j, kk: (j // per_half, kk, j % per_half))
    elif b_lead is None:
        b_spec = pl.BlockSpec(b_block, b_idx)
    else:
        b_spec = pl.BlockSpec((None,) + b_block, lambda i, j, kk: (b_lead,) + b_idx(i, j, kk))
    in_specs = [a_spec, b_spec]
    operands = [a, b]
    if res is not None:
        in_specs.append(pl.BlockSpec((tm, tn), lambda i, j, kk: (i, j)))
        operands.append(res)
    if out_shards is None:
        out_shape = jax.ShapeDtypeStruct((m, n), out_dtype)
        out_spec = pl.BlockSpec((tm, tn), lambda i, j, kk: (i, j))
    else:
        per = n // out_shards
        assert per % tn == 0
        nps = per // tn
        out_shape = jax.ShapeDtypeStruct((out_shards, m, per), out_dtype)
        out_spec = pl.BlockSpec((None, tm, tn), lambda i, j, kk: (j // nps, i, j % nps))

    def body(*refs):
        a_ref, b_ref = refs[0], refs[1]
        res_ref = refs[2] if res is not None else None
        o_ref = refs[3] if res is not None else refs[2]
        part = lax.dot_general(a_ref[...].astype(BF16), b_ref[...].astype(BF16), dims, preferred_element_type=F32)

        def finish(acc):
            if res_ref is not None:
                acc = acc + res_ref[...]
            o_ref[...] = acc.astype(out_dtype)

        if nk == 1:
            finish(part)
        else:
            acc_ref = refs[-1]
            kk = pl.program_id(2)

            @pl.when(kk == 0)
            def _():
                acc_ref[...] = part

            @pl.when(kk > 0)
            def _():
                acc_ref[...] += part

            @pl.when(kk == nk - 1)
            def _():
                finish(acc_ref[...])

    scratch = [] if nk == 1 else [pltpu.VMEM((tm, tn), F32)]
    outs = _call_with_carry(body, out_shape=(out_shape,), grid=grid, in_specs=in_specs, out_specs=(out_spec,),
                            scratch_shapes=scratch, name=name, semantics=("parallel", "parallel", "arbitrary"),
                            operands=operands, carry=carry)
    return outs[0] if carry is None else (outs[0], outs[1:])


ROW_CHUNK = 256
FFN_TILE = 1408
TOKEN_STEP = 2048
NT_DIMS = (((1,), (1,)), ((), ()))


def _norm_rows_into(x_ref, g_ref, h_ref, hs_ref, tm):
    rc = min(ROW_CHUNK, tm)
    for r0 in range(0, tm, rc):
        xv = x_ref[r0:r0 + rc, :]
        r = lax.rsqrt(jnp.mean(xv * xv, axis=-1, keepdims=True) + EPS)
        hv = (xv * r * g_ref[...]).astype(BF16)
        hs_ref[r0:r0 + rc, :] = hv
        h_ref[r0:r0 + rc, :] = hv


def _norm_mm(x, g, b, b_lead, *, tn, name, carry=None):
    s, d = x.shape
    slabs = b_lead is None
    n = b.shape[0] * b.shape[-1] if slabs else b.shape[-1]
    tm, tn = _blk(s, 1024), (b.shape[-1] if slabs else _blk(n, tn))
    b_index = (lambda i, j: (j, 0, 0)) if slabs else (lambda i, j: (b_lead, 0, j))

    def body(x_ref, g_ref, b_ref, h_ref, o_ref, hs_ref):
        @pl.when(pl.program_id(1) == 0)
        def _():
            _norm_rows_into(x_ref, g_ref, h_ref, hs_ref, tm)

        o_ref[...] = jnp.dot(hs_ref[...], b_ref[...], preferred_element_type=F32).astype(BF16)

    row = pl.BlockSpec((tm, d), lambda i, j: (i, 0))
    outs = _call_with_carry(body, out_shape=(jax.ShapeDtypeStruct((s, d), BF16), jax.ShapeDtypeStruct((s, n), BF16)),
                            grid=(s // tm, n // tn),
                            in_specs=[row, pl.BlockSpec((1, d), lambda i, j: (0, 0)),
                                      pl.BlockSpec((None, d, tn), b_index)],
                            out_specs=(row, pl.BlockSpec((tm, tn), lambda i, j: (i, j))),
                            scratch_shapes=[pltpu.VMEM((tm, d), BF16)], name=name,
                            semantics=("parallel", "arbitrary"), operands=[x, g, b], carry=carry)
    return outs if carry is None else (outs[0], outs[1], outs[2:])


def _norm_ffn_in(x, g, b, b_lead, name, carry=None):
    s, d = x.shape
    tm, tn = _blk(s, 512), FFN_TILE
    nj = FFN_HIDDEN // tn
    if b_lead is None:
        assert b.shape == (2 * nj, d, tn)
        gate_index, up_index = (lambda i, j: (j, 0, 0)), (lambda i, j: (j + nj, 0, 0))
    else:
        gate_index, up_index = (lambda i, j: (b_lead, 0, j)), (lambda i, j: (b_lead, 0, j + nj))

    def body(x_ref, g_ref, bg_ref, bu_ref, h_ref, gu_ref, act_ref, hs_ref):
        @pl.when(pl.program_id(1) == 0)
        def _():
            _norm_rows_into(x_ref, g_ref, h_ref, hs_ref, tm)

        hv = hs_ref[...]
        gate = jnp.dot(hv, bg_ref[...], preferred_element_type=F32)
        up = jnp.dot(hv, bu_ref[...], preferred_element_type=F32)
        gu_ref[0] = gate.astype(BF16)
        gu_ref[1] = up.astype(BF16)
        act_ref[...] = (gate * _sigmoid(gate) * up).astype(BF16)

    row = pl.BlockSpec((tm, d), lambda i, j: (i, 0))
    out_shape = (jax.ShapeDtypeStruct((s, d), BF16), jax.ShapeDtypeStruct((2, s, FFN_HIDDEN), BF16),
                 jax.ShapeDtypeStruct((s, FFN_HIDDEN), BF16))
    outs = _call_with_carry(body, out_shape=out_shape, grid=(s // tm, nj),
                            in_specs=[row, pl.BlockSpec((1, d), lambda i, j: (0, 0)),
                                      pl.BlockSpec((None, d, tn), gate_index), pl.BlockSpec((None, d, tn), up_index)],
                            out_specs=(row, pl.BlockSpec((2, tm, tn), lambda i, j: (0, i, j)),
                                       pl.BlockSpec((tm, tn), lambda i, j: (i, j))),
                            scratch_shapes=[pltpu.VMEM((tm, d), BF16)], name=name,
                            semantics=("parallel", "arbitrary"), operands=[x, g, b, b], carry=carry)
    return outs if carry is None else (outs[0], outs[1], outs[2], outs[3:])


def _d_act_swiglu(dx, w, b_lead, gu, name):
    s, d = dx.shape
    tm, tn = _blk(s, 512), FFN_TILE

    def body(dx_ref, w_ref, gu_ref, o_ref):
        dact = lax.dot_general(dx_ref[...].astype(BF16), w_ref[...], NT_DIMS, preferred_element_type=F32)
        gate = gu_ref[0].astype(F32)
        up = gu_ref[1].astype(F32)
        sg = _sigmoid(gate)
        o_ref[0] = (dact * up * sg * (1.0 + gate * (1.0 - sg))).astype(BF16)
        o_ref[1] = (dact * gate * sg).astype(BF16)

    halves = pl.BlockSpec((2, tm, tn), lambda j, i: (0, i, j))
    return pl.pallas_call(body, out_shape=jax.ShapeDtypeStruct(gu.shape, BF16), grid=(FFN_HIDDEN // tn, s // tm),
                          in_specs=[pl.BlockSpec((tm, d), lambda j, i: (i, 0)),
                                    pl.BlockSpec((None, tn, d), lambda j, i: (b_lead, j, 0)), halves],
                          out_specs=halves, name=name,
                          compiler_params=_params(("parallel", "parallel")))(dx, w, gu)


def _mm_nt_norm_bwd(a, b, b_lead, x, g, dres, *, tm, name, a_halves=False, carry=None):
    s, d = x.shape
    slabs = b_lead is None
    nslab, per = (b.shape[0], b.shape[-1]) if slabs else (1, b.shape[-1])
    kdim = nslab * per
    tm = _blk(s, tm)
    rc = min(ROW_CHUNK, tm)
    if a_halves:
        a_spec = pl.BlockSpec((2, tm, kdim // 2), lambda i: (0, i, 0))
    else:
        a_spec = pl.BlockSpec((tm, kdim), lambda i: (i, 0))

    def body(a_ref, b_ref, x_ref, g_ref, r_ref, dx_ref, dg_ref, *scratch):
        i = pl.program_id(0)
        if slabs:
            w_ref, sems = scratch

            @pl.when(i == 0)
            def _():
                copies = [pltpu.make_async_copy(b_ref.at[k], w_ref.at[:, pl.ds(k * per, per)], sems.at[k])
                          for k in range(nslab)]
                for cp in copies:
                    cp.start()
                for cp in copies:
                    cp.wait()
        else:
            w_ref = b_ref
        av = jnp.concatenate([a_ref[0], a_ref[1]], axis=1) if a_halves else a_ref[...]
        dh = lax.dot_general(av.astype(BF16), w_ref[...], NT_DIMS, preferred_element_type=F32)
        gv = g_ref[...]
        dg_part = None
        for r0 in range(0, tm, rc):
            xv = x_ref[r0:r0 + rc, :]
            dhv = dh[r0:r0 + rc, :]
            r = lax.rsqrt(jnp.mean(xv * xv, axis=-1, keepdims=True) + EPS)
            xhat = xv * r
            p = jnp.sum(dhv * xhat, axis=0, keepdims=True)
            dg_part = p if dg_part is None else dg_part + p
            dxhat = dhv * gv
            dx_ref[r0:r0 + rc, :] = r_ref[r0:r0 + rc, :] + r * (
                dxhat - xhat * jnp.mean(dxhat * xhat, axis=-1, keepdims=True))

        @pl.when(i == 0)
        def _():
            dg_ref[...] = dg_part

        @pl.when(i > 0)
        def _():
            dg_ref[...] += dg_part

    row = pl.BlockSpec((tm, d), lambda i: (i, 0))
    vec = pl.BlockSpec((1, d), lambda i: (0, 0))
    if slabs:
        b_spec, scratch = ANY, [pltpu.VMEM((d, kdim), BF16), pltpu.SemaphoreType.DMA((nslab,))]
    else:
        b_spec, scratch = pl.BlockSpec((None, d, kdim), lambda i: (b_lead, 0, 0), pipeline_mode=pl.Buffered(1)), []
    outs = _call_with_carry(body, out_shape=(jax.ShapeDtypeStruct((s, d), F32), jax.ShapeDtypeStruct((1, d), F32)),
                            grid=(s // tm,), in_specs=[a_spec, b_spec, row, vec, row], out_specs=(row, vec),
                            scratch_shapes=scratch, name=name, semantics=("arbitrary",), operands=[a, b, x, g, dres],
                            carry=carry)
    return outs if carry is None else (outs[0], outs[1], outs[2:])


def _rms_fwd(x, g, name):
    s, d = x.shape
    tm = _blk(s, 512)

    def body(x_ref, g_ref, o_ref):
        xv = x_ref[...]
        r = lax.rsqrt(jnp.mean(xv * xv, axis=-1, keepdims=True) + EPS)
        o_ref[...] = (xv * r * g_ref[...]).astype(BF16)

    return pl.pallas_call(body, out_shape=jax.ShapeDtypeStruct((s, d), BF16), grid=(s // tm,),
                          in_specs=[pl.BlockSpec((tm, d), lambda i: (i, 0)), pl.BlockSpec((1, d), lambda i: (0, 0))],
                          out_specs=pl.BlockSpec((tm, d), lambda i: (i, 0)), name=name,
                          compiler_params=_params(("parallel",)))(x, g)


def _rms_bwd(x, g, dh, dres, name):
    s, d = x.shape
    tm = _blk(s, 512)
    has_res = dres is not None

    def body(*refs):
        x_ref, g_ref, dh_ref = refs[0], refs[1], refs[2]
        dx_ref, dg_ref = refs[-2], refs[-1]
        xv = x_ref[...]
        r = lax.rsqrt(jnp.mean(xv * xv, axis=-1, keepdims=True) + EPS)
        xhat = xv * r
        dhv = dh_ref[...].astype(F32)
        part = jnp.sum(dhv * xhat, axis=0, keepdims=True)

        @pl.when(pl.program_id(0) == 0)
        def _():
            dg_ref[...] = part

        @pl.when(pl.program_id(0) > 0)
        def _():
            dg_ref[...] += part

        dxhat = dhv * g_ref[...]
        dx = r * (dxhat - xhat * jnp.mean(dxhat * xhat, axis=-1, keepdims=True))
        if has_res:
            dx = dx + refs[3][...]
        dx_ref[...] = dx

    row = pl.BlockSpec((tm, d), lambda i: (i, 0))
    vec = pl.BlockSpec((1, d), lambda i: (0, 0))
    in_specs = [row, vec, row] + ([row] if has_res else [])
    operands = [x, g, dh] + ([dres] if has_res else [])
    return pl.pallas_call(body, out_shape=(jax.ShapeDtypeStruct((s, d), F32), jax.ShapeDtypeStruct((1, d), F32)),
                          grid=(s // tm,), in_specs=in_specs, out_specs=(row, vec), name=name,
                          compiler_params=_params(("arbitrary",)))(*operands)


def _final_loss(x, g, target, name):
    s, d = x.shape
    tm = _blk(s, 512)

    def body(x_ref, g_ref, t_ref, loss_ref, dx_ref, dg_ref):
        xv = x_ref[...]
        gv = g_ref[...]
        r = lax.rsqrt(jnp.mean(xv * xv, axis=-1, keepdims=True) + EPS)
        xhat = xv * r
        err = xhat * gv - t_ref[...]
        lpart = 0.5 * jnp.sum(jnp.mean(err * err, axis=-1, keepdims=True), axis=0, keepdims=True)
        dy = err * (1.0 / d)
        gpart = jnp.sum(dy * xhat, axis=0, keepdims=True)

        @pl.when(pl.program_id(0) == 0)
        def _():
            loss_ref[...] = lpart
            dg_ref[...] = gpart

        @pl.when(pl.program_id(0) > 0)
        def _():
            loss_ref[...] += lpart
            dg_ref[...] += gpart

        dxhat = dy * gv
        dx_ref[...] = r * (dxhat - xhat * jnp.mean(dxhat * xhat, axis=-1, keepdims=True))

    row = pl.BlockSpec((tm, d), lambda i: (i, 0))
    vec = pl.BlockSpec((1, d), lambda i: (0, 0))
    one = pl.BlockSpec((1, 1), lambda i: (0, 0))
    return pl.pallas_call(body, out_shape=(jax.ShapeDtypeStruct((1, 1), F32), jax.ShapeDtypeStruct((s, d), F32),
                                           jax.ShapeDtypeStruct((1, d), F32)),
                          grid=(s // tm,), in_specs=[row, vec, row], out_specs=(one, row, vec), name=name,
                          compiler_params=_params(("arbitrary",)))(x, g, target)


def _erf_parts(x):
    cdf = 0.5 * (1.0 + lax.erf(x * (1.0 / math.sqrt(2.0))))
    return cdf


def _shift_down(ext, k):
    return pltpu.roll(ext, k, 0)


def _shift_up(ext, k):
    return pltpu.roll(ext, ext.shape[0] - k, 0)


def _mixer_forward_math(z, halo, row0, prm, tm):
    ln_g, ln_b, wtril, bsb, conv_w, wg, scale = prm
    za = z[:, 0:2 * A_WIDTH]
    gb = z[:, 1024:1536]
    gc = z[:, 1536:2048]
    xin = z[:, 2048:2560]
    zc = z[:, 2560:3072]
    cdf = _erf_parts(za)
    act = za * cdf
    u = act[:, :A_WIDTH]
    v = act[:, A_WIDTH:]
    mu = jnp.mean(v, axis=-1, keepdims=True)
    vc = v - mu
    rstd = lax.rsqrt(jnp.mean(vc * vc, axis=-1, keepdims=True) + EPS)
    vhat = vc * rstd
    vn = (vhat * ln_g + ln_b).astype(BF16)
    rows = []
    for c in range(tm // CHUNK):
        cols = []
        for g in range(A_GROUPS):
            blk = vn[c * CHUNK:(c + 1) * CHUNK, g * CHUNK:(g + 1) * CHUNK]
            cols.append(jnp.dot(wtril[g], blk, preferred_element_type=F32) + bsb[g])
        rows.append(jnp.concatenate(cols, axis=1))
    mixed = jnp.concatenate(rows, axis=0)
    ya = u * mixed
    y = gc * xin
    yext = jnp.concatenate([halo[:, 0:512] * halo[:, 512:1024], y], axis=0)
    y1 = _shift_down(yext, 1)[HALO:]
    y2 = _shift_down(yext, 2)[HALO:]
    conv = conv_w[0:1, :] * y2 + conv_w[1:2, :] * y1 + conv_w[2:3, :] * y
    yb = gb * conv
    t = row0 + lax.broadcasted_iota(jnp.int32, (tm, 1), 0)
    zext = jnp.concatenate([halo[:, 1024:1536], zc], axis=0)
    pooled, p = [], []
    for gi, win in enumerate(POOL_WINDOWS):
        sw = zext[:, gi * C_GROUP:(gi + 1) * C_GROUP]
        for step in range(gi + 1):
            sw = sw + _shift_down(sw, 2 ** step)
        cnt = jnp.minimum(t + 1, win).astype(F32)
        pg = sw[HALO:] / cnt - zc[:, gi * C_GROUP:(gi + 1) * C_GROUP]
        pooled.append(pg)
        p.append(jnp.dot(pg.astype(BF16), wg[gi], preferred_element_type=F32))
    p = jnp.concatenate(p, axis=1)
    yc = p * scale
    saved = dict(za=za, cdf=cdf, u=u, mixed=mixed, rstd=rstd, vhat=vhat, vn=vn, gb=gb, gc=gc, xin=xin, y=y, y1=y1, y2=y2,
                 conv=conv, pooled=pooled, p=p, t=t)
    return ya, yb, yc, saved


def _mixer_specs(tm, s):
    nb16 = tm // HALO
    main = pl.BlockSpec((tm, MIX_COLS), lambda i: (i, 0))
    prev = pl.BlockSpec((HALO, 1536), lambda i: (jnp.maximum(i * nb16 - 1, 0), 1))
    return main, prev


def _full(shape):
    n = len(shape)
    return pl.BlockSpec(shape, lambda i: (0,) * n)


def _mixer_fwd(proj, prm, name):
    s = proj.shape[0]
    tm = _blk(s, 256)
    main, prev = _mixer_specs(tm, s)

    def body(z_ref, h_ref, lng, lnb, wt, bsb, cw, wg, sc, o_ref):
        i = pl.program_id(0)
        halo = jnp.where(i > 0, h_ref[...].astype(F32), 0.0)
        prm_v = (lng[...], lnb[...], wt[...], bsb[...], cw[...], wg[...], sc[...])
        ya, yb, yc, _ = _mixer_forward_math(z_ref[...].astype(F32), halo, i * tm, prm_v, tm)
        o_ref[...] = jnp.concatenate([ya, yb, yc], axis=1).astype(BF16)

    return pl.pallas_call(body, out_shape=jax.ShapeDtypeStruct((s, 1536), BF16), grid=(s // tm,),
                          in_specs=[main, prev] + [_full(p.shape) for p in prm],
                          out_specs=pl.BlockSpec((tm, 1536), lambda i: (i, 0)), name=name,
                          compiler_params=_params(("parallel",)))(proj, proj, *prm)


def _mixer_bwd(proj, dypre, dproj_in, prm, wtril_t, name, carry=None):
    s = proj.shape[0]
    tm = _blk(s, 256)
    nblk = s // tm
    nb16 = tm // HALO
    main, prev = _mixer_specs(tm, s)
    nxt_row = lambda i: jnp.minimum((i + 1) * nb16, s // HALO - 1)
    next_dy = pl.BlockSpec((HALO, 1536), lambda i: (nxt_row(i), 0))
    next_gb = pl.BlockSpec((HALO, 512), lambda i: (nxt_row(i), 2))
    ones8 = jnp.ones((8, CHUNK), F32)

    def body(z_ref, h_ref, dy_ref, ndy_ref, ngb_ref, _alias, lng, lnb, wt, bsb, cw, wg, sc, wtt, ones_ref,
             dz_ref, dlng_ref, dlnb_ref, dws_ref, dbs_ref, dcw_ref, dwg_ref, dsc_ref):
        i = pl.program_id(0)
        first = i == 0
        halo = jnp.where(i > 0, h_ref[...].astype(F32), 0.0)
        prm_v = (lng[...], lnb[...], wt[...], bsb[...], cw[...], wg[...], sc[...])
        _, _, _, sv = _mixer_forward_math(z_ref[...].astype(F32), halo, i * tm, prm_v, tm)
        dy = dy_ref[...].astype(F32)
        dya, dyb, dyc = dy[:, 0:512], dy[:, 512:1024], dy[:, 1024:1536]
        not_last = i < nblk - 1
        ndy = jnp.where(not_last, ndy_ref[...].astype(F32), 0.0)
        ngb = ngb_ref[...].astype(F32)

        def accumulate(ref, val):
            @pl.when(first)
            def _():
                ref[...] = val

            @pl.when(jnp.logical_not(first))
            def _():
                ref[...] += val

        du = dya * sv["mixed"]
        dmixed = dya * sv["u"]
        vn = sv["vn"]
        wttv = wtt[...]
        dvn_rows = []
        dws_parts = [None] * A_GROUPS
        dbs_parts = [None] * A_GROUPS
        for c in range(tm // CHUNK):
            cols = []
            for g in range(A_GROUPS):
                dm = dmixed[c * CHUNK:(c + 1) * CHUNK, g * CHUNK:(g + 1) * CHUNK]
                dmb = dm.astype(BF16)
                vb = vn[c * CHUNK:(c + 1) * CHUNK, g * CHUNK:(g + 1) * CHUNK]
                w_part = lax.dot_general(dmb, vb, (((1,), (1,)), ((), ())), preferred_element_type=F32)
                b_part = lax.dot_general(ones_ref[...], dm, (((1,), (1,)), ((), ())), preferred_element_type=F32,
                                         precision=lax.Precision.HIGHEST)[0:1, :]
                dws_parts[g] = w_part if dws_parts[g] is None else dws_parts[g] + w_part
                dbs_parts[g] = b_part if dbs_parts[g] is None else dbs_parts[g] + b_part
                cols.append(jnp.dot(wttv[g], dmb, preferred_element_type=F32))
            dvn_rows.append(jnp.concatenate(cols, axis=1))
        dvn = jnp.concatenate(dvn_rows, axis=0)
        tri = lax.broadcasted_iota(jnp.int32, (CHUNK, CHUNK), 0) >= lax.broadcasted_iota(jnp.int32, (CHUNK, CHUNK), 1)
        accumulate(dws_ref, jnp.stack([jnp.where(tri, w, 0.0) for w in dws_parts], axis=0))
        accumulate(dbs_ref, jnp.concatenate(dbs_parts, axis=0))
        vhat = sv["vhat"]
        accumulate(dlng_ref, jnp.sum(dvn * vhat, axis=0, keepdims=True))
        accumulate(dlnb_ref, jnp.sum(dvn, axis=0, keepdims=True))
        dvhat = dvn * lng[...]
        dv = sv["rstd"] * (dvhat - jnp.mean(dvhat, axis=-1, keepdims=True)
                           - vhat * jnp.mean(dvhat * vhat, axis=-1, keepdims=True))
        za = sv["za"]
        dgelu = sv["cdf"] + za * jnp.exp(-0.5 * za * za) * (1.0 / math.sqrt(2.0 * math.pi))
        dza = jnp.concatenate([du, dv], axis=1) * dgelu
        cwv = cw[...]
        dconv = dyb * sv["gb"]
        dgb = dyb * sv["conv"]
        dcext = jnp.concatenate([dconv, ndy[:, 512:1024] * ngb], axis=0)
        d1 = _shift_up(dcext, 1)[:tm]
        d2 = _shift_up(dcext, 2)[:tm]
        dyy = cwv[2:3, :] * dconv + cwv[1:2, :] * d1 + cwv[0:1, :] * d2
        dgc = dyy * sv["xin"]
        dxin = dyy * sv["gc"]
        accumulate(dcw_ref, jnp.concatenate([jnp.sum(dconv * sv["y2"], axis=0, keepdims=True),
                                             jnp.sum(dconv * sv["y1"], axis=0, keepdims=True),
                                             jnp.sum(dconv * sv["y"], axis=0, keepdims=True)], axis=0))
        scv = sc[...]
        accumulate(dsc_ref, jnp.sum(dyc * sv["p"], axis=0, keepdims=True))
        dp = dyc * scv
        ndp = ndy[:, 1024:1536] * scv
        wgv = wg[...]
        dzc_cols, dwg_parts = [], []
        for gi, win in enumerate(POOL_WINDOWS):
            sl = slice(gi * C_GROUP, (gi + 1) * C_GROUP)
            dpb = dp[:, sl].astype(BF16)
            dpool = lax.dot_general(dpb, wgv[gi], (((1,), (1,)), ((), ())), preferred_element_type=F32)
            ndpool = lax.dot_general(ndp[:, sl].astype(BF16), wgv[gi], (((1,), (1,)), ((), ())),
                                     preferred_element_type=F32)
            dwg_parts.append(lax.dot_general(sv["pooled"][gi].astype(BF16), dpb, (((0,), (0,)), ((), ())),
                                             preferred_element_type=F32))
            cnt = jnp.minimum(sv["t"] + 1, win).astype(F32)
            fw = jnp.concatenate([dpool / cnt, ndpool * (1.0 / win)], axis=0)
            for step in range(gi + 1):
                fw = fw + _shift_up(fw, 2 ** step)
            dzc_cols.append(fw[:tm] - dpool)
        accumulate(dwg_ref, jnp.stack(dwg_parts, axis=0))
        dzc = jnp.concatenate(dzc_cols, axis=1)
        dz_ref[...] = jnp.concatenate([dza, dgb, dgc, dxin, dzc], axis=1).astype(BF16)

    out_shape = (jax.ShapeDtypeStruct(dproj_in.shape, BF16), jax.ShapeDtypeStruct((1, A_WIDTH), F32),
                 jax.ShapeDtypeStruct((1, A_WIDTH), F32), jax.ShapeDtypeStruct((A_GROUPS, CHUNK, CHUNK), F32),
                 jax.ShapeDtypeStruct((A_GROUPS, CHUNK), F32), jax.ShapeDtypeStruct((3, B_WIDTH), F32),
                 jax.ShapeDtypeStruct((4, C_GROUP, C_GROUP), F32), jax.ShapeDtypeStruct((1, C_WIDTH), F32))
    out_specs = (main,) + tuple(_full(o.shape) for o in out_shape[1:])
    in_specs = [main, prev, pl.BlockSpec((tm, 1536), lambda i: (i, 0)), next_dy, next_gb, ANY] + \
               [_full(p.shape) for p in prm] + [_full(wtril_t.shape), _full(ones8.shape)]
    outs = _call_with_carry(body, out_shape=out_shape, grid=(nblk,), in_specs=in_specs, out_specs=out_specs,
                            scratch_shapes=[], name=name, semantics=("arbitrary",),
                            operands=[proj, proj, dypre, dypre, proj, dproj_in, *prm, wtril_t, ones8], carry=carry,
                            aliases={5: 0})
    return tuple(outs) if carry is None else tuple(outs[:8]) + (outs[8:],)


def _sigmoid(x):
    return 1.0 / (1.0 + jnp.exp(-x))


def _merge_fwd(ypre, proj, wa, wb, wc, bgate, name):
    s = ypre.shape[0]
    tm = _blk(s, 512)

    def body(y_ref, zg_ref, wa_ref, wb_ref, wc_ref, bg_ref, o_ref):
        yv = y_ref[...]
        acc = None
        for i, w_ref in enumerate((wa_ref, wb_ref, wc_ref)):
            br = jnp.dot(yv[:, i * 512:(i + 1) * 512], w_ref[...], preferred_element_type=F32)
            gate = _sigmoid(zg_ref[:, i * D_MODEL:(i + 1) * D_MODEL].astype(F32) + bg_ref[i:i + 1, :])
            acc = gate * br if acc is None else acc + gate * br
        o_ref[...] = acc.astype(BF16)

    wspec = _full(wa.shape)
    return pl.pallas_call(body, out_shape=jax.ShapeDtypeStruct((s, D_MODEL), BF16), grid=(s // tm,),
                          in_specs=[pl.BlockSpec((tm, 1536), lambda i: (i, 0)),
                                    pl.BlockSpec((tm, 3 * D_MODEL), lambda i: (i, 1)), wspec, wspec, wspec,
                                    _full(bgate.shape)],
                          out_specs=pl.BlockSpec((tm, D_MODEL), lambda i: (i, 0)), name=name,
                          compiler_params=_params(("parallel",)))(ypre, proj, wa, wb, wc, bgate)


def _merge_bwd(ypre, proj, dmerged, wa, wb, wc, bgate, name, carry=None):
    s = ypre.shape[0]
    tm = _blk(s, 512)

    def body(y_ref, zg_ref, dm_ref, wa_ref, wb_ref, wc_ref, bg_ref, dyp_ref, dzg_ref, dbg_ref, dwa_ref, dwb_ref,
             dwc_ref):
        first = pl.program_id(0) == 0

        def accumulate(ref, val):
            @pl.when(first)
            def _():
                ref[...] = val

            @pl.when(jnp.logical_not(first))
            def _():
                ref[...] += val

        yv = y_ref[...]
        dm = dm_ref[...].astype(F32)
        dyp, dzg, dbg = [], [], []
        for i, (w_ref, dw_ref) in enumerate(((wa_ref, dwa_ref), (wb_ref, dwb_ref), (wc_ref, dwc_ref))):
            wv = w_ref[...]
            yi = yv[:, i * 512:(i + 1) * 512]
            br = jnp.dot(yi, wv, preferred_element_type=F32)
            gate = _sigmoid(zg_ref[:, i * D_MODEL:(i + 1) * D_MODEL].astype(F32) + bg_ref[i:i + 1, :])
            dbi = (dm * gate).astype(BF16)
            dzi = dm * br * gate * (1.0 - gate)
            dzg.append(dzi.astype(BF16))
            dbg.append(jnp.sum(dzi, axis=0, keepdims=True))
            dyp.append(lax.dot_general(dbi, wv, (((1,), (1,)), ((), ())), preferred_element_type=F32).astype(BF16))
            accumulate(dw_ref, lax.dot_general(yi, dbi, (((0,), (0,)), ((), ())), preferred_element_type=F32))
        dyp_ref[...] = jnp.concatenate(dyp, axis=1)
        dzg_ref[...] = jnp.concatenate(dzg, axis=1)
        accumulate(dbg_ref, jnp.concatenate(dbg, axis=0))

    wspec = _full(wa.shape)
    dw_shape = jax.ShapeDtypeStruct(wa.shape, F32)
    out_shape = (jax.ShapeDtypeStruct((s, 1536), BF16), jax.ShapeDtypeStruct((s, IN_COLS), BF16),
                 jax.ShapeDtypeStruct((3, D_MODEL), F32), dw_shape, dw_shape, dw_shape)
    outs = _call_with_carry(body, out_shape=out_shape, grid=(s // tm,),
                            in_specs=[pl.BlockSpec((tm, 1536), lambda i: (i, 0)),
                                      pl.BlockSpec((tm, 3 * D_MODEL), lambda i: (i, 1)),
                                      pl.BlockSpec((tm, D_MODEL), lambda i: (i, 0)), wspec, wspec, wspec,
                                      _full(bgate.shape)],
                            out_specs=(pl.BlockSpec((tm, 1536), lambda i: (i, 0)),
                                       pl.BlockSpec((tm, 3 * D_MODEL), lambda i: (i, 1)), _full((3, D_MODEL)),
                                       wspec, wspec, wspec),
                            scratch_shapes=[], name=name, semantics=("arbitrary",),
                            operands=[ypre, proj, dmerged, wa, wb, wc, bgate], carry=carry)
    return tuple(outs) if carry is None else tuple(outs[:6]) + (outs[6:],)


def _softmax_rows(q, k):
    sc = lax.dot_general(q, k, (((1,), (1,)), ((), ())), preferred_element_type=F32) * (HEAD_DIM ** -0.5)
    e = jnp.exp(sc - jnp.max(sc, axis=-1, keepdims=True))
    return e / jnp.sum(e, axis=-1, keepdims=True)


def _attn_fwd(q, kv, name):
    s = q.shape[0]
    tm = _blk(s, 512)

    def body(q_ref, kv_ref, o_ref):
        outs = []
        for h in range(XATTN_HEADS):
            sl = slice(h * HEAD_DIM, (h + 1) * HEAD_DIM)
            p = _softmax_rows(q_ref[:, sl], kv_ref[:, sl])
            outs.append(jnp.dot(p.astype(BF16), kv_ref[:, D_MODEL + h * HEAD_DIM:D_MODEL + (h + 1) * HEAD_DIM],
                                preferred_element_type=F32))
        o_ref[...] = jnp.concatenate(outs, axis=1).astype(BF16)

    return pl.pallas_call(body, out_shape=jax.ShapeDtypeStruct((s, D_MODEL), BF16), grid=(s // tm,),
                          in_specs=[pl.BlockSpec((tm, D_MODEL), lambda i: (i, 0)), _full(kv.shape)],
                          out_specs=pl.BlockSpec((tm, D_MODEL), lambda i: (i, 0)), name=name,
                          compiler_params=_params(("parallel",)))(q, kv)


def _attn_bwd(q, kv, do, name):
    s = q.shape[0]
    tm = _blk(s, 512)

    def body(q_ref, kv_ref, do_ref, dq_ref, dkv_ref):
        dqs, dks, dvs = [], [], []
        for h in range(XATTN_HEADS):
            sl = slice(h * HEAD_DIM, (h + 1) * HEAD_DIM)
            vsl = slice(D_MODEL + h * HEAD_DIM, D_MODEL + (h + 1) * HEAD_DIM)
            qh, kh, vh, doh = q_ref[:, sl], kv_ref[:, sl], kv_ref[:, vsl], do_ref[:, sl]
            p = _softmax_rows(qh, kh)
            pb = p.astype(BF16)
            dvs.append(lax.dot_general(pb, doh, (((0,), (0,)), ((), ())), preferred_element_type=F32))
            dp = lax.dot_general(doh, vh, (((1,), (1,)), ((), ())), preferred_element_type=F32)
            ds = p * (dp - jnp.sum(dp * p, axis=-1, keepdims=True)) * (HEAD_DIM ** -0.5)
            dsb = ds.astype(BF16)
            dqs.append(jnp.dot(dsb, kh, preferred_element_type=F32))
            dks.append(lax.dot_general(dsb, qh, (((0,), (0,)), ((), ())), preferred_element_type=F32))
        dq_ref[...] = jnp.concatenate(dqs, axis=1).astype(BF16)
        part = jnp.concatenate(dks + dvs, axis=1)

        @pl.when(pl.program_id(0) == 0)
        def _():
            dkv_ref[...] = part

        @pl.when(pl.program_id(0) > 0)
        def _():
            dkv_ref[...] += part

    row = pl.BlockSpec((tm, D_MODEL), lambda i: (i, 0))
    return pl.pallas_call(body, out_shape=(jax.ShapeDtypeStruct((s, D_MODEL), BF16),
                                           jax.ShapeDtypeStruct(kv.shape, F32)),
                          grid=(s // tm,), in_specs=[row, _full(kv.shape), row], out_specs=(row, _full(kv.shape)),
                          name=name, compiler_params=_params(("arbitrary",)))(q, kv, do)


def _rows_block(rows, cols):
    target = (512 * 1024) // cols
    fits = [cand for cand in range(8, rows + 1, 8) if rows % cand == 0 and cand <= target]
    return fits[-1] if fits else rows


def _elementwise(fn, ins, out_dtypes, name):
    lead, rows, cols = ins[0].shape
    tr = _rows_block(rows, cols)
    spec = pl.BlockSpec((None, tr, cols), lambda l, i: (l, i, 0))
    n_in = len(ins)

    def body(*refs):
        for o_ref, o in zip(refs[n_in:], fn(*[r[...] for r in refs[:n_in]])):
            o_ref[...] = o.astype(o_ref.dtype)

    out_shape = tuple(jax.ShapeDtypeStruct((lead, rows, cols), dt) for dt in out_dtypes)
    return pl.pallas_call(body, out_shape=out_shape, grid=(lead, rows // tr), in_specs=[spec] * n_in,
                          out_specs=tuple([spec] * len(out_dtypes)), name=name,
                          compiler_params=_params(("parallel", "parallel")))(*ins)


def _cast_to_slots(w, chip, name):
    _, rows, cols = w.shape
    tr = _rows_block(rows, cols)

    def body(chip_ref, w_ref, o0_ref, o1_ref):
        o0_ref[...] = w_ref[0].astype(BF16)
        o1_ref[...] = w_ref[1].astype(BF16)

    slot = pl.BlockSpec((None, tr, cols), lambda i, chip_ref: (chip_ref[0], i, 0))
    grid_spec = pltpu.PrefetchScalarGridSpec(
        num_scalar_prefetch=1, grid=(rows // tr,),
        in_specs=[pl.BlockSpec((DEPTH, tr, cols), lambda i, chip_ref: (0, i, 0))], out_specs=(slot, slot))
    shape = jax.ShapeDtypeStruct((N_CHIPS, rows, cols), BF16)
    return pl.pallas_call(body, out_shape=(shape, shape), grid_spec=grid_spec, name=name,
                          compiler_params=_params(("parallel",)))(chip, w)


def _chip_sum(pair0, rx0, pair1, rx1, chip, name):
    _, rows, cols = pair0.shape
    tr = _rows_block(rows, cols)

    def body(chip_ref, p0_ref, rx0_ref, p1_ref, rx1_ref, o_ref):
        for layer, (p_ref, rx_ref) in enumerate(((p0_ref, rx0_ref), (p1_ref, rx1_ref))):
            acc = p_ref[...].astype(F32)
            for j in range(3):
                acc = acc + rx_ref[j].astype(F32)
            o_ref[layer] = acc

    mine = pl.BlockSpec((None, tr, cols), lambda i, chip_ref: (chip_ref[0], i, 0))
    theirs = pl.BlockSpec((3, tr, cols), lambda i, chip_ref: (0, i, 0))
    grid_spec = pltpu.PrefetchScalarGridSpec(
        num_scalar_prefetch=1, grid=(rows // tr,), in_specs=[mine, theirs, mine, theirs],
        out_specs=pl.BlockSpec((DEPTH, tr, cols), lambda i, chip_ref: (0, i, 0)))
    return pl.pallas_call(body, out_shape=jax.ShapeDtypeStruct((DEPTH, rows, cols), F32), grid_spec=grid_spec,
                          name=name, compiler_params=_params(("parallel",)))(chip, pair0, rx0, pair1, rx1)


def _chip_sum_ordered(own, rx, chip, name):
    rows, cols = own.shape

    def body(chip_ref, own_ref, rx_ref, o_ref):
        me = chip_ref[0]
        acc = None
        for k in range(N_CHIPS):
            rel = jnp.bitwise_xor(me, k)
            term = jnp.where(rel == 0, own_ref[...],
                             jnp.where(rel == 2, rx_ref[0], jnp.where(rel == 1, rx_ref[1], rx_ref[2])))
            acc = term if acc is None else acc + term
        o_ref[...] = acc

    grid_spec = pltpu.PrefetchScalarGridSpec(
        num_scalar_prefetch=1, grid=(1,),
        in_specs=[pl.BlockSpec((rows, cols), lambda i, chip_ref: (0, 0)),
                  pl.BlockSpec((3, rows, cols), lambda i, chip_ref: (0, 0, 0))],
        out_specs=pl.BlockSpec((rows, cols), lambda i, chip_ref: (0, 0)))
    return pl.pallas_call(body, out_shape=jax.ShapeDtypeStruct((rows, cols), F32), grid_spec=grid_spec, name=name,
                          compiler_params=_params(("arbitrary",)))(chip, own, rx)


def _adamw_math(w, g, m, v):
    m = ADAM_B1 * m + (1.0 - ADAM_B1) * g
    v = ADAM_B2 * v + (1.0 - ADAM_B2) * (g * g)
    m_hat = m / (1.0 - ADAM_B1 ** ADAM_STEP)
    v_hat = v / (1.0 - ADAM_B2 ** ADAM_STEP)
    delta = -ADAM_LR * (m_hat / (jnp.sqrt(v_hat) + ADAM_EPS) + ADAM_WD * w)
    return delta, m, v


def _adamw_small(ws, gs, ms, vs, name):
    n = len(ws)

    def body(*refs):
        for i in range(n):
            w_ref, g_ref, m_ref, v_ref = (refs[k * n + i] for k in range(4))
            d_out, m_out, v_out = (refs[(4 + k) * n + i] for k in range(3))
            d_out[...], m_out[...], v_out[...] = _adamw_math(w_ref[...], g_ref[...], m_ref[...], v_ref[...])

    vmem = pl.BlockSpec(memory_space=pltpu.VMEM)
    shapes = tuple(jax.ShapeDtypeStruct(w.shape, F32) for w in ws)
    outs = pl.pallas_call(body, out_shape=shapes * 3, in_specs=[vmem] * (4 * n), out_specs=tuple([vmem] * (3 * n)),
                          name=name, compiler_params=_params())(*ws, *gs, *ms, *vs)
    return outs[:n], outs[n:2 * n], outs[2 * n:]


def _adamw_shard(w, mine, other, m, v, core, name):
    lead, rows, cols = w.shape
    half = rows // 2
    tr = _rows_block(half, cols)
    nhalf = half // tr

    def body(core_ref, w_ref, mine_ref, other_ref, m_ref, v_ref, g_out, d_out, m_out, v_out):
        g = jnp.where(pl.program_id(1) // nhalf == core_ref[0], mine_ref[...], other_ref[...])
        d_new, m_new, v_new = _adamw_math(w_ref[...], g, m_ref[...], v_ref[...])
        g_out[...] = g
        d_out[...] = d_new
        m_out[...] = m_new
        v_out[...] = v_new

    whole = pl.BlockSpec((None, tr, cols), lambda l, i, core_ref: (l, i, 0))
    own = pl.BlockSpec((None, tr, cols),
                       lambda l, i, core_ref: (l, jnp.where(i // nhalf == core_ref[0], i % nhalf, 0), 0))
    far = pl.BlockSpec((None, tr, cols),
                       lambda l, i, core_ref: (l, jnp.where(i // nhalf == core_ref[0], 0, i % nhalf), 0))
    shape = jax.ShapeDtypeStruct(w.shape, F32)
    grid_spec = pltpu.PrefetchScalarGridSpec(num_scalar_prefetch=1, grid=(lead, rows // tr),
                                             in_specs=[whole, own, far, whole, whole], out_specs=(whole,) * 4)
    return pl.pallas_call(body, out_shape=(shape,) * 4, grid_spec=grid_spec, name=name,
                          compiler_params=_params(("parallel", "parallel")))(core, w, mine, other, m, v)


def _place():
    x, y, c = lax.axis_index("x"), lax.axis_index("y"), lax.axis_index("c")
    chips = [(1 - x, y), (x, 1 - y), (1 - x, 1 - y)]
    return x, y, c, 2 * x + y, chips


def _run_remote(copies):
    for send, _ in copies:
        send.start()
    for send, recv in copies:
        send.wait_send()
        recv.wait_recv()


def _half_rows(ref, c):
    half = ref.shape[1] // 2
    return pl.ds(pl.multiple_of(c * half, 8), half)


def _gather_issue(whole):
    def issue(_, refs, ssem, rsem):
        x, y, c, me, chips = _place()
        copies = []
        for w, ref in enumerate(refs):
            part = (me,) if w >= len(refs) - whole else (me, _half_rows(ref, c))
            mine = ref.at[part]
            for j, (px, py) in enumerate(chips):
                sems = (ssem.at[3 * w + j], rsem.at[3 * w + j])
                peer = dict(device_id=(px, py, c), device_id_type=MESH)
                send = pltpu.make_async_remote_copy(mine, mine, *sems, **peer)
                recv = pltpu.make_async_remote_copy(mine, ref.at[(2 * px + py,) + part[1:]], *sems, **peer)
                copies.append((send, recv))
        return copies
    return issue


def _gather_carry(fulls, whole=0):
    return _Carry(arrays=tuple(fulls), outs=tuple(jax.ShapeDtypeStruct(f.shape, f.dtype) for f in fulls),
                  in_place=True, copies=3 * len(fulls), issue=_gather_issue(whole))


def _scatter_issue(ins, outs, ssem, rsem):
    x, y, c, me, chips = _place()
    copies = []
    for w, (src, dst) in enumerate(zip(ins, outs)):
        for j, (px, py) in enumerate(chips):
            cp = pltpu.make_async_remote_copy(src.at[2 * px + py], dst.at[j], ssem.at[3 * w + j], rsem.at[3 * w + j],
                                              device_id=(px, py, c), device_id_type=MESH)
            copies.append((cp, cp))
    return copies


def _scatter_carry(pairs):
    return _Carry(arrays=tuple(pairs), outs=tuple(jax.ShapeDtypeStruct((3,) + p.shape[1:], p.dtype) for p in pairs),
                  in_place=False, copies=3 * len(pairs), issue=_scatter_issue)


def _chip_gather(fulls, whole, name):
    n = len(fulls)
    issue = _gather_issue(whole)

    def body(*refs):
        _run_remote(issue(None, refs[n:2 * n], refs[2 * n], refs[2 * n + 1]))

    out_shape = tuple(jax.ShapeDtypeStruct(f.shape, f.dtype) for f in fulls)
    return pl.pallas_call(body, out_shape=out_shape, in_specs=[ANY] * n, out_specs=tuple([ANY] * n),
                          input_output_aliases={w: w for w in range(n)},
                          scratch_shapes=[pltpu.SemaphoreType.DMA((3 * n,)), pltpu.SemaphoreType.DMA((3 * n,))],
                          name=name)(*fulls)


class _Blocks(NamedTuple):
    tile: tuple
    dtype: object
    moves: list
    landed: object


STAGE_BYTES = 4 * 1024 * 1024


def _rows(start, size):
    return pl.ds(start if isinstance(start, int) else pl.multiple_of(start, 8), size)


def _stage_tile(lead, rows, cols, dtype):
    per_row = cols * jnp.dtype(dtype).itemsize
    if lead * rows * per_row <= STAGE_BYTES:
        return lead, rows
    fits = [cand for cand in range(8, rows + 1, 8) if rows % cand == 0 and cand * per_row <= STAGE_BYTES]
    return 1, (fits[-1] if fits else rows)


def _plan_gathered(_, refs, c, me):
    work = []
    for ref in refs:
        lead, rows, cols = ref.shape
        half = rows // 2
        nk, tr = _stage_tile(lead, half, cols, ref.dtype)
        if nk == lead:
            part = ref.at[pl.ds(0, lead), _rows(c * half, half)]
            moves, landed = [(part, part)], ref.at[pl.ds(0, lead), pl.ds(0, half)]
        else:
            moves = []
            for j in range(N_CHIPS - 1):
                k = jnp.bitwise_xor(me, j + 1)
                for b in range(half // tr):
                    part = ref.at[pl.ds(k, 1), _rows(c * half + b * tr, tr)]
                    moves.append((part, part))
            landed = ref.at[pl.ds(0, N_CHIPS - 1), pl.ds(0, half)]
        work.append(_Blocks((nk, tr, cols), ref.dtype, moves, landed))
    return work


def _plan_grad_halves(grads, rxs, c, me):
    work = []
    for g, rx in zip(grads, rxs):
        lead, half, cols = rx.shape
        nk, tr = _stage_tile(lead, half, cols, g.dtype)
        moves = [(g.at[pl.ds(k, nk), _rows((1 - c) * half + b * tr, tr)], rx.at[pl.ds(k, nk), _rows(b * tr, tr)])
                 for k in range(0, lead, nk) for b in range(half // tr)]
        work.append(_Blocks((nk, tr, cols), g.dtype, moves, rx))
    return work


def _plan_whole(srcs, dsts, c, me):
    work = []
    for src, dst in zip(srcs, dsts):
        lead, rows, cols = src.shape
        nk, tr = _stage_tile(lead, rows, cols, src.dtype)
        moves = [(src.at[pl.ds(k, nk), _rows(b * tr, tr)], dst.at[pl.ds(k, nk), _rows(b * tr, tr)])
                 for k in range(0, lead, nk) for b in range(rows // tr)]
        work.append(_Blocks((nk, tr, cols), src.dtype, moves, dst))
    return work


def _push_staged(arrays, outs, in_place, plan, name):
    n, no = len(arrays), len(outs)

    def body(*refs):
        ins, out_refs = refs[:n], refs[n:n + no]
        rsem, idle = refs[n + no], refs[n + no + 1]
        x, y, c, me, _ = _place()
        peer = dict(device_id=(x, y, 1 - c), device_id_type=MESH)
        work = plan(ins, out_refs, c, me)
        for w, blocks in enumerate(work):
            def staged(buf, lsem, ssem, w=w, blocks=blocks):
                sends = []
                for i, (src, dst) in enumerate(blocks.moves):
                    slot = i % 2
                    if i >= 2:
                        sends[i - 2].wait_send()
                    load = pltpu.make_async_copy(src, buf.at[slot], lsem.at[slot])
                    load.start()
                    load.wait()
                    send = pltpu.make_async_remote_copy(buf.at[slot], dst, ssem.at[slot], rsem.at[w], **peer)
                    send.start()
                    sends.append(send)
                for send in sends[-2:]:
                    send.wait_send()

            pl.run_scoped(staged, pltpu.VMEM((2,) + blocks.tile, blocks.dtype), pltpu.SemaphoreType.DMA((2,)),
                          pltpu.SemaphoreType.DMA((2,)))
        for w, blocks in enumerate(work):
            pltpu.make_async_remote_copy(blocks.landed, blocks.landed, idle, rsem.at[w], **peer).wait_recv()

    return pl.pallas_call(body, out_shape=tuple(outs), in_specs=[ANY] * n, out_specs=tuple([ANY] * no),
                          input_output_aliases={w: w for w in range(n)} if in_place else {},
                          scratch_shapes=[pltpu.SemaphoreType.DMA((n,)), pltpu.SemaphoreType.DMA],
                          name=name, compiler_params=_params())(*arrays)


def _pair_sum(g, rx, core, name):
    lead, half, cols = rx.shape
    tr = _rows_block(half, cols)
    nrow = half // tr

    def body(core_ref, g_ref, rx_ref, o_ref):
        o_ref[...] = (g_ref[...].astype(F32) + rx_ref[...].astype(F32)).astype(o_ref.dtype)

    spec = pl.BlockSpec((None, tr, cols), lambda k, i, core_ref: (k, i, 0))
    grid_spec = pltpu.PrefetchScalarGridSpec(
        num_scalar_prefetch=1, grid=(lead, nrow),
        in_specs=[pl.BlockSpec((None, tr, cols), lambda k, i, core_ref: (k, core_ref[0] * nrow + i, 0)), spec],
        out_specs=spec)
    return pl.pallas_call(body, out_shape=jax.ShapeDtypeStruct(rx.shape, rx.dtype), grid_spec=grid_spec, name=name,
                          compiler_params=_params(("parallel", "parallel")))(core, g, rx)


def _push_sibling(src, name):
    lead, rows, cols = src.shape
    tr = _rows_block(rows, cols)
    nrow = rows // tr

    def body(blk_ref, rx_ref, ssem, rsem):
        x, y, c = lax.axis_index("x"), lax.axis_index("y"), lax.axis_index("c")
        k, i = pl.program_id(0), pl.program_id(1)
        peer = dict(device_id=(x, y, 1 - c), device_id_type=MESH)
        dst = rx_ref.at[pl.ds(k, 1), pl.ds(pl.multiple_of(i * tr, 8), tr)]
        cp = pltpu.make_async_remote_copy(blk_ref, dst, ssem, rsem, **peer)
        cp.start()
        cp.wait_send()

        @pl.when(jnp.logical_and(k == lead - 1, i == nrow - 1))
        def _():
            pltpu.make_async_remote_copy(rx_ref, rx_ref, ssem, rsem, **peer).wait_recv()

    return pl.pallas_call(body, out_shape=jax.ShapeDtypeStruct(src.shape, src.dtype), grid=(lead, nrow),
                          in_specs=[pl.BlockSpec((1, tr, cols), lambda k, i: (k, i, 0))], out_specs=ANY,
                          scratch_shapes=[pltpu.SemaphoreType.DMA, pltpu.SemaphoreType.DMA], name=name,
                          compiler_params=_params(("arbitrary", "arbitrary")))(src)


def _chip_scatter(pairs, small, name):
    n = len(pairs)

    def body(*refs):
        ins, small_ref = refs[:n], refs[n]
        outs, small_out = refs[n + 1:2 * n + 1], refs[2 * n + 1]
        ssem, rsem = refs[2 * n + 2:]
        x, y, c, me, chips = _place()
        copies = []
        for w in range(n + 1):
            for j, (px, py) in enumerate(chips):
                src = ins[w].at[2 * px + py] if w < n else small_ref
                dst = (outs[w] if w < n else small_out).at[j]
                cp = pltpu.make_async_remote_copy(src, dst, ssem.at[3 * w + j], rsem.at[3 * w + j],
                                                  device_id=(px, py, c), device_id_type=MESH)
                copies.append((cp, cp))
        _run_remote(copies)

    out_shape = tuple(jax.ShapeDtypeStruct((3,) + it.shape[1:], it.dtype) for it in pairs) + \
                (jax.ShapeDtypeStruct((3,) + small.shape, small.dtype),)
    return pl.pallas_call(body, out_shape=out_shape, in_specs=[ANY] * (n + 1), out_specs=tuple([ANY] * (n + 1)),
                          scratch_shapes=[pltpu.SemaphoreType.DMA((3 * n + 3,)),
                                          pltpu.SemaphoreType.DMA((3 * n + 3,))],
                          name=name)(*pairs, small)


def _pack(arrays):
    flat = jnp.concatenate([a.reshape(-1).astype(F32) for a in arrays])
    pad = (-flat.shape[0]) % (8 * LANES)
    return jnp.pad(flat, (0, pad)).reshape(-1, LANES)


def _unpack(packed, shapes):
    flat = packed.reshape(-1)
    out, off = [], 0
    for shp in shapes:
        size = math.prod(shp)
        out.append(flat[off:off + size].reshape(shp))
        off += size
    return out


def kernel(x, mem, g_mix, w_in, b_gate, a_ln_g, a_ln_b, a_ws, a_bs, b_conv, c_wg, c_scale, w_branch_a, w_branch_b, w_branch_c, w_o, g_xattn, g_mem, w_xq, w_xkv, w_xo, g_ffn, w_ffn_in, w_ffn_out, g_final, loss_target, m_g_mix, m_w_in, m_b_gate, m_a_ln_g, m_a_ln_b, m_a_ws, m_a_bs, m_b_conv, m_c_wg, m_c_scale, m_w_branch_a, m_w_branch_b, m_w_branch_c, m_w_o, m_g_xattn, m_g_mem, m_w_xq, m_w_xkv, m_w_xo, m_g_ffn, m_w_ffn_in, m_w_ffn_out, m_g_final, v_g_mix, v_w_in, v_b_gate, v_a_ln_g, v_a_ln_b, v_a_ws, v_a_bs, v_b_conv, v_c_wg, v_c_scale, v_w_branch_a, v_w_branch_b, v_w_branch_c, v_w_o, v_g_xattn, v_g_mem, v_w_xq, v_w_xkv, v_w_xo, v_g_ffn, v_w_ffn_in, v_w_ffn_out, v_g_final):
    args = locals()
    wts = {n: args[n] for n in WEIGHTS}
    mom = {n: args["m_" + n] for n in WEIGHTS}
    var = {n: args["v_" + n] for n in WEIGHTS}
    xs = x[0]
    mems = mem[0]
    tgt = loss_target[0]
    chip = 2 * lax.axis_index("x") + lax.axis_index("y")

    bias_pack = jnp.zeros((DEPTH, 8, 384), F32)
    bias_pack = bias_pack.at[:, 0:3, 0:256].set(b_gate).at[:, 0:3, 256:384].set(b_conv)
    bias_slots = lax.dynamic_update_slice(jnp.zeros((N_CHIPS, DEPTH, 8, 384), F32), bias_pack[None], (chip, 0, 0, 0))
    core = lax.axis_index("c")
    chip_arr = jnp.reshape(chip, (1,)).astype(jnp.int32)
    core_arr = jnp.reshape(core, (1,)).astype(jnp.int32)
    slots = {n: _cast_to_slots(wts[n], chip_arr, name=f"cast_{n}") for n in BIG}
    layer_weights = [{}, {}]

    def land(l, names, gathered, tag):
        shapes = [jax.ShapeDtypeStruct(g.shape, g.dtype) for g in gathered]
        for n, g in zip(names, _push_staged(list(gathered), shapes, True, _plan_gathered,
                                            name=f"l{l}_gather_cores_{tag}")):
            if n in READ_AS_SLABS:
                layer_weights[l][n] = g
            elif n in COL_SHARDED:
                layer_weights[l][n] = jnp.transpose(g, (1, 0, 2)).reshape(1, g.shape[1], N_CHIPS * g.shape[2])
            else:
                layer_weights[l][n] = g.reshape(1, N_CHIPS * g.shape[1], g.shape[2])

    def gather_ride(l, names):
        return _gather_carry([slots[n][l] for n in names])

    rest_of_first = tuple(n for n in BIG if n not in NEED_FIRST)
    most_of_second = tuple(n for n in BIG if n not in NEED_LAST)
    first = _chip_gather([slots[n][0] for n in NEED_FIRST] + [bias_slots], 1, name="l0_gather_chips")
    land(0, NEED_FIRST, first[:-1], "first")
    biases = first[-1]
    b_gate_full = jnp.transpose(biases[:, :, 0:3, 0:256], (1, 2, 0, 3)).reshape(DEPTH, 3, D_MODEL)
    b_conv_full = jnp.transpose(biases[:, :, 0:3, 256:384], (1, 2, 0, 3)).reshape(DEPTH, 3, B_WIDTH)

    tril = jnp.tril(jnp.ones((CHUNK, CHUNK), bool))

    def mixer_params(l):
        wtril = jnp.where(tril[None], a_ws[l], 0.0)
        prm = (a_ln_g[l][None], a_ln_b[l][None], wtril.astype(BF16),
               jnp.broadcast_to(a_bs[l][:, :, None], (A_GROUPS, CHUNK, CHUNK)), b_conv_full[l],
               c_wg[l].astype(BF16), c_scale[l][None])
        return prm, jnp.swapaxes(wtril, 1, 2).astype(BF16)

    saved = []
    xc = xs
    for l in range(DEPTH):
        sv = {"x0": xc}
        prm, _ = mixer_params(l)
        fw = layer_weights[l]
        riding = rest_of_first if l == 0 else NEED_LAST
        out = _norm_mm(xc, g_mix[l][None], fw["w_in"], None, tn=1536, name=f"l{l}_proj", carry=gather_ride(l, riding))
        sv["h"], sv["proj"] = out[0], out[1]
        land(l, riding, out[2], "rest")
        sv["ypre"] = _mixer_fwd(sv["proj"], prm, name=f"l{l}_mixers")
        sv["merged"] = _merge_fwd(sv["ypre"], sv["proj"], fw["w_branch_a"][0], fw["w_branch_b"][0],
                                  fw["w_branch_c"][0], b_gate_full[l], name=f"l{l}_merge")
        xc = _mm(sv["merged"], fw["w_o"], b_lead=0, mode="nn", out_dtype=F32, name=f"l{l}_mix_out", tm=1024, tn=1024,
                 tk=1024, res=xc)
        sv["x1"] = xc
        sv["hq"], sv["q"] = _norm_mm(xc, g_xattn[l][None], fw["w_xq"], 0, tn=1024, name=f"l{l}_q")
        sv["m"] = _rms_fwd(mems, g_mem[l][None], name=f"l{l}_mem_norm")
        sv["kv"] = _mm(sv["m"], fw["w_xkv"], b_lead=0, mode="nn", out_dtype=BF16, name=f"l{l}_kv", tm=256, tn=512,
                       tk=1024)
        sv["o"] = _attn_fwd(sv["q"], sv["kv"], name=f"l{l}_attn")
        xc = _mm(sv["o"], fw["w_xo"], b_lead=0, mode="nn", out_dtype=F32, name=f"l{l}_xattn_out", tm=1024, tn=1024,
                 tk=1024, res=xc)
        sv["x2"] = xc
        out2 = _norm_ffn_in(xc, g_ffn[l][None], fw["w_ffn_in"], None, name=f"l{l}_ffn_in",
                            carry=gather_ride(1, most_of_second) if l == 0 else None)
        sv["h2"], sv["gu"], sv["act"] = out2[0], out2[1], out2[2]
        if l == 0:
            land(1, most_of_second, out2[3], "most")
        xc = _mm(sv["act"], fw["w_ffn_out"], b_lead=0, mode="nn", out_dtype=F32, name=f"l{l}_ffn_out", tm=1024,
                 tn=512, tk=FFN_HIDDEN, res=xc)
        saved.append(sv)

    loss_part, dx, dg_final = _final_loss(xc, g_final[None], tgt, name="final_loss")
    loss = lax.psum(loss_part[0, 0], ("x", "y", "c"))

    pairs = [{}, {}]
    received = [{}, {}]
    small_grads = [None] * DEPTH

    def core_pairs(l, names, grads, tag):
        halves = [jax.ShapeDtypeStruct((g.shape[0], g.shape[1] // 2, g.shape[2]), g.dtype) for g in grads]
        came = _push_staged(grads, halves, False, _plan_grad_halves, name=f"l{l}_reduce_cores_{tag}")
        for n, g, rx in zip(names, grads, came):
            pairs[l][n] = _pair_sum(g, rx, core_arr, name=f"l{l}_pair_sum_{n}")

    def scatter_ride(l, names, when):
        return _scatter_carry([pairs[l][n] for n in names]) if when else None

    ride_second = tuple(n for n in BIG if n not in RIDE_FIRST)
    early, early_big, early_small = NEED_LAST + ("w_xo", "w_xq", "w_xkv"), NEED_LAST + ("w_xkv",), ("w_xo", "w_xq")
    middle = ("w_o", "w_branch_a", "w_branch_b", "w_branch_c")
    for l in reversed(range(DEPTH)):
        sv = saved[l]
        prm, wtril_t = mixer_params(l)
        gb = {}
        sg = {}
        fw = layer_weights[l]
        dgu = _d_act_swiglu(dx, fw["w_ffn_out"], 0, sv["gu"], name=f"l{l}_d_act")
        gb["w_ffn_out"] = _mm(sv["act"], dx, mode="tn", out_dtype=BF16, name=f"l{l}_dw_ffn_out", tm=1408, tn=1024,
                              tk=TOKEN_STEP).reshape(N_CHIPS, FFN_HIDDEN // N_CHIPS, D_MODEL)
        out = _mm(sv["h2"], dgu, mode="tn", out_dtype=BF16, name=f"l{l}_dw_ffn_in", tm=1024, tn=FFN_TILE,
                  tk=TOKEN_STEP, b_halves=True, out_shards=N_CHIPS, carry=scatter_ride(1, RIDE_FIRST, l == 0))
        gb["w_ffn_in"] = out[0] if l == 0 else out
        out2 = _mm_nt_norm_bwd(dgu, fw["w_ffn_in"], None, sv["x2"], g_ffn[l][None], dx, tm=512, a_halves=True,
                               name=f"l{l}_d_h2", carry=scatter_ride(1, ride_second, l == 0))
        dx, sg["g_ffn"] = out2[0], out2[1]
        if l == 0:
            received[1].update(zip(RIDE_FIRST, out[1]))
            received[1].update(zip(ride_second, out2[2]))
        do = _mm(dx, fw["w_xo"], b_lead=0, mode="nt", out_dtype=BF16, name=f"l{l}_d_o", tm=1024, tn=1024, tk=1024)
        gb["w_xo"] = _mm(sv["o"], dx, mode="tn", out_dtype=BF16, name=f"l{l}_dw_xo", tm=1024, tn=1024,
                         tk=TOKEN_STEP).reshape(N_CHIPS, D_MODEL // N_CHIPS, D_MODEL)
        dq, dkv = _attn_bwd(sv["q"], sv["kv"], do, name=f"l{l}_d_attn")
        gb["w_xq"] = _mm(sv["hq"], dq, mode="tn", out_dtype=BF16, name=f"l{l}_dw_xq", tm=1024, tn=1024,
                         tk=TOKEN_STEP).reshape(N_CHIPS, D_MODEL // N_CHIPS, D_MODEL)
        dm = _mm(dkv, fw["w_xkv"], b_lead=0, mode="nt", out_dtype=BF16, name=f"l{l}_d_m", tm=256, tn=1024, tk=1024)
        gb["w_xkv"] = _mm(sv["m"], dkv, mode="tn", out_dtype=BF16, name=f"l{l}_dw_xkv", tm=1024, tn=512, tk=256,
                          out_shards=N_CHIPS)
        _, sg["g_mem"] = _rms_bwd(mems, g_mem[l][None], dm, None, name=f"l{l}_d_mem_norm")
        if l == 0:
            core_pairs(0, early, [gb[n] for n in early], "early")
        dx, sg["g_xattn"] = _mm_nt_norm_bwd(dq, fw["w_xq"], 0, sv["x1"], g_xattn[l][None], dx, tm=1024,
                                            name=f"l{l}_d_hq")
        dmerged = _mm(dx, fw["w_o"], b_lead=0, mode="nt", out_dtype=BF16, name=f"l{l}_d_merged", tm=1024, tn=1024,
                      tk=1024)
        gb["w_o"] = _mm(sv["merged"], dx, mode="tn", out_dtype=BF16, name=f"l{l}_dw_o", tm=1024, tn=1024,
                        tk=TOKEN_STEP).reshape(N_CHIPS, D_MODEL // N_CHIPS, D_MODEL)
        out3 = _merge_bwd(sv["ypre"], sv["proj"], dmerged, fw["w_branch_a"][0], fw["w_branch_b"][0],
                          fw["w_branch_c"][0], b_gate_full[l], name=f"l{l}_d_merge",
                          carry=scatter_ride(0, early_big, l == 0))
        dypre, dproj, sg["b_gate"] = out3[0], out3[1], out3[2]
        for n, dw in zip(("w_branch_a", "w_branch_b", "w_branch_c"), out3[3:6]):
            gb[n] = jnp.transpose(dw.reshape(A_WIDTH, N_CHIPS, D_MODEL // N_CHIPS), (1, 0, 2)).astype(BF16)
        if l == 0:
            core_pairs(0, middle, [gb[n] for n in middle], "middle")
        out5 = _mixer_bwd(sv["proj"], dypre, dproj, prm, wtril_t, name=f"l{l}_d_mixers",
                          carry=scatter_ride(0, middle, l == 0))
        (dproj, sg["a_ln_g"], sg["a_ln_b"], sg["a_ws"], sg["a_bs"], sg["b_conv"], sg["c_wg"], sg["c_scale"]) = out5[:8]
        out4 = _mm(sv["h"], dproj, mode="tn", out_dtype=BF16, name=f"l{l}_dw_in", tm=1024, tn=1536, tk=TOKEN_STEP,
                   out_shards=N_CHIPS, carry=scatter_ride(0, early_small, l == 0))
        gb["w_in"] = out4[0] if l == 0 else out4
        if l == 0:
            core_pairs(0, ("w_in",), [gb["w_in"]], "last")
        out6 = _mm_nt_norm_bwd(dproj, fw["w_in"], None, sv["x0"], g_mix[l][None], dx, tm=512, name=f"l{l}_d_h",
                               carry=scatter_ride(0, ("w_in",), l == 0))
        dx, sg["g_mix"] = out6[0], out6[1]
        if l == 0:
            received[0].update(zip(early_big, out3[6]))
            received[0].update(zip(middle, out5[8]))
            received[0].update(zip(early_small, out4[1]))
            received[0].update(zip(("w_in",), out6[2]))
        else:
            core_pairs(1, BIG, [gb[n] for n in BIG], "all")
        small_grads[l] = sg
    grad_x = dx[None]

    small_names = [n for n in WEIGHTS if n not in BIG]
    small_full_shapes = {n: ((DEPTH, 3, D_MODEL) if n == "b_gate" else (DEPTH, 3, B_WIDTH) if n == "b_conv"
                             else wts[n].shape) for n in small_names}
    small_local = []
    for n in small_names:
        if n == "g_final":
            small_local.append(dg_final)
        else:
            small_local.append(jnp.stack([small_grads[l][n].reshape(small_full_shapes[n][1:]) for l in range(DEPTH)]))
    small_pack = _pack(small_local)

    small_rx = _push_sibling(small_pack[None], name="reduce_cores_small")
    small_pair = _elementwise(lambda a, b: (a + b,), [small_pack[None], small_rx], [F32], name="pair_sum_small")[0][0]
    small_came = _chip_scatter([], small_pair, name="reduce_chips_small")[0]
    small_sum = _chip_sum_ordered(small_pair, small_came, chip_arr, name="chip_sum_small")

    out_g, out_d, out_m, out_v = {}, {}, {}, {}
    mine = [_chip_sum(pairs[0][n], received[0][n], pairs[1][n], received[1][n], chip_arr, name=f"chip_sum_{n}")
            for n in BIG]
    other = _push_staged(mine, [jax.ShapeDtypeStruct(g.shape, g.dtype) for g in mine], False, _plan_whole,
                         name="share_cores")
    for i, n in enumerate(BIG):
        out_g[n], out_d[n], out_m[n], out_v[n] = _adamw_shard(wts[n], mine[i], other[i], mom[n], var[n], core_arr,
                                                              name=f"adamw_{n}")
    small_g = dict(zip(small_names, _unpack(small_sum, [small_full_shapes[n] for n in small_names])))
    small_g["b_gate"] = lax.dynamic_slice_in_dim(small_g["b_gate"], chip * 256, 256, axis=2)
    small_g["b_conv"] = lax.dynamic_slice_in_dim(small_g["b_conv"], chip * 128, 128, axis=2)
    def two_d(a):
        return a.reshape(1, -1) if a.ndim == 1 else a

    upd = _adamw_small(*[[two_d(d[n]) for n in small_names] for d in (wts, small_g, mom, var)], name="adamw_small")
    for n, d, m_new, v_new in zip(small_names, *upd):
        shape = wts[n].shape
        out_g[n], out_d[n], out_m[n], out_v[n] = small_g[n], d.reshape(shape), m_new.reshape(shape), v_new.reshape(shape)

    return (loss, grad_x, *[out_g[n] for n in WEIGHTS], *[out_d[n] for n in WEIGHTS], *[out_m[n] for n in WEIGHTS],
            *[out_v[n] for n in WEIGHTS])
```

```python
import functools
import math
from typing import Callable, NamedTuple

import jax
import jax.numpy as jnp
from jax import lax
from jax.experimental import pallas as pl
from jax.experimental.pallas import tpu as pltpu

F32 = jnp.float32
BF16 = jnp.bfloat16

D_MODEL = 1024
DEPTH = 2
MEM_LEN = 256
EPS = 1e-6
CHUNK = 128
A_GROUPS = 4
A_WIDTH = 512
B_WIDTH = 512
C_WIDTH = 512
C_GROUP = 128
POOL_WINDOWS = (2, 4, 8, 16)
HALO = 16
IN_COLS = 6144
MIX_COLS = 3072
XATTN_HEADS = 4
HEAD_DIM = 256
FFN_HIDDEN = 2816
N_CHIPS = 4

ADAM_LR = 0.001
ADAM_B1 = 0.9
ADAM_B2 = 0.999
ADAM_EPS = 1e-08
ADAM_WD = 0.01
ADAM_STEP = 10

V7X_VMEM_BYTES = 64 * 1024 * 1024
VMEM_LIMIT = (V7X_VMEM_BYTES * 3) // 4
LANES = 128
MESH = pl.DeviceIdType.MESH
ANY = pl.BlockSpec(memory_space=pl.ANY)

BIG = ("w_in", "w_branch_a", "w_branch_b", "w_branch_c", "w_o", "w_xq", "w_xkv", "w_xo", "w_ffn_in", "w_ffn_out")
COL_SHARDED = ("w_in", "w_branch_a", "w_branch_b", "w_branch_c", "w_xkv", "w_ffn_in")
RIDE_FIRST = ("w_in", "w_ffn_out", "w_branch_a", "w_branch_b", "w_branch_c")
NEED_FIRST = ("w_in", "w_branch_a", "w_branch_b", "w_branch_c", "w_o")
NEED_LAST = ("w_ffn_in", "w_ffn_out")
READ_AS_SLABS = ("w_in", "w_ffn_in")
SMALL_REPL = ("g_mix", "a_ln_g", "a_ln_b", "a_ws", "a_bs", "c_wg", "c_scale", "g_xattn", "g_mem", "g_ffn", "g_final")
SMALL_SHARDED = ("b_gate", "b_conv")
WEIGHTS = ("g_mix", "w_in", "b_gate", "a_ln_g", "a_ln_b", "a_ws", "a_bs", "b_conv", "c_wg", "c_scale", "w_branch_a",
           "w_branch_b", "w_branch_c", "w_o", "g_xattn", "g_mem", "w_xq", "w_xkv", "w_xo", "g_ffn", "w_ffn_in",
           "w_ffn_out", "g_final")


def _params(sem=None):
    return pltpu.CompilerParams(dimension_semantics=sem, vmem_limit_bytes=VMEM_LIMIT)


def _blk(dim, pref):
    return pref if dim % pref == 0 else dim


class _Carry(NamedTuple):
    arrays: tuple
    outs: tuple
    in_place: bool
    copies: int
    issue: Callable


def _call_with_carry(body, *, out_shape, grid, in_specs, out_specs, scratch_shapes, name, semantics, operands, carry,
                     aliases=None):
    own_aliases = dict(aliases or {})
    if carry is None:
        return pl.pallas_call(body, out_shape=tuple(out_shape), grid=grid, in_specs=list(in_specs),
                              out_specs=tuple(out_specs), scratch_shapes=list(scratch_shapes), name=name,
                              input_output_aliases=own_aliases, compiler_params=_params(semantics))(*operands)
    n_in, n_out, na, no = len(operands), len(out_shape), len(carry.arrays), len(carry.outs)

    def carried(*refs):
        ins, cins = refs[:n_in], refs[n_in:n_in + na]
        outs, couts = refs[n_in + na:n_in + na + n_out], refs[n_in + na + n_out:n_in + na + n_out + no]
        rest = refs[n_in + na + n_out + no:]
        scratch, ssem, rsem = rest[:-2], rest[-2], rest[-1]
        first = functools.reduce(jnp.logical_and, [pl.program_id(ax) == 0 for ax in range(len(grid))])
        last = functools.reduce(jnp.logical_and, [pl.program_id(ax) == grid[ax] - 1 for ax in range(len(grid))])

        @pl.when(first)
        def _():
            for send, _ in carry.issue(cins, couts, ssem, rsem):
                send.start()

        body(*ins, *outs, *scratch)

        @pl.when(last)
        def _():
            for send, recv in carry.issue(cins, couts, ssem, rsem):
                send.wait_send()
                recv.wait_recv()

    aliases = dict(own_aliases)
    if carry.in_place:
        aliases.update({n_in + i: n_out + i for i in range(na)})
    sems = [pltpu.SemaphoreType.DMA((carry.copies,)), pltpu.SemaphoreType.DMA((carry.copies,))]
    return pl.pallas_call(carried, out_shape=tuple(out_shape) + tuple(carry.outs), grid=grid,
                          in_specs=list(in_specs) + [ANY] * na, out_specs=tuple(out_specs) + tuple([ANY] * no),
                          scratch_shapes=list(scratch_shapes) + sems, input_output_aliases=aliases, name=name,
                          compiler_params=_params(("arbitrary",) * len(grid)))(*operands, *carry.arrays)


def _mm(a, b, *, mode, out_dtype, name, tm, tn, tk, res=None, b_lead=None, b_halves=False, out_shards=None,
        carry=None):
    dims = {"nn": (((1,), (0,)), ((), ())), "nt": (((1,), (1,)), ((), ())), "tn": (((0,), (0,)), ((), ()))}[mode]
    b_rows, b_last = b.shape[-2], b.shape[-1]
    if mode == "nn":
        m, k, n = a.shape[0], a.shape[1], b_last
        assert b_rows == k
    elif mode == "nt":
        m, k, n = a.shape[0], a.shape[1], b_rows
        assert b_last == k
    else:
        k, m, n = a.shape[0], a.shape[1], (2 * b_last if b_halves else b_last)
        assert b_rows == k
    tm, tn, tk = _blk(m, tm), _blk(n, tn), _blk(k, tk)
    nk = k // tk
    grid = (m // tm, n // tn, nk)

    if mode == "tn":
        a_spec = pl.BlockSpec((tk, tm), lambda i, j, kk: (kk, i))
        b_block, b_idx = (tk, tn), (lambda i, j, kk: (kk, j))
    else:
        a_spec = pl.BlockSpec((tm, tk), lambda i, j, kk: (i, kk))
        if mode == "nn":
            b_block, b_idx = (tk, tn), (lambda i, j, kk: (kk, j))
        else:
            b_block, b_idx = (tn, tk), (lambda i, j, kk: (j, kk))
    if b_halves:
        assert mode == "tn" and b_lead is None and b_last % tn == 0
        per_half = b_last // tn
        b_spec = pl.BlockSpec((None,) + b_block, lambda i, j, kk: (j // per_half, kk, j % per_half))
    elif b_lead is None:
        b_spec = pl.BlockSpec(b_block, b_idx)
    else:
        b_spec = pl.BlockSpec((None,) + b_block, lambda i, j, kk: (b_lead,) + b_idx(i, j, kk))
    in_specs = [a_spec, b_spec]
    operands = [a, b]
    if res is not None:
        in_specs.append(pl.BlockSpec((tm, tn), lambda i, j, kk: (i, j)))
        operands.append(res)
    if out_shards is None:
        out_shape = jax.ShapeDtypeStruct((m, n), out_dtype)
        out_spec = pl.BlockSpec((tm, tn), lambda i, j, kk: (i, j))
    else:
        per = n // out_shards
        assert per % tn == 0
        nps = per // tn
        out_shape = jax.ShapeDtypeStruct((out_shards, m, per), out_dtype)
        out_spec = pl.BlockSpec((None, tm, tn), lambda i, j, kk: (j // nps, i, j % nps))

    def body(*refs):
        a_ref, b_ref = refs[0], refs[1]
        res_ref = refs[2] if res is not None else None
        o_ref = refs[3] if res is not None else refs[2]
        part = lax.dot_general(a_ref[...].astype(BF16), b_ref[...].astype(BF16), dims, preferred_element_type=F32)

        def finish(acc):
            if res_ref is not None:
                acc = acc + res_ref[...]
            o_ref[...] = acc.astype(out_dtype)

        if nk == 1:
            finish(part)
        else:
            acc_ref = refs[-1]
            kk = pl.program_id(2)

            @pl.when(kk == 0)
            def _():
                acc_ref[...] = part

            @pl.when(kk > 0)
            def _():
                acc_ref[...] += part

            @pl.when(kk == nk - 1)
            def _():
                finish(acc_ref[...])

    scratch = [] if nk == 1 else [pltpu.VMEM((tm, tn), F32)]
    outs = _call_with_carry(body, out_shape=(out_shape,), grid=grid, in_specs=in_specs, out_specs=(out_spec,),
                            scratch_shapes=scratch, name=name, semantics=("parallel", "parallel", "arbitrary"),
                            operands=operands, carry=carry)
    return outs[0] if carry is None else (outs[0], outs[1:])


ROW_CHUNK = 256
FFN_TILE = 1408
TOKEN_STEP = 2048
NT_DIMS = (((1,), (1,)), ((), ()))


def _norm_rows_into(x_ref, g_ref, h_ref, hs_ref, tm):
    rc = min(ROW_CHUNK, tm)
    for r0 in range(0, tm, rc):
        xv = x_ref[r0:r0 + rc, :]
        r = lax.rsqrt(jnp.mean(xv * xv, axis=-1, keepdims=True) + EPS)
        hv = (xv * r * g_ref[...]).astype(BF16)
        hs_ref[r0:r0 + rc, :] = hv
        if h_ref is not None:
            h_ref[r0:r0 + rc, :] = hv


def _norm_mm(x, g, b, b_lead, *, tn, name, carry=None):
    s, d = x.shape
    slabs = b_lead is None
    n = b.shape[0] * b.shape[-1] if slabs else b.shape[-1]
    tm, tn = _blk(s, 1024), (b.shape[-1] if slabs else _blk(n, tn))
    b_index = (lambda i, j: (j, 0, 0)) if slabs else (lambda i, j: (b_lead, 0, j))

    def body(x_ref, g_ref, b_ref, h_ref, o_ref, hs_ref):
        @pl.when(pl.program_id(1) == 0)
        def _():
            _norm_rows_into(x_ref, g_ref, h_ref, hs_ref, tm)

        o_ref[...] = jnp.dot(hs_ref[...], b_ref[...], preferred_element_type=F32).astype(BF16)

    row = pl.BlockSpec((tm, d), lambda i, j: (i, 0))
    outs = _call_with_carry(body, out_shape=(jax.ShapeDtypeStruct((s, d), BF16), jax.ShapeDtypeStruct((s, n), BF16)),
                            grid=(s // tm, n // tn),
                            in_specs=[row, pl.BlockSpec((1, d), lambda i, j: (0, 0)),
                                      pl.BlockSpec((None, d, tn), b_index)],
                            out_specs=(row, pl.BlockSpec((tm, tn), lambda i, j: (i, j))),
                            scratch_shapes=[pltpu.VMEM((tm, d), BF16)], name=name,
                            semantics=("parallel", "arbitrary"), operands=[x, g, b], carry=carry)
    return outs if carry is None else (outs[0], outs[1], outs[2:])


def _norm_ffn_in(x, g, b, b_lead, name, carry=None):
    s, d = x.shape
    tm, tn = _blk(s, 512), FFN_TILE
    nj = FFN_HIDDEN // tn
    if b_lead is None:
        assert b.shape == (2 * nj, d, tn)
        gate_index, up_index = (lambda j, i: (j, 0, 0)), (lambda j, i: (j + nj, 0, 0))
    else:
        gate_index, up_index = (lambda j, i: (b_lead, 0, j)), (lambda j, i: (b_lead, 0, j + nj))

    n_rows = s // tm

    def body(x_ref, g_ref, bg_ref, bu_ref, h_ref, gu_ref, act_ref, hs_ref):
        first_tile = pl.program_id(0) == 0

        @pl.when(first_tile)
        def _():
            _norm_rows_into(x_ref, g_ref, h_ref, hs_ref, tm)

        @pl.when(jnp.logical_not(first_tile))
        def _():
            _norm_rows_into(x_ref, g_ref, None, hs_ref, tm)

        hv = hs_ref[...]
        gate = jnp.dot(hv, bg_ref[...], preferred_element_type=F32)
        up = jnp.dot(hv, bu_ref[...], preferred_element_type=F32)
        gu_ref[0] = gate.astype(BF16)
        gu_ref[1] = up.astype(BF16)
        act_ref[...] = (gate * _sigmoid(gate) * up).astype(BF16)

    row = pl.BlockSpec((tm, d), lambda j, i: (i, 0))
    out_shape = (jax.ShapeDtypeStruct((s, d), BF16), jax.ShapeDtypeStruct((2, s, FFN_HIDDEN), BF16),
                 jax.ShapeDtypeStruct((s, FFN_HIDDEN), BF16))
    outs = _call_with_carry(body, out_shape=out_shape, grid=(nj, s // tm),
                            in_specs=[row, pl.BlockSpec((1, d), lambda j, i: (0, 0)),
                                      pl.BlockSpec((None, d, tn), gate_index), pl.BlockSpec((None, d, tn), up_index)],
                            out_specs=(pl.BlockSpec((tm, d), lambda j, i: (jnp.where(j == 0, i, n_rows - 1), 0)),
                                       pl.BlockSpec((2, tm, tn), lambda j, i: (0, i, j)),
                                       pl.BlockSpec((tm, tn), lambda j, i: (i, j))),
                            scratch_shapes=[pltpu.VMEM((tm, d), BF16)], name=name,
                            semantics=("arbitrary", "arbitrary"), operands=[x, g, b, b], carry=carry)
    return outs if carry is None else (outs[0], outs[1], outs[2], outs[3:])


def _d_act_swiglu(dx, w, b_lead, gu, name):
    s, d = dx.shape
    tm, tn = _blk(s, 512), FFN_TILE

    def body(dx_ref, w_ref, gu_ref, o_ref):
        dact = lax.dot_general(dx_ref[...].astype(BF16), w_ref[...], NT_DIMS, preferred_element_type=F32)
        gate = gu_ref[0].astype(F32)
        up = gu_ref[1].astype(F32)
        sg = _sigmoid(gate)
        o_ref[0] = (dact * up * sg * (1.0 + gate * (1.0 - sg))).astype(BF16)
        o_ref[1] = (dact * gate * sg).astype(BF16)

    halves = pl.BlockSpec((2, tm, tn), lambda j, i: (0, i, j))
    return pl.pallas_call(body, out_shape=jax.ShapeDtypeStruct(gu.shape, BF16), grid=(FFN_HIDDEN // tn, s // tm),
                          in_specs=[pl.BlockSpec((tm, d), lambda j, i: (i, 0)),
                                    pl.BlockSpec((None, tn, d), lambda j, i: (b_lead, j, 0)), halves],
                          out_specs=halves, name=name,
                          compiler_params=_params(("parallel", "parallel")))(dx, w, gu)


def _mm_nt_norm_bwd(a, b, b_lead, x, g, dres, *, tm, name, a_halves=False, carry=None):
    s, d = x.shape
    slabs = b_lead is None
    nslab, per = (b.shape[0], b.shape[-1]) if slabs else (1, b.shape[-1])
    kdim = nslab * per
    tm = _blk(s, tm)
    rc = min(ROW_CHUNK, tm)
    if a_halves:
        a_spec = pl.BlockSpec((2, tm, kdim // 2), lambda i: (0, i, 0))
    else:
        a_spec = pl.BlockSpec((tm, kdim), lambda i: (i, 0))

    def body(a_ref, b_ref, x_ref, g_ref, r_ref, dx_ref, dg_ref, *scratch):
        i = pl.program_id(0)
        if slabs:
            w_ref, sems = scratch

            @pl.when(i == 0)
            def _():
                copies = [pltpu.make_async_copy(b_ref.at[k], w_ref.at[:, pl.ds(k * per, per)], sems.at[k])
                          for k in range(nslab)]
                for cp in copies:
                    cp.start()
                for cp in copies:
                    cp.wait()
        else:
            w_ref = b_ref
        av = jnp.concatenate([a_ref[0], a_ref[1]], axis=1) if a_halves else a_ref[...]
        dh = lax.dot_general(av.astype(BF16), w_ref[...], NT_DIMS, preferred_element_type=F32)
        gv = g_ref[...]
        dg_part = None
        for r0 in range(0, tm, rc):
            xv = x_ref[r0:r0 + rc, :]
            dhv = dh[r0:r0 + rc, :]
            r = lax.rsqrt(jnp.mean(xv * xv, axis=-1, keepdims=True) + EPS)
            xhat = xv * r
            p = jnp.sum(dhv * xhat, axis=0, keepdims=True)
            dg_part = p if dg_part is None else dg_part + p
            dxhat = dhv * gv
            dx_ref[r0:r0 + rc, :] = r_ref[r0:r0 + rc, :] + r * (
                dxhat - xhat * jnp.mean(dxhat * xhat, axis=-1, keepdims=True))

        @pl.when(i == 0)
        def _():
            dg_ref[...] = dg_part

        @pl.when(i > 0)
        def _():
            dg_ref[...] += dg_part

    row = pl.BlockSpec((tm, d), lambda i: (i, 0))
    vec = pl.BlockSpec((1, d), lambda i: (0, 0))
    if slabs:
        b_spec, scratch = ANY, [pltpu.VMEM((d, kdim), BF16), pltpu.SemaphoreType.DMA((nslab,))]
    else:
        b_spec, scratch = pl.BlockSpec((None, d, kdim), lambda i: (b_lead, 0, 0), pipeline_mode=pl.Buffered(1)), []
    outs = _call_with_carry(body, out_shape=(jax.ShapeDtypeStruct((s, d), F32), jax.ShapeDtypeStruct((1, d), F32)),
                            grid=(s // tm,), in_specs=[a_spec, b_spec, row, vec, row], out_specs=(row, vec),
                            scratch_shapes=scratch, name=name, semantics=("arbitrary",), operands=[a, b, x, g, dres],
                            carry=carry)
    return outs if carry is None else (outs[0], outs[1], outs[2:])


def _rms_fwd(x, g, name):
    s, d = x.shape
    tm = _blk(s, 512)

    def body(x_ref, g_ref, o_ref):
        xv = x_ref[...]
        r = lax.rsqrt(jnp.mean(xv * xv, axis=-1, keepdims=True) + EPS)
        o_ref[...] = (xv * r * g_ref[...]).astype(BF16)

    return pl.pallas_call(body, out_shape=jax.ShapeDtypeStruct((s, d), BF16), grid=(s // tm,),
                          in_specs=[pl.BlockSpec((tm, d), lambda i: (i, 0)), pl.BlockSpec((1, d), lambda i: (0, 0))],
                          out_specs=pl.BlockSpec((tm, d), lambda i: (i, 0)), name=name,
                          compiler_params=_params(("parallel",)))(x, g)


def _rms_bwd(x, g, dh, dres, name):
    s, d = x.shape
    tm = _blk(s, 512)
    has_res = dres is not None

    def body(*refs):
        x_ref, g_ref, dh_ref = refs[0], refs[1], refs[2]
        dx_ref, dg_ref = refs[-2], refs[-1]
        xv = x_ref[...]
        r = lax.rsqrt(jnp.mean(xv * xv, axis=-1, keepdims=True) + EPS)
        xhat = xv * r
        dhv = dh_ref[...].astype(F32)
        part = jnp.sum(dhv * xhat, axis=0, keepdims=True)

        @pl.when(pl.program_id(0) == 0)
        def _():
            dg_ref[...] = part

        @pl.when(pl.program_id(0) > 0)
        def _():
            dg_ref[...] += part

        dxhat = dhv * g_ref[...]
        dx = r * (dxhat - xhat * jnp.mean(dxhat * xhat, axis=-1, keepdims=True))
        if has_res:
            dx = dx + refs[3][...]
        dx_ref[...] = dx

    row = pl.BlockSpec((tm, d), lambda i: (i, 0))
    vec = pl.BlockSpec((1, d), lambda i: (0, 0))
    in_specs = [row, vec, row] + ([row] if has_res else [])
    operands = [x, g, dh] + ([dres] if has_res else [])
    return pl.pallas_call(body, out_shape=(jax.ShapeDtypeStruct((s, d), F32), jax.ShapeDtypeStruct((1, d), F32)),
                          grid=(s // tm,), in_specs=in_specs, out_specs=(row, vec), name=name,
                          compiler_params=_params(("arbitrary",)))(*operands)


def _final_loss(x, g, target, name):
    s, d = x.shape
    tm = _blk(s, 512)

    def body(x_ref, g_ref, t_ref, loss_ref, dx_ref, dg_ref):
        xv = x_ref[...]
        gv = g_ref[...]
        r = lax.rsqrt(jnp.mean(xv * xv, axis=-1, keepdims=True) + EPS)
        xhat = xv * r
        err = xhat * gv - t_ref[...]
        lpart = 0.5 * jnp.sum(jnp.mean(err * err, axis=-1, keepdims=True), axis=0, keepdims=True)
        dy = err * (1.0 / d)
        gpart = jnp.sum(dy * xhat, axis=0, keepdims=True)

        @pl.when(pl.program_id(0) == 0)
        def _():
            loss_ref[...] = lpart
            dg_ref[...] = gpart

        @pl.when(pl.program_id(0) > 0)
        def _():
            loss_ref[...] += lpart
            dg_ref[...] += gpart

        dxhat = dy * gv
        dx_ref[...] = r * (dxhat - xhat * jnp.mean(dxhat * xhat, axis=-1, keepdims=True))

    row = pl.BlockSpec((tm, d), lambda i: (i, 0))
    vec = pl.BlockSpec((1, d), lambda i: (0, 0))
    one = pl.BlockSpec((1, 1), lambda i: (0, 0))
    return pl.pallas_call(body, out_shape=(jax.ShapeDtypeStruct((1, 1), F32), jax.ShapeDtypeStruct((s, d), F32),
                                           jax.ShapeDtypeStruct((1, d), F32)),
                          grid=(s // tm,), in_specs=[row, vec, row], out_specs=(one, row, vec), name=name,
                          compiler_params=_params(("arbitrary",)))(x, g, target)


def _erf_parts(x):
    cdf = 0.5 * (1.0 + lax.erf(x * (1.0 / math.sqrt(2.0))))
    return cdf


def _shift_down(ext, k):
    return pltpu.roll(ext, k, 0)


def _shift_up(ext, k):
    return pltpu.roll(ext, ext.shape[0] - k, 0)


def _mixer_forward_math(z, halo, row0, prm, tm):
    ln_g, ln_b, wtril, bsb, conv_w, wg, scale = prm
    za = z[:, 0:2 * A_WIDTH]
    gb = z[:, 1024:1536]
    gc = z[:, 1536:2048]
    xin = z[:, 2048:2560]
    zc = z[:, 2560:3072]
    cdf = _erf_parts(za)
    act = za * cdf
    u = act[:, :A_WIDTH]
    v = act[:, A_WIDTH:]
    mu = jnp.mean(v, axis=-1, keepdims=True)
    vc = v - mu
    rstd = lax.rsqrt(jnp.mean(vc * vc, axis=-1, keepdims=True) + EPS)
    vhat = vc * rstd
    vn = (vhat * ln_g + ln_b).astype(BF16)
    rows = []
    for c in range(tm // CHUNK):
        cols = []
        for g in range(A_GROUPS):
            blk = vn[c * CHUNK:(c + 1) * CHUNK, g * CHUNK:(g + 1) * CHUNK]
            cols.append(jnp.dot(wtril[g], blk, preferred_element_type=F32) + bsb[g])
        rows.append(jnp.concatenate(cols, axis=1))
    mixed = jnp.concatenate(rows, axis=0)
    ya = u * mixed
    y = gc * xin
    yext = jnp.concatenate([halo[:, 0:512] * halo[:, 512:1024], y], axis=0)
    y1 = _shift_down(yext, 1)[HALO:]
    y2 = _shift_down(yext, 2)[HALO:]
    conv = conv_w[0:1, :] * y2 + conv_w[1:2, :] * y1 + conv_w[2:3, :] * y
    yb = gb * conv
    t = row0 + lax.broadcasted_iota(jnp.int32, (tm, 1), 0)
    zext = jnp.concatenate([halo[:, 1024:1536], zc], axis=0)
    pooled, p = [], []
    for gi, win in enumerate(POOL_WINDOWS):
        sw = zext[:, gi * C_GROUP:(gi + 1) * C_GROUP]
        for step in range(gi + 1):
            sw = sw + _shift_down(sw, 2 ** step)
        cnt = jnp.minimum(t + 1, win).astype(F32)
        pg = sw[HALO:] / cnt - zc[:, gi * C_GROUP:(gi + 1) * C_GROUP]
        pooled.append(pg)
        p.append(jnp.dot(pg.astype(BF16), wg[gi], preferred_element_type=F32))
    p = jnp.concatenate(p, axis=1)
    yc = p * scale
    saved = dict(za=za, cdf=cdf, u=u, mixed=mixed, rstd=rstd, vhat=vhat, vn=vn, gb=gb, gc=gc, xin=xin, y=y, y1=y1, y2=y2,
                 conv=conv, pooled=pooled, p=p, t=t)
    return ya, yb, yc, saved


def _mixer_specs(tm, s):
    nb16 = tm // HALO
    main = pl.BlockSpec((tm, MIX_COLS), lambda i: (i, 0))
    prev = pl.BlockSpec((HALO, 1536), lambda i: (jnp.maximum(i * nb16 - 1, 0), 1))
    return main, prev


def _full(shape):
    n = len(shape)
    return pl.BlockSpec(shape, lambda i: (0,) * n)


def _mixer_fwd(proj, prm, name):
    s = proj.shape[0]
    tm = _blk(s, 256)
    main, prev = _mixer_specs(tm, s)

    def body(z_ref, h_ref, lng, lnb, wt, bsb, cw, wg, sc, o_ref):
        i = pl.program_id(0)
        halo = jnp.where(i > 0, h_ref[...].astype(F32), 0.0)
        prm_v = (lng[...], lnb[...], wt[...], bsb[...], cw[...], wg[...], sc[...])
        ya, yb, yc, _ = _mixer_forward_math(z_ref[...].astype(F32), halo, i * tm, prm_v, tm)
        o_ref[...] = jnp.concatenate([ya, yb, yc], axis=1).astype(BF16)

    return pl.pallas_call(body, out_shape=jax.ShapeDtypeStruct((s, 1536), BF16), grid=(s // tm,),
                          in_specs=[main, prev] + [_full(p.shape) for p in prm],
                          out_specs=pl.BlockSpec((tm, 1536), lambda i: (i, 0)), name=name,
                          compiler_params=_params(("parallel",)))(proj, proj, *prm)


def _mixer_bwd(proj, dypre, dproj_in, prm, wtril_t, name, carry=None):
    s = proj.shape[0]
    tm = _blk(s, 256)
    nblk = s // tm
    nb16 = tm // HALO
    main, prev = _mixer_specs(tm, s)
    nxt_row = lambda i: jnp.minimum((i + 1) * nb16, s // HALO - 1)
    next_dy = pl.BlockSpec((HALO, 1536), lambda i: (nxt_row(i), 0))
    next_gb = pl.BlockSpec((HALO, 512), lambda i: (nxt_row(i), 2))
    ones8 = jnp.ones((8, CHUNK), F32)

    def body(z_ref, h_ref, dy_ref, ndy_ref, ngb_ref, _alias, lng, lnb, wt, bsb, cw, wg, sc, wtt, ones_ref,
             dz_ref, dlng_ref, dlnb_ref, dws_ref, dbs_ref, dcw_ref, dwg_ref, dsc_ref):
        i = pl.program_id(0)
        first = i == 0
        halo = jnp.where(i > 0, h_ref[...].astype(F32), 0.0)
        prm_v = (lng[...], lnb[...], wt[...], bsb[...], cw[...], wg[...], sc[...])
        _, _, _, sv = _mixer_forward_math(z_ref[...].astype(F32), halo, i * tm, prm_v, tm)
        dy = dy_ref[...].astype(F32)
        dya, dyb, dyc = dy[:, 0:512], dy[:, 512:1024], dy[:, 1024:1536]
        not_last = i < nblk - 1
        ndy = jnp.where(not_last, ndy_ref[...].astype(F32), 0.0)
        ngb = ngb_ref[...].astype(F32)

        def accumulate(ref, val):
            @pl.when(first)
            def _():
                ref[...] = val

            @pl.when(jnp.logical_not(first))
            def _():
                ref[...] += val

        du = dya * sv["mixed"]
        dmixed = dya * sv["u"]
        vn = sv["vn"]
        wttv = wtt[...]
        dvn_rows = []
        dws_parts = [None] * A_GROUPS
        dbs_parts = [None] * A_GROUPS
        for c in range(tm // CHUNK):
            cols = []
            for g in range(A_GROUPS):
                dm = dmixed[c * CHUNK:(c + 1) * CHUNK, g * CHUNK:(g + 1) * CHUNK]
                dmb = dm.astype(BF16)
                vb = vn[c * CHUNK:(c + 1) * CHUNK, g * CHUNK:(g + 1) * CHUNK]
                w_part = lax.dot_general(dmb, vb, (((1,), (1,)), ((), ())), preferred_element_type=F32)
                b_part = lax.dot_general(ones_ref[...], dm, (((1,), (1,)), ((), ())), preferred_element_type=F32,
                                         precision=lax.Precision.HIGHEST)[0:1, :]
                dws_parts[g] = w_part if dws_parts[g] is None else dws_parts[g] + w_part
                dbs_parts[g] = b_part if dbs_parts[g] is None else dbs_parts[g] + b_part
                cols.append(jnp.dot(wttv[g], dmb, preferred_element_type=F32))
            dvn_rows.append(jnp.concatenate(cols, axis=1))
        dvn = jnp.concatenate(dvn_rows, axis=0)
        tri = lax.broadcasted_iota(jnp.int32, (CHUNK, CHUNK), 0) >= lax.broadcasted_iota(jnp.int32, (CHUNK, CHUNK), 1)
        accumulate(dws_ref, jnp.stack([jnp.where(tri, w, 0.0) for w in dws_parts], axis=0))
        accumulate(dbs_ref, jnp.concatenate(dbs_parts, axis=0))
        vhat = sv["vhat"]
        accumulate(dlng_ref, jnp.sum(dvn * vhat, axis=0, keepdims=True))
        accumulate(dlnb_ref, jnp.sum(dvn, axis=0, keepdims=True))
        dvhat = dvn * lng[...]
        dv = sv["rstd"] * (dvhat - jnp.mean(dvhat, axis=-1, keepdims=True)
                           - vhat * jnp.mean(dvhat * vhat, axis=-1, keepdims=True))
        za = sv["za"]
        dgelu = sv["cdf"] + za * jnp.exp(-0.5 * za * za) * (1.0 / math.sqrt(2.0 * math.pi))
        dza = jnp.concatenate([du, dv], axis=1) * dgelu
        cwv = cw[...]
        dconv = dyb * sv["gb"]
        dgb = dyb * sv["conv"]
        dcext = jnp.concatenate([dconv, ndy[:, 512:1024] * ngb], axis=0)
        d1 = _shift_up(dcext, 1)[:tm]
        d2 = _shift_up(dcext, 2)[:tm]
        dyy = cwv[2:3, :] * dconv + cwv[1:2, :] * d1 + cwv[0:1, :] * d2
        dgc = dyy * sv["xin"]
        dxin = dyy * sv["gc"]
        accumulate(dcw_ref, jnp.concatenate([jnp.sum(dconv * sv["y2"], axis=0, keepdims=True),
                                             jnp.sum(dconv * sv["y1"], axis=0, keepdims=True),
                                             jnp.sum(dconv * sv["y"], axis=0, keepdims=True)], axis=0))
        scv = sc[...]
        accumulate(dsc_ref, jnp.sum(dyc * sv["p"], axis=0, keepdims=True))
        dp = dyc * scv
        ndp = ndy[:, 1024:1536] * scv
        wgv = wg[...]
        dzc_cols, dwg_parts = [], []
        for gi, win in enumerate(POOL_WINDOWS):
            sl = slice(gi * C_GROUP, (gi + 1) * C_GROUP)
            dpb = dp[:, sl].astype(BF16)
            dpool = lax.dot_general(dpb, wgv[gi], (((1,), (1,)), ((), ())), preferred_element_type=F32)
            ndpool = lax.dot_general(ndp[:, sl].astype(BF16), wgv[gi], (((1,), (1,)), ((), ())),
                                     preferred_element_type=F32)
            dwg_parts.append(lax.dot_general(sv["pooled"][gi].astype(BF16), dpb, (((0,), (0,)), ((), ())),
                                             preferred_element_type=F32))
            cnt = jnp.minimum(sv["t"] + 1, win).astype(F32)
            fw = jnp.concatenate([dpool / cnt, ndpool * (1.0 / win)], axis=0)
            for step in range(gi + 1):
                fw = fw + _shift_up(fw, 2 ** step)
            dzc_cols.append(fw[:tm] - dpool)
        accumulate(dwg_ref, jnp.stack(dwg_parts, axis=0))
        dzc = jnp.concatenate(dzc_cols, axis=1)
        dz_ref[...] = jnp.concatenate([dza, dgb, dgc, dxin, dzc], axis=1).astype(BF16)

    out_shape = (jax.ShapeDtypeStruct(dproj_in.shape, BF16), jax.ShapeDtypeStruct((1, A_WIDTH), F32),
                 jax.ShapeDtypeStruct((1, A_WIDTH), F32), jax.ShapeDtypeStruct((A_GROUPS, CHUNK, CHUNK), F32),
                 jax.ShapeDtypeStruct((A_GROUPS, CHUNK), F32), jax.ShapeDtypeStruct((3, B_WIDTH), F32),
                 jax.ShapeDtypeStruct((4, C_GROUP, C_GROUP), F32), jax.ShapeDtypeStruct((1, C_WIDTH), F32))
    out_specs = (main,) + tuple(_full(o.shape) for o in out_shape[1:])
    in_specs = [main, prev, pl.BlockSpec((tm, 1536), lambda i: (i, 0)), next_dy, next_gb, ANY] + \
               [_full(p.shape) for p in prm] + [_full(wtril_t.shape), _full(ones8.shape)]
    outs = _call_with_carry(body, out_shape=out_shape, grid=(nblk,), in_specs=in_specs, out_specs=out_specs,
                            scratch_shapes=[], name=name, semantics=("arbitrary",),
                            operands=[proj, proj, dypre, dypre, proj, dproj_in, *prm, wtril_t, ones8], carry=carry,
                            aliases={5: 0})
    return tuple(outs) if carry is None else tuple(outs[:8]) + (outs[8:],)


def _sigmoid(x):
    return 1.0 / (1.0 + jnp.exp(-x))


def _merge_fwd(ypre, proj, wa, wb, wc, bgate, name):
    s = ypre.shape[0]
    tm = _blk(s, 512)

    def body(y_ref, zg_ref, wa_ref, wb_ref, wc_ref, bg_ref, o_ref):
        yv = y_ref[...]
        acc = None
        for i, w_ref in enumerate((wa_ref, wb_ref, wc_ref)):
            br = jnp.dot(yv[:, i * 512:(i + 1) * 512], w_ref[...], preferred_element_type=F32)
            gate = _sigmoid(zg_ref[:, i * D_MODEL:(i + 1) * D_MODEL].astype(F32) + bg_ref[i:i + 1, :])
            acc = gate * br if acc is None else acc + gate * br
        o_ref[...] = acc.astype(BF16)

    wspec = _full(wa.shape)
    return pl.pallas_call(body, out_shape=jax.ShapeDtypeStruct((s, D_MODEL), BF16), grid=(s // tm,),
                          in_specs=[pl.BlockSpec((tm, 1536), lambda i: (i, 0)),
                                    pl.BlockSpec((tm, 3 * D_MODEL), lambda i: (i, 1)), wspec, wspec, wspec,
                                    _full(bgate.shape)],
                          out_specs=pl.BlockSpec((tm, D_MODEL), lambda i: (i, 0)), name=name,
                          compiler_params=_params(("parallel",)))(ypre, proj, wa, wb, wc, bgate)


def _merge_bwd(ypre, proj, dmerged, wa, wb, wc, bgate, name, carry=None):
    s = ypre.shape[0]
    tm = _blk(s, 512)

    def body(y_ref, zg_ref, dm_ref, wa_ref, wb_ref, wc_ref, bg_ref, dyp_ref, dzg_ref, dbg_ref, dwa_ref, dwb_ref,
             dwc_ref):
        first = pl.program_id(0) == 0

        def accumulate(ref, val):
            @pl.when(first)
            def _():
                ref[...] = val

            @pl.when(jnp.logical_not(first))
            def _():
                ref[...] += val

        yv = y_ref[...]
        dm = dm_ref[...].astype(F32)
        dyp, dzg, dbg = [], [], []
        for i, (w_ref, dw_ref) in enumerate(((wa_ref, dwa_ref), (wb_ref, dwb_ref), (wc_ref, dwc_ref))):
            wv = w_ref[...]
            yi = yv[:, i * 512:(i + 1) * 512]
            br = jnp.dot(yi, wv, preferred_element_type=F32)
            gate = _sigmoid(zg_ref[:, i * D_MODEL:(i + 1) * D_MODEL].astype(F32) + bg_ref[i:i + 1, :])
            dbi = (dm * gate).astype(BF16)
            dzi = dm * br * gate * (1.0 - gate)
            dzg.append(dzi.astype(BF16))
            dbg.append(jnp.sum(dzi, axis=0, keepdims=True))
            dyp.append(lax.dot_general(dbi, wv, (((1,), (1,)), ((), ())), preferred_element_type=F32).astype(BF16))
            accumulate(dw_ref, lax.dot_general(yi, dbi, (((0,), (0,)), ((), ())), preferred_element_type=F32))
        dyp_ref[...] = jnp.concatenate(dyp, axis=1)
        dzg_ref[...] = jnp.concatenate(dzg, axis=1)
        accumulate(dbg_ref, jnp.concatenate(dbg, axis=0))

    wspec = _full(wa.shape)
    dw_shape = jax.ShapeDtypeStruct(wa.shape, F32)
    out_shape = (jax.ShapeDtypeStruct((s, 1536), BF16), jax.ShapeDtypeStruct((s, IN_COLS), BF16),
                 jax.ShapeDtypeStruct((3, D_MODEL), F32), dw_shape, dw_shape, dw_shape)
    outs = _call_with_carry(body, out_shape=out_shape, grid=(s // tm,),
                            in_specs=[pl.BlockSpec((tm, 1536), lambda i: (i, 0)),
                                      pl.BlockSpec((tm, 3 * D_MODEL), lambda i: (i, 1)),
                                      pl.BlockSpec((tm, D_MODEL), lambda i: (i, 0)), wspec, wspec, wspec,
                                      _full(bgate.shape)],
                            out_specs=(pl.BlockSpec((tm, 1536), lambda i: (i, 0)),
                                       pl.BlockSpec((tm, 3 * D_MODEL), lambda i: (i, 1)), _full((3, D_MODEL)),
                                       wspec, wspec, wspec),
                            scratch_shapes=[], name=name, semantics=("arbitrary",),
                            operands=[ypre, proj, dmerged, wa, wb, wc, bgate], carry=carry)
    return tuple(outs) if carry is None else tuple(outs[:6]) + (outs[6:],)


def _softmax_rows(q, k):
    sc = lax.dot_general(q, k, (((1,), (1,)), ((), ())), preferred_element_type=F32) * (HEAD_DIM ** -0.5)
    e = jnp.exp(sc - jnp.max(sc, axis=-1, keepdims=True))
    return e / jnp.sum(e, axis=-1, keepdims=True)


def _attn_fwd(q, kv, name):
    s = q.shape[0]
    tm = _blk(s, 512)

    def body(q_ref, kv_ref, o_ref):
        outs = []
        for h in range(XATTN_HEADS):
            sl = slice(h * HEAD_DIM, (h + 1) * HEAD_DIM)
            p = _softmax_rows(q_ref[:, sl], kv_ref[:, sl])
            outs.append(jnp.dot(p.astype(BF16), kv_ref[:, D_MODEL + h * HEAD_DIM:D_MODEL + (h + 1) * HEAD_DIM],
                                preferred_element_type=F32))
        o_ref[...] = jnp.concatenate(outs, axis=1).astype(BF16)

    return pl.pallas_call(body, out_shape=jax.ShapeDtypeStruct((s, D_MODEL), BF16), grid=(s // tm,),
                          in_specs=[pl.BlockSpec((tm, D_MODEL), lambda i: (i, 0)), _full(kv.shape)],
                          out_specs=pl.BlockSpec((tm, D_MODEL), lambda i: (i, 0)), name=name,
                          compiler_params=_params(("parallel",)))(q, kv)


def _attn_bwd(q, kv, do, name):
    s = q.shape[0]
    tm = _blk(s, 512)

    def body(q_ref, kv_ref, do_ref, dq_ref, dkv_ref):
        dqs, dks, dvs = [], [], []
        for h in range(XATTN_HEADS):
            sl = slice(h * HEAD_DIM, (h + 1) * HEAD_DIM)
            vsl = slice(D_MODEL + h * HEAD_DIM, D_MODEL + (h + 1) * HEAD_DIM)
            qh, kh, vh, doh = q_ref[:, sl], kv_ref[:, sl], kv_ref[:, vsl], do_ref[:, sl]
            p = _softmax_rows(qh, kh)
            pb = p.astype(BF16)
            dvs.append(lax.dot_general(pb, doh, (((0,), (0,)), ((), ())), preferred_element_type=F32))
            dp = lax.dot_general(doh, vh, (((1,), (1,)), ((), ())), preferred_element_type=F32)
            ds = p * (dp - jnp.sum(dp * p, axis=-1, keepdims=True)) * (HEAD_DIM ** -0.5)
            dsb = ds.astype(BF16)
            dqs.append(jnp.dot(dsb, kh, preferred_element_type=F32))
            dks.append(lax.dot_general(dsb, qh, (((0,), (0,)), ((), ())), preferred_element_type=F32))
        dq_ref[...] = jnp.concatenate(dqs, axis=1).astype(BF16)
        part = jnp.concatenate(dks + dvs, axis=1)

        @pl.when(pl.program_id(0) == 0)
        def _():
            dkv_ref[...] = part

        @pl.when(pl.program_id(0) > 0)
        def _():
            dkv_ref[...] += part

    row = pl.BlockSpec((tm, D_MODEL), lambda i: (i, 0))
    return pl.pallas_call(body, out_shape=(jax.ShapeDtypeStruct((s, D_MODEL), BF16),
                                           jax.ShapeDtypeStruct(kv.shape, F32)),
                          grid=(s // tm,), in_specs=[row, _full(kv.shape), row], out_specs=(row, _full(kv.shape)),
                          name=name, compiler_params=_params(("arbitrary",)))(q, kv, do)


def _rows_block(rows, cols):
    target = (512 * 1024) // cols
    fits = [cand for cand in range(8, rows + 1, 8) if rows % cand == 0 and cand <= target]
    return fits[-1] if fits else rows


def _elementwise(fn, ins, out_dtypes, name):
    lead, rows, cols = ins[0].shape
    tr = _rows_block(rows, cols)
    spec = pl.BlockSpec((None, tr, cols), lambda l, i: (l, i, 0))
    n_in = len(ins)

    def body(*refs):
        for o_ref, o in zip(refs[n_in:], fn(*[r[...] for r in refs[:n_in]])):
            o_ref[...] = o.astype(o_ref.dtype)

    out_shape = tuple(jax.ShapeDtypeStruct((lead, rows, cols), dt) for dt in out_dtypes)
    return pl.pallas_call(body, out_shape=out_shape, grid=(lead, rows // tr), in_specs=[spec] * n_in,
                          out_specs=tuple([spec] * len(out_dtypes)), name=name,
                          compiler_params=_params(("parallel", "parallel")))(*ins)


def _cast_to_slots(w, chip, name):
    _, rows, cols = w.shape
    tr = _rows_block(rows, cols)

    def body(chip_ref, w_ref, o0_ref, o1_ref):
        o0_ref[...] = w_ref[0].astype(BF16)
        o1_ref[...] = w_ref[1].astype(BF16)

    slot = pl.BlockSpec((None, tr, cols), lambda i, chip_ref: (chip_ref[0], i, 0))
    grid_spec = pltpu.PrefetchScalarGridSpec(
        num_scalar_prefetch=1, grid=(rows // tr,),
        in_specs=[pl.BlockSpec((DEPTH, tr, cols), lambda i, chip_ref: (0, i, 0))], out_specs=(slot, slot))
    shape = jax.ShapeDtypeStruct((N_CHIPS, rows, cols), BF16)
    return pl.pallas_call(body, out_shape=(shape, shape), grid_spec=grid_spec, name=name,
                          compiler_params=_params(("parallel",)))(chip, w)


def _chip_sum(pair0, rx0, pair1, rx1, chip, name):
    _, rows, cols = pair0.shape
    tr = _rows_block(rows, cols)

    def body(chip_ref, p0_ref, rx0_ref, p1_ref, rx1_ref, o_ref):
        for layer, (p_ref, rx_ref) in enumerate(((p0_ref, rx0_ref), (p1_ref, rx1_ref))):
            acc = p_ref[...].astype(F32)
            for j in range(3):
                acc = acc + rx_ref[j].astype(F32)
            o_ref[layer] = acc

    mine = pl.BlockSpec((None, tr, cols), lambda i, chip_ref: (chip_ref[0], i, 0))
    theirs = pl.BlockSpec((3, tr, cols), lambda i, chip_ref: (0, i, 0))
    grid_spec = pltpu.PrefetchScalarGridSpec(
        num_scalar_prefetch=1, grid=(rows // tr,), in_specs=[mine, theirs, mine, theirs],
        out_specs=pl.BlockSpec((DEPTH, tr, cols), lambda i, chip_ref: (0, i, 0)))
    return pl.pallas_call(body, out_shape=jax.ShapeDtypeStruct((DEPTH, rows, cols), F32), grid_spec=grid_spec,
                          name=name, compiler_params=_params(("parallel",)))(chip, pair0, rx0, pair1, rx1)


def _chip_sum_ordered(own, rx, chip, name):
    rows, cols = own.shape

    def body(chip_ref, own_ref, rx_ref, o_ref):
        me = chip_ref[0]
        acc = None
        for k in range(N_CHIPS):
            rel = jnp.bitwise_xor(me, k)
            term = jnp.where(rel == 0, own_ref[...],
                             jnp.where(rel == 2, rx_ref[0], jnp.where(rel == 1, rx_ref[1], rx_ref[2])))
            acc = term if acc is None else acc + term
        o_ref[...] = acc

    grid_spec = pltpu.PrefetchScalarGridSpec(
        num_scalar_prefetch=1, grid=(1,),
        in_specs=[pl.BlockSpec((rows, cols), lambda i, chip_ref: (0, 0)),
                  pl.BlockSpec((3, rows, cols), lambda i, chip_ref: (0, 0, 0))],
        out_specs=pl.BlockSpec((rows, cols), lambda i, chip_ref: (0, 0)))
    return pl.pallas_call(body, out_shape=jax.ShapeDtypeStruct((rows, cols), F32), grid_spec=grid_spec, name=name,
                          compiler_params=_params(("arbitrary",)))(chip, own, rx)


def _adamw_math(w, g, m, v):
    m = ADAM_B1 * m + (1.0 - ADAM_B1) * g
    v = ADAM_B2 * v + (1.0 - ADAM_B2) * (g * g)
    m_hat = m / (1.0 - ADAM_B1 ** ADAM_STEP)
    v_hat = v / (1.0 - ADAM_B2 ** ADAM_STEP)
    delta = -ADAM_LR * (m_hat / (jnp.sqrt(v_hat) + ADAM_EPS) + ADAM_WD * w)
    return delta, m, v


def _adamw_small(ws, gs, ms, vs, name):
    n = len(ws)

    def body(*refs):
        for i in range(n):
            w_ref, g_ref, m_ref, v_ref = (refs[k * n + i] for k in range(4))
            d_out, m_out, v_out = (refs[(4 + k) * n + i] for k in range(3))
            d_out[...], m_out[...], v_out[...] = _adamw_math(w_ref[...], g_ref[...], m_ref[...], v_ref[...])

    vmem = pl.BlockSpec(memory_space=pltpu.VMEM)
    shapes = tuple(jax.ShapeDtypeStruct(w.shape, F32) for w in ws)
    outs = pl.pallas_call(body, out_shape=shapes * 3, in_specs=[vmem] * (4 * n), out_specs=tuple([vmem] * (3 * n)),
                          name=name, compiler_params=_params())(*ws, *gs, *ms, *vs)
    return outs[:n], outs[n:2 * n], outs[2 * n:]


def _adamw_shard(w, mine, other, m, v, core, name):
    lead, rows, cols = w.shape
    half = rows // 2
    tr = _rows_block(half, cols)
    nhalf = half // tr

    def body(core_ref, w_ref, mine_ref, other_ref, m_ref, v_ref, g_out, d_out, m_out, v_out):
        g = jnp.where(pl.program_id(1) // nhalf == core_ref[0], mine_ref[...], other_ref[...])
        d_new, m_new, v_new = _adamw_math(w_ref[...], g, m_ref[...], v_ref[...])
        g_out[...] = g
        d_out[...] = d_new
        m_out[...] = m_new
        v_out[...] = v_new

    whole = pl.BlockSpec((None, tr, cols), lambda l, i, core_ref: (l, i, 0))
    own = pl.BlockSpec((None, tr, cols),
                       lambda l, i, core_ref: (l, jnp.where(i // nhalf == core_ref[0], i % nhalf, 0), 0))
    far = pl.BlockSpec((None, tr, cols),
                       lambda l, i, core_ref: (l, jnp.where(i // nhalf == core_ref[0], 0, i % nhalf), 0))
    shape = jax.ShapeDtypeStruct(w.shape, F32)
    grid_spec = pltpu.PrefetchScalarGridSpec(num_scalar_prefetch=1, grid=(lead, rows // tr),
                                             in_specs=[whole, own, far, whole, whole], out_specs=(whole,) * 4)
    return pl.pallas_call(body, out_shape=(shape,) * 4, grid_spec=grid_spec, name=name,
                          compiler_params=_params(("parallel", "parallel")))(core, w, mine, other, m, v)


def _place():
    x, y, c = lax.axis_index("x"), lax.axis_index("y"), lax.axis_index("c")
    chips = [(1 - x, y), (x, 1 - y), (1 - x, 1 - y)]
    return x, y, c, 2 * x + y, chips


def _run_remote(copies):
    for send, _ in copies:
        send.start()
    for send, recv in copies:
        send.wait_send()
        recv.wait_recv()


def _half_rows(ref, c):
    half = ref.shape[1] // 2
    return pl.ds(pl.multiple_of(c * half, 8), half)


def _gather_issue(whole):
    def issue(_, refs, ssem, rsem):
        x, y, c, me, chips = _place()
        copies = []
        for w, ref in enumerate(refs):
            part = (me,) if w >= len(refs) - whole else (me, _half_rows(ref, c))
            mine = ref.at[part]
            for j, (px, py) in enumerate(chips):
                sems = (ssem.at[3 * w + j], rsem.at[3 * w + j])
                peer = dict(device_id=(px, py, c), device_id_type=MESH)
                send = pltpu.make_async_remote_copy(mine, mine, *sems, **peer)
                recv = pltpu.make_async_remote_copy(mine, ref.at[(2 * px + py,) + part[1:]], *sems, **peer)
                copies.append((send, recv))
        return copies
    return issue


def _gather_carry(fulls, whole=0):
    return _Carry(arrays=tuple(fulls), outs=tuple(jax.ShapeDtypeStruct(f.shape, f.dtype) for f in fulls),
                  in_place=True, copies=3 * len(fulls), issue=_gather_issue(whole))


def _scatter_issue(ins, outs, ssem, rsem):
    x, y, c, me, chips = _place()
    copies = []
    for w, (src, dst) in enumerate(zip(ins, outs)):
        for j, (px, py) in enumerate(chips):
            cp = pltpu.make_async_remote_copy(src.at[2 * px + py], dst.at[j], ssem.at[3 * w + j], rsem.at[3 * w + j],
                                              device_id=(px, py, c), device_id_type=MESH)
            copies.append((cp, cp))
    return copies


def _scatter_carry(pairs):
    return _Carry(arrays=tuple(pairs), outs=tuple(jax.ShapeDtypeStruct((3,) + p.shape[1:], p.dtype) for p in pairs),
                  in_place=False, copies=3 * len(pairs), issue=_scatter_issue)


def _chip_gather(fulls, whole, name):
    n = len(fulls)
    issue = _gather_issue(whole)

    def body(*refs):
        _run_remote(issue(None, refs[n:2 * n], refs[2 * n], refs[2 * n + 1]))

    out_shape = tuple(jax.ShapeDtypeStruct(f.shape, f.dtype) for f in fulls)
    return pl.pallas_call(body, out_shape=out_shape, in_specs=[ANY] * n, out_specs=tuple([ANY] * n),
                          input_output_aliases={w: w for w in range(n)},
                          scratch_shapes=[pltpu.SemaphoreType.DMA((3 * n,)), pltpu.SemaphoreType.DMA((3 * n,))],
                          name=name)(*fulls)


class _Blocks(NamedTuple):
    tile: tuple
    dtype: object
    moves: list
    landed: object


STAGE_BYTES = 4 * 1024 * 1024


def _rows(start, size):
    return pl.ds(start if isinstance(start, int) else pl.multiple_of(start, 8), size)


def _stage_tile(lead, rows, cols, dtype):
    per_row = cols * jnp.dtype(dtype).itemsize
    if lead * rows * per_row <= STAGE_BYTES:
        return lead, rows
    fits = [cand for cand in range(8, rows + 1, 8) if rows % cand == 0 and cand * per_row <= STAGE_BYTES]
    return 1, (fits[-1] if fits else rows)


def _plan_gathered(_, refs, c, me):
    work = []
    for ref in refs:
        lead, rows, cols = ref.shape
        half = rows // 2
        nk, tr = _stage_tile(lead, half, cols, ref.dtype)
        if nk == lead:
            part = ref.at[pl.ds(0, lead), _rows(c * half, half)]
            moves, landed = [(part, part)], ref.at[pl.ds(0, lead), pl.ds(0, half)]
        else:
            moves = []
            for j in range(N_CHIPS - 1):
                k = jnp.bitwise_xor(me, j + 1)
                for b in range(half // tr):
                    part = ref.at[pl.ds(k, 1), _rows(c * half + b * tr, tr)]
                    moves.append((part, part))
            landed = ref.at[pl.ds(0, N_CHIPS - 1), pl.ds(0, half)]
        work.append(_Blocks((nk, tr, cols), ref.dtype, moves, landed))
    return work


def _plan_grad_halves(grads, rxs, c, me):
    work = []
    for g, rx in zip(grads, rxs):
        lead, half, cols = rx.shape
        nk, tr = _stage_tile(lead, half, cols, g.dtype)
        moves = [(g.at[pl.ds(k, nk), _rows((1 - c) * half + b * tr, tr)], rx.at[pl.ds(k, nk), _rows(b * tr, tr)])
                 for k in range(0, lead, nk) for b in range(half // tr)]
        work.append(_Blocks((nk, tr, cols), g.dtype, moves, rx))
    return work


def _plan_whole(srcs, dsts, c, me):
    work = []
    for src, dst in zip(srcs, dsts):
        lead, rows, cols = src.shape
        nk, tr = _stage_tile(lead, rows, cols, src.dtype)
        moves = [(src.at[pl.ds(k, nk), _rows(b * tr, tr)], dst.at[pl.ds(k, nk), _rows(b * tr, tr)])
                 for k in range(0, lead, nk) for b in range(rows // tr)]
        work.append(_Blocks((nk, tr, cols), src.dtype, moves, dst))
    return work


def _push_staged(arrays, outs, in_place, plan, name):
    n, no = len(arrays), len(outs)

    def body(*refs):
        ins, out_refs = refs[:n], refs[n:n + no]
        rsem, idle = refs[n + no], refs[n + no + 1]
        x, y, c, me, _ = _place()
        peer = dict(device_id=(x, y, 1 - c), device_id_type=MESH)
        work = plan(ins, out_refs, c, me)
        for w, blocks in enumerate(work):
            def staged(buf, lsem, ssem, w=w, blocks=blocks):
                sends = []
                for i, (src, dst) in enumerate(blocks.moves):
                    slot = i % 2
                    if i >= 2:
                        sends[i - 2].wait_send()
                    load = pltpu.make_async_copy(src, buf.at[slot], lsem.at[slot])
                    load.start()
                    load.wait()
                    send = pltpu.make_async_remote_copy(buf.at[slot], dst, ssem.at[slot], rsem.at[w], **peer)
                    send.start()
                    sends.append(send)
                for send in sends[-2:]:
                    send.wait_send()

            pl.run_scoped(staged, pltpu.VMEM((2,) + blocks.tile, blocks.dtype), pltpu.SemaphoreType.DMA((2,)),
                          pltpu.SemaphoreType.DMA((2,)))
        for w, blocks in enumerate(work):
            pltpu.make_async_remote_copy(blocks.landed, blocks.landed, idle, rsem.at[w], **peer).wait_recv()

    return pl.pallas_call(body, out_shape=tuple(outs), in_specs=[ANY] * n, out_specs=tuple([ANY] * no),
                          input_output_aliases={w: w for w in range(n)} if in_place else {},
                          scratch_shapes=[pltpu.SemaphoreType.DMA((n,)), pltpu.SemaphoreType.DMA],
                          name=name, compiler_params=_params())(*arrays)


def _pair_sum(g, rx, core, name):
    lead, half, cols = rx.shape
    tr = _rows_block(half, cols)
    nrow = half // tr

    def body(core_ref, g_ref, rx_ref, o_ref):
        o_ref[...] = (g_ref[...].astype(F32) + rx_ref[...].astype(F32)).astype(o_ref.dtype)

    spec = pl.BlockSpec((None, tr, cols), lambda k, i, core_ref: (k, i, 0))
    grid_spec = pltpu.PrefetchScalarGridSpec(
        num_scalar_prefetch=1, grid=(lead, nrow),
        in_specs=[pl.BlockSpec((None, tr, cols), lambda k, i, core_ref: (k, core_ref[0] * nrow + i, 0)), spec],
        out_specs=spec)
    return pl.pallas_call(body, out_shape=jax.ShapeDtypeStruct(rx.shape, rx.dtype), grid_spec=grid_spec, name=name,
                          compiler_params=_params(("parallel", "parallel")))(core, g, rx)


def _push_sibling(src, name):
    lead, rows, cols = src.shape
    tr = _rows_block(rows, cols)
    nrow = rows // tr

    def body(blk_ref, rx_ref, ssem, rsem):
        x, y, c = lax.axis_index("x"), lax.axis_index("y"), lax.axis_index("c")
        k, i = pl.program_id(0), pl.program_id(1)
        peer = dict(device_id=(x, y, 1 - c), device_id_type=MESH)
        dst = rx_ref.at[pl.ds(k, 1), pl.ds(pl.multiple_of(i * tr, 8), tr)]
        cp = pltpu.make_async_remote_copy(blk_ref, dst, ssem, rsem, **peer)
        cp.start()
        cp.wait_send()

        @pl.when(jnp.logical_and(k == lead - 1, i == nrow - 1))
        def _():
            pltpu.make_async_remote_copy(rx_ref, rx_ref, ssem, rsem, **peer).wait_recv()

    return pl.pallas_call(body, out_shape=jax.ShapeDtypeStruct(src.shape, src.dtype), grid=(lead, nrow),
                          in_specs=[pl.BlockSpec((1, tr, cols), lambda k, i: (k, i, 0))], out_specs=ANY,
                          scratch_shapes=[pltpu.SemaphoreType.DMA, pltpu.SemaphoreType.DMA], name=name,
                          compiler_params=_params(("arbitrary", "arbitrary")))(src)


def _chip_scatter(pairs, small, name):
    n = len(pairs)

    def body(*refs):
        ins, small_ref = refs[:n], refs[n]
        outs, small_out = refs[n + 1:2 * n + 1], refs[2 * n + 1]
        ssem, rsem = refs[2 * n + 2:]
        x, y, c, me, chips = _place()
        copies = []
        for w in range(n + 1):
            for j, (px, py) in enumerate(chips):
                src = ins[w].at[2 * px + py] if w < n else small_ref
                dst = (outs[w] if w < n else small_out).at[j]
                cp = pltpu.make_async_remote_copy(src, dst, ssem.at[3 * w + j], rsem.at[3 * w + j],
                                                  device_id=(px, py, c), device_id_type=MESH)
                copies.append((cp, cp))
        _run_remote(copies)

    out_shape = tuple(jax.ShapeDtypeStruct((3,) + it.shape[1:], it.dtype) for it in pairs) + \
                (jax.ShapeDtypeStruct((3,) + small.shape, small.dtype),)
    return pl.pallas_call(body, out_shape=out_shape, in_specs=[ANY] * (n + 1), out_specs=tuple([ANY] * (n + 1)),
                          scratch_shapes=[pltpu.SemaphoreType.DMA((3 * n + 3,)),
                                          pltpu.SemaphoreType.DMA((3 * n + 3,))],
                          name=name)(*pairs, small)


def _pack(arrays):
    flat = jnp.concatenate([a.reshape(-1).astype(F32) for a in arrays])
    pad = (-flat.shape[0]) % (8 * LANES)
    return jnp.pad(flat, (0, pad)).reshape(-1, LANES)


def _unpack(packed, shapes):
    flat = packed.reshape(-1)
    out, off = [], 0
    for shp in shapes:
        size = math.prod(shp)
        out.append(flat[off:off + size].reshape(shp))
        off += size
    return out


def kernel(x, mem, g_mix, w_in, b_gate, a_ln_g, a_ln_b, a_ws, a_bs, b_conv, c_wg, c_scale, w_branch_a, w_branch_b, w_branch_c, w_o, g_xattn, g_mem, w_xq, w_xkv, w_xo, g_ffn, w_ffn_in, w_ffn_out, g_final, loss_target, m_g_mix, m_w_in, m_b_gate, m_a_ln_g, m_a_ln_b, m_a_ws, m_a_bs, m_b_conv, m_c_wg, m_c_scale, m_w_branch_a, m_w_branch_b, m_w_branch_c, m_w_o, m_g_xattn, m_g_mem, m_w_xq, m_w_xkv, m_w_xo, m_g_ffn, m_w_ffn_in, m_w_ffn_out, m_g_final, v_g_mix, v_w_in, v_b_gate, v_a_ln_g, v_a_ln_b, v_a_ws, v_a_bs, v_b_conv, v_c_wg, v_c_scale, v_w_branch_a, v_w_branch_b, v_w_branch_c, v_w_o, v_g_xattn, v_g_mem, v_w_xq, v_w_xkv, v_w_xo, v_g_ffn, v_w_ffn_in, v_w_ffn_out, v_g_final):
    args = locals()
    wts = {n: args[n] for n in WEIGHTS}
    mom = {n: args["m_" + n] for n in WEIGHTS}
    var = {n: args["v_" + n] for n in WEIGHTS}
    xs = x[0]
    mems = mem[0]
    tgt = loss_target[0]
    chip = 2 * lax.axis_index("x") + lax.axis_index("y")

    bias_pack = jnp.zeros((DEPTH, 8, 384), F32)
    bias_pack = bias_pack.at[:, 0:3, 0:256].set(b_gate).at[:, 0:3, 256:384].set(b_conv)
    bias_slots = lax.dynamic_update_slice(jnp.zeros((N_CHIPS, DEPTH, 8, 384), F32), bias_pack[None], (chip, 0, 0, 0))
    core = lax.axis_index("c")
    chip_arr = jnp.reshape(chip, (1,)).astype(jnp.int32)
    core_arr = jnp.reshape(core, (1,)).astype(jnp.int32)
    slots = {n: _cast_to_slots(wts[n], chip_arr, name=f"cast_{n}") for n in BIG}
    layer_weights = [{}, {}]

    def land(l, names, gathered, tag):
        shapes = [jax.ShapeDtypeStruct(g.shape, g.dtype) for g in gathered]
        for n, g in zip(names, _push_staged(list(gathered), shapes, True, _plan_gathered,
                                            name=f"l{l}_gather_cores_{tag}")):
            if n in READ_AS_SLABS:
                layer_weights[l][n] = g
            elif n in COL_SHARDED:
                layer_weights[l][n] = jnp.transpose(g, (1, 0, 2)).reshape(1, g.shape[1], N_CHIPS * g.shape[2])
            else:
                layer_weights[l][n] = g.reshape(1, N_CHIPS * g.shape[1], g.shape[2])

    def gather_ride(l, names):
        return _gather_carry([slots[n][l] for n in names])

    rest_of_first = tuple(n for n in BIG if n not in NEED_FIRST)
    most_of_second = tuple(n for n in BIG if n not in NEED_LAST)
    first = _chip_gather([slots[n][0] for n in NEED_FIRST] + [bias_slots], 1, name="l0_gather_chips")
    land(0, NEED_FIRST, first[:-1], "first")
    biases = first[-1]
    b_gate_full = jnp.transpose(biases[:, :, 0:3, 0:256], (1, 2, 0, 3)).reshape(DEPTH, 3, D_MODEL)
    b_conv_full = jnp.transpose(biases[:, :, 0:3, 256:384], (1, 2, 0, 3)).reshape(DEPTH, 3, B_WIDTH)

    tril = jnp.tril(jnp.ones((CHUNK, CHUNK), bool))

    def mixer_params(l):
        wtril = jnp.where(tril[None], a_ws[l], 0.0)
        prm = (a_ln_g[l][None], a_ln_b[l][None], wtril.astype(BF16),
               jnp.broadcast_to(a_bs[l][:, :, None], (A_GROUPS, CHUNK, CHUNK)), b_conv_full[l],
               c_wg[l].astype(BF16), c_scale[l][None])
        return prm, jnp.swapaxes(wtril, 1, 2).astype(BF16)

    saved = []
    xc = xs
    for l in range(DEPTH):
        sv = {"x0": xc}
        prm, _ = mixer_params(l)
        fw = layer_weights[l]
        riding = rest_of_first if l == 0 else NEED_LAST
        out = _norm_mm(xc, g_mix[l][None], fw["w_in"], None, tn=1536, name=f"l{l}_proj", carry=gather_ride(l, riding))
        sv["h"], sv["proj"] = out[0], out[1]
        land(l, riding, out[2], "rest")
        sv["ypre"] = _mixer_fwd(sv["proj"], prm, name=f"l{l}_mixers")
        sv["merged"] = _merge_fwd(sv["ypre"], sv["proj"], fw["w_branch_a"][0], fw["w_branch_b"][0],
                                  fw["w_branch_c"][0], b_gate_full[l], name=f"l{l}_merge")
        xc = _mm(sv["merged"], fw["w_o"], b_lead=0, mode="nn", out_dtype=F32, name=f"l{l}_mix_out", tm=1024, tn=1024,
                 tk=1024, res=xc)
        sv["x1"] = xc
        sv["hq"], sv["q"] = _norm_mm(xc, g_xattn[l][None], fw["w_xq"], 0, tn=1024, name=f"l{l}_q")
        sv["m"] = _rms_fwd(mems, g_mem[l][None], name=f"l{l}_mem_norm")
        sv["kv"] = _mm(sv["m"], fw["w_xkv"], b_lead=0, mode="nn", out_dtype=BF16, name=f"l{l}_kv", tm=256, tn=512,
                       tk=1024)
        sv["o"] = _attn_fwd(sv["q"], sv["kv"], name=f"l{l}_attn")
        xc = _mm(sv["o"], fw["w_xo"], b_lead=0, mode="nn", out_dtype=F32, name=f"l{l}_xattn_out", tm=1024, tn=1024,
                 tk=1024, res=xc)
        sv["x2"] = xc
        out2 = _norm_ffn_in(xc, g_ffn[l][None], fw["w_ffn_in"], None, name=f"l{l}_ffn_in",
                            carry=gather_ride(1, most_of_second) if l == 0 else None)
        sv["h2"], sv["gu"], sv["act"] = out2[0], out2[1], out2[2]
        if l == 0:
            land(1, most_of_second, out2[3], "most")
        xc = _mm(sv["act"], fw["w_ffn_out"], b_lead=0, mode="nn", out_dtype=F32, name=f"l{l}_ffn_out", tm=1024,
                 tn=512, tk=FFN_HIDDEN, res=xc)
        saved.append(sv)

    loss_part, dx, dg_final = _final_loss(xc, g_final[None], tgt, name="final_loss")
    loss = lax.psum(loss_part[0, 0], ("x", "y", "c"))

    pairs = [{}, {}]
    received = [{}, {}]
    small_grads = [None] * DEPTH

    def core_pairs(l, names, grads, tag):
        halves = [jax.ShapeDtypeStruct((g.shape[0], g.shape[1] // 2, g.shape[2]), g.dtype) for g in grads]
        came = _push_staged(grads, halves, False, _plan_grad_halves, name=f"l{l}_reduce_cores_{tag}")
        for n, g, rx in zip(names, grads, came):
            pairs[l][n] = _pair_sum(g, rx, core_arr, name=f"l{l}_pair_sum_{n}")

    def scatter_ride(l, names, when):
        return _scatter_carry([pairs[l][n] for n in names]) if when else None

    ride_second = tuple(n for n in BIG if n not in RIDE_FIRST)
    early, early_big, early_small = NEED_LAST + ("w_xo", "w_xq", "w_xkv"), NEED_LAST + ("w_xkv",), ("w_xo", "w_xq")
    middle = ("w_o", "w_branch_a", "w_branch_b", "w_branch_c")
    for l in reversed(range(DEPTH)):
        sv = saved[l]
        prm, wtril_t = mixer_params(l)
        gb = {}
        sg = {}
        fw = layer_weights[l]
        dgu = _d_act_swiglu(dx, fw["w_ffn_out"], 0, sv["gu"], name=f"l{l}_d_act")
        gb["w_ffn_out"] = _mm(sv["act"], dx, mode="tn", out_dtype=BF16, name=f"l{l}_dw_ffn_out", tm=1408, tn=1024,
                              tk=TOKEN_STEP).reshape(N_CHIPS, FFN_HIDDEN // N_CHIPS, D_MODEL)
        out = _mm(sv["h2"], dgu, mode="tn", out_dtype=BF16, name=f"l{l}_dw_ffn_in", tm=1024, tn=FFN_TILE,
                  tk=TOKEN_STEP, b_halves=True, out_shards=N_CHIPS, carry=scatter_ride(1, RIDE_FIRST, l == 0))
        gb["w_ffn_in"] = out[0] if l == 0 else out
        out2 = _mm_nt_norm_bwd(dgu, fw["w_ffn_in"], None, sv["x2"], g_ffn[l][None], dx, tm=512, a_halves=True,
                               name=f"l{l}_d_h2", carry=scatter_ride(1, ride_second, l == 0))
        dx, sg["g_ffn"] = out2[0], out2[1]
        if l == 0:
            received[1].update(zip(RIDE_FIRST, out[1]))
            received[1].update(zip(ride_second, out2[2]))
        do = _mm(dx, fw["w_xo"], b_lead=0, mode="nt", out_dtype=BF16, name=f"l{l}_d_o", tm=1024, tn=1024, tk=1024)
        gb["w_xo"] = _mm(sv["o"], dx, mode="tn", out_dtype=BF16, name=f"l{l}_dw_xo", tm=1024, tn=1024,
                         tk=TOKEN_STEP).reshape(N_CHIPS, D_MODEL // N_CHIPS, D_MODEL)
        dq, dkv = _attn_bwd(sv["q"], sv["kv"], do, name=f"l{l}_d_attn")
        gb["w_xq"] = _mm(sv["hq"], dq, mode="tn", out_dtype=BF16, name=f"l{l}_dw_xq", tm=1024, tn=1024,
                         tk=TOKEN_STEP).reshape(N_CHIPS, D_MODEL // N_CHIPS, D_MODEL)
        dm = _mm(dkv, fw["w_xkv"], b_lead=0, mode="nt", out_dtype=BF16, name=f"l{l}_d_m", tm=256, tn=1024, tk=1024)
        gb["w_xkv"] = _mm(sv["m"], dkv, mode="tn", out_dtype=BF16, name=f"l{l}_dw_xkv", tm=1024, tn=512, tk=256,
                          out_shards=N_CHIPS)
        _, sg["g_mem"] = _rms_bwd(mems, g_mem[l][None], dm, None, name=f"l{l}_d_mem_norm")
        if l == 0:
            core_pairs(0, early, [gb[n] for n in early], "early")
        dx, sg["g_xattn"] = _mm_nt_norm_bwd(dq, fw["w_xq"], 0, sv["x1"], g_xattn[l][None], dx, tm=1024,
                                            name=f"l{l}_d_hq")
        dmerged = _mm(dx, fw["w_o"], b_lead=0, mode="nt", out_dtype=BF16, name=f"l{l}_d_merged", tm=1024, tn=1024,
                      tk=1024)
        gb["w_o"] = _mm(sv["merged"], dx, mode="tn", out_dtype=BF16, name=f"l{l}_dw_o", tm=1024, tn=1024,
                        tk=TOKEN_STEP).reshape(N_CHIPS, D_MODEL // N_CHIPS, D_MODEL)
        out3 = _merge_bwd(sv["ypre"], sv["proj"], dmerged, fw["w_branch_a"][0], fw["w_branch_b"][0],
                          fw["w_branch_c"][0], b_gate_full[l], name=f"l{l}_d_merge",
                          carry=scatter_ride(0, early_big, l == 0))
        dypre, dproj, sg["b_gate"] = out3[0], out3[1], out3[2]
        for n, dw in zip(("w_branch_a", "w_branch_b", "w_branch_c"), out3[3:6]):
            gb[n] = jnp.transpose(dw.reshape(A_WIDTH, N_CHIPS, D_MODEL // N_CHIPS), (1, 0, 2)).astype(BF16)
        if l == 0:
            core_pairs(0, middle, [gb[n] for n in middle], "middle")
        out5 = _mixer_bwd(sv["proj"], dypre, dproj, prm, wtril_t, name=f"l{l}_d_mixers",
                          carry=scatter_ride(0, middle, l == 0))
        (dproj, sg["a_ln_g"], sg["a_ln_b"], sg["a_ws"], sg["a_bs"], sg["b_conv"], sg["c_wg"], sg["c_scale"]) = out5[:8]
        out4 = _mm(sv["h"], dproj, mode="tn", out_dtype=BF16, name=f"l{l}_dw_in", tm=1024, tn=1536, tk=TOKEN_STEP,
                   out_shards=N_CHIPS, carry=scatter_ride(0, early_small, l == 0))
        gb["w_in"] = out4[0] if l == 0 else out4
        if l == 0:
            core_pairs(0, ("w_in",), [gb["w_in"]], "last")
        out6 = _mm_nt_norm_bwd(dproj, fw["w_in"], None, sv["x0"], g_mix[l][None], dx, tm=512, name=f"l{l}_d_h",
                               carry=scatter_ride(0, ("w_in",), l == 0))
        dx, sg["g_mix"] = out6[0], out6[1]
        if l == 0:
            received[0].update(zip(early_big, out3[6]))
            received[0].update(zip(middle, out5[8]))
            received[0].update(zip(early_small, out4[1]))
            received[0].update(zip(("w_in",), out6[2]))
        else:
            core_pairs(1, BIG, [gb[n] for n in BIG], "all")
        small_grads[l] = sg
    grad_x = dx[None]

    small_names = [n for n in WEIGHTS if n not in BIG]
    small_full_shapes = {n: ((DEPTH, 3, D_MODEL) if n == "b_gate" else (DEPTH, 3, B_WIDTH) if n == "b_conv"
                             else wts[n].shape) for n in small_names}
    small_local = []
    for n in small_names:
        if n == "g_final":
            small_local.append(dg_final)
        else:
            small_local.append(jnp.stack([small_grads[l][n].reshape(small_full_shapes[n][1:]) for l in range(DEPTH)]))
    small_pack = _pack(small_local)

    small_rx = _push_sibling(small_pack[None], name="reduce_cores_small")
    small_pair = _elementwise(lambda a, b: (a + b,), [small_pack[None], small_rx], [F32], name="pair_sum_small")[0][0]
    small_came = _chip_scatter([], small_pair, name="reduce_chips_small")[0]
    small_sum = _chip_sum_ordered(small_pair, small_came, chip_arr, name="chip_sum_small")

    out_g, out_d, out_m, out_v = {}, {}, {}, {}
    mine = [_chip_sum(pairs[0][n], received[0][n], pairs[1][n], received[1][n], chip_arr, name=f"chip_sum_{n}")
            for n in BIG]
    other = _push_staged(mine, [jax.ShapeDtypeStruct(g.shape, g.dtype) for g in mine], False, _plan_whole,
                         name="share_cores")
    for i, n in enumerate(BIG):
        out_g[n], out_d[n], out_m[n], out_v[n] = _adamw_shard(wts[n], mine[i], other[i], mom[n], var[n], core_arr,
                                                              name=f"adamw_{n}")
    small_g = dict(zip(small_names, _unpack(small_sum, [small_full_shapes[n] for n in small_names])))
    small_g["b_gate"] = lax.dynamic_slice_in_dim(small_g["b_gate"], chip * 256, 256, axis=2)
    small_g["b_conv"] = lax.dynamic_slice_in_dim(small_g["b_conv"], chip * 128, 128, axis=2)
    def two_d(a):
        return a.reshape(1, -1) if a.ndim == 1 else a

    upd = _adamw_small(*[[two_d(d[n]) for n in small_names] for d in (wts, small_g, mom, var)], name="adamw_small")
    for n, d, m_new, v_new in zip(small_names, *upd):
        shape = wts[n].shape
        out_g[n], out_d[n], out_m[n], out_v[n] = small_g[n], d.reshape(shape), m_new.reshape(shape), v_new.reshape(shape)

    return (loss, grad_x, *[out_g[n] for n in WEIGHTS], *[out_d[n] for n in WEIGHTS], *[out_m[n] for n in WEIGHTS],
            *[out_v[n] for n in WEIGHTS])
```

```python
import functools
import math
from typing import Callable, NamedTuple

import jax
import jax.numpy as jnp
from jax import lax
from jax.experimental import pallas as pl
from jax.experimental.pallas import tpu as pltpu

F32 = jnp.float32
BF16 = jnp.bfloat16

D_MODEL = 1024
DEPTH = 2
MEM_LEN = 256
EPS = 1e-6
CHUNK = 128
A_GROUPS = 4
A_WIDTH = 512
B_WIDTH = 512
C_WIDTH = 512
C_GROUP = 128
POOL_WINDOWS = (2, 4, 8, 16)
HALO = 16
MIXER_ROWS = 2 * CHUNK
IN_COLS = 6144
MIX_COLS = 3072
XATTN_HEADS = 4
HEAD_DIM = 256
FFN_HIDDEN = 2816
N_CHIPS = 4

ADAM_LR = 0.001
ADAM_B1 = 0.9
ADAM_B2 = 0.999
ADAM_EPS = 1e-08
ADAM_WD = 0.01
ADAM_STEP = 10

V7X_VMEM_BYTES = 64 * 1024 * 1024
VMEM_LIMIT = (V7X_VMEM_BYTES * 3) // 4
LANES = 128
MESH = pl.DeviceIdType.MESH
ANY = pl.BlockSpec(memory_space=pl.ANY)

BIG = ("w_in", "w_branch_a", "w_branch_b", "w_branch_c", "w_o", "w_xq", "w_xkv", "w_xo", "w_ffn_in", "w_ffn_out")
COL_SHARDED = ("w_in", "w_branch_a", "w_branch_b", "w_branch_c", "w_xkv", "w_ffn_in")
RIDE_FIRST = ("w_in", "w_ffn_out", "w_branch_a", "w_branch_b", "w_branch_c")
NEED_FIRST = ("w_in", "w_branch_a", "w_branch_b", "w_branch_c", "w_o")
NEED_LAST = ("w_ffn_in", "w_ffn_out")
READ_AS_SLABS = ("w_in", "w_ffn_in")
SMALL_REPL = ("g_mix", "a_ln_g", "a_ln_b", "a_ws", "a_bs", "c_wg", "c_scale", "g_xattn", "g_mem", "g_ffn", "g_final")
SMALL_SHARDED = ("b_gate", "b_conv")
WEIGHTS = ("g_mix", "w_in", "b_gate", "a_ln_g", "a_ln_b", "a_ws", "a_bs", "b_conv", "c_wg", "c_scale", "w_branch_a",
           "w_branch_b", "w_branch_c", "w_o", "g_xattn", "g_mem", "w_xq", "w_xkv", "w_xo", "g_ffn", "w_ffn_in",
           "w_ffn_out", "g_final")


def _params(sem=None):
    return pltpu.CompilerParams(dimension_semantics=sem, vmem_limit_bytes=VMEM_LIMIT)


def _blk(dim, pref):
    return pref if dim % pref == 0 else dim


class _Carry(NamedTuple):
    arrays: tuple
    outs: tuple
    in_place: bool
    copies: int
    issue: Callable


def _call_with_carry(body, *, out_shape, grid, in_specs, out_specs, scratch_shapes, name, semantics, operands, carry,
                     aliases=None):
    own_aliases = dict(aliases or {})
    if carry is None:
        return pl.pallas_call(body, out_shape=tuple(out_shape), grid=grid, in_specs=list(in_specs),
                              out_specs=tuple(out_specs), scratch_shapes=list(scratch_shapes), name=name,
                              input_output_aliases=own_aliases, compiler_params=_params(semantics))(*operands)
    n_in, n_out, na, no = len(operands), len(out_shape), len(carry.arrays), len(carry.outs)

    def carried(*refs):
        ins, cins = refs[:n_in], refs[n_in:n_in + na]
        outs, couts = refs[n_in + na:n_in + na + n_out], refs[n_in + na + n_out:n_in + na + n_out + no]
        rest = refs[n_in + na + n_out + no:]
        scratch, ssem, rsem = rest[:-2], rest[-2], rest[-1]
        first = functools.reduce(jnp.logical_and, [pl.program_id(ax) == 0 for ax in range(len(grid))])
        last = functools.reduce(jnp.logical_and, [pl.program_id(ax) == grid[ax] - 1 for ax in range(len(grid))])

        @pl.when(first)
        def _():
            for send, _ in carry.issue(cins, couts, ssem, rsem):
                send.start()

        body(*ins, *outs, *scratch)

        @pl.when(last)
        def _():
            for send, recv in carry.issue(cins, couts, ssem, rsem):
                send.wait_send()
                recv.wait_recv()

    aliases = dict(own_aliases)
    if carry.in_place:
        aliases.update({n_in + i: n_out + i for i in range(na)})
    sems = [pltpu.SemaphoreType.DMA((carry.copies,)), pltpu.SemaphoreType.DMA((carry.copies,))]
    return pl.pallas_call(carried, out_shape=tuple(out_shape) + tuple(carry.outs), grid=grid,
                          in_specs=list(in_specs) + [ANY] * na, out_specs=tuple(out_specs) + tuple([ANY] * no),
                          scratch_shapes=list(scratch_shapes) + sems, input_output_aliases=aliases, name=name,
                          compiler_params=_params(("arbitrary",) * len(grid)))(*operands, *carry.arrays)


def _mm(a, b, *, mode, out_dtype, name, tm, tn, tk, res=None, b_lead=None, b_halves=False, out_shards=None,
        carry=None):
    dims = {"nn": (((1,), (0,)), ((), ())), "nt": (((1,), (1,)), ((), ())), "tn": (((0,), (0,)), ((), ()))}[mode]
    b_rows, b_last = b.shape[-2], b.shape[-1]
    if mode == "nn":
        m, k, n = a.shape[0], a.shape[1], b_last
        assert b_rows == k
    elif mode == "nt":
        m, k, n = a.shape[0], a.shape[1], b_rows
        assert b_last == k
    else:
        k, m, n = a.shape[0], a.shape[1], (2 * b_last if b_halves else b_last)
        assert b_rows == k
    tm, tn, tk = _blk(m, tm), _blk(n, tn), _blk(k, tk)
    nk = k // tk
    grid = (m // tm, n // tn, nk)

    if mode == "tn":
        a_spec = pl.BlockSpec((tk, tm), lambda i, j, kk: (kk, i))
        b_block, b_idx = (tk, tn), (lambda i, j, kk: (kk, j))
    else:
        a_spec = pl.BlockSpec((tm, tk), lambda i, j, kk: (i, kk))
        if mode == "nn":
            b_block, b_idx = (tk, tn), (lambda i, j, kk: (kk, j))
        else:
            b_block, b_idx = (tn, tk), (lambda i, j, kk: (j, kk))
    if b_halves:
        assert mode == "tn" and b_lead is None and b_last % tn == 0
        per_half = b_last // tn
        b_spec = pl.BlockSpec((None,) + b_block, lambda i, j, kk: (j // per_half, kk, j % per_half))
    elif b_lead is None:
        b_spec = pl.BlockSpec(b_block, b_idx)
    else:
        b_spec = pl.BlockSpec((None,) + b_block, lambda i, j, kk: (b_lead,) + b_idx(i, j, kk))
    in_specs = [a_spec, b_spec]
    operands = [a, b]
    if res is not None:
        in_specs.append(pl.BlockSpec((tm, tn), lambda i, j, kk: (i, j)))
        operands.append(res)
    if out_shards is None:
        out_shape = jax.ShapeDtypeStruct((m, n), out_dtype)
        out_spec = pl.BlockSpec((tm, tn), lambda i, j, kk: (i, j))
    else:
        per = n // out_shards
        assert per % tn == 0
        nps = per // tn
        out_shape = jax.ShapeDtypeStruct((out_shards, m, per), out_dtype)
        out_spec = pl.BlockSpec((None, tm, tn), lambda i, j, kk: (j // nps, i, j % nps))

    def body(*refs):
        a_ref, b_ref = refs[0], refs[1]
        res_ref = refs[2] if res is not None else None
        o_ref = refs[3] if res is not None else refs[2]
        part = lax.dot_general(a_ref[...].astype(BF16), b_ref[...].astype(BF16), dims, preferred_element_type=F32)

        def finish(acc):
            if res_ref is not None:
                acc = acc + res_ref[...]
            o_ref[...] = acc.astype(out_dtype)

        if nk == 1:
            finish(part)
        else:
            acc_ref = refs[-1]
            kk = pl.program_id(2)

            @pl.when(kk == 0)
            def _():
                acc_ref[...] = part

            @pl.when(kk > 0)
            def _():
                acc_ref[...] += part

            @pl.when(kk == nk - 1)
            def _():
                finish(acc_ref[...])

    scratch = [] if nk == 1 else [pltpu.VMEM((tm, tn), F32)]
    outs = _call_with_carry(body, out_shape=(out_shape,), grid=grid, in_specs=in_specs, out_specs=(out_spec,),
                            scratch_shapes=scratch, name=name, semantics=("parallel", "parallel", "arbitrary"),
                            operands=operands, carry=carry)
    return outs[0] if carry is None else (outs[0], outs[1:])


ROW_CHUNK = 256
FFN_TILE = 1408
TOKEN_STEP = 2048
NT_DIMS = (((1,), (1,)), ((), ()))


def _norm_rows_into(x_ref, g_ref, h_ref, hs_ref, tm):
    rc = min(ROW_CHUNK, tm)
    for r0 in range(0, tm, rc):
        xv = x_ref[r0:r0 + rc, :]
        r = lax.rsqrt(jnp.mean(xv * xv, axis=-1, keepdims=True) + EPS)
        hv = (xv * r * g_ref[...]).astype(BF16)
        hs_ref[r0:r0 + rc, :] = hv
        if h_ref is not None:
            h_ref[r0:r0 + rc, :] = hv


def _norm_mm(x, g, b, b_lead, *, tn, name, carry=None):
    s, d = x.shape
    slabs = b_lead is None
    n = b.shape[0] * b.shape[-1] if slabs else b.shape[-1]
    tm, tn = _blk(s, 1024), (b.shape[-1] if slabs else _blk(n, tn))
    b_index = (lambda i, j: (j, 0, 0)) if slabs else (lambda i, j: (b_lead, 0, j))

    def body(x_ref, g_ref, b_ref, h_ref, o_ref, hs_ref):
        @pl.when(pl.program_id(1) == 0)
        def _():
            _norm_rows_into(x_ref, g_ref, h_ref, hs_ref, tm)

        o_ref[...] = jnp.dot(hs_ref[...], b_ref[...], preferred_element_type=F32).astype(BF16)

    row = pl.BlockSpec((tm, d), lambda i, j: (i, 0))
    outs = _call_with_carry(body, out_shape=(jax.ShapeDtypeStruct((s, d), BF16), jax.ShapeDtypeStruct((s, n), BF16)),
                            grid=(s // tm, n // tn),
                            in_specs=[row, pl.BlockSpec((1, d), lambda i, j: (0, 0)),
                                      pl.BlockSpec((None, d, tn), b_index)],
                            out_specs=(row, pl.BlockSpec((tm, tn), lambda i, j: (i, j))),
                            scratch_shapes=[pltpu.VMEM((tm, d), BF16)], name=name,
                            semantics=("parallel", "arbitrary"), operands=[x, g, b], carry=carry)
    return outs if carry is None else (outs[0], outs[1], outs[2:])


def _norm_ffn_in(x, g, b, b_lead, name, carry=None):
    s, d = x.shape
    tm, tn = _blk(s, 512), FFN_TILE
    nj = FFN_HIDDEN // tn
    if b_lead is None:
        assert b.shape == (2 * nj, d, tn)
        gate_index, up_index = (lambda j, i: (j, 0, 0)), (lambda j, i: (j + nj, 0, 0))
    else:
        gate_index, up_index = (lambda j, i: (b_lead, 0, j)), (lambda j, i: (b_lead, 0, j + nj))

    n_rows = s // tm

    def body(x_ref, g_ref, bg_ref, bu_ref, h_ref, gu_ref, act_ref, hs_ref):
        first_tile = pl.program_id(0) == 0

        @pl.when(first_tile)
        def _():
            _norm_rows_into(x_ref, g_ref, h_ref, hs_ref, tm)

        @pl.when(jnp.logical_not(first_tile))
        def _():
            _norm_rows_into(x_ref, g_ref, None, hs_ref, tm)

        hv = hs_ref[...]
        gate = jnp.dot(hv, bg_ref[...], preferred_element_type=F32)
        up = jnp.dot(hv, bu_ref[...], preferred_element_type=F32)
        gu_ref[0] = gate.astype(BF16)
        gu_ref[1] = up.astype(BF16)
        act_ref[...] = (gate * _sigmoid(gate) * up).astype(BF16)

    row = pl.BlockSpec((tm, d), lambda j, i: (i, 0))
    out_shape = (jax.ShapeDtypeStruct((s, d), BF16), jax.ShapeDtypeStruct((2, s, FFN_HIDDEN), BF16),
                 jax.ShapeDtypeStruct((s, FFN_HIDDEN), BF16))
    outs = _call_with_carry(body, out_shape=out_shape, grid=(nj, s // tm),
                            in_specs=[row, pl.BlockSpec((1, d), lambda j, i: (0, 0)),
                                      pl.BlockSpec((None, d, tn), gate_index), pl.BlockSpec((None, d, tn), up_index)],
                            out_specs=(pl.BlockSpec((tm, d), lambda j, i: (jnp.where(j == 0, i, n_rows - 1), 0)),
                                       pl.BlockSpec((2, tm, tn), lambda j, i: (0, i, j)),
                                       pl.BlockSpec((tm, tn), lambda j, i: (i, j))),
                            scratch_shapes=[pltpu.VMEM((tm, d), BF16)], name=name,
                            semantics=("arbitrary", "arbitrary"), operands=[x, g, b, b], carry=carry)
    return outs if carry is None else (outs[0], outs[1], outs[2], outs[3:])


def _d_act_swiglu(dx, w, b_lead, gu, name):
    s, d = dx.shape
    tm, tn = _blk(s, 512), FFN_TILE

    def body(dx_ref, w_ref, gu_ref, o_ref):
        dact = lax.dot_general(dx_ref[...].astype(BF16), w_ref[...], NT_DIMS, preferred_element_type=F32)
        gate = gu_ref[0].astype(F32)
        up = gu_ref[1].astype(F32)
        sg = _sigmoid(gate)
        o_ref[0] = (dact * up * sg * (1.0 + gate * (1.0 - sg))).astype(BF16)
        o_ref[1] = (dact * gate * sg).astype(BF16)

    halves = pl.BlockSpec((2, tm, tn), lambda j, i: (0, i, j))
    return pl.pallas_call(body, out_shape=jax.ShapeDtypeStruct(gu.shape, BF16), grid=(FFN_HIDDEN // tn, s // tm),
                          in_specs=[pl.BlockSpec((tm, d), lambda j, i: (i, 0)),
                                    pl.BlockSpec((None, tn, d), lambda j, i: (b_lead, j, 0)), halves],
                          out_specs=halves, name=name,
                          compiler_params=_params(("parallel", "parallel")))(dx, w, gu)


def _mm_nt_norm_bwd(a, b, b_lead, x, g, dres, *, tm, name, a_halves=False, carry=None):
    s, d = x.shape
    slabs = b_lead is None
    nslab, per = (b.shape[0], b.shape[-1]) if slabs else (1, b.shape[-1])
    kdim = nslab * per
    tm = _blk(s, tm)
    rc = min(ROW_CHUNK, tm)
    if a_halves:
        a_spec = pl.BlockSpec((2, tm, kdim // 2), lambda i: (0, i, 0))
    else:
        a_spec = pl.BlockSpec((tm, kdim), lambda i: (i, 0))

    def body(a_ref, b_ref, x_ref, g_ref, r_ref, dx_ref, dg_ref, *scratch):
        i = pl.program_id(0)
        if slabs:
            w_ref, sems = scratch

            @pl.when(i == 0)
            def _():
                copies = [pltpu.make_async_copy(b_ref.at[k], w_ref.at[:, pl.ds(k * per, per)], sems.at[k])
                          for k in range(nslab)]
                for cp in copies:
                    cp.start()
                for cp in copies:
                    cp.wait()
        else:
            w_ref = b_ref
        av = jnp.concatenate([a_ref[0], a_ref[1]], axis=1) if a_halves else a_ref[...]
        dh = lax.dot_general(av.astype(BF16), w_ref[...], NT_DIMS, preferred_element_type=F32)
        gv = g_ref[...]
        dg_part = None
        for r0 in range(0, tm, rc):
            xv = x_ref[r0:r0 + rc, :]
            dhv = dh[r0:r0 + rc, :]
            r = lax.rsqrt(jnp.mean(xv * xv, axis=-1, keepdims=True) + EPS)
            xhat = xv * r
            p = jnp.sum(dhv * xhat, axis=0, keepdims=True)
            dg_part = p if dg_part is None else dg_part + p
            dxhat = dhv * gv
            dx_ref[r0:r0 + rc, :] = r_ref[r0:r0 + rc, :] + r * (
                dxhat - xhat * jnp.mean(dxhat * xhat, axis=-1, keepdims=True))

        @pl.when(i == 0)
        def _():
            dg_ref[...] = dg_part

        @pl.when(i > 0)
        def _():
            dg_ref[...] += dg_part

    row = pl.BlockSpec((tm, d), lambda i: (i, 0))
    vec = pl.BlockSpec((1, d), lambda i: (0, 0))
    if slabs:
        b_spec, scratch = ANY, [pltpu.VMEM((d, kdim), BF16), pltpu.SemaphoreType.DMA((nslab,))]
    else:
        b_spec, scratch = pl.BlockSpec((None, d, kdim), lambda i: (b_lead, 0, 0), pipeline_mode=pl.Buffered(1)), []
    outs = _call_with_carry(body, out_shape=(jax.ShapeDtypeStruct((s, d), F32), jax.ShapeDtypeStruct((1, d), F32)),
                            grid=(s // tm,), in_specs=[a_spec, b_spec, row, vec, row], out_specs=(row, vec),
                            scratch_shapes=scratch, name=name, semantics=("arbitrary",), operands=[a, b, x, g, dres],
                            carry=carry)
    return outs if carry is None else (outs[0], outs[1], outs[2:])


def _rms_fwd(x, g, name):
    s, d = x.shape
    tm = _blk(s, 512)

    def body(x_ref, g_ref, o_ref):
        xv = x_ref[...]
        r = lax.rsqrt(jnp.mean(xv * xv, axis=-1, keepdims=True) + EPS)
        o_ref[...] = (xv * r * g_ref[...]).astype(BF16)

    return pl.pallas_call(body, out_shape=jax.ShapeDtypeStruct((s, d), BF16), grid=(s // tm,),
                          in_specs=[pl.BlockSpec((tm, d), lambda i: (i, 0)), pl.BlockSpec((1, d), lambda i: (0, 0))],
                          out_specs=pl.BlockSpec((tm, d), lambda i: (i, 0)), name=name,
                          compiler_params=_params(("parallel",)))(x, g)


def _rms_bwd(x, g, dh, dres, name):
    s, d = x.shape
    tm = _blk(s, 512)
    has_res = dres is not None

    def body(*refs):
        x_ref, g_ref, dh_ref = refs[0], refs[1], refs[2]
        dx_ref, dg_ref = refs[-2], refs[-1]
        xv = x_ref[...]
        r = lax.rsqrt(jnp.mean(xv * xv, axis=-1, keepdims=True) + EPS)
        xhat = xv * r
        dhv = dh_ref[...].astype(F32)
        part = jnp.sum(dhv * xhat, axis=0, keepdims=True)

        @pl.when(pl.program_id(0) == 0)
        def _():
            dg_ref[...] = part

        @pl.when(pl.program_id(0) > 0)
        def _():
            dg_ref[...] += part

        dxhat = dhv * g_ref[...]
        dx = r * (dxhat - xhat * jnp.mean(dxhat * xhat, axis=-1, keepdims=True))
        if has_res:
            dx = dx + refs[3][...]
        dx_ref[...] = dx

    row = pl.BlockSpec((tm, d), lambda i: (i, 0))
    vec = pl.BlockSpec((1, d), lambda i: (0, 0))
    in_specs = [row, vec, row] + ([row] if has_res else [])
    operands = [x, g, dh] + ([dres] if has_res else [])
    return pl.pallas_call(body, out_shape=(jax.ShapeDtypeStruct((s, d), F32), jax.ShapeDtypeStruct((1, d), F32)),
                          grid=(s // tm,), in_specs=in_specs, out_specs=(row, vec), name=name,
                          compiler_params=_params(("arbitrary",)))(*operands)


def _final_loss(x, g, target, name):
    s, d = x.shape
    tm = _blk(s, 512)

    def body(x_ref, g_ref, t_ref, loss_ref, dx_ref, dg_ref):
        xv = x_ref[...]
        gv = g_ref[...]
        r = lax.rsqrt(jnp.mean(xv * xv, axis=-1, keepdims=True) + EPS)
        xhat = xv * r
        err = xhat * gv - t_ref[...]
        lpart = 0.5 * jnp.sum(jnp.mean(err * err, axis=-1, keepdims=True), axis=0, keepdims=True)
        dy = err * (1.0 / d)
        gpart = jnp.sum(dy * xhat, axis=0, keepdims=True)

        @pl.when(pl.program_id(0) == 0)
        def _():
            loss_ref[...] = lpart
            dg_ref[...] = gpart

        @pl.when(pl.program_id(0) > 0)
        def _():
            loss_ref[...] += lpart
            dg_ref[...] += gpart

        dxhat = dy * gv
        dx_ref[...] = r * (dxhat - xhat * jnp.mean(dxhat * xhat, axis=-1, keepdims=True))

    row = pl.BlockSpec((tm, d), lambda i: (i, 0))
    vec = pl.BlockSpec((1, d), lambda i: (0, 0))
    one = pl.BlockSpec((1, 1), lambda i: (0, 0))
    return pl.pallas_call(body, out_shape=(jax.ShapeDtypeStruct((1, 1), F32), jax.ShapeDtypeStruct((s, d), F32),
                                           jax.ShapeDtypeStruct((1, d), F32)),
                          grid=(s // tm,), in_specs=[row, vec, row], out_specs=(one, row, vec), name=name,
                          compiler_params=_params(("arbitrary",)))(x, g, target)


def _erf_parts(x):
    cdf = 0.5 * (1.0 + lax.erf(x * (1.0 / math.sqrt(2.0))))
    return cdf


def _shift_down(ext, k):
    return pltpu.roll(ext, k, 0)


def _shift_up(ext, k):
    return pltpu.roll(ext, ext.shape[0] - k, 0)


def _mixer_forward_math(z, halo, row0, prm, tm):
    ln_g, ln_b, wtril, bsb, conv_w, wg, scale = prm
    za = z[:, 0:2 * A_WIDTH]
    gb = z[:, 1024:1536]
    gc = z[:, 1536:2048]
    xin = z[:, 2048:2560]
    zc = z[:, 2560:3072]
    cdf = [_erf_parts(za[:, c0:c0 + LANES]) for c0 in range(0, 2 * A_WIDTH, LANES)]
    act = [za[:, k * LANES:(k + 1) * LANES] * cdf[k] for k in range(len(cdf))]
    u = jnp.concatenate(act[:A_WIDTH // LANES], axis=1)
    v = jnp.concatenate(act[A_WIDTH // LANES:], axis=1)
    mu = jnp.mean(v, axis=-1, keepdims=True)
    vc = v - mu
    rstd = lax.rsqrt(jnp.mean(vc * vc, axis=-1, keepdims=True) + EPS)
    vhat = vc * rstd
    vn = (vhat * ln_g + ln_b).astype(BF16)
    rows = []
    for c in range(tm // CHUNK):
        cols = []
        for g in range(A_GROUPS):
            blk = vn[c * CHUNK:(c + 1) * CHUNK, g * CHUNK:(g + 1) * CHUNK]
            cols.append(jnp.dot(wtril[g], blk, preferred_element_type=F32) + bsb[g])
        rows.append(jnp.concatenate(cols, axis=1))
    mixed = jnp.concatenate(rows, axis=0)
    ya = u * mixed
    y, y1, y2, conv, yb = [], [], [], [], []
    for c0 in range(0, B_WIDTH, LANES):
        sl = slice(c0, c0 + LANES)
        ys = gc[:, sl] * xin[:, sl]
        yext = jnp.concatenate([halo[:, sl] * halo[:, B_WIDTH + c0:B_WIDTH + c0 + LANES], ys], axis=0)
        y1s = _shift_down(yext, 1)[HALO:]
        y2s = _shift_down(yext, 2)[HALO:]
        convs = conv_w[0:1, sl] * y2s + conv_w[1:2, sl] * y1s + conv_w[2:3, sl] * ys
        for lst, val in ((y, ys), (y1, y1s), (y2, y2s), (conv, convs), (yb, gb[:, sl] * convs)):
            lst.append(val)
    yb = jnp.concatenate(yb, axis=1)
    t = row0 + lax.broadcasted_iota(jnp.int32, (tm, 1), 0)
    zext = jnp.concatenate([halo[:, 1024:1536], zc], axis=0)
    pooled, p = [], []
    for gi, win in enumerate(POOL_WINDOWS):
        sw = zext[:, gi * C_GROUP:(gi + 1) * C_GROUP]
        for step in range(gi + 1):
            sw = sw + _shift_down(sw, 2 ** step)
        cnt = jnp.minimum(t + 1, win).astype(F32)
        pg = sw[HALO:] / cnt - zc[:, gi * C_GROUP:(gi + 1) * C_GROUP]
        pooled.append(pg)
        p.append(jnp.dot(pg.astype(BF16), wg[gi], preferred_element_type=F32))
    p = jnp.concatenate(p, axis=1)
    yc = p * scale
    saved = dict(za=za, cdf=cdf, u=u, mixed=mixed, rstd=rstd, vhat=vhat, vn=vn, gb=gb, gc=gc, xin=xin, y=y, y1=y1, y2=y2,
                 conv=conv, pooled=pooled, p=p, t=t)
    return ya, yb, yc, saved


def _mixer_specs(tm, s):
    nb16 = tm // HALO
    main = pl.BlockSpec((tm, MIX_COLS), lambda i: (i, 0))
    prev = pl.BlockSpec((HALO, 1536), lambda i: (jnp.maximum(i * nb16 - 1, 0), 1))
    return main, prev


def _full(shape):
    n = len(shape)
    return pl.BlockSpec(shape, lambda i: (0,) * n)


def _mixer_fwd(proj, prm, name):
    s = proj.shape[0]
    tm = _blk(s, MIXER_ROWS)
    main, prev = _mixer_specs(tm, s)

    def body(z_ref, h_ref, lng, lnb, wt, bsb, cw, wg, sc, o_ref):
        i = pl.program_id(0)
        halo = jnp.where(i > 0, h_ref[...].astype(F32), 0.0)
        prm_v = (lng[...], lnb[...], wt[...], bsb[...], cw[...], wg[...], sc[...])
        ya, yb, yc, _ = _mixer_forward_math(z_ref[...].astype(F32), halo, i * tm, prm_v, tm)
        o_ref[...] = jnp.concatenate([ya, yb, yc], axis=1).astype(BF16)

    return pl.pallas_call(body, out_shape=jax.ShapeDtypeStruct((s, 1536), BF16), grid=(s // tm,),
                          in_specs=[main, prev] + [_full(p.shape) for p in prm],
                          out_specs=pl.BlockSpec((tm, 1536), lambda i: (i, 0)), name=name,
                          compiler_params=_params(("parallel",)))(proj, proj, *prm)


def _mixer_bwd(proj, dypre, dproj_in, prm, wtril_t, name, carry=None):
    s = proj.shape[0]
    tm = _blk(s, MIXER_ROWS)
    nblk = s // tm
    nb16 = tm // HALO
    main, prev = _mixer_specs(tm, s)
    nxt_row = lambda i: jnp.minimum((i + 1) * nb16, s // HALO - 1)
    next_dy = pl.BlockSpec((HALO, 1536), lambda i: (nxt_row(i), 0))
    next_gb = pl.BlockSpec((HALO, 512), lambda i: (nxt_row(i), 2))
    ones8 = jnp.ones((8, CHUNK), F32)

    def body(z_ref, h_ref, dy_ref, ndy_ref, ngb_ref, _alias, lng, lnb, wt, bsb, cw, wg, sc, wtt, ones_ref,
             dz_ref, dlng_ref, dlnb_ref, dws_ref, dbs_ref, dcw_ref, dwg_ref, dsc_ref):
        i = pl.program_id(0)
        first = i == 0
        halo = jnp.where(i > 0, h_ref[...].astype(F32), 0.0)
        prm_v = (lng[...], lnb[...], wt[...], bsb[...], cw[...], wg[...], sc[...])
        _, _, _, sv = _mixer_forward_math(z_ref[...].astype(F32), halo, i * tm, prm_v, tm)
        dy = dy_ref[...].astype(F32)
        dya, dyb, dyc = dy[:, 0:512], dy[:, 512:1024], dy[:, 1024:1536]
        not_last = i < nblk - 1
        ndy = jnp.where(not_last, ndy_ref[...].astype(F32), 0.0)
        ngb = ngb_ref[...].astype(F32)

        def accumulate(ref, val):
            @pl.when(first)
            def _():
                ref[...] = val

            @pl.when(jnp.logical_not(first))
            def _():
                ref[...] += val

        du = dya * sv["mixed"]
        dmixed = dya * sv["u"]
        vn = sv["vn"]
        wttv = wtt[...]
        dvn_rows = []
        dws_parts = [None] * A_GROUPS
        dbs_parts = [None] * A_GROUPS
        for c in range(tm // CHUNK):
            cols = []
            for g in range(A_GROUPS):
                dm = dmixed[c * CHUNK:(c + 1) * CHUNK, g * CHUNK:(g + 1) * CHUNK]
                dmb = dm.astype(BF16)
                vb = vn[c * CHUNK:(c + 1) * CHUNK, g * CHUNK:(g + 1) * CHUNK]
                w_part = lax.dot_general(dmb, vb, (((1,), (1,)), ((), ())), preferred_element_type=F32)
                b_part = lax.dot_general(ones_ref[...], dm, (((1,), (1,)), ((), ())), preferred_element_type=F32,
                                         precision=lax.Precision.HIGHEST)[0:1, :]
                dws_parts[g] = w_part if dws_parts[g] is None else dws_parts[g] + w_part
                dbs_parts[g] = b_part if dbs_parts[g] is None else dbs_parts[g] + b_part
                cols.append(jnp.dot(wttv[g], dmb, preferred_element_type=F32))
            dvn_rows.append(jnp.concatenate(cols, axis=1))
        dvn = jnp.concatenate(dvn_rows, axis=0)
        tri = lax.broadcasted_iota(jnp.int32, (CHUNK, CHUNK), 0) >= lax.broadcasted_iota(jnp.int32, (CHUNK, CHUNK), 1)
        accumulate(dws_ref, jnp.stack([jnp.where(tri, w, 0.0) for w in dws_parts], axis=0))
        accumulate(dbs_ref, jnp.concatenate(dbs_parts, axis=0))
        vhat = sv["vhat"]
        accumulate(dlng_ref, jnp.sum(dvn * vhat, axis=0, keepdims=True))
        accumulate(dlnb_ref, jnp.sum(dvn, axis=0, keepdims=True))
        dvhat = dvn * lng[...]
        dv = sv["rstd"] * (dvhat - jnp.mean(dvhat, axis=-1, keepdims=True)
                           - vhat * jnp.mean(dvhat * vhat, axis=-1, keepdims=True))
        za = sv["za"]
        dza = []
        for k, c0 in enumerate(range(0, 2 * A_WIDTH, LANES)):
            zs = za[:, c0:c0 + LANES]
            dzs = du[:, c0:c0 + LANES] if c0 < A_WIDTH else dv[:, c0 - A_WIDTH:c0 - A_WIDTH + LANES]
            dza.append(dzs * (sv["cdf"][k] + zs * jnp.exp(-0.5 * zs * zs) * (1.0 / math.sqrt(2.0 * math.pi))))
        dza = jnp.concatenate(dza, axis=1)
        cwv = cw[...]
        dgb, dgc, dxin, dcw = [], [], [], []
        for k, c0 in enumerate(range(0, B_WIDTH, LANES)):
            sl = slice(c0, c0 + LANES)
            dconv = dyb[:, sl] * sv["gb"][:, sl]
            dgb.append(dyb[:, sl] * sv["conv"][k])
            dcext = jnp.concatenate([dconv, ndy[:, B_WIDTH + c0:B_WIDTH + c0 + LANES] * ngb[:, sl]], axis=0)
            d1 = _shift_up(dcext, 1)[:tm]
            d2 = _shift_up(dcext, 2)[:tm]
            dyy = cwv[2:3, sl] * dconv + cwv[1:2, sl] * d1 + cwv[0:1, sl] * d2
            dgc.append(dyy * sv["xin"][:, sl])
            dxin.append(dyy * sv["gc"][:, sl])
            dcw.append(jnp.concatenate([jnp.sum(dconv * sv["y2"][k], axis=0, keepdims=True),
                                        jnp.sum(dconv * sv["y1"][k], axis=0, keepdims=True),
                                        jnp.sum(dconv * sv["y"][k], axis=0, keepdims=True)], axis=0))
        dgb, dgc, dxin = (jnp.concatenate(v, axis=1) for v in (dgb, dgc, dxin))
        accumulate(dcw_ref, jnp.concatenate(dcw, axis=1))
        scv = sc[...]
        accumulate(dsc_ref, jnp.sum(dyc * sv["p"], axis=0, keepdims=True))
        dp = dyc * scv
        ndp = ndy[:, 1024:1536] * scv
        wgv = wg[...]
        dzc_cols, dwg_parts = [], []
        for gi, win in enumerate(POOL_WINDOWS):
            sl = slice(gi * C_GROUP, (gi + 1) * C_GROUP)
            dpb = dp[:, sl].astype(BF16)
            dpool = lax.dot_general(dpb, wgv[gi], (((1,), (1,)), ((), ())), preferred_element_type=F32)
            ndpool = lax.dot_general(ndp[:, sl].astype(BF16), wgv[gi], (((1,), (1,)), ((), ())),
                                     preferred_element_type=F32)
            dwg_parts.append(lax.dot_general(sv["pooled"][gi].astype(BF16), dpb, (((0,), (0,)), ((), ())),
                                             preferred_element_type=F32))
            cnt = jnp.minimum(sv["t"] + 1, win).astype(F32)
            fw = jnp.concatenate([dpool / cnt, ndpool * (1.0 / win)], axis=0)
            for step in range(gi + 1):
                fw = fw + _shift_up(fw, 2 ** step)
            dzc_cols.append(fw[:tm] - dpool)
        accumulate(dwg_ref, jnp.stack(dwg_parts, axis=0))
        dzc = jnp.concatenate(dzc_cols, axis=1)
        dz_ref[...] = jnp.concatenate([dza, dgb, dgc, dxin, dzc], axis=1).astype(BF16)

    out_shape = (jax.ShapeDtypeStruct(dproj_in.shape, BF16), jax.ShapeDtypeStruct((1, A_WIDTH), F32),
                 jax.ShapeDtypeStruct((1, A_WIDTH), F32), jax.ShapeDtypeStruct((A_GROUPS, CHUNK, CHUNK), F32),
                 jax.ShapeDtypeStruct((A_GROUPS, CHUNK), F32), jax.ShapeDtypeStruct((3, B_WIDTH), F32),
                 jax.ShapeDtypeStruct((4, C_GROUP, C_GROUP), F32), jax.ShapeDtypeStruct((1, C_WIDTH), F32))
    out_specs = (main,) + tuple(_full(o.shape) for o in out_shape[1:])
    in_specs = [main, prev, pl.BlockSpec((tm, 1536), lambda i: (i, 0)), next_dy, next_gb, ANY] + \
               [_full(p.shape) for p in prm] + [_full(wtril_t.shape), _full(ones8.shape)]
    outs = _call_with_carry(body, out_shape=out_shape, grid=(nblk,), in_specs=in_specs, out_specs=out_specs,
                            scratch_shapes=[], name=name, semantics=("arbitrary",),
                            operands=[proj, proj, dypre, dypre, proj, dproj_in, *prm, wtril_t, ones8], carry=carry,
                            aliases={5: 0})
    return tuple(outs) if carry is None else tuple(outs[:8]) + (outs[8:],)


def _sigmoid(x):
    return 1.0 / (1.0 + jnp.exp(-x))


def _merge_fwd(ypre, proj, wa, wb, wc, bgate, name):
    s = ypre.shape[0]
    tm = _blk(s, 512)

    def body(y_ref, zg_ref, wa_ref, wb_ref, wc_ref, bg_ref, o_ref):
        yv = y_ref[...]
        acc = None
        for i, w_ref in enumerate((wa_ref, wb_ref, wc_ref)):
            br = jnp.dot(yv[:, i * 512:(i + 1) * 512], w_ref[...], preferred_element_type=F32)
            gate = _sigmoid(zg_ref[:, i * D_MODEL:(i + 1) * D_MODEL].astype(F32) + bg_ref[i:i + 1, :])
            acc = gate * br if acc is None else acc + gate * br
        o_ref[...] = acc.astype(BF16)

    wspec = _full(wa.shape)
    return pl.pallas_call(body, out_shape=jax.ShapeDtypeStruct((s, D_MODEL), BF16), grid=(s // tm,),
                          in_specs=[pl.BlockSpec((tm, 1536), lambda i: (i, 0)),
                                    pl.BlockSpec((tm, 3 * D_MODEL), lambda i: (i, 1)), wspec, wspec, wspec,
                                    _full(bgate.shape)],
                          out_specs=pl.BlockSpec((tm, D_MODEL), lambda i: (i, 0)), name=name,
                          compiler_params=_params(("parallel",)))(ypre, proj, wa, wb, wc, bgate)


def _merge_bwd(ypre, proj, dmerged, wa, wb, wc, bgate, name, carry=None):
    s = ypre.shape[0]
    tm = _blk(s, 512)

    def body(y_ref, zg_ref, dm_ref, wa_ref, wb_ref, wc_ref, bg_ref, dyp_ref, dzg_ref, dbg_ref, dwa_ref, dwb_ref,
             dwc_ref):
        first = pl.program_id(0) == 0

        def accumulate(ref, val):
            @pl.when(first)
            def _():
                ref[...] = val

            @pl.when(jnp.logical_not(first))
            def _():
                ref[...] += val

        yv = y_ref[...]
        dm = dm_ref[...].astype(F32)
        dyp, dzg, dbg = [], [], []
        for i, (w_ref, dw_ref) in enumerate(((wa_ref, dwa_ref), (wb_ref, dwb_ref), (wc_ref, dwc_ref))):
            wv = w_ref[...]
            yi = yv[:, i * 512:(i + 1) * 512]
            br = jnp.dot(yi, wv, preferred_element_type=F32)
            gate = _sigmoid(zg_ref[:, i * D_MODEL:(i + 1) * D_MODEL].astype(F32) + bg_ref[i:i + 1, :])
            dbi = (dm * gate).astype(BF16)
            dzi = dm * br * gate * (1.0 - gate)
            dzg.append(dzi.astype(BF16))
            dbg.append(jnp.sum(dzi, axis=0, keepdims=True))
            dyp.append(lax.dot_general(dbi, wv, (((1,), (1,)), ((), ())), preferred_element_type=F32).astype(BF16))
            accumulate(dw_ref, lax.dot_general(yi, dbi, (((0,), (0,)), ((), ())), preferred_element_type=F32))
        dyp_ref[...] = jnp.concatenate(dyp, axis=1)
        dzg_ref[...] = jnp.concatenate(dzg, axis=1)
        accumulate(dbg_ref, jnp.concatenate(dbg, axis=0))

    wspec = _full(wa.shape)
    dw_shape = jax.ShapeDtypeStruct(wa.shape, F32)
    out_shape = (jax.ShapeDtypeStruct((s, 1536), BF16), jax.ShapeDtypeStruct((s, IN_COLS), BF16),
                 jax.ShapeDtypeStruct((3, D_MODEL), F32), dw_shape, dw_shape, dw_shape)
    outs = _call_with_carry(body, out_shape=out_shape, grid=(s // tm,),
                            in_specs=[pl.BlockSpec((tm, 1536), lambda i: (i, 0)),
                                      pl.BlockSpec((tm, 3 * D_MODEL), lambda i: (i, 1)),
                                      pl.BlockSpec((tm, D_MODEL), lambda i: (i, 0)), wspec, wspec, wspec,
                                      _full(bgate.shape)],
                            out_specs=(pl.BlockSpec((tm, 1536), lambda i: (i, 0)),
                                       pl.BlockSpec((tm, 3 * D_MODEL), lambda i: (i, 1)), _full((3, D_MODEL)),
                                       wspec, wspec, wspec),
                            scratch_shapes=[], name=name, semantics=("arbitrary",),
                            operands=[ypre, proj, dmerged, wa, wb, wc, bgate], carry=carry)
    return tuple(outs) if carry is None else tuple(outs[:6]) + (outs[6:],)


def _softmax_rows(q, k):
    sc = lax.dot_general(q, k, (((1,), (1,)), ((), ())), preferred_element_type=F32) * (HEAD_DIM ** -0.5)
    e = jnp.exp(sc - jnp.max(sc, axis=-1, keepdims=True))
    return e / jnp.sum(e, axis=-1, keepdims=True)


def _attn_fwd(q, kv, name):
    s = q.shape[0]
    tm = _blk(s, 512)

    def body(q_ref, kv_ref, o_ref):
        outs = []
        for h in range(XATTN_HEADS):
            sl = slice(h * HEAD_DIM, (h + 1) * HEAD_DIM)
            p = _softmax_rows(q_ref[:, sl], kv_ref[:, sl])
            outs.append(jnp.dot(p.astype(BF16), kv_ref[:, D_MODEL + h * HEAD_DIM:D_MODEL + (h + 1) * HEAD_DIM],
                                preferred_element_type=F32))
        o_ref[...] = jnp.concatenate(outs, axis=1).astype(BF16)

    return pl.pallas_call(body, out_shape=jax.ShapeDtypeStruct((s, D_MODEL), BF16), grid=(s // tm,),
                          in_specs=[pl.BlockSpec((tm, D_MODEL), lambda i: (i, 0)), _full(kv.shape)],
                          out_specs=pl.BlockSpec((tm, D_MODEL), lambda i: (i, 0)), name=name,
                          compiler_params=_params(("parallel",)))(q, kv)


def _attn_bwd(q, kv, do, name):
    s = q.shape[0]
    tm = _blk(s, 512)

    def body(q_ref, kv_ref, do_ref, dq_ref, dkv_ref):
        dqs, dks, dvs = [], [], []
        for h in range(XATTN_HEADS):
            sl = slice(h * HEAD_DIM, (h + 1) * HEAD_DIM)
            vsl = slice(D_MODEL + h * HEAD_DIM, D_MODEL + (h + 1) * HEAD_DIM)
            qh, kh, vh, doh = q_ref[:, sl], kv_ref[:, sl], kv_ref[:, vsl], do_ref[:, sl]
            p = _softmax_rows(qh, kh)
            pb = p.astype(BF16)
            dvs.append(lax.dot_general(pb, doh, (((0,), (0,)), ((), ())), preferred_element_type=F32))
            dp = lax.dot_general(doh, vh, (((1,), (1,)), ((), ())), preferred_element_type=F32)
            ds = p * (dp - jnp.sum(dp * p, axis=-1, keepdims=True)) * (HEAD_DIM ** -0.5)
            dsb = ds.astype(BF16)
            dqs.append(jnp.dot(dsb, kh, preferred_element_type=F32))
            dks.append(lax.dot_general(dsb, qh, (((0,), (0,)), ((), ())), preferred_element_type=F32))
        dq_ref[...] = jnp.concatenate(dqs, axis=1).astype(BF16)
        part = jnp.concatenate(dks + dvs, axis=1)

        @pl.when(pl.program_id(0) == 0)
        def _():
            dkv_ref[...] = part

        @pl.when(pl.program_id(0) > 0)
        def _():
            dkv_ref[...] += part

    row = pl.BlockSpec((tm, D_MODEL), lambda i: (i, 0))
    return pl.pallas_call(body, out_shape=(jax.ShapeDtypeStruct((s, D_MODEL), BF16),
                                           jax.ShapeDtypeStruct(kv.shape, F32)),
                          grid=(s // tm,), in_specs=[row, _full(kv.shape), row], out_specs=(row, _full(kv.shape)),
                          name=name, compiler_params=_params(("arbitrary",)))(q, kv, do)


def _rows_block(rows, cols):
    target = (512 * 1024) // cols
    fits = [cand for cand in range(8, rows + 1, 8) if rows % cand == 0 and cand <= target]
    return fits[-1] if fits else rows


def _elementwise(fn, ins, out_dtypes, name):
    lead, rows, cols = ins[0].shape
    tr = _rows_block(rows, cols)
    spec = pl.BlockSpec((None, tr, cols), lambda l, i: (l, i, 0))
    n_in = len(ins)

    def body(*refs):
        for o_ref, o in zip(refs[n_in:], fn(*[r[...] for r in refs[:n_in]])):
            o_ref[...] = o.astype(o_ref.dtype)

    out_shape = tuple(jax.ShapeDtypeStruct((lead, rows, cols), dt) for dt in out_dtypes)
    return pl.pallas_call(body, out_shape=out_shape, grid=(lead, rows // tr), in_specs=[spec] * n_in,
                          out_specs=tuple([spec] * len(out_dtypes)), name=name,
                          compiler_params=_params(("parallel", "parallel")))(*ins)


def _cast_to_slots(w, chip, name):
    _, rows, cols = w.shape
    tr = _rows_block(rows, cols)

    def body(chip_ref, w_ref, o0_ref, o1_ref):
        o0_ref[...] = w_ref[0].astype(BF16)
        o1_ref[...] = w_ref[1].astype(BF16)

    slot = pl.BlockSpec((None, tr, cols), lambda i, chip_ref: (chip_ref[0], i, 0))
    grid_spec = pltpu.PrefetchScalarGridSpec(
        num_scalar_prefetch=1, grid=(rows // tr,),
        in_specs=[pl.BlockSpec((DEPTH, tr, cols), lambda i, chip_ref: (0, i, 0))], out_specs=(slot, slot))
    shape = jax.ShapeDtypeStruct((N_CHIPS, rows, cols), BF16)
    return pl.pallas_call(body, out_shape=(shape, shape), grid_spec=grid_spec, name=name,
                          compiler_params=_params(("parallel",)))(chip, w)


def _chip_sum(pair0, rx0, pair1, rx1, chip, name):
    _, rows, cols = pair0.shape
    tr = _rows_block(rows, cols)

    def body(chip_ref, p0_ref, rx0_ref, p1_ref, rx1_ref, o_ref):
        for layer, (p_ref, rx_ref) in enumerate(((p0_ref, rx0_ref), (p1_ref, rx1_ref))):
            acc = p_ref[...].astype(F32)
            for j in range(3):
                acc = acc + rx_ref[j].astype(F32)
            o_ref[layer] = acc

    mine = pl.BlockSpec((None, tr, cols), lambda i, chip_ref: (chip_ref[0], i, 0))
    theirs = pl.BlockSpec((3, tr, cols), lambda i, chip_ref: (0, i, 0))
    grid_spec = pltpu.PrefetchScalarGridSpec(
        num_scalar_prefetch=1, grid=(rows // tr,), in_specs=[mine, theirs, mine, theirs],
        out_specs=pl.BlockSpec((DEPTH, tr, cols), lambda i, chip_ref: (0, i, 0)))
    return pl.pallas_call(body, out_shape=jax.ShapeDtypeStruct((DEPTH, rows, cols), F32), grid_spec=grid_spec,
                          name=name, compiler_params=_params(("parallel",)))(chip, pair0, rx0, pair1, rx1)


def _chip_sum_ordered(own, rx, chip, name):
    rows, cols = own.shape

    def body(chip_ref, own_ref, rx_ref, o_ref):
        me = chip_ref[0]
        acc = None
        for k in range(N_CHIPS):
            rel = jnp.bitwise_xor(me, k)
            term = jnp.where(rel == 0, own_ref[...],
                             jnp.where(rel == 2, rx_ref[0], jnp.where(rel == 1, rx_ref[1], rx_ref[2])))
            acc = term if acc is None else acc + term
        o_ref[...] = acc

    grid_spec = pltpu.PrefetchScalarGridSpec(
        num_scalar_prefetch=1, grid=(1,),
        in_specs=[pl.BlockSpec((rows, cols), lambda i, chip_ref: (0, 0)),
                  pl.BlockSpec((3, rows, cols), lambda i, chip_ref: (0, 0, 0))],
        out_specs=pl.BlockSpec((rows, cols), lambda i, chip_ref: (0, 0)))
    return pl.pallas_call(body, out_shape=jax.ShapeDtypeStruct((rows, cols), F32), grid_spec=grid_spec, name=name,
                          compiler_params=_params(("arbitrary",)))(chip, own, rx)


def _adamw_math(w, g, m, v):
    m = ADAM_B1 * m + (1.0 - ADAM_B1) * g
    v = ADAM_B2 * v + (1.0 - ADAM_B2) * (g * g)
    m_hat = m / (1.0 - ADAM_B1 ** ADAM_STEP)
    v_hat = v / (1.0 - ADAM_B2 ** ADAM_STEP)
    delta = -ADAM_LR * (m_hat / (jnp.sqrt(v_hat) + ADAM_EPS) + ADAM_WD * w)
    return delta, m, v


def _adamw_small(ws, gs, ms, vs, name):
    n = len(ws)

    def body(*refs):
        for i in range(n):
            w_ref, g_ref, m_ref, v_ref = (refs[k * n + i] for k in range(4))
            d_out, m_out, v_out = (refs[(4 + k) * n + i] for k in range(3))
            d_out[...], m_out[...], v_out[...] = _adamw_math(w_ref[...], g_ref[...], m_ref[...], v_ref[...])

    vmem = pl.BlockSpec(memory_space=pltpu.VMEM)
    shapes = tuple(jax.ShapeDtypeStruct(w.shape, F32) for w in ws)
    outs = pl.pallas_call(body, out_shape=shapes * 3, in_specs=[vmem] * (4 * n), out_specs=tuple([vmem] * (3 * n)),
                          name=name, compiler_params=_params())(*ws, *gs, *ms, *vs)
    return outs[:n], outs[n:2 * n], outs[2 * n:]


def _adamw_shard(w, mine, other, m, v, core, name):
    lead, rows, cols = w.shape
    half = rows // 2
    tr = _rows_block(half, cols)
    nhalf = half // tr

    def body(core_ref, w_ref, mine_ref, other_ref, m_ref, v_ref, g_out, d_out, m_out, v_out):
        g = jnp.where(pl.program_id(1) // nhalf == core_ref[0], mine_ref[...], other_ref[...])
        d_new, m_new, v_new = _adamw_math(w_ref[...], g, m_ref[...], v_ref[...])
        g_out[...] = g
        d_out[...] = d_new
        m_out[...] = m_new
        v_out[...] = v_new

    whole = pl.BlockSpec((None, tr, cols), lambda l, i, core_ref: (l, i, 0))
    own = pl.BlockSpec((None, tr, cols),
                       lambda l, i, core_ref: (l, jnp.where(i // nhalf == core_ref[0], i % nhalf, 0), 0))
    far = pl.BlockSpec((None, tr, cols),
                       lambda l, i, core_ref: (l, jnp.where(i // nhalf == core_ref[0], 0, i % nhalf), 0))
    shape = jax.ShapeDtypeStruct(w.shape, F32)
    grid_spec = pltpu.PrefetchScalarGridSpec(num_scalar_prefetch=1, grid=(lead, rows // tr),
                                             in_specs=[whole, own, far, whole, whole], out_specs=(whole,) * 4)
    return pl.pallas_call(body, out_shape=(shape,) * 4, grid_spec=grid_spec, name=name,
                          compiler_params=_params(("parallel", "parallel")))(core, w, mine, other, m, v)


def _place():
    x, y, c = lax.axis_index("x"), lax.axis_index("y"), lax.axis_index("c")
    chips = [(1 - x, y), (x, 1 - y), (1 - x, 1 - y)]
    return x, y, c, 2 * x + y, chips


def _run_remote(copies):
    for send, _ in copies:
        send.start()
    for send, recv in copies:
        send.wait_send()
        recv.wait_recv()


def _half_rows(ref, c):
    half = ref.shape[1] // 2
    return pl.ds(pl.multiple_of(c * half, 8), half)


def _gather_issue(whole):
    def issue(_, refs, ssem, rsem):
        x, y, c, me, chips = _place()
        copies = []
        for w, ref in enumerate(refs):
            part = (me,) if w >= len(refs) - whole else (me, _half_rows(ref, c))
            mine = ref.at[part]
            for j, (px, py) in enumerate(chips):
                sems = (ssem.at[3 * w + j], rsem.at[3 * w + j])
                peer = dict(device_id=(px, py, c), device_id_type=MESH)
                send = pltpu.make_async_remote_copy(mine, mine, *sems, **peer)
                recv = pltpu.make_async_remote_copy(mine, ref.at[(2 * px + py,) + part[1:]], *sems, **peer)
                copies.append((send, recv))
        return copies
    return issue


def _gather_carry(fulls, whole=0):
    return _Carry(arrays=tuple(fulls), outs=tuple(jax.ShapeDtypeStruct(f.shape, f.dtype) for f in fulls),
                  in_place=True, copies=3 * len(fulls), issue=_gather_issue(whole))


def _scatter_issue(ins, outs, ssem, rsem):
    x, y, c, me, chips = _place()
    copies = []
    for w, (src, dst) in enumerate(zip(ins, outs)):
        for j, (px, py) in enumerate(chips):
            cp = pltpu.make_async_remote_copy(src.at[2 * px + py], dst.at[j], ssem.at[3 * w + j], rsem.at[3 * w + j],
                                              device_id=(px, py, c), device_id_type=MESH)
            copies.append((cp, cp))
    return copies


def _scatter_carry(pairs):
    return _Carry(arrays=tuple(pairs), outs=tuple(jax.ShapeDtypeStruct((3,) + p.shape[1:], p.dtype) for p in pairs),
                  in_place=False, copies=3 * len(pairs), issue=_scatter_issue)


def _chip_gather(fulls, whole, name):
    n = len(fulls)
    issue = _gather_issue(whole)

    def body(*refs):
        _run_remote(issue(None, refs[n:2 * n], refs[2 * n], refs[2 * n + 1]))

    out_shape = tuple(jax.ShapeDtypeStruct(f.shape, f.dtype) for f in fulls)
    return pl.pallas_call(body, out_shape=out_shape, in_specs=[ANY] * n, out_specs=tuple([ANY] * n),
                          input_output_aliases={w: w for w in range(n)},
                          scratch_shapes=[pltpu.SemaphoreType.DMA((3 * n,)), pltpu.SemaphoreType.DMA((3 * n,))],
                          name=name)(*fulls)


class _Blocks(NamedTuple):
    tile: tuple
    dtype: object
    moves: list
    landed: object


STAGE_BYTES = 4 * 1024 * 1024


def _rows(start, size):
    return pl.ds(start if isinstance(start, int) else pl.multiple_of(start, 8), size)


def _stage_tile(lead, rows, cols, dtype):
    per_row = cols * jnp.dtype(dtype).itemsize
    if lead * rows * per_row <= STAGE_BYTES:
        return lead, rows
    fits = [cand for cand in range(8, rows + 1, 8) if rows % cand == 0 and cand * per_row <= STAGE_BYTES]
    return 1, (fits[-1] if fits else rows)


def _plan_gathered(_, refs, c, me):
    work = []
    for ref in refs:
        lead, rows, cols = ref.shape
        half = rows // 2
        nk, tr = _stage_tile(lead, half, cols, ref.dtype)
        if nk == lead:
            part = ref.at[pl.ds(0, lead), _rows(c * half, half)]
            moves, landed = [(part, part)], ref.at[pl.ds(0, lead), pl.ds(0, half)]
        else:
            moves = []
            for j in range(N_CHIPS - 1):
                k = jnp.bitwise_xor(me, j + 1)
                for b in range(half // tr):
                    part = ref.at[pl.ds(k, 1), _rows(c * half + b * tr, tr)]
                    moves.append((part, part))
            landed = ref.at[pl.ds(0, N_CHIPS - 1), pl.ds(0, half)]
        work.append(_Blocks((nk, tr, cols), ref.dtype, moves, landed))
    return work


def _plan_grad_halves(grads, rxs, c, me):
    work = []
    for g, rx in zip(grads, rxs):
        lead, half, cols = rx.shape
        nk, tr = _stage_tile(lead, half, cols, g.dtype)
        moves = [(g.at[pl.ds(k, nk), _rows((1 - c) * half + b * tr, tr)], rx.at[pl.ds(k, nk), _rows(b * tr, tr)])
                 for k in range(0, lead, nk) for b in range(half // tr)]
        work.append(_Blocks((nk, tr, cols), g.dtype, moves, rx))
    return work


def _plan_whole(srcs, dsts, c, me):
    work = []
    for src, dst in zip(srcs, dsts):
        lead, rows, cols = src.shape
        nk, tr = _stage_tile(lead, rows, cols, src.dtype)
        moves = [(src.at[pl.ds(k, nk), _rows(b * tr, tr)], dst.at[pl.ds(k, nk), _rows(b * tr, tr)])
                 for k in range(0, lead, nk) for b in range(rows // tr)]
        work.append(_Blocks((nk, tr, cols), src.dtype, moves, dst))
    return work


def _push_staged(arrays, outs, in_place, plan, name):
    n, no = len(arrays), len(outs)

    def body(*refs):
        ins, out_refs = refs[:n], refs[n:n + no]
        rsem, idle = refs[n + no], refs[n + no + 1]
        x, y, c, me, _ = _place()
        peer = dict(device_id=(x, y, 1 - c), device_id_type=MESH)
        work = plan(ins, out_refs, c, me)
        for w, blocks in enumerate(work):
            def staged(buf, lsem, ssem, w=w, blocks=blocks):
                sends = []
                for i, (src, dst) in enumerate(blocks.moves):
                    slot = i % 2
                    if i >= 2:
                        sends[i - 2].wait_send()
                    load = pltpu.make_async_copy(src, buf.at[slot], lsem.at[slot])
                    load.start()
                    load.wait()
                    send = pltpu.make_async_remote_copy(buf.at[slot], dst, ssem.at[slot], rsem.at[w], **peer)
                    send.start()
                    sends.append(send)
                for send in sends[-2:]:
                    send.wait_send()

            pl.run_scoped(staged, pltpu.VMEM((2,) + blocks.tile, blocks.dtype), pltpu.SemaphoreType.DMA((2,)),
                          pltpu.SemaphoreType.DMA((2,)))
        for w, blocks in enumerate(work):
            pltpu.make_async_remote_copy(blocks.landed, blocks.landed, idle, rsem.at[w], **peer).wait_recv()

    return pl.pallas_call(body, out_shape=tuple(outs), in_specs=[ANY] * n, out_specs=tuple([ANY] * no),
                          input_output_aliases={w: w for w in range(n)} if in_place else {},
                          scratch_shapes=[pltpu.SemaphoreType.DMA((n,)), pltpu.SemaphoreType.DMA],
                          name=name, compiler_params=_params())(*arrays)


def _pair_sum(g, rx, core, name):
    lead, half, cols = rx.shape
    tr = _rows_block(half, cols)
    nrow = half // tr

    def body(core_ref, g_ref, rx_ref, o_ref):
        o_ref[...] = (g_ref[...].astype(F32) + rx_ref[...].astype(F32)).astype(o_ref.dtype)

    spec = pl.BlockSpec((None, tr, cols), lambda k, i, core_ref: (k, i, 0))
    grid_spec = pltpu.PrefetchScalarGridSpec(
        num_scalar_prefetch=1, grid=(lead, nrow),
        in_specs=[pl.BlockSpec((None, tr, cols), lambda k, i, core_ref: (k, core_ref[0] * nrow + i, 0)), spec],
        out_specs=spec)
    return pl.pallas_call(body, out_shape=jax.ShapeDtypeStruct(rx.shape, rx.dtype), grid_spec=grid_spec, name=name,
                          compiler_params=_params(("parallel", "parallel")))(core, g, rx)


def _push_sibling(src, name):
    lead, rows, cols = src.shape
    tr = _rows_block(rows, cols)
    nrow = rows // tr

    def body(blk_ref, rx_ref, ssem, rsem):
        x, y, c = lax.axis_index("x"), lax.axis_index("y"), lax.axis_index("c")
        k, i = pl.program_id(0), pl.program_id(1)
        peer = dict(device_id=(x, y, 1 - c), device_id_type=MESH)
        dst = rx_ref.at[pl.ds(k, 1), pl.ds(pl.multiple_of(i * tr, 8), tr)]
        cp = pltpu.make_async_remote_copy(blk_ref, dst, ssem, rsem, **peer)
        cp.start()
        cp.wait_send()

        @pl.when(jnp.logical_and(k == lead - 1, i == nrow - 1))
        def _():
            pltpu.make_async_remote_copy(rx_ref, rx_ref, ssem, rsem, **peer).wait_recv()

    return pl.pallas_call(body, out_shape=jax.ShapeDtypeStruct(src.shape, src.dtype), grid=(lead, nrow),
                          in_specs=[pl.BlockSpec((1, tr, cols), lambda k, i: (k, i, 0))], out_specs=ANY,
                          scratch_shapes=[pltpu.SemaphoreType.DMA, pltpu.SemaphoreType.DMA], name=name,
                          compiler_params=_params(("arbitrary", "arbitrary")))(src)


def _chip_scatter(pairs, small, name):
    n = len(pairs)

    def body(*refs):
        ins, small_ref = refs[:n], refs[n]
        outs, small_out = refs[n + 1:2 * n + 1], refs[2 * n + 1]
        ssem, rsem = refs[2 * n + 2:]
        x, y, c, me, chips = _place()
        copies = []
        for w in range(n + 1):
            for j, (px, py) in enumerate(chips):
                src = ins[w].at[2 * px + py] if w < n else small_ref
                dst = (outs[w] if w < n else small_out).at[j]
                cp = pltpu.make_async_remote_copy(src, dst, ssem.at[3 * w + j], rsem.at[3 * w + j],
                                                  device_id=(px, py, c), device_id_type=MESH)
                copies.append((cp, cp))
        _run_remote(copies)

    out_shape = tuple(jax.ShapeDtypeStruct((3,) + it.shape[1:], it.dtype) for it in pairs) + \
                (jax.ShapeDtypeStruct((3,) + small.shape, small.dtype),)
    return pl.pallas_call(body, out_shape=out_shape, in_specs=[ANY] * (n + 1), out_specs=tuple([ANY] * (n + 1)),
                          scratch_shapes=[pltpu.SemaphoreType.DMA((3 * n + 3,)),
                                          pltpu.SemaphoreType.DMA((3 * n + 3,))],
                          name=name)(*pairs, small)


def _pack(arrays):
    flat = jnp.concatenate([a.reshape(-1).astype(F32) for a in arrays])
    pad = (-flat.shape[0]) % (8 * LANES)
    return jnp.pad(flat, (0, pad)).reshape(-1, LANES)


def _unpack(packed, shapes):
    flat = packed.reshape(-1)
    out, off = [], 0
    for shp in shapes:
        size = math.prod(shp)
        out.append(flat[off:off + size].reshape(shp))
        off += size
    return out


def kernel(x, mem, g_mix, w_in, b_gate, a_ln_g, a_ln_b, a_ws, a_bs, b_conv, c_wg, c_scale, w_branch_a, w_branch_b, w_branch_c, w_o, g_xattn, g_mem, w_xq, w_xkv, w_xo, g_ffn, w_ffn_in, w_ffn_out, g_final, loss_target, m_g_mix, m_w_in, m_b_gate, m_a_ln_g, m_a_ln_b, m_a_ws, m_a_bs, m_b_conv, m_c_wg, m_c_scale, m_w_branch_a, m_w_branch_b, m_w_branch_c, m_w_o, m_g_xattn, m_g_mem, m_w_xq, m_w_xkv, m_w_xo, m_g_ffn, m_w_ffn_in, m_w_ffn_out, m_g_final, v_g_mix, v_w_in, v_b_gate, v_a_ln_g, v_a_ln_b, v_a_ws, v_a_bs, v_b_conv, v_c_wg, v_c_scale, v_w_branch_a, v_w_branch_b, v_w_branch_c, v_w_o, v_g_xattn, v_g_mem, v_w_xq, v_w_xkv, v_w_xo, v_g_ffn, v_w_ffn_in, v_w_ffn_out, v_g_final):
    args = locals()
    wts = {n: args[n] for n in WEIGHTS}
    mom = {n: args["m_" + n] for n in WEIGHTS}
    var = {n: args["v_" + n] for n in WEIGHTS}
    xs = x[0]
    mems = mem[0]
    tgt = loss_target[0]
    chip = 2 * lax.axis_index("x") + lax.axis_index("y")

    bias_pack = jnp.zeros((DEPTH, 8, 384), F32)
    bias_pack = bias_pack.at[:, 0:3, 0:256].set(b_gate).at[:, 0:3, 256:384].set(b_conv)
    bias_slots = lax.dynamic_update_slice(jnp.zeros((N_CHIPS, DEPTH, 8, 384), F32), bias_pack[None], (chip, 0, 0, 0))
    core = lax.axis_index("c")
    chip_arr = jnp.reshape(chip, (1,)).astype(jnp.int32)
    core_arr = jnp.reshape(core, (1,)).astype(jnp.int32)
    slots = {n: _cast_to_slots(wts[n], chip_arr, name=f"cast_{n}") for n in BIG}
    layer_weights = [{}, {}]

    def land(l, names, gathered, tag):
        shapes = [jax.ShapeDtypeStruct(g.shape, g.dtype) for g in gathered]
        for n, g in zip(names, _push_staged(list(gathered), shapes, True, _plan_gathered,
                                            name=f"l{l}_gather_cores_{tag}")):
            if n in READ_AS_SLABS:
                layer_weights[l][n] = g
            elif n in COL_SHARDED:
                layer_weights[l][n] = jnp.transpose(g, (1, 0, 2)).reshape(1, g.shape[1], N_CHIPS * g.shape[2])
            else:
                layer_weights[l][n] = g.reshape(1, N_CHIPS * g.shape[1], g.shape[2])

    def gather_ride(l, names):
        return _gather_carry([slots[n][l] for n in names])

    rest_of_first = tuple(n for n in BIG if n not in NEED_FIRST)
    most_of_second = tuple(n for n in BIG if n not in NEED_LAST)
    first = _chip_gather([slots[n][0] for n in NEED_FIRST] + [bias_slots], 1, name="l0_gather_chips")
    land(0, NEED_FIRST, first[:-1], "first")
    biases = first[-1]
    b_gate_full = jnp.transpose(biases[:, :, 0:3, 0:256], (1, 2, 0, 3)).reshape(DEPTH, 3, D_MODEL)
    b_conv_full = jnp.transpose(biases[:, :, 0:3, 256:384], (1, 2, 0, 3)).reshape(DEPTH, 3, B_WIDTH)

    tril = jnp.tril(jnp.ones((CHUNK, CHUNK), bool))

    def mixer_params(l):
        wtril = jnp.where(tril[None], a_ws[l], 0.0)
        prm = (a_ln_g[l][None], a_ln_b[l][None], wtril.astype(BF16),
               jnp.broadcast_to(a_bs[l][:, :, None], (A_GROUPS, CHUNK, CHUNK)), b_conv_full[l],
               c_wg[l].astype(BF16), c_scale[l][None])
        return prm, jnp.swapaxes(wtril, 1, 2).astype(BF16)

    saved = []
    xc = xs
    for l in range(DEPTH):
        sv = {"x0": xc}
        prm, _ = mixer_params(l)
        fw = layer_weights[l]
        riding = rest_of_first if l == 0 else NEED_LAST
        out = _norm_mm(xc, g_mix[l][None], fw["w_in"], None, tn=1536, name=f"l{l}_proj", carry=gather_ride(l, riding))
        sv["h"], sv["proj"] = out[0], out[1]
        land(l, riding, out[2], "rest")
        sv["ypre"] = _mixer_fwd(sv["proj"], prm, name=f"l{l}_mixers")
        sv["merged"] = _merge_fwd(sv["ypre"], sv["proj"], fw["w_branch_a"][0], fw["w_branch_b"][0],
                                  fw["w_branch_c"][0], b_gate_full[l], name=f"l{l}_merge")
        xc = _mm(sv["merged"], fw["w_o"], b_lead=0, mode="nn", out_dtype=F32, name=f"l{l}_mix_out", tm=1024, tn=1024,
                 tk=1024, res=xc)
        sv["x1"] = xc
        sv["hq"], sv["q"] = _norm_mm(xc, g_xattn[l][None], fw["w_xq"], 0, tn=1024, name=f"l{l}_q")
        sv["m"] = _rms_fwd(mems, g_mem[l][None], name=f"l{l}_mem_norm")
        sv["kv"] = _mm(sv["m"], fw["w_xkv"], b_lead=0, mode="nn", out_dtype=BF16, name=f"l{l}_kv", tm=256, tn=512,
                       tk=1024)
        sv["o"] = _attn_fwd(sv["q"], sv["kv"], name=f"l{l}_attn")
        xc = _mm(sv["o"], fw["w_xo"], b_lead=0, mode="nn", out_dtype=F32, name=f"l{l}_xattn_out", tm=1024, tn=1024,
                 tk=1024, res=xc)
        sv["x2"] = xc
        out2 = _norm_ffn_in(xc, g_ffn[l][None], fw["w_ffn_in"], None, name=f"l{l}_ffn_in",
                            carry=gather_ride(1, most_of_second) if l == 0 else None)
        sv["h2"], sv["gu"], sv["act"] = out2[0], out2[1], out2[2]
        if l == 0:
            land(1, most_of_second, out2[3], "most")
        xc = _mm(sv["act"], fw["w_ffn_out"], b_lead=0, mode="nn", out_dtype=F32, name=f"l{l}_ffn_out", tm=1024,
                 tn=512, tk=FFN_HIDDEN, res=xc)
        saved.append(sv)

    loss_part, dx, dg_final = _final_loss(xc, g_final[None], tgt, name="final_loss")
    loss = lax.psum(loss_part[0, 0], ("x", "y", "c"))

    pairs = [{}, {}]
    received = [{}, {}]
    small_grads = [None] * DEPTH

    def core_pairs(l, names, grads, tag):
        halves = [jax.ShapeDtypeStruct((g.shape[0], g.shape[1] // 2, g.shape[2]), g.dtype) for g in grads]
        came = _push_staged(grads, halves, False, _plan_grad_halves, name=f"l{l}_reduce_cores_{tag}")
        for n, g, rx in zip(names, grads, came):
            pairs[l][n] = _pair_sum(g, rx, core_arr, name=f"l{l}_pair_sum_{n}")

    def scatter_ride(l, names, when):
        return _scatter_carry([pairs[l][n] for n in names]) if when else None

    ride_second = tuple(n for n in BIG if n not in RIDE_FIRST)
    early, early_big, early_small = NEED_LAST + ("w_xo", "w_xq", "w_xkv"), NEED_LAST + ("w_xkv",), ("w_xo", "w_xq")
    middle = ("w_o", "w_branch_a", "w_branch_b", "w_branch_c")
    for l in reversed(range(DEPTH)):
        sv = saved[l]
        prm, wtril_t = mixer_params(l)
        gb = {}
        sg = {}
        fw = layer_weights[l]
        dgu = _d_act_swiglu(dx, fw["w_ffn_out"], 0, sv["gu"], name=f"l{l}_d_act")
        gb["w_ffn_out"] = _mm(sv["act"], dx, mode="tn", out_dtype=BF16, name=f"l{l}_dw_ffn_out", tm=1408, tn=1024,
                              tk=TOKEN_STEP).reshape(N_CHIPS, FFN_HIDDEN // N_CHIPS, D_MODEL)
        out = _mm(sv["h2"], dgu, mode="tn", out_dtype=BF16, name=f"l{l}_dw_ffn_in", tm=1024, tn=FFN_TILE,
                  tk=TOKEN_STEP, b_halves=True, out_shards=N_CHIPS, carry=scatter_ride(1, RIDE_FIRST, l == 0))
        gb["w_ffn_in"] = out[0] if l == 0 else out
        out2 = _mm_nt_norm_bwd(dgu, fw["w_ffn_in"], None, sv["x2"], g_ffn[l][None], dx, tm=512, a_halves=True,
                               name=f"l{l}_d_h2", carry=scatter_ride(1, ride_second, l == 0))
        dx, sg["g_ffn"] = out2[0], out2[1]
        if l == 0:
            received[1].update(zip(RIDE_FIRST, out[1]))
            received[1].update(zip(ride_second, out2[2]))
        do = _mm(dx, fw["w_xo"], b_lead=0, mode="nt", out_dtype=BF16, name=f"l{l}_d_o", tm=1024, tn=1024, tk=1024)
        gb["w_xo"] = _mm(sv["o"], dx, mode="tn", out_dtype=BF16, name=f"l{l}_dw_xo", tm=1024, tn=1024,
                         tk=TOKEN_STEP).reshape(N_CHIPS, D_MODEL // N_CHIPS, D_MODEL)
        dq, dkv = _attn_bwd(sv["q"], sv["kv"], do, name=f"l{l}_d_attn")
        gb["w_xq"] = _mm(sv["hq"], dq, mode="tn", out_dtype=BF16, name=f"l{l}_dw_xq", tm=1024, tn=1024,
                         tk=TOKEN_STEP).reshape(N_CHIPS, D_MODEL // N_CHIPS, D_MODEL)
        dm = _mm(dkv, fw["w_xkv"], b_lead=0, mode="nt", out_dtype=BF16, name=f"l{l}_d_m", tm=256, tn=1024, tk=1024)
        gb["w_xkv"] = _mm(sv["m"], dkv, mode="tn", out_dtype=BF16, name=f"l{l}_dw_xkv", tm=1024, tn=512, tk=256,
                          out_shards=N_CHIPS)
        _, sg["g_mem"] = _rms_bwd(mems, g_mem[l][None], dm, None, name=f"l{l}_d_mem_norm")
        if l == 0:
            core_pairs(0, early, [gb[n] for n in early], "early")
        dx, sg["g_xattn"] = _mm_nt_norm_bwd(dq, fw["w_xq"], 0, sv["x1"], g_xattn[l][None], dx, tm=1024,
                                            name=f"l{l}_d_hq")
        dmerged = _mm(dx, fw["w_o"], b_lead=0, mode="nt", out_dtype=BF16, name=f"l{l}_d_merged", tm=1024, tn=1024,
                      tk=1024)
        gb["w_o"] = _mm(sv["merged"], dx, mode="tn", out_dtype=BF16, name=f"l{l}_dw_o", tm=1024, tn=1024,
                        tk=TOKEN_STEP).reshape(N_CHIPS, D_MODEL // N_CHIPS, D_MODEL)
        out3 = _merge_bwd(sv["ypre"], sv["proj"], dmerged, fw["w_branch_a"][0], fw["w_branch_b"][0],
                          fw["w_branch_c"][0], b_gate_full[l], name=f"l{l}_d_merge",
                          carry=scatter_ride(0, early_big, l == 0))
        dypre, dproj, sg["b_gate"] = out3[0], out3[1], out3[2]
        for n, dw in zip(("w_branch_a", "w_branch_b", "w_branch_c"), out3[3:6]):
            gb[n] = jnp.transpose(dw.reshape(A_WIDTH, N_CHIPS, D_MODEL // N_CHIPS), (1, 0, 2)).astype(BF16)
        if l == 0:
            core_pairs(0, middle, [gb[n] for n in middle], "middle")
        out5 = _mixer_bwd(sv["proj"], dypre, dproj, prm, wtril_t, name=f"l{l}_d_mixers",
                          carry=scatter_ride(0, middle, l == 0))
        (dproj, sg["a_ln_g"], sg["a_ln_b"], sg["a_ws"], sg["a_bs"], sg["b_conv"], sg["c_wg"], sg["c_scale"]) = out5[:8]
        out4 = _mm(sv["h"], dproj, mode="tn", out_dtype=BF16, name=f"l{l}_dw_in", tm=1024, tn=1536, tk=TOKEN_STEP,
                   out_shards=N_CHIPS, carry=scatter_ride(0, early_small, l == 0))
        gb["w_in"] = out4[0] if l == 0 else out4
        if l == 0:
            core_pairs(0, ("w_in",), [gb["w_in"]], "last")
        out6 = _mm_nt_norm_bwd(dproj, fw["w_in"], None, sv["x0"], g_mix[l][None], dx, tm=512, name=f"l{l}_d_h",
                               carry=scatter_ride(0, ("w_in",), l == 0))
        dx, sg["g_mix"] = out6[0], out6[1]
        if l == 0:
            received[0].update(zip(early_big, out3[6]))
            received[0].update(zip(middle, out5[8]))
            received[0].update(zip(early_small, out4[1]))
            received[0].update(zip(("w_in",), out6[2]))
        else:
            core_pairs(1, BIG, [gb[n] for n in BIG], "all")
        small_grads[l] = sg
    grad_x = dx[None]

    small_names = [n for n in WEIGHTS if n not in BIG]
    small_full_shapes = {n: ((DEPTH, 3, D_MODEL) if n == "b_gate" else (DEPTH, 3, B_WIDTH) if n == "b_conv"
                             else wts[n].shape) for n in small_names}
    small_local = []
    for n in small_names:
        if n == "g_final":
            small_local.append(dg_final)
        else:
            small_local.append(jnp.stack([small_grads[l][n].reshape(small_full_shapes[n][1:]) for l in range(DEPTH)]))
    small_pack = _pack(small_local)

    small_rx = _push_sibling(small_pack[None], name="reduce_cores_small")
    small_pair = _elementwise(lambda a, b: (a + b,), [small_pack[None], small_rx], [F32], name="pair_sum_small")[0][0]
    small_came = _chip_scatter([], small_pair, name="reduce_chips_small")[0]
    small_sum = _chip_sum_ordered(small_pair, small_came, chip_arr, name="chip_sum_small")

    out_g, out_d, out_m, out_v = {}, {}, {}, {}
    mine = [_chip_sum(pairs[0][n], received[0][n], pairs[1][n], received[1][n], chip_arr, name=f"chip_sum_{n}")
            for n in BIG]
    other = _push_staged(mine, [jax.ShapeDtypeStruct(g.shape, g.dtype) for g in mine], False, _plan_whole,
                         name="share_cores")
    for i, n in enumerate(BIG):
        out_g[n], out_d[n], out_m[n], out_v[n] = _adamw_shard(wts[n], mine[i], other[i], mom[n], var[n], core_arr,
                                                              name=f"adamw_{n}")
    small_g = dict(zip(small_names, _unpack(small_sum, [small_full_shapes[n] for n in small_names])))
    small_g["b_gate"] = lax.dynamic_slice_in_dim(small_g["b_gate"], chip * 256, 256, axis=2)
    small_g["b_conv"] = lax.dynamic_slice_in_dim(small_g["b_conv"], chip * 128, 128, axis=2)
    def two_d(a):
        return a.reshape(1, -1) if a.ndim == 1 else a

    upd = _adamw_small(*[[two_d(d[n]) for n in small_names] for d in (wts, small_g, mom, var)], name="adamw_small")
    for n, d, m_new, v_new in zip(small_names, *upd):
        shape = wts[n].shape
        out_g[n], out_d[n], out_m[n], out_v[n] = small_g[n], d.reshape(shape), m_new.reshape(shape), v_new.reshape(shape)

    return (loss, grad_x, *[out_g[n] for n in WEIGHTS], *[out_d[n] for n in WEIGHTS], *[out_m[n] for n in WEIGHTS],
            *[out_v[n] for n in WEIGHTS])
```
